```python
import jax, jax.numpy as jnp
from jax import lax
import numpy as np

D_MODEL = 1024
BATCH = 8
SEQ = 4096
DEPTH = 1

ATTN_HEADS = 8
ATTN_HEAD_DIM = 64
RET_HEADS = 8
RET_KEY_DIM = 64
RET_VALUE_DIM = 64
ATTN_WIDTH = ATTN_HEADS * ATTN_HEAD_DIM
RET_KEY_WIDTH = RET_HEADS * RET_KEY_DIM
RET_WIDTH = RET_HEADS * RET_VALUE_DIM
MIX_WIDTH = ATTN_WIDTH + RET_WIDTH
IN_PROJ_WIDTH = 3 * ATTN_WIDTH + 2 * RET_KEY_WIDTH + 2 * RET_WIDTH

DILATED_BRANCHES = ((128, 1), (512, 4), (2048, 16))
ROPE_THETA = 500000.0
ROPE_DIM = ATTN_HEAD_DIM // 4

RET_THETA = 10000.0
RET_CHUNK = 128
RET_DECAY_BASE = 5.0

MOE_GROUPS = 4
EXPERTS_PER_GROUP = 8
N_EXPERTS = MOE_GROUPS * EXPERTS_PER_GROUP
EXPERT_FF = D_MODEL // 2
MOE_TOP_K = 2
MOE_BLOCK = 256

NORM_EPS = 1e-6
NEG_INF = -1e30

kernel_name = "hymba_dilated_retention_hiermoe_encoder"


def rms_norm(x, gain):
    xf = x.astype(jnp.float32)
    y = xf * lax.rsqrt(jnp.mean(xf * xf, axis=-1, keepdims=True) + NORM_EPS)
    return (y * gain.astype(jnp.float32)).astype(x.dtype)


def apply_rotary(t, pos, freqs):
    half = freqs.shape[0]
    ang = pos[:, None] * freqs[None, :]
    cos, sin = jnp.cos(ang), jnp.sin(ang)
    t1, t2, rest = t[..., :half], t[..., half:2 * half], t[..., 2 * half:]
    return jnp.concatenate([t1 * cos - t2 * sin, t2 * cos + t1 * sin, rest], axis=-1)


def dilated_window_branch(q, k, v, window, dilation):
    b, h, s, dh = q.shape
    reach = (window // 2) // dilation
    blk = reach
    length = s // dilation
    nb = -(-length // blk)
    lp = nb * blk

    def by_stride(t):
        t = t.reshape(b, h, length, dilation, dh).transpose(0, 1, 3, 2, 4)
        return jnp.pad(t, ((0, 0), (0, 0), (0, 0), (0, lp - length), (0, 0)))

    def neighbours(t):
        t = jnp.pad(t, ((0, 0), (0, 0), (0, 0), (blk, blk), (0, 0)))
        t = t.reshape(b, h, dilation, nb + 2, blk, dh)
        return jnp.concatenate([t[:, :, :, :-2], t[:, :, :, 1:-1], t[:, :, :, 2:]], axis=-2)

    qb = by_stride(q).reshape(b, h, dilation, nb, blk, dh)
    kn = neighbours(by_stride(k))
    vn = neighbours(by_stride(v))

    t_idx = jnp.arange(blk)[:, None]
    u_idx = jnp.arange(3 * blk)[None, :]
    key_pos = (jnp.arange(nb)[:, None, None] - 1) * blk + u_idx
    offset = u_idx - blk - t_idx
    valid = (jnp.abs(offset) <= reach)[None] & (key_pos >= 0) & (key_pos < length)

    scores = jnp.einsum('bhrjtd,bhrjud->bhrjtu', qb, kn)
    scores = jnp.where(valid, scores, NEG_INF)
    m = jnp.max(scores, axis=-1, keepdims=True)
    p = jnp.exp(scores - m)
    denom = jnp.sum(p, axis=-1, keepdims=True)
    o = jnp.einsum('bhrjtu,bhrjud->bhrjtd', p, vn) / denom
    lse = (m + jnp.log(denom))[..., 0]

    o = o.reshape(b, h, dilation, lp, dh)[:, :, :, :length]
    o = o.transpose(0, 1, 3, 2, 4).reshape(b, h, s, dh)
    lse = lse.reshape(b, h, dilation, lp)[:, :, :, :length]
    lse = lse.transpose(0, 1, 3, 2).reshape(b, h, s)
    return o, lse


def dilated_attention(q, k, v):
    outs, lses = [], []
    for window, dilation in DILATED_BRANCHES:
        o, lse = dilated_window_branch(q, k, v, window, dilation)
        outs.append(o)
        lses.append(lse)
    weights = jax.nn.softmax(jnp.stack(lses), axis=0)
    return jnp.einsum('nbhs,nbhsd->bhsd', weights, jnp.stack(outs))


def retention_chunkwise(q, k, v, log_gamma, include_diag):
    b, h, s, dk = q.shape
    dv = v.shape[-1]
    c = RET_CHUNK
    n = s // c
    qc = q.reshape(b, h, n, c, dk)
    kc = k.reshape(b, h, n, c, dk)
    vc = v.reshape(b, h, n, c, dv)
    idx = jnp.arange(c, dtype=jnp.float32)
    diff = idx[:, None] - idx[None, :]
    mask = (diff >= 0) if include_diag else (diff > 0)
    decay = jnp.where(mask, jnp.exp(jnp.maximum(diff, 0.0) * log_gamma[:, None, None]), 0.0)
    inner = jnp.einsum('bhnid,bhnjd->bhnij', qc, kc) * decay[:, None]
    o_inner = jnp.einsum('bhnij,bhnje->bhnie', inner, vc)

    k_decay = jnp.exp((c - 1 - idx)[None, :] * log_gamma[:, None])
    q_decay = jnp.exp((idx + 1)[None, :] * log_gamma[:, None])
    chunk_kv = jnp.einsum('bhnjd,bhnje->nbhde', kc * k_decay[:, None, :, None], vc)
    chunk_decay = jnp.exp(c * log_gamma)[:, None, None]

    def step(state, kv):
        return state * chunk_decay + kv, state

    _, prev_states = lax.scan(step, jnp.zeros((b, h, dk, dv), jnp.float32), chunk_kv)
    o_cross = jnp.einsum('bhnid,nbhde->bhnie', qc * q_decay[:, None, :, None], prev_states)
    return (o_inner + o_cross).reshape(b, h, s, dv)


def hybrid_mixer(xn, w_in, attn_out_gain, ret_decay_fwd, ret_decay_bwd, ret_out_gain, w_out):
    b, s, _ = xn.shape
    proj = (xn @ w_in).astype(jnp.float32)
    cuts = np.cumsum([ATTN_WIDTH, ATTN_WIDTH, ATTN_WIDTH, RET_KEY_WIDTH, RET_KEY_WIDTH, RET_WIDTH]).tolist()
    qa, ka, va, qr, kr, vr, gr = jnp.split(proj, cuts, axis=-1)

    def heads(t, n_heads):
        return t.reshape(b, s, n_heads, -1).transpose(0, 2, 1, 3)

    pos = jnp.arange(s, dtype=jnp.float32)

    rope_freqs = ROPE_THETA ** (-jnp.arange(0, ROPE_DIM, 2, dtype=jnp.float32) / ROPE_DIM)
    qa = apply_rotary(heads(qa, ATTN_HEADS), pos, rope_freqs) * (ATTN_HEAD_DIM ** -0.5)
    ka = apply_rotary(heads(ka, ATTN_HEADS), pos, rope_freqs)
    oa = dilated_attention(qa, ka, heads(va, ATTN_HEADS))
    oa = rms_norm(oa.transpose(0, 2, 1, 3).reshape(b, s, ATTN_WIDTH), attn_out_gain)

    ret_freqs = RET_THETA ** (-jnp.linspace(0.0, 1.0, RET_KEY_DIM // 2, dtype=jnp.float32))
    qr = apply_rotary(heads(qr, RET_HEADS), pos, ret_freqs)
    kr = apply_rotary(heads(kr, RET_HEADS), pos, ret_freqs) * (RET_KEY_DIM ** -0.5)
    vr = heads(vr, RET_HEADS)
    lg_f = jnp.log1p(-jnp.exp2(ret_decay_fwd.astype(jnp.float32)))
    lg_b = jnp.log1p(-jnp.exp2(ret_decay_bwd.astype(jnp.float32)))
    o_f = retention_chunkwise(qr, kr, vr, lg_f, True)
    o_b = jnp.flip(retention_chunkwise(jnp.flip(qr, 2), jnp.flip(kr, 2), jnp.flip(vr, 2), lg_b, False), 2)
    orr = o_f + o_b
    mu = jnp.mean(orr, axis=-1, keepdims=True)
    var = jnp.mean(jnp.square(orr - mu), axis=-1, keepdims=True)
    orr = (orr - mu) * lax.rsqrt(var + NORM_EPS)
    orr = orr.transpose(0, 2, 1, 3).reshape(b, s, RET_WIDTH)
    orr = orr * ret_out_gain.astype(jnp.float32) * jax.nn.silu(gr)

    mixed = jnp.concatenate([oa.astype(jnp.float32), orr], axis=-1).astype(xn.dtype)
    return mixed @ w_out


def hierarchical_moe(xn, w_route_group, b_route_group, w_route_expert, b_route_expert,
                     w_expert_gate, w_expert_up, w_expert_down):
    b, s, d = xn.shape
    t = b * s
    xt = xn.reshape(t, d)
    xf = xt.astype(jnp.float32)
    group_prob = jax.nn.softmax(xf @ w_route_group.astype(jnp.float32)
                                + b_route_group.astype(jnp.float32), axis=-1)
    group_idx = jnp.argmax(group_prob, axis=-1).astype(jnp.int32)
    group_gate = jnp.take_along_axis(group_prob, group_idx[:, None], axis=-1)
    expert_logits = (xf @ w_route_expert.astype(jnp.float32)
                     + b_route_expert.astype(jnp.float32)).reshape(t, MOE_GROUPS, EXPERTS_PER_GROUP)
    in_group = jnp.take_along_axis(expert_logits, group_idx[:, None, None], axis=1)[:, 0]
    top_logits, top_idx = lax.top_k(in_group, MOE_TOP_K)
    gates = group_gate * jax.nn.softmax(top_logits, axis=-1)

    expert_ids = (group_idx[:, None] * EXPERTS_PER_GROUP + top_idx).reshape(-1).astype(jnp.int32)
    token_ids = jnp.repeat(jnp.arange(t, dtype=jnp.int32), MOE_TOP_K)
    gate_flat = gates.reshape(-1)
    n_assign = t * MOE_TOP_K
    n_blocks = -(-n_assign // MOE_BLOCK) + N_EXPERTS
    n_slots = n_blocks * MOE_BLOCK

    counts = jax.ops.segment_sum(jnp.ones((n_assign,), jnp.int32), expert_ids, num_segments=N_EXPERTS)
    padded = ((counts + MOE_BLOCK - 1) // MOE_BLOCK) * MOE_BLOCK
    pad_end = jnp.cumsum(padded)
    pad_start = pad_end - padded
    start = jnp.cumsum(counts) - counts
    order = jnp.argsort(expert_ids)
    sorted_e = expert_ids[order]
    dest = pad_start[sorted_e] + (jnp.arange(n_assign, dtype=jnp.int32) - start[sorted_e])
    slot_tok = jnp.full((n_slots,), t, jnp.int32).at[dest].set(token_ids[order])
    slot_gate = jnp.zeros((n_slots,), jnp.float32).at[dest].set(gate_flat[order])
    block_start = jnp.arange(n_blocks, dtype=jnp.int32) * MOE_BLOCK
    block_expert = jnp.clip(jnp.searchsorted(pad_end, block_start, side='right'), 0, N_EXPERTS - 1)

    x_pad = jnp.concatenate([xt, jnp.zeros((1, d), xt.dtype)], axis=0)
    xs = x_pad[slot_tok].reshape(n_blocks, MOE_BLOCK, d)

    def expert_block(args):
        xb, e = args
        hid = jax.nn.silu(xb @ w_expert_gate[e]) * (xb @ w_expert_up[e])
        return hid @ w_expert_down[e]

    ys = lax.map(expert_block, (xs, block_expert)).reshape(n_slots, d)
    out = jax.ops.segment_sum(ys * slot_gate[:, None].astype(ys.dtype), slot_tok, num_segments=t + 1)[:t]
    return out.reshape(b, s, d)


def setup_inputs(seed: int = 0) -> dict:
    key = jax.random.key(seed)
    ks = jax.random.split(key, 20)
    f32 = jnp.float32
    nrm = lambda k, shape, scale: jax.random.normal(k, shape, f32) * scale
    decay_base = -(RET_DECAY_BASE + jnp.arange(RET_HEADS, dtype=f32))
    return {
        "x": nrm(ks[0], (BATCH, SEQ, D_MODEL), 1.0),
        "mix_norm_gain": 1.0 + nrm(ks[1], (DEPTH, D_MODEL), 0.02),
        "w_in": nrm(ks[2], (DEPTH, D_MODEL, IN_PROJ_WIDTH), D_MODEL ** -0.5),
        "attn_out_gain": 1.0 + nrm(ks[3], (DEPTH, ATTN_WIDTH), 0.02),
        "ret_decay_fwd": decay_base + nrm(ks[4], (DEPTH, RET_HEADS), 0.1),
        "ret_decay_bwd": decay_base + nrm(ks[5], (DEPTH, RET_HEADS), 0.1),
        "ret_out_gain": 1.0 + nrm(ks[6], (DEPTH, RET_WIDTH), 0.02),
        "w_out": nrm(ks[7], (DEPTH, MIX_WIDTH, D_MODEL), MIX_WIDTH ** -0.5),
        "ffn_norm_gain": 1.0 + nrm(ks[8], (DEPTH, D_MODEL), 0.02),
        "w_route_group": nrm(ks[9], (DEPTH, D_MODEL, MOE_GROUPS), D_MODEL ** -0.5),
        "b_route_group": nrm(ks[10], (DEPTH, MOE_GROUPS), 0.01),
        "w_route_expert": nrm(ks[11], (DEPTH, D_MODEL, N_EXPERTS), D_MODEL ** -0.5),
        "b_route_expert": nrm(ks[12], (DEPTH, N_EXPERTS), 0.01),
        "w_expert_gate": nrm(ks[13], (DEPTH, N_EXPERTS, D_MODEL, EXPERT_FF), D_MODEL ** -0.5),
        "w_expert_up": nrm(ks[14], (DEPTH, N_EXPERTS, D_MODEL, EXPERT_FF), D_MODEL ** -0.5),
        "w_expert_down": nrm(ks[15], (DEPTH, N_EXPERTS, EXPERT_FF, D_MODEL), EXPERT_FF ** -0.5),
        "final_norm_gain": 1.0 + nrm(ks[16], (D_MODEL,), 0.02),
    }


def reference(x, mix_norm_gain, w_in, attn_out_gain, ret_decay_fwd, ret_decay_bwd, ret_out_gain,
              w_out, ffn_norm_gain, w_route_group, b_route_group, w_route_expert, b_route_expert,
              w_expert_gate, w_expert_up, w_expert_down, final_norm_gain):
    h = x
    for l in range(DEPTH):
        hn = rms_norm(h, mix_norm_gain[l])
        h = h + hybrid_mixer(hn, w_in[l], attn_out_gain[l], ret_decay_fwd[l], ret_decay_bwd[l],
                             ret_out_gain[l], w_out[l]).astype(h.dtype)
        hn = rms_norm(h, ffn_norm_gain[l])
        h = h + hierarchical_moe(hn, w_route_group[l], b_route_group[l], w_route_expert[l],
                                 b_route_expert[l], w_expert_gate[l], w_expert_up[l],
                                 w_expert_down[l]).astype(h.dtype)
    return rms_norm(h, final_norm_gain)
```

```python
import functools

import numpy as np
import jax
import jax.numpy as jnp
from jax import lax
from jax.experimental import pallas as pl
from jax.experimental.pallas import tpu as pltpu

F32 = jnp.float32
BF16 = jnp.bfloat16

ATTN_HEADS = 8
HEAD_DIM = 64
RET_HEADS = 8
HEAD_GROUP_WIDTH = 512
N_PROJ_GROUPS = 7
DILATED_BRANCHES = ((128, 1), (512, 4), (2048, 16))
ROPE_THETA = 500000.0
ROPE_DIM = HEAD_DIM // 4
RET_THETA = 10000.0
RET_CHUNK = 128
MOE_GROUPS = 4
EXPERTS_PER_GROUP = 8
N_EXPERTS = MOE_GROUPS * EXPERTS_PER_GROUP
MOE_BLOCK = 256
NORM_EPS = 1e-6
NEG_INF = -1e30

LANES = 128
VMEM_LIMIT = 56 * 1024 * 1024


def _cparams(sem):
    return pltpu.CompilerParams(dimension_semantics=sem, vmem_limit_bytes=VMEM_LIMIT)


def _rotary_tables(seq, half, freqs):
    pos = np.arange(seq, dtype=np.float64)[:, None]
    ang = pos * freqs[None, :].astype(np.float64)
    cos, sin = np.cos(ang), np.sin(ang)
    c = np.ones((seq, HEAD_DIM)); sp = np.zeros((seq, HEAD_DIM)); sm = np.zeros((seq, HEAD_DIM))
    c[:, :half] = cos; c[:, half:2 * half] = cos
    sp[:, half:2 * half] = sin
    sm[:, :half] = -sin
    rep = LANES // HEAD_DIM
    return tuple(jnp.asarray(np.tile(t, (1, rep)), F32) for t in (c, sp, sm))


def _rotate(t, c, sp, sm, half):
    outs = []
    for g in range(t.shape[1] // LANES):
        tg = t[:, g * LANES:(g + 1) * LANES]
        outs.append(tg * c + pltpu.roll(tg, half, 1) * sp + pltpu.roll(tg, LANES - half, 1) * sm)
    return jnp.concatenate(outs, axis=1)


def _inproj_kernel(x_ref, gain_ref, w_ref, ca_ref, spa_ref, sma_ref, cr_ref, spr_ref, smr_ref,
                   qa_ref, ka_ref, va_ref, qr_ref, kr_ref, vr_ref, gr_ref):
    x = x_ref[0]
    ms = jnp.mean(x * x, axis=-1, keepdims=True)
    xn = (x * lax.rsqrt(ms + NORM_EPS) * gain_ref[...]).astype(BF16)
    gw = HEAD_GROUP_WIDTH

    def proj(c):
        return jnp.dot(xn, w_ref[:, c * gw:(c + 1) * gw], preferred_element_type=F32)

    a_tabs = (ca_ref[...], spa_ref[...], sma_ref[...])
    r_tabs = (cr_ref[...], spr_ref[...], smr_ref[...])
    qa_ref[0] = (_rotate(proj(0), *a_tabs, ROPE_DIM // 2) * (HEAD_DIM ** -0.5)).astype(BF16)
    ka_ref[0] = _rotate(proj(1), *a_tabs, ROPE_DIM // 2).astype(BF16)
    va_ref[0] = proj(2).astype(BF16)
    qr_ref[0] = _rotate(proj(3), *r_tabs, HEAD_DIM // 2).astype(BF16)
    kr_ref[0] = (_rotate(proj(4), *r_tabs, HEAD_DIM // 2) * (HEAD_DIM ** -0.5)).astype(BF16)
    vr_ref[0] = proj(5).astype(BF16)
    g = proj(6)
    gr_ref[0] = (g * jax.nn.sigmoid(g)).astype(BF16)


def _inproj(x, gain, w_in_bf16, tm):
    b, s, d = x.shape
    rope_freqs = ROPE_THETA ** (-np.arange(0, ROPE_DIM, 2, dtype=np.float32) / ROPE_DIM)
    ret_freqs = RET_THETA ** (-np.linspace(0.0, 1.0, HEAD_DIM // 2, dtype=np.float32))
    tabs = _rotary_tables(s, ROPE_DIM // 2, rope_freqs) + _rotary_tables(s, HEAD_DIM // 2, ret_freqs)
    tab_spec = pl.BlockSpec((tm, LANES), lambda si, bi: (si, 0))
    out_spec = pl.BlockSpec((1, tm, HEAD_GROUP_WIDTH), lambda si, bi: (bi, si, 0))
    out_shape = jax.ShapeDtypeStruct((b, s, HEAD_GROUP_WIDTH), BF16)
    return pl.pallas_call(
        _inproj_kernel,
        grid=(s // tm, b),
        in_specs=[pl.BlockSpec((1, tm, d), lambda si, bi: (bi, si, 0)),
                  pl.BlockSpec((1, d), lambda si, bi: (0, 0)),
                  pl.BlockSpec(w_in_bf16.shape, lambda si, bi: (0, 0))] + [tab_spec] * 6,
        out_specs=[out_spec] * N_PROJ_GROUPS,
        out_shape=[out_shape] * N_PROJ_GROUPS,
        compiler_params=_cparams(("arbitrary", "arbitrary")),
        name="inproj",
    )(x, gain.reshape(1, d), w_in_bf16, *tabs)


def _attn_kernel(q_ref, k_ref, v_ref, o_ref, lse_ref, *, length, tq, reach):
    tqb = q_ref.shape[1]
    win = tq + 2 * reach
    qi = pl.program_id(2)
    lane = lax.broadcasted_iota(jnp.int32, (1, LANES), 1)
    diff = (lax.broadcasted_iota(jnp.int32, (tq, win), 1)
            - lax.broadcasted_iota(jnp.int32, (tq, win), 0))
    lane_t = lax.broadcasted_iota(jnp.int32, (tq, LANES), 1)

    def sub(t, carry):
        q0 = qi * tqb + t * tq
        ws = jnp.clip(q0 - reach, 0, length - win)
        ws = pl.multiple_of(ws, reach)
        off = q0 - ws
        valid = (diff >= off - reach) & (diff <= off + reach)
        rows = pl.ds(pl.multiple_of(t * tq, tq), tq)
        lse_tile = jnp.zeros((tq, LANES), F32)
        for g in range(HEAD_GROUP_WIDTH // LANES):
            cols = slice(g * LANES, (g + 1) * LANES)
            qg = q_ref[0, rows, cols]
            kw = k_ref[0, pl.ds(ws, win), cols]
            vw = v_ref[0, pl.ds(ws, win), cols]
            o_pair = jnp.zeros((tq, LANES), F32)
            for hh in range(LANES // HEAD_DIM):
                hm = (lane >= hh * HEAD_DIM) & (lane < (hh + 1) * HEAD_DIM)
                qh = jnp.where(hm, qg, jnp.zeros_like(qg))
                sc = lax.dot_general(qh, kw, (((1,), (1,)), ((), ())), preferred_element_type=F32)
                sc = jnp.where(valid, sc, NEG_INF)
                m = jnp.max(sc, axis=-1, keepdims=True)
                p = jnp.exp(sc - m)
                l = jnp.sum(p, axis=-1, keepdims=True)
                pv = jnp.dot(p.astype(BF16), vw, preferred_element_type=F32)
                o_pair = jnp.where(hm, pv / l, o_pair)
                head = g * (LANES // HEAD_DIM) + hh
                lse_tile = jnp.where(lane_t == head, m + jnp.log(l), lse_tile)
            o_ref[0, rows, cols] = o_pair.astype(BF16)
        lse_ref[0, rows, :] = lse_tile
        return carry

    lax.fori_loop(0, tqb // tq, sub, 0)


def _attn_branch(q, k, v, dilation, reach, tq, tqb):
    b, s, w = q.shape
    length = s // dilation
    tqb = min(tqb, length)

    def to_class(t):
        return t.reshape(b, length, dilation * t.shape[-1])

    qc, kc, vc = to_class(q), to_class(k), to_class(v)
    q_spec = pl.BlockSpec((1, tqb, w), lambda bi, r, qi: (bi, qi, r))
    kv_spec = pl.BlockSpec((1, length, w), lambda bi, r, qi: (bi, 0, r))
    o, lse = pl.pallas_call(
        functools.partial(_attn_kernel, length=length, tq=tq, reach=reach),
        grid=(b, dilation, length // tqb),
        in_specs=[q_spec, kv_spec, kv_spec],
        out_specs=[q_spec, pl.BlockSpec((1, tqb, LANES), lambda bi, r, qi: (bi, qi, r))],
        out_shape=[jax.ShapeDtypeStruct((b, length, dilation * w), BF16),
                   jax.ShapeDtypeStruct((b, length, dilation * LANES), F32)],
        compiler_params=_cparams(("arbitrary", "arbitrary", "arbitrary")),
        name=f"attn_d{dilation}",
    )(qc, kc, vc)
    return o.reshape(b, s, w), lse.reshape(b, s, LANES)


def _retention_kernel(lgf_ref, lgb_ref, q_ref, k_ref, v_ref, g_ref, gain_ref, o_ref, sb_ref, *, chunk):
    c = chunk
    n = q_ref.shape[1] // c
    pair = pl.program_id(1)
    heads_per_pair = LANES // HEAD_DIM
    lane = lax.broadcasted_iota(jnp.int32, (1, LANES), 1)
    lane_hi = lane >= HEAD_DIM
    lgf = [lgf_ref[pair * heads_per_pair + hh] for hh in range(heads_per_pair)]
    lgb = [lgb_ref[pair * heads_per_pair + hh] for hh in range(heads_per_pair)]
    lgf_lane = jnp.where(lane_hi, lgf[1], lgf[0])
    lgb_lane = jnp.where(lane_hi, lgb[1], lgb[0])
    idx = lax.broadcasted_iota(jnp.int32, (c, LANES), 0).astype(F32)
    qdf = jnp.exp((idx + 1.0) * lgf_lane)
    qdb = jnp.exp((c - idx) * lgb_lane)
    kdf = jnp.exp((c - 1.0 - idx) * lgf_lane)
    kdb = jnp.exp(idx * lgb_lane)
    sdf = jnp.exp(c * lgf_lane)
    sdb = jnp.exp(c * lgb_lane)
    dmat = (lax.broadcasted_iota(jnp.int32, (c, c), 0)
            - lax.broadcasted_iota(jnp.int32, (c, c), 1)).astype(F32)
    decay = [jnp.where(dmat >= 0, jnp.exp(dmat * lgf[hh]), jnp.exp(-dmat * lgb[hh]))
             for hh in range(heads_per_pair)]
    row_hi = lax.broadcasted_iota(jnp.int32, (LANES, LANES), 0) >= HEAD_DIM
    col_hi = lax.broadcasted_iota(jnp.int32, (LANES, LANES), 1) >= HEAD_DIM
    blockdiag = row_hi == col_hi

    def kv_state(kd, vv):
        kt = jnp.transpose(kd).astype(BF16)
        return jnp.where(blockdiag, jnp.dot(kt, vv, preferred_element_type=F32), 0.0)

    def back(i, sb):
        nn = n - 1 - i
        rows = pl.ds(pl.multiple_of(nn * c, c), c)
        sb_ref[nn] = sb
        kd = k_ref[0, rows, :].astype(F32) * kdb
        return sb * sdb + kv_state(kd, v_ref[0, rows, :])

    lax.fori_loop(0, n, back, jnp.zeros((LANES, LANES), F32))

    gain = gain_ref[...]

    def fwd(nn, sf):
        rows = pl.ds(pl.multiple_of(nn * c, c), c)
        qq = q_ref[0, rows, :]
        kk = k_ref[0, rows, :]
        vv = v_ref[0, rows, :]
        qf = qq.astype(F32)
        o = (jnp.dot((qf * qdf).astype(BF16), sf.astype(BF16), preferred_element_type=F32)
             + jnp.dot((qf * qdb).astype(BF16), sb_ref[nn].astype(BF16), preferred_element_type=F32))
        o_in = jnp.zeros((c, LANES), F32)
        for hh in range(heads_per_pair):
            hm = (lane >= hh * HEAD_DIM) & (lane < (hh + 1) * HEAD_DIM)
            qh = jnp.where(hm, qq, jnp.zeros_like(qq))
            sc = lax.dot_general(qh, kk, (((1,), (1,)), ((), ())), preferred_element_type=F32)
            pv = jnp.dot((sc * decay[hh]).astype(BF16), vv, preferred_element_type=F32)
            o_in = jnp.where(hm, pv, o_in)
        o = o + o_in
        s_lo = jnp.sum(jnp.where(lane_hi, 0.0, o), axis=-1, keepdims=True)
        s_hi = jnp.sum(jnp.where(lane_hi, o, 0.0), axis=-1, keepdims=True)
        mu = jnp.where(lane_hi, s_hi, s_lo) * (1.0 / HEAD_DIM)
        dev = o - mu
        d2 = dev * dev
        v_lo = jnp.sum(jnp.where(lane_hi, 0.0, d2), axis=-1, keepdims=True)
        v_hi = jnp.sum(jnp.where(lane_hi, d2, 0.0), axis=-1, keepdims=True)
        var = jnp.where(lane_hi, v_hi, v_lo) * (1.0 / HEAD_DIM)
        out = dev * lax.rsqrt(var + NORM_EPS) * gain * g_ref[0, rows, :].astype(F32)
        o_ref[0, rows, :] = out.astype(BF16)
        kd = kk.astype(F32) * kdf
        return sf * sdf + kv_state(kd, vv)

    lax.fori_loop(0, n, fwd, jnp.zeros((LANES, LANES), F32))


def _retention(qr, kr, vr, gate, lg_f, lg_b, out_gain):
    b, s, w = qr.shape
    spec = pl.BlockSpec((1, s, LANES), lambda bi, p, *_: (bi, 0, p))
    grid_spec = pltpu.PrefetchScalarGridSpec(
        num_scalar_prefetch=2,
        grid=(b, w // LANES),
        in_specs=[spec, spec, spec, spec, pl.BlockSpec((1, LANES), lambda bi, p, *_: (0, p))],
        out_specs=spec,
        scratch_shapes=[pltpu.VMEM((s // RET_CHUNK, LANES, LANES), F32)],
    )
    return pl.pallas_call(
        functools.partial(_retention_kernel, chunk=RET_CHUNK),
        grid_spec=grid_spec,
        out_shape=jax.ShapeDtypeStruct((b, s, w), BF16),
        compiler_params=_cparams(("arbitrary", "arbitrary")),
        name="retention",
    )(lg_f, lg_b, qr, kr, vr, gate, out_gain.reshape(1, w))


def _split_bf16(t):
    hi = t.astype(BF16)
    lo = (t - hi.astype(F32)).astype(BF16)
    return hi, lo


def _pack_bf16_pairs(t):
    n = t.shape[1] // 2
    hi = pltpu.bitcast(t[:, :n].astype(BF16).astype(F32), jnp.uint32)
    lo = pltpu.bitcast(t[:, n:].astype(BF16).astype(F32), jnp.uint32)
    return hi | (lo >> 16)


def _unpack_bf16_pairs(u):
    hi = pltpu.bitcast(u & jnp.uint32(0xFFFF0000), F32)
    lo = pltpu.bitcast(u << 16, F32)
    return jnp.concatenate([hi, lo], axis=1)


def _outproj_kernel(x_ref, o1_ref, o2_ref, o3_ref, l1_ref, l2_ref, l3_ref, orr_ref, ga_ref, expand_ref,
                    wout_ref, gf_ref, wr_hi_ref, wr_lo_ref, br_ref, h_ref, hn_ref, logit_ref):
    ls = [l1_ref[...], l2_ref[...], l3_ref[...]]
    mx = jnp.maximum(jnp.maximum(ls[0], ls[1]), ls[2])
    es = [jnp.exp(l - mx) for l in ls]
    inv = 1.0 / (es[0] + es[1] + es[2])
    expand = expand_ref[...]
    oa = jnp.zeros(o1_ref.shape, F32)
    for e, o_ref in zip(es, (o1_ref, o2_ref, o3_ref)):
        w_hi, w_lo = _split_bf16(e * inv)
        wexp = (jnp.dot(w_hi, expand, preferred_element_type=F32)
                + jnp.dot(w_lo, expand, preferred_element_type=F32))
        oa = oa + wexp * o_ref[...].astype(F32)
    oa = oa * lax.rsqrt(jnp.mean(oa * oa, axis=-1, keepdims=True) + NORM_EPS) * ga_ref[...]
    mixed = jnp.concatenate([oa.astype(BF16), orr_ref[...]], axis=1)
    h = x_ref[...] + jnp.dot(mixed, wout_ref[...], preferred_element_type=F32)
    h_ref[...] = h
    hn = h * lax.rsqrt(jnp.mean(h * h, axis=-1, keepdims=True) + NORM_EPS) * gf_ref[...]
    hn_ref[...] = _pack_bf16_pairs(hn)
    hn_hi, hn_lo = _split_bf16(hn)
    wr_hi = wr_hi_ref[...]
    logit_ref[...] = (jnp.dot(hn_hi, wr_hi, preferred_element_type=F32)
                      + jnp.dot(hn_lo, wr_hi, preferred_element_type=F32)
                      + jnp.dot(hn_hi, wr_lo_ref[...], preferred_element_type=F32)
                      + br_ref[...])


def _outproj(x2, o_list, lse_list, orr, attn_gain, w_out_bf16, ffn_gain, w_router, b_router, tm):
    t, d = x2.shape
    w = HEAD_GROUP_WIDTH
    expand = np.zeros((LANES, w), np.float32)
    for hd in range(ATTN_HEADS):
        expand[hd, hd * HEAD_DIM:(hd + 1) * HEAD_DIM] = 1.0
    expand = jnp.asarray(expand, BF16)
    wr_hi = w_router.astype(BF16)
    wr_lo = (w_router - wr_hi.astype(F32)).astype(BF16)
    row = lambda width: pl.BlockSpec((tm, width), lambda i: (i, 0))
    full = lambda a: pl.BlockSpec(a.shape, lambda i: (0,) * a.ndim)
    ga = attn_gain.reshape(1, w)
    gf = ffn_gain.reshape(1, d)
    return pl.pallas_call(
        _outproj_kernel,
        grid=(t // tm,),
        in_specs=[row(d), row(w), row(w), row(w), row(LANES), row(LANES), row(LANES), row(w),
                  full(ga), full(expand), full(w_out_bf16), full(gf), full(wr_hi), full(wr_lo), full(b_router)],
        out_specs=[row(d), row(d // 2), row(LANES)],
        out_shape=[jax.ShapeDtypeStruct((t, d), F32),
                   jax.ShapeDtypeStruct((t, d // 2), jnp.uint32),
                   jax.ShapeDtypeStruct((t, LANES), F32)],
        compiler_params=_cparams(("arbitrary",)),
        name="outproj",
    )(x2, *o_list, *lse_list, orr, ga, expand, w_out_bf16, gf, wr_hi, wr_lo, b_router)


ROUTE_E1, ROUTE_E2, ROUTE_G1, ROUTE_G2, ROUTE_R1, ROUTE_R2 = range(6)
EXPERT_LANE0 = MOE_GROUPS


def _route_kernel(logit_ref, tri_ref, route_ref, count_ref, run_ref):
    @pl.when(pl.program_id(0) == 0)
    def _():
        run_ref[...] = jnp.zeros_like(run_ref)

    lg = logit_ref[...]
    tm = lg.shape[0]
    lane = lax.broadcasted_iota(jnp.int32, lg.shape, 1)
    big = jnp.int32(1 << 20)

    def top(vals):
        m = jnp.max(vals, axis=-1, keepdims=True)
        i = jnp.min(jnp.where(vals == m, lane, big), axis=-1, keepdims=True)
        return m, i

    gl = jnp.where(lane < MOE_GROUPS, lg, -jnp.inf)
    gmax, gidx = top(gl)
    group_gate = 1.0 / jnp.sum(jnp.exp(gl - gmax), axis=-1, keepdims=True)
    lo = EXPERT_LANE0 + gidx * EXPERTS_PER_GROUP
    el = jnp.where((lane >= lo) & (lane < lo + EXPERTS_PER_GROUP), lg, -jnp.inf)
    t1, i1 = top(el)
    t2, i2 = top(jnp.where(lane == i1, -jnp.inf, el))
    e21 = jnp.exp(t2 - t1)
    g1 = group_gate / (1.0 + e21)
    g2 = group_gate * e21 / (1.0 + e21)
    e1 = i1 - EXPERT_LANE0
    e2 = i2 - EXPERT_LANE0
    oh1 = lane == e1
    oh2 = lane == e2
    cnt = oh1.astype(F32) + oh2.astype(F32)
    prefix = jnp.dot(tri_ref[...], cnt.astype(BF16), preferred_element_type=F32) + run_ref[0:1, :]
    r1 = jnp.sum(jnp.where(oh1, prefix, 0.0), axis=-1, keepdims=True)
    r2 = jnp.sum(jnp.where(oh2, prefix, 0.0), axis=-1, keepdims=True)
    new_run = run_ref[0:1, :] + jnp.sum(cnt, axis=0, keepdims=True)
    run_ref[...] = jnp.broadcast_to(new_run, run_ref.shape)
    count_ref[...] = jnp.broadcast_to(new_run, count_ref.shape)
    out = jnp.zeros(lg.shape, F32)
    for ln, val in ((ROUTE_E1, e1.astype(F32)), (ROUTE_E2, e2.astype(F32)), (ROUTE_G1, g1),
                    (ROUTE_G2, g2), (ROUTE_R1, r1), (ROUTE_R2, r2)):
        out = jnp.where(lane == ln, val, out)
    route_ref[...] = out


def _route(logits, tm):
    t = logits.shape[0]
    tri = jnp.asarray(np.tril(np.ones((tm, tm), np.float32), -1), BF16)
    return pl.pallas_call(
        _route_kernel,
        grid=(t // tm,),
        in_specs=[pl.BlockSpec((tm, LANES), lambda i: (i, 0)), pl.BlockSpec((tm, tm), lambda i: (0, 0))],
        out_specs=[pl.BlockSpec((tm, LANES), lambda i: (i, 0)), pl.BlockSpec((8, LANES), lambda i: (0, 0))],
        out_shape=[jax.ShapeDtypeStruct((t, LANES), F32), jax.ShapeDtypeStruct((8, LANES), F32)],
        scratch_shapes=[pltpu.VMEM((8, LANES), F32)],
        compiler_params=_cparams(("arbitrary",)),
        name="route",
    )(logits, tri)


def _for_rows(n, body):
    def run(j, c):
        body(j)
        return c

    @pl.when(n == MOE_BLOCK)
    def _():
        lax.fori_loop(0, MOE_BLOCK, run, 0, unroll=8)

    @pl.when(n < MOE_BLOCK)
    def _():
        lax.fori_loop(0, n, run, 0)


def _wait_rows(n, block_copy, row_copy):
    @pl.when(n == MOE_BLOCK)
    def _():
        block_copy.wait()

    @pl.when(n < MOE_BLOCK)
    def _():
        def run(j, c):
            row_copy.wait()
            return c
        lax.fori_loop(0, n, run, 0)


def _expert_kernel(bexp_ref, nreal_ref, tok_ref, tok_next_ref, dst_ref, hn_hbm, wg_ref, wu_ref, wd_ref, y_hbm,
                   xbuf, ybuf, gsem, ssem):
    del bexp_ref
    i = pl.program_id(0)
    nb = pl.num_programs(0)
    slot = i % 2
    nslot = 1 - slot
    n_cur = nreal_ref[i]

    def gather_start(idx_ref, s, n):
        def body(j):
            pltpu.make_async_copy(hn_hbm.at[pl.ds(idx_ref[0, 0, j], 1)], xbuf.at[s, pl.ds(j, 1)],
                                  gsem.at[s]).start()
        _for_rows(n, body)

    def gather_wait(s, n):
        _wait_rows(n, pltpu.make_async_copy(hn_hbm.at[pl.ds(0, MOE_BLOCK)], xbuf.at[s], gsem.at[s]),
                   pltpu.make_async_copy(hn_hbm.at[pl.ds(0, 1)], xbuf.at[s, pl.ds(0, 1)], gsem.at[s]))

    def scatter_wait(s, n):
        _wait_rows(n, pltpu.make_async_copy(ybuf.at[s], y_hbm.at[pl.ds(0, MOE_BLOCK)], ssem.at[s]),
                   pltpu.make_async_copy(ybuf.at[s, pl.ds(0, 1)], y_hbm.at[pl.ds(0, 1)], ssem.at[s]))

    @pl.when(i == 0)
    def _():
        xbuf[...] = jnp.zeros_like(xbuf)
        gather_start(tok_ref, slot, n_cur)

    @pl.when(i + 1 < nb)
    def _():
        gather_start(tok_next_ref, nslot, nreal_ref[jnp.minimum(i + 1, nb - 1)])

    gather_wait(slot, n_cur)

    @pl.when(i >= 2)
    def _():
        scatter_wait(slot, nreal_ref[jnp.maximum(i - 2, 0)])

    @pl.when(n_cur > 0)
    def _():
        xb = _unpack_bf16_pairs(xbuf[slot]).astype(BF16)
        gate = jnp.dot(xb, wg_ref[0], preferred_element_type=F32)
        up = jnp.dot(xb, wu_ref[0], preferred_element_type=F32)
        hid = (gate * jax.nn.sigmoid(gate) * up).astype(BF16)
        ybuf[slot] = _pack_bf16_pairs(jnp.dot(hid, wd_ref[0], preferred_element_type=F32))

        def sbody(j):
            pltpu.make_async_copy(ybuf.at[slot, pl.ds(j, 1)], y_hbm.at[pl.ds(dst_ref[0, 0, j], 1)],
                                  ssem.at[slot]).start()
        _for_rows(n_cur, sbody)

    @pl.when(i == nb - 1)
    def _():
        scatter_wait(slot, n_cur)
        scatter_wait(nslot, nreal_ref[jnp.maximum(i - 1, 0)])


def _experts(hn_packed, slot_tok, slot_dst, block_expert, block_nreal, wg, wu, wd, n_rows_out):
    n_blocks = block_expert.shape[0]
    assert n_blocks >= 2
    t, dp = hn_packed.shape
    _, d, ff = wg.shape
    tok2 = slot_tok.reshape(n_blocks, 1, MOE_BLOCK)
    dst2 = slot_dst.reshape(n_blocks, 1, MOE_BLOCK)
    smem_blk = lambda imap: pl.BlockSpec((1, 1, MOE_BLOCK), imap, memory_space=pltpu.SMEM)
    grid_spec = pltpu.PrefetchScalarGridSpec(
        num_scalar_prefetch=2,
        grid=(n_blocks,),
        in_specs=[smem_blk(lambda i, be, nr: (i, 0, 0)),
                  smem_blk(lambda i, be, nr: (jnp.minimum(i + 1, n_blocks - 1), 0, 0)),
                  smem_blk(lambda i, be, nr: (i, 0, 0)),
                  pl.BlockSpec(memory_space=pl.ANY),
                  pl.BlockSpec((1, d, ff), lambda i, be, nr: (be[i], 0, 0)),
                  pl.BlockSpec((1, d, ff), lambda i, be, nr: (be[i], 0, 0)),
                  pl.BlockSpec((1, ff, d), lambda i, be, nr: (be[i], 0, 0))],
        out_specs=pl.BlockSpec(memory_space=pl.ANY),
        scratch_shapes=[pltpu.VMEM((2, MOE_BLOCK, dp), jnp.uint32),
                        pltpu.VMEM((2, MOE_BLOCK, dp), jnp.uint32),
                        pltpu.SemaphoreType.DMA((2,)),
                        pltpu.SemaphoreType.DMA((2,))],
    )
    return pl.pallas_call(
        _expert_kernel,
        grid_spec=grid_spec,
        out_shape=jax.ShapeDtypeStruct((n_rows_out, dp), jnp.uint32),
        compiler_params=_cparams(("arbitrary",)),
        name="experts",
    )(block_expert, block_nreal, tok2, tok2, dst2, hn_packed, wg, wu, wd)


def _final_kernel(h_ref, y1_ref, y2_ref, route_ref, gain_ref, out_ref):
    r = route_ref[...]
    g1 = r[:, ROUTE_G1:ROUTE_G1 + 1]
    g2 = r[:, ROUTE_G2:ROUTE_G2 + 1]
    y = h_ref[...] + g1 * _unpack_bf16_pairs(y1_ref[...]) + g2 * _unpack_bf16_pairs(y2_ref[...])
    out_ref[...] = y * lax.rsqrt(jnp.mean(y * y, axis=-1, keepdims=True) + NORM_EPS) * gain_ref[...]


def _final(h, y, route, gain, tm, t_pad):
    t, d = h.shape
    nt = t_pad // tm
    return pl.pallas_call(
        _final_kernel,
        grid=(t // tm,),
        in_specs=[pl.BlockSpec((tm, d), lambda i: (i, 0)),
                  pl.BlockSpec((tm, d // 2), lambda i: (i, 0)),
                  pl.BlockSpec((tm, d // 2), lambda i: (i + nt, 0)),
                  pl.BlockSpec((tm, LANES), lambda i: (i, 0)),
                  pl.BlockSpec((1, d), lambda i: (0, 0))],
        out_specs=pl.BlockSpec((tm, d), lambda i: (i, 0)),
        out_shape=jax.ShapeDtypeStruct((t, d), F32),
        compiler_params=_cparams(("arbitrary",)),
        name="final",
    )(h, y, y, route, gain.reshape(1, d))


def _layer(h3, mix_gain, w_in, attn_gain, decay_f, decay_b, ret_gain, w_out, ffn_gain,
           w_rg, b_rg, w_re, b_re, w_eg, w_eu, w_ed, final_gain):
    b, s, d = h3.shape
    t = b * s
    tm = 512
    qa, ka, va, qr, kr, vr, gr = _inproj(h3, mix_gain, w_in.astype(BF16), tm)

    o_list, lse_list = [], []
    for window, dilation in DILATED_BRANCHES:
        reach = (window // 2) // dilation
        o, lse = _attn_branch(qa, ka, va, dilation, reach, tq=128, tqb=512)
        o_list.append(o.reshape(t, HEAD_GROUP_WIDTH))
        lse_list.append(lse.reshape(t, LANES))

    lg_f = jnp.log1p(-jnp.exp2(decay_f.astype(F32)))
    lg_b = jnp.log1p(-jnp.exp2(decay_b.astype(F32)))
    orr = _retention(qr, kr, vr, gr, lg_f, lg_b, ret_gain).reshape(t, HEAD_GROUP_WIDTH)

    n_route = MOE_GROUPS + N_EXPERTS
    w_router = jnp.zeros((d, LANES), F32).at[:, :n_route].set(jnp.concatenate([w_rg, w_re], axis=1).astype(F32))
    b_router = jnp.zeros((1, LANES), F32).at[0, :n_route].set(jnp.concatenate([b_rg, b_re]).astype(F32))
    h, hn_packed, logits = _outproj(h3.reshape(t, d), o_list, lse_list, orr, attn_gain, w_out.astype(BF16),
                                    ffn_gain, w_router, b_router, tm)

    route, counts8 = _route(logits, tm)

    n_assign = 2 * t
    n_blocks = -(-n_assign // MOE_BLOCK) + N_EXPERTS
    n_slots = n_blocks * MOE_BLOCK
    counts = counts8[0, :N_EXPERTS].astype(jnp.int32)
    padded = ((counts + MOE_BLOCK - 1) // MOE_BLOCK) * MOE_BLOCK
    pad_end = jnp.cumsum(padded)
    pad_start = pad_end - padded
    e12 = route[:, ROUTE_E1:ROUTE_E2 + 1].astype(jnp.int32)
    r12 = route[:, ROUTE_R1:ROUTE_R2 + 1].astype(jnp.int32)
    dest = pad_start[e12] + r12
    tok = jnp.broadcast_to(jnp.arange(t, dtype=jnp.int32)[:, None], (t, 2))
    t_pad = t
    out_row = tok + jnp.arange(2, dtype=jnp.int32)[None, :] * t_pad
    slot_tok = jnp.zeros((n_slots,), jnp.int32).at[dest.reshape(-1)].set(tok.reshape(-1))
    slot_dst = jnp.zeros((n_slots,), jnp.int32).at[dest.reshape(-1)].set(out_row.reshape(-1))
    block_start = jnp.arange(n_blocks, dtype=jnp.int32) * MOE_BLOCK
    block_expert = jnp.clip(jnp.searchsorted(pad_end, block_start, side='right'), 0, N_EXPERTS - 1).astype(jnp.int32)
    block_nreal = jnp.clip((pad_start + counts)[block_expert] - block_start, 0, MOE_BLOCK).astype(jnp.int32)

    y = _experts(hn_packed, slot_tok, slot_dst, block_expert, block_nreal, w_eg.astype(BF16),
                 w_eu.astype(BF16), w_ed.astype(BF16), 2 * t_pad)
    out = _final(h, y, route, final_gain, tm, t_pad)
    return out.reshape(b, s, d)


def kernel(x, mix_norm_gain, w_in, attn_out_gain, ret_decay_fwd, ret_decay_bwd, ret_out_gain, w_out,
           ffn_norm_gain, w_route_group, b_route_group, w_route_expert, b_route_expert,
           w_expert_gate, w_expert_up, w_expert_down, final_norm_gain):
    depth = mix_norm_gain.shape[0]
    assert depth == 1, "the final rmsnorm is fused into the single layer's combine kernel"
    l = 0
    return _layer(x, mix_norm_gain[l], w_in[l], attn_out_gain[l], ret_decay_fwd[l], ret_decay_bwd[l],
                  ret_out_gain[l], w_out[l], ffn_norm_gain[l], w_route_group[l], b_route_group[l],
                  w_route_expert[l], b_route_expert[l], w_expert_gate[l], w_expert_up[l], w_expert_down[l],
                  final_norm_gain)
```

```python
import functools

import numpy as np
import jax
import jax.numpy as jnp
from jax import lax
from jax.experimental import pallas as pl
from jax.experimental.pallas import tpu as pltpu

F32 = jnp.float32
BF16 = jnp.bfloat16

ATTN_HEADS = 8
HEAD_DIM = 64
RET_HEADS = 8
HEAD_GROUP_WIDTH = 512
N_PROJ_GROUPS = 7
DILATED_BRANCHES = ((128, 1), (512, 4), (2048, 16))
ROPE_THETA = 500000.0
ROPE_DIM = HEAD_DIM // 4
RET_THETA = 10000.0
RET_CHUNK = 128
MOE_GROUPS = 4
EXPERTS_PER_GROUP = 8
N_EXPERTS = MOE_GROUPS * EXPERTS_PER_GROUP
MOE_BLOCK = 256
NORM_EPS = 1e-6
NEG_INF = -1e30

LANES = 128
VMEM_LIMIT = 56 * 1024 * 1024


def _cparams(sem):
    return pltpu.CompilerParams(dimension_semantics=sem, vmem_limit_bytes=VMEM_LIMIT)


def _rotary_tables(seq, half, freqs):
    pos = np.arange(seq, dtype=np.float64)[:, None]
    ang = pos * freqs[None, :].astype(np.float64)
    cos, sin = np.cos(ang), np.sin(ang)
    c = np.ones((seq, HEAD_DIM)); sp = np.zeros((seq, HEAD_DIM)); sm = np.zeros((seq, HEAD_DIM))
    c[:, :half] = cos; c[:, half:2 * half] = cos
    sp[:, half:2 * half] = sin
    sm[:, :half] = -sin
    rep = LANES // HEAD_DIM
    return tuple(jnp.asarray(np.tile(t, (1, rep)), F32) for t in (c, sp, sm))


def _rotate(t, c, sp, sm, half):
    outs = []
    for g in range(t.shape[1] // LANES):
        tg = t[:, g * LANES:(g + 1) * LANES]
        outs.append(tg * c + pltpu.roll(tg, half, 1) * sp + pltpu.roll(tg, LANES - half, 1) * sm)
    return jnp.concatenate(outs, axis=1)


def _inproj_kernel(x_ref, gain_ref, w_ref, ca_ref, spa_ref, sma_ref, cr_ref, spr_ref, smr_ref,
                   qa_ref, ka_ref, va_ref, qr_ref, kr_ref, vr_ref, gr_ref):
    x = x_ref[0]
    ms = jnp.mean(x * x, axis=-1, keepdims=True)
    xn = (x * lax.rsqrt(ms + NORM_EPS) * gain_ref[...]).astype(BF16)
    gw = HEAD_GROUP_WIDTH

    def proj(c):
        return jnp.dot(xn, w_ref[:, c * gw:(c + 1) * gw], preferred_element_type=F32)

    a_tabs = (ca_ref[...], spa_ref[...], sma_ref[...])
    r_tabs = (cr_ref[...], spr_ref[...], smr_ref[...])
    qa_ref[0] = (_rotate(proj(0), *a_tabs, ROPE_DIM // 2) * (HEAD_DIM ** -0.5)).astype(BF16)
    ka_ref[0] = _rotate(proj(1), *a_tabs, ROPE_DIM // 2).astype(BF16)
    va_ref[0] = proj(2).astype(BF16)
    qr_ref[0] = _rotate(proj(3), *r_tabs, HEAD_DIM // 2).astype(BF16)
    kr_ref[0] = (_rotate(proj(4), *r_tabs, HEAD_DIM // 2) * (HEAD_DIM ** -0.5)).astype(BF16)
    vr_ref[0] = proj(5).astype(BF16)
    g = proj(6)
    gr_ref[0] = (g * jax.nn.sigmoid(g)).astype(BF16)


def _inproj(x, gain, w_in_bf16, tm):
    b, s, d = x.shape
    rope_freqs = ROPE_THETA ** (-np.arange(0, ROPE_DIM, 2, dtype=np.float32) / ROPE_DIM)
    ret_freqs = RET_THETA ** (-np.linspace(0.0, 1.0, HEAD_DIM // 2, dtype=np.float32))
    tabs = _rotary_tables(s, ROPE_DIM // 2, rope_freqs) + _rotary_tables(s, HEAD_DIM // 2, ret_freqs)
    tab_spec = pl.BlockSpec((tm, LANES), lambda si, bi: (si, 0))
    out_spec = pl.BlockSpec((1, tm, HEAD_GROUP_WIDTH), lambda si, bi: (bi, si, 0))
    out_shape = jax.ShapeDtypeStruct((b, s, HEAD_GROUP_WIDTH), BF16)
    return pl.pallas_call(
        _inproj_kernel,
        grid=(s // tm, b),
        in_specs=[pl.BlockSpec((1, tm, d), lambda si, bi: (bi, si, 0)),
                  pl.BlockSpec((1, d), lambda si, bi: (0, 0)),
                  pl.BlockSpec(w_in_bf16.shape, lambda si, bi: (0, 0))] + [tab_spec] * 6,
        out_specs=[out_spec] * N_PROJ_GROUPS,
        out_shape=[out_shape] * N_PROJ_GROUPS,
        compiler_params=_cparams(("arbitrary", "arbitrary")),
        name="inproj",
    )(x, gain.reshape(1, d), w_in_bf16, *tabs)


def _attn_kernel(q_ref, k_ref, v_ref, o_ref, lse_ref, *, length, tq, reach):
    tqb = q_ref.shape[1]
    win = tq + 2 * reach
    qi = pl.program_id(2)
    lane = lax.broadcasted_iota(jnp.int32, (1, LANES), 1)
    diff = (lax.broadcasted_iota(jnp.int32, (tq, win), 1)
            - lax.broadcasted_iota(jnp.int32, (tq, win), 0))
    lane_t = lax.broadcasted_iota(jnp.int32, (tq, LANES), 1)

    def sub(t, carry):
        q0 = qi * tqb + t * tq
        ws = jnp.clip(q0 - reach, 0, length - win)
        ws = pl.multiple_of(ws, reach)
        off = q0 - ws
        valid = (diff >= off - reach) & (diff <= off + reach)
        rows = pl.ds(pl.multiple_of(t * tq, tq), tq)
        lse_tile = jnp.zeros((tq, LANES), F32)
        for g in range(HEAD_GROUP_WIDTH // LANES):
            cols = slice(g * LANES, (g + 1) * LANES)
            qg = q_ref[0, rows, cols]
            kw = k_ref[0, pl.ds(ws, win), cols]
            vw = v_ref[0, pl.ds(ws, win), cols]
            o_pair = jnp.zeros((tq, LANES), F32)
            for hh in range(LANES // HEAD_DIM):
                hm = (lane >= hh * HEAD_DIM) & (lane < (hh + 1) * HEAD_DIM)
                qh = jnp.where(hm, qg, jnp.zeros_like(qg))
                sc = lax.dot_general(qh, kw, (((1,), (1,)), ((), ())), preferred_element_type=F32)
                sc = jnp.where(valid, sc, NEG_INF)
                m = jnp.max(sc, axis=-1, keepdims=True)
                p = jnp.exp(sc - m)
                l = jnp.sum(p, axis=-1, keepdims=True)
                pv = jnp.dot(p.astype(BF16), vw, preferred_element_type=F32)
                o_pair = jnp.where(hm, pv / l, o_pair)
                head = g * (LANES // HEAD_DIM) + hh
                lse_tile = jnp.where(lane_t == head, m + jnp.log(l), lse_tile)
            o_ref[0, rows, cols] = o_pair.astype(BF16)
        lse_ref[0, rows, :] = lse_tile
        return carry

    lax.fori_loop(0, tqb // tq, sub, 0)


def _attn_branch(q, k, v, dilation, reach, tq, tqb):
    b, s, w = q.shape
    length = s // dilation
    tqb = min(tqb, length)

    def to_class(t):
        return t.reshape(b, length, dilation * t.shape[-1])

    qc, kc, vc = to_class(q), to_class(k), to_class(v)
    q_spec = pl.BlockSpec((1, tqb, w), lambda bi, r, qi: (bi, qi, r))
    kv_spec = pl.BlockSpec((1, length, w), lambda bi, r, qi: (bi, 0, r))
    o, lse = pl.pallas_call(
        functools.partial(_attn_kernel, length=length, tq=tq, reach=reach),
        grid=(b, dilation, length // tqb),
        in_specs=[q_spec, kv_spec, kv_spec],
        out_specs=[q_spec, pl.BlockSpec((1, tqb, LANES), lambda bi, r, qi: (bi, qi, r))],
        out_shape=[jax.ShapeDtypeStruct((b, length, dilation * w), BF16),
                   jax.ShapeDtypeStruct((b, length, dilation * LANES), F32)],
        compiler_params=_cparams(("arbitrary", "arbitrary", "arbitrary")),
        name=f"attn_d{dilation}",
    )(qc, kc, vc)
    return o.reshape(b, s, w), lse.reshape(b, s, LANES)


RET_TAB_QF, RET_TAB_QB, RET_TAB_KF, RET_TAB_KB = range(4)


def _retention_kernel(lgf_ref, lgb_ref, q_ref, k_ref, v_ref, g_ref, gain_ref, o_ref,
                      tab_ref, dec_ref, sb_ref, st_ref, *, chunk, unroll):
    c = chunk
    n = q_ref.shape[1] // c
    width = q_ref.shape[2]
    n_pairs = width // LANES
    heads_per_pair = LANES // HEAD_DIM
    n_heads = n_pairs * heads_per_pair
    head0 = pl.program_id(1) * n_heads
    lane_w = lax.broadcasted_iota(jnp.int32, (1, width), 1)
    lgf = [lgf_ref[head0 + hd] for hd in range(n_heads)]
    lgb = [lgb_ref[head0 + hd] for hd in range(n_heads)]
    lgf_lane = jnp.zeros((1, width), F32)
    lgb_lane = jnp.zeros((1, width), F32)
    for hd in range(n_heads):
        in_head = (lane_w >= hd * HEAD_DIM) & (lane_w < (hd + 1) * HEAD_DIM)
        lgf_lane = jnp.where(in_head, lgf[hd], lgf_lane)
        lgb_lane = jnp.where(in_head, lgb[hd], lgb_lane)
    idx = lax.broadcasted_iota(jnp.int32, (c, width), 0).astype(F32)
    tab_ref[RET_TAB_QF] = jnp.exp((idx + 1.0) * lgf_lane)
    tab_ref[RET_TAB_QB] = jnp.exp((c - idx) * lgb_lane)
    tab_ref[RET_TAB_KF] = jnp.exp((c - 1.0 - idx) * lgf_lane)
    tab_ref[RET_TAB_KB] = jnp.exp(idx * lgb_lane)
    sdf = jnp.exp(c * lgf_lane)
    sdb = jnp.exp(c * lgb_lane)
    dmat = (lax.broadcasted_iota(jnp.int32, (c, c), 0)
            - lax.broadcasted_iota(jnp.int32, (c, c), 1)).astype(F32)
    for hd in range(n_heads):
        dec_ref[hd] = jnp.where(dmat >= 0, jnp.exp(dmat * lgf[hd]), jnp.exp(-dmat * lgb[hd]))
    lane = lax.broadcasted_iota(jnp.int32, (1, LANES), 1)
    lane_hi = lane >= HEAD_DIM
    row_hi = lax.broadcasted_iota(jnp.int32, (LANES, LANES), 0) >= HEAD_DIM
    col_hi = lax.broadcasted_iota(jnp.int32, (LANES, LANES), 1) >= HEAD_DIM
    blockdiag = row_hi == col_hi

    def kv_state(kd, vv):
        kt = jnp.transpose(kd).astype(BF16)
        return jnp.where(blockdiag, jnp.dot(kt, vv, preferred_element_type=F32), 0.0)

    st_ref[...] = jnp.zeros_like(st_ref)
    sb_ref[n - 1] = jnp.zeros(sb_ref.shape[1:], sb_ref.dtype)

    def back(i, carry):
        nn = n - 1 - i
        rows = pl.ds(pl.multiple_of(nn * c, c), c)
        for p in range(n_pairs):
            cols = slice(p * LANES, (p + 1) * LANES)
            kd = k_ref[0, rows, cols].astype(F32) * tab_ref[RET_TAB_KB, :, cols]
            new = st_ref[p] * sdb[:, cols] + kv_state(kd, v_ref[0, rows, cols])
            st_ref[p] = new
            sb_ref[nn - 1, p] = new.astype(BF16)
        return carry

    lax.fori_loop(0, n - 1, back, 0, unroll=unroll)

    st_ref[...] = jnp.zeros_like(st_ref)

    def fwd(nn, carry):
        rows = pl.ds(pl.multiple_of(nn * c, c), c)
        for p in range(n_pairs):
            cols = slice(p * LANES, (p + 1) * LANES)
            qq = q_ref[0, rows, cols]
            kk = k_ref[0, rows, cols]
            vv = v_ref[0, rows, cols]
            qf = qq.astype(F32)
            sf = st_ref[p]
            qcat = jnp.concatenate([(qf * tab_ref[RET_TAB_QF, :, cols]).astype(BF16),
                                    (qf * tab_ref[RET_TAB_QB, :, cols]).astype(BF16)], axis=1)
            scat = jnp.concatenate([sf.astype(BF16), sb_ref[nn, p]], axis=0)
            o = jnp.dot(qcat, scat, preferred_element_type=F32)
            ps, vs = [], []
            for hh in range(heads_per_pair):
                hm = (lane >= hh * HEAD_DIM) & (lane < (hh + 1) * HEAD_DIM)
                qh = jnp.where(hm, qq, jnp.zeros_like(qq))
                sc = lax.dot_general(qh, kk, (((1,), (1,)), ((), ())), preferred_element_type=F32)
                ps.append((sc * dec_ref[p * heads_per_pair + hh]).astype(BF16))
                vs.append(jnp.where(hm, vv, jnp.zeros_like(vv)))
            o = o + jnp.dot(jnp.concatenate(ps, axis=1), jnp.concatenate(vs, axis=0),
                            preferred_element_type=F32)
            s_lo = jnp.sum(jnp.where(lane_hi, 0.0, o), axis=-1, keepdims=True)
            s_hi = jnp.sum(jnp.where(lane_hi, o, 0.0), axis=-1, keepdims=True)
            mu = jnp.where(lane_hi, s_hi, s_lo) * (1.0 / HEAD_DIM)
            dev = o - mu
            d2 = dev * dev
            v_lo = jnp.sum(jnp.where(lane_hi, 0.0, d2), axis=-1, keepdims=True)
            v_hi = jnp.sum(jnp.where(lane_hi, d2, 0.0), axis=-1, keepdims=True)
            var = jnp.where(lane_hi, v_hi, v_lo) * (1.0 / HEAD_DIM)
            out = dev * lax.rsqrt(var + NORM_EPS) * gain_ref[:, cols] * g_ref[0, rows, cols].astype(F32)
            o_ref[0, rows, cols] = out.astype(BF16)
            kd = kk.astype(F32) * tab_ref[RET_TAB_KF, :, cols]
            st_ref[p] = sf * sdf[:, cols] + kv_state(kd, vv)
        return carry

    lax.fori_loop(0, n, fwd, 0, unroll=unroll)


def _retention(qr, kr, vr, gate, lg_f, lg_b, out_gain, width=256, unroll=1):
    b, s, w = qr.shape
    n_pairs = width // LANES
    n_heads = width // HEAD_DIM
    spec = pl.BlockSpec((1, s, width), lambda bi, p, *_: (bi, 0, p))
    grid_spec = pltpu.PrefetchScalarGridSpec(
        num_scalar_prefetch=2,
        grid=(b, w // width),
        in_specs=[spec, spec, spec, spec, pl.BlockSpec((1, width), lambda bi, p, *_: (0, p))],
        out_specs=spec,
        scratch_shapes=[pltpu.VMEM((4, RET_CHUNK, width), F32),
                        pltpu.VMEM((n_heads, RET_CHUNK, RET_CHUNK), F32),
                        pltpu.VMEM((s // RET_CHUNK, n_pairs, LANES, LANES), BF16),
                        pltpu.VMEM((n_pairs, LANES, LANES), F32)],
    )
    return pl.pallas_call(
        functools.partial(_retention_kernel, chunk=RET_CHUNK, unroll=unroll),
        grid_spec=grid_spec,
        out_shape=jax.ShapeDtypeStruct((b, s, w), BF16),
        compiler_params=_cparams(("arbitrary", "arbitrary")),
        name="retention",
    )(lg_f, lg_b, qr, kr, vr, gate, out_gain.reshape(1, w))


def _split_bf16(t):
    hi = t.astype(BF16)
    lo = (t - hi.astype(F32)).astype(BF16)
    return hi, lo


def _pack_bf16_pairs(t):
    n = t.shape[1] // 2
    hi = pltpu.bitcast(t[:, :n].astype(BF16).astype(F32), jnp.uint32)
    lo = pltpu.bitcast(t[:, n:].astype(BF16).astype(F32), jnp.uint32)
    return hi | (lo >> 16)


def _unpack_bf16_pairs(u):
    hi = pltpu.bitcast(u & jnp.uint32(0xFFFF0000), F32)
    lo = pltpu.bitcast(u << 16, F32)
    return jnp.concatenate([hi, lo], axis=1)


def _outproj_kernel(x_ref, o1_ref, o2_ref, o3_ref, l1_ref, l2_ref, l3_ref, orr_ref, ga_ref, expand_ref,
                    wout_ref, gf_ref, wr_hi_ref, wr_lo_ref, br_ref, h_ref, hn_ref, logit_ref):
    ls = [l1_ref[...], l2_ref[...], l3_ref[...]]
    mx = jnp.maximum(jnp.maximum(ls[0], ls[1]), ls[2])
    es = [jnp.exp(l - mx) for l in ls]
    inv = 1.0 / (es[0] + es[1] + es[2])
    expand = expand_ref[...]
    oa = jnp.zeros(o1_ref.shape, F32)
    for e, o_ref in zip(es, (o1_ref, o2_ref, o3_ref)):
        w_hi, w_lo = _split_bf16(e * inv)
        wexp = (jnp.dot(w_hi, expand, preferred_element_type=F32)
                + jnp.dot(w_lo, expand, preferred_element_type=F32))
        oa = oa + wexp * o_ref[...].astype(F32)
    oa = oa * lax.rsqrt(jnp.mean(oa * oa, axis=-1, keepdims=True) + NORM_EPS) * ga_ref[...]
    mixed = jnp.concatenate([oa.astype(BF16), orr_ref[...]], axis=1)
    h = x_ref[...] + jnp.dot(mixed, wout_ref[...], preferred_element_type=F32)
    h_ref[...] = h
    hn = h * lax.rsqrt(jnp.mean(h * h, axis=-1, keepdims=True) + NORM_EPS) * gf_ref[...]
    hn_ref[...] = _pack_bf16_pairs(hn)
    hn_hi, hn_lo = _split_bf16(hn)
    wr_hi = wr_hi_ref[...]
    logit_ref[...] = (jnp.dot(hn_hi, wr_hi, preferred_element_type=F32)
                      + jnp.dot(hn_lo, wr_hi, preferred_element_type=F32)
                      + jnp.dot(hn_hi, wr_lo_ref[...], preferred_element_type=F32)
                      + br_ref[...])


def _outproj(x2, o_list, lse_list, orr, attn_gain, w_out_bf16, ffn_gain, w_router, b_router, tm):
    t, d = x2.shape
    w = HEAD_GROUP_WIDTH
    expand = np.zeros((LANES, w), np.float32)
    for hd in range(ATTN_HEADS):
        expand[hd, hd * HEAD_DIM:(hd + 1) * HEAD_DIM] = 1.0
    expand = jnp.asarray(expand, BF16)
    wr_hi = w_router.astype(BF16)
    wr_lo = (w_router - wr_hi.astype(F32)).astype(BF16)
    row = lambda width: pl.BlockSpec((tm, width), lambda i: (i, 0))
    full = lambda a: pl.BlockSpec(a.shape, lambda i: (0,) * a.ndim)
    ga = attn_gain.reshape(1, w)
    gf = ffn_gain.reshape(1, d)
    return pl.pallas_call(
        _outproj_kernel,
        grid=(t // tm,),
        in_specs=[row(d), row(w), row(w), row(w), row(LANES), row(LANES), row(LANES), row(w),
                  full(ga), full(expand), full(w_out_bf16), full(gf), full(wr_hi), full(wr_lo), full(b_router)],
        out_specs=[row(d), row(d // 2), row(LANES)],
        out_shape=[jax.ShapeDtypeStruct((t, d), F32),
                   jax.ShapeDtypeStruct((t, d // 2), jnp.uint32),
                   jax.ShapeDtypeStruct((t, LANES), F32)],
        compiler_params=_cparams(("arbitrary",)),
        name="outproj",
    )(x2, *o_list, *lse_list, orr, ga, expand, w_out_bf16, gf, wr_hi, wr_lo, b_router)


ROUTE_E1, ROUTE_E2, ROUTE_G1, ROUTE_G2, ROUTE_R1, ROUTE_R2 = range(6)
EXPERT_LANE0 = MOE_GROUPS


def _route_kernel(logit_ref, tri_ref, route_ref, count_ref, run_ref):
    @pl.when(pl.program_id(0) == 0)
    def _():
        run_ref[...] = jnp.zeros_like(run_ref)

    lg = logit_ref[...]
    tm = lg.shape[0]
    lane = lax.broadcasted_iota(jnp.int32, lg.shape, 1)
    big = jnp.int32(1 << 20)

    def top(vals):
        m = jnp.max(vals, axis=-1, keepdims=True)
        i = jnp.min(jnp.where(vals == m, lane, big), axis=-1, keepdims=True)
        return m, i

    gl = jnp.where(lane < MOE_GROUPS, lg, -jnp.inf)
    gmax, gidx = top(gl)
    group_gate = 1.0 / jnp.sum(jnp.exp(gl - gmax), axis=-1, keepdims=True)
    lo = EXPERT_LANE0 + gidx * EXPERTS_PER_GROUP
    el = jnp.where((lane >= lo) & (lane < lo + EXPERTS_PER_GROUP), lg, -jnp.inf)
    t1, i1 = top(el)
    t2, i2 = top(jnp.where(lane == i1, -jnp.inf, el))
    e21 = jnp.exp(t2 - t1)
    g1 = group_gate / (1.0 + e21)
    g2 = group_gate * e21 / (1.0 + e21)
    e1 = i1 - EXPERT_LANE0
    e2 = i2 - EXPERT_LANE0
    oh1 = lane == e1
    oh2 = lane == e2
    cnt = oh1.astype(F32) + oh2.astype(F32)
    prefix = jnp.dot(tri_ref[...], cnt.astype(BF16), preferred_element_type=F32) + run_ref[0:1, :]
    r1 = jnp.sum(jnp.where(oh1, prefix, 0.0), axis=-1, keepdims=True)
    r2 = jnp.sum(jnp.where(oh2, prefix, 0.0), axis=-1, keepdims=True)
    new_run = run_ref[0:1, :] + jnp.sum(cnt, axis=0, keepdims=True)
    run_ref[...] = jnp.broadcast_to(new_run, run_ref.shape)
    count_ref[...] = jnp.broadcast_to(new_run, count_ref.shape)
    out = jnp.zeros(lg.shape, F32)
    for ln, val in ((ROUTE_E1, e1.astype(F32)), (ROUTE_E2, e2.astype(F32)), (ROUTE_G1, g1),
                    (ROUTE_G2, g2), (ROUTE_R1, r1), (ROUTE_R2, r2)):
        out = jnp.where(lane == ln, val, out)
    route_ref[...] = out


def _route(logits, tm):
    t = logits.shape[0]
    tri = jnp.asarray(np.tril(np.ones((tm, tm), np.float32), -1), BF16)
    return pl.pallas_call(
        _route_kernel,
        grid=(t // tm,),
        in_specs=[pl.BlockSpec((tm, LANES), lambda i: (i, 0)), pl.BlockSpec((tm, tm), lambda i: (0, 0))],
        out_specs=[pl.BlockSpec((tm, LANES), lambda i: (i, 0)), pl.BlockSpec((8, LANES), lambda i: (0, 0))],
        out_shape=[jax.ShapeDtypeStruct((t, LANES), F32), jax.ShapeDtypeStruct((8, LANES), F32)],
        scratch_shapes=[pltpu.VMEM((8, LANES), F32)],
        compiler_params=_cparams(("arbitrary",)),
        name="route",
    )(logits, tri)


TOP_K = 2
DMA_ISSUE_UNROLL = 8


SEG_ALIGN = 8
SORTED_TAIL = N_EXPERTS * SEG_ALIGN + MOE_BLOCK


def _dispatch_kernel(seg_end_ref, pos_ref, hn_ref, xs_hbm, zbuf, sem, zsem, *, n_rows):
    tm = hn_ref.shape[0]

    @pl.when(pl.program_id(0) == 0)
    def _():
        zbuf[...] = jnp.zeros_like(zbuf)
        tail = pltpu.make_async_copy(zbuf, xs_hbm.at[pl.ds(n_rows, SORTED_TAIL)], zsem)
        tail.start()
        tail.wait()

        def hole(e):
            end = seg_end_ref[e]
            start = pl.multiple_of(end - (end & (SEG_ALIGN - 1)), SEG_ALIGN)
            return pltpu.make_async_copy(zbuf.at[pl.ds(0, SEG_ALIGN)], xs_hbm.at[pl.ds(start, SEG_ALIGN)], zsem)

        for e in range(N_EXPERTS):
            @pl.when((seg_end_ref[e] & (SEG_ALIGN - 1)) != 0)
            def _():
                hole(e).start()
        for e in range(N_EXPERTS):
            @pl.when((seg_end_ref[e] & (SEG_ALIGN - 1)) != 0)
            def _():
                hole(e).wait()

    def body(j, c):
        for k in range(TOP_K):
            pltpu.make_async_copy(hn_ref.at[pl.ds(j, 1)], xs_hbm.at[pl.ds(pos_ref[0, 0, TOP_K * j + k], 1)],
                                  sem).start()
        return c
    lax.fori_loop(0, tm, body, 0, unroll=DMA_ISSUE_UNROLL)

    tile = pltpu.make_async_copy(hn_ref, xs_hbm.at[pl.ds(0, tm)], sem)
    for k in range(TOP_K):
        tile.wait()


def _dispatch(hn_packed, pos, seg_end, tm):
    t, dp = hn_packed.shape
    n_rows = TOP_K * t
    pos3 = pos.reshape(t // tm, 1, TOP_K * tm)
    grid_spec = pltpu.PrefetchScalarGridSpec(
        num_scalar_prefetch=1,
        grid=(t // tm,),
        in_specs=[pl.BlockSpec((1, 1, TOP_K * tm), lambda i, se: (i, 0, 0), memory_space=pltpu.SMEM),
                  pl.BlockSpec((tm, dp), lambda i, se: (i, 0))],
        out_specs=pl.BlockSpec(memory_space=pl.ANY),
        scratch_shapes=[pltpu.VMEM((SORTED_TAIL, dp), jnp.uint32),
                        pltpu.SemaphoreType.DMA,
                        pltpu.SemaphoreType.DMA],
    )
    return pl.pallas_call(
        functools.partial(_dispatch_kernel, n_rows=n_rows),
        grid_spec=grid_spec,
        out_shape=jax.ShapeDtypeStruct((n_rows + SORTED_TAIL, dp), jnp.uint32),
        compiler_params=_cparams(("arbitrary",)),
        name="dispatch",
    )(seg_end, pos3, hn_packed)


def _expert_kernel(bexp_ref, nreal_ref, row0_ref, xs_hbm, wg_ref, wu_ref, wd_ref, ys_hbm,
                   xbuf, ybuf, isem, osem, *, n_rows):
    del bexp_ref
    i = pl.program_id(0)
    nb = pl.num_programs(0)
    slot = i % 2
    nslot = 1 - slot
    n_cur = nreal_ref[i]
    prev = jnp.maximum(i - 1, 0)
    nxt = jnp.minimum(i + 1, nb - 1)

    def in_copy(blk, s):
        row0 = pl.multiple_of(row0_ref[blk], SEG_ALIGN)
        return pltpu.make_async_copy(xs_hbm.at[pl.ds(row0, MOE_BLOCK)], xbuf.at[s], isem.at[s])

    def out_copy(blk, s):
        row0 = pl.multiple_of(row0_ref[blk], SEG_ALIGN)
        return pltpu.make_async_copy(ybuf.at[s], ys_hbm.at[pl.ds(row0, MOE_BLOCK)], osem.at[s])

    @pl.when(i == 0)
    def _():
        ybuf[...] = jnp.zeros_like(ybuf)
        tails = [pltpu.make_async_copy(ybuf.at[s], ys_hbm.at[pl.ds(n_rows + s * MOE_BLOCK, MOE_BLOCK)], osem.at[s])
                 for s in range(SORTED_TAIL // MOE_BLOCK)]
        for tail in tails:
            tail.start()
        for tail in tails:
            tail.wait()

        @pl.when(n_cur > 0)
        def _():
            in_copy(i, slot).start()

    @pl.when((i + 1 < nb) & (nreal_ref[nxt] > 0))
    def _():
        in_copy(nxt, nslot).start()

    @pl.when(n_cur > 0)
    def _():
        in_copy(i, slot).wait()
        xb = _unpack_bf16_pairs(xbuf[slot]).astype(BF16)
        gate = jnp.dot(xb, wg_ref[0], preferred_element_type=F32)
        up = jnp.dot(xb, wu_ref[0], preferred_element_type=F32)
        hid = (gate * jax.nn.sigmoid(gate) * up).astype(BF16)
        ybuf[slot] = _pack_bf16_pairs(jnp.dot(hid, wd_ref[0], preferred_element_type=F32))

    @pl.when((i >= 1) & (nreal_ref[prev] > 0))
    def _():
        out_copy(prev, nslot).wait()

    @pl.when(n_cur > 0)
    def _():
        out_copy(i, slot).start()

        @pl.when(i == nb - 1)
        def _():
            out_copy(i, slot).wait()


def _experts(xs, block_expert, block_nreal, block_row0, wg, wu, wd):
    n_blocks = block_expert.shape[0]
    assert n_blocks >= 2 and SORTED_TAIL == 2 * MOE_BLOCK
    n_rows_pad, dp = xs.shape
    _, d, ff = wg.shape
    grid_spec = pltpu.PrefetchScalarGridSpec(
        num_scalar_prefetch=3,
        grid=(n_blocks,),
        in_specs=[pl.BlockSpec(memory_space=pl.ANY),
                  pl.BlockSpec((1, d, ff), lambda i, be, nr, r0: (be[i], 0, 0)),
                  pl.BlockSpec((1, d, ff), lambda i, be, nr, r0: (be[i], 0, 0)),
                  pl.BlockSpec((1, ff, d), lambda i, be, nr, r0: (be[i], 0, 0))],
        out_specs=pl.BlockSpec(memory_space=pl.ANY),
        scratch_shapes=[pltpu.VMEM((2, MOE_BLOCK, dp), jnp.uint32),
                        pltpu.VMEM((2, MOE_BLOCK, dp), jnp.uint32),
                        pltpu.SemaphoreType.DMA((2,)),
                        pltpu.SemaphoreType.DMA((2,))],
    )
    return pl.pallas_call(
        functools.partial(_expert_kernel, n_rows=n_rows_pad - SORTED_TAIL),
        grid_spec=grid_spec,
        out_shape=jax.ShapeDtypeStruct((n_rows_pad, dp), jnp.uint32),
        compiler_params=_cparams(("arbitrary",)),
        name="experts",
    )(block_expert, block_nreal, block_row0, xs, wg, wu, wd)


def _final_kernel(pos_ref, pos_next_ref, h_ref, route_ref, gain_ref, ys_hbm, out_ref, ybuf, sem):
    i = pl.program_id(0)
    nt = pl.num_programs(0)
    slot = i % 2
    tm = h_ref.shape[0]

    def gather_start(idx_ref, s):
        def body(j, c):
            for k in range(TOP_K):
                pltpu.make_async_copy(ys_hbm.at[pl.ds(idx_ref[0, 0, TOP_K * j + k], 1)],
                                      ybuf.at[s, k, pl.ds(j, 1)], sem.at[s]).start()
            return c
        lax.fori_loop(0, tm, body, 0, unroll=DMA_ISSUE_UNROLL)

    @pl.when(i == 0)
    def _():
        gather_start(pos_ref, slot)

    @pl.when(i + 1 < nt)
    def _():
        gather_start(pos_next_ref, 1 - slot)

    for k in range(TOP_K):
        pltpu.make_async_copy(ys_hbm.at[pl.ds(0, tm)], ybuf.at[slot, k], sem.at[slot]).wait()

    r = route_ref[...]
    y = h_ref[...]
    for k, gate_lane in enumerate((ROUTE_G1, ROUTE_G2)):
        y = y + r[:, gate_lane:gate_lane + 1] * _unpack_bf16_pairs(ybuf[slot, k])
    out_ref[...] = y * lax.rsqrt(jnp.mean(y * y, axis=-1, keepdims=True) + NORM_EPS) * gain_ref[...]


def _final(h, ys, pos, route, gain, tm):
    t, d = h.shape
    nt = t // tm
    pos3 = pos.reshape(nt, 1, TOP_K * tm)
    smem_blk = lambda imap: pl.BlockSpec((1, 1, TOP_K * tm), imap, memory_space=pltpu.SMEM)
    return pl.pallas_call(
        _final_kernel,
        grid=(nt,),
        in_specs=[smem_blk(lambda i: (i, 0, 0)),
                  smem_blk(lambda i: (jnp.minimum(i + 1, nt - 1), 0, 0)),
                  pl.BlockSpec((tm, d), lambda i: (i, 0)),
                  pl.BlockSpec((tm, LANES), lambda i: (i, 0)),
                  pl.BlockSpec((1, d), lambda i: (0, 0)),
                  pl.BlockSpec(memory_space=pl.ANY)],
        out_specs=pl.BlockSpec((tm, d), lambda i: (i, 0)),
        out_shape=jax.ShapeDtypeStruct((t, d), F32),
        scratch_shapes=[pltpu.VMEM((2, TOP_K, tm, d // 2), jnp.uint32),
                        pltpu.SemaphoreType.DMA((2,))],
        compiler_params=_cparams(("arbitrary",)),
        name="final",
    )(pos3, pos3, h, route, gain.reshape(1, d), ys)


def _layer(h3, mix_gain, w_in, attn_gain, decay_f, decay_b, ret_gain, w_out, ffn_gain,
           w_rg, b_rg, w_re, b_re, w_eg, w_eu, w_ed, final_gain):
    b, s, d = h3.shape
    t = b * s
    tm = 512
    qa, ka, va, qr, kr, vr, gr = _inproj(h3, mix_gain, w_in.astype(BF16), tm)

    o_list, lse_list = [], []
    for window, dilation in DILATED_BRANCHES:
        reach = (window // 2) // dilation
        o, lse = _attn_branch(qa, ka, va, dilation, reach, tq=128, tqb=512)
        o_list.append(o.reshape(t, HEAD_GROUP_WIDTH))
        lse_list.append(lse.reshape(t, LANES))

    lg_f = jnp.log1p(-jnp.exp2(decay_f.astype(F32)))
    lg_b = jnp.log1p(-jnp.exp2(decay_b.astype(F32)))
    orr = _retention(qr, kr, vr, gr, lg_f, lg_b, ret_gain).reshape(t, HEAD_GROUP_WIDTH)

    n_route = MOE_GROUPS + N_EXPERTS
    w_router = jnp.zeros((d, LANES), F32).at[:, :n_route].set(jnp.concatenate([w_rg, w_re], axis=1).astype(F32))
    b_router = jnp.zeros((1, LANES), F32).at[0, :n_route].set(jnp.concatenate([b_rg, b_re]).astype(F32))
    h, hn_packed, logits = _outproj(h3.reshape(t, d), o_list, lse_list, orr, attn_gain, w_out.astype(BF16),
                                    ffn_gain, w_router, b_router, tm)

    route, counts8 = _route(logits, tm)

    n_blocks = -(-TOP_K * t // MOE_BLOCK) + N_EXPERTS
    counts = counts8[0, :N_EXPERTS].astype(jnp.int32)
    aligned = ((counts + SEG_ALIGN - 1) // SEG_ALIGN) * SEG_ALIGN
    seg_start = jnp.cumsum(aligned) - aligned
    expert_iota = jnp.arange(N_EXPERTS, dtype=jnp.int32)
    e12 = route[:, ROUTE_E1:ROUTE_E2 + 1].astype(jnp.int32)
    r12 = route[:, ROUTE_R1:ROUTE_R2 + 1].astype(jnp.int32)
    pos = r12 + jnp.sum(jnp.where(e12[..., None] == expert_iota, seg_start, 0), axis=-1)
    nblk = (counts + MOE_BLOCK - 1) // MOE_BLOCK
    blk_end = jnp.cumsum(nblk)
    blk_start = blk_end - nblk
    blk = jnp.arange(n_blocks, dtype=jnp.int32)[:, None]
    owner = (blk >= blk_start) & (blk < blk_end)
    local = (blk - blk_start) * MOE_BLOCK
    block_row0 = jnp.sum(jnp.where(owner, seg_start + local, 0), axis=-1).astype(jnp.int32)
    block_nreal = jnp.sum(jnp.where(owner, jnp.clip(counts - local, 0, MOE_BLOCK), 0), axis=-1).astype(jnp.int32)
    block_expert = jnp.minimum(jnp.sum((blk >= blk_end).astype(jnp.int32), axis=-1), N_EXPERTS - 1)

    xs = _dispatch(hn_packed, pos, seg_start + counts, tm)
    ys = _experts(xs, block_expert, block_nreal, block_row0, w_eg.astype(BF16), w_eu.astype(BF16),
                  w_ed.astype(BF16))
    out = _final(h, ys, pos, route, final_gain, tm)
    return out.reshape(b, s, d)


def kernel(x, mix_norm_gain, w_in, attn_out_gain, ret_decay_fwd, ret_decay_bwd, ret_out_gain, w_out,
           ffn_norm_gain, w_route_group, b_route_group, w_route_expert, b_route_expert,
           w_expert_gate, w_expert_up, w_expert_down, final_norm_gain):
    depth = mix_norm_gain.shape[0]
    assert depth == 1, "the final rmsnorm is fused into the single layer's combine kernel"
    l = 0
    return _layer(x, mix_norm_gain[l], w_in[l], attn_out_gain[l], ret_decay_fwd[l], ret_decay_bwd[l],
                  ret_out_gain[l], w_out[l], ffn_norm_gain[l], w_route_group[l], b_route_group[l],
                  w_route_expert[l], b_route_expert[l], w_expert_gate[l], w_expert_up[l], w_expert_down[l],
                  final_norm_gain)
```

```python
import functools

import numpy as np
import jax
import jax.numpy as jnp
from jax import lax
from jax.experimental import pallas as pl
from jax.experimental.pallas import tpu as pltpu

F32 = jnp.float32
BF16 = jnp.bfloat16

ATTN_HEADS = 8
HEAD_DIM = 64
RET_HEADS = 8
HEAD_GROUP_WIDTH = 512
N_PROJ_GROUPS = 7
DILATED_BRANCHES = ((128, 1), (512, 4), (2048, 16))
ROPE_THETA = 500000.0
ROPE_DIM = HEAD_DIM // 4
RET_THETA = 10000.0
RET_CHUNK = 128
MOE_GROUPS = 4
EXPERTS_PER_GROUP = 8
N_EXPERTS = MOE_GROUPS * EXPERTS_PER_GROUP
MOE_BLOCK = 256
NORM_EPS = 1e-6
NEG_INF = -1e30

LANES = 128
VMEM_LIMIT = 56 * 1024 * 1024


def _cparams(sem):
    return pltpu.CompilerParams(dimension_semantics=sem, vmem_limit_bytes=VMEM_LIMIT)


def _rotary_tables(seq, half, freqs):
    pos = np.arange(seq, dtype=np.float64)[:, None]
    ang = pos * freqs[None, :].astype(np.float64)
    cos, sin = np.cos(ang), np.sin(ang)
    c = np.ones((seq, HEAD_DIM)); sp = np.zeros((seq, HEAD_DIM)); sm = np.zeros((seq, HEAD_DIM))
    c[:, :half] = cos; c[:, half:2 * half] = cos
    sp[:, half:2 * half] = sin
    sm[:, :half] = -sin
    rep = LANES // HEAD_DIM
    return tuple(jnp.asarray(np.tile(t, (1, rep)), F32) for t in (c, sp, sm))


def _rotate(t, c, sp, sm, half):
    outs = []
    for g in range(t.shape[1] // LANES):
        tg = t[:, g * LANES:(g + 1) * LANES]
        outs.append(tg * c + pltpu.roll(tg, half, 1) * sp + pltpu.roll(tg, LANES - half, 1) * sm)
    return jnp.concatenate(outs, axis=1)


CLASS_DILATIONS = tuple(d for _, d in DILATED_BRANCHES if d > 1)


def _inproj_kernel(x_ref, gain_ref, w_ref, ca_ref, spa_ref, sma_ref, cr_ref, spr_ref, smr_ref, *rest):
    n_cls = len(CLASS_DILATIONS)
    nat_refs = rest[0:3]
    cls_refs = [rest[3 + 3 * c:6 + 3 * c] for c in range(n_cls)]
    qr_ref, kr_ref, vr_ref, gr_ref = rest[3 + 3 * n_cls:7 + 3 * n_cls]
    stage_ref = rest[7 + 3 * n_cls]
    x = x_ref[0]
    tm = x.shape[0]
    ms = jnp.mean(x * x, axis=-1, keepdims=True)
    xn = (x * lax.rsqrt(ms + NORM_EPS) * gain_ref[...]).astype(BF16)
    gw = HEAD_GROUP_WIDTH

    def proj(c):
        return jnp.dot(xn, w_ref[:, c * gw:(c + 1) * gw], preferred_element_type=F32)

    a_tabs = (ca_ref[...], spa_ref[...], sma_ref[...])
    r_tabs = (cr_ref[...], spr_ref[...], smr_ref[...])
    attn_vals = ((_rotate(proj(0), *a_tabs, ROPE_DIM // 2) * (HEAD_DIM ** -0.5)),
                 _rotate(proj(1), *a_tabs, ROPE_DIM // 2),
                 proj(2))
    for j, val in enumerate(attn_vals):
        nat_refs[j][0] = val.astype(BF16)
        for g in range(gw // LANES):
            stage_ref[g] = val[:, g * LANES:(g + 1) * LANES]
        for c, d in enumerate(CLASS_DILATIONS):
            for r in range(d):
                for g in range(gw // LANES):
                    col = r * gw + g * LANES
                    cls_refs[c][j][0, :, col:col + LANES] = (
                        stage_ref[g, pl.ds(r, tm // d, stride=d), :].astype(BF16))
    qr_ref[0] = _rotate(proj(3), *r_tabs, HEAD_DIM // 2).astype(BF16)
    kr_ref[0] = (_rotate(proj(4), *r_tabs, HEAD_DIM // 2) * (HEAD_DIM ** -0.5)).astype(BF16)
    vr_ref[0] = proj(5).astype(BF16)
    g = proj(6)
    gr_ref[0] = (g * jax.nn.sigmoid(g)).astype(BF16)


def _inproj(x, gain, w_in_bf16, tm):
    b, s, d = x.shape
    rope_freqs = ROPE_THETA ** (-np.arange(0, ROPE_DIM, 2, dtype=np.float32) / ROPE_DIM)
    ret_freqs = RET_THETA ** (-np.linspace(0.0, 1.0, HEAD_DIM // 2, dtype=np.float32))
    tabs = _rotary_tables(s, ROPE_DIM // 2, rope_freqs) + _rotary_tables(s, HEAD_DIM // 2, ret_freqs)
    gw = HEAD_GROUP_WIDTH
    tab_spec = pl.BlockSpec((tm, LANES), lambda si, bi: (si, 0))

    def view(dil):
        return (pl.BlockSpec((1, tm // dil, dil * gw), lambda si, bi: (bi, si, 0)),
                jax.ShapeDtypeStruct((b, s // dil, dil * gw), BF16))

    views = [view(1)] * 3 + [view(dil) for dil in CLASS_DILATIONS for _ in range(3)] + [view(1)] * 4
    outs = pl.pallas_call(
        _inproj_kernel,
        grid=(s // tm, b),
        in_specs=[pl.BlockSpec((1, tm, d), lambda si, bi: (bi, si, 0)),
                  pl.BlockSpec((1, d), lambda si, bi: (0, 0)),
                  pl.BlockSpec(w_in_bf16.shape, lambda si, bi: (0, 0))] + [tab_spec] * 6,
        out_specs=[v[0] for v in views],
        out_shape=[v[1] for v in views],
        scratch_shapes=[pltpu.VMEM((gw // LANES, tm, LANES), F32)],
        compiler_params=_cparams(("arbitrary", "arbitrary")),
        name="inproj",
    )(x, gain.reshape(1, d), w_in_bf16, *tabs)
    n_attn = 3 * (1 + len(CLASS_DILATIONS))
    attn_qkv = {dil: outs[3 * c:3 * c + 3] for c, dil in enumerate((1,) + CLASS_DILATIONS)}
    return attn_qkv, outs[n_attn:]


def _attn_kernel(q_ref, k_ref, v_ref, o_ref, lse_ref, *, length, tq, reach):
    tqb = q_ref.shape[1]
    win = tq + 2 * reach
    qi = pl.program_id(2)
    lane = lax.broadcasted_iota(jnp.int32, (1, LANES), 1)
    diff = (lax.broadcasted_iota(jnp.int32, (tq, win), 1)
            - lax.broadcasted_iota(jnp.int32, (tq, win), 0))
    lane_t = lax.broadcasted_iota(jnp.int32, (tq, LANES), 1)

    def sub(t, carry):
        q0 = qi * tqb + t * tq
        ws = jnp.clip(q0 - reach, 0, length - win)
        ws = pl.multiple_of(ws, reach)
        off = q0 - ws
        valid = (diff >= off - reach) & (diff <= off + reach)
        rows = pl.ds(pl.multiple_of(t * tq, tq), tq)
        lse_tile = jnp.zeros((tq, LANES), F32)
        for g in range(HEAD_GROUP_WIDTH // LANES):
            cols = slice(g * LANES, (g + 1) * LANES)
            qg = q_ref[0, rows, cols]
            kw = k_ref[0, pl.ds(ws, win), cols]
            vw = v_ref[0, pl.ds(ws, win), cols]
            o_pair = jnp.zeros((tq, LANES), F32)
            for hh in range(LANES // HEAD_DIM):
                hm = (lane >= hh * HEAD_DIM) & (lane < (hh + 1) * HEAD_DIM)
                qh = jnp.where(hm, qg, jnp.zeros_like(qg))
                sc = lax.dot_general(qh, kw, (((1,), (1,)), ((), ())), preferred_element_type=F32)
                sc = jnp.where(valid, sc, NEG_INF)
                m = jnp.max(sc, axis=-1, keepdims=True)
                p = jnp.exp(sc - m)
                l = jnp.sum(p, axis=-1, keepdims=True)
                pv = jnp.dot(p.astype(BF16), vw, preferred_element_type=F32)
                o_pair = jnp.where(hm, pv / l, o_pair)
                head = g * (LANES // HEAD_DIM) + hh
                lse_tile = jnp.where(lane_t == head, m + jnp.log(l), lse_tile)
            o_ref[0, rows, cols] = o_pair.astype(BF16)
        lse_ref[0, rows, :] = lse_tile
        return carry

    lax.fori_loop(0, tqb // tq, sub, 0)


def _attn_branch(qc, kc, vc, dilation, reach, tq, tqb):
    b, length, dw = qc.shape
    w = dw // dilation
    tqb = min(tqb, length)
    q_spec = pl.BlockSpec((1, tqb, w), lambda bi, r, qi: (bi, qi, r))
    kv_spec = pl.BlockSpec((1, length, w), lambda bi, r, qi: (bi, 0, r))
    o, lse = pl.pallas_call(
        functools.partial(_attn_kernel, length=length, tq=tq, reach=reach),
        grid=(b, dilation, length // tqb),
        in_specs=[q_spec, kv_spec, kv_spec],
        out_specs=[q_spec, pl.BlockSpec((1, tqb, LANES), lambda bi, r, qi: (bi, qi, r))],
        out_shape=[jax.ShapeDtypeStruct((b, length, dilation * w), BF16),
                   jax.ShapeDtypeStruct((b, length, dilation * LANES), F32)],
        compiler_params=_cparams(("arbitrary", "arbitrary", "arbitrary")),
        name=f"attn_d{dilation}",
    )(qc, kc, vc)
    return o, lse


RET_TAB_QF, RET_TAB_QB, RET_TAB_KF, RET_TAB_KB = range(4)


def _retention_kernel(lgf_ref, lgb_ref, q_ref, k_ref, v_ref, g_ref, gain_ref, o_ref,
                      tab_ref, dec_ref, sb_ref, st_ref, *, chunk, unroll):
    c = chunk
    n = q_ref.shape[1] // c
    width = q_ref.shape[2]
    n_pairs = width // LANES
    heads_per_pair = LANES // HEAD_DIM
    n_heads = n_pairs * heads_per_pair
    head0 = pl.program_id(1) * n_heads
    lane_w = lax.broadcasted_iota(jnp.int32, (1, width), 1)
    lgf = [lgf_ref[head0 + hd] for hd in range(n_heads)]
    lgb = [lgb_ref[head0 + hd] for hd in range(n_heads)]
    lgf_lane = jnp.zeros((1, width), F32)
    lgb_lane = jnp.zeros((1, width), F32)
    for hd in range(n_heads):
        in_head = (lane_w >= hd * HEAD_DIM) & (lane_w < (hd + 1) * HEAD_DIM)
        lgf_lane = jnp.where(in_head, lgf[hd], lgf_lane)
        lgb_lane = jnp.where(in_head, lgb[hd], lgb_lane)
    idx = lax.broadcasted_iota(jnp.int32, (c, width), 0).astype(F32)
    tab_ref[RET_TAB_QF] = jnp.exp((idx + 1.0) * lgf_lane)
    tab_ref[RET_TAB_QB] = jnp.exp((c - idx) * lgb_lane)
    tab_ref[RET_TAB_KF] = jnp.exp((c - 1.0 - idx) * lgf_lane)
    tab_ref[RET_TAB_KB] = jnp.exp(idx * lgb_lane)
    sdf = jnp.exp(c * lgf_lane)
    sdb = jnp.exp(c * lgb_lane)
    dmat = (lax.broadcasted_iota(jnp.int32, (c, c), 0)
            - lax.broadcasted_iota(jnp.int32, (c, c), 1)).astype(F32)
    for hd in range(n_heads):
        dec_ref[hd] = jnp.where(dmat >= 0, jnp.exp(dmat * lgf[hd]), jnp.exp(-dmat * lgb[hd]))
    lane = lax.broadcasted_iota(jnp.int32, (1, LANES), 1)
    lane_hi = lane >= HEAD_DIM
    row_hi = lax.broadcasted_iota(jnp.int32, (LANES, LANES), 0) >= HEAD_DIM
    col_hi = lax.broadcasted_iota(jnp.int32, (LANES, LANES), 1) >= HEAD_DIM
    blockdiag = row_hi == col_hi

    def kv_state(kd, vv):
        kt = jnp.transpose(kd).astype(BF16)
        return jnp.where(blockdiag, jnp.dot(kt, vv, preferred_element_type=F32), 0.0)

    st_ref[...] = jnp.zeros_like(st_ref)
    sb_ref[n - 1] = jnp.zeros(sb_ref.shape[1:], sb_ref.dtype)

    def back(i, carry):
        nn = n - 1 - i
        rows = pl.ds(pl.multiple_of(nn * c, c), c)
        for p in range(n_pairs):
            cols = slice(p * LANES, (p + 1) * LANES)
            kd = k_ref[0, rows, cols].astype(F32) * tab_ref[RET_TAB_KB, :, cols]
            new = st_ref[p] * sdb[:, cols] + kv_state(kd, v_ref[0, rows, cols])
            st_ref[p] = new
            sb_ref[nn - 1, p] = new.astype(BF16)
        return carry

    lax.fori_loop(0, n - 1, back, 0, unroll=unroll)

    st_ref[...] = jnp.zeros_like(st_ref)

    def fwd(nn, carry):
        rows = pl.ds(pl.multiple_of(nn * c, c), c)
        for p in range(n_pairs):
            cols = slice(p * LANES, (p + 1) * LANES)
            qq = q_ref[0, rows, cols]
            kk = k_ref[0, rows, cols]
            vv = v_ref[0, rows, cols]
            qf = qq.astype(F32)
            sf = st_ref[p]
            qcat = jnp.concatenate([(qf * tab_ref[RET_TAB_QF, :, cols]).astype(BF16),
                                    (qf * tab_ref[RET_TAB_QB, :, cols]).astype(BF16)], axis=1)
            scat = jnp.concatenate([sf.astype(BF16), sb_ref[nn, p]], axis=0)
            o = jnp.dot(qcat, scat, preferred_element_type=F32)
            ps, vs = [], []
            for hh in range(heads_per_pair):
                hm = (lane >= hh * HEAD_DIM) & (lane < (hh + 1) * HEAD_DIM)
                qh = jnp.where(hm, qq, jnp.zeros_like(qq))
                sc = lax.dot_general(qh, kk, (((1,), (1,)), ((), ())), preferred_element_type=F32)
                ps.append((sc * dec_ref[p * heads_per_pair + hh]).astype(BF16))
                vs.append(jnp.where(hm, vv, jnp.zeros_like(vv)))
            o = o + jnp.dot(jnp.concatenate(ps, axis=1), jnp.concatenate(vs, axis=0),
                            preferred_element_type=F32)
            s_lo = jnp.sum(jnp.where(lane_hi, 0.0, o), axis=-1, keepdims=True)
            s_hi = jnp.sum(jnp.where(lane_hi, o, 0.0), axis=-1, keepdims=True)
            mu = jnp.where(lane_hi, s_hi, s_lo) * (1.0 / HEAD_DIM)
            dev = o - mu
            d2 = dev * dev
            v_lo = jnp.sum(jnp.where(lane_hi, 0.0, d2), axis=-1, keepdims=True)
            v_hi = jnp.sum(jnp.where(lane_hi, d2, 0.0), axis=-1, keepdims=True)
            var = jnp.where(lane_hi, v_hi, v_lo) * (1.0 / HEAD_DIM)
            out = dev * lax.rsqrt(var + NORM_EPS) * gain_ref[:, cols] * g_ref[0, rows, cols].astype(F32)
            o_ref[0, rows, cols] = out.astype(BF16)
            kd = kk.astype(F32) * tab_ref[RET_TAB_KF, :, cols]
            st_ref[p] = sf * sdf[:, cols] + kv_state(kd, vv)
        return carry

    lax.fori_loop(0, n, fwd, 0, unroll=unroll)


def _retention(qr, kr, vr, gate, lg_f, lg_b, out_gain, width=256, unroll=1):
    b, s, w = qr.shape
    n_pairs = width // LANES
    n_heads = width // HEAD_DIM
    spec = pl.BlockSpec((1, s, width), lambda bi, p, *_: (bi, 0, p))
    grid_spec = pltpu.PrefetchScalarGridSpec(
        num_scalar_prefetch=2,
        grid=(b, w // width),
        in_specs=[spec, spec, spec, spec, pl.BlockSpec((1, width), lambda bi, p, *_: (0, p))],
        out_specs=spec,
        scratch_shapes=[pltpu.VMEM((4, RET_CHUNK, width), F32),
                        pltpu.VMEM((n_heads, RET_CHUNK, RET_CHUNK), F32),
                        pltpu.VMEM((s // RET_CHUNK, n_pairs, LANES, LANES), BF16),
                        pltpu.VMEM((n_pairs, LANES, LANES), F32)],
    )
    return pl.pallas_call(
        functools.partial(_retention_kernel, chunk=RET_CHUNK, unroll=unroll),
        grid_spec=grid_spec,
        out_shape=jax.ShapeDtypeStruct((b, s, w), BF16),
        compiler_params=_cparams(("arbitrary", "arbitrary")),
        name="retention",
    )(lg_f, lg_b, qr, kr, vr, gate, out_gain.reshape(1, w))


def _split_bf16(t):
    hi = t.astype(BF16)
    lo = (t - hi.astype(F32)).astype(BF16)
    return hi, lo


def _pack_bf16_pairs(t):
    n = t.shape[1] // 2
    hi = pltpu.bitcast(t[:, :n].astype(BF16).astype(F32), jnp.uint32)
    lo = pltpu.bitcast(t[:, n:].astype(BF16).astype(F32), jnp.uint32)
    return hi | (lo >> 16)


def _unpack_bf16_pairs(u):
    hi = pltpu.bitcast(u & jnp.uint32(0xFFFF0000), F32)
    lo = pltpu.bitcast(u << 16, F32)
    return jnp.concatenate([hi, lo], axis=1)


def _outproj_kernel(x_ref, o1_ref, o2_ref, o3_ref, l1_ref, l2_ref, l3_ref, orr_ref, ga_ref, expand_ref,
                    wout_ref, gf_ref, wr_hi_ref, wr_lo_ref, br_ref, h_ref, hn_ref, logit_ref,
                    *nat_refs):
    tm = x_ref.shape[0]
    gw = HEAD_GROUP_WIDTH
    os, ls = [], []
    for (_, dil), o_ref, l_ref in zip(DILATED_BRANCHES, (o1_ref, o2_ref, o3_ref), (l1_ref, l2_ref, l3_ref)):
        if dil == 1:
            os.append(o_ref[...].astype(F32))
            ls.append(l_ref[...])
            continue
        c = CLASS_DILATIONS.index(dil)
        onat_ref, lnat_ref = nat_refs[2 * c], nat_refs[2 * c + 1]
        for r in range(dil):
            rows = pl.ds(r, tm // dil, stride=dil)
            for g in range(gw // LANES):
                col = r * gw + g * LANES
                onat_ref[g, rows, :] = o_ref[:, col:col + LANES].astype(F32)
            lnat_ref[rows, :] = l_ref[:, r * LANES:(r + 1) * LANES]
        os.append(jnp.concatenate([onat_ref[g] for g in range(gw // LANES)], axis=1))
        ls.append(lnat_ref[...])
    mx = jnp.maximum(jnp.maximum(ls[0], ls[1]), ls[2])
    es = [jnp.exp(l - mx) for l in ls]
    inv = 1.0 / (es[0] + es[1] + es[2])
    expand = expand_ref[...]
    oa = jnp.zeros((tm, gw), F32)
    for e, o in zip(es, os):
        w_hi, w_lo = _split_bf16(e * inv)
        wexp = (jnp.dot(w_hi, expand, preferred_element_type=F32)
                + jnp.dot(w_lo, expand, preferred_element_type=F32))
        oa = oa + wexp * o
    oa = oa * lax.rsqrt(jnp.mean(oa * oa, axis=-1, keepdims=True) + NORM_EPS) * ga_ref[...]
    mixed = jnp.concatenate([oa.astype(BF16), orr_ref[...]], axis=1)
    h = x_ref[...] + jnp.dot(mixed, wout_ref[...], preferred_element_type=F32)
    h_ref[...] = h
    hn = h * lax.rsqrt(jnp.mean(h * h, axis=-1, keepdims=True) + NORM_EPS) * gf_ref[...]
    hn_ref[...] = _pack_bf16_pairs(hn)
    hn_hi, hn_lo = _split_bf16(hn)
    wr_hi = wr_hi_ref[...]
    logit_ref[...] = (jnp.dot(hn_hi, wr_hi, preferred_element_type=F32)
                      + jnp.dot(hn_lo, wr_hi, preferred_element_type=F32)
                      + jnp.dot(hn_hi, wr_lo_ref[...], preferred_element_type=F32)
                      + br_ref[...])


def _outproj(x2, o_list, lse_list, orr, attn_gain, w_out_bf16, ffn_gain, w_router, b_router, tm):
    t, d = x2.shape
    w = HEAD_GROUP_WIDTH
    expand = np.zeros((LANES, w), np.float32)
    for hd in range(ATTN_HEADS):
        expand[hd, hd * HEAD_DIM:(hd + 1) * HEAD_DIM] = 1.0
    expand = jnp.asarray(expand, BF16)
    wr_hi = w_router.astype(BF16)
    wr_lo = (w_router - wr_hi.astype(F32)).astype(BF16)
    row = lambda width, dil=1: pl.BlockSpec((tm // dil, dil * width), lambda i: (i, 0))
    full = lambda a: pl.BlockSpec(a.shape, lambda i: (0,) * a.ndim)
    ga = attn_gain.reshape(1, w)
    gf = ffn_gain.reshape(1, d)
    dils = [dil for _, dil in DILATED_BRANCHES]
    o_flat = [o.reshape(t // dil, dil * w) for o, dil in zip(o_list, dils)]
    l_flat = [l.reshape(t // dil, dil * LANES) for l, dil in zip(lse_list, dils)]
    nat_scratch = []
    for _ in CLASS_DILATIONS:
        nat_scratch += [pltpu.VMEM((w // LANES, tm, LANES), F32), pltpu.VMEM((tm, LANES), F32)]
    return pl.pallas_call(
        _outproj_kernel,
        grid=(t // tm,),
        in_specs=[row(d)] + [row(w, dil) for dil in dils] + [row(LANES, dil) for dil in dils] + [row(w)]
                 + [full(ga), full(expand), full(w_out_bf16), full(gf), full(wr_hi), full(wr_lo), full(b_router)],
        out_specs=[row(d), row(d // 2), row(LANES)],
        out_shape=[jax.ShapeDtypeStruct((t, d), F32),
                   jax.ShapeDtypeStruct((t, d // 2), jnp.uint32),
                   jax.ShapeDtypeStruct((t, LANES), F32)],
        scratch_shapes=nat_scratch,
        compiler_params=_cparams(("arbitrary",)),
        name="outproj",
    )(x2, *o_flat, *l_flat, orr, ga, expand, w_out_bf16, gf, wr_hi, wr_lo, b_router)


ROUTE_E1, ROUTE_E2, ROUTE_G1, ROUTE_G2, ROUTE_R1, ROUTE_R2 = range(6)
EXPERT_LANE0 = MOE_GROUPS


def _route_kernel(logit_ref, tri_ref, route_ref, count_ref, run_ref):
    @pl.when(pl.program_id(0) == 0)
    def _():
        run_ref[...] = jnp.zeros_like(run_ref)

    lg = logit_ref[...]
    tm = lg.shape[0]
    lane = lax.broadcasted_iota(jnp.int32, lg.shape, 1)
    big = jnp.int32(1 << 20)

    def top(vals):
        m = jnp.max(vals, axis=-1, keepdims=True)
        i = jnp.min(jnp.where(vals == m, lane, big), axis=-1, keepdims=True)
        return m, i

    gl = jnp.where(lane < MOE_GROUPS, lg, -jnp.inf)
    gmax, gidx = top(gl)
    group_gate = 1.0 / jnp.sum(jnp.exp(gl - gmax), axis=-1, keepdims=True)
    lo = EXPERT_LANE0 + gidx * EXPERTS_PER_GROUP
    el = jnp.where((lane >= lo) & (lane < lo + EXPERTS_PER_GROUP), lg, -jnp.inf)
    t1, i1 = top(el)
    t2, i2 = top(jnp.where(lane == i1, -jnp.inf, el))
    e21 = jnp.exp(t2 - t1)
    g1 = group_gate / (1.0 + e21)
    g2 = group_gate * e21 / (1.0 + e21)
    e1 = i1 - EXPERT_LANE0
    e2 = i2 - EXPERT_LANE0
    oh1 = lane == e1
    oh2 = lane == e2
    cnt = oh1.astype(F32) + oh2.astype(F32)
    prefix = jnp.dot(tri_ref[...], cnt.astype(BF16), preferred_element_type=F32) + run_ref[0:1, :]
    r1 = jnp.sum(jnp.where(oh1, prefix, 0.0), axis=-1, keepdims=True)
    r2 = jnp.sum(jnp.where(oh2, prefix, 0.0), axis=-1, keepdims=True)
    new_run = run_ref[0:1, :] + jnp.sum(cnt, axis=0, keepdims=True)
    run_ref[...] = jnp.broadcast_to(new_run, run_ref.shape)
    count_ref[...] = jnp.broadcast_to(new_run, count_ref.shape)
    out = jnp.zeros(lg.shape, F32)
    for ln, val in ((ROUTE_E1, e1.astype(F32)), (ROUTE_E2, e2.astype(F32)), (ROUTE_G1, g1),
                    (ROUTE_G2, g2), (ROUTE_R1, r1), (ROUTE_R2, r2)):
        out = jnp.where(lane == ln, val, out)
    route_ref[...] = out


def _route(logits, tm):
    t = logits.shape[0]
    tri = jnp.asarray(np.tril(np.ones((tm, tm), np.float32), -1), BF16)
    return pl.pallas_call(
        _route_kernel,
        grid=(t // tm,),
        in_specs=[pl.BlockSpec((tm, LANES), lambda i: (i, 0)), pl.BlockSpec((tm, tm), lambda i: (0, 0))],
        out_specs=[pl.BlockSpec((tm, LANES), lambda i: (i, 0)), pl.BlockSpec((8, LANES), lambda i: (0, 0))],
        out_shape=[jax.ShapeDtypeStruct((t, LANES), F32), jax.ShapeDtypeStruct((8, LANES), F32)],
        scratch_shapes=[pltpu.VMEM((8, LANES), F32)],
        compiler_params=_cparams(("arbitrary",)),
        name="route",
    )(logits, tri)


TOP_K = 2
DMA_ISSUE_UNROLL = 8


SEG_ALIGN = 8
SORTED_TAIL = N_EXPERTS * SEG_ALIGN + MOE_BLOCK


def _dispatch_kernel(seg_end_ref, pos_ref, hn_ref, xs_hbm, zbuf, sem, zsem, *, n_rows):
    tm = hn_ref.shape[0]

    @pl.when(pl.program_id(0) == 0)
    def _():
        zbuf[...] = jnp.zeros_like(zbuf)
        tail = pltpu.make_async_copy(zbuf, xs_hbm.at[pl.ds(n_rows, SORTED_TAIL)], zsem)
        tail.start()
        tail.wait()

        def hole(e):
            end = seg_end_ref[e]
            start = pl.multiple_of(end - (end & (SEG_ALIGN - 1)), SEG_ALIGN)
            return pltpu.make_async_copy(zbuf.at[pl.ds(0, SEG_ALIGN)], xs_hbm.at[pl.ds(start, SEG_ALIGN)], zsem)

        for e in range(N_EXPERTS):
            @pl.when((seg_end_ref[e] & (SEG_ALIGN - 1)) != 0)
            def _():
                hole(e).start()
        for e in range(N_EXPERTS):
            @pl.when((seg_end_ref[e] & (SEG_ALIGN - 1)) != 0)
            def _():
                hole(e).wait()

    def body(j, c):
        for k in range(TOP_K):
            pltpu.make_async_copy(hn_ref.at[pl.ds(j, 1)], xs_hbm.at[pl.ds(pos_ref[0, 0, TOP_K * j + k], 1)],
                                  sem).start()
        return c
    lax.fori_loop(0, tm, body, 0, unroll=DMA_ISSUE_UNROLL)

    tile = pltpu.make_async_copy(hn_ref, xs_hbm.at[pl.ds(0, tm)], sem)
    for k in range(TOP_K):
        tile.wait()


def _dispatch(hn_packed, pos, seg_end, tm):
    t, dp = hn_packed.shape
    n_rows = TOP_K * t
    pos3 = pos.reshape(t // tm, 1, TOP_K * tm)
    grid_spec = pltpu.PrefetchScalarGridSpec(
        num_scalar_prefetch=1,
        grid=(t // tm,),
        in_specs=[pl.BlockSpec((1, 1, TOP_K * tm), lambda i, se: (i, 0, 0), memory_space=pltpu.SMEM),
                  pl.BlockSpec((tm, dp), lambda i, se: (i, 0))],
        out_specs=pl.BlockSpec(memory_space=pl.ANY),
        scratch_shapes=[pltpu.VMEM((SORTED_TAIL, dp), jnp.uint32),
                        pltpu.SemaphoreType.DMA,
                        pltpu.SemaphoreType.DMA],
    )
    return pl.pallas_call(
        functools.partial(_dispatch_kernel, n_rows=n_rows),
        grid_spec=grid_spec,
        out_shape=jax.ShapeDtypeStruct((n_rows + SORTED_TAIL, dp), jnp.uint32),
        compiler_params=_cparams(("arbitrary",)),
        name="dispatch",
    )(seg_end, pos3, hn_packed)


def _expert_kernel(bexp_ref, nreal_ref, row0_ref, xs_hbm, wg_ref, wu_ref, wd_ref, ys_hbm,
                   xbuf, ybuf, isem, osem, *, n_rows):
    del bexp_ref
    i = pl.program_id(0)
    nb = pl.num_programs(0)
    slot = i % 2
    nslot = 1 - slot
    n_cur = nreal_ref[i]
    prev = jnp.maximum(i - 1, 0)
    nxt = jnp.minimum(i + 1, nb - 1)

    def in_copy(blk, s):
        row0 = pl.multiple_of(row0_ref[blk], SEG_ALIGN)
        return pltpu.make_async_copy(xs_hbm.at[pl.ds(row0, MOE_BLOCK)], xbuf.at[s], isem.at[s])

    def out_copy(blk, s):
        row0 = pl.multiple_of(row0_ref[blk], SEG_ALIGN)
        return pltpu.make_async_copy(ybuf.at[s], ys_hbm.at[pl.ds(row0, MOE_BLOCK)], osem.at[s])

    @pl.when(i == 0)
    def _():
        ybuf[...] = jnp.zeros_like(ybuf)
        tails = [pltpu.make_async_copy(ybuf.at[s], ys_hbm.at[pl.ds(n_rows + s * MOE_BLOCK, MOE_BLOCK)], osem.at[s])
                 for s in range(SORTED_TAIL // MOE_BLOCK)]
        for tail in tails:
            tail.start()
        for tail in tails:
            tail.wait()

        @pl.when(n_cur > 0)
        def _():
            in_copy(i, slot).start()

    @pl.when((i + 1 < nb) & (nreal_ref[nxt] > 0))
    def _():
        in_copy(nxt, nslot).start()

    @pl.when(n_cur > 0)
    def _():
        in_copy(i, slot).wait()
        xb = _unpack_bf16_pairs(xbuf[slot]).astype(BF16)
        gate = jnp.dot(xb, wg_ref[0], preferred_element_type=F32)
        up = jnp.dot(xb, wu_ref[0], preferred_element_type=F32)
        hid = (gate * jax.nn.sigmoid(gate) * up).astype(BF16)
        ybuf[slot] = _pack_bf16_pairs(jnp.dot(hid, wd_ref[0], preferred_element_type=F32))

    @pl.when((i >= 1) & (nreal_ref[prev] > 0))
    def _():
        out_copy(prev, nslot).wait()

    @pl.when(n_cur > 0)
    def _():
        out_copy(i, slot).start()

        @pl.when(i == nb - 1)
        def _():
            out_copy(i, slot).wait()


def _experts(xs, block_expert, block_nreal, block_row0, wg, wu, wd):
    n_blocks = block_expert.shape[0]
    assert n_blocks >= 2 and SORTED_TAIL == 2 * MOE_BLOCK
    n_rows_pad, dp = xs.shape
    _, d, ff = wg.shape
    grid_spec = pltpu.PrefetchScalarGridSpec(
        num_scalar_prefetch=3,
        grid=(n_blocks,),
        in_specs=[pl.BlockSpec(memory_space=pl.ANY),
                  pl.BlockSpec((1, d, ff), lambda i, be, nr, r0: (be[i], 0, 0)),
                  pl.BlockSpec((1, d, ff), lambda i, be, nr, r0: (be[i], 0, 0)),
                  pl.BlockSpec((1, ff, d), lambda i, be, nr, r0: (be[i], 0, 0))],
        out_specs=pl.BlockSpec(memory_space=pl.ANY),
        scratch_shapes=[pltpu.VMEM((2, MOE_BLOCK, dp), jnp.uint32),
                        pltpu.VMEM((2, MOE_BLOCK, dp), jnp.uint32),
                        pltpu.SemaphoreType.DMA((2,)),
                        pltpu.SemaphoreType.DMA((2,))],
    )
    return pl.pallas_call(
        functools.partial(_expert_kernel, n_rows=n_rows_pad - SORTED_TAIL),
        grid_spec=grid_spec,
        out_shape=jax.ShapeDtypeStruct((n_rows_pad, dp), jnp.uint32),
        compiler_params=_cparams(("arbitrary",)),
        name="experts",
    )(block_expert, block_nreal, block_row0, xs, wg, wu, wd)


def _final_kernel(pos_ref, pos_next_ref, h_ref, route_ref, gain_ref, ys_hbm, out_ref, ybuf, sem):
    i = pl.program_id(0)
    nt = pl.num_programs(0)
    slot = i % 2
    tm = h_ref.shape[0]

    def gather_start(idx_ref, s):
        def body(j, c):
            for k in range(TOP_K):
                pltpu.make_async_copy(ys_hbm.at[pl.ds(idx_ref[0, 0, TOP_K * j + k], 1)],
                                      ybuf.at[s, k, pl.ds(j, 1)], sem.at[s]).start()
            return c
        lax.fori_loop(0, tm, body, 0, unroll=DMA_ISSUE_UNROLL)

    @pl.when(i == 0)
    def _():
        gather_start(pos_ref, slot)

    @pl.when(i + 1 < nt)
    def _():
        gather_start(pos_next_ref, 1 - slot)

    for k in range(TOP_K):
        pltpu.make_async_copy(ys_hbm.at[pl.ds(0, tm)], ybuf.at[slot, k], sem.at[slot]).wait()

    r = route_ref[...]
    y = h_ref[...]
    for k, gate_lane in enumerate((ROUTE_G1, ROUTE_G2)):
        y = y + r[:, gate_lane:gate_lane + 1] * _unpack_bf16_pairs(ybuf[slot, k])
    out_ref[...] = y * lax.rsqrt(jnp.mean(y * y, axis=-1, keepdims=True) + NORM_EPS) * gain_ref[...]


def _final(h, ys, pos, route, gain, tm):
    t, d = h.shape
    nt = t // tm
    pos3 = pos.reshape(nt, 1, TOP_K * tm)
    smem_blk = lambda imap: pl.BlockSpec((1, 1, TOP_K * tm), imap, memory_space=pltpu.SMEM)
    return pl.pallas_call(
        _final_kernel,
        grid=(nt,),
        in_specs=[smem_blk(lambda i: (i, 0, 0)),
                  smem_blk(lambda i: (jnp.minimum(i + 1, nt - 1), 0, 0)),
                  pl.BlockSpec((tm, d), lambda i: (i, 0)),
                  pl.BlockSpec((tm, LANES), lambda i: (i, 0)),
                  pl.BlockSpec((1, d), lambda i: (0, 0)),
                  pl.BlockSpec(memory_space=pl.ANY)],
        out_specs=pl.BlockSpec((tm, d), lambda i: (i, 0)),
        out_shape=jax.ShapeDtypeStruct((t, d), F32),
        scratch_shapes=[pltpu.VMEM((2, TOP_K, tm, d // 2), jnp.uint32),
                        pltpu.SemaphoreType.DMA((2,))],
        compiler_params=_cparams(("arbitrary",)),
        name="final",
    )(pos3, pos3, h, route, gain.reshape(1, d), ys)


def _layer(h3, mix_gain, w_in, attn_gain, decay_f, decay_b, ret_gain, w_out, ffn_gain,
           w_rg, b_rg, w_re, b_re, w_eg, w_eu, w_ed, final_gain):
    b, s, d = h3.shape
    t = b * s
    tm = 512
    attn_qkv, (qr, kr, vr, gr) = _inproj(h3, mix_gain, w_in.astype(BF16), tm)

    o_list, lse_list = [], []
    for window, dilation in DILATED_BRANCHES:
        reach = (window // 2) // dilation
        o, lse = _attn_branch(*attn_qkv[dilation], dilation, reach, tq=128, tqb=512)
        o_list.append(o)
        lse_list.append(lse)

    lg_f = jnp.log1p(-jnp.exp2(decay_f.astype(F32)))
    lg_b = jnp.log1p(-jnp.exp2(decay_b.astype(F32)))
    orr = _retention(qr, kr, vr, gr, lg_f, lg_b, ret_gain).reshape(t, HEAD_GROUP_WIDTH)

    n_route = MOE_GROUPS + N_EXPERTS
    w_router = jnp.zeros((d, LANES), F32).at[:, :n_route].set(jnp.concatenate([w_rg, w_re], axis=1).astype(F32))
    b_router = jnp.zeros((1, LANES), F32).at[0, :n_route].set(jnp.concatenate([b_rg, b_re]).astype(F32))
    h, hn_packed, logits = _outproj(h3.reshape(t, d), o_list, lse_list, orr, attn_gain, w_out.astype(BF16),
                                    ffn_gain, w_router, b_router, tm)

    route, counts8 = _route(logits, tm)

    n_blocks = -(-TOP_K * t // MOE_BLOCK) + N_EXPERTS
    counts = counts8[0, :N_EXPERTS].astype(jnp.int32)
    aligned = ((counts + SEG_ALIGN - 1) // SEG_ALIGN) * SEG_ALIGN
    seg_start = jnp.cumsum(aligned) - aligned
    expert_iota = jnp.arange(N_EXPERTS, dtype=jnp.int32)
    e12 = route[:, ROUTE_E1:ROUTE_E2 + 1].astype(jnp.int32)
    r12 = route[:, ROUTE_R1:ROUTE_R2 + 1].astype(jnp.int32)
    pos = r12 + jnp.sum(jnp.where(e12[..., None] == expert_iota, seg_start, 0), axis=-1)
    nblk = (counts + MOE_BLOCK - 1) // MOE_BLOCK
    blk_end = jnp.cumsum(nblk)
    blk_start = blk_end - nblk
    blk = jnp.arange(n_blocks, dtype=jnp.int32)[:, None]
    owner = (blk >= blk_start) & (blk < blk_end)
    local = (blk - blk_start) * MOE_BLOCK
    block_row0 = jnp.sum(jnp.where(owner, seg_start + local, 0), axis=-1).astype(jnp.int32)
    block_nreal = jnp.sum(jnp.where(owner, jnp.clip(counts - local, 0, MOE_BLOCK), 0), axis=-1).astype(jnp.int32)
    block_expert = jnp.minimum(jnp.sum((blk >= blk_end).astype(jnp.int32), axis=-1), N_EXPERTS - 1)

    xs = _dispatch(hn_packed, pos, seg_start + counts, tm)
    ys = _experts(xs, block_expert, block_nreal, block_row0, w_eg.astype(BF16), w_eu.astype(BF16),
                  w_ed.astype(BF16))
    out = _final(h, ys, pos, route, final_gain, tm)
    return out.reshape(b, s, d)


def kernel(x, mix_norm_gain, w_in, attn_out_gain, ret_decay_fwd, ret_decay_bwd, ret_out_gain, w_out,
           ffn_norm_gain, w_route_group, b_route_group, w_route_expert, b_route_expert,
           w_expert_gate, w_expert_up, w_expert_down, final_norm_gain):
    depth = mix_norm_gain.shape[0]
    assert depth == 1, "the final rmsnorm is fused into the single layer's combine kernel"
    l = 0
    return _layer(x, mix_norm_gain[l], w_in[l], attn_out_gain[l], ret_decay_fwd[l], ret_decay_bwd[l],
                  ret_out_gain[l], w_out[l], ffn_norm_gain[l], w_route_group[l], b_route_group[l],
                  w_route_expert[l], b_route_expert[l], w_expert_gate[l], w_expert_up[l], w_expert_down[l],
                  final_norm_gain)
```

```python
import functools

import numpy as np
import jax
import jax.numpy as jnp
from jax import lax
from jax.experimental import pallas as pl
from jax.experimental.pallas import tpu as pltpu

F32 = jnp.float32
BF16 = jnp.bfloat16

ATTN_HEADS = 8
HEAD_DIM = 64
RET_HEADS = 8
HEAD_GROUP_WIDTH = 512
N_PROJ_GROUPS = 7
DILATED_BRANCHES = ((128, 1), (512, 4), (2048, 16))
ROPE_THETA = 500000.0
ROPE_DIM = HEAD_DIM // 4
RET_THETA = 10000.0
RET_CHUNK = 128
MOE_GROUPS = 4
EXPERTS_PER_GROUP = 8
N_EXPERTS = MOE_GROUPS * EXPERTS_PER_GROUP
MOE_BLOCK = 256
NORM_EPS = 1e-6
NEG_INF = -1e30

LANES = 128
VMEM_LIMIT = 56 * 1024 * 1024


def _cparams(sem):
    return pltpu.CompilerParams(dimension_semantics=sem, vmem_limit_bytes=VMEM_LIMIT)


def _rotary_tables(seq, half, freqs):
    pos = np.arange(seq, dtype=np.float64)[:, None]
    ang = pos * freqs[None, :].astype(np.float64)
    cos, sin = np.cos(ang), np.sin(ang)
    c = np.ones((seq, HEAD_DIM)); sp = np.zeros((seq, HEAD_DIM)); sm = np.zeros((seq, HEAD_DIM))
    c[:, :half] = cos; c[:, half:2 * half] = cos
    sp[:, half:2 * half] = sin
    sm[:, :half] = -sin
    rep = LANES // HEAD_DIM
    return tuple(jnp.asarray(np.tile(t, (1, rep)), F32) for t in (c, sp, sm))


def _rotate(t, c, sp, sm, half):
    outs = []
    for g in range(t.shape[1] // LANES):
        tg = t[:, g * LANES:(g + 1) * LANES]
        outs.append(tg * c + pltpu.roll(tg, half, 1) * sp + pltpu.roll(tg, LANES - half, 1) * sm)
    return jnp.concatenate(outs, axis=1)


CLASS_DILATIONS = tuple(d for _, d in DILATED_BRANCHES if d > 1)


def _inproj_kernel(x_ref, gain_ref, w_ref, ca_ref, spa_ref, sma_ref, cr_ref, spr_ref, smr_ref, *rest):
    n_cls = len(CLASS_DILATIONS)
    nat_refs = rest[0:3]
    cls_refs = [rest[3 + 3 * c:6 + 3 * c] for c in range(n_cls)]
    qr_ref, kr_ref, vr_ref, gr_ref = rest[3 + 3 * n_cls:7 + 3 * n_cls]
    stage_ref = rest[7 + 3 * n_cls]
    x = x_ref[0]
    tm = x.shape[0]
    ms = jnp.mean(x * x, axis=-1, keepdims=True)
    xn = (x * lax.rsqrt(ms + NORM_EPS) * gain_ref[...]).astype(BF16)
    gw = HEAD_GROUP_WIDTH

    def proj(c):
        return jnp.dot(xn, w_ref[:, c * gw:(c + 1) * gw], preferred_element_type=F32)

    a_tabs = (ca_ref[...], spa_ref[...], sma_ref[...])
    r_tabs = (cr_ref[...], spr_ref[...], smr_ref[...])
    attn_vals = ((_rotate(proj(0), *a_tabs, ROPE_DIM // 2) * (HEAD_DIM ** -0.5)),
                 _rotate(proj(1), *a_tabs, ROPE_DIM // 2),
                 proj(2))
    for j, val in enumerate(attn_vals):
        nat_refs[j][0] = val.astype(BF16)
        for g in range(gw // LANES):
            stage_ref[g] = val[:, g * LANES:(g + 1) * LANES]
        for c, d in enumerate(CLASS_DILATIONS):
            for r in range(d):
                for g in range(gw // LANES):
                    col = r * gw + g * LANES
                    cls_refs[c][j][0, :, col:col + LANES] = (
                        stage_ref[g, pl.ds(r, tm // d, stride=d), :].astype(BF16))
    qr_ref[0] = _rotate(proj(3), *r_tabs, HEAD_DIM // 2).astype(BF16)
    kr_ref[0] = (_rotate(proj(4), *r_tabs, HEAD_DIM // 2) * (HEAD_DIM ** -0.5)).astype(BF16)
    vr_ref[0] = proj(5).astype(BF16)
    g = proj(6)
    gr_ref[0] = (g * jax.nn.sigmoid(g)).astype(BF16)


def _inproj(x, gain, w_in_bf16, tm):
    b, s, d = x.shape
    rope_freqs = ROPE_THETA ** (-np.arange(0, ROPE_DIM, 2, dtype=np.float32) / ROPE_DIM)
    ret_freqs = RET_THETA ** (-np.linspace(0.0, 1.0, HEAD_DIM // 2, dtype=np.float32))
    tabs = _rotary_tables(s, ROPE_DIM // 2, rope_freqs) + _rotary_tables(s, HEAD_DIM // 2, ret_freqs)
    gw = HEAD_GROUP_WIDTH
    tab_spec = pl.BlockSpec((tm, LANES), lambda si, bi: (si, 0))

    def view(dil):
        return (pl.BlockSpec((1, tm // dil, dil * gw), lambda si, bi: (bi, si, 0)),
                jax.ShapeDtypeStruct((b, s // dil, dil * gw), BF16))

    views = [view(1)] * 3 + [view(dil) for dil in CLASS_DILATIONS for _ in range(3)] + [view(1)] * 4
    outs = pl.pallas_call(
        _inproj_kernel,
        grid=(s // tm, b),
        in_specs=[pl.BlockSpec((1, tm, d), lambda si, bi: (bi, si, 0)),
                  pl.BlockSpec((1, d), lambda si, bi: (0, 0)),
                  pl.BlockSpec(w_in_bf16.shape, lambda si, bi: (0, 0))] + [tab_spec] * 6,
        out_specs=[v[0] for v in views],
        out_shape=[v[1] for v in views],
        scratch_shapes=[pltpu.VMEM((gw // LANES, tm, LANES), F32)],
        compiler_params=_cparams(("arbitrary", "arbitrary")),
        name="inproj",
    )(x, gain.reshape(1, d), w_in_bf16, *tabs)
    n_attn = 3 * (1 + len(CLASS_DILATIONS))
    attn_qkv = {dil: outs[3 * c:3 * c + 3] for c, dil in enumerate((1,) + CLASS_DILATIONS)}
    return attn_qkv, outs[n_attn:]


def _attn_kernel(q_ref, k_ref, v_ref, o_ref, lse_ref, *, length, tq, reach):
    tqb = q_ref.shape[1]
    win = tq + 2 * reach
    qi = pl.program_id(2)
    lane = lax.broadcasted_iota(jnp.int32, (1, LANES), 1)
    diff = (lax.broadcasted_iota(jnp.int32, (tq, win), 1)
            - lax.broadcasted_iota(jnp.int32, (tq, win), 0))
    lane_t = lax.broadcasted_iota(jnp.int32, (tq, LANES), 1)

    def sub(t, carry):
        q0 = qi * tqb + t * tq
        ws = jnp.clip(q0 - reach, 0, length - win)
        ws = pl.multiple_of(ws, reach)
        off = q0 - ws
        valid = (diff >= off - reach) & (diff <= off + reach)
        rows = pl.ds(pl.multiple_of(t * tq, tq), tq)
        lse_tile = jnp.zeros((tq, LANES), F32)
        for g in range(HEAD_GROUP_WIDTH // LANES):
            cols = slice(g * LANES, (g + 1) * LANES)
            qg = q_ref[0, rows, cols]
            kw = k_ref[0, pl.ds(ws, win), cols]
            vw = v_ref[0, pl.ds(ws, win), cols]
            o_pair = jnp.zeros((tq, LANES), F32)
            for hh in range(LANES // HEAD_DIM):
                hm = (lane >= hh * HEAD_DIM) & (lane < (hh + 1) * HEAD_DIM)
                qh = jnp.where(hm, qg, jnp.zeros_like(qg))
                sc = lax.dot_general(qh, kw, (((1,), (1,)), ((), ())), preferred_element_type=F32)
                sc = jnp.where(valid, sc, NEG_INF)
                m = jnp.max(sc, axis=-1, keepdims=True)
                p = jnp.exp(sc - m)
                l = jnp.sum(p, axis=-1, keepdims=True)
                pv = jnp.dot(p.astype(BF16), vw, preferred_element_type=F32)
                o_pair = jnp.where(hm, pv / l, o_pair)
                head = g * (LANES // HEAD_DIM) + hh
                lse_tile = jnp.where(lane_t == head, m + jnp.log(l), lse_tile)
            o_ref[0, rows, cols] = o_pair.astype(BF16)
        lse_ref[0, rows, :] = lse_tile
        return carry

    lax.fori_loop(0, tqb // tq, sub, 0)


def _attn_branch(qc, kc, vc, dilation, reach, tq, tqb):
    b, length, dw = qc.shape
    w = dw // dilation
    tqb = min(tqb, length)
    q_spec = pl.BlockSpec((1, tqb, w), lambda bi, r, qi: (bi, qi, r))
    kv_spec = pl.BlockSpec((1, length, w), lambda bi, r, qi: (bi, 0, r))
    o, lse = pl.pallas_call(
        functools.partial(_attn_kernel, length=length, tq=tq, reach=reach),
        grid=(b, dilation, length // tqb),
        in_specs=[q_spec, kv_spec, kv_spec],
        out_specs=[q_spec, pl.BlockSpec((1, tqb, LANES), lambda bi, r, qi: (bi, qi, r))],
        out_shape=[jax.ShapeDtypeStruct((b, length, dilation * w), BF16),
                   jax.ShapeDtypeStruct((b, length, dilation * LANES), F32)],
        compiler_params=_cparams(("arbitrary", "arbitrary", "arbitrary")),
        name=f"attn_d{dilation}",
    )(qc, kc, vc)
    return o, lse


RET_TAB_QF, RET_TAB_QB, RET_TAB_KF, RET_TAB_KB = range(4)


def _retention_kernel(lgf_ref, lgb_ref, q_ref, k_ref, v_ref, g_ref, gain_ref, o_ref,
                      tab_ref, dec_ref, sb_ref, st_ref, *, chunk, unroll):
    c = chunk
    n = q_ref.shape[1] // c
    width = q_ref.shape[2]
    n_pairs = width // LANES
    heads_per_pair = LANES // HEAD_DIM
    n_heads = n_pairs * heads_per_pair
    head0 = pl.program_id(1) * n_heads
    lane_w = lax.broadcasted_iota(jnp.int32, (1, width), 1)
    lgf = [lgf_ref[head0 + hd] for hd in range(n_heads)]
    lgb = [lgb_ref[head0 + hd] for hd in range(n_heads)]
    lgf_lane = jnp.zeros((1, width), F32)
    lgb_lane = jnp.zeros((1, width), F32)
    for hd in range(n_heads):
        in_head = (lane_w >= hd * HEAD_DIM) & (lane_w < (hd + 1) * HEAD_DIM)
        lgf_lane = jnp.where(in_head, lgf[hd], lgf_lane)
        lgb_lane = jnp.where(in_head, lgb[hd], lgb_lane)
    idx = lax.broadcasted_iota(jnp.int32, (c, width), 0).astype(F32)
    tab_ref[RET_TAB_QF] = jnp.exp((idx + 1.0) * lgf_lane)
    tab_ref[RET_TAB_QB] = jnp.exp((c - idx) * lgb_lane)
    tab_ref[RET_TAB_KF] = jnp.exp((c - 1.0 - idx) * lgf_lane)
    tab_ref[RET_TAB_KB] = jnp.exp(idx * lgb_lane)
    sdf = jnp.exp(c * lgf_lane)
    sdb = jnp.exp(c * lgb_lane)
    dmat = (lax.broadcasted_iota(jnp.int32, (c, c), 0)
            - lax.broadcasted_iota(jnp.int32, (c, c), 1)).astype(F32)
    for hd in range(n_heads):
        dec_ref[hd] = jnp.where(dmat >= 0, jnp.exp(dmat * lgf[hd]), jnp.exp(-dmat * lgb[hd]))
    lane = lax.broadcasted_iota(jnp.int32, (1, LANES), 1)
    lane_hi = lane >= HEAD_DIM
    row_hi = lax.broadcasted_iota(jnp.int32, (LANES, LANES), 0) >= HEAD_DIM
    col_hi = lax.broadcasted_iota(jnp.int32, (LANES, LANES), 1) >= HEAD_DIM
    blockdiag = row_hi == col_hi

    def kv_state(kd, vv):
        kt = jnp.transpose(kd).astype(BF16)
        return jnp.where(blockdiag, jnp.dot(kt, vv, preferred_element_type=F32), 0.0)

    st_ref[...] = jnp.zeros_like(st_ref)
    sb_ref[n - 1] = jnp.zeros(sb_ref.shape[1:], sb_ref.dtype)

    def back(i, carry):
        nn = n - 1 - i
        rows = pl.ds(pl.multiple_of(nn * c, c), c)
        for p in range(n_pairs):
            cols = slice(p * LANES, (p + 1) * LANES)
            kd = k_ref[0, rows, cols].astype(F32) * tab_ref[RET_TAB_KB, :, cols]
            new = st_ref[p] * sdb[:, cols] + kv_state(kd, v_ref[0, rows, cols])
            st_ref[p] = new
            sb_ref[nn - 1, p] = new.astype(BF16)
        return carry

    lax.fori_loop(0, n - 1, back, 0, unroll=unroll)

    st_ref[...] = jnp.zeros_like(st_ref)

    def fwd(nn, carry):
        rows = pl.ds(pl.multiple_of(nn * c, c), c)
        for p in range(n_pairs):
            cols = slice(p * LANES, (p + 1) * LANES)
            qq = q_ref[0, rows, cols]
            kk = k_ref[0, rows, cols]
            vv = v_ref[0, rows, cols]
            qf = qq.astype(F32)
            sf = st_ref[p]
            qcat = jnp.concatenate([(qf * tab_ref[RET_TAB_QF, :, cols]).astype(BF16),
                                    (qf * tab_ref[RET_TAB_QB, :, cols]).astype(BF16)], axis=1)
            scat = jnp.concatenate([sf.astype(BF16), sb_ref[nn, p]], axis=0)
            o = jnp.dot(qcat, scat, preferred_element_type=F32)
            ps, vs = [], []
            for hh in range(heads_per_pair):
                hm = (lane >= hh * HEAD_DIM) & (lane < (hh + 1) * HEAD_DIM)
                qh = jnp.where(hm, qq, jnp.zeros_like(qq))
                sc = lax.dot_general(qh, kk, (((1,), (1,)), ((), ())), preferred_element_type=F32)
                ps.append((sc * dec_ref[p * heads_per_pair + hh]).astype(BF16))
                vs.append(jnp.where(hm, vv, jnp.zeros_like(vv)))
            o = o + jnp.dot(jnp.concatenate(ps, axis=1), jnp.concatenate(vs, axis=0),
                            preferred_element_type=F32)
            s_lo = jnp.sum(jnp.where(lane_hi, 0.0, o), axis=-1, keepdims=True)
            s_hi = jnp.sum(jnp.where(lane_hi, o, 0.0), axis=-1, keepdims=True)
            mu = jnp.where(lane_hi, s_hi, s_lo) * (1.0 / HEAD_DIM)
            dev = o - mu
            d2 = dev * dev
            v_lo = jnp.sum(jnp.where(lane_hi, 0.0, d2), axis=-1, keepdims=True)
            v_hi = jnp.sum(jnp.where(lane_hi, d2, 0.0), axis=-1, keepdims=True)
            var = jnp.where(lane_hi, v_hi, v_lo) * (1.0 / HEAD_DIM)
            out = dev * lax.rsqrt(var + NORM_EPS) * gain_ref[:, cols] * g_ref[0, rows, cols].astype(F32)
            o_ref[0, rows, cols] = out.astype(BF16)
            kd = kk.astype(F32) * tab_ref[RET_TAB_KF, :, cols]
            st_ref[p] = sf * sdf[:, cols] + kv_state(kd, vv)
        return carry

    lax.fori_loop(0, n, fwd, 0, unroll=unroll)


def _retention(qr, kr, vr, gate, lg_f, lg_b, out_gain, width=512, unroll=1):
    b, s, w = qr.shape
    n_pairs = width // LANES
    n_heads = width // HEAD_DIM
    spec = pl.BlockSpec((1, s, width), lambda bi, p, *_: (bi, 0, p))
    grid_spec = pltpu.PrefetchScalarGridSpec(
        num_scalar_prefetch=2,
        grid=(b, w // width),
        in_specs=[spec, spec, spec, spec, pl.BlockSpec((1, width), lambda bi, p, *_: (0, p))],
        out_specs=spec,
        scratch_shapes=[pltpu.VMEM((4, RET_CHUNK, width), F32),
                        pltpu.VMEM((n_heads, RET_CHUNK, RET_CHUNK), F32),
                        pltpu.VMEM((s // RET_CHUNK, n_pairs, LANES, LANES), BF16),
                        pltpu.VMEM((n_pairs, LANES, LANES), F32)],
    )
    return pl.pallas_call(
        functools.partial(_retention_kernel, chunk=RET_CHUNK, unroll=unroll),
        grid_spec=grid_spec,
        out_shape=jax.ShapeDtypeStruct((b, s, w), BF16),
        compiler_params=_cparams(("arbitrary", "arbitrary")),
        name="retention",
    )(lg_f, lg_b, qr, kr, vr, gate, out_gain.reshape(1, w))


def _split_bf16(t):
    hi = t.astype(BF16)
    lo = (t - hi.astype(F32)).astype(BF16)
    return hi, lo


def _pack_bf16_pairs(t):
    n = t.shape[1] // 2
    hi = pltpu.bitcast(t[:, :n].astype(BF16).astype(F32), jnp.uint32)
    lo = pltpu.bitcast(t[:, n:].astype(BF16).astype(F32), jnp.uint32)
    return hi | (lo >> 16)


def _unpack_bf16_pairs(u):
    hi = pltpu.bitcast(u & jnp.uint32(0xFFFF0000), F32)
    lo = pltpu.bitcast(u << 16, F32)
    return jnp.concatenate([hi, lo], axis=1)


def _outproj_kernel(x_ref, o1_ref, o2_ref, o3_ref, l1_ref, l2_ref, l3_ref, orr_ref, ga_ref, expand_ref,
                    wout_ref, gf_ref, wr_hi_ref, wr_lo_ref, br_ref, h_ref, hn_ref, logit_ref,
                    *nat_refs):
    tm = x_ref.shape[0]
    gw = HEAD_GROUP_WIDTH
    os, ls = [], []
    for (_, dil), o_ref, l_ref in zip(DILATED_BRANCHES, (o1_ref, o2_ref, o3_ref), (l1_ref, l2_ref, l3_ref)):
        if dil == 1:
            os.append(o_ref[...].astype(F32))
            ls.append(l_ref[...])
            continue
        c = CLASS_DILATIONS.index(dil)
        onat_ref, lnat_ref = nat_refs[2 * c], nat_refs[2 * c + 1]
        for r in range(dil):
            rows = pl.ds(r, tm // dil, stride=dil)
            for g in range(gw // LANES):
                col = r * gw + g * LANES
                onat_ref[g, rows, :] = o_ref[:, col:col + LANES].astype(F32)
            lnat_ref[rows, :] = l_ref[:, r * LANES:(r + 1) * LANES]
        os.append(jnp.concatenate([onat_ref[g] for g in range(gw // LANES)], axis=1))
        ls.append(lnat_ref[...])
    mx = jnp.maximum(jnp.maximum(ls[0], ls[1]), ls[2])
    es = [jnp.exp(l - mx) for l in ls]
    inv = 1.0 / (es[0] + es[1] + es[2])
    expand = expand_ref[...]
    oa = jnp.zeros((tm, gw), F32)
    for e, o in zip(es, os):
        w_hi, w_lo = _split_bf16(e * inv)
        wexp = (jnp.dot(w_hi, expand, preferred_element_type=F32)
                + jnp.dot(w_lo, expand, preferred_element_type=F32))
        oa = oa + wexp * o
    oa = oa * lax.rsqrt(jnp.mean(oa * oa, axis=-1, keepdims=True) + NORM_EPS) * ga_ref[...]
    mixed = jnp.concatenate([oa.astype(BF16), orr_ref[...]], axis=1)
    h = x_ref[...] + jnp.dot(mixed, wout_ref[...], preferred_element_type=F32)
    h_ref[...] = h
    hn = h * lax.rsqrt(jnp.mean(h * h, axis=-1, keepdims=True) + NORM_EPS) * gf_ref[...]
    hn_ref[...] = _pack_bf16_pairs(hn)
    hn_hi, hn_lo = _split_bf16(hn)
    wr_hi = wr_hi_ref[...]
    logit_ref[...] = (jnp.dot(hn_hi, wr_hi, preferred_element_type=F32)
                      + jnp.dot(hn_lo, wr_hi, preferred_element_type=F32)
                      + jnp.dot(hn_hi, wr_lo_ref[...], preferred_element_type=F32)
                      + br_ref[...])


def _outproj(x2, o_list, lse_list, orr, attn_gain, w_out_bf16, ffn_gain, w_router, b_router, tm):
    t, d = x2.shape
    w = HEAD_GROUP_WIDTH
    expand = np.zeros((LANES, w), np.float32)
    for hd in range(ATTN_HEADS):
        expand[hd, hd * HEAD_DIM:(hd + 1) * HEAD_DIM] = 1.0
    expand = jnp.asarray(expand, BF16)
    wr_hi = w_router.astype(BF16)
    wr_lo = (w_router - wr_hi.astype(F32)).astype(BF16)
    row = lambda width, dil=1: pl.BlockSpec((tm // dil, dil * width), lambda i: (i, 0))
    full = lambda a: pl.BlockSpec(a.shape, lambda i: (0,) * a.ndim)
    ga = attn_gain.reshape(1, w)
    gf = ffn_gain.reshape(1, d)
    dils = [dil for _, dil in DILATED_BRANCHES]
    o_flat = [o.reshape(t // dil, dil * w) for o, dil in zip(o_list, dils)]
    l_flat = [l.reshape(t // dil, dil * LANES) for l, dil in zip(lse_list, dils)]
    nat_scratch = []
    for _ in CLASS_DILATIONS:
        nat_scratch += [pltpu.VMEM((w // LANES, tm, LANES), F32), pltpu.VMEM((tm, LANES), F32)]
    return pl.pallas_call(
        _outproj_kernel,
        grid=(t // tm,),
        in_specs=[row(d)] + [row(w, dil) for dil in dils] + [row(LANES, dil) for dil in dils] + [row(w)]
                 + [full(ga), full(expand), full(w_out_bf16), full(gf), full(wr_hi), full(wr_lo), full(b_router)],
        out_specs=[row(d), row(d // 2), row(LANES)],
        out_shape=[jax.ShapeDtypeStruct((t, d), F32),
                   jax.ShapeDtypeStruct((t, d // 2), jnp.uint32),
                   jax.ShapeDtypeStruct((t, LANES), F32)],
        scratch_shapes=nat_scratch,
        compiler_params=_cparams(("arbitrary",)),
        name="outproj",
    )(x2, *o_flat, *l_flat, orr, ga, expand, w_out_bf16, gf, wr_hi, wr_lo, b_router)


ROUTE_E1, ROUTE_E2, ROUTE_G1, ROUTE_G2, ROUTE_R1, ROUTE_R2 = range(6)
EXPERT_LANE0 = MOE_GROUPS


def _route_kernel(logit_ref, tri_ref, route_ref, count_ref, run_ref):
    @pl.when(pl.program_id(0) == 0)
    def _():
        run_ref[...] = jnp.zeros_like(run_ref)

    lg = logit_ref[...]
    tm = lg.shape[0]
    lane = lax.broadcasted_iota(jnp.int32, lg.shape, 1)
    big = jnp.int32(1 << 20)

    def top(vals):
        m = jnp.max(vals, axis=-1, keepdims=True)
        i = jnp.min(jnp.where(vals == m, lane, big), axis=-1, keepdims=True)
        return m, i

    gl = jnp.where(lane < MOE_GROUPS, lg, -jnp.inf)
    gmax, gidx = top(gl)
    group_gate = 1.0 / jnp.sum(jnp.exp(gl - gmax), axis=-1, keepdims=True)
    lo = EXPERT_LANE0 + gidx * EXPERTS_PER_GROUP
    el = jnp.where((lane >= lo) & (lane < lo + EXPERTS_PER_GROUP), lg, -jnp.inf)
    t1, i1 = top(el)
    t2, i2 = top(jnp.where(lane == i1, -jnp.inf, el))
    e21 = jnp.exp(t2 - t1)
    g1 = group_gate / (1.0 + e21)
    g2 = group_gate * e21 / (1.0 + e21)
    e1 = i1 - EXPERT_LANE0
    e2 = i2 - EXPERT_LANE0
    oh1 = lane == e1
    oh2 = lane == e2
    cnt = oh1.astype(F32) + oh2.astype(F32)
    prefix = jnp.dot(tri_ref[...], cnt.astype(BF16), preferred_element_type=F32) + run_ref[0:1, :]
    r1 = jnp.sum(jnp.where(oh1, prefix, 0.0), axis=-1, keepdims=True)
    r2 = jnp.sum(jnp.where(oh2, prefix, 0.0), axis=-1, keepdims=True)
    new_run = run_ref[0:1, :] + jnp.sum(cnt, axis=0, keepdims=True)
    run_ref[...] = jnp.broadcast_to(new_run, run_ref.shape)
    count_ref[...] = jnp.broadcast_to(new_run, count_ref.shape)
    out = jnp.zeros(lg.shape, F32)
    for ln, val in ((ROUTE_E1, e1.astype(F32)), (ROUTE_E2, e2.astype(F32)), (ROUTE_G1, g1),
                    (ROUTE_G2, g2), (ROUTE_R1, r1), (ROUTE_R2, r2)):
        out = jnp.where(lane == ln, val, out)
    route_ref[...] = out


def _route(logits, tm):
    t = logits.shape[0]
    tri = jnp.asarray(np.tril(np.ones((tm, tm), np.float32), -1), BF16)
    return pl.pallas_call(
        _route_kernel,
        grid=(t // tm,),
        in_specs=[pl.BlockSpec((tm, LANES), lambda i: (i, 0)), pl.BlockSpec((tm, tm), lambda i: (0, 0))],
        out_specs=[pl.BlockSpec((tm, LANES), lambda i: (i, 0)), pl.BlockSpec((8, LANES), lambda i: (0, 0))],
        out_shape=[jax.ShapeDtypeStruct((t, LANES), F32), jax.ShapeDtypeStruct((8, LANES), F32)],
        scratch_shapes=[pltpu.VMEM((8, LANES), F32)],
        compiler_params=_cparams(("arbitrary",)),
        name="route",
    )(logits, tri)


TOP_K = 2
DMA_ISSUE_UNROLL = 8
DMA_PRIORITIES = 2


SEG_ALIGN = 8
SORTED_TAIL = N_EXPERTS * SEG_ALIGN + MOE_BLOCK


def _dispatch_kernel(seg_end_ref, pos_ref, hn_ref, xs_hbm, zbuf, sem, zsem, *, n_rows):
    tm = hn_ref.shape[0]

    @pl.when(pl.program_id(0) == 0)
    def _():
        zbuf[...] = jnp.zeros_like(zbuf)
        tail = pltpu.make_async_copy(zbuf, xs_hbm.at[pl.ds(n_rows, SORTED_TAIL)], zsem)
        tail.start()
        tail.wait()

        def hole(e):
            end = seg_end_ref[e]
            start = pl.multiple_of(end - (end & (SEG_ALIGN - 1)), SEG_ALIGN)
            return pltpu.make_async_copy(zbuf.at[pl.ds(0, SEG_ALIGN)], xs_hbm.at[pl.ds(start, SEG_ALIGN)], zsem)

        for e in range(N_EXPERTS):
            @pl.when((seg_end_ref[e] & (SEG_ALIGN - 1)) != 0)
            def _():
                hole(e).start()
        for e in range(N_EXPERTS):
            @pl.when((seg_end_ref[e] & (SEG_ALIGN - 1)) != 0)
            def _():
                hole(e).wait()

    def body(j, c):
        for k in range(TOP_K):
            pltpu.make_async_copy(hn_ref.at[pl.ds(j, 1)], xs_hbm.at[pl.ds(pos_ref[0, 0, TOP_K * j + k], 1)],
                                  sem).start(priority=k % DMA_PRIORITIES)
        return c
    lax.fori_loop(0, tm, body, 0, unroll=DMA_ISSUE_UNROLL)

    tile = pltpu.make_async_copy(hn_ref, xs_hbm.at[pl.ds(0, tm)], sem)
    for k in range(TOP_K):
        tile.wait()


def _dispatch(hn_packed, pos, seg_end, tm):
    t, dp = hn_packed.shape
    n_rows = TOP_K * t
    pos3 = pos.reshape(t // tm, 1, TOP_K * tm)
    grid_spec = pltpu.PrefetchScalarGridSpec(
        num_scalar_prefetch=1,
        grid=(t // tm,),
        in_specs=[pl.BlockSpec((1, 1, TOP_K * tm), lambda i, se: (i, 0, 0), memory_space=pltpu.SMEM),
                  pl.BlockSpec((tm, dp), lambda i, se: (i, 0))],
        out_specs=pl.BlockSpec(memory_space=pl.ANY),
        scratch_shapes=[pltpu.VMEM((SORTED_TAIL, dp), jnp.uint32),
                        pltpu.SemaphoreType.DMA,
                        pltpu.SemaphoreType.DMA],
    )
    return pl.pallas_call(
        functools.partial(_dispatch_kernel, n_rows=n_rows),
        grid_spec=grid_spec,
        out_shape=jax.ShapeDtypeStruct((n_rows + SORTED_TAIL, dp), jnp.uint32),
        compiler_params=_cparams(("arbitrary",)),
        name="dispatch",
    )(seg_end, pos3, hn_packed)


def _expert_kernel(bexp_ref, nreal_ref, row0_ref, xs_hbm, wg_ref, wu_ref, wd_ref, ys_hbm,
                   xbuf, ybuf, isem, osem, *, n_rows):
    del bexp_ref
    i = pl.program_id(0)
    nb = pl.num_programs(0)
    slot = i % 2
    nslot = 1 - slot
    n_cur = nreal_ref[i]
    prev = jnp.maximum(i - 1, 0)
    nxt = jnp.minimum(i + 1, nb - 1)

    def in_copy(blk, s):
        row0 = pl.multiple_of(row0_ref[blk], SEG_ALIGN)
        return pltpu.make_async_copy(xs_hbm.at[pl.ds(row0, MOE_BLOCK)], xbuf.at[s], isem.at[s])

    def out_copy(blk, s):
        row0 = pl.multiple_of(row0_ref[blk], SEG_ALIGN)
        return pltpu.make_async_copy(ybuf.at[s], ys_hbm.at[pl.ds(row0, MOE_BLOCK)], osem.at[s])

    @pl.when(i == 0)
    def _():
        ybuf[...] = jnp.zeros_like(ybuf)
        tails = [pltpu.make_async_copy(ybuf.at[s], ys_hbm.at[pl.ds(n_rows + s * MOE_BLOCK, MOE_BLOCK)], osem.at[s])
                 for s in range(SORTED_TAIL // MOE_BLOCK)]
        for tail in tails:
            tail.start()
        for tail in tails:
            tail.wait()

        @pl.when(n_cur > 0)
        def _():
            in_copy(i, slot).start()

    @pl.when((i + 1 < nb) & (nreal_ref[nxt] > 0))
    def _():
        in_copy(nxt, nslot).start()

    @pl.when(n_cur > 0)
    def _():
        in_copy(i, slot).wait()
        xb = _unpack_bf16_pairs(xbuf[slot]).astype(BF16)
        gate = jnp.dot(xb, wg_ref[0], preferred_element_type=F32)
        up = jnp.dot(xb, wu_ref[0], preferred_element_type=F32)
        hid = (gate * jax.nn.sigmoid(gate) * up).astype(BF16)
        ybuf[slot] = _pack_bf16_pairs(jnp.dot(hid, wd_ref[0], preferred_element_type=F32))

    @pl.when((i >= 1) & (nreal_ref[prev] > 0))
    def _():
        out_copy(prev, nslot).wait()

    @pl.when(n_cur > 0)
    def _():
        out_copy(i, slot).start()

        @pl.when(i == nb - 1)
        def _():
            out_copy(i, slot).wait()


def _experts(xs, block_expert, block_nreal, block_row0, wg, wu, wd):
    n_blocks = block_expert.shape[0]
    assert n_blocks >= 2 and SORTED_TAIL == 2 * MOE_BLOCK
    n_rows_pad, dp = xs.shape
    _, d, ff = wg.shape
    grid_spec = pltpu.PrefetchScalarGridSpec(
        num_scalar_prefetch=3,
        grid=(n_blocks,),
        in_specs=[pl.BlockSpec(memory_space=pl.ANY),
                  pl.BlockSpec((1, d, ff), lambda i, be, nr, r0: (be[i], 0, 0)),
                  pl.BlockSpec((1, d, ff), lambda i, be, nr, r0: (be[i], 0, 0)),
                  pl.BlockSpec((1, ff, d), lambda i, be, nr, r0: (be[i], 0, 0))],
        out_specs=pl.BlockSpec(memory_space=pl.ANY),
        scratch_shapes=[pltpu.VMEM((2, MOE_BLOCK, dp), jnp.uint32),
                        pltpu.VMEM((2, MOE_BLOCK, dp), jnp.uint32),
                        pltpu.SemaphoreType.DMA((2,)),
                        pltpu.SemaphoreType.DMA((2,))],
    )
    return pl.pallas_call(
        functools.partial(_expert_kernel, n_rows=n_rows_pad - SORTED_TAIL),
        grid_spec=grid_spec,
        out_shape=jax.ShapeDtypeStruct((n_rows_pad, dp), jnp.uint32),
        compiler_params=_cparams(("arbitrary",)),
        name="experts",
    )(block_expert, block_nreal, block_row0, xs, wg, wu, wd)


def _final_kernel(pos_ref, pos_next_ref, h_ref, route_ref, gain_ref, ys_hbm, out_ref, ybuf, sem):
    i = pl.program_id(0)
    nt = pl.num_programs(0)
    slot = i % 2
    tm = h_ref.shape[0]

    def gather_start(idx_ref, s):
        def body(j, c):
            for k in range(TOP_K):
                pltpu.make_async_copy(ys_hbm.at[pl.ds(idx_ref[0, 0, TOP_K * j + k], 1)],
                                      ybuf.at[s, k, pl.ds(j, 1)], sem.at[s]).start(priority=k % DMA_PRIORITIES)
            return c
        lax.fori_loop(0, tm, body, 0, unroll=DMA_ISSUE_UNROLL)

    @pl.when(i == 0)
    def _():
        gather_start(pos_ref, slot)

    @pl.when(i + 1 < nt)
    def _():
        gather_start(pos_next_ref, 1 - slot)

    for k in range(TOP_K):
        pltpu.make_async_copy(ys_hbm.at[pl.ds(0, tm)], ybuf.at[slot, k], sem.at[slot]).wait()

    r = route_ref[...]
    y = h_ref[...]
    for k, gate_lane in enumerate((ROUTE_G1, ROUTE_G2)):
        y = y + r[:, gate_lane:gate_lane + 1] * _unpack_bf16_pairs(ybuf[slot, k])
    out_ref[...] = y * lax.rsqrt(jnp.mean(y * y, axis=-1, keepdims=True) + NORM_EPS) * gain_ref[...]


def _final(h, ys, pos, route, gain, tm):
    t, d = h.shape
    nt = t // tm
    pos3 = pos.reshape(nt, 1, TOP_K * tm)
    smem_blk = lambda imap: pl.BlockSpec((1, 1, TOP_K * tm), imap, memory_space=pltpu.SMEM)
    return pl.pallas_call(
        _final_kernel,
        grid=(nt,),
        in_specs=[smem_blk(lambda i: (i, 0, 0)),
                  smem_blk(lambda i: (jnp.minimum(i + 1, nt - 1), 0, 0)),
                  pl.BlockSpec((tm, d), lambda i: (i, 0)),
                  pl.BlockSpec((tm, LANES), lambda i: (i, 0)),
                  pl.BlockSpec((1, d), lambda i: (0, 0)),
                  pl.BlockSpec(memory_space=pl.ANY)],
        out_specs=pl.BlockSpec((tm, d), lambda i: (i, 0)),
        out_shape=jax.ShapeDtypeStruct((t, d), F32),
        scratch_shapes=[pltpu.VMEM((2, TOP_K, tm, d // 2), jnp.uint32),
                        pltpu.SemaphoreType.DMA((2,))],
        compiler_params=_cparams(("arbitrary",)),
        name="final",
    )(pos3, pos3, h, route, gain.reshape(1, d), ys)


def _layer(h3, mix_gain, w_in, attn_gain, decay_f, decay_b, ret_gain, w_out, ffn_gain,
           w_rg, b_rg, w_re, b_re, w_eg, w_eu, w_ed, final_gain):
    b, s, d = h3.shape
    t = b * s
    tm = 512
    attn_qkv, (qr, kr, vr, gr) = _inproj(h3, mix_gain, w_in.astype(BF16), tm)

    o_list, lse_list = [], []
    for window, dilation in DILATED_BRANCHES:
        reach = (window // 2) // dilation
        o, lse = _attn_branch(*attn_qkv[dilation], dilation, reach, tq=128, tqb=512)
        o_list.append(o)
        lse_list.append(lse)

    lg_f = jnp.log1p(-jnp.exp2(decay_f.astype(F32)))
    lg_b = jnp.log1p(-jnp.exp2(decay_b.astype(F32)))
    orr = _retention(qr, kr, vr, gr, lg_f, lg_b, ret_gain).reshape(t, HEAD_GROUP_WIDTH)

    n_route = MOE_GROUPS + N_EXPERTS
    w_router = jnp.zeros((d, LANES), F32).at[:, :n_route].set(jnp.concatenate([w_rg, w_re], axis=1).astype(F32))
    b_router = jnp.zeros((1, LANES), F32).at[0, :n_route].set(jnp.concatenate([b_rg, b_re]).astype(F32))
    h, hn_packed, logits = _outproj(h3.reshape(t, d), o_list, lse_list, orr, attn_gain, w_out.astype(BF16),
                                    ffn_gain, w_router, b_router, tm)

    route, counts8 = _route(logits, tm)

    n_blocks = -(-TOP_K * t // MOE_BLOCK) + N_EXPERTS
    counts = counts8[0, :N_EXPERTS].astype(jnp.int32)
    aligned = ((counts + SEG_ALIGN - 1) // SEG_ALIGN) * SEG_ALIGN
    seg_start = jnp.cumsum(aligned) - aligned
    expert_iota = jnp.arange(N_EXPERTS, dtype=jnp.int32)
    e12 = route[:, ROUTE_E1:ROUTE_E2 + 1].astype(jnp.int32)
    r12 = route[:, ROUTE_R1:ROUTE_R2 + 1].astype(jnp.int32)
    pos = r12 + jnp.sum(jnp.where(e12[..., None] == expert_iota, seg_start, 0), axis=-1)
    nblk = (counts + MOE_BLOCK - 1) // MOE_BLOCK
    blk_end = jnp.cumsum(nblk)
    blk_start = blk_end - nblk
    blk = jnp.arange(n_blocks, dtype=jnp.int32)[:, None]
    owner = (blk >= blk_start) & (blk < blk_end)
    local = (blk - blk_start) * MOE_BLOCK
    block_row0 = jnp.sum(jnp.where(owner, seg_start + local, 0), axis=-1).astype(jnp.int32)
    block_nreal = jnp.sum(jnp.where(owner, jnp.clip(counts - local, 0, MOE_BLOCK), 0), axis=-1).astype(jnp.int32)
    block_expert = jnp.minimum(jnp.sum((blk >= blk_end).astype(jnp.int32), axis=-1), N_EXPERTS - 1)

    xs = _dispatch(hn_packed, pos, seg_start + counts, tm)
    ys = _experts(xs, block_expert, block_nreal, block_row0, w_eg.astype(BF16), w_eu.astype(BF16),
                  w_ed.astype(BF16))
    out = _final(h, ys, pos, route, final_gain, tm)
    return out.reshape(b, s, d)


def kernel(x, mix_norm_gain, w_in, attn_out_gain, ret_decay_fwd, ret_decay_bwd, ret_out_gain, w_out,
           ffn_norm_gain, w_route_group, b_route_group, w_route_expert, b_route_expert,
           w_expert_gate, w_expert_up, w_expert_down, final_norm_gain):
    depth = mix_norm_gain.shape[0]
    assert depth == 1, "the final rmsnorm is fused into the single layer's combine kernel"
    l = 0
    return _layer(x, mix_norm_gain[l], w_in[l], attn_out_gain[l], ret_decay_fwd[l], ret_decay_bwd[l],
                  ret_out_gain[l], w_out[l], ffn_norm_gain[l], w_route_group[l], b_route_group[l],
                  w_route_expert[l], b_route_expert[l], w_expert_gate[l], w_expert_up[l], w_expert_down[l],
                  final_norm_gain)
```

```python
import functools

import numpy as np
import jax
import jax.numpy as jnp
from jax import lax
from jax.experimental import pallas as pl
from jax.experimental.pallas import tpu as pltpu

F32 = jnp.float32
BF16 = jnp.bfloat16

ATTN_HEADS = 8
HEAD_DIM = 64
RET_HEADS = 8
HEAD_GROUP_WIDTH = 512
N_PROJ_GROUPS = 7
DILATED_BRANCHES = ((128, 1), (512, 4), (2048, 16))
ROPE_THETA = 500000.0
ROPE_DIM = HEAD_DIM // 4
RET_THETA = 10000.0
RET_CHUNK = 128
MOE_GROUPS = 4
EXPERTS_PER_GROUP = 8
N_EXPERTS = MOE_GROUPS * EXPERTS_PER_GROUP
MOE_BLOCK = 256
NORM_EPS = 1e-6
NEG_INF = -1e30

LANES = 128
VMEM_LIMIT = 56 * 1024 * 1024


def _cparams(sem):
    return pltpu.CompilerParams(dimension_semantics=sem, vmem_limit_bytes=VMEM_LIMIT)


def _rotary_tables(seq, half, freqs):
    pos = np.arange(seq, dtype=np.float64)[:, None]
    ang = pos * freqs[None, :].astype(np.float64)
    cos, sin = np.cos(ang), np.sin(ang)
    c = np.ones((seq, HEAD_DIM)); sp = np.zeros((seq, HEAD_DIM)); sm = np.zeros((seq, HEAD_DIM))
    c[:, :half] = cos; c[:, half:2 * half] = cos
    sp[:, half:2 * half] = sin
    sm[:, :half] = -sin
    rep = LANES // HEAD_DIM
    return tuple(jnp.asarray(np.tile(t, (1, rep)), F32) for t in (c, sp, sm))


def _rotate(t, c, sp, sm, half):
    outs = []
    for g in range(t.shape[1] // LANES):
        tg = t[:, g * LANES:(g + 1) * LANES]
        outs.append(tg * c + pltpu.roll(tg, half, 1) * sp + pltpu.roll(tg, LANES - half, 1) * sm)
    return jnp.concatenate(outs, axis=1)


CLASS_DILATIONS = tuple(d for _, d in DILATED_BRANCHES if d > 1)
ATTN_Q_SCALE = float(np.log2(np.e)) * HEAD_DIM ** -0.5


def _inproj_kernel(x_ref, gain_ref, w_ref, ca_ref, spa_ref, sma_ref, cr_ref, spr_ref, smr_ref, *rest):
    n_cls = len(CLASS_DILATIONS)
    nat_refs = rest[0:3]
    cls_refs = [rest[3 + 3 * c:6 + 3 * c] for c in range(n_cls)]
    qr_ref, kr_ref, vr_ref, gr_ref = rest[3 + 3 * n_cls:7 + 3 * n_cls]
    stage_ref = rest[7 + 3 * n_cls]
    x = x_ref[0]
    tm = x.shape[0]
    ms = jnp.mean(x * x, axis=-1, keepdims=True)
    xn = (x * lax.rsqrt(ms + NORM_EPS) * gain_ref[...]).astype(BF16)
    gw = HEAD_GROUP_WIDTH

    def proj(c):
        return jnp.dot(xn, w_ref[:, c * gw:(c + 1) * gw], preferred_element_type=F32)

    a_tabs = (ca_ref[...], spa_ref[...], sma_ref[...])
    r_tabs = (cr_ref[...], spr_ref[...], smr_ref[...])
    attn_vals = ((_rotate(proj(0), *a_tabs, ROPE_DIM // 2) * ATTN_Q_SCALE),
                 _rotate(proj(1), *a_tabs, ROPE_DIM // 2),
                 proj(2))
    for j, val in enumerate(attn_vals):
        nat_refs[j][0] = val.astype(BF16)
        for g in range(gw // LANES):
            stage_ref[g] = val[:, g * LANES:(g + 1) * LANES]
        for c, d in enumerate(CLASS_DILATIONS):
            for r in range(d):
                for g in range(gw // LANES):
                    col = r * gw + g * LANES
                    cls_refs[c][j][0, :, col:col + LANES] = (
                        stage_ref[g, pl.ds(r, tm // d, stride=d), :].astype(BF16))
    qr_ref[0] = _rotate(proj(3), *r_tabs, HEAD_DIM // 2).astype(BF16)
    kr_ref[0] = (_rotate(proj(4), *r_tabs, HEAD_DIM // 2) * (HEAD_DIM ** -0.5)).astype(BF16)
    vr_ref[0] = proj(5).astype(BF16)
    g = proj(6)
    gr_ref[0] = (g * jax.nn.sigmoid(g)).astype(BF16)


def _inproj(x, gain, w_in_bf16, tm):
    b, s, d = x.shape
    rope_freqs = ROPE_THETA ** (-np.arange(0, ROPE_DIM, 2, dtype=np.float32) / ROPE_DIM)
    ret_freqs = RET_THETA ** (-np.linspace(0.0, 1.0, HEAD_DIM // 2, dtype=np.float32))
    tabs = _rotary_tables(s, ROPE_DIM // 2, rope_freqs) + _rotary_tables(s, HEAD_DIM // 2, ret_freqs)
    gw = HEAD_GROUP_WIDTH
    tab_spec = pl.BlockSpec((tm, LANES), lambda si, bi: (si, 0))

    def view(dil):
        return (pl.BlockSpec((1, tm // dil, dil * gw), lambda si, bi: (bi, si, 0)),
                jax.ShapeDtypeStruct((b, s // dil, dil * gw), BF16))

    views = [view(1)] * 3 + [view(dil) for dil in CLASS_DILATIONS for _ in range(3)] + [view(1)] * 4
    outs = pl.pallas_call(
        _inproj_kernel,
        grid=(s // tm, b),
        in_specs=[pl.BlockSpec((1, tm, d), lambda si, bi: (bi, si, 0)),
                  pl.BlockSpec((1, d), lambda si, bi: (0, 0)),
                  pl.BlockSpec(w_in_bf16.shape, lambda si, bi: (0, 0))] + [tab_spec] * 6,
        out_specs=[v[0] for v in views],
        out_shape=[v[1] for v in views],
        scratch_shapes=[pltpu.VMEM((gw // LANES, tm, LANES), F32)],
        compiler_params=_cparams(("arbitrary", "arbitrary")),
        name="inproj",
    )(x, gain.reshape(1, d), w_in_bf16, *tabs)
    n_attn = 3 * (1 + len(CLASS_DILATIONS))
    attn_qkv = {dil: outs[3 * c:3 * c + 3] for c, dil in enumerate((1,) + CLASS_DILATIONS)}
    return attn_qkv, outs[n_attn:]


ATTN_WINDOW_CASES = 3


def _attn_kernel(q_ref, k_ref, v_ref, o_ref, lse_ref, bias_ref, *, length, tq, reach):
    tqb = q_ref.shape[1]
    win = tq + 2 * reach
    heads_per_pair = LANES // HEAD_DIM
    qi = pl.program_id(2)
    lane = lax.broadcasted_iota(jnp.int32, (1, LANES), 1)
    lane_t = lax.broadcasted_iota(jnp.int32, (tq, LANES), 1)
    ones = jnp.ones((win, LANES), BF16)

    @pl.when((pl.program_id(0) == 0) & (pl.program_id(1) == 0) & (qi == 0))
    def _():
        diff = (lax.broadcasted_iota(jnp.int32, (heads_per_pair * tq, win), 1)
                - (lax.broadcasted_iota(jnp.int32, (heads_per_pair * tq, win), 0) & (tq - 1)))
        for case in range(ATTN_WINDOW_CASES):
            off = case * reach
            bias_ref[case] = jnp.where((diff >= off - reach) & (diff <= off + reach), 0.0, NEG_INF)

    def sub(t, carry):
        q0 = qi * tqb + t * tq
        ws = jnp.clip(q0 - reach, 0, length - win)
        ws = pl.multiple_of(ws, reach)
        bias = bias_ref[(q0 - ws) // reach]
        rows = pl.ds(pl.multiple_of(t * tq, tq), tq)
        m_tile = jnp.zeros((tq, LANES), F32)
        l_tile = jnp.ones((tq, LANES), F32)
        for g in range(HEAD_GROUP_WIDTH // LANES):
            cols = slice(g * LANES, (g + 1) * LANES)
            qg = q_ref[0, rows, cols]
            kw = k_ref[0, pl.ds(ws, win), cols]
            v_ones = jnp.concatenate([v_ref[0, pl.ds(ws, win), cols], ones], axis=1)
            hms = [(lane >= hh * HEAD_DIM) & (lane < (hh + 1) * HEAD_DIM) for hh in range(heads_per_pair)]
            q2 = jnp.concatenate([jnp.where(hm, qg, jnp.zeros_like(qg)) for hm in hms], axis=0)
            sc = lax.dot_general(q2, kw, (((1,), (1,)), ((), ())), preferred_element_type=F32) + bias
            m = jnp.max(sc, axis=-1, keepdims=True)
            p = jnp.exp2(sc - m).astype(BF16)
            pv = jnp.dot(p, v_ones, preferred_element_type=F32)
            l = pv[:, LANES:]
            o = pv[:, :LANES] / l
            o_pair = jnp.zeros((tq, LANES), F32)
            for hh in range(heads_per_pair):
                part = slice(hh * tq, (hh + 1) * tq)
                head_lane = lane_t == g * heads_per_pair + hh
                o_pair = jnp.where(hms[hh], o[part], o_pair)
                m_tile = jnp.where(head_lane, m[part], m_tile)
                l_tile = jnp.where(head_lane, l[part], l_tile)
            o_ref[0, rows, cols] = o_pair.astype(BF16)
        lse_ref[0, rows, :] = m_tile + jnp.log2(l_tile)
        return carry

    lax.fori_loop(0, tqb // tq, sub, 0)


def _attn_branch(qc, kc, vc, dilation, reach, tq, tqb):
    b, length, dw = qc.shape
    w = dw // dilation
    tqb = min(tqb, length)
    assert tq % reach == 0 and tq > reach and length % tq == 0 and length >= tq + 2 * reach
    q_spec = pl.BlockSpec((1, tqb, w), lambda bi, r, qi: (bi, qi, r))
    kv_spec = pl.BlockSpec((1, length, w), lambda bi, r, qi: (bi, 0, r))
    o, lse = pl.pallas_call(
        functools.partial(_attn_kernel, length=length, tq=tq, reach=reach),
        grid=(b, dilation, length // tqb),
        in_specs=[q_spec, kv_spec, kv_spec],
        out_specs=[q_spec, pl.BlockSpec((1, tqb, LANES), lambda bi, r, qi: (bi, qi, r))],
        out_shape=[jax.ShapeDtypeStruct((b, length, dilation * w), BF16),
                   jax.ShapeDtypeStruct((b, length, dilation * LANES), F32)],
        scratch_shapes=[pltpu.VMEM((ATTN_WINDOW_CASES, (LANES // HEAD_DIM) * tq, tq + 2 * reach), F32)],
        compiler_params=_cparams(("arbitrary", "arbitrary", "arbitrary")),
        name=f"attn_d{dilation}",
    )(qc, kc, vc)
    return o, lse


RET_TAB_QF, RET_TAB_QB, RET_TAB_KF, RET_TAB_KB = range(4)


def _retention_kernel(lgf_ref, lgb_ref, q_ref, k_ref, v_ref, g_ref, gain_ref, o_ref,
                      tab_ref, dec_ref, sb_ref, st_ref, *, chunk, unroll):
    c = chunk
    n = q_ref.shape[1] // c
    width = q_ref.shape[2]
    n_pairs = width // LANES
    heads_per_pair = LANES // HEAD_DIM
    n_heads = n_pairs * heads_per_pair
    head0 = pl.program_id(1) * n_heads
    lane_w = lax.broadcasted_iota(jnp.int32, (1, width), 1)
    lgf = [lgf_ref[head0 + hd] for hd in range(n_heads)]
    lgb = [lgb_ref[head0 + hd] for hd in range(n_heads)]
    lgf_lane = jnp.zeros((1, width), F32)
    lgb_lane = jnp.zeros((1, width), F32)
    for hd in range(n_heads):
        in_head = (lane_w >= hd * HEAD_DIM) & (lane_w < (hd + 1) * HEAD_DIM)
        lgf_lane = jnp.where(in_head, lgf[hd], lgf_lane)
        lgb_lane = jnp.where(in_head, lgb[hd], lgb_lane)
    idx = lax.broadcasted_iota(jnp.int32, (c, width), 0).astype(F32)
    tab_ref[RET_TAB_QF] = jnp.exp((idx + 1.0) * lgf_lane)
    tab_ref[RET_TAB_QB] = jnp.exp((c - idx) * lgb_lane)
    tab_ref[RET_TAB_KF] = jnp.exp((c - 1.0 - idx) * lgf_lane)
    tab_ref[RET_TAB_KB] = jnp.exp(idx * lgb_lane)
    sdf = jnp.exp(c * lgf_lane)
    sdb = jnp.exp(c * lgb_lane)
    dmat = (lax.broadcasted_iota(jnp.int32, (c, c), 0)
            - lax.broadcasted_iota(jnp.int32, (c, c), 1)).astype(F32)
    for hd in range(n_heads):
        dec_ref[hd] = jnp.where(dmat >= 0, jnp.exp(dmat * lgf[hd]), jnp.exp(-dmat * lgb[hd]))
    lane = lax.broadcasted_iota(jnp.int32, (1, LANES), 1)
    lane_hi = lane >= HEAD_DIM
    row_hi = lax.broadcasted_iota(jnp.int32, (LANES, LANES), 0) >= HEAD_DIM
    col_hi = lax.broadcasted_iota(jnp.int32, (LANES, LANES), 1) >= HEAD_DIM
    blockdiag = row_hi == col_hi

    def kv_state(kd, vv):
        kt = jnp.transpose(kd).astype(BF16)
        return jnp.where(blockdiag, jnp.dot(kt, vv, preferred_element_type=F32), 0.0)

    st_ref[...] = jnp.zeros_like(st_ref)
    sb_ref[n - 1] = jnp.zeros(sb_ref.shape[1:], sb_ref.dtype)

    def back(i, carry):
        nn = n - 1 - i
        rows = pl.ds(pl.multiple_of(nn * c, c), c)
        for p in range(n_pairs):
            cols = slice(p * LANES, (p + 1) * LANES)
            kd = k_ref[0, rows, cols].astype(F32) * tab_ref[RET_TAB_KB, :, cols]
            new = st_ref[p] * sdb[:, cols] + kv_state(kd, v_ref[0, rows, cols])
            st_ref[p] = new
            sb_ref[nn - 1, p] = new.astype(BF16)
        return carry

    lax.fori_loop(0, n - 1, back, 0, unroll=unroll)

    st_ref[...] = jnp.zeros_like(st_ref)

    def fwd(nn, carry):
        rows = pl.ds(pl.multiple_of(nn * c, c), c)
        for p in range(n_pairs):
            cols = slice(p * LANES, (p + 1) * LANES)
            qq = q_ref[0, rows, cols]
            kk = k_ref[0, rows, cols]
            vv = v_ref[0, rows, cols]
            qf = qq.astype(F32)
            sf = st_ref[p]
            qcat = jnp.concatenate([(qf * tab_ref[RET_TAB_QF, :, cols]).astype(BF16),
                                    (qf * tab_ref[RET_TAB_QB, :, cols]).astype(BF16)], axis=1)
            scat = jnp.concatenate([sf.astype(BF16), sb_ref[nn, p]], axis=0)
            o = jnp.dot(qcat, scat, preferred_element_type=F32)
            ps, vs = [], []
            for hh in range(heads_per_pair):
                hm = (lane >= hh * HEAD_DIM) & (lane < (hh + 1) * HEAD_DIM)
                qh = jnp.where(hm, qq, jnp.zeros_like(qq))
                sc = lax.dot_general(qh, kk, (((1,), (1,)), ((), ())), preferred_element_type=F32)
                ps.append((sc * dec_ref[p * heads_per_pair + hh]).astype(BF16))
                vs.append(jnp.where(hm, vv, jnp.zeros_like(vv)))
            o = o + jnp.dot(jnp.concatenate(ps, axis=1), jnp.concatenate(vs, axis=0),
                            preferred_element_type=F32)
            s_lo = jnp.sum(jnp.where(lane_hi, 0.0, o), axis=-1, keepdims=True)
            s_hi = jnp.sum(jnp.where(lane_hi, o, 0.0), axis=-1, keepdims=True)
            mu = jnp.where(lane_hi, s_hi, s_lo) * (1.0 / HEAD_DIM)
            dev = o - mu
            d2 = dev * dev
            v_lo = jnp.sum(jnp.where(lane_hi, 0.0, d2), axis=-1, keepdims=True)
            v_hi = jnp.sum(jnp.where(lane_hi, d2, 0.0), axis=-1, keepdims=True)
            var = jnp.where(lane_hi, v_hi, v_lo) * (1.0 / HEAD_DIM)
            out = dev * lax.rsqrt(var + NORM_EPS) * gain_ref[:, cols] * g_ref[0, rows, cols].astype(F32)
            o_ref[0, rows, cols] = out.astype(BF16)
            kd = kk.astype(F32) * tab_ref[RET_TAB_KF, :, cols]
            st_ref[p] = sf * sdf[:, cols] + kv_state(kd, vv)
        return carry

    lax.fori_loop(0, n, fwd, 0, unroll=unroll)


def _retention(qr, kr, vr, gate, lg_f, lg_b, out_gain, width=512, unroll=1):
    b, s, w = qr.shape
    n_pairs = width // LANES
    n_heads = width // HEAD_DIM
    spec = pl.BlockSpec((1, s, width), lambda bi, p, *_: (bi, 0, p))
    grid_spec = pltpu.PrefetchScalarGridSpec(
        num_scalar_prefetch=2,
        grid=(b, w // width),
        in_specs=[spec, spec, spec, spec, pl.BlockSpec((1, width), lambda bi, p, *_: (0, p))],
        out_specs=spec,
        scratch_shapes=[pltpu.VMEM((4, RET_CHUNK, width), F32),
                        pltpu.VMEM((n_heads, RET_CHUNK, RET_CHUNK), F32),
                        pltpu.VMEM((s // RET_CHUNK, n_pairs, LANES, LANES), BF16),
                        pltpu.VMEM((n_pairs, LANES, LANES), F32)],
    )
    return pl.pallas_call(
        functools.partial(_retention_kernel, chunk=RET_CHUNK, unroll=unroll),
        grid_spec=grid_spec,
        out_shape=jax.ShapeDtypeStruct((b, s, w), BF16),
        compiler_params=_cparams(("arbitrary", "arbitrary")),
        name="retention",
    )(lg_f, lg_b, qr, kr, vr, gate, out_gain.reshape(1, w))


def _split_bf16(t):
    hi = t.astype(BF16)
    lo = (t - hi.astype(F32)).astype(BF16)
    return hi, lo


def _pack_bf16_pairs(t):
    n = t.shape[1] // 2
    hi = pltpu.bitcast(t[:, :n].astype(BF16).astype(F32), jnp.uint32)
    lo = pltpu.bitcast(t[:, n:].astype(BF16).astype(F32), jnp.uint32)
    return hi | (lo >> 16)


def _unpack_bf16_pairs(u):
    hi = pltpu.bitcast(u & jnp.uint32(0xFFFF0000), F32)
    lo = pltpu.bitcast(u << 16, F32)
    return jnp.concatenate([hi, lo], axis=1)


def _outproj_kernel(x_ref, o1_ref, o2_ref, o3_ref, l1_ref, l2_ref, l3_ref, orr_ref, ga_ref, expand_ref,
                    wout_ref, gf_ref, wr_hi_ref, wr_lo_ref, br_ref, h_ref, hn_ref, logit_ref,
                    *nat_refs):
    tm = x_ref.shape[0]
    gw = HEAD_GROUP_WIDTH
    os, ls = [], []
    for (_, dil), o_ref, l_ref in zip(DILATED_BRANCHES, (o1_ref, o2_ref, o3_ref), (l1_ref, l2_ref, l3_ref)):
        if dil == 1:
            os.append(o_ref[...].astype(F32))
            ls.append(l_ref[...])
            continue
        c = CLASS_DILATIONS.index(dil)
        onat_ref, lnat_ref = nat_refs[2 * c], nat_refs[2 * c + 1]
        for r in range(dil):
            rows = pl.ds(r, tm // dil, stride=dil)
            for g in range(gw // LANES):
                col = r * gw + g * LANES
                onat_ref[g, rows, :] = o_ref[:, col:col + LANES].astype(F32)
            lnat_ref[rows, :] = l_ref[:, r * LANES:(r + 1) * LANES]
        os.append(jnp.concatenate([onat_ref[g] for g in range(gw // LANES)], axis=1))
        ls.append(lnat_ref[...])
    mx = jnp.maximum(jnp.maximum(ls[0], ls[1]), ls[2])
    es = [jnp.exp2(l - mx) for l in ls]
    inv = 1.0 / (es[0] + es[1] + es[2])
    expand = expand_ref[...]
    oa = jnp.zeros((tm, gw), F32)
    for e, o in zip(es, os):
        w_hi, w_lo = _split_bf16(e * inv)
        wexp = (jnp.dot(w_hi, expand, preferred_element_type=F32)
                + jnp.dot(w_lo, expand, preferred_element_type=F32))
        oa = oa + wexp * o
    oa = oa * lax.rsqrt(jnp.mean(oa * oa, axis=-1, keepdims=True) + NORM_EPS) * ga_ref[...]
    mixed = jnp.concatenate([oa.astype(BF16), orr_ref[...]], axis=1)
    h = x_ref[...] + jnp.dot(mixed, wout_ref[...], preferred_element_type=F32)
    h_ref[...] = h
    hn = h * lax.rsqrt(jnp.mean(h * h, axis=-1, keepdims=True) + NORM_EPS) * gf_ref[...]
    hn_ref[...] = _pack_bf16_pairs(hn)
    hn_hi, hn_lo = _split_bf16(hn)
    wr_hi = wr_hi_ref[...]
    logit_ref[...] = (jnp.dot(hn_hi, wr_hi, preferred_element_type=F32)
                      + jnp.dot(hn_lo, wr_hi, preferred_element_type=F32)
                      + jnp.dot(hn_hi, wr_lo_ref[...], preferred_element_type=F32)
                      + br_ref[...])


def _outproj(x2, o_list, lse_list, orr, attn_gain, w_out_bf16, ffn_gain, w_router, b_router, tm):
    t, d = x2.shape
    w = HEAD_GROUP_WIDTH
    expand = np.zeros((LANES, w), np.float32)
    for hd in range(ATTN_HEADS):
        expand[hd, hd * HEAD_DIM:(hd + 1) * HEAD_DIM] = 1.0
    expand = jnp.asarray(expand, BF16)
    wr_hi = w_router.astype(BF16)
    wr_lo = (w_router - wr_hi.astype(F32)).astype(BF16)
    row = lambda width, dil=1: pl.BlockSpec((tm // dil, dil * width), lambda i: (i, 0))
    full = lambda a: pl.BlockSpec(a.shape, lambda i: (0,) * a.ndim)
    ga = attn_gain.reshape(1, w)
    gf = ffn_gain.reshape(1, d)
    dils = [dil for _, dil in DILATED_BRANCHES]
    o_flat = [o.reshape(t // dil, dil * w) for o, dil in zip(o_list, dils)]
    l_flat = [l.reshape(t // dil, dil * LANES) for l, dil in zip(lse_list, dils)]
    nat_scratch = []
    for _ in CLASS_DILATIONS:
        nat_scratch += [pltpu.VMEM((w // LANES, tm, LANES), F32), pltpu.VMEM((tm, LANES), F32)]
    return pl.pallas_call(
        _outproj_kernel,
        grid=(t // tm,),
        in_specs=[row(d)] + [row(w, dil) for dil in dils] + [row(LANES, dil) for dil in dils] + [row(w)]
                 + [full(ga), full(expand), full(w_out_bf16), full(gf), full(wr_hi), full(wr_lo), full(b_router)],
        out_specs=[row(d), row(d // 2), row(LANES)],
        out_shape=[jax.ShapeDtypeStruct((t, d), F32),
                   jax.ShapeDtypeStruct((t, d // 2), jnp.uint32),
                   jax.ShapeDtypeStruct((t, LANES), F32)],
        scratch_shapes=nat_scratch,
        compiler_params=_cparams(("arbitrary",)),
        name="outproj",
    )(x2, *o_flat, *l_flat, orr, ga, expand, w_out_bf16, gf, wr_hi, wr_lo, b_router)


ROUTE_E1, ROUTE_E2, ROUTE_G1, ROUTE_G2, ROUTE_R1, ROUTE_R2 = range(6)
EXPERT_LANE0 = MOE_GROUPS


def _route_kernel(logit_ref, tri_ref, route_ref, count_ref, run_ref):
    @pl.when(pl.program_id(0) == 0)
    def _():
        run_ref[...] = jnp.zeros_like(run_ref)

    lg = logit_ref[...]
    tm = lg.shape[0]
    lane = lax.broadcasted_iota(jnp.int32, lg.shape, 1)
    big = jnp.int32(1 << 20)

    def top(vals):
        m = jnp.max(vals, axis=-1, keepdims=True)
        i = jnp.min(jnp.where(vals == m, lane, big), axis=-1, keepdims=True)
        return m, i

    gl = jnp.where(lane < MOE_GROUPS, lg, -jnp.inf)
    gmax, gidx = top(gl)
    group_gate = 1.0 / jnp.sum(jnp.exp(gl - gmax), axis=-1, keepdims=True)
    lo = EXPERT_LANE0 + gidx * EXPERTS_PER_GROUP
    el = jnp.where((lane >= lo) & (lane < lo + EXPERTS_PER_GROUP), lg, -jnp.inf)
    t1, i1 = top(el)
    t2, i2 = top(jnp.where(lane == i1, -jnp.inf, el))
    e21 = jnp.exp(t2 - t1)
    g1 = group_gate / (1.0 + e21)
    g2 = group_gate * e21 / (1.0 + e21)
    e1 = i1 - EXPERT_LANE0
    e2 = i2 - EXPERT_LANE0
    oh1 = lane == e1
    oh2 = lane == e2
    cnt = oh1.astype(F32) + oh2.astype(F32)
    prefix = jnp.dot(tri_ref[...], cnt.astype(BF16), preferred_element_type=F32) + run_ref[0:1, :]
    r1 = jnp.sum(jnp.where(oh1, prefix, 0.0), axis=-1, keepdims=True)
    r2 = jnp.sum(jnp.where(oh2, prefix, 0.0), axis=-1, keepdims=True)
    new_run = run_ref[0:1, :] + jnp.sum(cnt, axis=0, keepdims=True)
    run_ref[...] = jnp.broadcast_to(new_run, run_ref.shape)
    count_ref[...] = jnp.broadcast_to(new_run, count_ref.shape)
    out = jnp.zeros(lg.shape, F32)
    for ln, val in ((ROUTE_E1, e1.astype(F32)), (ROUTE_E2, e2.astype(F32)), (ROUTE_G1, g1),
                    (ROUTE_G2, g2), (ROUTE_R1, r1), (ROUTE_R2, r2)):
        out = jnp.where(lane == ln, val, out)
    route_ref[...] = out


def _route(logits, tm):
    t = logits.shape[0]
    tri = jnp.asarray(np.tril(np.ones((tm, tm), np.float32), -1), BF16)
    return pl.pallas_call(
        _route_kernel,
        grid=(t // tm,),
        in_specs=[pl.BlockSpec((tm, LANES), lambda i: (i, 0)), pl.BlockSpec((tm, tm), lambda i: (0, 0))],
        out_specs=[pl.BlockSpec((tm, LANES), lambda i: (i, 0)), pl.BlockSpec((8, LANES), lambda i: (0, 0))],
        out_shape=[jax.ShapeDtypeStruct((t, LANES), F32), jax.ShapeDtypeStruct((8, LANES), F32)],
        scratch_shapes=[pltpu.VMEM((8, LANES), F32)],
        compiler_params=_cparams(("arbitrary",)),
        name="route",
    )(logits, tri)


TOP_K = 2
DMA_ISSUE_UNROLL = 8
DMA_PRIORITIES = 2


SEG_ALIGN = 8
SORTED_TAIL = N_EXPERTS * SEG_ALIGN + MOE_BLOCK


def _dispatch_kernel(seg_end_ref, pos_ref, hn_ref, xs_hbm, zbuf, sem, zsem, *, n_rows):
    tm = hn_ref.shape[0]

    @pl.when(pl.program_id(0) == 0)
    def _():
        zbuf[...] = jnp.zeros_like(zbuf)
        tail = pltpu.make_async_copy(zbuf, xs_hbm.at[pl.ds(n_rows, SORTED_TAIL)], zsem)
        tail.start()
        tail.wait()

        def hole(e):
            end = seg_end_ref[e]
            start = pl.multiple_of(end - (end & (SEG_ALIGN - 1)), SEG_ALIGN)
            return pltpu.make_async_copy(zbuf.at[pl.ds(0, SEG_ALIGN)], xs_hbm.at[pl.ds(start, SEG_ALIGN)], zsem)

        for e in range(N_EXPERTS):
            @pl.when((seg_end_ref[e] & (SEG_ALIGN - 1)) != 0)
            def _():
                hole(e).start()
        for e in range(N_EXPERTS):
            @pl.when((seg_end_ref[e] & (SEG_ALIGN - 1)) != 0)
            def _():
                hole(e).wait()

    def body(j, c):
        for k in range(TOP_K):
            pltpu.make_async_copy(hn_ref.at[pl.ds(j, 1)], xs_hbm.at[pl.ds(pos_ref[0, 0, TOP_K * j + k], 1)],
                                  sem).start(priority=k % DMA_PRIORITIES)
        return c
    lax.fori_loop(0, tm, body, 0, unroll=DMA_ISSUE_UNROLL)

    tile = pltpu.make_async_copy(hn_ref, xs_hbm.at[pl.ds(0, tm)], sem)
    for k in range(TOP_K):
        tile.wait()


def _dispatch(hn_packed, pos, seg_end, tm):
    t, dp = hn_packed.shape
    n_rows = TOP_K * t
    pos3 = pos.reshape(t // tm, 1, TOP_K * tm)
    grid_spec = pltpu.PrefetchScalarGridSpec(
        num_scalar_prefetch=1,
        grid=(t // tm,),
        in_specs=[pl.BlockSpec((1, 1, TOP_K * tm), lambda i, se: (i, 0, 0), memory_space=pltpu.SMEM),
                  pl.BlockSpec((tm, dp), lambda i, se: (i, 0))],
        out_specs=pl.BlockSpec(memory_space=pl.ANY),
        scratch_shapes=[pltpu.VMEM((SORTED_TAIL, dp), jnp.uint32),
                        pltpu.SemaphoreType.DMA,
                        pltpu.SemaphoreType.DMA],
    )
    return pl.pallas_call(
        functools.partial(_dispatch_kernel, n_rows=n_rows),
        grid_spec=grid_spec,
        out_shape=jax.ShapeDtypeStruct((n_rows + SORTED_TAIL, dp), jnp.uint32),
        compiler_params=_cparams(("arbitrary",)),
        name="dispatch",
    )(seg_end, pos3, hn_packed)


def _expert_kernel(bexp_ref, nreal_ref, row0_ref, xs_hbm, wg_ref, wu_ref, wd_ref, ys_hbm,
                   xbuf, ybuf, isem, osem, *, n_rows):
    del bexp_ref
    i = pl.program_id(0)
    nb = pl.num_programs(0)
    slot = i % 2
    nslot = 1 - slot
    n_cur = nreal_ref[i]
    prev = jnp.maximum(i - 1, 0)
    nxt = jnp.minimum(i + 1, nb - 1)

    def in_copy(blk, s):
        row0 = pl.multiple_of(row0_ref[blk], SEG_ALIGN)
        return pltpu.make_async_copy(xs_hbm.at[pl.ds(row0, MOE_BLOCK)], xbuf.at[s], isem.at[s])

    def out_copy(blk, s):
        row0 = pl.multiple_of(row0_ref[blk], SEG_ALIGN)
        return pltpu.make_async_copy(ybuf.at[s], ys_hbm.at[pl.ds(row0, MOE_BLOCK)], osem.at[s])

    @pl.when(i == 0)
    def _():
        ybuf[...] = jnp.zeros_like(ybuf)
        tails = [pltpu.make_async_copy(ybuf.at[s], ys_hbm.at[pl.ds(n_rows + s * MOE_BLOCK, MOE_BLOCK)], osem.at[s])
                 for s in range(SORTED_TAIL // MOE_BLOCK)]
        for tail in tails:
            tail.start()
        for tail in tails:
            tail.wait()

        @pl.when(n_cur > 0)
        def _():
            in_copy(i, slot).start()

    @pl.when((i + 1 < nb) & (nreal_ref[nxt] > 0))
    def _():
        in_copy(nxt, nslot).start()

    @pl.when(n_cur > 0)
    def _():
        in_copy(i, slot).wait()
        xb = _unpack_bf16_pairs(xbuf[slot]).astype(BF16)
        gate = jnp.dot(xb, wg_ref[0], preferred_element_type=F32)
        up = jnp.dot(xb, wu_ref[0], preferred_element_type=F32)
        hid = (gate * jax.nn.sigmoid(gate) * up).astype(BF16)
        ybuf[slot] = _pack_bf16_pairs(jnp.dot(hid, wd_ref[0], preferred_element_type=F32))

    @pl.when((i >= 1) & (nreal_ref[prev] > 0))
    def _():
        out_copy(prev, nslot).wait()

    @pl.when(n_cur > 0)
    def _():
        out_copy(i, slot).start()

        @pl.when(i == nb - 1)
        def _():
            out_copy(i, slot).wait()


def _experts(xs, block_expert, block_nreal, block_row0, wg, wu, wd):
    n_blocks = block_expert.shape[0]
    assert n_blocks >= 2 and SORTED_TAIL == 2 * MOE_BLOCK
    n_rows_pad, dp = xs.shape
    _, d, ff = wg.shape
    grid_spec = pltpu.PrefetchScalarGridSpec(
        num_scalar_prefetch=3,
        grid=(n_blocks,),
        in_specs=[pl.BlockSpec(memory_space=pl.ANY),
                  pl.BlockSpec((1, d, ff), lambda i, be, nr, r0: (be[i], 0, 0)),
                  pl.BlockSpec((1, d, ff), lambda i, be, nr, r0: (be[i], 0, 0)),
                  pl.BlockSpec((1, ff, d), lambda i, be, nr, r0: (be[i], 0, 0))],
        out_specs=pl.BlockSpec(memory_space=pl.ANY),
        scratch_shapes=[pltpu.VMEM((2, MOE_BLOCK, dp), jnp.uint32),
                        pltpu.VMEM((2, MOE_BLOCK, dp), jnp.uint32),
                        pltpu.SemaphoreType.DMA((2,)),
                        pltpu.SemaphoreType.DMA((2,))],
    )
    return pl.pallas_call(
        functools.partial(_expert_kernel, n_rows=n_rows_pad - SORTED_TAIL),
        grid_spec=grid_spec,
        out_shape=jax.ShapeDtypeStruct((n_rows_pad, dp), jnp.uint32),
        compiler_params=_cparams(("arbitrary",)),
        name="experts",
    )(block_expert, block_nreal, block_row0, xs, wg, wu, wd)


def _final_kernel(pos_ref, pos_next_ref, h_ref, route_ref, gain_ref, ys_hbm, out_ref, ybuf, sem):
    i = pl.program_id(0)
    nt = pl.num_programs(0)
    slot = i % 2
    tm = h_ref.shape[0]

    def gather_start(idx_ref, s):
        def body(j, c):
            for k in range(TOP_K):
                pltpu.make_async_copy(ys_hbm.at[pl.ds(idx_ref[0, 0, TOP_K * j + k], 1)],
                                      ybuf.at[s, k, pl.ds(j, 1)], sem.at[s]).start(priority=k % DMA_PRIORITIES)
            return c
        lax.fori_loop(0, tm, body, 0, unroll=DMA_ISSUE_UNROLL)

    @pl.when(i == 0)
    def _():
        gather_start(pos_ref, slot)

    @pl.when(i + 1 < nt)
    def _():
        gather_start(pos_next_ref, 1 - slot)

    for k in range(TOP_K):
        pltpu.make_async_copy(ys_hbm.at[pl.ds(0, tm)], ybuf.at[slot, k], sem.at[slot]).wait()

    r = route_ref[...]
    y = h_ref[...]
    for k, gate_lane in enumerate((ROUTE_G1, ROUTE_G2)):
        y = y + r[:, gate_lane:gate_lane + 1] * _unpack_bf16_pairs(ybuf[slot, k])
    out_ref[...] = y * lax.rsqrt(jnp.mean(y * y, axis=-1, keepdims=True) + NORM_EPS) * gain_ref[...]


def _final(h, ys, pos, route, gain, tm):
    t, d = h.shape
    nt = t // tm
    pos3 = pos.reshape(nt, 1, TOP_K * tm)
    smem_blk = lambda imap: pl.BlockSpec((1, 1, TOP_K * tm), imap, memory_space=pltpu.SMEM)
    return pl.pallas_call(
        _final_kernel,
        grid=(nt,),
        in_specs=[smem_blk(lambda i: (i, 0, 0)),
                  smem_blk(lambda i: (jnp.minimum(i + 1, nt - 1), 0, 0)),
                  pl.BlockSpec((tm, d), lambda i: (i, 0)),
                  pl.BlockSpec((tm, LANES), lambda i: (i, 0)),
                  pl.BlockSpec((1, d), lambda i: (0, 0)),
                  pl.BlockSpec(memory_space=pl.ANY)],
        out_specs=pl.BlockSpec((tm, d), lambda i: (i, 0)),
        out_shape=jax.ShapeDtypeStruct((t, d), F32),
        scratch_shapes=[pltpu.VMEM((2, TOP_K, tm, d // 2), jnp.uint32),
                        pltpu.SemaphoreType.DMA((2,))],
        compiler_params=_cparams(("arbitrary",)),
        name="final",
    )(pos3, pos3, h, route, gain.reshape(1, d), ys)


def _layer(h3, mix_gain, w_in, attn_gain, decay_f, decay_b, ret_gain, w_out, ffn_gain,
           w_rg, b_rg, w_re, b_re, w_eg, w_eu, w_ed, final_gain):
    b, s, d = h3.shape
    t = b * s
    tm = 512
    attn_qkv, (qr, kr, vr, gr) = _inproj(h3, mix_gain, w_in.astype(BF16), tm)

    o_list, lse_list = [], []
    for window, dilation in DILATED_BRANCHES:
        reach = (window // 2) // dilation
        o, lse = _attn_branch(*attn_qkv[dilation], dilation, reach, tq=128, tqb=512)
        o_list.append(o)
        lse_list.append(lse)

    lg_f = jnp.log1p(-jnp.exp2(decay_f.astype(F32)))
    lg_b = jnp.log1p(-jnp.exp2(decay_b.astype(F32)))
    orr = _retention(qr, kr, vr, gr, lg_f, lg_b, ret_gain).reshape(t, HEAD_GROUP_WIDTH)

    n_route = MOE_GROUPS + N_EXPERTS
    w_router = jnp.zeros((d, LANES), F32).at[:, :n_route].set(jnp.concatenate([w_rg, w_re], axis=1).astype(F32))
    b_router = jnp.zeros((1, LANES), F32).at[0, :n_route].set(jnp.concatenate([b_rg, b_re]).astype(F32))
    h, hn_packed, logits = _outproj(h3.reshape(t, d), o_list, lse_list, orr, attn_gain, w_out.astype(BF16),
                                    ffn_gain, w_router, b_router, tm)

    route, counts8 = _route(logits, tm)

    n_blocks = -(-TOP_K * t // MOE_BLOCK) + N_EXPERTS
    counts = counts8[0, :N_EXPERTS].astype(jnp.int32)
    aligned = ((counts + SEG_ALIGN - 1) // SEG_ALIGN) * SEG_ALIGN
    seg_start = jnp.cumsum(aligned) - aligned
    expert_iota = jnp.arange(N_EXPERTS, dtype=jnp.int32)
    e12 = route[:, ROUTE_E1:ROUTE_E2 + 1].astype(jnp.int32)
    r12 = route[:, ROUTE_R1:ROUTE_R2 + 1].astype(jnp.int32)
    pos = r12 + jnp.sum(jnp.where(e12[..., None] == expert_iota, seg_start, 0), axis=-1)
    nblk = (counts + MOE_BLOCK - 1) // MOE_BLOCK
    blk_end = jnp.cumsum(nblk)
    blk_start = blk_end - nblk
    blk = jnp.arange(n_blocks, dtype=jnp.int32)[:, None]
    owner = (blk >= blk_start) & (blk < blk_end)
    local = (blk - blk_start) * MOE_BLOCK
    block_row0 = jnp.sum(jnp.where(owner, seg_start + local, 0), axis=-1).astype(jnp.int32)
    block_nreal = jnp.sum(jnp.where(owner, jnp.clip(counts - local, 0, MOE_BLOCK), 0), axis=-1).astype(jnp.int32)
    block_expert = jnp.minimum(jnp.sum((blk >= blk_end).astype(jnp.int32), axis=-1), N_EXPERTS - 1)

    xs = _dispatch(hn_packed, pos, seg_start + counts, tm)
    ys = _experts(xs, block_expert, block_nreal, block_row0, w_eg.astype(BF16), w_eu.astype(BF16),
                  w_ed.astype(BF16))
    out = _final(h, ys, pos, route, final_gain, tm)
    return out.reshape(b, s, d)


def kernel(x, mix_norm_gain, w_in, attn_out_gain, ret_decay_fwd, ret_decay_bwd, ret_out_gain, w_out,
           ffn_norm_gain, w_route_group, b_route_group, w_route_expert, b_route_expert,
           w_expert_gate, w_expert_up, w_expert_down, final_norm_gain):
    depth = mix_norm_gain.shape[0]
    assert depth == 1, "the final rmsnorm is fused into the single layer's combine kernel"
    l = 0
    return _layer(x, mix_norm_gain[l], w_in[l], attn_out_gain[l], ret_decay_fwd[l], ret_decay_bwd[l],
                  ret_out_gain[l], w_out[l], ffn_norm_gain[l], w_route_group[l], b_route_group[l],
                  w_route_expert[l], b_route_expert[l], w_expert_gate[l], w_expert_up[l], w_expert_down[l],
                  final_norm_gain)
```

```python
import functools

import numpy as np
import jax
import jax.numpy as jnp
from jax import lax
from jax.experimental import pallas as pl
from jax.experimental.pallas import tpu as pltpu

F32 = jnp.float32
BF16 = jnp.bfloat16

ATTN_HEADS = 8
HEAD_DIM = 64
RET_HEADS = 8
HEAD_GROUP_WIDTH = 512
N_PROJ_GROUPS = 7
DILATED_BRANCHES = ((128, 1), (512, 4), (2048, 16))
ROPE_THETA = 500000.0
ROPE_DIM = HEAD_DIM // 4
RET_THETA = 10000.0
RET_CHUNK = 128
MOE_GROUPS = 4
EXPERTS_PER_GROUP = 8
N_EXPERTS = MOE_GROUPS * EXPERTS_PER_GROUP
MOE_BLOCK = 256
NORM_EPS = 1e-6
NEG_INF = -1e30

LANES = 128
VMEM_LIMIT = 56 * 1024 * 1024


def _cparams(sem):
    return pltpu.CompilerParams(dimension_semantics=sem, vmem_limit_bytes=VMEM_LIMIT)


def _rotary_tables(seq, half, freqs):
    pos = np.arange(seq, dtype=np.float64)[:, None]
    ang = pos * freqs[None, :].astype(np.float64)
    cos, sin = np.cos(ang), np.sin(ang)
    c = np.ones((seq, HEAD_DIM)); sp = np.zeros((seq, HEAD_DIM)); sm = np.zeros((seq, HEAD_DIM))
    c[:, :half] = cos; c[:, half:2 * half] = cos
    sp[:, half:2 * half] = sin
    sm[:, :half] = -sin
    rep = LANES // HEAD_DIM
    return tuple(jnp.asarray(np.tile(t, (1, rep)), F32) for t in (c, sp, sm))


def _rotate(t, c, sp, sm, half):
    outs = []
    for g in range(t.shape[1] // LANES):
        tg = t[:, g * LANES:(g + 1) * LANES]
        outs.append(tg * c + pltpu.roll(tg, half, 1) * sp + pltpu.roll(tg, LANES - half, 1) * sm)
    return jnp.concatenate(outs, axis=1)


CLASS_DILATIONS = tuple(d for _, d in DILATED_BRANCHES if d > 1)
ATTN_Q_SCALE = float(np.log2(np.e)) * HEAD_DIM ** -0.5


def _inproj_kernel(x_ref, gain_ref, w_ref, ca_ref, spa_ref, sma_ref, cr_ref, spr_ref, smr_ref, *rest):
    n_cls = len(CLASS_DILATIONS)
    nat_refs = rest[0:3]
    cls_refs = [rest[3 + 3 * c:6 + 3 * c] for c in range(n_cls)]
    qr_ref, kr_ref, vr_ref, gr_ref = rest[3 + 3 * n_cls:7 + 3 * n_cls]
    stage_ref = rest[7 + 3 * n_cls]
    x = x_ref[0]
    tm = x.shape[0]
    ms = jnp.mean(x * x, axis=-1, keepdims=True)
    xn = (x * lax.rsqrt(ms + NORM_EPS) * gain_ref[...]).astype(BF16)
    gw = HEAD_GROUP_WIDTH

    def proj(c):
        return jnp.dot(xn, w_ref[:, c * gw:(c + 1) * gw], preferred_element_type=F32)

    a_tabs = (ca_ref[...], spa_ref[...], sma_ref[...])
    r_tabs = (cr_ref[...], spr_ref[...], smr_ref[...])
    attn_vals = ((_rotate(proj(0), *a_tabs, ROPE_DIM // 2) * ATTN_Q_SCALE),
                 _rotate(proj(1), *a_tabs, ROPE_DIM // 2),
                 proj(2))
    for j, val in enumerate(attn_vals):
        nat_refs[j][0] = val.astype(BF16)
        for g in range(gw // LANES):
            stage_ref[g] = val[:, g * LANES:(g + 1) * LANES]
        for c, d in enumerate(CLASS_DILATIONS):
            for r in range(d):
                for g in range(gw // LANES):
                    col = r * gw + g * LANES
                    cls_refs[c][j][0, :, col:col + LANES] = (
                        stage_ref[g, pl.ds(r, tm // d, stride=d), :].astype(BF16))
    qr_ref[0] = _rotate(proj(3), *r_tabs, HEAD_DIM // 2).astype(BF16)
    kr_ref[0] = (_rotate(proj(4), *r_tabs, HEAD_DIM // 2) * (HEAD_DIM ** -0.5)).astype(BF16)
    vr_ref[0] = proj(5).astype(BF16)
    g = proj(6)
    gr_ref[0] = (g * jax.nn.sigmoid(g)).astype(BF16)


def _inproj(x, gain, w_in_bf16, tm):
    b, s, d = x.shape
    rope_freqs = ROPE_THETA ** (-np.arange(0, ROPE_DIM, 2, dtype=np.float32) / ROPE_DIM)
    ret_freqs = RET_THETA ** (-np.linspace(0.0, 1.0, HEAD_DIM // 2, dtype=np.float32))
    tabs = _rotary_tables(s, ROPE_DIM // 2, rope_freqs) + _rotary_tables(s, HEAD_DIM // 2, ret_freqs)
    gw = HEAD_GROUP_WIDTH
    tab_spec = pl.BlockSpec((tm, LANES), lambda si, bi: (si, 0))

    def view(dil):
        return (pl.BlockSpec((1, tm // dil, dil * gw), lambda si, bi: (bi, si, 0)),
                jax.ShapeDtypeStruct((b, s // dil, dil * gw), BF16))

    views = [view(1)] * 3 + [view(dil) for dil in CLASS_DILATIONS for _ in range(3)] + [view(1)] * 4
    outs = pl.pallas_call(
        _inproj_kernel,
        grid=(s // tm, b),
        in_specs=[pl.BlockSpec((1, tm, d), lambda si, bi: (bi, si, 0)),
                  pl.BlockSpec((1, d), lambda si, bi: (0, 0)),
                  pl.BlockSpec(w_in_bf16.shape, lambda si, bi: (0, 0))] + [tab_spec] * 6,
        out_specs=[v[0] for v in views],
        out_shape=[v[1] for v in views],
        scratch_shapes=[pltpu.VMEM((gw // LANES, tm, LANES), F32)],
        compiler_params=_cparams(("arbitrary", "arbitrary")),
        name="inproj",
    )(x, gain.reshape(1, d), w_in_bf16, *tabs)
    n_attn = 3 * (1 + len(CLASS_DILATIONS))
    attn_qkv = {dil: outs[3 * c:3 * c + 3] for c, dil in enumerate((1,) + CLASS_DILATIONS)}
    return attn_qkv, outs[n_attn:]


ATTN_WINDOW_CASES = 3


def _attn_kernel(q_ref, k_ref, v_ref, o_ref, lse_ref, bias_ref, *, length, tq, reach):
    tqb = q_ref.shape[1]
    win = tq + 2 * reach
    heads_per_pair = LANES // HEAD_DIM
    qi = pl.program_id(2)
    lane = lax.broadcasted_iota(jnp.int32, (1, LANES), 1)
    lane_t = lax.broadcasted_iota(jnp.int32, (tq, LANES), 1)
    ones = jnp.ones((win, LANES), BF16)

    @pl.when((pl.program_id(0) == 0) & (pl.program_id(1) == 0) & (qi == 0))
    def _():
        diff = (lax.broadcasted_iota(jnp.int32, (heads_per_pair * tq, win), 1)
                - (lax.broadcasted_iota(jnp.int32, (heads_per_pair * tq, win), 0) & (tq - 1)))
        for case in range(ATTN_WINDOW_CASES):
            off = case * reach
            bias_ref[case] = jnp.where((diff >= off - reach) & (diff <= off + reach), 0.0, NEG_INF)

    def sub(t, carry):
        q0 = qi * tqb + t * tq
        ws = jnp.clip(q0 - reach, 0, length - win)
        ws = pl.multiple_of(ws, reach)
        bias = bias_ref[(q0 - ws) // reach]
        rows = pl.ds(pl.multiple_of(t * tq, tq), tq)
        m_tile = jnp.zeros((tq, LANES), F32)
        l_tile = jnp.ones((tq, LANES), F32)
        for g in range(HEAD_GROUP_WIDTH // LANES):
            cols = slice(g * LANES, (g + 1) * LANES)
            qg = q_ref[0, rows, cols]
            kw = k_ref[0, pl.ds(ws, win), cols]
            v_ones = jnp.concatenate([v_ref[0, pl.ds(ws, win), cols], ones], axis=1)
            hms = [(lane >= hh * HEAD_DIM) & (lane < (hh + 1) * HEAD_DIM) for hh in range(heads_per_pair)]
            q2 = jnp.concatenate([qg * hm.astype(BF16) for hm in hms], axis=0)
            sc = lax.dot_general(q2, kw, (((1,), (1,)), ((), ())), preferred_element_type=F32) + bias
            m = jnp.max(sc, axis=-1, keepdims=True)
            p = jnp.exp2(sc - m).astype(BF16)
            pv = jnp.dot(p, v_ones, preferred_element_type=F32)
            l = pv[:, LANES:]
            o = pv[:, :LANES] / l
            o_pair = jnp.zeros((tq, LANES), F32)
            for hh in range(heads_per_pair):
                part = slice(hh * tq, (hh + 1) * tq)
                head_lane = lane_t == g * heads_per_pair + hh
                o_pair = jnp.where(hms[hh], o[part], o_pair)
                m_tile = jnp.where(head_lane, m[part], m_tile)
                l_tile = jnp.where(head_lane, l[part], l_tile)
            o_ref[0, rows, cols] = o_pair.astype(BF16)
        lse_ref[0, rows, :] = m_tile + jnp.log2(l_tile)
        return carry

    lax.fori_loop(0, tqb // tq, sub, 0, unroll=True)


def _attn_branch(qc, kc, vc, dilation, reach, tq, tqb):
    b, length, dw = qc.shape
    w = dw // dilation
    tqb = min(tqb, length)
    assert tq % reach == 0 and tq > reach and length % tq == 0 and length >= tq + 2 * reach
    q_spec = pl.BlockSpec((1, tqb, w), lambda bi, r, qi: (bi, qi, r))
    kv_spec = pl.BlockSpec((1, length, w), lambda bi, r, qi: (bi, 0, r))
    o, lse = pl.pallas_call(
        functools.partial(_attn_kernel, length=length, tq=tq, reach=reach),
        grid=(b, dilation, length // tqb),
        in_specs=[q_spec, kv_spec, kv_spec],
        out_specs=[q_spec, pl.BlockSpec((1, tqb, LANES), lambda bi, r, qi: (bi, qi, r))],
        out_shape=[jax.ShapeDtypeStruct((b, length, dilation * w), BF16),
                   jax.ShapeDtypeStruct((b, length, dilation * LANES), F32)],
        scratch_shapes=[pltpu.VMEM((ATTN_WINDOW_CASES, (LANES // HEAD_DIM) * tq, tq + 2 * reach), F32)],
        compiler_params=_cparams(("arbitrary", "arbitrary", "arbitrary")),
        name=f"attn_d{dilation}",
    )(qc, kc, vc)
    return o, lse


RET_TAB_QF, RET_TAB_QB, RET_TAB_KF, RET_TAB_KB = range(4)


def _retention_kernel(lgf_ref, lgb_ref, q_ref, k_ref, v_ref, g_ref, gain_ref, o_ref,
                      tab_ref, dec_ref, sb_ref, st_ref, *, chunk, unroll):
    c = chunk
    n = q_ref.shape[1] // c
    width = q_ref.shape[2]
    n_pairs = width // LANES
    heads_per_pair = LANES // HEAD_DIM
    n_heads = n_pairs * heads_per_pair
    head0 = pl.program_id(1) * n_heads
    lane_w = lax.broadcasted_iota(jnp.int32, (1, width), 1)
    lgf = [lgf_ref[head0 + hd] for hd in range(n_heads)]
    lgb = [lgb_ref[head0 + hd] for hd in range(n_heads)]
    lgf_lane = jnp.zeros((1, width), F32)
    lgb_lane = jnp.zeros((1, width), F32)
    for hd in range(n_heads):
        in_head = (lane_w >= hd * HEAD_DIM) & (lane_w < (hd + 1) * HEAD_DIM)
        lgf_lane = jnp.where(in_head, lgf[hd], lgf_lane)
        lgb_lane = jnp.where(in_head, lgb[hd], lgb_lane)
    idx = lax.broadcasted_iota(jnp.int32, (c, width), 0).astype(F32)
    tab_ref[RET_TAB_QF] = jnp.exp((idx + 1.0) * lgf_lane)
    tab_ref[RET_TAB_QB] = jnp.exp((c - idx) * lgb_lane)
    tab_ref[RET_TAB_KF] = jnp.exp((c - 1.0 - idx) * lgf_lane)
    tab_ref[RET_TAB_KB] = jnp.exp(idx * lgb_lane)
    sdf = jnp.exp(c * lgf_lane)
    sdb = jnp.exp(c * lgb_lane)
    dmat = (lax.broadcasted_iota(jnp.int32, (c, c), 0)
            - lax.broadcasted_iota(jnp.int32, (c, c), 1)).astype(F32)
    for hd in range(n_heads):
        dec_ref[hd // heads_per_pair, :, (hd % heads_per_pair) * c:(hd % heads_per_pair + 1) * c] = (
            jnp.where(dmat >= 0, jnp.exp(dmat * lgf[hd]), jnp.exp(-dmat * lgb[hd])))
    lane = lax.broadcasted_iota(jnp.int32, (1, LANES), 1)
    lane_hi = lane >= HEAD_DIM
    head_masks = [((lane >= hh * HEAD_DIM) & (lane < (hh + 1) * HEAD_DIM)).astype(BF16)
                  for hh in range(heads_per_pair)]
    row_hi = lax.broadcasted_iota(jnp.int32, (LANES, LANES), 0) >= HEAD_DIM
    col_hi = lax.broadcasted_iota(jnp.int32, (LANES, LANES), 1) >= HEAD_DIM
    blockdiag = row_hi == col_hi

    def kv_state(kd, vv):
        kt = jnp.transpose(kd).astype(BF16)
        return jnp.where(blockdiag, jnp.dot(kt, vv, preferred_element_type=F32), 0.0)

    st_ref[...] = jnp.zeros_like(st_ref)
    sb_ref[n - 1] = jnp.zeros(sb_ref.shape[1:], sb_ref.dtype)

    def back(i, carry):
        nn = n - 1 - i
        rows = pl.ds(pl.multiple_of(nn * c, c), c)
        for p in range(n_pairs):
            cols = slice(p * LANES, (p + 1) * LANES)
            kd = k_ref[0, rows, cols].astype(F32) * tab_ref[RET_TAB_KB, :, cols]
            new = st_ref[p] * sdb[:, cols] + kv_state(kd, v_ref[0, rows, cols])
            st_ref[p] = new
            sb_ref[nn - 1, p] = new.astype(BF16)
        return carry

    lax.fori_loop(0, n - 1, back, 0, unroll=unroll)

    st_ref[...] = jnp.zeros_like(st_ref)

    def fwd(nn, carry):
        rows = pl.ds(pl.multiple_of(nn * c, c), c)
        for p in range(n_pairs):
            cols = slice(p * LANES, (p + 1) * LANES)
            qq = q_ref[0, rows, cols]
            kk = k_ref[0, rows, cols]
            vv = v_ref[0, rows, cols]
            qf = qq.astype(F32)
            sf = st_ref[p]
            qcat = jnp.concatenate([(qf * tab_ref[RET_TAB_QF, :, cols]).astype(BF16),
                                    (qf * tab_ref[RET_TAB_QB, :, cols]).astype(BF16)], axis=1)
            scat = jnp.concatenate([sf.astype(BF16), sb_ref[nn, p]], axis=0)
            o = jnp.dot(qcat, scat, preferred_element_type=F32)
            k2 = jnp.concatenate([kk * hm for hm in head_masks], axis=0)
            v2 = jnp.concatenate([vv * hm for hm in head_masks], axis=0)
            sc = lax.dot_general(qq, k2, (((1,), (1,)), ((), ())), preferred_element_type=F32)
            o = o + jnp.dot((sc * dec_ref[p]).astype(BF16), v2, preferred_element_type=F32)
            s_lo = jnp.sum(jnp.where(lane_hi, 0.0, o), axis=-1, keepdims=True)
            s_hi = jnp.sum(jnp.where(lane_hi, o, 0.0), axis=-1, keepdims=True)
            mu = jnp.where(lane_hi, s_hi, s_lo) * (1.0 / HEAD_DIM)
            dev = o - mu
            d2 = dev * dev
            v_lo = jnp.sum(jnp.where(lane_hi, 0.0, d2), axis=-1, keepdims=True)
            v_hi = jnp.sum(jnp.where(lane_hi, d2, 0.0), axis=-1, keepdims=True)
            var = jnp.where(lane_hi, v_hi, v_lo) * (1.0 / HEAD_DIM)
            out = dev * lax.rsqrt(var + NORM_EPS) * gain_ref[:, cols] * g_ref[0, rows, cols].astype(F32)
            o_ref[0, rows, cols] = out.astype(BF16)
            kd = kk.astype(F32) * tab_ref[RET_TAB_KF, :, cols]
            st_ref[p] = sf * sdf[:, cols] + kv_state(kd, vv)
        return carry

    lax.fori_loop(0, n, fwd, 0, unroll=unroll)


def _retention(qr, kr, vr, gate, lg_f, lg_b, out_gain, width=512, unroll=2):
    b, s, w = qr.shape
    n_pairs = width // LANES
    n_heads = width // HEAD_DIM
    spec = pl.BlockSpec((1, s, width), lambda bi, p, *_: (bi, 0, p))
    grid_spec = pltpu.PrefetchScalarGridSpec(
        num_scalar_prefetch=2,
        grid=(b, w // width),
        in_specs=[spec, spec, spec, spec, pl.BlockSpec((1, width), lambda bi, p, *_: (0, p))],
        out_specs=spec,
        scratch_shapes=[pltpu.VMEM((4, RET_CHUNK, width), F32),
                        pltpu.VMEM((n_pairs, RET_CHUNK, (LANES // HEAD_DIM) * RET_CHUNK), F32),
                        pltpu.VMEM((s // RET_CHUNK, n_pairs, LANES, LANES), BF16),
                        pltpu.VMEM((n_pairs, LANES, LANES), F32)],
    )
    return pl.pallas_call(
        functools.partial(_retention_kernel, chunk=RET_CHUNK, unroll=unroll),
        grid_spec=grid_spec,
        out_shape=jax.ShapeDtypeStruct((b, s, w), BF16),
        compiler_params=_cparams(("arbitrary", "arbitrary")),
        name="retention",
    )(lg_f, lg_b, qr, kr, vr, gate, out_gain.reshape(1, w))


def _split_bf16(t):
    hi = t.astype(BF16)
    lo = (t - hi.astype(F32)).astype(BF16)
    return hi, lo


def _pack_bf16_pairs(t):
    n = t.shape[1] // 2
    hi = pltpu.bitcast(t[:, :n].astype(BF16).astype(F32), jnp.uint32)
    lo = pltpu.bitcast(t[:, n:].astype(BF16).astype(F32), jnp.uint32)
    return hi | (lo >> 16)


def _unpack_bf16_pairs(u):
    hi = pltpu.bitcast(u & jnp.uint32(0xFFFF0000), F32)
    lo = pltpu.bitcast(u << 16, F32)
    return jnp.concatenate([hi, lo], axis=1)


def _outproj_kernel(x_ref, o1_ref, o2_ref, o3_ref, l1_ref, l2_ref, l3_ref, orr_ref, ga_ref, expand_ref,
                    wout_ref, gf_ref, wr_hi_ref, wr_lo_ref, br_ref, h_ref, hn_ref, logit_ref,
                    *nat_refs):
    tm = x_ref.shape[0]
    gw = HEAD_GROUP_WIDTH
    os, ls = [], []
    for (_, dil), o_ref, l_ref in zip(DILATED_BRANCHES, (o1_ref, o2_ref, o3_ref), (l1_ref, l2_ref, l3_ref)):
        if dil == 1:
            os.append(o_ref[...].astype(F32))
            ls.append(l_ref[...])
            continue
        c = CLASS_DILATIONS.index(dil)
        onat_ref, lnat_ref = nat_refs[2 * c], nat_refs[2 * c + 1]
        for r in range(dil):
            rows = pl.ds(r, tm // dil, stride=dil)
            for g in range(gw // LANES):
                col = r * gw + g * LANES
                onat_ref[g, rows, :] = o_ref[:, col:col + LANES].astype(F32)
            lnat_ref[rows, :] = l_ref[:, r * LANES:(r + 1) * LANES]
        os.append(jnp.concatenate([onat_ref[g] for g in range(gw // LANES)], axis=1))
        ls.append(lnat_ref[...])
    mx = jnp.maximum(jnp.maximum(ls[0], ls[1]), ls[2])
    es = [jnp.exp2(l - mx) for l in ls]
    inv = 1.0 / (es[0] + es[1] + es[2])
    expand = expand_ref[...]
    oa = jnp.zeros((tm, gw), F32)
    for e, o in zip(es, os):
        w_hi, w_lo = _split_bf16(e * inv)
        wexp = (jnp.dot(w_hi, expand, preferred_element_type=F32)
                + jnp.dot(w_lo, expand, preferred_element_type=F32))
        oa = oa + wexp * o
    oa = oa * lax.rsqrt(jnp.mean(oa * oa, axis=-1, keepdims=True) + NORM_EPS) * ga_ref[...]
    mixed = jnp.concatenate([oa.astype(BF16), orr_ref[...]], axis=1)
    h = x_ref[...] + jnp.dot(mixed, wout_ref[...], preferred_element_type=F32)
    h_ref[...] = h
    hn = h * lax.rsqrt(jnp.mean(h * h, axis=-1, keepdims=True) + NORM_EPS) * gf_ref[...]
    hn_ref[...] = _pack_bf16_pairs(hn)
    hn_hi, hn_lo = _split_bf16(hn)
    wr_hi = wr_hi_ref[...]
    logit_ref[...] = (jnp.dot(hn_hi, wr_hi, preferred_element_type=F32)
                      + jnp.dot(hn_lo, wr_hi, preferred_element_type=F32)
                      + jnp.dot(hn_hi, wr_lo_ref[...], preferred_element_type=F32)
                      + br_ref[...])


def _outproj(x2, o_list, lse_list, orr, attn_gain, w_out_bf16, ffn_gain, w_router, b_router, tm):
    t, d = x2.shape
    w = HEAD_GROUP_WIDTH
    expand = np.zeros((LANES, w), np.float32)
    for hd in range(ATTN_HEADS):
        expand[hd, hd * HEAD_DIM:(hd + 1) * HEAD_DIM] = 1.0
    expand = jnp.asarray(expand, BF16)
    wr_hi = w_router.astype(BF16)
    wr_lo = (w_router - wr_hi.astype(F32)).astype(BF16)
    row = lambda width, dil=1: pl.BlockSpec((tm // dil, dil * width), lambda i: (i, 0))
    full = lambda a: pl.BlockSpec(a.shape, lambda i: (0,) * a.ndim)
    ga = attn_gain.reshape(1, w)
    gf = ffn_gain.reshape(1, d)
    dils = [dil for _, dil in DILATED_BRANCHES]
    o_flat = [o.reshape(t // dil, dil * w) for o, dil in zip(o_list, dils)]
    l_flat = [l.reshape(t // dil, dil * LANES) for l, dil in zip(lse_list, dils)]
    nat_scratch = []
    for _ in CLASS_DILATIONS:
        nat_scratch += [pltpu.VMEM((w // LANES, tm, LANES), F32), pltpu.VMEM((tm, LANES), F32)]
    return pl.pallas_call(
        _outproj_kernel,
        grid=(t // tm,),
        in_specs=[row(d)] + [row(w, dil) for dil in dils] + [row(LANES, dil) for dil in dils] + [row(w)]
                 + [full(ga), full(expand), full(w_out_bf16), full(gf), full(wr_hi), full(wr_lo), full(b_router)],
        out_specs=[row(d), row(d // 2), row(LANES)],
        out_shape=[jax.ShapeDtypeStruct((t, d), F32),
                   jax.ShapeDtypeStruct((t, d // 2), jnp.uint32),
                   jax.ShapeDtypeStruct((t, LANES), F32)],
        scratch_shapes=nat_scratch,
        compiler_params=_cparams(("arbitrary",)),
        name="outproj",
    )(x2, *o_flat, *l_flat, orr, ga, expand, w_out_bf16, gf, wr_hi, wr_lo, b_router)


ROUTE_E1, ROUTE_E2, ROUTE_G1, ROUTE_G2, ROUTE_R1, ROUTE_R2 = range(6)
EXPERT_LANE0 = MOE_GROUPS


def _route_kernel(logit_ref, tri_ref, route_ref, count_ref, run_ref):
    @pl.when(pl.program_id(0) == 0)
    def _():
        run_ref[...] = jnp.zeros_like(run_ref)

    lg = logit_ref[...]
    tm = lg.shape[0]
    lane = lax.broadcasted_iota(jnp.int32, lg.shape, 1)
    big = jnp.int32(1 << 20)

    def top(vals):
        m = jnp.max(vals, axis=-1, keepdims=True)
        i = jnp.min(jnp.where(vals == m, lane, big), axis=-1, keepdims=True)
        return m, i

    gl = jnp.where(lane < MOE_GROUPS, lg, -jnp.inf)
    gmax, gidx = top(gl)
    group_gate = 1.0 / jnp.sum(jnp.exp(gl - gmax), axis=-1, keepdims=True)
    lo = EXPERT_LANE0 + gidx * EXPERTS_PER_GROUP
    el = jnp.where((lane >= lo) & (lane < lo + EXPERTS_PER_GROUP), lg, -jnp.inf)
    t1, i1 = top(el)
    t2, i2 = top(jnp.where(lane == i1, -jnp.inf, el))
    e21 = jnp.exp(t2 - t1)
    g1 = group_gate / (1.0 + e21)
    g2 = group_gate * e21 / (1.0 + e21)
    e1 = i1 - EXPERT_LANE0
    e2 = i2 - EXPERT_LANE0
    oh1 = lane == e1
    oh2 = lane == e2
    cnt = oh1.astype(F32) + oh2.astype(F32)
    prefix = jnp.dot(tri_ref[...], cnt.astype(BF16), preferred_element_type=F32) + run_ref[0:1, :]
    r1 = jnp.sum(jnp.where(oh1, prefix, 0.0), axis=-1, keepdims=True)
    r2 = jnp.sum(jnp.where(oh2, prefix, 0.0), axis=-1, keepdims=True)
    new_run = run_ref[0:1, :] + jnp.sum(cnt, axis=0, keepdims=True)
    run_ref[...] = jnp.broadcast_to(new_run, run_ref.shape)
    count_ref[...] = jnp.broadcast_to(new_run, count_ref.shape)
    out = jnp.zeros(lg.shape, F32)
    for ln, val in ((ROUTE_E1, e1.astype(F32)), (ROUTE_E2, e2.astype(F32)), (ROUTE_G1, g1),
                    (ROUTE_G2, g2), (ROUTE_R1, r1), (ROUTE_R2, r2)):
        out = jnp.where(lane == ln, val, out)
    route_ref[...] = out


def _route(logits, tm):
    t = logits.shape[0]
    tri = jnp.asarray(np.tril(np.ones((tm, tm), np.float32), -1), BF16)
    return pl.pallas_call(
        _route_kernel,
        grid=(t // tm,),
        in_specs=[pl.BlockSpec((tm, LANES), lambda i: (i, 0)), pl.BlockSpec((tm, tm), lambda i: (0, 0))],
        out_specs=[pl.BlockSpec((tm, LANES), lambda i: (i, 0)), pl.BlockSpec((8, LANES), lambda i: (0, 0))],
        out_shape=[jax.ShapeDtypeStruct((t, LANES), F32), jax.ShapeDtypeStruct((8, LANES), F32)],
        scratch_shapes=[pltpu.VMEM((8, LANES), F32)],
        compiler_params=_cparams(("arbitrary",)),
        name="route",
    )(logits, tri)


TOP_K = 2
DMA_ISSUE_UNROLL = 8
DMA_PRIORITIES = 2


SEG_ALIGN = 8
SORTED_TAIL = N_EXPERTS * SEG_ALIGN + MOE_BLOCK


def _dispatch_kernel(seg_end_ref, pos_ref, hn_ref, xs_hbm, zbuf, sem, zsem, *, n_rows):
    tm = hn_ref.shape[0]

    @pl.when(pl.program_id(0) == 0)
    def _():
        zbuf[...] = jnp.zeros_like(zbuf)
        tail = pltpu.make_async_copy(zbuf, xs_hbm.at[pl.ds(n_rows, SORTED_TAIL)], zsem)
        tail.start()
        tail.wait()

        def hole(e):
            end = seg_end_ref[e]
            start = pl.multiple_of(end - (end & (SEG_ALIGN - 1)), SEG_ALIGN)
            return pltpu.make_async_copy(zbuf.at[pl.ds(0, SEG_ALIGN)], xs_hbm.at[pl.ds(start, SEG_ALIGN)], zsem)

        for e in range(N_EXPERTS):
            @pl.when((seg_end_ref[e] & (SEG_ALIGN - 1)) != 0)
            def _():
                hole(e).start()
        for e in range(N_EXPERTS):
            @pl.when((seg_end_ref[e] & (SEG_ALIGN - 1)) != 0)
            def _():
                hole(e).wait()

    def body(j, c):
        for k in range(TOP_K):
            pltpu.make_async_copy(hn_ref.at[pl.ds(j, 1)], xs_hbm.at[pl.ds(pos_ref[0, 0, TOP_K * j + k], 1)],
                                  sem).start(priority=k % DMA_PRIORITIES)
        return c
    lax.fori_loop(0, tm, body, 0, unroll=DMA_ISSUE_UNROLL)

    tile = pltpu.make_async_copy(hn_ref, xs_hbm.at[pl.ds(0, tm)], sem)
    for k in range(TOP_K):
        tile.wait()


def _dispatch(hn_packed, pos, seg_end, tm):
    t, dp = hn_packed.shape
    n_rows = TOP_K * t
    pos3 = pos.reshape(t // tm, 1, TOP_K * tm)
    grid_spec = pltpu.PrefetchScalarGridSpec(
        num_scalar_prefetch=1,
        grid=(t // tm,),
        in_specs=[pl.BlockSpec((1, 1, TOP_K * tm), lambda i, se: (i, 0, 0), memory_space=pltpu.SMEM),
                  pl.BlockSpec((tm, dp), lambda i, se: (i, 0))],
        out_specs=pl.BlockSpec(memory_space=pl.ANY),
        scratch_shapes=[pltpu.VMEM((SORTED_TAIL, dp), jnp.uint32),
                        pltpu.SemaphoreType.DMA,
                        pltpu.SemaphoreType.DMA],
    )
    return pl.pallas_call(
        functools.partial(_dispatch_kernel, n_rows=n_rows),
        grid_spec=grid_spec,
        out_shape=jax.ShapeDtypeStruct((n_rows + SORTED_TAIL, dp), jnp.uint32),
        compiler_params=_cparams(("arbitrary",)),
        name="dispatch",
    )(seg_end, pos3, hn_packed)


def _expert_kernel(bexp_ref, nreal_ref, row0_ref, xs_hbm, wg_ref, wu_ref, wd_ref, ys_hbm,
                   xbuf, ybuf, isem, osem, *, n_rows):
    del bexp_ref
    i = pl.program_id(0)
    nb = pl.num_programs(0)
    slot = i % 2
    nslot = 1 - slot
    n_cur = nreal_ref[i]
    prev = jnp.maximum(i - 1, 0)
    nxt = jnp.minimum(i + 1, nb - 1)

    def in_copy(blk, s):
        row0 = pl.multiple_of(row0_ref[blk], SEG_ALIGN)
        return pltpu.make_async_copy(xs_hbm.at[pl.ds(row0, MOE_BLOCK)], xbuf.at[s], isem.at[s])

    def out_copy(blk, s):
        row0 = pl.multiple_of(row0_ref[blk], SEG_ALIGN)
        return pltpu.make_async_copy(ybuf.at[s], ys_hbm.at[pl.ds(row0, MOE_BLOCK)], osem.at[s])

    @pl.when(i == 0)
    def _():
        ybuf[...] = jnp.zeros_like(ybuf)
        tails = [pltpu.make_async_copy(ybuf.at[s], ys_hbm.at[pl.ds(n_rows + s * MOE_BLOCK, MOE_BLOCK)], osem.at[s])
                 for s in range(SORTED_TAIL // MOE_BLOCK)]
        for tail in tails:
            tail.start()
        for tail in tails:
            tail.wait()

        @pl.when(n_cur > 0)
        def _():
            in_copy(i, slot).start()

    @pl.when((i + 1 < nb) & (nreal_ref[nxt] > 0))
    def _():
        in_copy(nxt, nslot).start()

    @pl.when(n_cur > 0)
    def _():
        in_copy(i, slot).wait()
        xb = _unpack_bf16_pairs(xbuf[slot]).astype(BF16)
        gate = jnp.dot(xb, wg_ref[0], preferred_element_type=F32)
        up = jnp.dot(xb, wu_ref[0], preferred_element_type=F32)
        hid = (gate * jax.nn.sigmoid(gate) * up).astype(BF16)
        ybuf[slot] = _pack_bf16_pairs(jnp.dot(hid, wd_ref[0], preferred_element_type=F32))

    @pl.when((i >= 1) & (nreal_ref[prev] > 0))
    def _():
        out_copy(prev, nslot).wait()

    @pl.when(n_cur > 0)
    def _():
        out_copy(i, slot).start()

        @pl.when(i == nb - 1)
        def _():
            out_copy(i, slot).wait()


def _experts(xs, block_expert, block_nreal, block_row0, wg, wu, wd):
    n_blocks = block_expert.shape[0]
    assert n_blocks >= 2 and SORTED_TAIL == 2 * MOE_BLOCK
    n_rows_pad, dp = xs.shape
    _, d, ff = wg.shape
    grid_spec = pltpu.PrefetchScalarGridSpec(
        num_scalar_prefetch=3,
        grid=(n_blocks,),
        in_specs=[pl.BlockSpec(memory_space=pl.ANY),
                  pl.BlockSpec((1, d, ff), lambda i, be, nr, r0: (be[i], 0, 0)),
                  pl.BlockSpec((1, d, ff), lambda i, be, nr, r0: (be[i], 0, 0)),
                  pl.BlockSpec((1, ff, d), lambda i, be, nr, r0: (be[i], 0, 0))],
        out_specs=pl.BlockSpec(memory_space=pl.ANY),
        scratch_shapes=[pltpu.VMEM((2, MOE_BLOCK, dp), jnp.uint32),
                        pltpu.VMEM((2, MOE_BLOCK, dp), jnp.uint32),
                        pltpu.SemaphoreType.DMA((2,)),
                        pltpu.SemaphoreType.DMA((2,))],
    )
    return pl.pallas_call(
        functools.partial(_expert_kernel, n_rows=n_rows_pad - SORTED_TAIL),
        grid_spec=grid_spec,
        out_shape=jax.ShapeDtypeStruct((n_rows_pad, dp), jnp.uint32),
        compiler_params=_cparams(("arbitrary",)),
        name="experts",
    )(block_expert, block_nreal, block_row0, xs, wg, wu, wd)


def _final_kernel(pos_ref, pos_next_ref, h_ref, route_ref, gain_ref, ys_hbm, out_ref, ybuf, sem):
    i = pl.program_id(0)
    nt = pl.num_programs(0)
    slot = i % 2
    tm = h_ref.shape[0]

    def gather_start(idx_ref, s):
        def body(j, c):
            for k in range(TOP_K):
                pltpu.make_async_copy(ys_hbm.at[pl.ds(idx_ref[0, 0, TOP_K * j + k], 1)],
                                      ybuf.at[s, k, pl.ds(j, 1)], sem.at[s]).start(priority=k % DMA_PRIORITIES)
            return c
        lax.fori_loop(0, tm, body, 0, unroll=DMA_ISSUE_UNROLL)

    @pl.when(i == 0)
    def _():
        gather_start(pos_ref, slot)

    @pl.when(i + 1 < nt)
    def _():
        gather_start(pos_next_ref, 1 - slot)

    for k in range(TOP_K):
        pltpu.make_async_copy(ys_hbm.at[pl.ds(0, tm)], ybuf.at[slot, k], sem.at[slot]).wait()

    r = route_ref[...]
    y = h_ref[...]
    for k, gate_lane in enumerate((ROUTE_G1, ROUTE_G2)):
        y = y + r[:, gate_lane:gate_lane + 1] * _unpack_bf16_pairs(ybuf[slot, k])
    out_ref[...] = y * lax.rsqrt(jnp.mean(y * y, axis=-1, keepdims=True) + NORM_EPS) * gain_ref[...]


def _final(h, ys, pos, route, gain, tm):
    t, d = h.shape
    nt = t // tm
    pos3 = pos.reshape(nt, 1, TOP_K * tm)
    smem_blk = lambda imap: pl.BlockSpec((1, 1, TOP_K * tm), imap, memory_space=pltpu.SMEM)
    return pl.pallas_call(
        _final_kernel,
        grid=(nt,),
        in_specs=[smem_blk(lambda i: (i, 0, 0)),
                  smem_blk(lambda i: (jnp.minimum(i + 1, nt - 1), 0, 0)),
                  pl.BlockSpec((tm, d), lambda i: (i, 0)),
                  pl.BlockSpec((tm, LANES), lambda i: (i, 0)),
                  pl.BlockSpec((1, d), lambda i: (0, 0)),
                  pl.BlockSpec(memory_space=pl.ANY)],
        out_specs=pl.BlockSpec((tm, d), lambda i: (i, 0)),
        out_shape=jax.ShapeDtypeStruct((t, d), F32),
        scratch_shapes=[pltpu.VMEM((2, TOP_K, tm, d // 2), jnp.uint32),
                        pltpu.SemaphoreType.DMA((2,))],
        compiler_params=_cparams(("arbitrary",)),
        name="final",
    )(pos3, pos3, h, route, gain.reshape(1, d), ys)


def _layer(h3, mix_gain, w_in, attn_gain, decay_f, decay_b, ret_gain, w_out, ffn_gain,
           w_rg, b_rg, w_re, b_re, w_eg, w_eu, w_ed, final_gain):
    b, s, d = h3.shape
    t = b * s
    tm = 512
    attn_qkv, (qr, kr, vr, gr) = _inproj(h3, mix_gain, w_in.astype(BF16), tm)

    o_list, lse_list = [], []
    for window, dilation in DILATED_BRANCHES:
        reach = (window // 2) // dilation
        o, lse = _attn_branch(*attn_qkv[dilation], dilation, reach, tq=128, tqb=512)
        o_list.append(o)
        lse_list.append(lse)

    lg_f = jnp.log1p(-jnp.exp2(decay_f.astype(F32)))
    lg_b = jnp.log1p(-jnp.exp2(decay_b.astype(F32)))
    orr = _retention(qr, kr, vr, gr, lg_f, lg_b, ret_gain).reshape(t, HEAD_GROUP_WIDTH)

    n_route = MOE_GROUPS + N_EXPERTS
    w_router = jnp.zeros((d, LANES), F32).at[:, :n_route].set(jnp.concatenate([w_rg, w_re], axis=1).astype(F32))
    b_router = jnp.zeros((1, LANES), F32).at[0, :n_route].set(jnp.concatenate([b_rg, b_re]).astype(F32))
    h, hn_packed, logits = _outproj(h3.reshape(t, d), o_list, lse_list, orr, attn_gain, w_out.astype(BF16),
                                    ffn_gain, w_router, b_router, tm)

    route, counts8 = _route(logits, tm)

    n_blocks = -(-TOP_K * t // MOE_BLOCK) + N_EXPERTS
    counts = counts8[0, :N_EXPERTS].astype(jnp.int32)
    aligned = ((counts + SEG_ALIGN - 1) // SEG_ALIGN) * SEG_ALIGN
    seg_start = jnp.cumsum(aligned) - aligned
    expert_iota = jnp.arange(N_EXPERTS, dtype=jnp.int32)
    e12 = route[:, ROUTE_E1:ROUTE_E2 + 1].astype(jnp.int32)
    r12 = route[:, ROUTE_R1:ROUTE_R2 + 1].astype(jnp.int32)
    pos = r12 + jnp.sum(jnp.where(e12[..., None] == expert_iota, seg_start, 0), axis=-1)
    nblk = (counts + MOE_BLOCK - 1) // MOE_BLOCK
    blk_end = jnp.cumsum(nblk)
    blk_start = blk_end - nblk
    blk = jnp.arange(n_blocks, dtype=jnp.int32)[:, None]
    owner = (blk >= blk_start) & (blk < blk_end)
    local = (blk - blk_start) * MOE_BLOCK
    block_row0 = jnp.sum(jnp.where(owner, seg_start + local, 0), axis=-1).astype(jnp.int32)
    block_nreal = jnp.sum(jnp.where(owner, jnp.clip(counts - local, 0, MOE_BLOCK), 0), axis=-1).astype(jnp.int32)
    block_expert = jnp.minimum(jnp.sum((blk >= blk_end).astype(jnp.int32), axis=-1), N_EXPERTS - 1)

    xs = _dispatch(hn_packed, pos, seg_start + counts, tm)
    ys = _experts(xs, block_expert, block_nreal, block_row0, w_eg.astype(BF16), w_eu.astype(BF16),
                  w_ed.astype(BF16))
    out = _final(h, ys, pos, route, final_gain, tm)
    return out.reshape(b, s, d)


def kernel(x, mix_norm_gain, w_in, attn_out_gain, ret_decay_fwd, ret_decay_bwd, ret_out_gain, w_out,
           ffn_norm_gain, w_route_group, b_route_group, w_route_expert, b_route_expert,
           w_expert_gate, w_expert_up, w_expert_down, final_norm_gain):
    depth = mix_norm_gain.shape[0]
    assert depth == 1, "the final rmsnorm is fused into the single layer's combine kernel"
    l = 0
    return _layer(x, mix_norm_gain[l], w_in[l], attn_out_gain[l], ret_decay_fwd[l], ret_decay_bwd[l],
                  ret_out_gain[l], w_out[l], ffn_norm_gain[l], w_route_group[l], b_route_group[l],
                  w_route_expert[l], b_route_expert[l], w_expert_gate[l], w_expert_up[l], w_expert_down[l],
                  final_norm_gain)
```

```python
import functools

import numpy as np
import jax
import jax.numpy as jnp
from jax import lax
from jax.experimental import pallas as pl
from jax.experimental.pallas import tpu as pltpu

F32 = jnp.float32
BF16 = jnp.bfloat16

ATTN_HEADS = 8
HEAD_DIM = 64
RET_HEADS = 8
HEAD_GROUP_WIDTH = 512
N_PROJ_GROUPS = 7
DILATED_BRANCHES = ((128, 1), (512, 4), (2048, 16))
ROPE_THETA = 500000.0
ROPE_DIM = HEAD_DIM // 4
RET_THETA = 10000.0
RET_CHUNK = 128
MOE_GROUPS = 4
EXPERTS_PER_GROUP = 8
N_EXPERTS = MOE_GROUPS * EXPERTS_PER_GROUP
MOE_BLOCK = 512
NORM_EPS = 1e-6
NEG_INF = -1e30

LANES = 128
VMEM_LIMIT = 56 * 1024 * 1024


def _cparams(sem):
    return pltpu.CompilerParams(dimension_semantics=sem, vmem_limit_bytes=VMEM_LIMIT)


def _rotary_tables(seq, half, freqs):
    pos = np.arange(seq, dtype=np.float64)[:, None]
    ang = pos * freqs[None, :].astype(np.float64)
    cos, sin = np.cos(ang), np.sin(ang)
    c = np.ones((seq, HEAD_DIM)); sp = np.zeros((seq, HEAD_DIM)); sm = np.zeros((seq, HEAD_DIM))
    c[:, :half] = cos; c[:, half:2 * half] = cos
    sp[:, half:2 * half] = sin
    sm[:, :half] = -sin
    rep = LANES // HEAD_DIM
    return tuple(jnp.asarray(np.tile(t, (1, rep)), F32) for t in (c, sp, sm))


def _rotate(t, c, sp, sm, half):
    outs = []
    for g in range(t.shape[1] // LANES):
        tg = t[:, g * LANES:(g + 1) * LANES]
        outs.append(tg * c + pltpu.roll(tg, half, 1) * sp + pltpu.roll(tg, LANES - half, 1) * sm)
    return jnp.concatenate(outs, axis=1)


CLASS_DILATIONS = tuple(d for _, d in DILATED_BRANCHES if d > 1)
ATTN_Q_SCALE = float(np.log2(np.e)) * HEAD_DIM ** -0.5


def _inproj_kernel(x_ref, gain_ref, w_ref, ca_ref, spa_ref, sma_ref, cr_ref, spr_ref, smr_ref, *rest):
    n_cls = len(CLASS_DILATIONS)
    nat_refs = rest[0:3]
    cls_refs = [rest[3 + 3 * c:6 + 3 * c] for c in range(n_cls)]
    qr_ref, kr_ref, vr_ref, gr_ref = rest[3 + 3 * n_cls:7 + 3 * n_cls]
    stage_ref = rest[7 + 3 * n_cls]
    x = x_ref[0]
    tm = x.shape[0]
    ms = jnp.mean(x * x, axis=-1, keepdims=True)
    xn = (x * lax.rsqrt(ms + NORM_EPS) * gain_ref[...]).astype(BF16)
    gw = HEAD_GROUP_WIDTH

    def proj(c):
        return jnp.dot(xn, w_ref[:, c * gw:(c + 1) * gw], preferred_element_type=F32)

    a_tabs = (ca_ref[...], spa_ref[...], sma_ref[...])
    r_tabs = (cr_ref[...], spr_ref[...], smr_ref[...])
    attn_vals = ((_rotate(proj(0), *a_tabs, ROPE_DIM // 2) * ATTN_Q_SCALE),
                 _rotate(proj(1), *a_tabs, ROPE_DIM // 2),
                 proj(2))
    for j, val in enumerate(attn_vals):
        nat_refs[j][0] = val.astype(BF16)
        for g in range(gw // LANES):
            stage_ref[g] = val[:, g * LANES:(g + 1) * LANES]
        for c, d in enumerate(CLASS_DILATIONS):
            for r in range(d):
                for g in range(gw // LANES):
                    col = r * gw + g * LANES
                    cls_refs[c][j][0, :, col:col + LANES] = (
                        stage_ref[g, pl.ds(r, tm // d, stride=d), :].astype(BF16))
    qr_ref[0] = _rotate(proj(3), *r_tabs, HEAD_DIM // 2).astype(BF16)
    kr_ref[0] = (_rotate(proj(4), *r_tabs, HEAD_DIM // 2) * (HEAD_DIM ** -0.5)).astype(BF16)
    vr_ref[0] = proj(5).astype(BF16)
    g = proj(6)
    gr_ref[0] = (g * jax.nn.sigmoid(g)).astype(BF16)


def _inproj(x, gain, w_in_bf16, tm):
    b, s, d = x.shape
    rope_freqs = ROPE_THETA ** (-np.arange(0, ROPE_DIM, 2, dtype=np.float32) / ROPE_DIM)
    ret_freqs = RET_THETA ** (-np.linspace(0.0, 1.0, HEAD_DIM // 2, dtype=np.float32))
    tabs = _rotary_tables(s, ROPE_DIM // 2, rope_freqs) + _rotary_tables(s, HEAD_DIM // 2, ret_freqs)
    gw = HEAD_GROUP_WIDTH
    tab_spec = pl.BlockSpec((tm, LANES), lambda si, bi: (si, 0))

    def view(dil):
        return (pl.BlockSpec((1, tm // dil, dil * gw), lambda si, bi: (bi, si, 0)),
                jax.ShapeDtypeStruct((b, s // dil, dil * gw), BF16))

    views = [view(1)] * 3 + [view(dil) for dil in CLASS_DILATIONS for _ in range(3)] + [view(1)] * 4
    outs = pl.pallas_call(
        _inproj_kernel,
        grid=(s // tm, b),
        in_specs=[pl.BlockSpec((1, tm, d), lambda si, bi: (bi, si, 0)),
                  pl.BlockSpec((1, d), lambda si, bi: (0, 0)),
                  pl.BlockSpec(w_in_bf16.shape, lambda si, bi: (0, 0))] + [tab_spec] * 6,
        out_specs=[v[0] for v in views],
        out_shape=[v[1] for v in views],
        scratch_shapes=[pltpu.VMEM((gw // LANES, tm, LANES), F32)],
        compiler_params=_cparams(("arbitrary", "arbitrary")),
        name="inproj",
    )(x, gain.reshape(1, d), w_in_bf16, *tabs)
    n_attn = 3 * (1 + len(CLASS_DILATIONS))
    attn_qkv = {dil: outs[3 * c:3 * c + 3] for c, dil in enumerate((1,) + CLASS_DILATIONS)}
    return attn_qkv, outs[n_attn:]


ATTN_WINDOW_CASES = 3


def _attn_kernel(q_ref, k_ref, v_ref, o_ref, lse_ref, bias_ref, *, length, tq, reach):
    tqb = q_ref.shape[1]
    win = tq + 2 * reach
    heads_per_pair = LANES // HEAD_DIM
    qi = pl.program_id(2)
    lane = lax.broadcasted_iota(jnp.int32, (1, LANES), 1)
    lane_t = lax.broadcasted_iota(jnp.int32, (tq, LANES), 1)
    ones = jnp.ones((win, LANES), BF16)

    @pl.when((pl.program_id(0) == 0) & (pl.program_id(1) == 0) & (qi == 0))
    def _():
        diff = (lax.broadcasted_iota(jnp.int32, (heads_per_pair * tq, win), 1)
                - (lax.broadcasted_iota(jnp.int32, (heads_per_pair * tq, win), 0) & (tq - 1)))
        for case in range(ATTN_WINDOW_CASES):
            off = case * reach
            bias_ref[case] = jnp.where((diff >= off - reach) & (diff <= off + reach), 0.0, NEG_INF)

    def sub(t, carry):
        q0 = qi * tqb + t * tq
        ws = jnp.clip(q0 - reach, 0, length - win)
        ws = pl.multiple_of(ws, reach)
        bias = bias_ref[(q0 - ws) // reach]
        rows = pl.ds(pl.multiple_of(t * tq, tq), tq)
        m_tile = jnp.zeros((tq, LANES), F32)
        l_tile = jnp.ones((tq, LANES), F32)
        for g in range(HEAD_GROUP_WIDTH // LANES):
            cols = slice(g * LANES, (g + 1) * LANES)
            qg = q_ref[0, rows, cols]
            kw = k_ref[0, pl.ds(ws, win), cols]
            v_ones = jnp.concatenate([v_ref[0, pl.ds(ws, win), cols], ones], axis=1)
            hms = [(lane >= hh * HEAD_DIM) & (lane < (hh + 1) * HEAD_DIM) for hh in range(heads_per_pair)]
            q2 = jnp.concatenate([qg * hm.astype(BF16) for hm in hms], axis=0)
            sc = lax.dot_general(q2, kw, (((1,), (1,)), ((), ())), preferred_element_type=F32) + bias
            m = jnp.max(sc, axis=-1, keepdims=True)
            p = jnp.exp2(sc - m).astype(BF16)
            pv = jnp.dot(p, v_ones, preferred_element_type=F32)
            l = pv[:, LANES:]
            o = pv[:, :LANES] / l
            o_pair = jnp.zeros((tq, LANES), F32)
            for hh in range(heads_per_pair):
                part = slice(hh * tq, (hh + 1) * tq)
                head_lane = lane_t == g * heads_per_pair + hh
                o_pair = jnp.where(hms[hh], o[part], o_pair)
                m_tile = jnp.where(head_lane, m[part], m_tile)
                l_tile = jnp.where(head_lane, l[part], l_tile)
            o_ref[0, rows, cols] = o_pair.astype(BF16)
        lse_ref[0, rows, :] = m_tile + jnp.log2(l_tile)
        return carry

    lax.fori_loop(0, tqb // tq, sub, 0, unroll=True)


def _attn_branch(qc, kc, vc, dilation, reach, tq, tqb):
    b, length, dw = qc.shape
    w = dw // dilation
    tqb = min(tqb, length)
    assert tq % reach == 0 and tq > reach and length % tq == 0 and length >= tq + 2 * reach
    q_spec = pl.BlockSpec((1, tqb, w), lambda bi, r, qi: (bi, qi, r))
    kv_spec = pl.BlockSpec((1, length, w), lambda bi, r, qi: (bi, 0, r))
    o, lse = pl.pallas_call(
        functools.partial(_attn_kernel, length=length, tq=tq, reach=reach),
        grid=(b, dilation, length // tqb),
        in_specs=[q_spec, kv_spec, kv_spec],
        out_specs=[q_spec, pl.BlockSpec((1, tqb, LANES), lambda bi, r, qi: (bi, qi, r))],
        out_shape=[jax.ShapeDtypeStruct((b, length, dilation * w), BF16),
                   jax.ShapeDtypeStruct((b, length, dilation * LANES), F32)],
        scratch_shapes=[pltpu.VMEM((ATTN_WINDOW_CASES, (LANES // HEAD_DIM) * tq, tq + 2 * reach), F32)],
        compiler_params=_cparams(("arbitrary", "arbitrary", "arbitrary")),
        name=f"attn_d{dilation}",
    )(qc, kc, vc)
    return o, lse


RET_TAB_QF, RET_TAB_QB, RET_TAB_KF, RET_TAB_KB = range(4)


def _retention_kernel(lgf_ref, lgb_ref, q_ref, k_ref, v_ref, g_ref, gain_ref, o_ref,
                      tab_ref, dec_ref, sb_ref, st_ref, *, chunk, unroll):
    c = chunk
    n = q_ref.shape[1] // c
    width = q_ref.shape[2]
    n_pairs = width // LANES
    heads_per_pair = LANES // HEAD_DIM
    n_heads = n_pairs * heads_per_pair
    head0 = pl.program_id(1) * n_heads
    lane_w = lax.broadcasted_iota(jnp.int32, (1, width), 1)
    lgf = [lgf_ref[head0 + hd] for hd in range(n_heads)]
    lgb = [lgb_ref[head0 + hd] for hd in range(n_heads)]
    lgf_lane = jnp.zeros((1, width), F32)
    lgb_lane = jnp.zeros((1, width), F32)
    for hd in range(n_heads):
        in_head = (lane_w >= hd * HEAD_DIM) & (lane_w < (hd + 1) * HEAD_DIM)
        lgf_lane = jnp.where(in_head, lgf[hd], lgf_lane)
        lgb_lane = jnp.where(in_head, lgb[hd], lgb_lane)
    idx = lax.broadcasted_iota(jnp.int32, (c, width), 0).astype(F32)
    tab_ref[RET_TAB_QF] = jnp.exp((idx + 1.0) * lgf_lane)
    tab_ref[RET_TAB_QB] = jnp.exp((c - idx) * lgb_lane)
    tab_ref[RET_TAB_KF] = jnp.exp((c - 1.0 - idx) * lgf_lane)
    tab_ref[RET_TAB_KB] = jnp.exp(idx * lgb_lane)
    sdf = jnp.exp(c * lgf_lane)
    sdb = jnp.exp(c * lgb_lane)
    dmat = (lax.broadcasted_iota(jnp.int32, (c, c), 0)
            - lax.broadcasted_iota(jnp.int32, (c, c), 1)).astype(F32)
    for hd in range(n_heads):
        dec_ref[hd // heads_per_pair, :, (hd % heads_per_pair) * c:(hd % heads_per_pair + 1) * c] = (
            jnp.where(dmat >= 0, jnp.exp(dmat * lgf[hd]), jnp.exp(-dmat * lgb[hd])))
    lane = lax.broadcasted_iota(jnp.int32, (1, LANES), 1)
    lane_hi = lane >= HEAD_DIM
    head_masks = [((lane >= hh * HEAD_DIM) & (lane < (hh + 1) * HEAD_DIM)).astype(BF16)
                  for hh in range(heads_per_pair)]
    row_hi = lax.broadcasted_iota(jnp.int32, (LANES, LANES), 0) >= HEAD_DIM
    col_hi = lax.broadcasted_iota(jnp.int32, (LANES, LANES), 1) >= HEAD_DIM
    blockdiag = row_hi == col_hi

    def kv_state(kd, vv):
        kt = jnp.transpose(kd).astype(BF16)
        return jnp.where(blockdiag, jnp.dot(kt, vv, preferred_element_type=F32), 0.0)

    st_ref[...] = jnp.zeros_like(st_ref)
    sb_ref[n - 1] = jnp.zeros(sb_ref.shape[1:], sb_ref.dtype)

    def back(i, carry):
        nn = n - 1 - i
        rows = pl.ds(pl.multiple_of(nn * c, c), c)
        for p in range(n_pairs):
            cols = slice(p * LANES, (p + 1) * LANES)
            kd = k_ref[0, rows, cols].astype(F32) * tab_ref[RET_TAB_KB, :, cols]
            new = st_ref[p] * sdb[:, cols] + kv_state(kd, v_ref[0, rows, cols])
            st_ref[p] = new
            sb_ref[nn - 1, p] = new.astype(BF16)
        return carry

    lax.fori_loop(0, n - 1, back, 0, unroll=unroll)

    st_ref[...] = jnp.zeros_like(st_ref)

    def fwd(nn, carry):
        rows = pl.ds(pl.multiple_of(nn * c, c), c)
        for p in range(n_pairs):
            cols = slice(p * LANES, (p + 1) * LANES)
            qq = q_ref[0, rows, cols]
            kk = k_ref[0, rows, cols]
            vv = v_ref[0, rows, cols]
            qf = qq.astype(F32)
            sf = st_ref[p]
            qcat = jnp.concatenate([(qf * tab_ref[RET_TAB_QF, :, cols]).astype(BF16),
                                    (qf * tab_ref[RET_TAB_QB, :, cols]).astype(BF16)], axis=1)
            scat = jnp.concatenate([sf.astype(BF16), sb_ref[nn, p]], axis=0)
            o = jnp.dot(qcat, scat, preferred_element_type=F32)
            k2 = jnp.concatenate([kk * hm for hm in head_masks], axis=0)
            v2 = jnp.concatenate([vv * hm for hm in head_masks], axis=0)
            sc = lax.dot_general(qq, k2, (((1,), (1,)), ((), ())), preferred_element_type=F32)
            o = o + jnp.dot((sc * dec_ref[p]).astype(BF16), v2, preferred_element_type=F32)
            s_lo = jnp.sum(jnp.where(lane_hi, 0.0, o), axis=-1, keepdims=True)
            s_hi = jnp.sum(jnp.where(lane_hi, o, 0.0), axis=-1, keepdims=True)
            mu = jnp.where(lane_hi, s_hi, s_lo) * (1.0 / HEAD_DIM)
            dev = o - mu
            d2 = dev * dev
            v_lo = jnp.sum(jnp.where(lane_hi, 0.0, d2), axis=-1, keepdims=True)
            v_hi = jnp.sum(jnp.where(lane_hi, d2, 0.0), axis=-1, keepdims=True)
            var = jnp.where(lane_hi, v_hi, v_lo) * (1.0 / HEAD_DIM)
            out = dev * lax.rsqrt(var + NORM_EPS) * gain_ref[:, cols] * g_ref[0, rows, cols].astype(F32)
            o_ref[0, rows, cols] = out.astype(BF16)
            kd = kk.astype(F32) * tab_ref[RET_TAB_KF, :, cols]
            st_ref[p] = sf * sdf[:, cols] + kv_state(kd, vv)
        return carry

    lax.fori_loop(0, n, fwd, 0, unroll=unroll)


def _retention(qr, kr, vr, gate, lg_f, lg_b, out_gain, width=512, unroll=2):
    b, s, w = qr.shape
    n_pairs = width // LANES
    n_heads = width // HEAD_DIM
    spec = pl.BlockSpec((1, s, width), lambda bi, p, *_: (bi, 0, p))
    grid_spec = pltpu.PrefetchScalarGridSpec(
        num_scalar_prefetch=2,
        grid=(b, w // width),
        in_specs=[spec, spec, spec, spec, pl.BlockSpec((1, width), lambda bi, p, *_: (0, p))],
        out_specs=spec,
        scratch_shapes=[pltpu.VMEM((4, RET_CHUNK, width), F32),
                        pltpu.VMEM((n_pairs, RET_CHUNK, (LANES // HEAD_DIM) * RET_CHUNK), F32),
                        pltpu.VMEM((s // RET_CHUNK, n_pairs, LANES, LANES), BF16),
                        pltpu.VMEM((n_pairs, LANES, LANES), F32)],
    )
    return pl.pallas_call(
        functools.partial(_retention_kernel, chunk=RET_CHUNK, unroll=unroll),
        grid_spec=grid_spec,
        out_shape=jax.ShapeDtypeStruct((b, s, w), BF16),
        compiler_params=_cparams(("arbitrary", "arbitrary")),
        name="retention",
    )(lg_f, lg_b, qr, kr, vr, gate, out_gain.reshape(1, w))


def _split_bf16(t):
    hi = t.astype(BF16)
    lo = (t - hi.astype(F32)).astype(BF16)
    return hi, lo


def _pack_bf16_pairs(t):
    n = t.shape[1] // 2
    hi = pltpu.bitcast(t[:, :n].astype(BF16).astype(F32), jnp.uint32)
    lo = pltpu.bitcast(t[:, n:].astype(BF16).astype(F32), jnp.uint32)
    return hi | (lo >> 16)


def _unpack_bf16_pairs(u):
    hi = pltpu.bitcast(u & jnp.uint32(0xFFFF0000), F32)
    lo = pltpu.bitcast(u << 16, F32)
    return jnp.concatenate([hi, lo], axis=1)


def _outproj_kernel(x_ref, o1_ref, o2_ref, o3_ref, l1_ref, l2_ref, l3_ref, orr_ref, ga_ref, expand_ref,
                    wout_ref, gf_ref, wr_hi_ref, wr_lo_ref, br_ref, h_ref, hn_ref, logit_ref,
                    *nat_refs):
    tm = x_ref.shape[0]
    gw = HEAD_GROUP_WIDTH
    os, ls = [], []
    for (_, dil), o_ref, l_ref in zip(DILATED_BRANCHES, (o1_ref, o2_ref, o3_ref), (l1_ref, l2_ref, l3_ref)):
        if dil == 1:
            os.append(o_ref[...].astype(F32))
            ls.append(l_ref[...])
            continue
        c = CLASS_DILATIONS.index(dil)
        onat_ref, lnat_ref = nat_refs[2 * c], nat_refs[2 * c + 1]
        for r in range(dil):
            rows = pl.ds(r, tm // dil, stride=dil)
            for g in range(gw // LANES):
                col = r * gw + g * LANES
                onat_ref[g, rows, :] = o_ref[:, col:col + LANES].astype(F32)
            lnat_ref[rows, :] = l_ref[:, r * LANES:(r + 1) * LANES]
        os.append(jnp.concatenate([onat_ref[g] for g in range(gw // LANES)], axis=1))
        ls.append(lnat_ref[...])
    mx = jnp.maximum(jnp.maximum(ls[0], ls[1]), ls[2])
    es = [jnp.exp2(l - mx) for l in ls]
    inv = 1.0 / (es[0] + es[1] + es[2])
    expand = expand_ref[...]
    oa = jnp.zeros((tm, gw), F32)
    for e, o in zip(es, os):
        w_hi, w_lo = _split_bf16(e * inv)
        wexp = (jnp.dot(w_hi, expand, preferred_element_type=F32)
                + jnp.dot(w_lo, expand, preferred_element_type=F32))
        oa = oa + wexp * o
    oa = oa * lax.rsqrt(jnp.mean(oa * oa, axis=-1, keepdims=True) + NORM_EPS) * ga_ref[...]
    mixed = jnp.concatenate([oa.astype(BF16), orr_ref[...]], axis=1)
    h = x_ref[...] + jnp.dot(mixed, wout_ref[...], preferred_element_type=F32)
    h_ref[...] = h
    hn = h * lax.rsqrt(jnp.mean(h * h, axis=-1, keepdims=True) + NORM_EPS) * gf_ref[...]
    hn_ref[...] = _pack_bf16_pairs(hn)
    hn_hi, hn_lo = _split_bf16(hn)
    wr_hi = wr_hi_ref[...]
    logit_ref[...] = (jnp.dot(hn_hi, wr_hi, preferred_element_type=F32)
                      + jnp.dot(hn_lo, wr_hi, preferred_element_type=F32)
                      + jnp.dot(hn_hi, wr_lo_ref[...], preferred_element_type=F32)
                      + br_ref[...])


def _outproj(x2, o_list, lse_list, orr, attn_gain, w_out_bf16, ffn_gain, w_router, b_router, tm):
    t, d = x2.shape
    w = HEAD_GROUP_WIDTH
    expand = np.zeros((LANES, w), np.float32)
    for hd in range(ATTN_HEADS):
        expand[hd, hd * HEAD_DIM:(hd + 1) * HEAD_DIM] = 1.0
    expand = jnp.asarray(expand, BF16)
    wr_hi = w_router.astype(BF16)
    wr_lo = (w_router - wr_hi.astype(F32)).astype(BF16)
    row = lambda width, dil=1: pl.BlockSpec((tm // dil, dil * width), lambda i: (i, 0))
    full = lambda a: pl.BlockSpec(a.shape, lambda i: (0,) * a.ndim)
    ga = attn_gain.reshape(1, w)
    gf = ffn_gain.reshape(1, d)
    dils = [dil for _, dil in DILATED_BRANCHES]
    o_flat = [o.reshape(t // dil, dil * w) for o, dil in zip(o_list, dils)]
    l_flat = [l.reshape(t // dil, dil * LANES) for l, dil in zip(lse_list, dils)]
    nat_scratch = []
    for _ in CLASS_DILATIONS:
        nat_scratch += [pltpu.VMEM((w // LANES, tm, LANES), F32), pltpu.VMEM((tm, LANES), F32)]
    return pl.pallas_call(
        _outproj_kernel,
        grid=(t // tm,),
        in_specs=[row(d)] + [row(w, dil) for dil in dils] + [row(LANES, dil) for dil in dils] + [row(w)]
                 + [full(ga), full(expand), full(w_out_bf16), full(gf), full(wr_hi), full(wr_lo), full(b_router)],
        out_specs=[row(d), row(d // 2), row(LANES)],
        out_shape=[jax.ShapeDtypeStruct((t, d), F32),
                   jax.ShapeDtypeStruct((t, d // 2), jnp.uint32),
                   jax.ShapeDtypeStruct((t, LANES), F32)],
        scratch_shapes=nat_scratch,
        compiler_params=_cparams(("arbitrary",)),
        name="outproj",
    )(x2, *o_flat, *l_flat, orr, ga, expand, w_out_bf16, gf, wr_hi, wr_lo, b_router)


ROUTE_E1, ROUTE_E2, ROUTE_G1, ROUTE_G2, ROUTE_R1, ROUTE_R2 = range(6)
EXPERT_LANE0 = MOE_GROUPS


def _route_kernel(logit_ref, tri_ref, route_ref, count_ref, run_ref):
    @pl.when(pl.program_id(0) == 0)
    def _():
        run_ref[...] = jnp.zeros_like(run_ref)

    lg = logit_ref[...]
    tm = lg.shape[0]
    lane = lax.broadcasted_iota(jnp.int32, lg.shape, 1)
    big = jnp.int32(1 << 20)

    def top(vals):
        m = jnp.max(vals, axis=-1, keepdims=True)
        i = jnp.min(jnp.where(vals == m, lane, big), axis=-1, keepdims=True)
        return m, i

    gl = jnp.where(lane < MOE_GROUPS, lg, -jnp.inf)
    gmax, gidx = top(gl)
    group_gate = 1.0 / jnp.sum(jnp.exp(gl - gmax), axis=-1, keepdims=True)
    lo = EXPERT_LANE0 + gidx * EXPERTS_PER_GROUP
    el = jnp.where((lane >= lo) & (lane < lo + EXPERTS_PER_GROUP), lg, -jnp.inf)
    t1, i1 = top(el)
    t2, i2 = top(jnp.where(lane == i1, -jnp.inf, el))
    e21 = jnp.exp(t2 - t1)
    g1 = group_gate / (1.0 + e21)
    g2 = group_gate * e21 / (1.0 + e21)
    e1 = i1 - EXPERT_LANE0
    e2 = i2 - EXPERT_LANE0
    oh1 = lane == e1
    oh2 = lane == e2
    cnt = oh1.astype(F32) + oh2.astype(F32)
    prefix = jnp.dot(tri_ref[...], cnt.astype(BF16), preferred_element_type=F32) + run_ref[0:1, :]
    r1 = jnp.sum(jnp.where(oh1, prefix, 0.0), axis=-1, keepdims=True)
    r2 = jnp.sum(jnp.where(oh2, prefix, 0.0), axis=-1, keepdims=True)
    new_run = run_ref[0:1, :] + jnp.sum(cnt, axis=0, keepdims=True)
    run_ref[...] = jnp.broadcast_to(new_run, run_ref.shape)
    count_ref[...] = jnp.broadcast_to(new_run, count_ref.shape)
    out = jnp.zeros(lg.shape, F32)
    for ln, val in ((ROUTE_E1, e1.astype(F32)), (ROUTE_E2, e2.astype(F32)), (ROUTE_G1, g1),
                    (ROUTE_G2, g2), (ROUTE_R1, r1), (ROUTE_R2, r2)):
        out = jnp.where(lane == ln, val, out)
    route_ref[...] = out


def _route(logits, tm):
    t = logits.shape[0]
    tri = jnp.asarray(np.tril(np.ones((tm, tm), np.float32), -1), BF16)
    return pl.pallas_call(
        _route_kernel,
        grid=(t // tm,),
        in_specs=[pl.BlockSpec((tm, LANES), lambda i: (i, 0)), pl.BlockSpec((tm, tm), lambda i: (0, 0))],
        out_specs=[pl.BlockSpec((tm, LANES), lambda i: (i, 0)), pl.BlockSpec((8, LANES), lambda i: (0, 0))],
        out_shape=[jax.ShapeDtypeStruct((t, LANES), F32), jax.ShapeDtypeStruct((8, LANES), F32)],
        scratch_shapes=[pltpu.VMEM((8, LANES), F32)],
        compiler_params=_cparams(("arbitrary",)),
        name="route",
    )(logits, tri)


TOP_K = 2
DMA_ISSUE_UNROLL = 8
DMA_PRIORITIES = 2


SEG_ALIGN = 8
SORTED_TAIL = N_EXPERTS * SEG_ALIGN + MOE_BLOCK


def _tail_pieces():
    full, rest = divmod(SORTED_TAIL, MOE_BLOCK)
    return [MOE_BLOCK] * full + ([rest] if rest else [])


def _dispatch_kernel(seg_end_ref, pos_ref, hn_ref, xs_hbm, zbuf, sem, zsem, *, n_rows):
    tm = hn_ref.shape[0]

    @pl.when(pl.program_id(0) == 0)
    def _():
        zbuf[...] = jnp.zeros_like(zbuf)
        tail = pltpu.make_async_copy(zbuf, xs_hbm.at[pl.ds(n_rows, SORTED_TAIL)], zsem)
        tail.start()
        tail.wait()

        def hole(e):
            end = seg_end_ref[e]
            start = pl.multiple_of(end - (end & (SEG_ALIGN - 1)), SEG_ALIGN)
            return pltpu.make_async_copy(zbuf.at[pl.ds(0, SEG_ALIGN)], xs_hbm.at[pl.ds(start, SEG_ALIGN)], zsem)

        for e in range(N_EXPERTS):
            @pl.when((seg_end_ref[e] & (SEG_ALIGN - 1)) != 0)
            def _():
                hole(e).start()
        for e in range(N_EXPERTS):
            @pl.when((seg_end_ref[e] & (SEG_ALIGN - 1)) != 0)
            def _():
                hole(e).wait()

    def body(j, c):
        for k in range(TOP_K):
            pltpu.make_async_copy(hn_ref.at[pl.ds(j, 1)], xs_hbm.at[pl.ds(pos_ref[0, 0, TOP_K * j + k], 1)],
                                  sem).start(priority=k % DMA_PRIORITIES)
        return c
    lax.fori_loop(0, tm, body, 0, unroll=DMA_ISSUE_UNROLL)

    tile = pltpu.make_async_copy(hn_ref, xs_hbm.at[pl.ds(0, tm)], sem)
    for k in range(TOP_K):
        tile.wait()


def _dispatch(hn_packed, pos, seg_end, tm):
    t, dp = hn_packed.shape
    n_rows = TOP_K * t
    pos3 = pos.reshape(t // tm, 1, TOP_K * tm)
    grid_spec = pltpu.PrefetchScalarGridSpec(
        num_scalar_prefetch=1,
        grid=(t // tm,),
        in_specs=[pl.BlockSpec((1, 1, TOP_K * tm), lambda i, se: (i, 0, 0), memory_space=pltpu.SMEM),
                  pl.BlockSpec((tm, dp), lambda i, se: (i, 0))],
        out_specs=pl.BlockSpec(memory_space=pl.ANY),
        scratch_shapes=[pltpu.VMEM((SORTED_TAIL, dp), jnp.uint32),
                        pltpu.SemaphoreType.DMA,
                        pltpu.SemaphoreType.DMA],
    )
    return pl.pallas_call(
        functools.partial(_dispatch_kernel, n_rows=n_rows),
        grid_spec=grid_spec,
        out_shape=jax.ShapeDtypeStruct((n_rows + SORTED_TAIL, dp), jnp.uint32),
        compiler_params=_cparams(("arbitrary",)),
        name="dispatch",
    )(seg_end, pos3, hn_packed)


def _expert_kernel(bexp_ref, nreal_ref, row0_ref, xs_hbm, wg_ref, wu_ref, wd_ref, ys_hbm,
                   xbuf, ybuf, isem, osem, *, n_rows):
    del bexp_ref
    i = pl.program_id(0)
    nb = pl.num_programs(0)
    slot = i % 2
    nslot = 1 - slot
    n_cur = nreal_ref[i]
    prev = jnp.maximum(i - 1, 0)
    nxt = jnp.minimum(i + 1, nb - 1)

    def in_copy(blk, s):
        row0 = pl.multiple_of(row0_ref[blk], SEG_ALIGN)
        return pltpu.make_async_copy(xs_hbm.at[pl.ds(row0, MOE_BLOCK)], xbuf.at[s], isem.at[s])

    def out_copy(blk, s):
        row0 = pl.multiple_of(row0_ref[blk], SEG_ALIGN)
        return pltpu.make_async_copy(ybuf.at[s], ys_hbm.at[pl.ds(row0, MOE_BLOCK)], osem.at[s])

    @pl.when(i == 0)
    def _():
        ybuf[...] = jnp.zeros_like(ybuf)
        tails = [pltpu.make_async_copy(ybuf.at[s, pl.ds(0, size)], ys_hbm.at[pl.ds(n_rows + s * MOE_BLOCK, size)],
                                       osem.at[s])
                 for s, size in enumerate(_tail_pieces())]
        for tail in tails:
            tail.start()
        for tail in tails:
            tail.wait()

        @pl.when(n_cur > 0)
        def _():
            in_copy(i, slot).start()

    @pl.when((i + 1 < nb) & (nreal_ref[nxt] > 0))
    def _():
        in_copy(nxt, nslot).start()

    @pl.when(n_cur > 0)
    def _():
        in_copy(i, slot).wait()
        xb = _unpack_bf16_pairs(xbuf[slot]).astype(BF16)
        gate = jnp.dot(xb, wg_ref[0], preferred_element_type=F32)
        up = jnp.dot(xb, wu_ref[0], preferred_element_type=F32)
        hid = (gate * jax.nn.sigmoid(gate) * up).astype(BF16)
        ybuf[slot] = _pack_bf16_pairs(jnp.dot(hid, wd_ref[0], preferred_element_type=F32))

    @pl.when((i >= 1) & (nreal_ref[prev] > 0))
    def _():
        out_copy(prev, nslot).wait()

    @pl.when(n_cur > 0)
    def _():
        out_copy(i, slot).start()

        @pl.when(i == nb - 1)
        def _():
            out_copy(i, slot).wait()


def _experts(xs, block_expert, block_nreal, block_row0, wg, wu, wd):
    n_blocks = block_expert.shape[0]
    assert n_blocks >= 2 and len(_tail_pieces()) <= 2
    n_rows_pad, dp = xs.shape
    _, d, ff = wg.shape
    grid_spec = pltpu.PrefetchScalarGridSpec(
        num_scalar_prefetch=3,
        grid=(n_blocks,),
        in_specs=[pl.BlockSpec(memory_space=pl.ANY),
                  pl.BlockSpec((1, d, ff), lambda i, be, nr, r0: (be[i], 0, 0)),
                  pl.BlockSpec((1, d, ff), lambda i, be, nr, r0: (be[i], 0, 0)),
                  pl.BlockSpec((1, ff, d), lambda i, be, nr, r0: (be[i], 0, 0))],
        out_specs=pl.BlockSpec(memory_space=pl.ANY),
        scratch_shapes=[pltpu.VMEM((2, MOE_BLOCK, dp), jnp.uint32),
                        pltpu.VMEM((2, MOE_BLOCK, dp), jnp.uint32),
                        pltpu.SemaphoreType.DMA((2,)),
                        pltpu.SemaphoreType.DMA((2,))],
    )
    return pl.pallas_call(
        functools.partial(_expert_kernel, n_rows=n_rows_pad - SORTED_TAIL),
        grid_spec=grid_spec,
        out_shape=jax.ShapeDtypeStruct((n_rows_pad, dp), jnp.uint32),
        compiler_params=_cparams(("arbitrary",)),
        name="experts",
    )(block_expert, block_nreal, block_row0, xs, wg, wu, wd)


def _final_kernel(pos_ref, pos_next_ref, h_ref, route_ref, gain_ref, ys_hbm, out_ref, ybuf, sem):
    i = pl.program_id(0)
    nt = pl.num_programs(0)
    slot = i % 2
    tm = h_ref.shape[0]

    def gather_start(idx_ref, s):
        def body(j, c):
            for k in range(TOP_K):
                pltpu.make_async_copy(ys_hbm.at[pl.ds(idx_ref[0, 0, TOP_K * j + k], 1)],
                                      ybuf.at[s, k, pl.ds(j, 1)], sem.at[s]).start(priority=k % DMA_PRIORITIES)
            return c
        lax.fori_loop(0, tm, body, 0, unroll=DMA_ISSUE_UNROLL)

    @pl.when(i == 0)
    def _():
        gather_start(pos_ref, slot)

    @pl.when(i + 1 < nt)
    def _():
        gather_start(pos_next_ref, 1 - slot)

    for k in range(TOP_K):
        pltpu.make_async_copy(ys_hbm.at[pl.ds(0, tm)], ybuf.at[slot, k], sem.at[slot]).wait()

    r = route_ref[...]
    y = h_ref[...]
    for k, gate_lane in enumerate((ROUTE_G1, ROUTE_G2)):
        y = y + r[:, gate_lane:gate_lane + 1] * _unpack_bf16_pairs(ybuf[slot, k])
    out_ref[...] = y * lax.rsqrt(jnp.mean(y * y, axis=-1, keepdims=True) + NORM_EPS) * gain_ref[...]


def _final(h, ys, pos, route, gain, tm):
    t, d = h.shape
    nt = t // tm
    pos3 = pos.reshape(nt, 1, TOP_K * tm)
    smem_blk = lambda imap: pl.BlockSpec((1, 1, TOP_K * tm), imap, memory_space=pltpu.SMEM)
    return pl.pallas_call(
        _final_kernel,
        grid=(nt,),
        in_specs=[smem_blk(lambda i: (i, 0, 0)),
                  smem_blk(lambda i: (jnp.minimum(i + 1, nt - 1), 0, 0)),
                  pl.BlockSpec((tm, d), lambda i: (i, 0)),
                  pl.BlockSpec((tm, LANES), lambda i: (i, 0)),
                  pl.BlockSpec((1, d), lambda i: (0, 0)),
                  pl.BlockSpec(memory_space=pl.ANY)],
        out_specs=pl.BlockSpec((tm, d), lambda i: (i, 0)),
        out_shape=jax.ShapeDtypeStruct((t, d), F32),
        scratch_shapes=[pltpu.VMEM((2, TOP_K, tm, d // 2), jnp.uint32),
                        pltpu.SemaphoreType.DMA((2,))],
        compiler_params=_cparams(("arbitrary",)),
        name="final",
    )(pos3, pos3, h, route, gain.reshape(1, d), ys)


def _layer(h3, mix_gain, w_in, attn_gain, decay_f, decay_b, ret_gain, w_out, ffn_gain,
           w_rg, b_rg, w_re, b_re, w_eg, w_eu, w_ed, final_gain):
    b, s, d = h3.shape
    t = b * s
    tm = 512
    attn_qkv, (qr, kr, vr, gr) = _inproj(h3, mix_gain, w_in.astype(BF16), tm)

    o_list, lse_list = [], []
    for window, dilation in DILATED_BRANCHES:
        reach = (window // 2) // dilation
        o, lse = _attn_branch(*attn_qkv[dilation], dilation, reach, tq=128, tqb=512)
        o_list.append(o)
        lse_list.append(lse)

    lg_f = jnp.log1p(-jnp.exp2(decay_f.astype(F32)))
    lg_b = jnp.log1p(-jnp.exp2(decay_b.astype(F32)))
    orr = _retention(qr, kr, vr, gr, lg_f, lg_b, ret_gain).reshape(t, HEAD_GROUP_WIDTH)

    n_route = MOE_GROUPS + N_EXPERTS
    w_router = jnp.zeros((d, LANES), F32).at[:, :n_route].set(jnp.concatenate([w_rg, w_re], axis=1).astype(F32))
    b_router = jnp.zeros((1, LANES), F32).at[0, :n_route].set(jnp.concatenate([b_rg, b_re]).astype(F32))
    h, hn_packed, logits = _outproj(h3.reshape(t, d), o_list, lse_list, orr, attn_gain, w_out.astype(BF16),
                                    ffn_gain, w_router, b_router, tm)

    route, counts8 = _route(logits, tm)

    n_blocks = -(-TOP_K * t // MOE_BLOCK) + N_EXPERTS
    counts = counts8[0, :N_EXPERTS].astype(jnp.int32)
    aligned = ((counts + SEG_ALIGN - 1) // SEG_ALIGN) * SEG_ALIGN
    seg_start = jnp.cumsum(aligned) - aligned
    expert_iota = jnp.arange(N_EXPERTS, dtype=jnp.int32)
    e12 = route[:, ROUTE_E1:ROUTE_E2 + 1].astype(jnp.int32)
    r12 = route[:, ROUTE_R1:ROUTE_R2 + 1].astype(jnp.int32)
    pos = r12 + jnp.sum(jnp.where(e12[..., None] == expert_iota, seg_start, 0), axis=-1)
    nblk = (counts + MOE_BLOCK - 1) // MOE_BLOCK
    blk_end = jnp.cumsum(nblk)
    blk_start = blk_end - nblk
    blk = jnp.arange(n_blocks, dtype=jnp.int32)[:, None]
    owner = (blk >= blk_start) & (blk < blk_end)
    local = (blk - blk_start) * MOE_BLOCK
    block_row0 = jnp.sum(jnp.where(owner, seg_start + local, 0), axis=-1).astype(jnp.int32)
    block_nreal = jnp.sum(jnp.where(owner, jnp.clip(counts - local, 0, MOE_BLOCK), 0), axis=-1).astype(jnp.int32)
    block_expert = jnp.minimum(jnp.sum((blk >= blk_end).astype(jnp.int32), axis=-1), N_EXPERTS - 1)

    xs = _dispatch(hn_packed, pos, seg_start + counts, tm)
    ys = _experts(xs, block_expert, block_nreal, block_row0, w_eg.astype(BF16), w_eu.astype(BF16),
                  w_ed.astype(BF16))
    out = _final(h, ys, pos, route, final_gain, tm)
    return out.reshape(b, s, d)


def kernel(x, mix_norm_gain, w_in, attn_out_gain, ret_decay_fwd, ret_decay_bwd, ret_out_gain, w_out,
           ffn_norm_gain, w_route_group, b_route_group, w_route_expert, b_route_expert,
           w_expert_gate, w_expert_up, w_expert_down, final_norm_gain):
    depth = mix_norm_gain.shape[0]
    assert depth == 1, "the final rmsnorm is fused into the single layer's combine kernel"
    l = 0
    return _layer(x, mix_norm_gain[l], w_in[l], attn_out_gain[l], ret_decay_fwd[l], ret_decay_bwd[l],
                  ret_out_gain[l], w_out[l], ffn_norm_gain[l], w_route_group[l], b_route_group[l],
                  w_route_expert[l], b_route_expert[l], w_expert_gate[l], w_expert_up[l], w_expert_down[l],
                  final_norm_gain)
```

```python
import functools

import numpy as np
import jax
import jax.numpy as jnp
from jax import lax
from jax.experimental import pallas as pl
from jax.experimental.pallas import tpu as pltpu

F32 = jnp.float32
BF16 = jnp.bfloat16

ATTN_HEADS = 8
HEAD_DIM = 64
RET_HEADS = 8
HEAD_GROUP_WIDTH = 512
N_PROJ_GROUPS = 7
DILATED_BRANCHES = ((128, 1), (512, 4), (2048, 16))
ROPE_THETA = 500000.0
ROPE_DIM = HEAD_DIM // 4
RET_THETA = 10000.0
RET_CHUNK = 128
MOE_GROUPS = 4
EXPERTS_PER_GROUP = 8
N_EXPERTS = MOE_GROUPS * EXPERTS_PER_GROUP
MOE_BLOCK = 512
NORM_EPS = 1e-6
NEG_INF = -1e30

LANES = 128
VMEM_LIMIT = 56 * 1024 * 1024


def _cparams(sem):
    return pltpu.CompilerParams(dimension_semantics=sem, vmem_limit_bytes=VMEM_LIMIT)


def _rotary_tables(seq, half, freqs):
    pos = np.arange(seq, dtype=np.float64)[:, None]
    ang = pos * freqs[None, :].astype(np.float64)
    cos, sin = np.cos(ang), np.sin(ang)
    c = np.ones((seq, HEAD_DIM)); sp = np.zeros((seq, HEAD_DIM)); sm = np.zeros((seq, HEAD_DIM))
    c[:, :half] = cos; c[:, half:2 * half] = cos
    sp[:, half:2 * half] = sin
    sm[:, :half] = -sin
    rep = LANES // HEAD_DIM
    return tuple(jnp.asarray(np.tile(t, (1, rep)), F32) for t in (c, sp, sm))


def _rotate(t, c, sp, sm, half):
    outs = []
    for g in range(t.shape[1] // LANES):
        tg = t[:, g * LANES:(g + 1) * LANES]
        outs.append(tg * c + pltpu.roll(tg, half, 1) * sp + pltpu.roll(tg, LANES - half, 1) * sm)
    return jnp.concatenate(outs, axis=1)


CLASS_DILATIONS = tuple(d for _, d in DILATED_BRANCHES if d > 1)
ATTN_Q_SCALE = float(np.log2(np.e)) * HEAD_DIM ** -0.5


def _inproj_kernel(x_ref, gain_ref, w_ref, ca_ref, spa_ref, sma_ref, cr_ref, spr_ref, smr_ref, *rest):
    n_cls = len(CLASS_DILATIONS)
    nat_refs = rest[0:3]
    cls_refs = [rest[3 + 3 * c:6 + 3 * c] for c in range(n_cls)]
    qr_ref, kr_ref, vr_ref, gr_ref = rest[3 + 3 * n_cls:7 + 3 * n_cls]
    stage_ref = rest[7 + 3 * n_cls]
    x = x_ref[0]
    tm = x.shape[0]
    ms = jnp.mean(x * x, axis=-1, keepdims=True)
    xn = (x * lax.rsqrt(ms + NORM_EPS) * gain_ref[...]).astype(BF16)
    gw = HEAD_GROUP_WIDTH

    def proj(c):
        return jnp.dot(xn, w_ref[:, c * gw:(c + 1) * gw], preferred_element_type=F32)

    a_tabs = (ca_ref[...], spa_ref[...], sma_ref[...])
    r_tabs = (cr_ref[...], spr_ref[...], smr_ref[...])
    attn_vals = ((_rotate(proj(0), *a_tabs, ROPE_DIM // 2) * ATTN_Q_SCALE),
                 _rotate(proj(1), *a_tabs, ROPE_DIM // 2),
                 proj(2))
    for j, val in enumerate(attn_vals):
        nat_refs[j][0] = val.astype(BF16)
        for g in range(gw // LANES):
            stage_ref[g] = val[:, g * LANES:(g + 1) * LANES]
        for c, d in enumerate(CLASS_DILATIONS):
            for r in range(d):
                for g in range(gw // LANES):
                    col = r * gw + g * LANES
                    cls_refs[c][j][0, :, col:col + LANES] = (
                        stage_ref[g, pl.ds(r, tm // d, stride=d), :].astype(BF16))
    qr_ref[0] = _rotate(proj(3), *r_tabs, HEAD_DIM // 2).astype(BF16)
    kr_ref[0] = (_rotate(proj(4), *r_tabs, HEAD_DIM // 2) * (HEAD_DIM ** -0.5)).astype(BF16)
    vr_ref[0] = proj(5).astype(BF16)
    g = proj(6)
    gr_ref[0] = (g * jax.nn.sigmoid(g)).astype(BF16)


def _inproj(x, gain, w_in_bf16, tm):
    b, s, d = x.shape
    rope_freqs = ROPE_THETA ** (-np.arange(0, ROPE_DIM, 2, dtype=np.float32) / ROPE_DIM)
    ret_freqs = RET_THETA ** (-np.linspace(0.0, 1.0, HEAD_DIM // 2, dtype=np.float32))
    tabs = _rotary_tables(s, ROPE_DIM // 2, rope_freqs) + _rotary_tables(s, HEAD_DIM // 2, ret_freqs)
    gw = HEAD_GROUP_WIDTH
    tab_spec = pl.BlockSpec((tm, LANES), lambda si, bi: (si, 0))

    def view(dil):
        return (pl.BlockSpec((1, tm // dil, dil * gw), lambda si, bi: (bi, si, 0)),
                jax.ShapeDtypeStruct((b, s // dil, dil * gw), BF16))

    views = [view(1)] * 3 + [view(dil) for dil in CLASS_DILATIONS for _ in range(3)] + [view(1)] * 4
    outs = pl.pallas_call(
        _inproj_kernel,
        grid=(s // tm, b),
        in_specs=[pl.BlockSpec((1, tm, d), lambda si, bi: (bi, si, 0)),
                  pl.BlockSpec((1, d), lambda si, bi: (0, 0)),
                  pl.BlockSpec(w_in_bf16.shape, lambda si, bi: (0, 0))] + [tab_spec] * 6,
        out_specs=[v[0] for v in views],
        out_shape=[v[1] for v in views],
        scratch_shapes=[pltpu.VMEM((gw // LANES, tm, LANES), F32)],
        compiler_params=_cparams(("arbitrary", "arbitrary")),
        name="inproj",
    )(x, gain.reshape(1, d), w_in_bf16, *tabs)
    n_attn = 3 * (1 + len(CLASS_DILATIONS))
    attn_qkv = {dil: outs[3 * c:3 * c + 3] for c, dil in enumerate((1,) + CLASS_DILATIONS)}
    return attn_qkv, outs[n_attn:]


ATTN_WINDOW_CASES = 3


def _attn_kernel(q_ref, k_ref, v_ref, o_ref, lse_ref, bias_ref, *, length, tq, reach):
    tqb = q_ref.shape[1]
    win = tq + 2 * reach
    heads_per_pair = LANES // HEAD_DIM
    qi = pl.program_id(2)
    lane = lax.broadcasted_iota(jnp.int32, (1, LANES), 1)
    lane_t = lax.broadcasted_iota(jnp.int32, (tq, LANES), 1)
    ones = jnp.ones((win, LANES), BF16)

    @pl.when((pl.program_id(0) == 0) & (pl.program_id(1) == 0) & (qi == 0))
    def _():
        diff = (lax.broadcasted_iota(jnp.int32, (heads_per_pair * tq, win), 1)
                - (lax.broadcasted_iota(jnp.int32, (heads_per_pair * tq, win), 0) & (tq - 1)))
        for case in range(ATTN_WINDOW_CASES):
            off = case * reach
            bias_ref[case] = jnp.where((diff >= off - reach) & (diff <= off + reach), 0.0, NEG_INF)

    def sub(t, carry):
        q0 = qi * tqb + t * tq
        ws = jnp.clip(q0 - reach, 0, length - win)
        ws = pl.multiple_of(ws, reach)
        bias = bias_ref[(q0 - ws) // reach]
        rows = pl.ds(pl.multiple_of(t * tq, tq), tq)
        m_tile = jnp.zeros((tq, LANES), F32)
        l_tile = jnp.ones((tq, LANES), F32)
        for g in range(HEAD_GROUP_WIDTH // LANES):
            cols = slice(g * LANES, (g + 1) * LANES)
            qg = q_ref[0, rows, cols]
            kw = k_ref[0, pl.ds(ws, win), cols]
            v_ones = jnp.concatenate([v_ref[0, pl.ds(ws, win), cols], ones], axis=1)
            hms = [(lane >= hh * HEAD_DIM) & (lane < (hh + 1) * HEAD_DIM) for hh in range(heads_per_pair)]
            q2 = jnp.concatenate([qg * hm.astype(BF16) for hm in hms], axis=0)
            sc = lax.dot_general(q2, kw, (((1,), (1,)), ((), ())), preferred_element_type=F32) + bias
            m = jnp.max(sc, axis=-1, keepdims=True)
            p = jnp.exp2(sc - m).astype(BF16)
            pv = jnp.dot(p, v_ones, preferred_element_type=F32)
            l = pv[:, LANES:]
            o = pv[:, :LANES] / l
            o_pair = jnp.zeros((tq, LANES), F32)
            for hh in range(heads_per_pair):
                part = slice(hh * tq, (hh + 1) * tq)
                head_lane = lane_t == g * heads_per_pair + hh
                o_pair = jnp.where(hms[hh], o[part], o_pair)
                m_tile = jnp.where(head_lane, m[part], m_tile)
                l_tile = jnp.where(head_lane, l[part], l_tile)
            o_ref[0, rows, cols] = o_pair.astype(BF16)
        lse_ref[0, rows, :] = m_tile + jnp.log2(l_tile)
        return carry

    lax.fori_loop(0, tqb // tq, sub, 0, unroll=True)


def _attn_branch(qc, kc, vc, dilation, reach, tq, tqb):
    b, length, dw = qc.shape
    w = dw // dilation
    tqb = min(tqb, length)
    assert tq % reach == 0 and tq > reach and length % tq == 0 and length >= tq + 2 * reach
    q_spec = pl.BlockSpec((1, tqb, w), lambda bi, r, qi: (bi, qi, r))
    kv_spec = pl.BlockSpec((1, length, w), lambda bi, r, qi: (bi, 0, r))
    o, lse = pl.pallas_call(
        functools.partial(_attn_kernel, length=length, tq=tq, reach=reach),
        grid=(b, dilation, length // tqb),
        in_specs=[q_spec, kv_spec, kv_spec],
        out_specs=[q_spec, pl.BlockSpec((1, tqb, LANES), lambda bi, r, qi: (bi, qi, r))],
        out_shape=[jax.ShapeDtypeStruct((b, length, dilation * w), BF16),
                   jax.ShapeDtypeStruct((b, length, dilation * LANES), F32)],
        scratch_shapes=[pltpu.VMEM((ATTN_WINDOW_CASES, (LANES // HEAD_DIM) * tq, tq + 2 * reach), F32)],
        compiler_params=_cparams(("arbitrary", "arbitrary", "arbitrary")),
        name=f"attn_d{dilation}",
    )(qc, kc, vc)
    return o, lse


RET_TAB_QF, RET_TAB_QB, RET_TAB_KF, RET_TAB_KB = range(4)


def _retention_kernel(lgf_ref, lgb_ref, q_ref, k_ref, v_ref, g_ref, gain_ref, o_ref,
                      tab_ref, dec_ref, sb_ref, st_ref, *, chunk, unroll):
    c = chunk
    n = q_ref.shape[1] // c
    width = q_ref.shape[2]
    n_pairs = width // LANES
    heads_per_pair = LANES // HEAD_DIM
    n_heads = n_pairs * heads_per_pair
    head0 = pl.program_id(1) * n_heads
    lane_w = lax.broadcasted_iota(jnp.int32, (1, width), 1)
    lgf = [lgf_ref[head0 + hd] for hd in range(n_heads)]
    lgb = [lgb_ref[head0 + hd] for hd in range(n_heads)]
    lgf_lane = jnp.zeros((1, width), F32)
    lgb_lane = jnp.zeros((1, width), F32)
    for hd in range(n_heads):
        in_head = (lane_w >= hd * HEAD_DIM) & (lane_w < (hd + 1) * HEAD_DIM)
        lgf_lane = jnp.where(in_head, lgf[hd], lgf_lane)
        lgb_lane = jnp.where(in_head, lgb[hd], lgb_lane)
    idx = lax.broadcasted_iota(jnp.int32, (c, width), 0).astype(F32)
    tab_ref[RET_TAB_QF] = jnp.exp((idx + 1.0) * lgf_lane)
    tab_ref[RET_TAB_QB] = jnp.exp((c - idx) * lgb_lane)
    tab_ref[RET_TAB_KF] = jnp.exp((c - 1.0 - idx) * lgf_lane)
    tab_ref[RET_TAB_KB] = jnp.exp(idx * lgb_lane)
    sdf = jnp.exp(c * lgf_lane)
    sdb = jnp.exp(c * lgb_lane)
    dmat = (lax.broadcasted_iota(jnp.int32, (c, c), 0)
            - lax.broadcasted_iota(jnp.int32, (c, c), 1)).astype(F32)
    for hd in range(n_heads):
        dec_ref[hd // heads_per_pair, :, (hd % heads_per_pair) * c:(hd % heads_per_pair + 1) * c] = (
            jnp.where(dmat >= 0, jnp.exp(dmat * lgf[hd]), jnp.exp(-dmat * lgb[hd])))
    lane = lax.broadcasted_iota(jnp.int32, (1, LANES), 1)
    lane_hi = lane >= HEAD_DIM
    head_masks = [((lane >= hh * HEAD_DIM) & (lane < (hh + 1) * HEAD_DIM)).astype(BF16)
                  for hh in range(heads_per_pair)]
    row_hi = lax.broadcasted_iota(jnp.int32, (LANES, LANES), 0) >= HEAD_DIM
    col_hi = lax.broadcasted_iota(jnp.int32, (LANES, LANES), 1) >= HEAD_DIM
    blockdiag = row_hi == col_hi

    def kv_state(kd, vv):
        kt = jnp.transpose(kd).astype(BF16)
        return jnp.where(blockdiag, jnp.dot(kt, vv, preferred_element_type=F32), 0.0)

    st_ref[...] = jnp.zeros_like(st_ref)
    sb_ref[n - 1] = jnp.zeros(sb_ref.shape[1:], sb_ref.dtype)

    def back(i, carry):
        nn = n - 1 - i
        rows = pl.ds(pl.multiple_of(nn * c, c), c)
        for p in range(n_pairs):
            cols = slice(p * LANES, (p + 1) * LANES)
            kd = k_ref[0, rows, cols].astype(F32) * tab_ref[RET_TAB_KB, :, cols]
            new = st_ref[p] * sdb[:, cols] + kv_state(kd, v_ref[0, rows, cols])
            st_ref[p] = new
            sb_ref[nn - 1, p] = new.astype(BF16)
        return carry

    lax.fori_loop(0, n - 1, back, 0, unroll=unroll)

    st_ref[...] = jnp.zeros_like(st_ref)

    def fwd(nn, carry):
        rows = pl.ds(pl.multiple_of(nn * c, c), c)
        for p in range(n_pairs):
            cols = slice(p * LANES, (p + 1) * LANES)
            qq = q_ref[0, rows, cols]
            kk = k_ref[0, rows, cols]
            vv = v_ref[0, rows, cols]
            qf = qq.astype(F32)
            sf = st_ref[p]
            qcat = jnp.concatenate([(qf * tab_ref[RET_TAB_QF, :, cols]).astype(BF16),
                                    (qf * tab_ref[RET_TAB_QB, :, cols]).astype(BF16)], axis=1)
            scat = jnp.concatenate([sf.astype(BF16), sb_ref[nn, p]], axis=0)
            o = jnp.dot(qcat, scat, preferred_element_type=F32)
            k2 = jnp.concatenate([kk * hm for hm in head_masks], axis=0)
            v2 = jnp.concatenate([vv * hm for hm in head_masks], axis=0)
            sc = lax.dot_general(qq, k2, (((1,), (1,)), ((), ())), preferred_element_type=F32)
            o = o + jnp.dot((sc * dec_ref[p]).astype(BF16), v2, preferred_element_type=F32)
            s_lo = jnp.sum(jnp.where(lane_hi, 0.0, o), axis=-1, keepdims=True)
            s_hi = jnp.sum(jnp.where(lane_hi, o, 0.0), axis=-1, keepdims=True)
            mu = jnp.where(lane_hi, s_hi, s_lo) * (1.0 / HEAD_DIM)
            dev = o - mu
            d2 = dev * dev
            v_lo = jnp.sum(jnp.where(lane_hi, 0.0, d2), axis=-1, keepdims=True)
            v_hi = jnp.sum(jnp.where(lane_hi, d2, 0.0), axis=-1, keepdims=True)
            var = jnp.where(lane_hi, v_hi, v_lo) * (1.0 / HEAD_DIM)
            out = dev * lax.rsqrt(var + NORM_EPS) * gain_ref[:, cols] * g_ref[0, rows, cols].astype(F32)
            o_ref[0, rows, cols] = out.astype(BF16)
            kd = kk.astype(F32) * tab_ref[RET_TAB_KF, :, cols]
            st_ref[p] = sf * sdf[:, cols] + kv_state(kd, vv)
        return carry

    lax.fori_loop(0, n, fwd, 0, unroll=unroll)


def _retention(qr, kr, vr, gate, lg_f, lg_b, out_gain, width=512, unroll=2):
    b, s, w = qr.shape
    n_pairs = width // LANES
    n_heads = width // HEAD_DIM
    spec = pl.BlockSpec((1, s, width), lambda bi, p, *_: (bi, 0, p))
    grid_spec = pltpu.PrefetchScalarGridSpec(
        num_scalar_prefetch=2,
        grid=(b, w // width),
        in_specs=[spec, spec, spec, spec, pl.BlockSpec((1, width), lambda bi, p, *_: (0, p))],
        out_specs=spec,
        scratch_shapes=[pltpu.VMEM((4, RET_CHUNK, width), F32),
                        pltpu.VMEM((n_pairs, RET_CHUNK, (LANES // HEAD_DIM) * RET_CHUNK), F32),
                        pltpu.VMEM((s // RET_CHUNK, n_pairs, LANES, LANES), BF16),
                        pltpu.VMEM((n_pairs, LANES, LANES), F32)],
    )
    return pl.pallas_call(
        functools.partial(_retention_kernel, chunk=RET_CHUNK, unroll=unroll),
        grid_spec=grid_spec,
        out_shape=jax.ShapeDtypeStruct((b, s, w), BF16),
        compiler_params=_cparams(("arbitrary", "arbitrary")),
        name="retention",
    )(lg_f, lg_b, qr, kr, vr, gate, out_gain.reshape(1, w))


def _split_bf16(t):
    hi = t.astype(BF16)
    lo = (t - hi.astype(F32)).astype(BF16)
    return hi, lo


def _pack_bf16_pairs(t):
    n = t.shape[1] // 2
    hi = pltpu.bitcast(t[:, :n].astype(BF16).astype(F32), jnp.uint32)
    lo = pltpu.bitcast(t[:, n:].astype(BF16).astype(F32), jnp.uint32)
    return hi | (lo >> 16)


def _unpack_bf16_pairs(u):
    hi = pltpu.bitcast(u & jnp.uint32(0xFFFF0000), F32)
    lo = pltpu.bitcast(u << 16, F32)
    return jnp.concatenate([hi, lo], axis=1)


def _outproj_kernel(x_ref, o1_ref, o2_ref, o3_ref, l1_ref, l2_ref, l3_ref, orr_ref, ga_ref, expand_ref,
                    wout_ref, gf_ref, wr_hi_ref, wr_lo_ref, br_ref, h_ref, hn_ref, logit_ref,
                    *nat_refs):
    tm = x_ref.shape[0]
    gw = HEAD_GROUP_WIDTH
    os, ls = [], []
    for (_, dil), o_ref, l_ref in zip(DILATED_BRANCHES, (o1_ref, o2_ref, o3_ref), (l1_ref, l2_ref, l3_ref)):
        if dil == 1:
            os.append(o_ref[...].astype(F32))
            ls.append(l_ref[...])
            continue
        c = CLASS_DILATIONS.index(dil)
        onat_ref, lnat_ref = nat_refs[2 * c], nat_refs[2 * c + 1]
        for r in range(dil):
            rows = pl.ds(r, tm // dil, stride=dil)
            for g in range(gw // LANES):
                col = r * gw + g * LANES
                onat_ref[g, rows, :] = o_ref[:, col:col + LANES].astype(F32)
            lnat_ref[rows, :] = l_ref[:, r * LANES:(r + 1) * LANES]
        os.append(jnp.concatenate([onat_ref[g] for g in range(gw // LANES)], axis=1))
        ls.append(lnat_ref[...])
    mx = jnp.maximum(jnp.maximum(ls[0], ls[1]), ls[2])
    es = [jnp.exp2(l - mx) for l in ls]
    inv = 1.0 / (es[0] + es[1] + es[2])
    expand = expand_ref[...]
    oa = jnp.zeros((tm, gw), F32)
    for e, o in zip(es, os):
        w_hi, w_lo = _split_bf16(e * inv)
        wexp = (jnp.dot(w_hi, expand, preferred_element_type=F32)
                + jnp.dot(w_lo, expand, preferred_element_type=F32))
        oa = oa + wexp * o
    oa = oa * lax.rsqrt(jnp.mean(oa * oa, axis=-1, keepdims=True) + NORM_EPS) * ga_ref[...]
    mixed = jnp.concatenate([oa.astype(BF16), orr_ref[...]], axis=1)
    h = x_ref[...] + jnp.dot(mixed, wout_ref[...], preferred_element_type=F32)
    h_ref[...] = h
    hn = h * lax.rsqrt(jnp.mean(h * h, axis=-1, keepdims=True) + NORM_EPS) * gf_ref[...]
    hn_ref[...] = _pack_bf16_pairs(hn)
    hn_hi, hn_lo = _split_bf16(hn)
    wr_hi = wr_hi_ref[...]
    logit_ref[...] = (jnp.dot(hn_hi, wr_hi, preferred_element_type=F32)
                      + jnp.dot(hn_lo, wr_hi, preferred_element_type=F32)
                      + jnp.dot(hn_hi, wr_lo_ref[...], preferred_element_type=F32)
                      + br_ref[...])


def _outproj(x2, o_list, lse_list, orr, attn_gain, w_out_bf16, ffn_gain, w_router, b_router, tm):
    t, d = x2.shape
    w = HEAD_GROUP_WIDTH
    expand = np.zeros((LANES, w), np.float32)
    for hd in range(ATTN_HEADS):
        expand[hd, hd * HEAD_DIM:(hd + 1) * HEAD_DIM] = 1.0
    expand = jnp.asarray(expand, BF16)
    wr_hi = w_router.astype(BF16)
    wr_lo = (w_router - wr_hi.astype(F32)).astype(BF16)
    row = lambda width, dil=1: pl.BlockSpec((tm // dil, dil * width), lambda i: (i, 0))
    full = lambda a: pl.BlockSpec(a.shape, lambda i: (0,) * a.ndim)
    ga = attn_gain.reshape(1, w)
    gf = ffn_gain.reshape(1, d)
    dils = [dil for _, dil in DILATED_BRANCHES]
    o_flat = [o.reshape(t // dil, dil * w) for o, dil in zip(o_list, dils)]
    l_flat = [l.reshape(t // dil, dil * LANES) for l, dil in zip(lse_list, dils)]
    nat_scratch = []
    for _ in CLASS_DILATIONS:
        nat_scratch += [pltpu.VMEM((w // LANES, tm, LANES), F32), pltpu.VMEM((tm, LANES), F32)]
    return pl.pallas_call(
        _outproj_kernel,
        grid=(t // tm,),
        in_specs=[row(d)] + [row(w, dil) for dil in dils] + [row(LANES, dil) for dil in dils] + [row(w)]
                 + [full(ga), full(expand), full(w_out_bf16), full(gf), full(wr_hi), full(wr_lo), full(b_router)],
        out_specs=[row(d), row(d // 2), row(LANES)],
        out_shape=[jax.ShapeDtypeStruct((t, d), F32),
                   jax.ShapeDtypeStruct((t, d // 2), jnp.uint32),
                   jax.ShapeDtypeStruct((t, LANES), F32)],
        scratch_shapes=nat_scratch,
        compiler_params=_cparams(("arbitrary",)),
        name="outproj",
    )(x2, *o_flat, *l_flat, orr, ga, expand, w_out_bf16, gf, wr_hi, wr_lo, b_router)


ROUTE_E1, ROUTE_E2, ROUTE_G1, ROUTE_G2, ROUTE_R1, ROUTE_R2 = range(6)
EXPERT_LANE0 = MOE_GROUPS


def _route_kernel(logit_ref, tri_ref, route_ref, count_ref, run_ref):
    @pl.when(pl.program_id(0) == 0)
    def _():
        run_ref[...] = jnp.zeros_like(run_ref)

    lg = logit_ref[...]
    tm = lg.shape[0]
    lane = lax.broadcasted_iota(jnp.int32, lg.shape, 1)
    big = jnp.int32(1 << 20)

    def top(vals):
        m = jnp.max(vals, axis=-1, keepdims=True)
        i = jnp.min(jnp.where(vals == m, lane, big), axis=-1, keepdims=True)
        return m, i

    gl = jnp.where(lane < MOE_GROUPS, lg, -jnp.inf)
    gmax, gidx = top(gl)
    group_gate = 1.0 / jnp.sum(jnp.exp(gl - gmax), axis=-1, keepdims=True)
    lo = EXPERT_LANE0 + gidx * EXPERTS_PER_GROUP
    el = jnp.where((lane >= lo) & (lane < lo + EXPERTS_PER_GROUP), lg, -jnp.inf)
    t1, i1 = top(el)
    t2, i2 = top(jnp.where(lane == i1, -jnp.inf, el))
    e21 = jnp.exp(t2 - t1)
    g1 = group_gate / (1.0 + e21)
    g2 = group_gate * e21 / (1.0 + e21)
    e1 = i1 - EXPERT_LANE0
    e2 = i2 - EXPERT_LANE0
    oh1 = lane == e1
    oh2 = lane == e2
    cnt = oh1.astype(F32) + oh2.astype(F32)
    prefix = jnp.dot(tri_ref[...], cnt.astype(BF16), preferred_element_type=F32) + run_ref[0:1, :]
    r1 = jnp.sum(jnp.where(oh1, prefix, 0.0), axis=-1, keepdims=True)
    r2 = jnp.sum(jnp.where(oh2, prefix, 0.0), axis=-1, keepdims=True)
    new_run = run_ref[0:1, :] + jnp.sum(cnt, axis=0, keepdims=True)
    run_ref[...] = jnp.broadcast_to(new_run, run_ref.shape)
    count_ref[...] = jnp.broadcast_to(new_run, count_ref.shape)
    out = jnp.zeros(lg.shape, F32)
    for ln, val in ((ROUTE_E1, e1.astype(F32)), (ROUTE_E2, e2.astype(F32)), (ROUTE_G1, g1),
                    (ROUTE_G2, g2), (ROUTE_R1, r1), (ROUTE_R2, r2)):
        out = jnp.where(lane == ln, val, out)
    route_ref[...] = out


def _route(logits, tm):
    t = logits.shape[0]
    tri = jnp.asarray(np.tril(np.ones((tm, tm), np.float32), -1), BF16)
    return pl.pallas_call(
        _route_kernel,
        grid=(t // tm,),
        in_specs=[pl.BlockSpec((tm, LANES), lambda i: (i, 0)), pl.BlockSpec((tm, tm), lambda i: (0, 0))],
        out_specs=[pl.BlockSpec((tm, LANES), lambda i: (i, 0)), pl.BlockSpec((8, LANES), lambda i: (0, 0))],
        out_shape=[jax.ShapeDtypeStruct((t, LANES), F32), jax.ShapeDtypeStruct((8, LANES), F32)],
        scratch_shapes=[pltpu.VMEM((8, LANES), F32)],
        compiler_params=_cparams(("arbitrary",)),
        name="route",
    )(logits, tri)


TOP_K = 2
DMA_PRIORITIES = 2


SEG_ALIGN = 8
SORTED_TAIL = N_EXPERTS * SEG_ALIGN + MOE_BLOCK


def _tail_pieces():
    full, rest = divmod(SORTED_TAIL, MOE_BLOCK)
    return [MOE_BLOCK] * full + ([rest] if rest else [])


def _dispatch_kernel(seg_end_ref, pos_ref, hn_ref, xs_hbm, zbuf, sem, zsem, *, n_rows):
    tm = hn_ref.shape[0]

    @pl.when(pl.program_id(0) == 0)
    def _():
        zbuf[...] = jnp.zeros_like(zbuf)
        tail = pltpu.make_async_copy(zbuf, xs_hbm.at[pl.ds(n_rows, SORTED_TAIL)], zsem)
        tail.start()
        tail.wait()

        def hole(e):
            end = seg_end_ref[e]
            start = pl.multiple_of(end - (end & (SEG_ALIGN - 1)), SEG_ALIGN)
            return pltpu.make_async_copy(zbuf.at[pl.ds(0, SEG_ALIGN)], xs_hbm.at[pl.ds(start, SEG_ALIGN)], zsem)

        for e in range(N_EXPERTS):
            @pl.when((seg_end_ref[e] & (SEG_ALIGN - 1)) != 0)
            def _():
                hole(e).start()
        for e in range(N_EXPERTS):
            @pl.when((seg_end_ref[e] & (SEG_ALIGN - 1)) != 0)
            def _():
                hole(e).wait()

    for j in range(tm):
        for k in range(TOP_K):
            pltpu.make_async_copy(hn_ref.at[pl.ds(j, 1)], xs_hbm.at[pl.ds(pos_ref[0, 0, TOP_K * j + k], 1)],
                                  sem).start(priority=k % DMA_PRIORITIES)

    tile = pltpu.make_async_copy(hn_ref, xs_hbm.at[pl.ds(0, tm)], sem)
    for k in range(TOP_K):
        tile.wait()


def _dispatch(hn_packed, pos, seg_end, tm):
    t, dp = hn_packed.shape
    n_rows = TOP_K * t
    pos3 = pos.reshape(t // tm, 1, TOP_K * tm)
    grid_spec = pltpu.PrefetchScalarGridSpec(
        num_scalar_prefetch=1,
        grid=(t // tm,),
        in_specs=[pl.BlockSpec((1, 1, TOP_K * tm), lambda i, se: (i, 0, 0), memory_space=pltpu.SMEM),
                  pl.BlockSpec((tm, dp), lambda i, se: (i, 0))],
        out_specs=pl.BlockSpec(memory_space=pl.ANY),
        scratch_shapes=[pltpu.VMEM((SORTED_TAIL, dp), jnp.uint32),
                        pltpu.SemaphoreType.DMA,
                        pltpu.SemaphoreType.DMA],
    )
    return pl.pallas_call(
        functools.partial(_dispatch_kernel, n_rows=n_rows),
        grid_spec=grid_spec,
        out_shape=jax.ShapeDtypeStruct((n_rows + SORTED_TAIL, dp), jnp.uint32),
        compiler_params=_cparams(("arbitrary",)),
        name="dispatch",
    )(seg_end, pos3, hn_packed)


def _expert_kernel(bexp_ref, nreal_ref, row0_ref, xs_hbm, wg_ref, wu_ref, wd_ref, ys_hbm,
                   xbuf, ybuf, wg_bf, wu_bf, wd_bf, isem, osem, *, n_rows):
    i = pl.program_id(0)
    nb = pl.num_programs(0)
    slot = i % 2
    nslot = 1 - slot
    n_cur = nreal_ref[i]
    prev = jnp.maximum(i - 1, 0)
    nxt = jnp.minimum(i + 1, nb - 1)

    def in_copy(blk, s):
        row0 = pl.multiple_of(row0_ref[blk], SEG_ALIGN)
        return pltpu.make_async_copy(xs_hbm.at[pl.ds(row0, MOE_BLOCK)], xbuf.at[s], isem.at[s])

    def out_copy(blk, s):
        row0 = pl.multiple_of(row0_ref[blk], SEG_ALIGN)
        return pltpu.make_async_copy(ybuf.at[s], ys_hbm.at[pl.ds(row0, MOE_BLOCK)], osem.at[s])

    @pl.when(i == 0)
    def _():
        ybuf[...] = jnp.zeros_like(ybuf)
        tails = [pltpu.make_async_copy(ybuf.at[s, pl.ds(0, size)], ys_hbm.at[pl.ds(n_rows + s * MOE_BLOCK, size)],
                                       osem.at[s])
                 for s, size in enumerate(_tail_pieces())]
        for tail in tails:
            tail.start()
        for tail in tails:
            tail.wait()

        @pl.when(n_cur > 0)
        def _():
            in_copy(i, slot).start()

    @pl.when((i + 1 < nb) & (nreal_ref[nxt] > 0))
    def _():
        in_copy(nxt, nslot).start()

    @pl.when((n_cur > 0) & ((i == 0) | (bexp_ref[i] != bexp_ref[prev])))
    def _():
        wg_bf[...] = wg_ref[0].astype(BF16)
        wu_bf[...] = wu_ref[0].astype(BF16)
        wd_bf[...] = wd_ref[0].astype(BF16)

    @pl.when(n_cur > 0)
    def _():
        in_copy(i, slot).wait()
        xb = _unpack_bf16_pairs(xbuf[slot]).astype(BF16)
        gate = jnp.dot(xb, wg_bf[...], preferred_element_type=F32)
        up = jnp.dot(xb, wu_bf[...], preferred_element_type=F32)
        hid = (gate * jax.nn.sigmoid(gate) * up).astype(BF16)
        ybuf[slot] = _pack_bf16_pairs(jnp.dot(hid, wd_bf[...], preferred_element_type=F32))

    @pl.when((i >= 1) & (nreal_ref[prev] > 0))
    def _():
        out_copy(prev, nslot).wait()

    @pl.when(n_cur > 0)
    def _():
        out_copy(i, slot).start()

        @pl.when(i == nb - 1)
        def _():
            out_copy(i, slot).wait()


def _experts(xs, block_expert, block_nreal, block_row0, wg, wu, wd):
    n_blocks = block_expert.shape[0]
    assert n_blocks >= 2 and len(_tail_pieces()) <= 2
    n_rows_pad, dp = xs.shape
    _, d, ff = wg.shape
    grid_spec = pltpu.PrefetchScalarGridSpec(
        num_scalar_prefetch=3,
        grid=(n_blocks,),
        in_specs=[pl.BlockSpec(memory_space=pl.ANY),
                  pl.BlockSpec((1, d, ff), lambda i, be, nr, r0: (be[i], 0, 0)),
                  pl.BlockSpec((1, d, ff), lambda i, be, nr, r0: (be[i], 0, 0)),
                  pl.BlockSpec((1, ff, d), lambda i, be, nr, r0: (be[i], 0, 0))],
        out_specs=pl.BlockSpec(memory_space=pl.ANY),
        scratch_shapes=[pltpu.VMEM((2, MOE_BLOCK, dp), jnp.uint32),
                        pltpu.VMEM((2, MOE_BLOCK, dp), jnp.uint32),
                        pltpu.VMEM((d, ff), BF16),
                        pltpu.VMEM((d, ff), BF16),
                        pltpu.VMEM((ff, d), BF16),
                        pltpu.SemaphoreType.DMA((2,)),
                        pltpu.SemaphoreType.DMA((2,))],
    )
    return pl.pallas_call(
        functools.partial(_expert_kernel, n_rows=n_rows_pad - SORTED_TAIL),
        grid_spec=grid_spec,
        out_shape=jax.ShapeDtypeStruct((n_rows_pad, dp), jnp.uint32),
        compiler_params=_cparams(("arbitrary",)),
        name="experts",
    )(block_expert, block_nreal, block_row0, xs, wg, wu, wd)


def _final_kernel(pos_ref, h_ref, route_ref, gain_ref, ys_hbm, out_ref, ybuf, sem):
    i = pl.program_id(0)
    nt = pl.num_programs(0) - 1
    tm = h_ref.shape[0]

    for s in range(2):
        @pl.when((i < nt) & (i % 2 == s))
        def _():
            for j in range(tm):
                for k in range(TOP_K):
                    pltpu.make_async_copy(ys_hbm.at[pl.ds(pos_ref[0, 0, TOP_K * j + k], 1)],
                                          ybuf.at[s, k, pl.ds(j, 1)], sem.at[s]).start(priority=k % DMA_PRIORITIES)

    @pl.when(i >= 1)
    def _():
        slot = (i - 1) % 2
        for k in range(TOP_K):
            pltpu.make_async_copy(ys_hbm.at[pl.ds(0, tm)], ybuf.at[slot, k], sem.at[slot]).wait()
        r = route_ref[...]
        y = h_ref[...]
        for k, gate_lane in enumerate((ROUTE_G1, ROUTE_G2)):
            y = y + r[:, gate_lane:gate_lane + 1] * _unpack_bf16_pairs(ybuf[slot, k])
        out_ref[...] = y * lax.rsqrt(jnp.mean(y * y, axis=-1, keepdims=True) + NORM_EPS) * gain_ref[...]


def _final(h, ys, pos, route, gain, tm):
    t, d = h.shape
    nt = t // tm
    pos3 = pos.reshape(nt, 1, TOP_K * tm)
    done = lambda i: (jnp.maximum(i - 1, 0), 0)
    return pl.pallas_call(
        _final_kernel,
        grid=(nt + 1,),
        in_specs=[pl.BlockSpec((1, 1, TOP_K * tm), lambda i: (jnp.minimum(i, nt - 1), 0, 0),
                               memory_space=pltpu.SMEM),
                  pl.BlockSpec((tm, d), done),
                  pl.BlockSpec((tm, LANES), done),
                  pl.BlockSpec((1, d), lambda i: (0, 0)),
                  pl.BlockSpec(memory_space=pl.ANY)],
        out_specs=pl.BlockSpec((tm, d), done),
        out_shape=jax.ShapeDtypeStruct((t, d), F32),
        scratch_shapes=[pltpu.VMEM((2, TOP_K, tm, d // 2), jnp.uint32),
                        pltpu.SemaphoreType.DMA((2,))],
        compiler_params=_cparams(("arbitrary",)),
        name="final",
    )(pos3, h, route, gain.reshape(1, d), ys)


def _layer(h3, mix_gain, w_in, attn_gain, decay_f, decay_b, ret_gain, w_out, ffn_gain,
           w_rg, b_rg, w_re, b_re, w_eg, w_eu, w_ed, final_gain):
    b, s, d = h3.shape
    t = b * s
    tm = 512
    attn_qkv, (qr, kr, vr, gr) = _inproj(h3, mix_gain, w_in.astype(BF16), tm)

    o_list, lse_list = [], []
    for window, dilation in DILATED_BRANCHES:
        reach = (window // 2) // dilation
        o, lse = _attn_branch(*attn_qkv[dilation], dilation, reach, tq=128, tqb=512)
        o_list.append(o)
        lse_list.append(lse)

    lg_f = jnp.log1p(-jnp.exp2(decay_f.astype(F32)))
    lg_b = jnp.log1p(-jnp.exp2(decay_b.astype(F32)))
    orr = _retention(qr, kr, vr, gr, lg_f, lg_b, ret_gain).reshape(t, HEAD_GROUP_WIDTH)

    n_route = MOE_GROUPS + N_EXPERTS
    w_router = jnp.zeros((d, LANES), F32).at[:, :n_route].set(jnp.concatenate([w_rg, w_re], axis=1).astype(F32))
    b_router = jnp.zeros((1, LANES), F32).at[0, :n_route].set(jnp.concatenate([b_rg, b_re]).astype(F32))
    h, hn_packed, logits = _outproj(h3.reshape(t, d), o_list, lse_list, orr, attn_gain, w_out.astype(BF16),
                                    ffn_gain, w_router, b_router, tm)

    route, counts8 = _route(logits, tm)

    n_blocks = -(-TOP_K * t // MOE_BLOCK) + N_EXPERTS
    counts = counts8[0, :N_EXPERTS].astype(jnp.int32)
    aligned = ((counts + SEG_ALIGN - 1) // SEG_ALIGN) * SEG_ALIGN
    seg_start = jnp.cumsum(aligned) - aligned
    expert_iota = jnp.arange(N_EXPERTS, dtype=jnp.int32)
    e12 = route[:, ROUTE_E1:ROUTE_E2 + 1].astype(jnp.int32)
    r12 = route[:, ROUTE_R1:ROUTE_R2 + 1].astype(jnp.int32)
    pos = r12 + jnp.sum(jnp.where(e12[..., None] == expert_iota, seg_start, 0), axis=-1)
    nblk = (counts + MOE_BLOCK - 1) // MOE_BLOCK
    blk_end = jnp.cumsum(nblk)
    blk_start = blk_end - nblk
    blk = jnp.arange(n_blocks, dtype=jnp.int32)[:, None]
    owner = (blk >= blk_start) & (blk < blk_end)
    local = (blk - blk_start) * MOE_BLOCK
    block_row0 = jnp.sum(jnp.where(owner, seg_start + local, 0), axis=-1).astype(jnp.int32)
    block_nreal = jnp.sum(jnp.where(owner, jnp.clip(counts - local, 0, MOE_BLOCK), 0), axis=-1).astype(jnp.int32)
    block_expert = jnp.minimum(jnp.sum((blk >= blk_end).astype(jnp.int32), axis=-1), N_EXPERTS - 1)

    xs = _dispatch(hn_packed, pos, seg_start + counts, tm)
    ys = _experts(xs, block_expert, block_nreal, block_row0, w_eg.astype(F32), w_eu.astype(F32),
                  w_ed.astype(F32))
    out = _final(h, ys, pos, route, final_gain, tm)
    return out.reshape(b, s, d)


def kernel(x, mix_norm_gain, w_in, attn_out_gain, ret_decay_fwd, ret_decay_bwd, ret_out_gain, w_out,
           ffn_norm_gain, w_route_group, b_route_group, w_route_expert, b_route_expert,
           w_expert_gate, w_expert_up, w_expert_down, final_norm_gain):
    depth = mix_norm_gain.shape[0]
    assert depth == 1, "the final rmsnorm is fused into the single layer's combine kernel"
    l = 0
    return _layer(x, mix_norm_gain[l], w_in[l], attn_out_gain[l], ret_decay_fwd[l], ret_decay_bwd[l],
                  ret_out_gain[l], w_out[l], ffn_norm_gain[l], w_route_group[l], b_route_group[l],
                  w_route_expert[l], b_route_expert[l], w_expert_gate[l], w_expert_up[l], w_expert_down[l],
                  final_norm_gain)
```

```python
import functools

import numpy as np
import jax
import jax.numpy as jnp
from jax import lax
from jax.experimental import pallas as pl
from jax.experimental.pallas import tpu as pltpu

F32 = jnp.float32
BF16 = jnp.bfloat16

ATTN_HEADS = 8
HEAD_DIM = 64
RET_HEADS = 8
HEAD_GROUP_WIDTH = 512
N_PROJ_GROUPS = 7
DILATED_BRANCHES = ((128, 1), (512, 4), (2048, 16))
ROPE_THETA = 500000.0
ROPE_DIM = HEAD_DIM // 4
RET_THETA = 10000.0
RET_CHUNK = 128
MOE_GROUPS = 4
EXPERTS_PER_GROUP = 8
N_EXPERTS = MOE_GROUPS * EXPERTS_PER_GROUP
MOE_BLOCK = 512
NORM_EPS = 1e-6
NEG_INF = -1e30

LANES = 128
VMEM_LIMIT = 56 * 1024 * 1024


def _cparams(sem):
    return pltpu.CompilerParams(dimension_semantics=sem, vmem_limit_bytes=VMEM_LIMIT)


def _rotary_tables(seq, half, freqs):
    pos = np.arange(seq, dtype=np.float64)[:, None]
    ang = pos * freqs[None, :].astype(np.float64)
    cos, sin = np.cos(ang), np.sin(ang)
    c = np.ones((seq, HEAD_DIM)); sp = np.zeros((seq, HEAD_DIM)); sm = np.zeros((seq, HEAD_DIM))
    c[:, :half] = cos; c[:, half:2 * half] = cos
    sp[:, half:2 * half] = sin
    sm[:, :half] = -sin
    rep = LANES // HEAD_DIM
    return tuple(jnp.asarray(np.tile(t, (1, rep)), F32) for t in (c, sp, sm))


def _rotate(t, c, sp, sm, half):
    outs = []
    for g in range(t.shape[1] // LANES):
        tg = t[:, g * LANES:(g + 1) * LANES]
        outs.append(tg * c + pltpu.roll(tg, half, 1) * sp + pltpu.roll(tg, LANES - half, 1) * sm)
    return jnp.concatenate(outs, axis=1)


CLASS_DILATIONS = tuple(d for _, d in DILATED_BRANCHES if d > 1)
ATTN_Q_SCALE = float(np.log2(np.e)) * HEAD_DIM ** -0.5


def _inproj_kernel(x_ref, gain_ref, w_ref, ca_ref, spa_ref, sma_ref, cr_ref, spr_ref, smr_ref, *rest):
    n_cls = len(CLASS_DILATIONS)
    nat_refs = rest[0:3]
    cls_refs = [rest[3 + 3 * c:6 + 3 * c] for c in range(n_cls)]
    qr_ref, kr_ref, vr_ref, gr_ref = rest[3 + 3 * n_cls:7 + 3 * n_cls]
    stage_ref = rest[7 + 3 * n_cls]
    x = x_ref[0]
    tm = x.shape[0]
    ms = jnp.mean(x * x, axis=-1, keepdims=True)
    xn = (x * lax.rsqrt(ms + NORM_EPS) * gain_ref[...]).astype(BF16)
    gw = HEAD_GROUP_WIDTH

    def proj(c):
        return jnp.dot(xn, w_ref[:, c * gw:(c + 1) * gw], preferred_element_type=F32)

    a_tabs = (ca_ref[...], spa_ref[...], sma_ref[...])
    r_tabs = (cr_ref[...], spr_ref[...], smr_ref[...])
    attn_vals = ((_rotate(proj(0), *a_tabs, ROPE_DIM // 2) * ATTN_Q_SCALE),
                 _rotate(proj(1), *a_tabs, ROPE_DIM // 2),
                 proj(2))
    for j, val in enumerate(attn_vals):
        nat_refs[j][0] = val.astype(BF16)
        for g in range(gw // LANES):
            stage_ref[g] = val[:, g * LANES:(g + 1) * LANES]
        for c, d in enumerate(CLASS_DILATIONS):
            for r in range(d):
                for g in range(gw // LANES):
                    col = r * gw + g * LANES
                    cls_refs[c][j][0, :, col:col + LANES] = (
                        stage_ref[g, pl.ds(r, tm // d, stride=d), :].astype(BF16))
    qr_ref[0] = _rotate(proj(3), *r_tabs, HEAD_DIM // 2).astype(BF16)
    kr_ref[0] = (_rotate(proj(4), *r_tabs, HEAD_DIM // 2) * (HEAD_DIM ** -0.5)).astype(BF16)
    vr_ref[0] = proj(5).astype(BF16)
    g = proj(6)
    gr_ref[0] = (g * jax.nn.sigmoid(g)).astype(BF16)


def _inproj(x, gain, w_in_bf16, tm):
    b, s, d = x.shape
    rope_freqs = ROPE_THETA ** (-np.arange(0, ROPE_DIM, 2, dtype=np.float32) / ROPE_DIM)
    ret_freqs = RET_THETA ** (-np.linspace(0.0, 1.0, HEAD_DIM // 2, dtype=np.float32))
    tabs = _rotary_tables(s, ROPE_DIM // 2, rope_freqs) + _rotary_tables(s, HEAD_DIM // 2, ret_freqs)
    gw = HEAD_GROUP_WIDTH
    tab_spec = pl.BlockSpec((tm, LANES), lambda si, bi: (si, 0))

    def view(dil):
        return (pl.BlockSpec((1, tm // dil, dil * gw), lambda si, bi: (bi, si, 0)),
                jax.ShapeDtypeStruct((b, s // dil, dil * gw), BF16))

    views = [view(1)] * 3 + [view(dil) for dil in CLASS_DILATIONS for _ in range(3)] + [view(1)] * 4
    outs = pl.pallas_call(
        _inproj_kernel,
        grid=(s // tm, b),
        in_specs=[pl.BlockSpec((1, tm, d), lambda si, bi: (bi, si, 0)),
                  pl.BlockSpec((1, d), lambda si, bi: (0, 0)),
                  pl.BlockSpec(w_in_bf16.shape, lambda si, bi: (0, 0))] + [tab_spec] * 6,
        out_specs=[v[0] for v in views],
        out_shape=[v[1] for v in views],
        scratch_shapes=[pltpu.VMEM((gw // LANES, tm, LANES), F32)],
        compiler_params=_cparams(("arbitrary", "arbitrary")),
        name="inproj",
    )(x, gain.reshape(1, d), w_in_bf16, *tabs)
    n_attn = 3 * (1 + len(CLASS_DILATIONS))
    attn_qkv = {dil: outs[3 * c:3 * c + 3] for c, dil in enumerate((1,) + CLASS_DILATIONS)}
    return attn_qkv, outs[n_attn:]


ATTN_WINDOW_CASES = 3


def _attn_kernel(q_ref, k_ref, v_ref, o_ref, lse_ref, bias_ref, *, length, tq, reach):
    tqb = q_ref.shape[1]
    win = tq + 2 * reach
    heads_per_pair = LANES // HEAD_DIM
    qi = pl.program_id(2)
    lane = lax.broadcasted_iota(jnp.int32, (1, LANES), 1)
    lane_t = lax.broadcasted_iota(jnp.int32, (tq, LANES), 1)
    ones = jnp.ones((win, LANES), BF16)

    @pl.when((pl.program_id(0) == 0) & (pl.program_id(1) == 0) & (qi == 0))
    def _():
        diff = (lax.broadcasted_iota(jnp.int32, (heads_per_pair * tq, win), 1)
                - (lax.broadcasted_iota(jnp.int32, (heads_per_pair * tq, win), 0) & (tq - 1)))
        for case in range(ATTN_WINDOW_CASES):
            off = case * reach
            bias_ref[case] = jnp.where((diff >= off - reach) & (diff <= off + reach), 0.0, NEG_INF)

    def sub(t, cls):
        q0 = qi * tqb + t * tq
        ws = jnp.clip(q0 - reach, 0, length - win)
        ws = pl.multiple_of(ws, reach)
        bias = bias_ref[(q0 - ws) // reach]
        rows = pl.ds(t * tq, tq)
        m_tile = jnp.zeros((tq, LANES), F32)
        l_tile = jnp.ones((tq, LANES), F32)
        for g in range(HEAD_GROUP_WIDTH // LANES):
            cols = slice(cls * HEAD_GROUP_WIDTH + g * LANES, cls * HEAD_GROUP_WIDTH + (g + 1) * LANES)
            qg = q_ref[0, rows, cols]
            kw = k_ref[0, pl.ds(ws, win), cols]
            v_ones = jnp.concatenate([v_ref[0, pl.ds(ws, win), cols], ones], axis=1)
            hms = [(lane >= hh * HEAD_DIM) & (lane < (hh + 1) * HEAD_DIM) for hh in range(heads_per_pair)]
            q2 = jnp.concatenate([qg * hm.astype(BF16) for hm in hms], axis=0)
            sc = lax.dot_general(q2, kw, (((1,), (1,)), ((), ())), preferred_element_type=F32) + bias
            m = jnp.max(sc, axis=-1, keepdims=True)
            p = jnp.exp2(sc - m).astype(BF16)
            pv = jnp.dot(p, v_ones, preferred_element_type=F32)
            l = pv[:, LANES:]
            o = pv[:, :LANES] / l
            o_pair = jnp.zeros((tq, LANES), F32)
            for hh in range(heads_per_pair):
                part = slice(hh * tq, (hh + 1) * tq)
                head_lane = lane_t == g * heads_per_pair + hh
                o_pair = jnp.where(hms[hh], o[part], o_pair)
                m_tile = jnp.where(head_lane, m[part], m_tile)
                l_tile = jnp.where(head_lane, l[part], l_tile)
            o_ref[0, rows, cols] = o_pair.astype(BF16)
        lse_ref[0, rows, cls * LANES:(cls + 1) * LANES] = m_tile + jnp.log2(l_tile)

    for cls in range(q_ref.shape[2] // HEAD_GROUP_WIDTH):
        for t in range(tqb // tq):
            sub(t, cls)


def _attn_branch(qc, kc, vc, dilation, reach, tq, tqb, sub_tiles=4):
    b, length, dw = qc.shape
    w = dw // dilation
    tqb = min(tqb, length)
    assert tq % reach == 0 and tq > reach and length % tq == 0 and length >= tq + 2 * reach
    ncls = min(dilation, max(1, sub_tiles // (tqb // tq)))
    q_spec = pl.BlockSpec((1, tqb, ncls * w), lambda bi, r, qi: (bi, qi, r))
    kv_spec = pl.BlockSpec((1, length, ncls * w), lambda bi, r, qi: (bi, 0, r))
    o, lse = pl.pallas_call(
        functools.partial(_attn_kernel, length=length, tq=tq, reach=reach),
        grid=(b, dilation // ncls, length // tqb),
        in_specs=[q_spec, kv_spec, kv_spec],
        out_specs=[q_spec, pl.BlockSpec((1, tqb, ncls * LANES), lambda bi, r, qi: (bi, qi, r))],
        out_shape=[jax.ShapeDtypeStruct((b, length, dilation * w), BF16),
                   jax.ShapeDtypeStruct((b, length, dilation * LANES), F32)],
        scratch_shapes=[pltpu.VMEM((ATTN_WINDOW_CASES, (LANES // HEAD_DIM) * tq, tq + 2 * reach), F32)],
        compiler_params=_cparams(("arbitrary", "arbitrary", "arbitrary")),
        name=f"attn_d{dilation}",
    )(qc, kc, vc)
    return o, lse


RET_TAB_QF, RET_TAB_QB, RET_TAB_KF, RET_TAB_KB = range(4)


def _retention_kernel(lgf_ref, lgb_ref, q_ref, k_ref, v_ref, g_ref, gain_ref, o_ref,
                      tab_ref, dec_ref, sb_ref, st_ref, *, chunk, unroll):
    c = chunk
    n = q_ref.shape[1] // c
    width = q_ref.shape[2]
    n_pairs = width // LANES
    heads_per_pair = LANES // HEAD_DIM
    n_heads = n_pairs * heads_per_pair
    head0 = pl.program_id(1) * n_heads
    lane_w = lax.broadcasted_iota(jnp.int32, (1, width), 1)
    lgf = [lgf_ref[head0 + hd] for hd in range(n_heads)]
    lgb = [lgb_ref[head0 + hd] for hd in range(n_heads)]
    lgf_lane = jnp.zeros((1, width), F32)
    lgb_lane = jnp.zeros((1, width), F32)
    for hd in range(n_heads):
        in_head = (lane_w >= hd * HEAD_DIM) & (lane_w < (hd + 1) * HEAD_DIM)
        lgf_lane = jnp.where(in_head, lgf[hd], lgf_lane)
        lgb_lane = jnp.where(in_head, lgb[hd], lgb_lane)
    idx = lax.broadcasted_iota(jnp.int32, (c, width), 0).astype(F32)
    tab_ref[RET_TAB_QF] = jnp.exp((idx + 1.0) * lgf_lane)
    tab_ref[RET_TAB_QB] = jnp.exp((c - idx) * lgb_lane)
    tab_ref[RET_TAB_KF] = jnp.exp((c - 1.0 - idx) * lgf_lane)
    tab_ref[RET_TAB_KB] = jnp.exp(idx * lgb_lane)
    sdf = jnp.exp(c * lgf_lane)
    sdb = jnp.exp(c * lgb_lane)
    dmat = (lax.broadcasted_iota(jnp.int32, (c, c), 0)
            - lax.broadcasted_iota(jnp.int32, (c, c), 1)).astype(F32)
    for hd in range(n_heads):
        dec_ref[hd // heads_per_pair, :, (hd % heads_per_pair) * c:(hd % heads_per_pair + 1) * c] = (
            jnp.where(dmat >= 0, jnp.exp(dmat * lgf[hd]), jnp.exp(-dmat * lgb[hd])))
    lane = lax.broadcasted_iota(jnp.int32, (1, LANES), 1)
    lane_hi = lane >= HEAD_DIM
    head_masks = [((lane >= hh * HEAD_DIM) & (lane < (hh + 1) * HEAD_DIM)).astype(BF16)
                  for hh in range(heads_per_pair)]
    row_hi = lax.broadcasted_iota(jnp.int32, (LANES, LANES), 0) >= HEAD_DIM
    col_hi = lax.broadcasted_iota(jnp.int32, (LANES, LANES), 1) >= HEAD_DIM
    blockdiag = row_hi == col_hi

    def kv_state(kd, vv):
        kt = jnp.transpose(kd).astype(BF16)
        return jnp.where(blockdiag, jnp.dot(kt, vv, preferred_element_type=F32), 0.0)

    st_ref[...] = jnp.zeros_like(st_ref)
    sb_ref[n - 1] = jnp.zeros(sb_ref.shape[1:], sb_ref.dtype)

    def back(i, carry):
        nn = n - 1 - i
        rows = pl.ds(pl.multiple_of(nn * c, c), c)
        for p in range(n_pairs):
            cols = slice(p * LANES, (p + 1) * LANES)
            kd = k_ref[0, rows, cols].astype(F32) * tab_ref[RET_TAB_KB, :, cols]
            new = st_ref[p] * sdb[:, cols] + kv_state(kd, v_ref[0, rows, cols])
            st_ref[p] = new
            sb_ref[nn - 1, p] = new.astype(BF16)
        return carry

    lax.fori_loop(0, n - 1, back, 0, unroll=unroll)

    st_ref[...] = jnp.zeros_like(st_ref)

    def fwd(nn, carry):
        rows = pl.ds(pl.multiple_of(nn * c, c), c)
        for p in range(n_pairs):
            cols = slice(p * LANES, (p + 1) * LANES)
            qq = q_ref[0, rows, cols]
            kk = k_ref[0, rows, cols]
            vv = v_ref[0, rows, cols]
            qf = qq.astype(F32)
            sf = st_ref[p]
            qcat = jnp.concatenate([(qf * tab_ref[RET_TAB_QF, :, cols]).astype(BF16),
                                    (qf * tab_ref[RET_TAB_QB, :, cols]).astype(BF16)], axis=1)
            scat = jnp.concatenate([sf.astype(BF16), sb_ref[nn, p]], axis=0)
            o = jnp.dot(qcat, scat, preferred_element_type=F32)
            k2 = jnp.concatenate([kk * hm for hm in head_masks], axis=0)
            v2 = jnp.concatenate([vv * hm for hm in head_masks], axis=0)
            sc = lax.dot_general(qq, k2, (((1,), (1,)), ((), ())), preferred_element_type=F32)
            o = o + jnp.dot((sc * dec_ref[p]).astype(BF16), v2, preferred_element_type=F32)
            s_lo = jnp.sum(jnp.where(lane_hi, 0.0, o), axis=-1, keepdims=True)
            s_hi = jnp.sum(jnp.where(lane_hi, o, 0.0), axis=-1, keepdims=True)
            mu = jnp.where(lane_hi, s_hi, s_lo) * (1.0 / HEAD_DIM)
            dev = o - mu
            d2 = dev * dev
            v_lo = jnp.sum(jnp.where(lane_hi, 0.0, d2), axis=-1, keepdims=True)
            v_hi = jnp.sum(jnp.where(lane_hi, d2, 0.0), axis=-1, keepdims=True)
            var = jnp.where(lane_hi, v_hi, v_lo) * (1.0 / HEAD_DIM)
            out = dev * lax.rsqrt(var + NORM_EPS) * gain_ref[:, cols] * g_ref[0, rows, cols].astype(F32)
            o_ref[0, rows, cols] = out.astype(BF16)
            kd = kk.astype(F32) * tab_ref[RET_TAB_KF, :, cols]
            st_ref[p] = sf * sdf[:, cols] + kv_state(kd, vv)
        return carry

    lax.fori_loop(0, n, fwd, 0, unroll=unroll)


def _retention(qr, kr, vr, gate, lg_f, lg_b, out_gain, width=512, unroll=4):
    b, s, w = qr.shape
    n_pairs = width // LANES
    n_heads = width // HEAD_DIM
    spec = pl.BlockSpec((1, s, width), lambda bi, p, *_: (bi, 0, p))
    grid_spec = pltpu.PrefetchScalarGridSpec(
        num_scalar_prefetch=2,
        grid=(b, w // width),
        in_specs=[spec, spec, spec, spec, pl.BlockSpec((1, width), lambda bi, p, *_: (0, p))],
        out_specs=spec,
        scratch_shapes=[pltpu.VMEM((4, RET_CHUNK, width), F32),
                        pltpu.VMEM((n_pairs, RET_CHUNK, (LANES // HEAD_DIM) * RET_CHUNK), F32),
                        pltpu.VMEM((s // RET_CHUNK, n_pairs, LANES, LANES), BF16),
                        pltpu.VMEM((n_pairs, LANES, LANES), F32)],
    )
    return pl.pallas_call(
        functools.partial(_retention_kernel, chunk=RET_CHUNK, unroll=unroll),
        grid_spec=grid_spec,
        out_shape=jax.ShapeDtypeStruct((b, s, w), BF16),
        compiler_params=_cparams(("arbitrary", "arbitrary")),
        name="retention",
    )(lg_f, lg_b, qr, kr, vr, gate, out_gain.reshape(1, w))


def _split_bf16(t):
    hi = t.astype(BF16)
    lo = (t - hi.astype(F32)).astype(BF16)
    return hi, lo


def _pack_bf16_pairs(t):
    n = t.shape[1] // 2
    hi = pltpu.bitcast(t[:, :n].astype(BF16).astype(F32), jnp.uint32)
    lo = pltpu.bitcast(t[:, n:].astype(BF16).astype(F32), jnp.uint32)
    return hi | (lo >> 16)


def _unpack_bf16_pairs(u):
    hi = pltpu.bitcast(u & jnp.uint32(0xFFFF0000), F32)
    lo = pltpu.bitcast(u << 16, F32)
    return jnp.concatenate([hi, lo], axis=1)


def _outproj_kernel(x_ref, o1_ref, o2_ref, o3_ref, l1_ref, l2_ref, l3_ref, orr_ref, ga_ref, expand_ref,
                    wout_ref, gf_ref, wr_ref, br_ref, h_ref, hn_ref, logit_ref,
                    *nat_refs):
    tm = x_ref.shape[0]
    gw = HEAD_GROUP_WIDTH
    os, ls = [], []
    for (_, dil), o_ref, l_ref in zip(DILATED_BRANCHES, (o1_ref, o2_ref, o3_ref), (l1_ref, l2_ref, l3_ref)):
        if dil == 1:
            os.append(o_ref[...].astype(F32))
            ls.append(l_ref[...])
            continue
        c = CLASS_DILATIONS.index(dil)
        onat_ref, lnat_ref = nat_refs[2 * c], nat_refs[2 * c + 1]
        for r in range(dil):
            rows = pl.ds(r, tm // dil, stride=dil)
            for g in range(gw // LANES):
                col = r * gw + g * LANES
                onat_ref[g, rows, :] = o_ref[:, col:col + LANES].astype(F32)
            lnat_ref[rows, :] = l_ref[:, r * LANES:(r + 1) * LANES]
        os.append(jnp.concatenate([onat_ref[g] for g in range(gw // LANES)], axis=1))
        ls.append(lnat_ref[...])
    mx = jnp.maximum(jnp.maximum(ls[0], ls[1]), ls[2])
    es = [jnp.exp2(l - mx) for l in ls]
    inv = 1.0 / (es[0] + es[1] + es[2])
    expand = expand_ref[...]
    oa = jnp.zeros((tm, gw), F32)
    for e, o in zip(es, os):
        wexp = jnp.dot(jnp.concatenate(_split_bf16(e * inv), axis=1), expand, preferred_element_type=F32)
        oa = oa + wexp * o
    oa = oa * lax.rsqrt(jnp.mean(oa * oa, axis=-1, keepdims=True) + NORM_EPS) * ga_ref[...]
    mixed = jnp.concatenate([oa.astype(BF16), orr_ref[...]], axis=1)
    h = x_ref[...] + jnp.dot(mixed, wout_ref[...], preferred_element_type=F32)
    h_ref[...] = h
    hn = h * lax.rsqrt(jnp.mean(h * h, axis=-1, keepdims=True) + NORM_EPS) * gf_ref[...]
    hn_ref[...] = _pack_bf16_pairs(hn)
    prod = jnp.dot(jnp.concatenate(_split_bf16(hn), axis=0), wr_ref[...], preferred_element_type=F32)
    logit_ref[...] = prod[:tm, :LANES] + prod[:tm, LANES:] + prod[tm:, :LANES] + br_ref[...]


def _outproj(x2, o_list, lse_list, orr, attn_gain, w_out_bf16, ffn_gain, w_router, b_router, tm):
    t, d = x2.shape
    w = HEAD_GROUP_WIDTH
    expand = np.zeros((LANES, w), np.float32)
    for hd in range(ATTN_HEADS):
        expand[hd, hd * HEAD_DIM:(hd + 1) * HEAD_DIM] = 1.0
    expand = jnp.asarray(np.concatenate([expand, expand], axis=0), BF16)
    wr_hi = w_router.astype(BF16)
    wr = jnp.concatenate([wr_hi, (w_router - wr_hi.astype(F32)).astype(BF16)], axis=1)
    row = lambda width, dil=1: pl.BlockSpec((tm // dil, dil * width), lambda i: (i, 0))
    full = lambda a: pl.BlockSpec(a.shape, lambda i: (0,) * a.ndim)
    ga = attn_gain.reshape(1, w)
    gf = ffn_gain.reshape(1, d)
    dils = [dil for _, dil in DILATED_BRANCHES]
    o_flat = [o.reshape(t // dil, dil * w) for o, dil in zip(o_list, dils)]
    l_flat = [l.reshape(t // dil, dil * LANES) for l, dil in zip(lse_list, dils)]
    nat_scratch = []
    for _ in CLASS_DILATIONS:
        nat_scratch += [pltpu.VMEM((w // LANES, tm, LANES), F32), pltpu.VMEM((tm, LANES), F32)]
    return pl.pallas_call(
        _outproj_kernel,
        grid=(t // tm,),
        in_specs=[row(d)] + [row(w, dil) for dil in dils] + [row(LANES, dil) for dil in dils] + [row(w)]
                 + [full(ga), full(expand), full(w_out_bf16), full(gf), full(wr), full(b_router)],
        out_specs=[row(d), row(d // 2), row(LANES)],
        out_shape=[jax.ShapeDtypeStruct((t, d), F32),
                   jax.ShapeDtypeStruct((t, d // 2), jnp.uint32),
                   jax.ShapeDtypeStruct((t, LANES), F32)],
        scratch_shapes=nat_scratch,
        compiler_params=_cparams(("arbitrary",)),
        name="outproj",
    )(x2, *o_flat, *l_flat, orr, ga, expand, w_out_bf16, gf, wr, b_router)


ROUTE_E1, ROUTE_E2, ROUTE_G1, ROUTE_G2, ROUTE_R1, ROUTE_R2 = range(6)
EXPERT_LANE0 = MOE_GROUPS


def _route_kernel(logit_ref, tri_ref, route_ref, count_ref, run_ref):
    @pl.when(pl.program_id(0) == 0)
    def _():
        run_ref[...] = jnp.zeros_like(run_ref)

    lg = logit_ref[...]
    tm = lg.shape[0]
    lane = lax.broadcasted_iota(jnp.int32, lg.shape, 1)
    big = jnp.int32(1 << 20)

    def top(vals):
        m = jnp.max(vals, axis=-1, keepdims=True)
        i = jnp.min(jnp.where(vals == m, lane, big), axis=-1, keepdims=True)
        return m, i

    gl = jnp.where(lane < MOE_GROUPS, lg, -jnp.inf)
    gmax, gidx = top(gl)
    group_gate = 1.0 / jnp.sum(jnp.exp(gl - gmax), axis=-1, keepdims=True)
    lo = EXPERT_LANE0 + gidx * EXPERTS_PER_GROUP
    el = jnp.where((lane >= lo) & (lane < lo + EXPERTS_PER_GROUP), lg, -jnp.inf)
    t1, i1 = top(el)
    t2, i2 = top(jnp.where(lane == i1, -jnp.inf, el))
    e21 = jnp.exp(t2 - t1)
    g1 = group_gate / (1.0 + e21)
    g2 = group_gate * e21 / (1.0 + e21)
    e1 = i1 - EXPERT_LANE0
    e2 = i2 - EXPERT_LANE0
    oh1 = lane == e1
    oh2 = lane == e2
    cnt = oh1.astype(F32) + oh2.astype(F32)
    prefix = jnp.dot(tri_ref[...], cnt.astype(BF16), preferred_element_type=F32) + run_ref[0:1, :]
    r1 = jnp.sum(jnp.where(oh1, prefix, 0.0), axis=-1, keepdims=True)
    r2 = jnp.sum(jnp.where(oh2, prefix, 0.0), axis=-1, keepdims=True)
    new_run = run_ref[0:1, :] + jnp.sum(cnt, axis=0, keepdims=True)
    run_ref[...] = jnp.broadcast_to(new_run, run_ref.shape)
    count_ref[...] = jnp.broadcast_to(new_run, count_ref.shape)
    out = jnp.zeros(lg.shape, F32)
    for ln, val in ((ROUTE_E1, e1.astype(F32)), (ROUTE_E2, e2.astype(F32)), (ROUTE_G1, g1),
                    (ROUTE_G2, g2), (ROUTE_R1, r1), (ROUTE_R2, r2)):
        out = jnp.where(lane == ln, val, out)
    route_ref[...] = out


def _route(logits, tm):
    t = logits.shape[0]
    tri = jnp.asarray(np.tril(np.ones((tm, tm), np.float32), -1), BF16)
    return pl.pallas_call(
        _route_kernel,
        grid=(t // tm,),
        in_specs=[pl.BlockSpec((tm, LANES), lambda i: (i, 0)), pl.BlockSpec((tm, tm), lambda i: (0, 0))],
        out_specs=[pl.BlockSpec((tm, LANES), lambda i: (i, 0)), pl.BlockSpec((8, LANES), lambda i: (0, 0))],
        out_shape=[jax.ShapeDtypeStruct((t, LANES), F32), jax.ShapeDtypeStruct((8, LANES), F32)],
        scratch_shapes=[pltpu.VMEM((8, LANES), F32)],
        compiler_params=_cparams(("arbitrary",)),
        name="route",
    )(logits, tri)


TOP_K = 2
DMA_PRIORITIES = 2


SEG_ALIGN = 8
SORTED_TAIL = N_EXPERTS * SEG_ALIGN + MOE_BLOCK


def _tail_pieces():
    full, rest = divmod(SORTED_TAIL, MOE_BLOCK)
    return [MOE_BLOCK] * full + ([rest] if rest else [])


def _dispatch_kernel(seg_end_ref, pos_ref, hn_ref, xs_hbm, zbuf, sem, zsem, *, n_rows):
    tm = hn_ref.shape[0]

    @pl.when(pl.program_id(0) == 0)
    def _():
        zbuf[...] = jnp.zeros_like(zbuf)
        tail = pltpu.make_async_copy(zbuf, xs_hbm.at[pl.ds(n_rows, SORTED_TAIL)], zsem)
        tail.start()
        tail.wait()

        def hole(e):
            end = seg_end_ref[e]
            start = pl.multiple_of(end - (end & (SEG_ALIGN - 1)), SEG_ALIGN)
            return pltpu.make_async_copy(zbuf.at[pl.ds(0, SEG_ALIGN)], xs_hbm.at[pl.ds(start, SEG_ALIGN)], zsem)

        for e in range(N_EXPERTS):
            @pl.when((seg_end_ref[e] & (SEG_ALIGN - 1)) != 0)
            def _():
                hole(e).start()
        for e in range(N_EXPERTS):
            @pl.when((seg_end_ref[e] & (SEG_ALIGN - 1)) != 0)
            def _():
                hole(e).wait()

    for j in range(tm):
        for k in range(TOP_K):
            pltpu.make_async_copy(hn_ref.at[pl.ds(j, 1)], xs_hbm.at[pl.ds(pos_ref[0, 0, TOP_K * j + k], 1)],
                                  sem).start(priority=k % DMA_PRIORITIES)

    tile = pltpu.make_async_copy(hn_ref, xs_hbm.at[pl.ds(0, tm)], sem)
    for k in range(TOP_K):
        tile.wait()


def _dispatch(hn_packed, pos, seg_end, tm):
    t, dp = hn_packed.shape
    n_rows = TOP_K * t
    pos3 = pos.reshape(t // tm, 1, TOP_K * tm)
    grid_spec = pltpu.PrefetchScalarGridSpec(
        num_scalar_prefetch=1,
        grid=(t // tm,),
        in_specs=[pl.BlockSpec((1, 1, TOP_K * tm), lambda i, se: (i, 0, 0), memory_space=pltpu.SMEM),
                  pl.BlockSpec((tm, dp), lambda i, se: (i, 0))],
        out_specs=pl.BlockSpec(memory_space=pl.ANY),
        scratch_shapes=[pltpu.VMEM((SORTED_TAIL, dp), jnp.uint32),
                        pltpu.SemaphoreType.DMA,
                        pltpu.SemaphoreType.DMA],
    )
    return pl.pallas_call(
        functools.partial(_dispatch_kernel, n_rows=n_rows),
        grid_spec=grid_spec,
        out_shape=jax.ShapeDtypeStruct((n_rows + SORTED_TAIL, dp), jnp.uint32),
        compiler_params=_cparams(("arbitrary",)),
        name="dispatch",
    )(seg_end, pos3, hn_packed)


def _expert_kernel(bexp_ref, nreal_ref, row0_ref, xs_hbm, wg_ref, wu_ref, wd_ref, ys_hbm,
                   xbuf, ybuf, wg_bf, wu_bf, wd_bf, isem, osem, *, n_rows):
    i = pl.program_id(0)
    nb = pl.num_programs(0)
    slot = i % 2
    nslot = 1 - slot
    n_cur = nreal_ref[i]
    prev = jnp.maximum(i - 1, 0)
    nxt = jnp.minimum(i + 1, nb - 1)

    def in_copy(blk, s):
        row0 = pl.multiple_of(row0_ref[blk], SEG_ALIGN)
        return pltpu.make_async_copy(xs_hbm.at[pl.ds(row0, MOE_BLOCK)], xbuf.at[s], isem.at[s])

    def out_copy(blk, s):
        row0 = pl.multiple_of(row0_ref[blk], SEG_ALIGN)
        return pltpu.make_async_copy(ybuf.at[s], ys_hbm.at[pl.ds(row0, MOE_BLOCK)], osem.at[s])

    @pl.when(i == 0)
    def _():
        ybuf[...] = jnp.zeros_like(ybuf)
        tails = [pltpu.make_async_copy(ybuf.at[s, pl.ds(0, size)], ys_hbm.at[pl.ds(n_rows + s * MOE_BLOCK, size)],
                                       osem.at[s])
                 for s, size in enumerate(_tail_pieces())]
        for tail in tails:
            tail.start()
        for tail in tails:
            tail.wait()

        @pl.when(n_cur > 0)
        def _():
            in_copy(i, slot).start()

    @pl.when((i + 1 < nb) & (nreal_ref[nxt] > 0))
    def _():
        in_copy(nxt, nslot).start()

    @pl.when((n_cur > 0) & ((i == 0) | (bexp_ref[i] != bexp_ref[prev])))
    def _():
        wg_bf[...] = wg_ref[0].astype(BF16)
        wu_bf[...] = wu_ref[0].astype(BF16)
        wd_bf[...] = wd_ref[0].astype(BF16)

    @pl.when(n_cur > 0)
    def _():
        in_copy(i, slot).wait()
        xb = _unpack_bf16_pairs(xbuf[slot]).astype(BF16)
        gate = jnp.dot(xb, wg_bf[...], preferred_element_type=F32)
        up = jnp.dot(xb, wu_bf[...], preferred_element_type=F32)
        hid = (gate * jax.nn.sigmoid(gate) * up).astype(BF16)
        ybuf[slot] = _pack_bf16_pairs(jnp.dot(hid, wd_bf[...], preferred_element_type=F32))

    @pl.when((i >= 1) & (nreal_ref[prev] > 0))
    def _():
        out_copy(prev, nslot).wait()

    @pl.when(n_cur > 0)
    def _():
        out_copy(i, slot).start()

        @pl.when(i == nb - 1)
        def _():
            out_copy(i, slot).wait()


def _experts(xs, block_expert, block_nreal, block_row0, wg, wu, wd):
    n_blocks = block_expert.shape[0]
    assert n_blocks >= 2 and len(_tail_pieces()) <= 2
    n_rows_pad, dp = xs.shape
    _, d, ff = wg.shape
    grid_spec = pltpu.PrefetchScalarGridSpec(
        num_scalar_prefetch=3,
        grid=(n_blocks,),
        in_specs=[pl.BlockSpec(memory_space=pl.ANY),
                  pl.BlockSpec((1, d, ff), lambda i, be, nr, r0: (be[i], 0, 0)),
                  pl.BlockSpec((1, d, ff), lambda i, be, nr, r0: (be[i], 0, 0)),
                  pl.BlockSpec((1, ff, d), lambda i, be, nr, r0: (be[i], 0, 0))],
        out_specs=pl.BlockSpec(memory_space=pl.ANY),
        scratch_shapes=[pltpu.VMEM((2, MOE_BLOCK, dp), jnp.uint32),
                        pltpu.VMEM((2, MOE_BLOCK, dp), jnp.uint32),
                        pltpu.VMEM((d, ff), BF16),
                        pltpu.VMEM((d, ff), BF16),
                        pltpu.VMEM((ff, d), BF16),
                        pltpu.SemaphoreType.DMA((2,)),
                        pltpu.SemaphoreType.DMA((2,))],
    )
    return pl.pallas_call(
        functools.partial(_expert_kernel, n_rows=n_rows_pad - SORTED_TAIL),
        grid_spec=grid_spec,
        out_shape=jax.ShapeDtypeStruct((n_rows_pad, dp), jnp.uint32),
        compiler_params=_cparams(("arbitrary",)),
        name="experts",
    )(block_expert, block_nreal, block_row0, xs, wg, wu, wd)


def _final_kernel(pos_ref, h_ref, route_ref, gain_ref, ys_hbm, out_ref, ybuf, sem):
    i = pl.program_id(0)
    nt = pl.num_programs(0) - 1
    tm = h_ref.shape[0]

    for s in range(2):
        @pl.when((i < nt) & (i % 2 == s))
        def _():
            for j in range(tm):
                for k in range(TOP_K):
                    pltpu.make_async_copy(ys_hbm.at[pl.ds(pos_ref[0, 0, TOP_K * j + k], 1)],
                                          ybuf.at[s, k, pl.ds(j, 1)], sem.at[s]).start(priority=k % DMA_PRIORITIES)

    @pl.when(i >= 1)
    def _():
        slot = (i - 1) % 2
        for k in range(TOP_K):
            pltpu.make_async_copy(ys_hbm.at[pl.ds(0, tm)], ybuf.at[slot, k], sem.at[slot]).wait()
        r = route_ref[...]
        y = h_ref[...]
        for k, gate_lane in enumerate((ROUTE_G1, ROUTE_G2)):
            y = y + r[:, gate_lane:gate_lane + 1] * _unpack_bf16_pairs(ybuf[slot, k])
        out_ref[...] = y * lax.rsqrt(jnp.mean(y * y, axis=-1, keepdims=True) + NORM_EPS) * gain_ref[...]


def _final(h, ys, pos, route, gain, tm):
    t, d = h.shape
    nt = t // tm
    pos3 = pos.reshape(nt, 1, TOP_K * tm)
    done = lambda i: (jnp.maximum(i - 1, 0), 0)
    return pl.pallas_call(
        _final_kernel,
        grid=(nt + 1,),
        in_specs=[pl.BlockSpec((1, 1, TOP_K * tm), lambda i: (jnp.minimum(i, nt - 1), 0, 0),
                               memory_space=pltpu.SMEM),
                  pl.BlockSpec((tm, d), done),
                  pl.BlockSpec((tm, LANES), done),
                  pl.BlockSpec((1, d), lambda i: (0, 0)),
                  pl.BlockSpec(memory_space=pl.ANY)],
        out_specs=pl.BlockSpec((tm, d), done),
        out_shape=jax.ShapeDtypeStruct((t, d), F32),
        scratch_shapes=[pltpu.VMEM((2, TOP_K, tm, d // 2), jnp.uint32),
                        pltpu.SemaphoreType.DMA((2,))],
        compiler_params=_cparams(("arbitrary",)),
        name="final",
    )(pos3, h, route, gain.reshape(1, d), ys)


def _layer(h3, mix_gain, w_in, attn_gain, decay_f, decay_b, ret_gain, w_out, ffn_gain,
           w_rg, b_rg, w_re, b_re, w_eg, w_eu, w_ed, final_gain):
    b, s, d = h3.shape
    t = b * s
    tm = 512
    attn_qkv, (qr, kr, vr, gr) = _inproj(h3, mix_gain, w_in.astype(BF16), tm)

    o_list, lse_list = [], []
    for window, dilation in DILATED_BRANCHES:
        reach = (window // 2) // dilation
        o, lse = _attn_branch(*attn_qkv[dilation], dilation, reach, tq=128, tqb=512)
        o_list.append(o)
        lse_list.append(lse)

    lg_f = jnp.log1p(-jnp.exp2(decay_f.astype(F32)))
    lg_b = jnp.log1p(-jnp.exp2(decay_b.astype(F32)))
    orr = _retention(qr, kr, vr, gr, lg_f, lg_b, ret_gain).reshape(t, HEAD_GROUP_WIDTH)

    n_route = MOE_GROUPS + N_EXPERTS
    w_router = jnp.zeros((d, LANES), F32).at[:, :n_route].set(jnp.concatenate([w_rg, w_re], axis=1).astype(F32))
    b_router = jnp.zeros((1, LANES), F32).at[0, :n_route].set(jnp.concatenate([b_rg, b_re]).astype(F32))
    h, hn_packed, logits = _outproj(h3.reshape(t, d), o_list, lse_list, orr, attn_gain, w_out.astype(BF16),
                                    ffn_gain, w_router, b_router, tm)

    route, counts8 = _route(logits, tm)

    n_blocks = -(-TOP_K * t // MOE_BLOCK) + N_EXPERTS
    counts = counts8[0, :N_EXPERTS].astype(jnp.int32)
    aligned = ((counts + SEG_ALIGN - 1) // SEG_ALIGN) * SEG_ALIGN
    seg_start = jnp.cumsum(aligned) - aligned
    expert_iota = jnp.arange(N_EXPERTS, dtype=jnp.int32)
    e12 = route[:, ROUTE_E1:ROUTE_E2 + 1].astype(jnp.int32)
    r12 = route[:, ROUTE_R1:ROUTE_R2 + 1].astype(jnp.int32)
    pos = r12 + jnp.sum(jnp.where(e12[..., None] == expert_iota, seg_start, 0), axis=-1)
    nblk = (counts + MOE_BLOCK - 1) // MOE_BLOCK
    blk_end = jnp.cumsum(nblk)
    blk_start = blk_end - nblk
    blk = jnp.arange(n_blocks, dtype=jnp.int32)[:, None]
    owner = (blk >= blk_start) & (blk < blk_end)
    local = (blk - blk_start) * MOE_BLOCK
    block_row0 = jnp.sum(jnp.where(owner, seg_start + local, 0), axis=-1).astype(jnp.int32)
    block_nreal = jnp.sum(jnp.where(owner, jnp.clip(counts - local, 0, MOE_BLOCK), 0), axis=-1).astype(jnp.int32)
    block_expert = jnp.minimum(jnp.sum((blk >= blk_end).astype(jnp.int32), axis=-1), N_EXPERTS - 1)

    xs = _dispatch(hn_packed, pos, seg_start + counts, tm)
    ys = _experts(xs, block_expert, block_nreal, block_row0, w_eg.astype(F32), w_eu.astype(F32),
                  w_ed.astype(F32))
    out = _final(h, ys, pos, route, final_gain, tm)
    return out.reshape(b, s, d)


def kernel(x, mix_norm_gain, w_in, attn_out_gain, ret_decay_fwd, ret_decay_bwd, ret_out_gain, w_out,
           ffn_norm_gain, w_route_group, b_route_group, w_route_expert, b_route_expert,
           w_expert_gate, w_expert_up, w_expert_down, final_norm_gain):
    depth = mix_norm_gain.shape[0]
    assert depth == 1, "the final rmsnorm is fused into the single layer's combine kernel"
    l = 0
    return _layer(x, mix_norm_gain[l], w_in[l], attn_out_gain[l], ret_decay_fwd[l], ret_decay_bwd[l],
                  ret_out_gain[l], w_out[l], ffn_norm_gain[l], w_route_group[l], b_route_group[l],
                  w_route_expert[l], b_route_expert[l], w_expert_gate[l], w_expert_up[l], w_expert_down[l],
                  final_norm_gain)
```

```python
import functools

import numpy as np
import jax
import jax.numpy as jnp
from jax import lax
from jax.experimental import pallas as pl
from jax.experimental.pallas import tpu as pltpu
from jax.experimental.pallas import tpu_sc as plsc

F32 = jnp.float32
BF16 = jnp.bfloat16

ATTN_HEADS = 8
HEAD_DIM = 64
RET_HEADS = 8
HEAD_GROUP_WIDTH = 512
N_PROJ_GROUPS = 7
DILATED_BRANCHES = ((128, 1), (512, 4), (2048, 16))
ROPE_THETA = 500000.0
ROPE_DIM = HEAD_DIM // 4
RET_THETA = 10000.0
RET_CHUNK = 128
MOE_GROUPS = 4
EXPERTS_PER_GROUP = 8
N_EXPERTS = MOE_GROUPS * EXPERTS_PER_GROUP
MOE_BLOCK = 512
NORM_EPS = 1e-6
NEG_INF = -1e30

LANES = 128
VMEM_LIMIT = 56 * 1024 * 1024


def _cparams(sem):
    return pltpu.CompilerParams(dimension_semantics=sem, vmem_limit_bytes=VMEM_LIMIT)


def _rotary_tables(seq, half, freqs):
    pos = np.arange(seq, dtype=np.float64)[:, None]
    ang = pos * freqs[None, :].astype(np.float64)
    cos, sin = np.cos(ang), np.sin(ang)
    c = np.ones((seq, HEAD_DIM)); sp = np.zeros((seq, HEAD_DIM)); sm = np.zeros((seq, HEAD_DIM))
    c[:, :half] = cos; c[:, half:2 * half] = cos
    sp[:, half:2 * half] = sin
    sm[:, :half] = -sin
    rep = LANES // HEAD_DIM
    return tuple(jnp.asarray(np.tile(t, (1, rep)), F32) for t in (c, sp, sm))


def _rotate(t, c, sp, sm, half):
    outs = []
    for g in range(t.shape[1] // LANES):
        tg = t[:, g * LANES:(g + 1) * LANES]
        outs.append(tg * c + pltpu.roll(tg, half, 1) * sp + pltpu.roll(tg, LANES - half, 1) * sm)
    return jnp.concatenate(outs, axis=1)


CLASS_DILATIONS = tuple(d for _, d in DILATED_BRANCHES if d > 1)
ATTN_Q_SCALE = float(np.log2(np.e)) * HEAD_DIM ** -0.5


def _inproj_kernel(x_ref, gain_ref, w_ref, ca_ref, spa_ref, sma_ref, cr_ref, spr_ref, smr_ref, *rest):
    n_cls = len(CLASS_DILATIONS)
    nat_refs = rest[0:3]
    cls_refs = [rest[3 + 3 * c:6 + 3 * c] for c in range(n_cls)]
    qr_ref, kr_ref, vr_ref, gr_ref = rest[3 + 3 * n_cls:7 + 3 * n_cls]
    stage_ref = rest[7 + 3 * n_cls]
    x = x_ref[0]
    tm = x.shape[0]
    ms = jnp.mean(x * x, axis=-1, keepdims=True)
    xn = (x * lax.rsqrt(ms + NORM_EPS) * gain_ref[...]).astype(BF16)
    gw = HEAD_GROUP_WIDTH

    def proj(c):
        return jnp.dot(xn, w_ref[:, c * gw:(c + 1) * gw], preferred_element_type=F32)

    a_tabs = (ca_ref[...], spa_ref[...], sma_ref[...])
    r_tabs = (cr_ref[...], spr_ref[...], smr_ref[...])
    attn_vals = ((_rotate(proj(0), *a_tabs, ROPE_DIM // 2) * ATTN_Q_SCALE),
                 _rotate(proj(1), *a_tabs, ROPE_DIM // 2),
                 proj(2))
    for j, val in enumerate(attn_vals):
        nat_refs[j][0] = val.astype(BF16)
        for g in range(gw // LANES):
            stage_ref[g] = val[:, g * LANES:(g + 1) * LANES]
        for c, d in enumerate(CLASS_DILATIONS):
            for r in range(d):
                for g in range(gw // LANES):
                    col = r * gw + g * LANES
                    cls_refs[c][j][0, :, col:col + LANES] = (
                        stage_ref[g, pl.ds(r, tm // d, stride=d), :].astype(BF16))
    qr_ref[0] = _rotate(proj(3), *r_tabs, HEAD_DIM // 2).astype(BF16)
    kr_ref[0] = (_rotate(proj(4), *r_tabs, HEAD_DIM // 2) * (HEAD_DIM ** -0.5)).astype(BF16)
    vr_ref[0] = proj(5).astype(BF16)
    g = proj(6)
    gr_ref[0] = (g * jax.nn.sigmoid(g)).astype(BF16)


def _inproj(x, gain, w_in_bf16, tm):
    b, s, d = x.shape
    rope_freqs = ROPE_THETA ** (-np.arange(0, ROPE_DIM, 2, dtype=np.float32) / ROPE_DIM)
    ret_freqs = RET_THETA ** (-np.linspace(0.0, 1.0, HEAD_DIM // 2, dtype=np.float32))
    tabs = _rotary_tables(s, ROPE_DIM // 2, rope_freqs) + _rotary_tables(s, HEAD_DIM // 2, ret_freqs)
    gw = HEAD_GROUP_WIDTH
    tab_spec = pl.BlockSpec((tm, LANES), lambda si, bi: (si, 0))

    def view(dil):
        return (pl.BlockSpec((1, tm // dil, dil * gw), lambda si, bi: (bi, si, 0)),
                jax.ShapeDtypeStruct((b, s // dil, dil * gw), BF16))

    views = [view(1)] * 3 + [view(dil) for dil in CLASS_DILATIONS for _ in range(3)] + [view(1)] * 4
    outs = pl.pallas_call(
        _inproj_kernel,
        grid=(s // tm, b),
        in_specs=[pl.BlockSpec((1, tm, d), lambda si, bi: (bi, si, 0)),
                  pl.BlockSpec((1, d), lambda si, bi: (0, 0)),
                  pl.BlockSpec(w_in_bf16.shape, lambda si, bi: (0, 0))] + [tab_spec] * 6,
        out_specs=[v[0] for v in views],
        out_shape=[v[1] for v in views],
        scratch_shapes=[pltpu.VMEM((gw // LANES, tm, LANES), F32)],
        compiler_params=_cparams(("arbitrary", "arbitrary")),
        name="inproj",
    )(x, gain.reshape(1, d), w_in_bf16, *tabs)
    n_attn = 3 * (1 + len(CLASS_DILATIONS))
    attn_qkv = {dil: outs[3 * c:3 * c + 3] for c, dil in enumerate((1,) + CLASS_DILATIONS)}
    return attn_qkv, outs[n_attn:]


ATTN_WINDOW_CASES = 3


def _attn_kernel(q_ref, k_ref, v_ref, o_ref, lse_ref, bias_ref, *, length, tq, reach):
    tqb = q_ref.shape[1]
    win = tq + 2 * reach
    heads_per_pair = LANES // HEAD_DIM
    qi = pl.program_id(2)
    lane = lax.broadcasted_iota(jnp.int32, (1, LANES), 1)
    lane_t = lax.broadcasted_iota(jnp.int32, (tq, LANES), 1)
    ones = jnp.ones((win, LANES), BF16)

    @pl.when((pl.program_id(0) == 0) & (pl.program_id(1) == 0) & (qi == 0))
    def _():
        diff = (lax.broadcasted_iota(jnp.int32, (heads_per_pair * tq, win), 1)
                - (lax.broadcasted_iota(jnp.int32, (heads_per_pair * tq, win), 0) & (tq - 1)))
        for case in range(ATTN_WINDOW_CASES):
            off = case * reach
            bias_ref[case] = jnp.where((diff >= off - reach) & (diff <= off + reach), 0.0, NEG_INF)

    def sub(t, cls):
        q0 = qi * tqb + t * tq
        ws = jnp.clip(q0 - reach, 0, length - win)
        ws = pl.multiple_of(ws, reach)
        bias = bias_ref[(q0 - ws) // reach]
        rows = pl.ds(t * tq, tq)
        m_tile = jnp.zeros((tq, LANES), F32)
        l_tile = jnp.ones((tq, LANES), F32)
        for g in range(HEAD_GROUP_WIDTH // LANES):
            cols = slice(cls * HEAD_GROUP_WIDTH + g * LANES, cls * HEAD_GROUP_WIDTH + (g + 1) * LANES)
            qg = q_ref[0, rows, cols]
            kw = k_ref[0, pl.ds(ws, win), cols]
            v_ones = jnp.concatenate([v_ref[0, pl.ds(ws, win), cols], ones], axis=1)
            hms = [(lane >= hh * HEAD_DIM) & (lane < (hh + 1) * HEAD_DIM) for hh in range(heads_per_pair)]
            q2 = jnp.concatenate([qg * hm.astype(BF16) for hm in hms], axis=0)
            sc = lax.dot_general(q2, kw, (((1,), (1,)), ((), ())), preferred_element_type=F32) + bias
            m = jnp.max(sc, axis=-1, keepdims=True)
            p = jnp.exp2(sc - m).astype(BF16)
            pv = jnp.dot(p, v_ones, preferred_element_type=F32)
            l = pv[:, LANES:]
            o = pv[:, :LANES] / l
            o_pair = jnp.zeros((tq, LANES), F32)
            for hh in range(heads_per_pair):
                part = slice(hh * tq, (hh + 1) * tq)
                head_lane = lane_t == g * heads_per_pair + hh
                o_pair = jnp.where(hms[hh], o[part], o_pair)
                m_tile = jnp.where(head_lane, m[part], m_tile)
                l_tile = jnp.where(head_lane, l[part], l_tile)
            o_ref[0, rows, cols] = o_pair.astype(BF16)
        lse_ref[0, rows, cls * LANES:(cls + 1) * LANES] = m_tile + jnp.log2(l_tile)

    for cls in range(q_ref.shape[2] // HEAD_GROUP_WIDTH):
        for t in range(tqb // tq):
            sub(t, cls)


def _attn_branch(qc, kc, vc, dilation, reach, tq, tqb, sub_tiles=4):
    b, length, dw = qc.shape
    w = dw // dilation
    tqb = min(tqb, length)
    assert tq % reach == 0 and tq > reach and length % tq == 0 and length >= tq + 2 * reach
    ncls = min(dilation, max(1, sub_tiles // (tqb // tq)))
    q_spec = pl.BlockSpec((1, tqb, ncls * w), lambda bi, r, qi: (bi, qi, r))
    kv_spec = pl.BlockSpec((1, length, ncls * w), lambda bi, r, qi: (bi, 0, r))
    o, lse = pl.pallas_call(
        functools.partial(_attn_kernel, length=length, tq=tq, reach=reach),
        grid=(b, dilation // ncls, length // tqb),
        in_specs=[q_spec, kv_spec, kv_spec],
        out_specs=[q_spec, pl.BlockSpec((1, tqb, ncls * LANES), lambda bi, r, qi: (bi, qi, r))],
        out_shape=[jax.ShapeDtypeStruct((b, length, dilation * w), BF16),
                   jax.ShapeDtypeStruct((b, length, dilation * LANES), F32)],
        scratch_shapes=[pltpu.VMEM((ATTN_WINDOW_CASES, (LANES // HEAD_DIM) * tq, tq + 2 * reach), F32)],
        compiler_params=_cparams(("arbitrary", "arbitrary", "arbitrary")),
        name=f"attn_d{dilation}",
    )(qc, kc, vc)
    return o, lse


RET_TAB_QF, RET_TAB_QB, RET_TAB_KF, RET_TAB_KB = range(4)


def _retention_kernel(lgf_ref, lgb_ref, q_ref, k_ref, v_ref, g_ref, gain_ref, o_ref,
                      tab_ref, dec_ref, sb_ref, st_ref, *, chunk, unroll):
    c = chunk
    n = q_ref.shape[1] // c
    width = q_ref.shape[2]
    n_pairs = width // LANES
    heads_per_pair = LANES // HEAD_DIM
    n_heads = n_pairs * heads_per_pair
    head0 = pl.program_id(1) * n_heads
    lane_w = lax.broadcasted_iota(jnp.int32, (1, width), 1)
    lgf = [lgf_ref[head0 + hd] for hd in range(n_heads)]
    lgb = [lgb_ref[head0 + hd] for hd in range(n_heads)]
    lgf_lane = jnp.zeros((1, width), F32)
    lgb_lane = jnp.zeros((1, width), F32)
    for hd in range(n_heads):
        in_head = (lane_w >= hd * HEAD_DIM) & (lane_w < (hd + 1) * HEAD_DIM)
        lgf_lane = jnp.where(in_head, lgf[hd], lgf_lane)
        lgb_lane = jnp.where(in_head, lgb[hd], lgb_lane)
    idx = lax.broadcasted_iota(jnp.int32, (c, width), 0).astype(F32)
    tab_ref[RET_TAB_QF] = jnp.exp((idx + 1.0) * lgf_lane)
    tab_ref[RET_TAB_QB] = jnp.exp((c - idx) * lgb_lane)
    tab_ref[RET_TAB_KF] = jnp.exp((c - 1.0 - idx) * lgf_lane)
    tab_ref[RET_TAB_KB] = jnp.exp(idx * lgb_lane)
    sdf = jnp.exp(c * lgf_lane)
    sdb = jnp.exp(c * lgb_lane)
    dmat = (lax.broadcasted_iota(jnp.int32, (c, c), 0)
            - lax.broadcasted_iota(jnp.int32, (c, c), 1)).astype(F32)
    for hd in range(n_heads):
        dec_ref[hd // heads_per_pair, :, (hd % heads_per_pair) * c:(hd % heads_per_pair + 1) * c] = (
            jnp.where(dmat >= 0, jnp.exp(dmat * lgf[hd]), jnp.exp(-dmat * lgb[hd])))
    lane = lax.broadcasted_iota(jnp.int32, (1, LANES), 1)
    lane_hi = lane >= HEAD_DIM
    head_masks = [((lane >= hh * HEAD_DIM) & (lane < (hh + 1) * HEAD_DIM)).astype(BF16)
                  for hh in range(heads_per_pair)]
    row_hi = lax.broadcasted_iota(jnp.int32, (LANES, LANES), 0) >= HEAD_DIM
    col_hi = lax.broadcasted_iota(jnp.int32, (LANES, LANES), 1) >= HEAD_DIM
    blockdiag = row_hi == col_hi

    def kv_state(kd, vv):
        kt = jnp.transpose(kd).astype(BF16)
        return jnp.where(blockdiag, jnp.dot(kt, vv, preferred_element_type=F32), 0.0)

    st_ref[...] = jnp.zeros_like(st_ref)
    sb_ref[n - 1] = jnp.zeros(sb_ref.shape[1:], sb_ref.dtype)

    def back(i, carry):
        nn = n - 1 - i
        rows = pl.ds(pl.multiple_of(nn * c, c), c)
        for p in range(n_pairs):
            cols = slice(p * LANES, (p + 1) * LANES)
            kd = k_ref[0, rows, cols].astype(F32) * tab_ref[RET_TAB_KB, :, cols]
            new = st_ref[p] * sdb[:, cols] + kv_state(kd, v_ref[0, rows, cols])
            st_ref[p] = new
            sb_ref[nn - 1, p] = new.astype(BF16)
        return carry

    lax.fori_loop(0, n - 1, back, 0, unroll=unroll)

    st_ref[...] = jnp.zeros_like(st_ref)

    def fwd(nn, carry):
        rows = pl.ds(pl.multiple_of(nn * c, c), c)
        for p in range(n_pairs):
            cols = slice(p * LANES, (p + 1) * LANES)
            qq = q_ref[0, rows, cols]
            kk = k_ref[0, rows, cols]
            vv = v_ref[0, rows, cols]
            qf = qq.astype(F32)
            sf = st_ref[p]
            qcat = jnp.concatenate([(qf * tab_ref[RET_TAB_QF, :, cols]).astype(BF16),
                                    (qf * tab_ref[RET_TAB_QB, :, cols]).astype(BF16)], axis=1)
            scat = jnp.concatenate([sf.astype(BF16), sb_ref[nn, p]], axis=0)
            o = jnp.dot(qcat, scat, preferred_element_type=F32)
            k2 = jnp.concatenate([kk * hm for hm in head_masks], axis=0)
            v2 = jnp.concatenate([vv * hm for hm in head_masks], axis=0)
            sc = lax.dot_general(qq, k2, (((1,), (1,)), ((), ())), preferred_element_type=F32)
            o = o + jnp.dot((sc * dec_ref[p]).astype(BF16), v2, preferred_element_type=F32)
            s_lo = jnp.sum(jnp.where(lane_hi, 0.0, o), axis=-1, keepdims=True)
            s_hi = jnp.sum(jnp.where(lane_hi, o, 0.0), axis=-1, keepdims=True)
            mu = jnp.where(lane_hi, s_hi, s_lo) * (1.0 / HEAD_DIM)
            dev = o - mu
            d2 = dev * dev
            v_lo = jnp.sum(jnp.where(lane_hi, 0.0, d2), axis=-1, keepdims=True)
            v_hi = jnp.sum(jnp.where(lane_hi, d2, 0.0), axis=-1, keepdims=True)
            var = jnp.where(lane_hi, v_hi, v_lo) * (1.0 / HEAD_DIM)
            out = dev * lax.rsqrt(var + NORM_EPS) * gain_ref[:, cols] * g_ref[0, rows, cols].astype(F32)
            o_ref[0, rows, cols] = out.astype(BF16)
            kd = kk.astype(F32) * tab_ref[RET_TAB_KF, :, cols]
            st_ref[p] = sf * sdf[:, cols] + kv_state(kd, vv)
        return carry

    lax.fori_loop(0, n, fwd, 0, unroll=unroll)


def _retention(qr, kr, vr, gate, lg_f, lg_b, out_gain, width=512, unroll=4):
    b, s, w = qr.shape
    n_pairs = width // LANES
    n_heads = width // HEAD_DIM
    spec = pl.BlockSpec((1, s, width), lambda bi, p, *_: (bi, 0, p))
    grid_spec = pltpu.PrefetchScalarGridSpec(
        num_scalar_prefetch=2,
        grid=(b, w // width),
        in_specs=[spec, spec, spec, spec, pl.BlockSpec((1, width), lambda bi, p, *_: (0, p))],
        out_specs=spec,
        scratch_shapes=[pltpu.VMEM((4, RET_CHUNK, width), F32),
                        pltpu.VMEM((n_pairs, RET_CHUNK, (LANES // HEAD_DIM) * RET_CHUNK), F32),
                        pltpu.VMEM((s // RET_CHUNK, n_pairs, LANES, LANES), BF16),
                        pltpu.VMEM((n_pairs, LANES, LANES), F32)],
    )
    return pl.pallas_call(
        functools.partial(_retention_kernel, chunk=RET_CHUNK, unroll=unroll),
        grid_spec=grid_spec,
        out_shape=jax.ShapeDtypeStruct((b, s, w), BF16),
        compiler_params=_cparams(("arbitrary", "arbitrary")),
        name="retention",
    )(lg_f, lg_b, qr, kr, vr, gate, out_gain.reshape(1, w))


def _split_bf16(t):
    hi = t.astype(BF16)
    lo = (t - hi.astype(F32)).astype(BF16)
    return hi, lo


def _pack_bf16_pairs(t):
    n = t.shape[1] // 2
    hi = pltpu.bitcast(t[:, :n].astype(BF16).astype(F32), jnp.uint32)
    lo = pltpu.bitcast(t[:, n:].astype(BF16).astype(F32), jnp.uint32)
    return hi | (lo >> 16)


def _unpack_bf16_pairs(u):
    hi = pltpu.bitcast(u & jnp.uint32(0xFFFF0000), F32)
    lo = pltpu.bitcast(u << 16, F32)
    return jnp.concatenate([hi, lo], axis=1)


def _outproj_kernel(x_ref, o1_ref, o2_ref, o3_ref, l1_ref, l2_ref, l3_ref, orr_ref, ga_ref, expand_ref,
                    wout_ref, gf_ref, wr_ref, br_ref, h_ref, hn_ref, logit_ref,
                    *nat_refs):
    tm = x_ref.shape[0]
    gw = HEAD_GROUP_WIDTH
    os, ls = [], []
    for (_, dil), o_ref, l_ref in zip(DILATED_BRANCHES, (o1_ref, o2_ref, o3_ref), (l1_ref, l2_ref, l3_ref)):
        if dil == 1:
            os.append(o_ref[...].astype(F32))
            ls.append(l_ref[...])
            continue
        c = CLASS_DILATIONS.index(dil)
        onat_ref, lnat_ref = nat_refs[2 * c], nat_refs[2 * c + 1]
        for r in range(dil):
            rows = pl.ds(r, tm // dil, stride=dil)
            for g in range(gw // LANES):
                col = r * gw + g * LANES
                onat_ref[g, rows, :] = o_ref[:, col:col + LANES].astype(F32)
            lnat_ref[rows, :] = l_ref[:, r * LANES:(r + 1) * LANES]
        os.append(jnp.concatenate([onat_ref[g] for g in range(gw // LANES)], axis=1))
        ls.append(lnat_ref[...])
    mx = jnp.maximum(jnp.maximum(ls[0], ls[1]), ls[2])
    es = [jnp.exp2(l - mx) for l in ls]
    inv = 1.0 / (es[0] + es[1] + es[2])
    expand = expand_ref[...]
    oa = jnp.zeros((tm, gw), F32)
    for e, o in zip(es, os):
        wexp = jnp.dot(jnp.concatenate(_split_bf16(e * inv), axis=1), expand, preferred_element_type=F32)
        oa = oa + wexp * o
    oa = oa * lax.rsqrt(jnp.mean(oa * oa, axis=-1, keepdims=True) + NORM_EPS) * ga_ref[...]
    mixed = jnp.concatenate([oa.astype(BF16), orr_ref[...]], axis=1)
    h = x_ref[...] + jnp.dot(mixed, wout_ref[...], preferred_element_type=F32)
    h_ref[...] = h
    hn = h * lax.rsqrt(jnp.mean(h * h, axis=-1, keepdims=True) + NORM_EPS) * gf_ref[...]
    hn_ref[...] = _pack_bf16_pairs(hn)
    prod = jnp.dot(jnp.concatenate(_split_bf16(hn), axis=0), wr_ref[...], preferred_element_type=F32)
    logit_ref[...] = prod[:tm, :LANES] + prod[:tm, LANES:] + prod[tm:, :LANES] + br_ref[...]


def _outproj(x2, o_list, lse_list, orr, attn_gain, w_out_bf16, ffn_gain, w_router, b_router, tm):
    t, d = x2.shape
    w = HEAD_GROUP_WIDTH
    expand = np.zeros((LANES, w), np.float32)
    for hd in range(ATTN_HEADS):
        expand[hd, hd * HEAD_DIM:(hd + 1) * HEAD_DIM] = 1.0
    expand = jnp.asarray(np.concatenate([expand, expand], axis=0), BF16)
    wr_hi = w_router.astype(BF16)
    wr = jnp.concatenate([wr_hi, (w_router - wr_hi.astype(F32)).astype(BF16)], axis=1)
    row = lambda width, dil=1: pl.BlockSpec((tm // dil, dil * width), lambda i: (i, 0))
    full = lambda a: pl.BlockSpec(a.shape, lambda i: (0,) * a.ndim)
    ga = attn_gain.reshape(1, w)
    gf = ffn_gain.reshape(1, d)
    dils = [dil for _, dil in DILATED_BRANCHES]
    o_flat = [o.reshape(t // dil, dil * w) for o, dil in zip(o_list, dils)]
    l_flat = [l.reshape(t // dil, dil * LANES) for l, dil in zip(lse_list, dils)]
    nat_scratch = []
    for _ in CLASS_DILATIONS:
        nat_scratch += [pltpu.VMEM((w // LANES, tm, LANES), F32), pltpu.VMEM((tm, LANES), F32)]
    return pl.pallas_call(
        _outproj_kernel,
        grid=(t // tm,),
        in_specs=[row(d)] + [row(w, dil) for dil in dils] + [row(LANES, dil) for dil in dils] + [row(w)]
                 + [full(ga), full(expand), full(w_out_bf16), full(gf), full(wr), full(b_router)],
        out_specs=[row(d), row(d // 2), row(LANES)],
        out_shape=[jax.ShapeDtypeStruct((t, d), F32),
                   jax.ShapeDtypeStruct((t, d // 2), jnp.uint32),
                   jax.ShapeDtypeStruct((t, LANES), F32)],
        scratch_shapes=nat_scratch,
        compiler_params=_cparams(("arbitrary",)),
        name="outproj",
    )(x2, *o_flat, *l_flat, orr, ga, expand, w_out_bf16, gf, wr, b_router)


ROUTE_E1, ROUTE_E2, ROUTE_G1, ROUTE_G2, ROUTE_R1, ROUTE_R2 = range(6)
EXPERT_LANE0 = MOE_GROUPS


def _route_kernel(logit_ref, tri_ref, route_ref, count_ref, run_ref):
    @pl.when(pl.program_id(0) == 0)
    def _():
        run_ref[...] = jnp.zeros_like(run_ref)

    lg = logit_ref[...]
    tm = lg.shape[0]
    lane = lax.broadcasted_iota(jnp.int32, lg.shape, 1)
    big = jnp.int32(1 << 20)

    def top(vals):
        m = jnp.max(vals, axis=-1, keepdims=True)
        i = jnp.min(jnp.where(vals == m, lane, big), axis=-1, keepdims=True)
        return m, i

    gl = jnp.where(lane < MOE_GROUPS, lg, -jnp.inf)
    gmax, gidx = top(gl)
    group_gate = 1.0 / jnp.sum(jnp.exp(gl - gmax), axis=-1, keepdims=True)
    lo = EXPERT_LANE0 + gidx * EXPERTS_PER_GROUP
    el = jnp.where((lane >= lo) & (lane < lo + EXPERTS_PER_GROUP), lg, -jnp.inf)
    t1, i1 = top(el)
    t2, i2 = top(jnp.where(lane == i1, -jnp.inf, el))
    e21 = jnp.exp(t2 - t1)
    g1 = group_gate / (1.0 + e21)
    g2 = group_gate * e21 / (1.0 + e21)
    e1 = i1 - EXPERT_LANE0
    e2 = i2 - EXPERT_LANE0
    oh1 = lane == e1
    oh2 = lane == e2
    cnt = oh1.astype(F32) + oh2.astype(F32)
    prefix = jnp.dot(tri_ref[...], cnt.astype(BF16), preferred_element_type=F32) + run_ref[0:1, :]
    r1 = jnp.sum(jnp.where(oh1, prefix, 0.0), axis=-1, keepdims=True)
    r2 = jnp.sum(jnp.where(oh2, prefix, 0.0), axis=-1, keepdims=True)
    new_run = run_ref[0:1, :] + jnp.sum(cnt, axis=0, keepdims=True)
    run_ref[...] = jnp.broadcast_to(new_run, run_ref.shape)
    count_ref[...] = jnp.broadcast_to(new_run, count_ref.shape)
    out = jnp.zeros(lg.shape, F32)
    for ln, val in ((ROUTE_E1, e1.astype(F32)), (ROUTE_E2, e2.astype(F32)), (ROUTE_G1, g1),
                    (ROUTE_G2, g2), (ROUTE_R1, r1), (ROUTE_R2, r2)):
        out = jnp.where(lane == ln, val, out)
    route_ref[...] = out


def _route(logits, tm):
    t = logits.shape[0]
    tri = jnp.asarray(np.tril(np.ones((tm, tm), np.float32), -1), BF16)
    return pl.pallas_call(
        _route_kernel,
        grid=(t // tm,),
        in_specs=[pl.BlockSpec((tm, LANES), lambda i: (i, 0)), pl.BlockSpec((tm, tm), lambda i: (0, 0))],
        out_specs=[pl.BlockSpec((tm, LANES), lambda i: (i, 0)), pl.BlockSpec((8, LANES), lambda i: (0, 0))],
        out_shape=[jax.ShapeDtypeStruct((t, LANES), F32), jax.ShapeDtypeStruct((8, LANES), F32)],
        scratch_shapes=[pltpu.VMEM((8, LANES), F32)],
        compiler_params=_cparams(("arbitrary",)),
        name="route",
    )(logits, tri)


TOP_K = 2


SEG_ALIGN = 8
SORTED_TAIL = N_EXPERTS * SEG_ALIGN + MOE_BLOCK


def _tail_pieces():
    full, rest = divmod(SORTED_TAIL, MOE_BLOCK)
    return [MOE_BLOCK] * full + ([rest] if rest else [])


SC_CHUNK = 64
ZERO_ROWS = N_EXPERTS * SEG_ALIGN + SORTED_TAIL
SORTED_ROWS_EXTRA = SORTED_TAIL + SEG_ALIGN


def _sc_workers():
    info = plsc.get_sparse_core_info()
    return info.num_cores, info.num_cores * info.num_subcores


def _dispatch(hn_packed, pos, zero_pos):
    t, dp = hn_packed.shape
    ncores, nw = _sc_workers()
    per_w = t // nw
    n_ch = per_w // SC_CHUNK
    z_ch = ZERO_ROWS // SC_CHUNK
    assert per_w % SC_CHUNK == 0 and ZERO_ROWS % SC_CHUNK == 0 and z_ch <= nw and TOP_K == 2
    idx = [pos[:, k].reshape(nw, n_ch, SC_CHUNK) for k in range(TOP_K)]
    zeros = jnp.zeros((SC_CHUNK, dp), hn_packed.dtype)
    mesh = plsc.VectorSubcoreMesh(core_axis_name="c", subcore_axis_name="s")

    @functools.partial(
        pl.kernel, mesh=mesh,
        out_type=jax.ShapeDtypeStruct((TOP_K * t + SORTED_ROWS_EXTRA, dp), hn_packed.dtype),
        scratch_types=[pltpu.VMEM((n_ch, SC_CHUNK), jnp.int32), pltpu.VMEM((n_ch, SC_CHUNK), jnp.int32),
                       pltpu.VMEM((SC_CHUNK,), jnp.int32),
                       pltpu.VMEM((2, SC_CHUNK, dp), hn_packed.dtype),
                       pltpu.SemaphoreType.DMA((2,)), pltpu.SemaphoreType.DMA((2,))],
    )
    def scatter(hn_hbm, p0_hbm, p1_hbm, zpos_hbm, zeros_hbm, xs_hbm, i0_v, i1_v, iz_v, rows_v, lsem, ssem):
        wid = lax.axis_index("s") * ncores + lax.axis_index("c")
        base = wid * per_w

        @pl.when(wid < z_ch)
        def _():
            pltpu.sync_copy(zpos_hbm.at[wid], iz_v)
            pltpu.sync_copy(zeros_hbm, rows_v.at[0])
            pltpu.sync_copy(rows_v.at[0], xs_hbm.at[iz_v])

        pltpu.sync_copy(p0_hbm.at[wid], i0_v)
        pltpu.sync_copy(p1_hbm.at[wid], i1_v)

        def load(c):
            return pltpu.async_copy(hn_hbm.at[pl.ds(base + c * SC_CHUNK, SC_CHUNK)], rows_v.at[c % 2],
                                    lsem.at[c % 2])

        loads = {0: load(0)}
        scat = {}
        for c in range(n_ch):
            slot = c % 2
            loads[c].wait()
            scat[c] = (pltpu.async_copy(rows_v.at[slot], xs_hbm.at[i0_v.at[c]], ssem.at[slot]),
                       pltpu.async_copy(rows_v.at[slot], xs_hbm.at[i1_v.at[c]], ssem.at[slot]))
            if c >= 1:
                for d in scat[c - 1]:
                    d.wait()
            if c + 1 < n_ch:
                loads[c + 1] = load(c + 1)
        for d in scat[n_ch - 1]:
            d.wait()

    return scatter(hn_packed, idx[0], idx[1], zero_pos.reshape(z_ch, SC_CHUNK), zeros)


def _expert_kernel(bexp_ref, nreal_ref, row0_ref, xs_hbm, wg_ref, wu_ref, wd_ref, ys_hbm,
                   xbuf, ybuf, wg_bf, wu_bf, wd_bf, isem, osem, *, n_rows):
    i = pl.program_id(0)
    nb = pl.num_programs(0)
    slot = i % 2
    nslot = 1 - slot
    n_cur = nreal_ref[i]
    prev = jnp.maximum(i - 1, 0)
    nxt = jnp.minimum(i + 1, nb - 1)

    def in_copy(blk, s):
        row0 = pl.multiple_of(row0_ref[blk], SEG_ALIGN)
        return pltpu.make_async_copy(xs_hbm.at[pl.ds(row0, MOE_BLOCK)], xbuf.at[s], isem.at[s])

    def out_copy(blk, s):
        row0 = pl.multiple_of(row0_ref[blk], SEG_ALIGN)
        return pltpu.make_async_copy(ybuf.at[s], ys_hbm.at[pl.ds(row0, MOE_BLOCK)], osem.at[s])

    @pl.when(i == 0)
    def _():
        ybuf[...] = jnp.zeros_like(ybuf)
        tails = [pltpu.make_async_copy(ybuf.at[s, pl.ds(0, size)], ys_hbm.at[pl.ds(n_rows + s * MOE_BLOCK, size)],
                                       osem.at[s])
                 for s, size in enumerate(_tail_pieces())]
        for tail in tails:
            tail.start()
        for tail in tails:
            tail.wait()

        @pl.when(n_cur > 0)
        def _():
            in_copy(i, slot).start()

    @pl.when((i + 1 < nb) & (nreal_ref[nxt] > 0))
    def _():
        in_copy(nxt, nslot).start()

    @pl.when((n_cur > 0) & ((i == 0) | (bexp_ref[i] != bexp_ref[prev])))
    def _():
        wg_bf[...] = wg_ref[0].astype(BF16)
        wu_bf[...] = wu_ref[0].astype(BF16)
        wd_bf[...] = wd_ref[0].astype(BF16)

    @pl.when(n_cur > 0)
    def _():
        in_copy(i, slot).wait()
        xb = _unpack_bf16_pairs(xbuf[slot]).astype(BF16)
        gate = jnp.dot(xb, wg_bf[...], preferred_element_type=F32)
        up = jnp.dot(xb, wu_bf[...], preferred_element_type=F32)
        hid = (gate * jax.nn.sigmoid(gate) * up).astype(BF16)
        ybuf[slot] = _pack_bf16_pairs(jnp.dot(hid, wd_bf[...], preferred_element_type=F32))

    @pl.when((i >= 1) & (nreal_ref[prev] > 0))
    def _():
        out_copy(prev, nslot).wait()

    @pl.when(n_cur > 0)
    def _():
        out_copy(i, slot).start()

        @pl.when(i == nb - 1)
        def _():
            out_copy(i, slot).wait()


def _experts(xs, block_expert, block_nreal, block_row0, wg, wu, wd):
    n_blocks = block_expert.shape[0]
    assert n_blocks >= 2 and len(_tail_pieces()) <= 2
    n_rows_pad = xs.shape[0] - SORTED_ROWS_EXTRA + SORTED_TAIL
    dp = xs.shape[1]
    _, d, ff = wg.shape
    grid_spec = pltpu.PrefetchScalarGridSpec(
        num_scalar_prefetch=3,
        grid=(n_blocks,),
        in_specs=[pl.BlockSpec(memory_space=pl.ANY),
                  pl.BlockSpec((1, d, ff), lambda i, be, nr, r0: (be[i], 0, 0)),
                  pl.BlockSpec((1, d, ff), lambda i, be, nr, r0: (be[i], 0, 0)),
                  pl.BlockSpec((1, ff, d), lambda i, be, nr, r0: (be[i], 0, 0))],
        out_specs=pl.BlockSpec(memory_space=pl.ANY),
        scratch_shapes=[pltpu.VMEM((2, MOE_BLOCK, dp), jnp.uint32),
                        pltpu.VMEM((2, MOE_BLOCK, dp), jnp.uint32),
                        pltpu.VMEM((d, ff), BF16),
                        pltpu.VMEM((d, ff), BF16),
                        pltpu.VMEM((ff, d), BF16),
                        pltpu.SemaphoreType.DMA((2,)),
                        pltpu.SemaphoreType.DMA((2,))],
    )
    return pl.pallas_call(
        functools.partial(_expert_kernel, n_rows=n_rows_pad - SORTED_TAIL),
        grid_spec=grid_spec,
        out_shape=jax.ShapeDtypeStruct((n_rows_pad, dp), jnp.uint32),
        compiler_params=_cparams(("arbitrary",)),
        name="experts",
    )(block_expert, block_nreal, block_row0, xs, wg, wu, wd)


def _combine_gather(ys, pos):
    t = pos.shape[0]
    dp = ys.shape[1]
    ncores, nw = _sc_workers()
    per_w = t // nw
    n_ch = per_w // SC_CHUNK
    assert per_w % SC_CHUNK == 0
    idx = [pos[:, k].reshape(nw, n_ch, SC_CHUNK) for k in range(TOP_K)]
    mesh = plsc.VectorSubcoreMesh(core_axis_name="c", subcore_axis_name="s")

    @functools.partial(
        pl.kernel, mesh=mesh,
        out_type=jax.ShapeDtypeStruct((TOP_K, t, dp), ys.dtype),
        scratch_types=[pltpu.VMEM((TOP_K, n_ch, SC_CHUNK), jnp.int32),
                       pltpu.VMEM((2, SC_CHUNK, dp), ys.dtype),
                       pltpu.SemaphoreType.DMA((2,)), pltpu.SemaphoreType.DMA((2,))],
    )
    def gather(ys_hbm, p0_hbm, p1_hbm, out_hbm, idx_v, rows_v, gsem, wsem):
        wid = lax.axis_index("s") * ncores + lax.axis_index("c")
        base = wid * per_w
        pltpu.sync_copy(p0_hbm.at[wid], idx_v.at[0])
        pltpu.sync_copy(p1_hbm.at[wid], idx_v.at[1])
        units = [(c, k) for c in range(n_ch) for k in range(TOP_K)]

        def fetch(u):
            c, k = units[u]
            return pltpu.async_copy(ys_hbm.at[idx_v.at[k, c]], rows_v.at[u % 2], gsem.at[u % 2])

        fetches = {0: fetch(0)}
        writes = {}
        for u, (c, k) in enumerate(units):
            fetches[u].wait()
            writes[u] = pltpu.async_copy(rows_v.at[u % 2], out_hbm.at[k, pl.ds(base + c * SC_CHUNK, SC_CHUNK)],
                                         wsem.at[u % 2])
            if u >= 1:
                writes[u - 1].wait()
            if u + 1 < len(units):
                fetches[u + 1] = fetch(u + 1)
        writes[len(units) - 1].wait()

    return gather(ys, idx[0], idx[1])


def _final_kernel(h_ref, y_ref, route_ref, gain_ref, out_ref):
    r = route_ref[...]
    y = h_ref[...]
    for k, gate_lane in enumerate((ROUTE_G1, ROUTE_G2)):
        y = y + r[:, gate_lane:gate_lane + 1] * _unpack_bf16_pairs(y_ref[k])
    out_ref[...] = y * lax.rsqrt(jnp.mean(y * y, axis=-1, keepdims=True) + NORM_EPS) * gain_ref[...]


def _final(h, y_rows, route, gain, tm):
    t, d = h.shape
    return pl.pallas_call(
        _final_kernel,
        grid=(t // tm,),
        in_specs=[pl.BlockSpec((tm, d), lambda i: (i, 0)),
                  pl.BlockSpec((TOP_K, tm, d // 2), lambda i: (0, i, 0)),
                  pl.BlockSpec((tm, LANES), lambda i: (i, 0)),
                  pl.BlockSpec((1, d), lambda i: (0, 0))],
        out_specs=pl.BlockSpec((tm, d), lambda i: (i, 0)),
        out_shape=jax.ShapeDtypeStruct((t, d), F32),
        compiler_params=_cparams(("arbitrary",)),
        name="final",
    )(h, y_rows, route, gain.reshape(1, d))


def _layer(h3, mix_gain, w_in, attn_gain, decay_f, decay_b, ret_gain, w_out, ffn_gain,
           w_rg, b_rg, w_re, b_re, w_eg, w_eu, w_ed, final_gain):
    b, s, d = h3.shape
    t = b * s
    tm = 512
    attn_qkv, (qr, kr, vr, gr) = _inproj(h3, mix_gain, w_in.astype(BF16), tm)

    o_list, lse_list = [], []
    for window, dilation in DILATED_BRANCHES:
        reach = (window // 2) // dilation
        o, lse = _attn_branch(*attn_qkv[dilation], dilation, reach, tq=128, tqb=512)
        o_list.append(o)
        lse_list.append(lse)

    lg_f = jnp.log1p(-jnp.exp2(decay_f.astype(F32)))
    lg_b = jnp.log1p(-jnp.exp2(decay_b.astype(F32)))
    orr = _retention(qr, kr, vr, gr, lg_f, lg_b, ret_gain).reshape(t, HEAD_GROUP_WIDTH)

    n_route = MOE_GROUPS + N_EXPERTS
    w_router = jnp.zeros((d, LANES), F32).at[:, :n_route].set(jnp.concatenate([w_rg, w_re], axis=1).astype(F32))
    b_router = jnp.zeros((1, LANES), F32).at[0, :n_route].set(jnp.concatenate([b_rg, b_re]).astype(F32))
    h, hn_packed, logits = _outproj(h3.reshape(t, d), o_list, lse_list, orr, attn_gain, w_out.astype(BF16),
                                    ffn_gain, w_router, b_router, tm)

    route, counts8 = _route(logits, tm)

    n_blocks = -(-TOP_K * t // MOE_BLOCK) + N_EXPERTS
    counts = counts8[0, :N_EXPERTS].astype(jnp.int32)
    aligned = ((counts + SEG_ALIGN - 1) // SEG_ALIGN) * SEG_ALIGN
    seg_start = jnp.cumsum(aligned) - aligned
    expert_iota = jnp.arange(N_EXPERTS, dtype=jnp.int32)
    e12 = route[:, ROUTE_E1:ROUTE_E2 + 1].astype(jnp.int32)
    r12 = route[:, ROUTE_R1:ROUTE_R2 + 1].astype(jnp.int32)
    pos = r12 + jnp.sum(jnp.where(e12[..., None] == expert_iota, seg_start, 0), axis=-1)
    nblk = (counts + MOE_BLOCK - 1) // MOE_BLOCK
    blk_end = jnp.cumsum(nblk)
    blk_start = blk_end - nblk
    blk = jnp.arange(n_blocks, dtype=jnp.int32)[:, None]
    owner = (blk >= blk_start) & (blk < blk_end)
    local = (blk - blk_start) * MOE_BLOCK
    block_row0 = jnp.sum(jnp.where(owner, seg_start + local, 0), axis=-1).astype(jnp.int32)
    block_nreal = jnp.sum(jnp.where(owner, jnp.clip(counts - local, 0, MOE_BLOCK), 0), axis=-1).astype(jnp.int32)
    block_expert = jnp.minimum(jnp.sum((blk >= blk_end).astype(jnp.int32), axis=-1), N_EXPERTS - 1)

    n_sorted = TOP_K * t + SORTED_TAIL
    dump_row = n_sorted
    seg_end = seg_start + counts
    hole = seg_end[:, None] + jnp.arange(SEG_ALIGN, dtype=jnp.int32)[None, :]
    hole = jnp.where(hole < (seg_start + aligned)[:, None], hole, dump_row).reshape(-1)
    tail = (seg_start[-1] + aligned[-1]) + jnp.arange(SORTED_TAIL, dtype=jnp.int32)
    tail = jnp.where(tail < n_sorted, tail, dump_row)
    zero_pos = jnp.concatenate([hole, tail]).astype(jnp.int32)

    xs = _dispatch(hn_packed, pos, zero_pos)
    ys = _experts(xs, block_expert, block_nreal, block_row0, w_eg.astype(F32), w_eu.astype(F32),
                  w_ed.astype(F32))
    out = _final(h, _combine_gather(ys, pos), route, final_gain, tm)
    return out.reshape(b, s, d)


def kernel(x, mix_norm_gain, w_in, attn_out_gain, ret_decay_fwd, ret_decay_bwd, ret_out_gain, w_out,
           ffn_norm_gain, w_route_group, b_route_group, w_route_expert, b_route_expert,
           w_expert_gate, w_expert_up, w_expert_down, final_norm_gain):
    depth = mix_norm_gain.shape[0]
    assert depth == 1, "the final rmsnorm is fused into the single layer's combine kernel"
    l = 0
    return _layer(x, mix_norm_gain[l], w_in[l], attn_out_gain[l], ret_decay_fwd[l], ret_decay_bwd[l],
                  ret_out_gain[l], w_out[l], ffn_norm_gain[l], w_route_group[l], b_route_group[l],
                  w_route_expert[l], b_route_expert[l], w_expert_gate[l], w_expert_up[l], w_expert_down[l],
                  final_norm_gain)
```

```python
import functools

import numpy as np
import jax
import jax.numpy as jnp
from jax import lax
from jax.experimental import pallas as pl
from jax.experimental.pallas import tpu as pltpu
from jax.experimental.pallas import tpu_sc as plsc

F32 = jnp.float32
BF16 = jnp.bfloat16

ATTN_HEADS = 8
HEAD_DIM = 64
RET_HEADS = 8
HEAD_GROUP_WIDTH = 512
N_PROJ_GROUPS = 7
DILATED_BRANCHES = ((128, 1), (512, 4), (2048, 16))
ROPE_THETA = 500000.0
ROPE_DIM = HEAD_DIM // 4
RET_THETA = 10000.0
RET_CHUNK = 128
MOE_GROUPS = 4
EXPERTS_PER_GROUP = 8
N_EXPERTS = MOE_GROUPS * EXPERTS_PER_GROUP
MOE_BLOCK = 512
NORM_EPS = 1e-6
NEG_INF = -1e30

LANES = 128
VMEM_LIMIT = 56 * 1024 * 1024


def _cparams(sem):
    return pltpu.CompilerParams(dimension_semantics=sem, vmem_limit_bytes=VMEM_LIMIT)


def _rotary_tables(seq, half, freqs):
    pos = np.arange(seq, dtype=np.float64)[:, None]
    ang = pos * freqs[None, :].astype(np.float64)
    cos, sin = np.cos(ang), np.sin(ang)
    c = np.ones((seq, HEAD_DIM)); sp = np.zeros((seq, HEAD_DIM)); sm = np.zeros((seq, HEAD_DIM))
    c[:, :half] = cos; c[:, half:2 * half] = cos
    sp[:, half:2 * half] = sin
    sm[:, :half] = -sin
    rep = LANES // HEAD_DIM
    return tuple(jnp.asarray(np.tile(t, (1, rep)), F32) for t in (c, sp, sm))


def _rotate(t, c, sp, sm, half):
    outs = []
    for g in range(t.shape[1] // LANES):
        tg = t[:, g * LANES:(g + 1) * LANES]
        outs.append(tg * c + pltpu.roll(tg, half, 1) * sp + pltpu.roll(tg, LANES - half, 1) * sm)
    return jnp.concatenate(outs, axis=1)


CLASS_DILATIONS = tuple(d for _, d in DILATED_BRANCHES if d > 1)
ATTN_Q_SCALE = float(np.log2(np.e)) * HEAD_DIM ** -0.5


def _inproj_kernel(x_ref, gain_ref, w_ref, ca_ref, spa_ref, sma_ref, cr_ref, spr_ref, smr_ref, *rest):
    n_cls = len(CLASS_DILATIONS)
    nat_refs = rest[0:3]
    cls_refs = [rest[3 + 3 * c:6 + 3 * c] for c in range(n_cls)]
    qr_ref, kr_ref, vr_ref, gr_ref = rest[3 + 3 * n_cls:7 + 3 * n_cls]
    stage_ref = rest[7 + 3 * n_cls]
    x = x_ref[0]
    tm = x.shape[0]
    ms = jnp.mean(x * x, axis=-1, keepdims=True)
    xn = (x * lax.rsqrt(ms + NORM_EPS) * gain_ref[...]).astype(BF16)
    gw = HEAD_GROUP_WIDTH

    def proj(c):
        return jnp.dot(xn, w_ref[:, c * gw:(c + 1) * gw], preferred_element_type=F32)

    a_tabs = (ca_ref[...], spa_ref[...], sma_ref[...])
    r_tabs = (cr_ref[...], spr_ref[...], smr_ref[...])
    attn_vals = ((_rotate(proj(0), *a_tabs, ROPE_DIM // 2) * ATTN_Q_SCALE),
                 _rotate(proj(1), *a_tabs, ROPE_DIM // 2),
                 proj(2))
    for j, val in enumerate(attn_vals):
        nat_refs[j][0] = val.astype(BF16)
        for g in range(gw // LANES):
            stage_ref[g] = val[:, g * LANES:(g + 1) * LANES]
        for c, d in enumerate(CLASS_DILATIONS):
            for r in range(d):
                for g in range(gw // LANES):
                    col = r * gw + g * LANES
                    cls_refs[c][j][0, :, col:col + LANES] = (
                        stage_ref[g, pl.ds(r, tm // d, stride=d), :].astype(BF16))
    qr_ref[0] = _rotate(proj(3), *r_tabs, HEAD_DIM // 2).astype(BF16)
    kr_ref[0] = (_rotate(proj(4), *r_tabs, HEAD_DIM // 2) * (HEAD_DIM ** -0.5)).astype(BF16)
    vr_ref[0] = proj(5).astype(BF16)
    g = proj(6)
    gr_ref[0] = (g * jax.nn.sigmoid(g)).astype(BF16)


def _inproj(x, gain, w_in_bf16, tm):
    b, s, d = x.shape
    rope_freqs = ROPE_THETA ** (-np.arange(0, ROPE_DIM, 2, dtype=np.float32) / ROPE_DIM)
    ret_freqs = RET_THETA ** (-np.linspace(0.0, 1.0, HEAD_DIM // 2, dtype=np.float32))
    tabs = _rotary_tables(s, ROPE_DIM // 2, rope_freqs) + _rotary_tables(s, HEAD_DIM // 2, ret_freqs)
    gw = HEAD_GROUP_WIDTH
    tab_spec = pl.BlockSpec((tm, LANES), lambda si, bi: (si, 0))

    def view(dil):
        return (pl.BlockSpec((1, tm // dil, dil * gw), lambda si, bi: (bi, si, 0)),
                jax.ShapeDtypeStruct((b, s // dil, dil * gw), BF16))

    views = [view(1)] * 3 + [view(dil) for dil in CLASS_DILATIONS for _ in range(3)] + [view(1)] * 4
    outs = pl.pallas_call(
        _inproj_kernel,
        grid=(s // tm, b),
        in_specs=[pl.BlockSpec((1, tm, d), lambda si, bi: (bi, si, 0)),
                  pl.BlockSpec((1, d), lambda si, bi: (0, 0)),
                  pl.BlockSpec(w_in_bf16.shape, lambda si, bi: (0, 0))] + [tab_spec] * 6,
        out_specs=[v[0] for v in views],
        out_shape=[v[1] for v in views],
        scratch_shapes=[pltpu.VMEM((gw // LANES, tm, LANES), F32)],
        compiler_params=_cparams(("arbitrary", "arbitrary")),
        name="inproj",
    )(x, gain.reshape(1, d), w_in_bf16, *tabs)
    n_attn = 3 * (1 + len(CLASS_DILATIONS))
    attn_qkv = {dil: outs[3 * c:3 * c + 3] for c, dil in enumerate((1,) + CLASS_DILATIONS)}
    return attn_qkv, outs[n_attn:]


ATTN_WINDOW_CASES = 3


def _attn_kernel(q_ref, k_ref, v_ref, o_ref, lse_ref, bias_ref, *, length, tq, reach):
    tqb = q_ref.shape[1]
    win = tq + 2 * reach
    heads_per_pair = LANES // HEAD_DIM
    qi = pl.program_id(2)
    lane = lax.broadcasted_iota(jnp.int32, (1, LANES), 1)
    lane_t = lax.broadcasted_iota(jnp.int32, (tq, LANES), 1)
    ones = jnp.ones((win, LANES), BF16)

    @pl.when((pl.program_id(0) == 0) & (pl.program_id(1) == 0) & (qi == 0))
    def _():
        diff = (lax.broadcasted_iota(jnp.int32, (heads_per_pair * tq, win), 1)
                - (lax.broadcasted_iota(jnp.int32, (heads_per_pair * tq, win), 0) & (tq - 1)))
        for case in range(ATTN_WINDOW_CASES):
            off = case * reach
            bias_ref[case] = jnp.where((diff >= off - reach) & (diff <= off + reach), 0.0, NEG_INF)

    def sub(t, cls):
        q0 = qi * tqb + t * tq
        ws = jnp.clip(q0 - reach, 0, length - win)
        ws = pl.multiple_of(ws, reach)
        bias = bias_ref[(q0 - ws) // reach]
        rows = pl.ds(t * tq, tq)
        m_tile = jnp.zeros((tq, LANES), F32)
        l_tile = jnp.ones((tq, LANES), F32)
        for g in range(HEAD_GROUP_WIDTH // LANES):
            cols = slice(cls * HEAD_GROUP_WIDTH + g * LANES, cls * HEAD_GROUP_WIDTH + (g + 1) * LANES)
            qg = q_ref[0, rows, cols]
            kw = k_ref[0, pl.ds(ws, win), cols]
            v_ones = jnp.concatenate([v_ref[0, pl.ds(ws, win), cols], ones], axis=1)
            hms = [(lane >= hh * HEAD_DIM) & (lane < (hh + 1) * HEAD_DIM) for hh in range(heads_per_pair)]
            q2 = jnp.concatenate([qg * hm.astype(BF16) for hm in hms], axis=0)
            sc = lax.dot_general(q2, kw, (((1,), (1,)), ((), ())), preferred_element_type=F32) + bias
            m = jnp.max(sc, axis=-1, keepdims=True)
            p = jnp.exp2(sc - m).astype(BF16)
            pv = jnp.dot(p, v_ones, preferred_element_type=F32)
            l = pv[:, LANES:]
            o = pv[:, :LANES] / l
            o_pair = jnp.zeros((tq, LANES), F32)
            for hh in range(heads_per_pair):
                part = slice(hh * tq, (hh + 1) * tq)
                head_lane = lane_t == g * heads_per_pair + hh
                o_pair = jnp.where(hms[hh], o[part], o_pair)
                m_tile = jnp.where(head_lane, m[part], m_tile)
                l_tile = jnp.where(head_lane, l[part], l_tile)
            o_ref[0, rows, cols] = o_pair.astype(BF16)
        lse_ref[0, rows, cls * LANES:(cls + 1) * LANES] = m_tile + jnp.log2(l_tile)

    for cls in range(q_ref.shape[2] // HEAD_GROUP_WIDTH):
        for t in range(tqb // tq):
            sub(t, cls)


def _attn_branch(qc, kc, vc, dilation, reach, tq, tqb, sub_tiles=4):
    b, length, dw = qc.shape
    w = dw // dilation
    tqb = min(tqb, length)
    assert tq % reach == 0 and tq > reach and length % tq == 0 and length >= tq + 2 * reach
    ncls = min(dilation, max(1, sub_tiles // (tqb // tq)))
    q_spec = pl.BlockSpec((1, tqb, ncls * w), lambda bi, r, qi: (bi, qi, r))
    kv_spec = pl.BlockSpec((1, length, ncls * w), lambda bi, r, qi: (bi, 0, r))
    o, lse = pl.pallas_call(
        functools.partial(_attn_kernel, length=length, tq=tq, reach=reach),
        grid=(b, dilation // ncls, length // tqb),
        in_specs=[q_spec, kv_spec, kv_spec],
        out_specs=[q_spec, pl.BlockSpec((1, tqb, ncls * LANES), lambda bi, r, qi: (bi, qi, r))],
        out_shape=[jax.ShapeDtypeStruct((b, length, dilation * w), BF16),
                   jax.ShapeDtypeStruct((b, length, dilation * LANES), F32)],
        scratch_shapes=[pltpu.VMEM((ATTN_WINDOW_CASES, (LANES // HEAD_DIM) * tq, tq + 2 * reach), F32)],
        compiler_params=_cparams(("arbitrary", "arbitrary", "arbitrary")),
        name=f"attn_d{dilation}",
    )(qc, kc, vc)
    return o, lse


RET_TAB_QF, RET_TAB_QB, RET_TAB_KF, RET_TAB_KB = range(4)


def _retention_kernel(lgf_ref, lgb_ref, q_ref, k_ref, v_ref, g_ref, gain_ref, o_ref,
                      tab_ref, dec_ref, sb_ref, st_ref, *, chunk, unroll):
    c = chunk
    n = q_ref.shape[1] // c
    width = q_ref.shape[2]
    n_pairs = width // LANES
    heads_per_pair = LANES // HEAD_DIM
    n_heads = n_pairs * heads_per_pair
    head0 = pl.program_id(1) * n_heads
    lane_w = lax.broadcasted_iota(jnp.int32, (1, width), 1)
    lgf = [lgf_ref[head0 + hd] for hd in range(n_heads)]
    lgb = [lgb_ref[head0 + hd] for hd in range(n_heads)]
    lgf_lane = jnp.zeros((1, width), F32)
    lgb_lane = jnp.zeros((1, width), F32)
    for hd in range(n_heads):
        in_head = (lane_w >= hd * HEAD_DIM) & (lane_w < (hd + 1) * HEAD_DIM)
        lgf_lane = jnp.where(in_head, lgf[hd], lgf_lane)
        lgb_lane = jnp.where(in_head, lgb[hd], lgb_lane)
    idx = lax.broadcasted_iota(jnp.int32, (c, width), 0).astype(F32)
    tab_ref[RET_TAB_QF] = jnp.exp((idx + 1.0) * lgf_lane)
    tab_ref[RET_TAB_QB] = jnp.exp((c - idx) * lgb_lane)
    tab_ref[RET_TAB_KF] = jnp.exp((c - 1.0 - idx) * lgf_lane)
    tab_ref[RET_TAB_KB] = jnp.exp(idx * lgb_lane)
    sdf = jnp.exp(c * lgf_lane)
    sdb = jnp.exp(c * lgb_lane)
    dmat = (lax.broadcasted_iota(jnp.int32, (c, c), 0)
            - lax.broadcasted_iota(jnp.int32, (c, c), 1)).astype(F32)
    for hd in range(n_heads):
        dec_ref[hd // heads_per_pair, :, (hd % heads_per_pair) * c:(hd % heads_per_pair + 1) * c] = (
            jnp.where(dmat >= 0, jnp.exp(dmat * lgf[hd]), jnp.exp(-dmat * lgb[hd])))
    lane = lax.broadcasted_iota(jnp.int32, (1, LANES), 1)
    lane_hi = lane >= HEAD_DIM
    head_masks = [((lane >= hh * HEAD_DIM) & (lane < (hh + 1) * HEAD_DIM)).astype(BF16)
                  for hh in range(heads_per_pair)]
    row_hi = lax.broadcasted_iota(jnp.int32, (LANES, LANES), 0) >= HEAD_DIM
    col_hi = lax.broadcasted_iota(jnp.int32, (LANES, LANES), 1) >= HEAD_DIM
    blockdiag = row_hi == col_hi

    def kv_state(kd, vv):
        kt = jnp.transpose(kd).astype(BF16)
        return jnp.where(blockdiag, jnp.dot(kt, vv, preferred_element_type=F32), 0.0)

    st_ref[...] = jnp.zeros_like(st_ref)
    sb_ref[n - 1] = jnp.zeros(sb_ref.shape[1:], sb_ref.dtype)

    def back(i, carry):
        nn = n - 1 - i
        rows = pl.ds(pl.multiple_of(nn * c, c), c)
        for p in range(n_pairs):
            cols = slice(p * LANES, (p + 1) * LANES)
            kd = k_ref[0, rows, cols].astype(F32) * tab_ref[RET_TAB_KB, :, cols]
            new = st_ref[p] * sdb[:, cols] + kv_state(kd, v_ref[0, rows, cols])
            st_ref[p] = new
            sb_ref[nn - 1, p] = new.astype(BF16)
        return carry

    lax.fori_loop(0, n - 1, back, 0, unroll=unroll)

    st_ref[...] = jnp.zeros_like(st_ref)

    def fwd(nn, carry):
        rows = pl.ds(pl.multiple_of(nn * c, c), c)
        for p in range(n_pairs):
            cols = slice(p * LANES, (p + 1) * LANES)
            qq = q_ref[0, rows, cols]
            kk = k_ref[0, rows, cols]
            vv = v_ref[0, rows, cols]
            qf = qq.astype(F32)
            sf = st_ref[p]
            qcat = jnp.concatenate([(qf * tab_ref[RET_TAB_QF, :, cols]).astype(BF16),
                                    (qf * tab_ref[RET_TAB_QB, :, cols]).astype(BF16)], axis=1)
            scat = jnp.concatenate([sf.astype(BF16), sb_ref[nn, p]], axis=0)
            o = jnp.dot(qcat, scat, preferred_element_type=F32)
            k2 = jnp.concatenate([kk * hm for hm in head_masks], axis=0)
            v2 = jnp.concatenate([vv * hm for hm in head_masks], axis=0)
            sc = lax.dot_general(qq, k2, (((1,), (1,)), ((), ())), preferred_element_type=F32)
            o = o + jnp.dot((sc * dec_ref[p]).astype(BF16), v2, preferred_element_type=F32)
            s_lo = jnp.sum(jnp.where(lane_hi, 0.0, o), axis=-1, keepdims=True)
            s_hi = jnp.sum(jnp.where(lane_hi, o, 0.0), axis=-1, keepdims=True)
            mu = jnp.where(lane_hi, s_hi, s_lo) * (1.0 / HEAD_DIM)
            dev = o - mu
            d2 = dev * dev
            v_lo = jnp.sum(jnp.where(lane_hi, 0.0, d2), axis=-1, keepdims=True)
            v_hi = jnp.sum(jnp.where(lane_hi, d2, 0.0), axis=-1, keepdims=True)
            var = jnp.where(lane_hi, v_hi, v_lo) * (1.0 / HEAD_DIM)
            out = dev * lax.rsqrt(var + NORM_EPS) * gain_ref[:, cols] * g_ref[0, rows, cols].astype(F32)
            o_ref[0, rows, cols] = out.astype(BF16)
            kd = kk.astype(F32) * tab_ref[RET_TAB_KF, :, cols]
            st_ref[p] = sf * sdf[:, cols] + kv_state(kd, vv)
        return carry

    lax.fori_loop(0, n, fwd, 0, unroll=unroll)


def _retention(qr, kr, vr, gate, lg_f, lg_b, out_gain, width=512, unroll=4):
    b, s, w = qr.shape
    n_pairs = width // LANES
    n_heads = width // HEAD_DIM
    spec = pl.BlockSpec((1, s, width), lambda bi, p, *_: (bi, 0, p))
    grid_spec = pltpu.PrefetchScalarGridSpec(
        num_scalar_prefetch=2,
        grid=(b, w // width),
        in_specs=[spec, spec, spec, spec, pl.BlockSpec((1, width), lambda bi, p, *_: (0, p))],
        out_specs=spec,
        scratch_shapes=[pltpu.VMEM((4, RET_CHUNK, width), F32),
                        pltpu.VMEM((n_pairs, RET_CHUNK, (LANES // HEAD_DIM) * RET_CHUNK), F32),
                        pltpu.VMEM((s // RET_CHUNK, n_pairs, LANES, LANES), BF16),
                        pltpu.VMEM((n_pairs, LANES, LANES), F32)],
    )
    return pl.pallas_call(
        functools.partial(_retention_kernel, chunk=RET_CHUNK, unroll=unroll),
        grid_spec=grid_spec,
        out_shape=jax.ShapeDtypeStruct((b, s, w), BF16),
        compiler_params=_cparams(("arbitrary", "arbitrary")),
        name="retention",
    )(lg_f, lg_b, qr, kr, vr, gate, out_gain.reshape(1, w))


def _split_bf16(t):
    hi = t.astype(BF16)
    lo = (t - hi.astype(F32)).astype(BF16)
    return hi, lo


def _pack_bf16_pairs(t):
    n = t.shape[1] // 2
    hi = pltpu.bitcast(t[:, :n].astype(BF16).astype(F32), jnp.uint32)
    lo = pltpu.bitcast(t[:, n:].astype(BF16).astype(F32), jnp.uint32)
    return hi | (lo >> 16)


def _unpack_bf16_pairs(u):
    hi = pltpu.bitcast(u & jnp.uint32(0xFFFF0000), F32)
    lo = pltpu.bitcast(u << 16, F32)
    return jnp.concatenate([hi, lo], axis=1)


def _outproj_kernel(x_ref, o1_ref, o2_ref, o3_ref, l1_ref, l2_ref, l3_ref, orr_ref, ga_ref, expand_ref,
                    wout_ref, gf_ref, wr_ref, br_ref, h_ref, hn_ref, logit_ref,
                    *nat_refs):
    tm = x_ref.shape[0]
    gw = HEAD_GROUP_WIDTH
    os, ls = [], []
    for (_, dil), o_ref, l_ref in zip(DILATED_BRANCHES, (o1_ref, o2_ref, o3_ref), (l1_ref, l2_ref, l3_ref)):
        if dil == 1:
            os.append(o_ref[...].astype(F32))
            ls.append(l_ref[...])
            continue
        c = CLASS_DILATIONS.index(dil)
        onat_ref, lnat_ref = nat_refs[2 * c], nat_refs[2 * c + 1]
        for r in range(dil):
            rows = pl.ds(r, tm // dil, stride=dil)
            for g in range(gw // LANES):
                col = r * gw + g * LANES
                onat_ref[g, rows, :] = o_ref[:, col:col + LANES].astype(F32)
            lnat_ref[rows, :] = l_ref[:, r * LANES:(r + 1) * LANES]
        os.append(jnp.concatenate([onat_ref[g] for g in range(gw // LANES)], axis=1))
        ls.append(lnat_ref[...])
    mx = jnp.maximum(jnp.maximum(ls[0], ls[1]), ls[2])
    es = [jnp.exp2(l - mx) for l in ls]
    inv = 1.0 / (es[0] + es[1] + es[2])
    expand = expand_ref[...]
    oa = jnp.zeros((tm, gw), F32)
    for e, o in zip(es, os):
        wexp = jnp.dot(jnp.concatenate(_split_bf16(e * inv), axis=1), expand, preferred_element_type=F32)
        oa = oa + wexp * o
    oa = oa * lax.rsqrt(jnp.mean(oa * oa, axis=-1, keepdims=True) + NORM_EPS) * ga_ref[...]
    mixed = jnp.concatenate([oa.astype(BF16), orr_ref[...]], axis=1)
    h = x_ref[...] + jnp.dot(mixed, wout_ref[...], preferred_element_type=F32)
    h_ref[...] = h
    hn = h * lax.rsqrt(jnp.mean(h * h, axis=-1, keepdims=True) + NORM_EPS) * gf_ref[...]
    hn_ref[...] = _pack_bf16_pairs(hn)
    prod = jnp.dot(jnp.concatenate(_split_bf16(hn), axis=0), wr_ref[...], preferred_element_type=F32)
    logit_ref[...] = prod[:tm, :LANES] + prod[:tm, LANES:] + prod[tm:, :LANES] + br_ref[...]


def _outproj(x2, o_list, lse_list, orr, attn_gain, w_out_bf16, ffn_gain, w_router, b_router, tm):
    t, d = x2.shape
    w = HEAD_GROUP_WIDTH
    expand = np.zeros((LANES, w), np.float32)
    for hd in range(ATTN_HEADS):
        expand[hd, hd * HEAD_DIM:(hd + 1) * HEAD_DIM] = 1.0
    expand = jnp.asarray(np.concatenate([expand, expand], axis=0), BF16)
    wr_hi = w_router.astype(BF16)
    wr = jnp.concatenate([wr_hi, (w_router - wr_hi.astype(F32)).astype(BF16)], axis=1)
    row = lambda width, dil=1: pl.BlockSpec((tm // dil, dil * width), lambda i: (i, 0))
    full = lambda a: pl.BlockSpec(a.shape, lambda i: (0,) * a.ndim)
    ga = attn_gain.reshape(1, w)
    gf = ffn_gain.reshape(1, d)
    dils = [dil for _, dil in DILATED_BRANCHES]
    o_flat = [o.reshape(t // dil, dil * w) for o, dil in zip(o_list, dils)]
    l_flat = [l.reshape(t // dil, dil * LANES) for l, dil in zip(lse_list, dils)]
    nat_scratch = []
    for _ in CLASS_DILATIONS:
        nat_scratch += [pltpu.VMEM((w // LANES, tm, LANES), F32), pltpu.VMEM((tm, LANES), F32)]
    return pl.pallas_call(
        _outproj_kernel,
        grid=(t // tm,),
        in_specs=[row(d)] + [row(w, dil) for dil in dils] + [row(LANES, dil) for dil in dils] + [row(w)]
                 + [full(ga), full(expand), full(w_out_bf16), full(gf), full(wr), full(b_router)],
        out_specs=[row(d), row(d // 2), row(LANES)],
        out_shape=[jax.ShapeDtypeStruct((t, d), F32),
                   jax.ShapeDtypeStruct((t, d // 2), jnp.uint32),
                   jax.ShapeDtypeStruct((t, LANES), F32)],
        scratch_shapes=nat_scratch,
        compiler_params=_cparams(("arbitrary",)),
        name="outproj",
    )(x2, *o_flat, *l_flat, orr, ga, expand, w_out_bf16, gf, wr, b_router)


ROUTE_E1, ROUTE_E2, ROUTE_G1, ROUTE_G2, ROUTE_R1, ROUTE_R2 = range(6)
GROUP_LANE0 = N_EXPERTS
SUBLANES = 8


def _route_kernel(logit_ref, tri_ref, route_ref, count_ref, run_ref):
    @pl.when(pl.program_id(0) == 0)
    def _():
        run_ref[...] = jnp.zeros_like(run_ref)

    lg_t = jnp.transpose(logit_ref[...])
    tm = lg_t.shape[1]
    assert EXPERTS_PER_GROUP == SUBLANES and MOE_GROUPS <= SUBLANES
    rid = lax.broadcasted_iota(jnp.int32, (SUBLANES, tm), 0)
    big = jnp.int32(1 << 20)

    def top(vals):
        m = jnp.max(vals, axis=0, keepdims=True)
        i = jnp.min(jnp.where(vals == m, rid, big), axis=0, keepdims=True)
        return m, i

    gl = jnp.where(rid < MOE_GROUPS, lg_t[GROUP_LANE0:GROUP_LANE0 + SUBLANES], -jnp.inf)
    gmax, gidx = top(gl)
    group_gate = 1.0 / jnp.sum(jnp.exp(gl - gmax), axis=0, keepdims=True)
    el = lg_t[0:EXPERTS_PER_GROUP]
    for g in range(1, MOE_GROUPS):
        el = jnp.where(gidx == g, lg_t[g * EXPERTS_PER_GROUP:(g + 1) * EXPERTS_PER_GROUP], el)
    t1, i1 = top(el)
    t2, i2 = top(jnp.where(rid == i1, -jnp.inf, el))
    e21 = jnp.exp(t2 - t1)
    g1 = group_gate / (1.0 + e21)
    g2 = group_gate * e21 / (1.0 + e21)
    e1 = gidx * EXPERTS_PER_GROUP + i1
    e2 = gidx * EXPERTS_PER_GROUP + i2
    erow = lax.broadcasted_iota(jnp.int32, (N_EXPERTS, tm), 0)
    oh1 = erow == e1
    oh2 = erow == e2
    cnt = oh1.astype(F32) + oh2.astype(F32)
    run = run_ref[:, 0:1]
    prefix = jnp.dot(cnt.astype(BF16), tri_ref[...], preferred_element_type=F32) + run
    r1 = jnp.sum(jnp.where(oh1, prefix, 0.0), axis=0, keepdims=True)
    r2 = jnp.sum(jnp.where(oh2, prefix, 0.0), axis=0, keepdims=True)
    new_run = jnp.broadcast_to(run + jnp.sum(cnt, axis=1, keepdims=True), run_ref.shape)
    run_ref[...] = new_run
    count_ref[...] = new_run
    out = jnp.zeros((SUBLANES, tm), F32)
    for row, val in ((ROUTE_E1, e1.astype(F32)), (ROUTE_E2, e2.astype(F32)), (ROUTE_G1, g1),
                     (ROUTE_G2, g2), (ROUTE_R1, r1), (ROUTE_R2, r2)):
        out = jnp.where(rid == row, val, out)
    out_t = jnp.concatenate([out, jnp.zeros((LANES - SUBLANES, tm), F32)], axis=0)
    route_ref[...] = jnp.transpose(out_t)


def _route(logits, tm):
    t = logits.shape[0]
    tri = jnp.asarray(np.triu(np.ones((tm, tm), np.float32), 1), BF16)
    return pl.pallas_call(
        _route_kernel,
        grid=(t // tm,),
        in_specs=[pl.BlockSpec((tm, LANES), lambda i: (i, 0)), pl.BlockSpec((tm, tm), lambda i: (0, 0))],
        out_specs=[pl.BlockSpec((tm, LANES), lambda i: (i, 0)),
                   pl.BlockSpec((N_EXPERTS, LANES), lambda i: (0, 0))],
        out_shape=[jax.ShapeDtypeStruct((t, LANES), F32), jax.ShapeDtypeStruct((N_EXPERTS, LANES), F32)],
        scratch_shapes=[pltpu.VMEM((N_EXPERTS, LANES), F32)],
        compiler_params=_cparams(("arbitrary",)),
        name="route",
    )(logits, tri)


TOP_K = 2


SEG_ALIGN = 8
SORTED_TAIL = N_EXPERTS * SEG_ALIGN + MOE_BLOCK


def _tail_pieces():
    full, rest = divmod(SORTED_TAIL, MOE_BLOCK)
    return [MOE_BLOCK] * full + ([rest] if rest else [])


SC_CHUNK = 64
ZERO_ROWS = N_EXPERTS * SEG_ALIGN + SORTED_TAIL
SORTED_ROWS_EXTRA = SORTED_TAIL + SEG_ALIGN


def _sc_workers():
    info = plsc.get_sparse_core_info()
    return info.num_cores, info.num_cores * info.num_subcores


def _dispatch(hn_packed, pos, zero_pos):
    t, dp = hn_packed.shape
    ncores, nw = _sc_workers()
    per_w = t // nw
    n_ch = per_w // SC_CHUNK
    z_ch = ZERO_ROWS // SC_CHUNK
    assert per_w % SC_CHUNK == 0 and ZERO_ROWS % SC_CHUNK == 0 and z_ch <= nw and TOP_K == 2
    idx = [pos[:, k].reshape(nw, n_ch, SC_CHUNK) for k in range(TOP_K)]
    zeros = jnp.zeros((SC_CHUNK, dp), hn_packed.dtype)
    mesh = plsc.VectorSubcoreMesh(core_axis_name="c", subcore_axis_name="s")

    @functools.partial(
        pl.kernel, mesh=mesh,
        out_type=jax.ShapeDtypeStruct((TOP_K * t + SORTED_ROWS_EXTRA, dp), hn_packed.dtype),
        scratch_types=[pltpu.VMEM((n_ch, SC_CHUNK), jnp.int32), pltpu.VMEM((n_ch, SC_CHUNK), jnp.int32),
                       pltpu.VMEM((SC_CHUNK,), jnp.int32),
                       pltpu.VMEM((2, SC_CHUNK, dp), hn_packed.dtype),
                       pltpu.SemaphoreType.DMA((2,)), pltpu.SemaphoreType.DMA((2,))],
    )
    def scatter(hn_hbm, p0_hbm, p1_hbm, zpos_hbm, zeros_hbm, xs_hbm, i0_v, i1_v, iz_v, rows_v, lsem, ssem):
        wid = lax.axis_index("s") * ncores + lax.axis_index("c")
        base = wid * per_w

        @pl.when(wid < z_ch)
        def _():
            pltpu.sync_copy(zpos_hbm.at[wid], iz_v)
            pltpu.sync_copy(zeros_hbm, rows_v.at[0])
            pltpu.sync_copy(rows_v.at[0], xs_hbm.at[iz_v])

        pltpu.sync_copy(p0_hbm.at[wid], i0_v)
        pltpu.sync_copy(p1_hbm.at[wid], i1_v)

        def load(c):
            return pltpu.async_copy(hn_hbm.at[pl.ds(base + c * SC_CHUNK, SC_CHUNK)], rows_v.at[c % 2],
                                    lsem.at[c % 2])

        loads = {0: load(0)}
        scat = {}
        for c in range(n_ch):
            slot = c % 2
            loads[c].wait()
            scat[c] = (pltpu.async_copy(rows_v.at[slot], xs_hbm.at[i0_v.at[c]], ssem.at[slot]),
                       pltpu.async_copy(rows_v.at[slot], xs_hbm.at[i1_v.at[c]], ssem.at[slot]))
            if c >= 1:
                for d in scat[c - 1]:
                    d.wait()
            if c + 1 < n_ch:
                loads[c + 1] = load(c + 1)
        for d in scat[n_ch - 1]:
            d.wait()

    return scatter(hn_packed, idx[0], idx[1], zero_pos.reshape(z_ch, SC_CHUNK), zeros)


def _expert_kernel(bexp_ref, nreal_ref, row0_ref, xs_hbm, wg_ref, wu_ref, wd_ref, ys_hbm,
                   xbuf, ybuf, wg_bf, wu_bf, wd_bf, isem, osem, *, n_rows):
    i = pl.program_id(0)
    nb = pl.num_programs(0)
    slot = i % 2
    nslot = 1 - slot
    n_cur = nreal_ref[i]
    prev = jnp.maximum(i - 1, 0)
    nxt = jnp.minimum(i + 1, nb - 1)

    def in_copy(blk, s):
        row0 = pl.multiple_of(row0_ref[blk], SEG_ALIGN)
        return pltpu.make_async_copy(xs_hbm.at[pl.ds(row0, MOE_BLOCK)], xbuf.at[s], isem.at[s])

    def out_copy(blk, s):
        row0 = pl.multiple_of(row0_ref[blk], SEG_ALIGN)
        return pltpu.make_async_copy(ybuf.at[s], ys_hbm.at[pl.ds(row0, MOE_BLOCK)], osem.at[s])

    @pl.when(i == 0)
    def _():
        ybuf[...] = jnp.zeros_like(ybuf)
        tails = [pltpu.make_async_copy(ybuf.at[s, pl.ds(0, size)], ys_hbm.at[pl.ds(n_rows + s * MOE_BLOCK, size)],
                                       osem.at[s])
                 for s, size in enumerate(_tail_pieces())]
        for tail in tails:
            tail.start()
        for tail in tails:
            tail.wait()

        @pl.when(n_cur > 0)
        def _():
            in_copy(i, slot).start()

    @pl.when((i + 1 < nb) & (nreal_ref[nxt] > 0))
    def _():
        in_copy(nxt, nslot).start()

    @pl.when((n_cur > 0) & ((i == 0) | (bexp_ref[i] != bexp_ref[prev])))
    def _():
        wg_bf[...] = wg_ref[0].astype(BF16)
        wu_bf[...] = wu_ref[0].astype(BF16)
        wd_bf[...] = wd_ref[0].astype(BF16)

    @pl.when(n_cur > 0)
    def _():
        in_copy(i, slot).wait()
        xb = _unpack_bf16_pairs(xbuf[slot]).astype(BF16)
        gate = jnp.dot(xb, wg_bf[...], preferred_element_type=F32)
        up = jnp.dot(xb, wu_bf[...], preferred_element_type=F32)
        hid = (gate * jax.nn.sigmoid(gate) * up).astype(BF16)
        ybuf[slot] = _pack_bf16_pairs(jnp.dot(hid, wd_bf[...], preferred_element_type=F32))

    @pl.when((i >= 1) & (nreal_ref[prev] > 0))
    def _():
        out_copy(prev, nslot).wait()

    @pl.when(n_cur > 0)
    def _():
        out_copy(i, slot).start()

        @pl.when(i == nb - 1)
        def _():
            out_copy(i, slot).wait()


def _experts(xs, block_expert, block_nreal, block_row0, wg, wu, wd):
    n_blocks = block_expert.shape[0]
    assert n_blocks >= 2 and len(_tail_pieces()) <= 2
    n_rows_pad = xs.shape[0] - SORTED_ROWS_EXTRA + SORTED_TAIL
    dp = xs.shape[1]
    _, d, ff = wg.shape
    grid_spec = pltpu.PrefetchScalarGridSpec(
        num_scalar_prefetch=3,
        grid=(n_blocks,),
        in_specs=[pl.BlockSpec(memory_space=pl.ANY),
                  pl.BlockSpec((1, d, ff), lambda i, be, nr, r0: (be[i], 0, 0)),
                  pl.BlockSpec((1, d, ff), lambda i, be, nr, r0: (be[i], 0, 0)),
                  pl.BlockSpec((1, ff, d), lambda i, be, nr, r0: (be[i], 0, 0))],
        out_specs=pl.BlockSpec(memory_space=pl.ANY),
        scratch_shapes=[pltpu.VMEM((2, MOE_BLOCK, dp), jnp.uint32),
                        pltpu.VMEM((2, MOE_BLOCK, dp), jnp.uint32),
                        pltpu.VMEM((d, ff), BF16),
                        pltpu.VMEM((d, ff), BF16),
                        pltpu.VMEM((ff, d), BF16),
                        pltpu.SemaphoreType.DMA((2,)),
                        pltpu.SemaphoreType.DMA((2,))],
    )
    return pl.pallas_call(
        functools.partial(_expert_kernel, n_rows=n_rows_pad - SORTED_TAIL),
        grid_spec=grid_spec,
        out_shape=jax.ShapeDtypeStruct((n_rows_pad, dp), jnp.uint32),
        compiler_params=_cparams(("arbitrary",)),
        name="experts",
    )(block_expert, block_nreal, block_row0, xs, wg, wu, wd)


def _combine_gather(ys, pos):
    t = pos.shape[0]
    dp = ys.shape[1]
    ncores, nw = _sc_workers()
    per_w = t // nw
    n_ch = per_w // SC_CHUNK
    assert per_w % SC_CHUNK == 0
    idx = [pos[:, k].reshape(nw, n_ch, SC_CHUNK) for k in range(TOP_K)]
    mesh = plsc.VectorSubcoreMesh(core_axis_name="c", subcore_axis_name="s")

    @functools.partial(
        pl.kernel, mesh=mesh,
        out_type=jax.ShapeDtypeStruct((TOP_K, t, dp), ys.dtype),
        scratch_types=[pltpu.VMEM((TOP_K, n_ch, SC_CHUNK), jnp.int32),
                       pltpu.VMEM((2, SC_CHUNK, dp), ys.dtype),
                       pltpu.SemaphoreType.DMA((2,)), pltpu.SemaphoreType.DMA((2,))],
    )
    def gather(ys_hbm, p0_hbm, p1_hbm, out_hbm, idx_v, rows_v, gsem, wsem):
        wid = lax.axis_index("s") * ncores + lax.axis_index("c")
        base = wid * per_w
        pltpu.sync_copy(p0_hbm.at[wid], idx_v.at[0])
        pltpu.sync_copy(p1_hbm.at[wid], idx_v.at[1])
        units = [(c, k) for c in range(n_ch) for k in range(TOP_K)]

        def fetch(u):
            c, k = units[u]
            return pltpu.async_copy(ys_hbm.at[idx_v.at[k, c]], rows_v.at[u % 2], gsem.at[u % 2])

        fetches = {0: fetch(0)}
        writes = {}
        for u, (c, k) in enumerate(units):
            fetches[u].wait()
            writes[u] = pltpu.async_copy(rows_v.at[u % 2], out_hbm.at[k, pl.ds(base + c * SC_CHUNK, SC_CHUNK)],
                                         wsem.at[u % 2])
            if u >= 1:
                writes[u - 1].wait()
            if u + 1 < len(units):
                fetches[u + 1] = fetch(u + 1)
        writes[len(units) - 1].wait()

    return gather(ys, idx[0], idx[1])


def _final_kernel(h_ref, y_ref, route_ref, gain_ref, out_ref):
    r = route_ref[...]
    y = h_ref[...]
    for k, gate_lane in enumerate((ROUTE_G1, ROUTE_G2)):
        y = y + r[:, gate_lane:gate_lane + 1] * _unpack_bf16_pairs(y_ref[k])
    out_ref[...] = y * lax.rsqrt(jnp.mean(y * y, axis=-1, keepdims=True) + NORM_EPS) * gain_ref[...]


def _final(h, y_rows, route, gain, tm):
    t, d = h.shape
    return pl.pallas_call(
        _final_kernel,
        grid=(t // tm,),
        in_specs=[pl.BlockSpec((tm, d), lambda i: (i, 0)),
                  pl.BlockSpec((TOP_K, tm, d // 2), lambda i: (0, i, 0)),
                  pl.BlockSpec((tm, LANES), lambda i: (i, 0)),
                  pl.BlockSpec((1, d), lambda i: (0, 0))],
        out_specs=pl.BlockSpec((tm, d), lambda i: (i, 0)),
        out_shape=jax.ShapeDtypeStruct((t, d), F32),
        compiler_params=_cparams(("arbitrary",)),
        name="final",
    )(h, y_rows, route, gain.reshape(1, d))


def _layer(h3, mix_gain, w_in, attn_gain, decay_f, decay_b, ret_gain, w_out, ffn_gain,
           w_rg, b_rg, w_re, b_re, w_eg, w_eu, w_ed, final_gain):
    b, s, d = h3.shape
    t = b * s
    tm = 512
    attn_qkv, (qr, kr, vr, gr) = _inproj(h3, mix_gain, w_in.astype(BF16), tm)

    o_list, lse_list = [], []
    for window, dilation in DILATED_BRANCHES:
        reach = (window // 2) // dilation
        o, lse = _attn_branch(*attn_qkv[dilation], dilation, reach, tq=128, tqb=512)
        o_list.append(o)
        lse_list.append(lse)

    lg_f = jnp.log1p(-jnp.exp2(decay_f.astype(F32)))
    lg_b = jnp.log1p(-jnp.exp2(decay_b.astype(F32)))
    orr = _retention(qr, kr, vr, gr, lg_f, lg_b, ret_gain).reshape(t, HEAD_GROUP_WIDTH)

    n_route = MOE_GROUPS + N_EXPERTS
    w_router = jnp.zeros((d, LANES), F32).at[:, :n_route].set(jnp.concatenate([w_re, w_rg], axis=1).astype(F32))
    b_router = jnp.zeros((1, LANES), F32).at[0, :n_route].set(jnp.concatenate([b_re, b_rg]).astype(F32))
    h, hn_packed, logits = _outproj(h3.reshape(t, d), o_list, lse_list, orr, attn_gain, w_out.astype(BF16),
                                    ffn_gain, w_router, b_router, tm)

    route, counts_rep = _route(logits, tm)

    n_blocks = -(-TOP_K * t // MOE_BLOCK) + N_EXPERTS
    counts = counts_rep[:, 0].astype(jnp.int32)
    aligned = ((counts + SEG_ALIGN - 1) // SEG_ALIGN) * SEG_ALIGN
    seg_start = jnp.cumsum(aligned) - aligned
    expert_iota = jnp.arange(N_EXPERTS, dtype=jnp.int32)
    e12 = route[:, ROUTE_E1:ROUTE_E2 + 1].astype(jnp.int32)
    r12 = route[:, ROUTE_R1:ROUTE_R2 + 1].astype(jnp.int32)
    pos = r12 + jnp.sum(jnp.where(e12[..., None] == expert_iota, seg_start, 0), axis=-1)
    nblk = (counts + MOE_BLOCK - 1) // MOE_BLOCK
    blk_end = jnp.cumsum(nblk)
    blk_start = blk_end - nblk
    blk = jnp.arange(n_blocks, dtype=jnp.int32)[:, None]
    owner = (blk >= blk_start) & (blk < blk_end)
    local = (blk - blk_start) * MOE_BLOCK
    block_row0 = jnp.sum(jnp.where(owner, seg_start + local, 0), axis=-1).astype(jnp.int32)
    block_nreal = jnp.sum(jnp.where(owner, jnp.clip(counts - local, 0, MOE_BLOCK), 0), axis=-1).astype(jnp.int32)
    block_expert = jnp.minimum(jnp.sum((blk >= blk_end).astype(jnp.int32), axis=-1), N_EXPERTS - 1)

    n_sorted = TOP_K * t + SORTED_TAIL
    dump_row = n_sorted
    seg_end = seg_start + counts
    hole = seg_end[:, None] + jnp.arange(SEG_ALIGN, dtype=jnp.int32)[None, :]
    hole = jnp.where(hole < (seg_start + aligned)[:, None], hole, dump_row).reshape(-1)
    tail = (seg_start[-1] + aligned[-1]) + jnp.arange(SORTED_TAIL, dtype=jnp.int32)
    tail = jnp.where(tail < n_sorted, tail, dump_row)
    zero_pos = jnp.concatenate([hole, tail]).astype(jnp.int32)

    xs = _dispatch(hn_packed, pos, zero_pos)
    ys = _experts(xs, block_expert, block_nreal, block_row0, w_eg.astype(F32), w_eu.astype(F32),
                  w_ed.astype(F32))
    out = _final(h, _combine_gather(ys, pos), route, final_gain, tm)
    return out.reshape(b, s, d)


def kernel(x, mix_norm_gain, w_in, attn_out_gain, ret_decay_fwd, ret_decay_bwd, ret_out_gain, w_out,
           ffn_norm_gain, w_route_group, b_route_group, w_route_expert, b_route_expert,
           w_expert_gate, w_expert_up, w_expert_down, final_norm_gain):
    depth = mix_norm_gain.shape[0]
    assert depth == 1, "the final rmsnorm is fused into the single layer's combine kernel"
    l = 0
    return _layer(x, mix_norm_gain[l], w_in[l], attn_out_gain[l], ret_decay_fwd[l], ret_decay_bwd[l],
                  ret_out_gain[l], w_out[l], ffn_norm_gain[l], w_route_group[l], b_route_group[l],
                  w_route_expert[l], b_route_expert[l], w_expert_gate[l], w_expert_up[l], w_expert_down[l],
                  final_norm_gain)
```

```python
import functools

import numpy as np
import jax
import jax.numpy as jnp
from jax import lax
from jax.experimental import pallas as pl
from jax.experimental.pallas import tpu as pltpu
from jax.experimental.pallas import tpu_sc as plsc

F32 = jnp.float32
BF16 = jnp.bfloat16

ATTN_HEADS = 8
HEAD_DIM = 64
RET_HEADS = 8
HEAD_GROUP_WIDTH = 512
N_PROJ_GROUPS = 7
DILATED_BRANCHES = ((128, 1), (512, 4), (2048, 16))
ROPE_THETA = 500000.0
ROPE_DIM = HEAD_DIM // 4
RET_THETA = 10000.0
RET_CHUNK = 128
MOE_GROUPS = 4
EXPERTS_PER_GROUP = 8
N_EXPERTS = MOE_GROUPS * EXPERTS_PER_GROUP
MOE_BLOCK = 512
NORM_EPS = 1e-6
NEG_INF = -1e30

LANES = 128
VMEM_LIMIT = 56 * 1024 * 1024


def _cparams(sem):
    return pltpu.CompilerParams(dimension_semantics=sem, vmem_limit_bytes=VMEM_LIMIT)


def _rotary_tables(seq, half, freqs):
    pos = np.arange(seq, dtype=np.float64)[:, None]
    ang = pos * freqs[None, :].astype(np.float64)
    cos, sin = np.cos(ang), np.sin(ang)
    c = np.ones((seq, HEAD_DIM)); sp = np.zeros((seq, HEAD_DIM)); sm = np.zeros((seq, HEAD_DIM))
    c[:, :half] = cos; c[:, half:2 * half] = cos
    sp[:, half:2 * half] = sin
    sm[:, :half] = -sin
    rep = LANES // HEAD_DIM
    return tuple(jnp.asarray(np.tile(t, (1, rep)), F32) for t in (c, sp, sm))


def _rotate(t, c, sp, sm, half):
    outs = []
    for g in range(t.shape[1] // LANES):
        tg = t[:, g * LANES:(g + 1) * LANES]
        outs.append(tg * c + pltpu.roll(tg, half, 1) * sp + pltpu.roll(tg, LANES - half, 1) * sm)
    return jnp.concatenate(outs, axis=1)


CLASS_DILATIONS = tuple(d for _, d in DILATED_BRANCHES if d > 1)
ATTN_Q_SCALE = float(np.log2(np.e)) * HEAD_DIM ** -0.5


def _inproj_kernel(x_ref, gain_ref, w_ref, ca_ref, spa_ref, sma_ref, cr_ref, spr_ref, smr_ref, *rest):
    n_cls = len(CLASS_DILATIONS)
    nat_refs = rest[0:3]
    cls_refs = [rest[3 + 3 * c:6 + 3 * c] for c in range(n_cls)]
    qr_ref, kr_ref, vr_ref, gr_ref = rest[3 + 3 * n_cls:7 + 3 * n_cls]
    stage_ref = rest[7 + 3 * n_cls]
    x = x_ref[0]
    tm = x.shape[0]
    ms = jnp.mean(x * x, axis=-1, keepdims=True)
    xn = (x * lax.rsqrt(ms + NORM_EPS) * gain_ref[...]).astype(BF16)
    gw = HEAD_GROUP_WIDTH

    def proj(c):
        return jnp.dot(xn, w_ref[:, c * gw:(c + 1) * gw], preferred_element_type=F32)

    a_tabs = (ca_ref[...], spa_ref[...], sma_ref[...])
    r_tabs = (cr_ref[...], spr_ref[...], smr_ref[...])
    attn_vals = ((_rotate(proj(0), *a_tabs, ROPE_DIM // 2) * ATTN_Q_SCALE),
                 _rotate(proj(1), *a_tabs, ROPE_DIM // 2),
                 proj(2))
    for j, val in enumerate(attn_vals):
        nat_refs[j][0] = val.astype(BF16)
        for g in range(gw // LANES):
            stage_ref[g] = val[:, g * LANES:(g + 1) * LANES]
        for c, d in enumerate(CLASS_DILATIONS):
            for r in range(d):
                for g in range(gw // LANES):
                    col = r * gw + g * LANES
                    cls_refs[c][j][0, :, col:col + LANES] = (
                        stage_ref[g, pl.ds(r, tm // d, stride=d), :].astype(BF16))
    qr_ref[0] = _rotate(proj(3), *r_tabs, HEAD_DIM // 2).astype(BF16)
    kr_ref[0] = (_rotate(proj(4), *r_tabs, HEAD_DIM // 2) * (HEAD_DIM ** -0.5)).astype(BF16)
    vr_ref[0] = proj(5).astype(BF16)
    g = proj(6)
    gr_ref[0] = (g * jax.nn.sigmoid(g)).astype(BF16)


def _inproj(x, gain, w_in_bf16, tm):
    b, s, d = x.shape
    rope_freqs = ROPE_THETA ** (-np.arange(0, ROPE_DIM, 2, dtype=np.float32) / ROPE_DIM)
    ret_freqs = RET_THETA ** (-np.linspace(0.0, 1.0, HEAD_DIM // 2, dtype=np.float32))
    tabs = _rotary_tables(s, ROPE_DIM // 2, rope_freqs) + _rotary_tables(s, HEAD_DIM // 2, ret_freqs)
    gw = HEAD_GROUP_WIDTH
    tab_spec = pl.BlockSpec((tm, LANES), lambda si, bi: (si, 0))

    def view(dil):
        return (pl.BlockSpec((1, tm // dil, dil * gw), lambda si, bi: (bi, si, 0)),
                jax.ShapeDtypeStruct((b, s // dil, dil * gw), BF16))

    views = [view(1)] * 3 + [view(dil) for dil in CLASS_DILATIONS for _ in range(3)] + [view(1)] * 4
    outs = pl.pallas_call(
        _inproj_kernel,
        grid=(s // tm, b),
        in_specs=[pl.BlockSpec((1, tm, d), lambda si, bi: (bi, si, 0)),
                  pl.BlockSpec((1, d), lambda si, bi: (0, 0)),
                  pl.BlockSpec(w_in_bf16.shape, lambda si, bi: (0, 0))] + [tab_spec] * 6,
        out_specs=[v[0] for v in views],
        out_shape=[v[1] for v in views],
        scratch_shapes=[pltpu.VMEM((gw // LANES, tm, LANES), F32)],
        compiler_params=_cparams(("arbitrary", "arbitrary")),
        name="inproj",
    )(x, gain.reshape(1, d), w_in_bf16, *tabs)
    n_attn = 3 * (1 + len(CLASS_DILATIONS))
    attn_qkv = {dil: outs[3 * c:3 * c + 3] for c, dil in enumerate((1,) + CLASS_DILATIONS)}
    return attn_qkv, outs[n_attn:]


ATTN_WINDOW_CASES = 3


def _attn_kernel(q_ref, k_ref, v_ref, o_ref, lse_ref, bias_ref, *, length, tq, reach):
    tqb = q_ref.shape[1]
    win = tq + 2 * reach
    heads_per_pair = LANES // HEAD_DIM
    qi = pl.program_id(2)
    lane = lax.broadcasted_iota(jnp.int32, (1, LANES), 1)
    lane_t = lax.broadcasted_iota(jnp.int32, (tq, LANES), 1)
    ones = jnp.ones((win, LANES), BF16)

    @pl.when((pl.program_id(0) == 0) & (pl.program_id(1) == 0) & (qi == 0))
    def _():
        diff = (lax.broadcasted_iota(jnp.int32, (heads_per_pair * tq, win), 1)
                - (lax.broadcasted_iota(jnp.int32, (heads_per_pair * tq, win), 0) & (tq - 1)))
        for case in range(ATTN_WINDOW_CASES):
            off = case * reach
            bias_ref[case] = jnp.where((diff >= off - reach) & (diff <= off + reach), 0.0, NEG_INF)

    def sub(t, cls):
        q0 = qi * tqb + t * tq
        ws = jnp.clip(q0 - reach, 0, length - win)
        ws = pl.multiple_of(ws, reach)
        bias = bias_ref[(q0 - ws) // reach]
        rows = pl.ds(t * tq, tq)
        m_tile = jnp.zeros((tq, LANES), F32)
        l_tile = jnp.ones((tq, LANES), F32)
        for g in range(HEAD_GROUP_WIDTH // LANES):
            cols = slice(cls * HEAD_GROUP_WIDTH + g * LANES, cls * HEAD_GROUP_WIDTH + (g + 1) * LANES)
            qg = q_ref[0, rows, cols]
            kw = k_ref[0, pl.ds(ws, win), cols]
            v_ones = jnp.concatenate([v_ref[0, pl.ds(ws, win), cols], ones], axis=1)
            hms = [(lane >= hh * HEAD_DIM) & (lane < (hh + 1) * HEAD_DIM) for hh in range(heads_per_pair)]
            q2 = jnp.concatenate([qg * hm.astype(BF16) for hm in hms], axis=0)
            sc = lax.dot_general(q2, kw, (((1,), (1,)), ((), ())), preferred_element_type=F32) + bias
            m = jnp.max(sc, axis=-1, keepdims=True)
            p = jnp.exp2(sc - m).astype(BF16)
            pv = jnp.dot(p, v_ones, preferred_element_type=F32)
            l = pv[:, LANES:]
            o = pv[:, :LANES] / l
            o_pair = jnp.zeros((tq, LANES), F32)
            for hh in range(heads_per_pair):
                part = slice(hh * tq, (hh + 1) * tq)
                head_lane = lane_t == g * heads_per_pair + hh
                o_pair = jnp.where(hms[hh], o[part], o_pair)
                m_tile = jnp.where(head_lane, m[part], m_tile)
                l_tile = jnp.where(head_lane, l[part], l_tile)
            o_ref[0, rows, cols] = o_pair.astype(BF16)
        lse_ref[0, rows, cls * LANES:(cls + 1) * LANES] = m_tile + jnp.log2(l_tile)

    for cls in range(q_ref.shape[2] // HEAD_GROUP_WIDTH):
        for t in range(tqb // tq):
            sub(t, cls)


def _attn_branch(qc, kc, vc, dilation, reach, tq, tqb, sub_tiles=4):
    b, length, dw = qc.shape
    w = dw // dilation
    tqb = min(tqb, length)
    assert tq % reach == 0 and tq > reach and length % tq == 0 and length >= tq + 2 * reach
    ncls = min(dilation, max(1, sub_tiles // (tqb // tq)))
    q_spec = pl.BlockSpec((1, tqb, ncls * w), lambda bi, r, qi: (bi, qi, r))
    kv_spec = pl.BlockSpec((1, length, ncls * w), lambda bi, r, qi: (bi, 0, r))
    o, lse = pl.pallas_call(
        functools.partial(_attn_kernel, length=length, tq=tq, reach=reach),
        grid=(b, dilation // ncls, length // tqb),
        in_specs=[q_spec, kv_spec, kv_spec],
        out_specs=[q_spec, pl.BlockSpec((1, tqb, ncls * LANES), lambda bi, r, qi: (bi, qi, r))],
        out_shape=[jax.ShapeDtypeStruct((b, length, dilation * w), BF16),
                   jax.ShapeDtypeStruct((b, length, dilation * LANES), F32)],
        scratch_shapes=[pltpu.VMEM((ATTN_WINDOW_CASES, (LANES // HEAD_DIM) * tq, tq + 2 * reach), F32)],
        compiler_params=_cparams(("arbitrary", "arbitrary", "arbitrary")),
        name=f"attn_d{dilation}",
    )(qc, kc, vc)
    return o, lse


RET_TAB_QF, RET_TAB_QB, RET_TAB_KF, RET_TAB_KB = range(4)


def _retention_kernel(lgf_ref, lgb_ref, q_ref, k_ref, v_ref, g_ref, gain_ref, o_ref,
                      tab_ref, dec_ref, sb_ref, st_ref, *, chunk, unroll):
    c = chunk
    n = q_ref.shape[1] // c
    width = q_ref.shape[2]
    n_pairs = width // LANES
    heads_per_pair = LANES // HEAD_DIM
    n_heads = n_pairs * heads_per_pair
    head0 = pl.program_id(1) * n_heads
    lane_w = lax.broadcasted_iota(jnp.int32, (1, width), 1)
    lgf = [lgf_ref[head0 + hd] for hd in range(n_heads)]
    lgb = [lgb_ref[head0 + hd] for hd in range(n_heads)]
    lgf_lane = jnp.zeros((1, width), F32)
    lgb_lane = jnp.zeros((1, width), F32)
    for hd in range(n_heads):
        in_head = (lane_w >= hd * HEAD_DIM) & (lane_w < (hd + 1) * HEAD_DIM)
        lgf_lane = jnp.where(in_head, lgf[hd], lgf_lane)
        lgb_lane = jnp.where(in_head, lgb[hd], lgb_lane)
    idx = lax.broadcasted_iota(jnp.int32, (c, width), 0).astype(F32)
    tab_ref[RET_TAB_QF] = jnp.exp((idx + 1.0) * lgf_lane)
    tab_ref[RET_TAB_QB] = jnp.exp((c - idx) * lgb_lane)
    tab_ref[RET_TAB_KF] = jnp.exp((c - 1.0 - idx) * lgf_lane)
    tab_ref[RET_TAB_KB] = jnp.exp(idx * lgb_lane)
    sdf = jnp.exp(c * lgf_lane)
    sdb = jnp.exp(c * lgb_lane)
    dmat = (lax.broadcasted_iota(jnp.int32, (c, c), 0)
            - lax.broadcasted_iota(jnp.int32, (c, c), 1)).astype(F32)
    for hd in range(n_heads):
        dec_ref[hd // heads_per_pair, :, (hd % heads_per_pair) * c:(hd % heads_per_pair + 1) * c] = (
            jnp.where(dmat >= 0, jnp.exp(dmat * lgf[hd]), jnp.exp(-dmat * lgb[hd])))
    lane = lax.broadcasted_iota(jnp.int32, (1, LANES), 1)
    lane_hi = lane >= HEAD_DIM
    head_masks = [((lane >= hh * HEAD_DIM) & (lane < (hh + 1) * HEAD_DIM)).astype(BF16)
                  for hh in range(heads_per_pair)]
    row_hi = lax.broadcasted_iota(jnp.int32, (LANES, LANES), 0) >= HEAD_DIM
    col_hi = lax.broadcasted_iota(jnp.int32, (LANES, LANES), 1) >= HEAD_DIM
    blockdiag = row_hi == col_hi

    def kv_state(kd, vv):
        kt = jnp.transpose(kd).astype(BF16)
        return jnp.where(blockdiag, jnp.dot(kt, vv, preferred_element_type=F32), 0.0)

    st_ref[...] = jnp.zeros_like(st_ref)
    sb_ref[n - 1] = jnp.zeros(sb_ref.shape[1:], sb_ref.dtype)

    def back(i, carry):
        nn = n - 1 - i
        rows = pl.ds(pl.multiple_of(nn * c, c), c)
        for p in range(n_pairs):
            cols = slice(p * LANES, (p + 1) * LANES)
            kd = k_ref[0, rows, cols].astype(F32) * tab_ref[RET_TAB_KB, :, cols]
            new = st_ref[p] * sdb[:, cols] + kv_state(kd, v_ref[0, rows, cols])
            st_ref[p] = new
            sb_ref[nn - 1, p] = new.astype(BF16)
        return carry

    lax.fori_loop(0, n - 1, back, 0, unroll=unroll)

    st_ref[...] = jnp.zeros_like(st_ref)

    def fwd(nn, carry):
        rows = pl.ds(pl.multiple_of(nn * c, c), c)
        for p in range(n_pairs):
            cols = slice(p * LANES, (p + 1) * LANES)
            qq = q_ref[0, rows, cols]
            kk = k_ref[0, rows, cols]
            vv = v_ref[0, rows, cols]
            qf = qq.astype(F32)
            sf = st_ref[p]
            qcat = jnp.concatenate([(qf * tab_ref[RET_TAB_QF, :, cols]).astype(BF16),
                                    (qf * tab_ref[RET_TAB_QB, :, cols]).astype(BF16)], axis=1)
            scat = jnp.concatenate([sf.astype(BF16), sb_ref[nn, p]], axis=0)
            o = jnp.dot(qcat, scat, preferred_element_type=F32)
            k2 = jnp.concatenate([kk * hm for hm in head_masks], axis=0)
            v2 = jnp.concatenate([vv * hm for hm in head_masks], axis=0)
            sc = lax.dot_general(qq, k2, (((1,), (1,)), ((), ())), preferred_element_type=F32)
            o = o + jnp.dot((sc * dec_ref[p]).astype(BF16), v2, preferred_element_type=F32)
            s_lo = jnp.sum(jnp.where(lane_hi, 0.0, o), axis=-1, keepdims=True)
            s_hi = jnp.sum(jnp.where(lane_hi, o, 0.0), axis=-1, keepdims=True)
            mu = jnp.where(lane_hi, s_hi, s_lo) * (1.0 / HEAD_DIM)
            dev = o - mu
            d2 = dev * dev
            v_lo = jnp.sum(jnp.where(lane_hi, 0.0, d2), axis=-1, keepdims=True)
            v_hi = jnp.sum(jnp.where(lane_hi, d2, 0.0), axis=-1, keepdims=True)
            var = jnp.where(lane_hi, v_hi, v_lo) * (1.0 / HEAD_DIM)
            out = dev * lax.rsqrt(var + NORM_EPS) * gain_ref[:, cols] * g_ref[0, rows, cols].astype(F32)
            o_ref[0, rows, cols] = out.astype(BF16)
            kd = kk.astype(F32) * tab_ref[RET_TAB_KF, :, cols]
            st_ref[p] = sf * sdf[:, cols] + kv_state(kd, vv)
        return carry

    lax.fori_loop(0, n, fwd, 0, unroll=unroll)


def _retention(qr, kr, vr, gate, lg_f, lg_b, out_gain, width=512, unroll=4):
    b, s, w = qr.shape
    n_pairs = width // LANES
    n_heads = width // HEAD_DIM
    spec = pl.BlockSpec((1, s, width), lambda bi, p, *_: (bi, 0, p))
    grid_spec = pltpu.PrefetchScalarGridSpec(
        num_scalar_prefetch=2,
        grid=(b, w // width),
        in_specs=[spec, spec, spec, spec, pl.BlockSpec((1, width), lambda bi, p, *_: (0, p))],
        out_specs=spec,
        scratch_shapes=[pltpu.VMEM((4, RET_CHUNK, width), F32),
                        pltpu.VMEM((n_pairs, RET_CHUNK, (LANES // HEAD_DIM) * RET_CHUNK), F32),
                        pltpu.VMEM((s // RET_CHUNK, n_pairs, LANES, LANES), BF16),
                        pltpu.VMEM((n_pairs, LANES, LANES), F32)],
    )
    return pl.pallas_call(
        functools.partial(_retention_kernel, chunk=RET_CHUNK, unroll=unroll),
        grid_spec=grid_spec,
        out_shape=jax.ShapeDtypeStruct((b, s, w), BF16),
        compiler_params=_cparams(("arbitrary", "arbitrary")),
        name="retention",
    )(lg_f, lg_b, qr, kr, vr, gate, out_gain.reshape(1, w))


def _split_bf16(t):
    hi = t.astype(BF16)
    lo = (t - hi.astype(F32)).astype(BF16)
    return hi, lo


def _pack_bf16_pairs(t):
    n = t.shape[1] // 2
    hi = pltpu.bitcast(t[:, :n].astype(BF16).astype(F32), jnp.uint32)
    lo = pltpu.bitcast(t[:, n:].astype(BF16).astype(F32), jnp.uint32)
    return hi | (lo >> 16)


def _unpack_bf16_pairs(u):
    hi = pltpu.bitcast(u & jnp.uint32(0xFFFF0000), F32)
    lo = pltpu.bitcast(u << 16, F32)
    return jnp.concatenate([hi, lo], axis=1)


def _outproj_kernel(x_ref, o1_ref, o2_ref, o3_ref, l1_ref, l2_ref, l3_ref, orr_ref, ga_ref, expand_ref,
                    wout_ref, gf_ref, wr_ref, br_ref, h_ref, hn_ref, logit_ref,
                    *nat_refs):
    tm = x_ref.shape[0]
    gw = HEAD_GROUP_WIDTH
    os, ls = [], []
    for (_, dil), o_ref, l_ref in zip(DILATED_BRANCHES, (o1_ref, o2_ref, o3_ref), (l1_ref, l2_ref, l3_ref)):
        if dil == 1:
            os.append(o_ref[...].astype(F32))
            ls.append(l_ref[...])
            continue
        c = CLASS_DILATIONS.index(dil)
        onat_ref, lnat_ref = nat_refs[2 * c], nat_refs[2 * c + 1]
        for r in range(dil):
            rows = pl.ds(r, tm // dil, stride=dil)
            for g in range(gw // LANES):
                col = r * gw + g * LANES
                onat_ref[g, rows, :] = o_ref[:, col:col + LANES].astype(F32)
            lnat_ref[rows, :] = l_ref[:, r * LANES:(r + 1) * LANES]
        os.append(jnp.concatenate([onat_ref[g] for g in range(gw // LANES)], axis=1))
        ls.append(lnat_ref[...])
    mx = jnp.maximum(jnp.maximum(ls[0], ls[1]), ls[2])
    es = [jnp.exp2(l - mx) for l in ls]
    inv = 1.0 / (es[0] + es[1] + es[2])
    expand = expand_ref[...]
    oa = jnp.zeros((tm, gw), F32)
    for e, o in zip(es, os):
        wexp = jnp.dot(jnp.concatenate(_split_bf16(e * inv), axis=1), expand, preferred_element_type=F32)
        oa = oa + wexp * o
    oa = oa * lax.rsqrt(jnp.mean(oa * oa, axis=-1, keepdims=True) + NORM_EPS) * ga_ref[...]
    mixed = jnp.concatenate([oa.astype(BF16), orr_ref[...]], axis=1)
    h = x_ref[...] + jnp.dot(mixed, wout_ref[...], preferred_element_type=F32)
    h_ref[...] = h
    hn = h * lax.rsqrt(jnp.mean(h * h, axis=-1, keepdims=True) + NORM_EPS) * gf_ref[...]
    hn_ref[...] = _pack_bf16_pairs(hn)
    prod = jnp.dot(jnp.concatenate(_split_bf16(hn), axis=0), wr_ref[...], preferred_element_type=F32)
    logit_ref[...] = prod[:tm, :LANES] + prod[:tm, LANES:] + prod[tm:, :LANES] + br_ref[...]


def _outproj(x2, o_list, lse_list, orr, attn_gain, w_out_bf16, ffn_gain, w_router, b_router, tm):
    t, d = x2.shape
    w = HEAD_GROUP_WIDTH
    expand = np.zeros((LANES, w), np.float32)
    for hd in range(ATTN_HEADS):
        expand[hd, hd * HEAD_DIM:(hd + 1) * HEAD_DIM] = 1.0
    expand = jnp.asarray(np.concatenate([expand, expand], axis=0), BF16)
    wr_hi = w_router.astype(BF16)
    wr = jnp.concatenate([wr_hi, (w_router - wr_hi.astype(F32)).astype(BF16)], axis=1)
    row = lambda width, dil=1: pl.BlockSpec((tm // dil, dil * width), lambda i: (i, 0))
    full = lambda a: pl.BlockSpec(a.shape, lambda i: (0,) * a.ndim)
    ga = attn_gain.reshape(1, w)
    gf = ffn_gain.reshape(1, d)
    dils = [dil for _, dil in DILATED_BRANCHES]
    o_flat = [o.reshape(t // dil, dil * w) for o, dil in zip(o_list, dils)]
    l_flat = [l.reshape(t // dil, dil * LANES) for l, dil in zip(lse_list, dils)]
    nat_scratch = []
    for _ in CLASS_DILATIONS:
        nat_scratch += [pltpu.VMEM((w // LANES, tm, LANES), F32), pltpu.VMEM((tm, LANES), F32)]
    return pl.pallas_call(
        _outproj_kernel,
        grid=(t // tm,),
        in_specs=[row(d)] + [row(w, dil) for dil in dils] + [row(LANES, dil) for dil in dils] + [row(w)]
                 + [full(ga), full(expand), full(w_out_bf16), full(gf), full(wr), full(b_router)],
        out_specs=[row(d), row(d // 2), row(LANES)],
        out_shape=[jax.ShapeDtypeStruct((t, d), F32),
                   jax.ShapeDtypeStruct((t, d // 2), jnp.uint32),
                   jax.ShapeDtypeStruct((t, LANES), F32)],
        scratch_shapes=nat_scratch,
        compiler_params=_cparams(("arbitrary",)),
        name="outproj",
    )(x2, *o_flat, *l_flat, orr, ga, expand, w_out_bf16, gf, wr, b_router)


ROUTE_E1, ROUTE_E2, ROUTE_G1, ROUTE_G2, ROUTE_R1, ROUTE_R2 = range(6)
GROUP_LANE0 = N_EXPERTS
SUBLANES = 8


def _route_kernel(logit_ref, tri_ref, route_ref, count_ref, run_ref):
    @pl.when(pl.program_id(0) == 0)
    def _():
        run_ref[...] = jnp.zeros_like(run_ref)

    lg_t = jnp.transpose(logit_ref[...])
    tm = lg_t.shape[1]
    assert EXPERTS_PER_GROUP == SUBLANES and MOE_GROUPS <= SUBLANES
    rid = lax.broadcasted_iota(jnp.int32, (SUBLANES, tm), 0)
    big = jnp.int32(1 << 20)

    def top(vals):
        m = jnp.max(vals, axis=0, keepdims=True)
        i = jnp.min(jnp.where(vals == m, rid, big), axis=0, keepdims=True)
        return m, i

    gl = jnp.where(rid < MOE_GROUPS, lg_t[GROUP_LANE0:GROUP_LANE0 + SUBLANES], -jnp.inf)
    gmax, gidx = top(gl)
    group_gate = 1.0 / jnp.sum(jnp.exp(gl - gmax), axis=0, keepdims=True)
    el = lg_t[0:EXPERTS_PER_GROUP]
    for g in range(1, MOE_GROUPS):
        el = jnp.where(gidx == g, lg_t[g * EXPERTS_PER_GROUP:(g + 1) * EXPERTS_PER_GROUP], el)
    t1, i1 = top(el)
    t2, i2 = top(jnp.where(rid == i1, -jnp.inf, el))
    e21 = jnp.exp(t2 - t1)
    g1 = group_gate / (1.0 + e21)
    g2 = group_gate * e21 / (1.0 + e21)
    e1 = gidx * EXPERTS_PER_GROUP + i1
    e2 = gidx * EXPERTS_PER_GROUP + i2
    erow = lax.broadcasted_iota(jnp.int32, (N_EXPERTS, tm), 0)
    oh1 = erow == e1
    oh2 = erow == e2
    cnt = oh1.astype(F32) + oh2.astype(F32)
    run = run_ref[:, 0:1]
    prefix = jnp.dot(cnt.astype(BF16), tri_ref[...], preferred_element_type=F32) + run
    r1 = jnp.sum(jnp.where(oh1, prefix, 0.0), axis=0, keepdims=True)
    r2 = jnp.sum(jnp.where(oh2, prefix, 0.0), axis=0, keepdims=True)
    new_run = jnp.broadcast_to(run + jnp.sum(cnt, axis=1, keepdims=True), run_ref.shape)
    run_ref[...] = new_run
    count_ref[...] = new_run
    out = jnp.zeros((SUBLANES, tm), F32)
    for row, val in ((ROUTE_E1, e1.astype(F32)), (ROUTE_E2, e2.astype(F32)), (ROUTE_G1, g1),
                     (ROUTE_G2, g2), (ROUTE_R1, r1), (ROUTE_R2, r2)):
        out = jnp.where(rid == row, val, out)
    out_t = jnp.concatenate([out, jnp.zeros((LANES - SUBLANES, tm), F32)], axis=0)
    route_ref[...] = jnp.transpose(out_t)


def _route(logits, tm):
    t = logits.shape[0]
    tri = jnp.asarray(np.triu(np.ones((tm, tm), np.float32), 1), BF16)
    return pl.pallas_call(
        _route_kernel,
        grid=(t // tm,),
        in_specs=[pl.BlockSpec((tm, LANES), lambda i: (i, 0)), pl.BlockSpec((tm, tm), lambda i: (0, 0))],
        out_specs=[pl.BlockSpec((tm, LANES), lambda i: (i, 0)),
                   pl.BlockSpec((N_EXPERTS, LANES), lambda i: (0, 0))],
        out_shape=[jax.ShapeDtypeStruct((t, LANES), F32), jax.ShapeDtypeStruct((N_EXPERTS, LANES), F32)],
        scratch_shapes=[pltpu.VMEM((N_EXPERTS, LANES), F32)],
        compiler_params=_cparams(("arbitrary",)),
        name="route",
    )(logits, tri)


TOP_K = 2


SEG_ALIGN = 8
SORTED_TAIL = N_EXPERTS * SEG_ALIGN + MOE_BLOCK


def _tail_pieces():
    full, rest = divmod(SORTED_TAIL, MOE_BLOCK)
    return [MOE_BLOCK] * full + ([rest] if rest else [])


SC_CHUNK = 64
SC_BUFFERS = 3
ZERO_ROWS = N_EXPERTS * SEG_ALIGN + SORTED_TAIL
SORTED_ROWS_EXTRA = SORTED_TAIL + SEG_ALIGN


def _sc_workers():
    info = plsc.get_sparse_core_info()
    return info.num_cores, info.num_cores * info.num_subcores


def _dispatch(hn_packed, pos, zero_pos):
    t, dp = hn_packed.shape
    ncores, nw = _sc_workers()
    per_w = t // nw
    n_ch = per_w // SC_CHUNK
    z_rows = ZERO_ROWS // nw
    assert per_w % SC_CHUNK == 0 and ZERO_ROWS % nw == 0 and z_rows % SEG_ALIGN == 0 and TOP_K == 2
    idx = [pos[:, k].reshape(nw, n_ch, SC_CHUNK) for k in range(TOP_K)]
    zeros = jnp.zeros((z_rows, dp), hn_packed.dtype)
    mesh = plsc.VectorSubcoreMesh(core_axis_name="c", subcore_axis_name="s")

    @functools.partial(
        pl.kernel, mesh=mesh,
        out_type=jax.ShapeDtypeStruct((TOP_K * t + SORTED_ROWS_EXTRA, dp), hn_packed.dtype),
        scratch_types=[pltpu.VMEM((n_ch, SC_CHUNK), jnp.int32), pltpu.VMEM((n_ch, SC_CHUNK), jnp.int32),
                       pltpu.VMEM((z_rows,), jnp.int32),
                       pltpu.VMEM((SC_BUFFERS, SC_CHUNK, dp), hn_packed.dtype),
                       pltpu.VMEM((z_rows, dp), hn_packed.dtype),
                       pltpu.SemaphoreType.DMA((SC_BUFFERS,)), pltpu.SemaphoreType.DMA((SC_BUFFERS,)),
                       pltpu.SemaphoreType.DMA],
    )
    def scatter(hn_hbm, p0_hbm, p1_hbm, zpos_hbm, zeros_hbm, xs_hbm, i0_v, i1_v, iz_v, rows_v, zero_v,
                lsem, ssem, zsem):
        wid = lax.axis_index("s") * ncores + lax.axis_index("c")
        base = wid * per_w
        zero_load = pltpu.async_copy(zeros_hbm, zero_v, zsem)
        pltpu.sync_copy(zpos_hbm.at[wid], iz_v)
        pltpu.sync_copy(p0_hbm.at[wid], i0_v)
        pltpu.sync_copy(p1_hbm.at[wid], i1_v)

        def load(c):
            return pltpu.async_copy(hn_hbm.at[pl.ds(base + c * SC_CHUNK, SC_CHUNK)], rows_v.at[c % SC_BUFFERS],
                                    lsem.at[c % SC_BUFFERS])

        loads = {c: load(c) for c in range(min(SC_BUFFERS - 1, n_ch))}
        scat = {}
        for c in range(n_ch):
            slot = c % SC_BUFFERS
            loads[c].wait()
            scat[c] = (pltpu.async_copy(rows_v.at[slot], xs_hbm.at[i0_v.at[c]], ssem.at[slot]),
                       pltpu.async_copy(rows_v.at[slot], xs_hbm.at[i1_v.at[c]], ssem.at[slot]))
            if c >= 1:
                for d in scat[c - 1]:
                    d.wait()
            if c + SC_BUFFERS - 1 < n_ch:
                loads[c + SC_BUFFERS - 1] = load(c + SC_BUFFERS - 1)
        zero_load.wait()
        zero_scatter = pltpu.async_copy(zero_v, xs_hbm.at[iz_v], zsem)
        for d in scat[n_ch - 1]:
            d.wait()
        zero_scatter.wait()

    return scatter(hn_packed, idx[0], idx[1], zero_pos.reshape(nw, z_rows), zeros)


def _expert_kernel(bexp_ref, nreal_ref, row0_ref, xs_hbm, wg_ref, wu_ref, wd_ref, ys_hbm,
                   xbuf, ybuf, wg_bf, wu_bf, wd_bf, isem, osem, *, n_rows):
    i = pl.program_id(0)
    nb = pl.num_programs(0)
    slot = i % 2
    nslot = 1 - slot
    n_cur = nreal_ref[i]
    prev = jnp.maximum(i - 1, 0)
    nxt = jnp.minimum(i + 1, nb - 1)

    def in_copy(blk, s):
        row0 = pl.multiple_of(row0_ref[blk], SEG_ALIGN)
        return pltpu.make_async_copy(xs_hbm.at[pl.ds(row0, MOE_BLOCK)], xbuf.at[s], isem.at[s])

    def out_copy(blk, s):
        row0 = pl.multiple_of(row0_ref[blk], SEG_ALIGN)
        return pltpu.make_async_copy(ybuf.at[s], ys_hbm.at[pl.ds(row0, MOE_BLOCK)], osem.at[s])

    @pl.when(i == 0)
    def _():
        ybuf[...] = jnp.zeros_like(ybuf)
        tails = [pltpu.make_async_copy(ybuf.at[s, pl.ds(0, size)], ys_hbm.at[pl.ds(n_rows + s * MOE_BLOCK, size)],
                                       osem.at[s])
                 for s, size in enumerate(_tail_pieces())]
        for tail in tails:
            tail.start()
        for tail in tails:
            tail.wait()

        @pl.when(n_cur > 0)
        def _():
            in_copy(i, slot).start()

    @pl.when((i + 1 < nb) & (nreal_ref[nxt] > 0))
    def _():
        in_copy(nxt, nslot).start()

    @pl.when((n_cur > 0) & ((i == 0) | (bexp_ref[i] != bexp_ref[prev])))
    def _():
        wg_bf[...] = wg_ref[0].astype(BF16)
        wu_bf[...] = wu_ref[0].astype(BF16)
        wd_bf[...] = wd_ref[0].astype(BF16)

    @pl.when(n_cur > 0)
    def _():
        in_copy(i, slot).wait()
        xb = _unpack_bf16_pairs(xbuf[slot]).astype(BF16)
        gate = jnp.dot(xb, wg_bf[...], preferred_element_type=F32)
        up = jnp.dot(xb, wu_bf[...], preferred_element_type=F32)
        hid = (gate * jax.nn.sigmoid(gate) * up).astype(BF16)
        ybuf[slot] = _pack_bf16_pairs(jnp.dot(hid, wd_bf[...], preferred_element_type=F32))

    @pl.when((i >= 1) & (nreal_ref[prev] > 0))
    def _():
        out_copy(prev, nslot).wait()

    @pl.when(n_cur > 0)
    def _():
        out_copy(i, slot).start()

        @pl.when(i == nb - 1)
        def _():
            out_copy(i, slot).wait()


def _experts(xs, block_expert, block_nreal, block_row0, wg, wu, wd):
    n_blocks = block_expert.shape[0]
    assert n_blocks >= 2 and len(_tail_pieces()) <= 2
    n_rows_pad = xs.shape[0] - SORTED_ROWS_EXTRA + SORTED_TAIL
    dp = xs.shape[1]
    _, d, ff = wg.shape
    grid_spec = pltpu.PrefetchScalarGridSpec(
        num_scalar_prefetch=3,
        grid=(n_blocks,),
        in_specs=[pl.BlockSpec(memory_space=pl.ANY),
                  pl.BlockSpec((1, d, ff), lambda i, be, nr, r0: (be[i], 0, 0)),
                  pl.BlockSpec((1, d, ff), lambda i, be, nr, r0: (be[i], 0, 0)),
                  pl.BlockSpec((1, ff, d), lambda i, be, nr, r0: (be[i], 0, 0))],
        out_specs=pl.BlockSpec(memory_space=pl.ANY),
        scratch_shapes=[pltpu.VMEM((2, MOE_BLOCK, dp), jnp.uint32),
                        pltpu.VMEM((2, MOE_BLOCK, dp), jnp.uint32),
                        pltpu.VMEM((d, ff), BF16),
                        pltpu.VMEM((d, ff), BF16),
                        pltpu.VMEM((ff, d), BF16),
                        pltpu.SemaphoreType.DMA((2,)),
                        pltpu.SemaphoreType.DMA((2,))],
    )
    return pl.pallas_call(
        functools.partial(_expert_kernel, n_rows=n_rows_pad - SORTED_TAIL),
        grid_spec=grid_spec,
        out_shape=jax.ShapeDtypeStruct((n_rows_pad, dp), jnp.uint32),
        compiler_params=_cparams(("arbitrary",)),
        name="experts",
    )(block_expert, block_nreal, block_row0, xs, wg, wu, wd)


def _combine_gather(ys, pos):
    t = pos.shape[0]
    dp = ys.shape[1]
    ncores, nw = _sc_workers()
    per_w = t // nw
    n_ch = per_w // SC_CHUNK
    assert per_w % SC_CHUNK == 0
    idx = [pos[:, k].reshape(nw, n_ch, SC_CHUNK) for k in range(TOP_K)]
    mesh = plsc.VectorSubcoreMesh(core_axis_name="c", subcore_axis_name="s")

    @functools.partial(
        pl.kernel, mesh=mesh,
        out_type=jax.ShapeDtypeStruct((TOP_K, t, dp), ys.dtype),
        scratch_types=[pltpu.VMEM((TOP_K, n_ch, SC_CHUNK), jnp.int32),
                       pltpu.VMEM((SC_BUFFERS, SC_CHUNK, dp), ys.dtype),
                       pltpu.SemaphoreType.DMA((SC_BUFFERS,)), pltpu.SemaphoreType.DMA((SC_BUFFERS,))],
    )
    def gather(ys_hbm, p0_hbm, p1_hbm, out_hbm, idx_v, rows_v, gsem, wsem):
        wid = lax.axis_index("s") * ncores + lax.axis_index("c")
        base = wid * per_w
        pltpu.sync_copy(p0_hbm.at[wid], idx_v.at[0])
        pltpu.sync_copy(p1_hbm.at[wid], idx_v.at[1])
        units = [(c, k) for c in range(n_ch) for k in range(TOP_K)]

        def fetch(u):
            c, k = units[u]
            return pltpu.async_copy(ys_hbm.at[idx_v.at[k, c]], rows_v.at[u % SC_BUFFERS], gsem.at[u % SC_BUFFERS])

        fetches = {u: fetch(u) for u in range(min(SC_BUFFERS - 1, len(units)))}
        writes = {}
        for u, (c, k) in enumerate(units):
            fetches[u].wait()
            writes[u] = pltpu.async_copy(rows_v.at[u % SC_BUFFERS],
                                         out_hbm.at[k, pl.ds(base + c * SC_CHUNK, SC_CHUNK)],
                                         wsem.at[u % SC_BUFFERS])
            if u >= 1:
                writes[u - 1].wait()
            if u + SC_BUFFERS - 1 < len(units):
                fetches[u + SC_BUFFERS - 1] = fetch(u + SC_BUFFERS - 1)
        writes[len(units) - 1].wait()

    return gather(ys, idx[0], idx[1])


def _final_kernel(h_ref, y_ref, route_ref, gain_ref, out_ref):
    r = route_ref[...]
    y = h_ref[...]
    for k, gate_lane in enumerate((ROUTE_G1, ROUTE_G2)):
        y = y + r[:, gate_lane:gate_lane + 1] * _unpack_bf16_pairs(y_ref[k])
    out_ref[...] = y * lax.rsqrt(jnp.mean(y * y, axis=-1, keepdims=True) + NORM_EPS) * gain_ref[...]


def _final(h, y_rows, route, gain, tm):
    t, d = h.shape
    return pl.pallas_call(
        _final_kernel,
        grid=(t // tm,),
        in_specs=[pl.BlockSpec((tm, d), lambda i: (i, 0)),
                  pl.BlockSpec((TOP_K, tm, d // 2), lambda i: (0, i, 0)),
                  pl.BlockSpec((tm, LANES), lambda i: (i, 0)),
                  pl.BlockSpec((1, d), lambda i: (0, 0))],
        out_specs=pl.BlockSpec((tm, d), lambda i: (i, 0)),
        out_shape=jax.ShapeDtypeStruct((t, d), F32),
        compiler_params=_cparams(("arbitrary",)),
        name="final",
    )(h, y_rows, route, gain.reshape(1, d))


def _layer(h3, mix_gain, w_in, attn_gain, decay_f, decay_b, ret_gain, w_out, ffn_gain,
           w_rg, b_rg, w_re, b_re, w_eg, w_eu, w_ed, final_gain):
    b, s, d = h3.shape
    t = b * s
    tm = 512
    attn_qkv, (qr, kr, vr, gr) = _inproj(h3, mix_gain, w_in.astype(BF16), tm)

    o_list, lse_list = [], []
    for window, dilation in DILATED_BRANCHES:
        reach = (window // 2) // dilation
        o, lse = _attn_branch(*attn_qkv[dilation], dilation, reach, tq=128, tqb=512)
        o_list.append(o)
        lse_list.append(lse)

    lg_f = jnp.log1p(-jnp.exp2(decay_f.astype(F32)))
    lg_b = jnp.log1p(-jnp.exp2(decay_b.astype(F32)))
    orr = _retention(qr, kr, vr, gr, lg_f, lg_b, ret_gain).reshape(t, HEAD_GROUP_WIDTH)

    n_route = MOE_GROUPS + N_EXPERTS
    w_router = jnp.zeros((d, LANES), F32).at[:, :n_route].set(jnp.concatenate([w_re, w_rg], axis=1).astype(F32))
    b_router = jnp.zeros((1, LANES), F32).at[0, :n_route].set(jnp.concatenate([b_re, b_rg]).astype(F32))
    h, hn_packed, logits = _outproj(h3.reshape(t, d), o_list, lse_list, orr, attn_gain, w_out.astype(BF16),
                                    ffn_gain, w_router, b_router, tm)

    route, counts_rep = _route(logits, 2 * tm)

    n_blocks = -(-TOP_K * t // MOE_BLOCK) + N_EXPERTS
    counts = counts_rep[:, 0].astype(jnp.int32)
    aligned = ((counts + SEG_ALIGN - 1) // SEG_ALIGN) * SEG_ALIGN
    seg_start = jnp.cumsum(aligned) - aligned
    expert_iota = jnp.arange(N_EXPERTS, dtype=jnp.int32)
    e12 = route[:, ROUTE_E1:ROUTE_E2 + 1].astype(jnp.int32)
    r12 = route[:, ROUTE_R1:ROUTE_R2 + 1].astype(jnp.int32)
    pos = r12 + jnp.sum(jnp.where(e12[..., None] == expert_iota, seg_start, 0), axis=-1)
    nblk = (counts + MOE_BLOCK - 1) // MOE_BLOCK
    blk_end = jnp.cumsum(nblk)
    blk_start = blk_end - nblk
    blk = jnp.arange(n_blocks, dtype=jnp.int32)[:, None]
    owner = (blk >= blk_start) & (blk < blk_end)
    local = (blk - blk_start) * MOE_BLOCK
    block_row0 = jnp.sum(jnp.where(owner, seg_start + local, 0), axis=-1).astype(jnp.int32)
    block_nreal = jnp.sum(jnp.where(owner, jnp.clip(counts - local, 0, MOE_BLOCK), 0), axis=-1).astype(jnp.int32)
    block_expert = jnp.minimum(jnp.sum((blk >= blk_end).astype(jnp.int32), axis=-1), N_EXPERTS - 1)

    n_sorted = TOP_K * t + SORTED_TAIL
    dump_row = n_sorted
    seg_end = seg_start + counts
    hole = seg_end[:, None] + jnp.arange(SEG_ALIGN, dtype=jnp.int32)[None, :]
    hole = jnp.where(hole < (seg_start + aligned)[:, None], hole, dump_row).reshape(-1)
    tail = (seg_start[-1] + aligned[-1]) + jnp.arange(SORTED_TAIL, dtype=jnp.int32)
    tail = jnp.where(tail < n_sorted, tail, dump_row)
    zero_pos = jnp.concatenate([hole, tail]).astype(jnp.int32)

    xs = _dispatch(hn_packed, pos, zero_pos)
    ys = _experts(xs, block_expert, block_nreal, block_row0, w_eg.astype(F32), w_eu.astype(F32),
                  w_ed.astype(F32))
    out = _final(h, _combine_gather(ys, pos), route, final_gain, tm)
    return out.reshape(b, s, d)


def kernel(x, mix_norm_gain, w_in, attn_out_gain, ret_decay_fwd, ret_decay_bwd, ret_out_gain, w_out,
           ffn_norm_gain, w_route_group, b_route_group, w_route_expert, b_route_expert,
           w_expert_gate, w_expert_up, w_expert_down, final_norm_gain):
    depth = mix_norm_gain.shape[0]
    assert depth == 1, "the final rmsnorm is fused into the single layer's combine kernel"
    l = 0
    return _layer(x, mix_norm_gain[l], w_in[l], attn_out_gain[l], ret_decay_fwd[l], ret_decay_bwd[l],
                  ret_out_gain[l], w_out[l], ffn_norm_gain[l], w_route_group[l], b_route_group[l],
                  w_route_expert[l], b_route_expert[l], w_expert_gate[l], w_expert_up[l], w_expert_down[l],
                  final_norm_gain)
```

```python
import functools

import numpy as np
import jax
import jax.numpy as jnp
from jax import lax
from jax.experimental import pallas as pl
from jax.experimental.pallas import tpu as pltpu
from jax.experimental.pallas import tpu_sc as plsc

F32 = jnp.float32
BF16 = jnp.bfloat16

ATTN_HEADS = 8
HEAD_DIM = 64
RET_HEADS = 8
HEAD_GROUP_WIDTH = 512
N_PROJ_GROUPS = 7
DILATED_BRANCHES = ((128, 1), (512, 4), (2048, 16))
ROPE_THETA = 500000.0
ROPE_DIM = HEAD_DIM // 4
RET_THETA = 10000.0
RET_CHUNK = 128
MOE_GROUPS = 4
EXPERTS_PER_GROUP = 8
N_EXPERTS = MOE_GROUPS * EXPERTS_PER_GROUP
MOE_BLOCK = 512
NORM_EPS = 1e-6
NEG_INF = -1e30

LANES = 128
VMEM_LIMIT = 56 * 1024 * 1024


def _cparams(sem):
    return pltpu.CompilerParams(dimension_semantics=sem, vmem_limit_bytes=VMEM_LIMIT)


def _rotary_tables(seq, half, freqs):
    pos = np.arange(seq, dtype=np.float64)[:, None]
    ang = pos * freqs[None, :].astype(np.float64)
    cos, sin = np.cos(ang), np.sin(ang)
    c = np.ones((seq, HEAD_DIM)); sp = np.zeros((seq, HEAD_DIM)); sm = np.zeros((seq, HEAD_DIM))
    c[:, :half] = cos; c[:, half:2 * half] = cos
    sp[:, half:2 * half] = sin
    sm[:, :half] = -sin
    rep = LANES // HEAD_DIM
    return tuple(jnp.asarray(np.tile(t, (1, rep)), F32) for t in (c, sp, sm))


def _rotate(t, c, sp, sm, half):
    outs = []
    for g in range(t.shape[1] // LANES):
        tg = t[:, g * LANES:(g + 1) * LANES]
        outs.append(tg * c + pltpu.roll(tg, half, 1) * sp + pltpu.roll(tg, LANES - half, 1) * sm)
    return jnp.concatenate(outs, axis=1)


CLASS_DILATIONS = tuple(d for _, d in DILATED_BRANCHES if d > 1)
ATTN_Q_SCALE = float(np.log2(np.e)) * HEAD_DIM ** -0.5


def _inproj_kernel(x_ref, gain_ref, w_ref, ca_ref, spa_ref, sma_ref, cr_ref, spr_ref, smr_ref, *rest):
    n_cls = len(CLASS_DILATIONS)
    nat_refs = rest[0:3]
    cls_refs = [rest[3 + 3 * c:6 + 3 * c] for c in range(n_cls)]
    qr_ref, kr_ref, vr_ref, gr_ref = rest[3 + 3 * n_cls:7 + 3 * n_cls]
    stage_ref = rest[7 + 3 * n_cls]
    x = x_ref[0]
    tm = x.shape[0]
    ms = jnp.mean(x * x, axis=-1, keepdims=True)
    xn = (x * lax.rsqrt(ms + NORM_EPS) * gain_ref[...]).astype(BF16)
    gw = HEAD_GROUP_WIDTH

    def proj(c):
        return jnp.dot(xn, w_ref[:, c * gw:(c + 1) * gw], preferred_element_type=F32)

    a_tabs = (ca_ref[...], spa_ref[...], sma_ref[...])
    r_tabs = (cr_ref[...], spr_ref[...], smr_ref[...])
    attn_vals = ((_rotate(proj(0), *a_tabs, ROPE_DIM // 2) * ATTN_Q_SCALE),
                 _rotate(proj(1), *a_tabs, ROPE_DIM // 2),
                 proj(2))
    for j, val in enumerate(attn_vals):
        nat_refs[j][0] = val.astype(BF16)
        for g in range(gw // LANES):
            stage_ref[g] = val[:, g * LANES:(g + 1) * LANES]
        for c, d in enumerate(CLASS_DILATIONS):
            for r in range(d):
                for g in range(gw // LANES):
                    col = r * gw + g * LANES
                    cls_refs[c][j][0, :, col:col + LANES] = (
                        stage_ref[g, pl.ds(r, tm // d, stride=d), :].astype(BF16))
    qr_ref[0] = _rotate(proj(3), *r_tabs, HEAD_DIM // 2).astype(BF16)
    kr_ref[0] = (_rotate(proj(4), *r_tabs, HEAD_DIM // 2) * (HEAD_DIM ** -0.5)).astype(BF16)
    vr_ref[0] = proj(5).astype(BF16)
    g = proj(6)
    gr_ref[0] = (g * jax.nn.sigmoid(g)).astype(BF16)


def _inproj(x, gain, w_in_bf16, tm):
    b, s, d = x.shape
    rope_freqs = ROPE_THETA ** (-np.arange(0, ROPE_DIM, 2, dtype=np.float32) / ROPE_DIM)
    ret_freqs = RET_THETA ** (-np.linspace(0.0, 1.0, HEAD_DIM // 2, dtype=np.float32))
    tabs = _rotary_tables(s, ROPE_DIM // 2, rope_freqs) + _rotary_tables(s, HEAD_DIM // 2, ret_freqs)
    gw = HEAD_GROUP_WIDTH
    tab_spec = pl.BlockSpec((tm, LANES), lambda si, bi: (si, 0))

    def view(dil):
        return (pl.BlockSpec((1, tm // dil, dil * gw), lambda si, bi: (bi, si, 0)),
                jax.ShapeDtypeStruct((b, s // dil, dil * gw), BF16))

    views = [view(1)] * 3 + [view(dil) for dil in CLASS_DILATIONS for _ in range(3)] + [view(1)] * 4
    outs = pl.pallas_call(
        _inproj_kernel,
        grid=(s // tm, b),
        in_specs=[pl.BlockSpec((1, tm, d), lambda si, bi: (bi, si, 0)),
                  pl.BlockSpec((1, d), lambda si, bi: (0, 0)),
                  pl.BlockSpec(w_in_bf16.shape, lambda si, bi: (0, 0))] + [tab_spec] * 6,
        out_specs=[v[0] for v in views],
        out_shape=[v[1] for v in views],
        scratch_shapes=[pltpu.VMEM((gw // LANES, tm, LANES), F32)],
        compiler_params=_cparams(("arbitrary", "arbitrary")),
        name="inproj",
    )(x, gain.reshape(1, d), w_in_bf16, *tabs)
    n_attn = 3 * (1 + len(CLASS_DILATIONS))
    attn_qkv = {dil: outs[3 * c:3 * c + 3] for c, dil in enumerate((1,) + CLASS_DILATIONS)}
    return attn_qkv, outs[n_attn:]


ATTN_WINDOW_CASES = 3


def _attn_kernel(q_ref, k_ref, v_ref, o_ref, lse_ref, bias_ref, *, length, tq, reach):
    tqb = q_ref.shape[1]
    win = tq + 2 * reach
    heads_per_pair = LANES // HEAD_DIM
    qi = pl.program_id(2)
    lane = lax.broadcasted_iota(jnp.int32, (1, LANES), 1)
    lane_t = lax.broadcasted_iota(jnp.int32, (tq, LANES), 1)
    ones = jnp.ones((win, LANES), BF16)

    @pl.when((pl.program_id(0) == 0) & (pl.program_id(1) == 0) & (qi == 0))
    def _():
        diff = (lax.broadcasted_iota(jnp.int32, (heads_per_pair * tq, win), 1)
                - (lax.broadcasted_iota(jnp.int32, (heads_per_pair * tq, win), 0) & (tq - 1)))
        for case in range(ATTN_WINDOW_CASES):
            off = case * reach
            bias_ref[case] = jnp.where((diff >= off - reach) & (diff <= off + reach), 0.0, NEG_INF)

    def sub(t, cls):
        q0 = qi * tqb + t * tq
        ws = jnp.clip(q0 - reach, 0, length - win)
        ws = pl.multiple_of(ws, reach)
        bias = bias_ref[(q0 - ws) // reach]
        rows = pl.ds(t * tq, tq)
        m_tile = jnp.zeros((tq, LANES), F32)
        l_tile = jnp.ones((tq, LANES), F32)
        for g in range(HEAD_GROUP_WIDTH // LANES):
            cols = slice(cls * HEAD_GROUP_WIDTH + g * LANES, cls * HEAD_GROUP_WIDTH + (g + 1) * LANES)
            qg = q_ref[0, rows, cols]
            kw = k_ref[0, pl.ds(ws, win), cols]
            v_ones = jnp.concatenate([v_ref[0, pl.ds(ws, win), cols], ones], axis=1)
            hms = [(lane >= hh * HEAD_DIM) & (lane < (hh + 1) * HEAD_DIM) for hh in range(heads_per_pair)]
            q2 = jnp.concatenate([qg * hm.astype(BF16) for hm in hms], axis=0)
            sc = lax.dot_general(q2, kw, (((1,), (1,)), ((), ())), preferred_element_type=F32) + bias
            m = jnp.max(sc, axis=-1, keepdims=True)
            p = jnp.exp2(sc - m).astype(BF16)
            pv = jnp.dot(p, v_ones, preferred_element_type=F32)
            l = pv[:, LANES:]
            o = pv[:, :LANES] / l
            o_pair = jnp.zeros((tq, LANES), F32)
            for hh in range(heads_per_pair):
                part = slice(hh * tq, (hh + 1) * tq)
                head_lane = lane_t == g * heads_per_pair + hh
                o_pair = jnp.where(hms[hh], o[part], o_pair)
                m_tile = jnp.where(head_lane, m[part], m_tile)
                l_tile = jnp.where(head_lane, l[part], l_tile)
            o_ref[0, rows, cols] = o_pair.astype(BF16)
        lse_ref[0, rows, cls * LANES:(cls + 1) * LANES] = m_tile + jnp.log2(l_tile)

    for cls in range(q_ref.shape[2] // HEAD_GROUP_WIDTH):
        for t in range(tqb // tq):
            sub(t, cls)


def _attn_branch(qc, kc, vc, dilation, reach, tq, tqb, sub_tiles=4):
    b, length, dw = qc.shape
    w = dw // dilation
    tqb = min(tqb, length)
    assert tq % reach == 0 and tq > reach and length % tq == 0 and length >= tq + 2 * reach
    ncls = min(dilation, max(1, sub_tiles // (tqb // tq)))
    q_spec = pl.BlockSpec((1, tqb, ncls * w), lambda bi, r, qi: (bi, qi, r))
    kv_spec = pl.BlockSpec((1, length, ncls * w), lambda bi, r, qi: (bi, 0, r))
    o, lse = pl.pallas_call(
        functools.partial(_attn_kernel, length=length, tq=tq, reach=reach),
        grid=(b, dilation // ncls, length // tqb),
        in_specs=[q_spec, kv_spec, kv_spec],
        out_specs=[q_spec, pl.BlockSpec((1, tqb, ncls * LANES), lambda bi, r, qi: (bi, qi, r))],
        out_shape=[jax.ShapeDtypeStruct((b, length, dilation * w), BF16),
                   jax.ShapeDtypeStruct((b, length, dilation * LANES), F32)],
        scratch_shapes=[pltpu.VMEM((ATTN_WINDOW_CASES, (LANES // HEAD_DIM) * tq, tq + 2 * reach), F32)],
        compiler_params=_cparams(("arbitrary", "arbitrary", "arbitrary")),
        name=f"attn_d{dilation}",
    )(qc, kc, vc)
    return o, lse


RET_TAB_QF, RET_TAB_QB, RET_TAB_KF, RET_TAB_KB = range(4)


def _retention_kernel(lgf_ref, lgb_ref, q_ref, k_ref, v_ref, g_ref, gain_ref, o_ref,
                      tab_ref, dec_ref, sb_ref, st_ref, *, chunk, unroll):
    c = chunk
    n = q_ref.shape[1] // c
    width = q_ref.shape[2]
    n_pairs = width // LANES
    heads_per_pair = LANES // HEAD_DIM
    n_heads = n_pairs * heads_per_pair
    head0 = pl.program_id(1) * n_heads
    lane_w = lax.broadcasted_iota(jnp.int32, (1, width), 1)
    lgf = [lgf_ref[head0 + hd] for hd in range(n_heads)]
    lgb = [lgb_ref[head0 + hd] for hd in range(n_heads)]
    lgf_lane = jnp.zeros((1, width), F32)
    lgb_lane = jnp.zeros((1, width), F32)
    for hd in range(n_heads):
        in_head = (lane_w >= hd * HEAD_DIM) & (lane_w < (hd + 1) * HEAD_DIM)
        lgf_lane = jnp.where(in_head, lgf[hd], lgf_lane)
        lgb_lane = jnp.where(in_head, lgb[hd], lgb_lane)
    idx = lax.broadcasted_iota(jnp.int32, (c, width), 0).astype(F32)
    tab_ref[RET_TAB_QF] = jnp.exp((idx + 1.0) * lgf_lane)
    tab_ref[RET_TAB_QB] = jnp.exp((c - idx) * lgb_lane)
    tab_ref[RET_TAB_KF] = jnp.exp((c - 1.0 - idx) * lgf_lane)
    tab_ref[RET_TAB_KB] = jnp.exp(idx * lgb_lane)
    sdf = jnp.exp(c * lgf_lane)
    sdb = jnp.exp(c * lgb_lane)
    dmat = (lax.broadcasted_iota(jnp.int32, (c, c), 0)
            - lax.broadcasted_iota(jnp.int32, (c, c), 1)).astype(F32)
    for hd in range(n_heads):
        dec_ref[hd // heads_per_pair, :, (hd % heads_per_pair) * c:(hd % heads_per_pair + 1) * c] = (
            jnp.where(dmat >= 0, jnp.exp(dmat * lgf[hd]), jnp.exp(-dmat * lgb[hd])))
    lane = lax.broadcasted_iota(jnp.int32, (1, LANES), 1)
    lane_hi = lane >= HEAD_DIM
    head_masks = [((lane >= hh * HEAD_DIM) & (lane < (hh + 1) * HEAD_DIM)).astype(BF16)
                  for hh in range(heads_per_pair)]
    row_hi = lax.broadcasted_iota(jnp.int32, (LANES, LANES), 0) >= HEAD_DIM
    col_hi = lax.broadcasted_iota(jnp.int32, (LANES, LANES), 1) >= HEAD_DIM
    blockdiag = row_hi == col_hi

    def kv_state(kd, vv):
        kt = jnp.transpose(kd).astype(BF16)
        return jnp.where(blockdiag, jnp.dot(kt, vv, preferred_element_type=F32), 0.0)

    st_ref[...] = jnp.zeros_like(st_ref)
    sb_ref[n - 1] = jnp.zeros(sb_ref.shape[1:], sb_ref.dtype)

    def back(i, carry):
        nn = n - 1 - i
        rows = pl.ds(pl.multiple_of(nn * c, c), c)
        for p in range(n_pairs):
            cols = slice(p * LANES, (p + 1) * LANES)
            kd = k_ref[0, rows, cols].astype(F32) * tab_ref[RET_TAB_KB, :, cols]
            new = st_ref[p] * sdb[:, cols] + kv_state(kd, v_ref[0, rows, cols])
            st_ref[p] = new
            sb_ref[nn - 1, p] = new.astype(BF16)
        return carry

    lax.fori_loop(0, n - 1, back, 0, unroll=unroll)

    st_ref[...] = jnp.zeros_like(st_ref)

    def fwd(nn, carry):
        rows = pl.ds(pl.multiple_of(nn * c, c), c)
        for p in range(n_pairs):
            cols = slice(p * LANES, (p + 1) * LANES)
            qq = q_ref[0, rows, cols]
            kk = k_ref[0, rows, cols]
            vv = v_ref[0, rows, cols]
            qf = qq.astype(F32)
            sf = st_ref[p]
            qcat = jnp.concatenate([(qf * tab_ref[RET_TAB_QF, :, cols]).astype(BF16),
                                    (qf * tab_ref[RET_TAB_QB, :, cols]).astype(BF16)], axis=1)
            scat = jnp.concatenate([sf.astype(BF16), sb_ref[nn, p]], axis=0)
            o = jnp.dot(qcat, scat, preferred_element_type=F32)
            k2 = jnp.concatenate([kk * hm for hm in head_masks], axis=0)
            v2 = jnp.concatenate([vv * hm for hm in head_masks], axis=0)
            sc = lax.dot_general(qq, k2, (((1,), (1,)), ((), ())), preferred_element_type=F32)
            o = o + jnp.dot((sc * dec_ref[p]).astype(BF16), v2, preferred_element_type=F32)
            s_lo = jnp.sum(jnp.where(lane_hi, 0.0, o), axis=-1, keepdims=True)
            s_hi = jnp.sum(jnp.where(lane_hi, o, 0.0), axis=-1, keepdims=True)
            mu = jnp.where(lane_hi, s_hi, s_lo) * (1.0 / HEAD_DIM)
            dev = o - mu
            d2 = dev * dev
            v_lo = jnp.sum(jnp.where(lane_hi, 0.0, d2), axis=-1, keepdims=True)
            v_hi = jnp.sum(jnp.where(lane_hi, d2, 0.0), axis=-1, keepdims=True)
            var = jnp.where(lane_hi, v_hi, v_lo) * (1.0 / HEAD_DIM)
            out = dev * lax.rsqrt(var + NORM_EPS) * gain_ref[:, cols] * g_ref[0, rows, cols].astype(F32)
            o_ref[0, rows, cols] = out.astype(BF16)
            kd = kk.astype(F32) * tab_ref[RET_TAB_KF, :, cols]
            st_ref[p] = sf * sdf[:, cols] + kv_state(kd, vv)
        return carry

    lax.fori_loop(0, n, fwd, 0, unroll=unroll)


def _retention(qr, kr, vr, gate, lg_f, lg_b, out_gain, width=512, unroll=4):
    b, s, w = qr.shape
    n_pairs = width // LANES
    n_heads = width // HEAD_DIM
    spec = pl.BlockSpec((1, s, width), lambda bi, p, *_: (bi, 0, p))
    grid_spec = pltpu.PrefetchScalarGridSpec(
        num_scalar_prefetch=2,
        grid=(b, w // width),
        in_specs=[spec, spec, spec, spec, pl.BlockSpec((1, width), lambda bi, p, *_: (0, p))],
        out_specs=spec,
        scratch_shapes=[pltpu.VMEM((4, RET_CHUNK, width), F32),
                        pltpu.VMEM((n_pairs, RET_CHUNK, (LANES // HEAD_DIM) * RET_CHUNK), F32),
                        pltpu.VMEM((s // RET_CHUNK, n_pairs, LANES, LANES), BF16),
                        pltpu.VMEM((n_pairs, LANES, LANES), F32)],
    )
    return pl.pallas_call(
        functools.partial(_retention_kernel, chunk=RET_CHUNK, unroll=unroll),
        grid_spec=grid_spec,
        out_shape=jax.ShapeDtypeStruct((b, s, w), BF16),
        compiler_params=_cparams(("arbitrary", "arbitrary")),
        name="retention",
    )(lg_f, lg_b, qr, kr, vr, gate, out_gain.reshape(1, w))


def _split_bf16(t):
    hi = t.astype(BF16)
    lo = (t - hi.astype(F32)).astype(BF16)
    return hi, lo


def _pack_bf16_pairs(t):
    n = t.shape[1] // 2
    hi = pltpu.bitcast(t[:, :n].astype(BF16).astype(F32), jnp.uint32)
    lo = pltpu.bitcast(t[:, n:].astype(BF16).astype(F32), jnp.uint32)
    return hi | (lo >> 16)


def _unpack_bf16_pairs(u):
    hi = pltpu.bitcast(u & jnp.uint32(0xFFFF0000), F32)
    lo = pltpu.bitcast(u << 16, F32)
    return jnp.concatenate([hi, lo], axis=1)


def _outproj_kernel(x_ref, o1_ref, o2_ref, o3_ref, l1_ref, l2_ref, l3_ref, orr_ref, ga_ref, expand_ref,
                    wout_ref, gf_ref, wr_ref, br_ref, h_ref, hn_ref, logit_ref,
                    *nat_refs):
    tm = x_ref.shape[0]
    gw = HEAD_GROUP_WIDTH
    os, ls = [], []
    for (_, dil), o_ref, l_ref in zip(DILATED_BRANCHES, (o1_ref, o2_ref, o3_ref), (l1_ref, l2_ref, l3_ref)):
        if dil == 1:
            os.append(o_ref[...].astype(F32))
            ls.append(l_ref[...])
            continue
        c = CLASS_DILATIONS.index(dil)
        onat_ref, lnat_ref = nat_refs[2 * c], nat_refs[2 * c + 1]
        for r in range(dil):
            rows = pl.ds(r, tm // dil, stride=dil)
            for g in range(gw // LANES):
                col = r * gw + g * LANES
                onat_ref[g, rows, :] = o_ref[:, col:col + LANES].astype(F32)
            lnat_ref[rows, :] = l_ref[:, r * LANES:(r + 1) * LANES]
        os.append(jnp.concatenate([onat_ref[g] for g in range(gw // LANES)], axis=1))
        ls.append(lnat_ref[...])
    mx = jnp.maximum(jnp.maximum(ls[0], ls[1]), ls[2])
    es = [jnp.exp2(l - mx) for l in ls]
    inv = 1.0 / (es[0] + es[1] + es[2])
    expand = expand_ref[...]
    oa = jnp.zeros((tm, gw), F32)
    for e, o in zip(es, os):
        wexp = jnp.dot(jnp.concatenate(_split_bf16(e * inv), axis=1), expand, preferred_element_type=F32)
        oa = oa + wexp * o
    oa = oa * lax.rsqrt(jnp.mean(oa * oa, axis=-1, keepdims=True) + NORM_EPS) * ga_ref[...]
    mixed = jnp.concatenate([oa.astype(BF16), orr_ref[...]], axis=1)
    h = x_ref[...] + jnp.dot(mixed, wout_ref[...], preferred_element_type=F32)
    h_ref[...] = h
    hn = h * lax.rsqrt(jnp.mean(h * h, axis=-1, keepdims=True) + NORM_EPS) * gf_ref[...]
    hn_ref[...] = _pack_bf16_pairs(hn)
    prod = jnp.dot(jnp.concatenate(_split_bf16(hn), axis=0), wr_ref[...], preferred_element_type=F32)
    logit_ref[...] = prod[:tm, :LANES] + prod[:tm, LANES:] + prod[tm:, :LANES] + br_ref[...]


def _outproj(x2, o_list, lse_list, orr, attn_gain, w_out_bf16, ffn_gain, w_router, b_router, tm):
    t, d = x2.shape
    w = HEAD_GROUP_WIDTH
    expand = np.zeros((LANES, w), np.float32)
    for hd in range(ATTN_HEADS):
        expand[hd, hd * HEAD_DIM:(hd + 1) * HEAD_DIM] = 1.0
    expand = jnp.asarray(np.concatenate([expand, expand], axis=0), BF16)
    wr_hi = w_router.astype(BF16)
    wr = jnp.concatenate([wr_hi, (w_router - wr_hi.astype(F32)).astype(BF16)], axis=1)
    row = lambda width, dil=1: pl.BlockSpec((tm // dil, dil * width), lambda i: (i, 0))
    full = lambda a: pl.BlockSpec(a.shape, lambda i: (0,) * a.ndim)
    ga = attn_gain.reshape(1, w)
    gf = ffn_gain.reshape(1, d)
    dils = [dil for _, dil in DILATED_BRANCHES]
    o_flat = [o.reshape(t // dil, dil * w) for o, dil in zip(o_list, dils)]
    l_flat = [l.reshape(t // dil, dil * LANES) for l, dil in zip(lse_list, dils)]
    nat_scratch = []
    for _ in CLASS_DILATIONS:
        nat_scratch += [pltpu.VMEM((w // LANES, tm, LANES), F32), pltpu.VMEM((tm, LANES), F32)]
    return pl.pallas_call(
        _outproj_kernel,
        grid=(t // tm,),
        in_specs=[row(d)] + [row(w, dil) for dil in dils] + [row(LANES, dil) for dil in dils] + [row(w)]
                 + [full(ga), full(expand), full(w_out_bf16), full(gf), full(wr), full(b_router)],
        out_specs=[row(d), row(d // 2), row(LANES)],
        out_shape=[jax.ShapeDtypeStruct((t, d), F32),
                   jax.ShapeDtypeStruct((t, d // 2), jnp.uint32),
                   jax.ShapeDtypeStruct((t, LANES), F32)],
        scratch_shapes=nat_scratch,
        compiler_params=_cparams(("arbitrary",)),
        name="outproj",
    )(x2, *o_flat, *l_flat, orr, ga, expand, w_out_bf16, gf, wr, b_router)


ROUTE_E1, ROUTE_E2, ROUTE_G1, ROUTE_G2, ROUTE_R1, ROUTE_R2 = range(6)
GROUP_LANE0 = N_EXPERTS
SUBLANES = 8


def _route_kernel(logit_ref, tri_ref, route_ref, count_ref, run_ref):
    @pl.when(pl.program_id(0) == 0)
    def _():
        run_ref[...] = jnp.zeros_like(run_ref)

    lg_t = jnp.transpose(logit_ref[...])
    tm = lg_t.shape[1]
    assert EXPERTS_PER_GROUP == SUBLANES and MOE_GROUPS <= SUBLANES
    rid = lax.broadcasted_iota(jnp.int32, (SUBLANES, tm), 0)
    big = jnp.int32(1 << 20)

    def top(vals):
        m = jnp.max(vals, axis=0, keepdims=True)
        i = jnp.min(jnp.where(vals == m, rid, big), axis=0, keepdims=True)
        return m, i

    gl = jnp.where(rid < MOE_GROUPS, lg_t[GROUP_LANE0:GROUP_LANE0 + SUBLANES], -jnp.inf)
    gmax, gidx = top(gl)
    group_gate = 1.0 / jnp.sum(jnp.exp(gl - gmax), axis=0, keepdims=True)
    el = lg_t[0:EXPERTS_PER_GROUP]
    for g in range(1, MOE_GROUPS):
        el = jnp.where(gidx == g, lg_t[g * EXPERTS_PER_GROUP:(g + 1) * EXPERTS_PER_GROUP], el)
    t1, i1 = top(el)
    t2, i2 = top(jnp.where(rid == i1, -jnp.inf, el))
    e21 = jnp.exp(t2 - t1)
    g1 = group_gate / (1.0 + e21)
    g2 = group_gate * e21 / (1.0 + e21)
    e1 = gidx * EXPERTS_PER_GROUP + i1
    e2 = gidx * EXPERTS_PER_GROUP + i2
    erow = lax.broadcasted_iota(jnp.int32, (N_EXPERTS, tm), 0)
    oh1 = erow == e1
    oh2 = erow == e2
    cnt = oh1.astype(F32) + oh2.astype(F32)
    run = run_ref[:, 0:1]
    prefix = jnp.dot(cnt.astype(BF16), tri_ref[...], preferred_element_type=F32) + run
    r1 = jnp.sum(jnp.where(oh1, prefix, 0.0), axis=0, keepdims=True)
    r2 = jnp.sum(jnp.where(oh2, prefix, 0.0), axis=0, keepdims=True)
    new_run = jnp.broadcast_to(run + jnp.sum(cnt, axis=1, keepdims=True), run_ref.shape)
    run_ref[...] = new_run
    count_ref[...] = new_run
    out = jnp.zeros((SUBLANES, tm), F32)
    for row, val in ((ROUTE_E1, e1.astype(F32)), (ROUTE_E2, e2.astype(F32)), (ROUTE_G1, g1),
                     (ROUTE_G2, g2), (ROUTE_R1, r1), (ROUTE_R2, r2)):
        out = jnp.where(rid == row, val, out)
    out_t = jnp.concatenate([out, jnp.zeros((LANES - SUBLANES, tm), F32)], axis=0)
    route_ref[...] = jnp.transpose(out_t)


def _route(logits, tm):
    t = logits.shape[0]
    tri = jnp.asarray(np.triu(np.ones((tm, tm), np.float32), 1), BF16)
    return pl.pallas_call(
        _route_kernel,
        grid=(t // tm,),
        in_specs=[pl.BlockSpec((tm, LANES), lambda i: (i, 0)), pl.BlockSpec((tm, tm), lambda i: (0, 0))],
        out_specs=[pl.BlockSpec((tm, LANES), lambda i: (i, 0)),
                   pl.BlockSpec((N_EXPERTS, LANES), lambda i: (0, 0))],
        out_shape=[jax.ShapeDtypeStruct((t, LANES), F32), jax.ShapeDtypeStruct((N_EXPERTS, LANES), F32)],
        scratch_shapes=[pltpu.VMEM((N_EXPERTS, LANES), F32)],
        compiler_params=_cparams(("arbitrary",)),
        name="route",
    )(logits, tri)


TOP_K = 2


SEG_ALIGN = 8
SORTED_TAIL = N_EXPERTS * SEG_ALIGN + MOE_BLOCK


def _tail_pieces():
    full, rest = divmod(SORTED_TAIL, MOE_BLOCK)
    return [MOE_BLOCK] * full + ([rest] if rest else [])


SC_CHUNK = 64
SC_BUFFERS = 3
ZERO_ROWS = N_EXPERTS * SEG_ALIGN + SORTED_TAIL
SORTED_ROWS_EXTRA = SORTED_TAIL + SEG_ALIGN


def _sc_workers():
    info = plsc.get_sparse_core_info()
    return info.num_cores, info.num_cores * info.num_subcores


def _dispatch(hn_packed, pos, zero_pos):
    t, dp = hn_packed.shape
    ncores, nw = _sc_workers()
    per_w = t // nw
    n_ch = per_w // SC_CHUNK
    z_rows = ZERO_ROWS // nw
    assert per_w % SC_CHUNK == 0 and ZERO_ROWS % nw == 0 and z_rows % SEG_ALIGN == 0 and TOP_K == 2
    idx = [pos[:, k].reshape(nw, n_ch, SC_CHUNK) for k in range(TOP_K)]
    zeros = jnp.zeros((z_rows, dp), hn_packed.dtype)
    mesh = plsc.VectorSubcoreMesh(core_axis_name="c", subcore_axis_name="s")

    @functools.partial(
        pl.kernel, mesh=mesh,
        out_type=jax.ShapeDtypeStruct((TOP_K * t + SORTED_ROWS_EXTRA, dp), hn_packed.dtype),
        scratch_types=[pltpu.VMEM((n_ch, SC_CHUNK), jnp.int32), pltpu.VMEM((n_ch, SC_CHUNK), jnp.int32),
                       pltpu.VMEM((z_rows,), jnp.int32),
                       pltpu.VMEM((SC_BUFFERS, SC_CHUNK, dp), hn_packed.dtype),
                       pltpu.VMEM((z_rows, dp), hn_packed.dtype),
                       pltpu.SemaphoreType.DMA((SC_BUFFERS,)), pltpu.SemaphoreType.DMA((SC_BUFFERS,)),
                       pltpu.SemaphoreType.DMA],
    )
    def scatter(hn_hbm, p0_hbm, p1_hbm, zpos_hbm, zeros_hbm, xs_hbm, i0_v, i1_v, iz_v, rows_v, zero_v,
                lsem, ssem, zsem):
        wid = lax.axis_index("s") * ncores + lax.axis_index("c")
        base = wid * per_w
        zero_load = pltpu.async_copy(zeros_hbm, zero_v, zsem)
        pltpu.sync_copy(zpos_hbm.at[wid], iz_v)
        pltpu.sync_copy(p0_hbm.at[wid], i0_v)
        pltpu.sync_copy(p1_hbm.at[wid], i1_v)

        def load(c):
            return pltpu.async_copy(hn_hbm.at[pl.ds(base + c * SC_CHUNK, SC_CHUNK)], rows_v.at[c % SC_BUFFERS],
                                    lsem.at[c % SC_BUFFERS])

        loads = {c: load(c) for c in range(min(SC_BUFFERS - 1, n_ch))}
        scat = {}
        for c in range(n_ch):
            slot = c % SC_BUFFERS
            loads[c].wait()
            scat[c] = (pltpu.async_copy(rows_v.at[slot], xs_hbm.at[i0_v.at[c]], ssem.at[slot]),
                       pltpu.async_copy(rows_v.at[slot], xs_hbm.at[i1_v.at[c]], ssem.at[slot]))
            if c >= 1:
                for d in scat[c - 1]:
                    d.wait()
            if c + SC_BUFFERS - 1 < n_ch:
                loads[c + SC_BUFFERS - 1] = load(c + SC_BUFFERS - 1)
        zero_load.wait()
        zero_scatter = pltpu.async_copy(zero_v, xs_hbm.at[iz_v], zsem)
        for d in scat[n_ch - 1]:
            d.wait()
        zero_scatter.wait()

    return scatter(hn_packed, idx[0], idx[1], zero_pos.reshape(nw, z_rows), zeros)


def _expert_kernel(bexp_ref, nreal_ref, row0_ref, xs_hbm, wg_ref, wu_ref, wd_ref, ys_hbm,
                   xbuf, ybuf, wg_bf, wu_bf, wd_bf, isem, osem, *, n_rows):
    i = pl.program_id(0)
    nb = pl.num_programs(0)
    slot = i % 2
    nslot = 1 - slot
    n_cur = nreal_ref[i]
    prev = jnp.maximum(i - 1, 0)
    nxt = jnp.minimum(i + 1, nb - 1)

    def in_copy(blk, s):
        row0 = pl.multiple_of(row0_ref[blk], SEG_ALIGN)
        return pltpu.make_async_copy(xs_hbm.at[pl.ds(row0, MOE_BLOCK)], xbuf.at[s], isem.at[s])

    def out_copy(blk, s):
        row0 = pl.multiple_of(row0_ref[blk], SEG_ALIGN)
        return pltpu.make_async_copy(ybuf.at[s], ys_hbm.at[pl.ds(row0, MOE_BLOCK)], osem.at[s])

    @pl.when(i == 0)
    def _():
        ybuf[...] = jnp.zeros_like(ybuf)
        tails = [pltpu.make_async_copy(ybuf.at[s, pl.ds(0, size)], ys_hbm.at[pl.ds(n_rows + s * MOE_BLOCK, size)],
                                       osem.at[s])
                 for s, size in enumerate(_tail_pieces())]
        for tail in tails:
            tail.start()
        for tail in tails:
            tail.wait()

        @pl.when(n_cur > 0)
        def _():
            in_copy(i, slot).start()

    @pl.when((i + 1 < nb) & (nreal_ref[nxt] > 0))
    def _():
        in_copy(nxt, nslot).start()

    @pl.when((n_cur > 0) & ((i == 0) | (bexp_ref[i] != bexp_ref[prev])))
    def _():
        wg_bf[...] = wg_ref[0].astype(BF16)
        wu_bf[...] = wu_ref[0].astype(BF16)
        wd_bf[...] = wd_ref[0].astype(BF16)

    @pl.when(n_cur > 0)
    def _():
        in_copy(i, slot).wait()
        xb = _unpack_bf16_pairs(xbuf[slot]).astype(BF16)
        gate = jnp.dot(xb, wg_bf[...], preferred_element_type=F32)
        up = jnp.dot(xb, wu_bf[...], preferred_element_type=F32)
        hid = (gate * jax.nn.sigmoid(gate) * up).astype(BF16)
        ybuf[slot] = _pack_bf16_pairs(jnp.dot(hid, wd_bf[...], preferred_element_type=F32))

    @pl.when((i >= 1) & (nreal_ref[prev] > 0))
    def _():
        out_copy(prev, nslot).wait()

    @pl.when(n_cur > 0)
    def _():
        out_copy(i, slot).start()

        @pl.when(i == nb - 1)
        def _():
            out_copy(i, slot).wait()


def _experts(xs, block_expert, block_nreal, block_row0, wg, wu, wd):
    n_blocks = block_expert.shape[0]
    assert n_blocks >= 2 and len(_tail_pieces()) <= 2
    n_rows_pad = xs.shape[0] - SORTED_ROWS_EXTRA + SORTED_TAIL
    dp = xs.shape[1]
    _, d, ff = wg.shape
    grid_spec = pltpu.PrefetchScalarGridSpec(
        num_scalar_prefetch=3,
        grid=(n_blocks,),
        in_specs=[pl.BlockSpec(memory_space=pl.ANY),
                  pl.BlockSpec((1, d, ff), lambda i, be, nr, r0: (be[i], 0, 0)),
                  pl.BlockSpec((1, d, ff), lambda i, be, nr, r0: (be[i], 0, 0)),
                  pl.BlockSpec((1, ff, d), lambda i, be, nr, r0: (be[i], 0, 0))],
        out_specs=pl.BlockSpec(memory_space=pl.ANY),
        scratch_shapes=[pltpu.VMEM((2, MOE_BLOCK, dp), jnp.uint32),
                        pltpu.VMEM((2, MOE_BLOCK, dp), jnp.uint32),
                        pltpu.VMEM((d, ff), BF16),
                        pltpu.VMEM((d, ff), BF16),
                        pltpu.VMEM((ff, d), BF16),
                        pltpu.SemaphoreType.DMA((2,)),
                        pltpu.SemaphoreType.DMA((2,))],
    )
    return pl.pallas_call(
        functools.partial(_expert_kernel, n_rows=n_rows_pad - SORTED_TAIL),
        grid_spec=grid_spec,
        out_shape=jax.ShapeDtypeStruct((n_rows_pad, dp), jnp.uint32),
        compiler_params=_cparams(("arbitrary",)),
        name="experts",
    )(block_expert, block_nreal, block_row0, xs, wg, wu, wd)


def _combine_gather(ys, pos):
    t = pos.shape[0]
    dp = ys.shape[1]
    ncores, nw = _sc_workers()
    per_w = t // nw
    n_ch = per_w // SC_CHUNK
    assert per_w % SC_CHUNK == 0
    idx = [pos[:, k].reshape(nw, n_ch, SC_CHUNK) for k in range(TOP_K)]
    mesh = plsc.VectorSubcoreMesh(core_axis_name="c", subcore_axis_name="s")

    @functools.partial(
        pl.kernel, mesh=mesh,
        out_type=jax.ShapeDtypeStruct((TOP_K, t, dp), ys.dtype),
        scratch_types=[pltpu.VMEM((TOP_K, n_ch, SC_CHUNK), jnp.int32),
                       pltpu.VMEM((SC_BUFFERS, SC_CHUNK, dp), ys.dtype),
                       pltpu.SemaphoreType.DMA((SC_BUFFERS,)), pltpu.SemaphoreType.DMA((SC_BUFFERS,))],
    )
    def gather(ys_hbm, p0_hbm, p1_hbm, out_hbm, idx_v, rows_v, gsem, wsem):
        wid = lax.axis_index("s") * ncores + lax.axis_index("c")
        base = wid * per_w
        pltpu.sync_copy(p0_hbm.at[wid], idx_v.at[0])
        pltpu.sync_copy(p1_hbm.at[wid], idx_v.at[1])
        units = [(c, k) for c in range(n_ch) for k in range(TOP_K)]

        def fetch(u):
            c, k = units[u]
            return pltpu.async_copy(ys_hbm.at[idx_v.at[k, c]], rows_v.at[u % SC_BUFFERS], gsem.at[u % SC_BUFFERS])

        fetches = {u: fetch(u) for u in range(min(SC_BUFFERS - 1, len(units)))}
        writes = {}
        for u, (c, k) in enumerate(units):
            fetches[u].wait()
            writes[u] = pltpu.async_copy(rows_v.at[u % SC_BUFFERS],
                                         out_hbm.at[k, pl.ds(base + c * SC_CHUNK, SC_CHUNK)],
                                         wsem.at[u % SC_BUFFERS])
            if u >= 1:
                writes[u - 1].wait()
            if u + SC_BUFFERS - 1 < len(units):
                fetches[u + SC_BUFFERS - 1] = fetch(u + SC_BUFFERS - 1)
        writes[len(units) - 1].wait()

    return gather(ys, idx[0], idx[1])


def _final_kernel(h_ref, y_ref, route_ref, gain_ref, out_ref):
    r = route_ref[...]
    y = h_ref[...]
    for k, gate_lane in enumerate((ROUTE_G1, ROUTE_G2)):
        y = y + r[:, gate_lane:gate_lane + 1] * _unpack_bf16_pairs(y_ref[k])
    out_ref[...] = y * lax.rsqrt(jnp.mean(y * y, axis=-1, keepdims=True) + NORM_EPS) * gain_ref[...]


def _final(h, y_rows, route, gain, tm):
    t, d = h.shape
    return pl.pallas_call(
        _final_kernel,
        grid=(t // tm,),
        in_specs=[pl.BlockSpec((tm, d), lambda i: (i, 0)),
                  pl.BlockSpec((TOP_K, tm, d // 2), lambda i: (0, i, 0)),
                  pl.BlockSpec((tm, LANES), lambda i: (i, 0)),
                  pl.BlockSpec((1, d), lambda i: (0, 0))],
        out_specs=pl.BlockSpec((tm, d), lambda i: (i, 0)),
        out_shape=jax.ShapeDtypeStruct((t, d), F32),
        compiler_params=_cparams(("arbitrary",)),
        name="final",
    )(h, y_rows, route, gain.reshape(1, d))


def _layer(h3, mix_gain, w_in, attn_gain, decay_f, decay_b, ret_gain, w_out, ffn_gain,
           w_rg, b_rg, w_re, b_re, w_eg, w_eu, w_ed, final_gain):
    b, s, d = h3.shape
    t = b * s
    tm = 512
    attn_qkv, (qr, kr, vr, gr) = _inproj(h3, mix_gain, w_in.astype(BF16), 1024)

    o_list, lse_list = [], []
    for window, dilation in DILATED_BRANCHES:
        reach = (window // 2) // dilation
        o, lse = _attn_branch(*attn_qkv[dilation], dilation, reach, tq=128, tqb=1024, sub_tiles=8)
        o_list.append(o)
        lse_list.append(lse)

    lg_f = jnp.log1p(-jnp.exp2(decay_f.astype(F32)))
    lg_b = jnp.log1p(-jnp.exp2(decay_b.astype(F32)))
    orr = _retention(qr, kr, vr, gr, lg_f, lg_b, ret_gain).reshape(t, HEAD_GROUP_WIDTH)

    n_route = MOE_GROUPS + N_EXPERTS
    w_router = jnp.zeros((d, LANES), F32).at[:, :n_route].set(jnp.concatenate([w_re, w_rg], axis=1).astype(F32))
    b_router = jnp.zeros((1, LANES), F32).at[0, :n_route].set(jnp.concatenate([b_re, b_rg]).astype(F32))
    h, hn_packed, logits = _outproj(h3.reshape(t, d), o_list, lse_list, orr, attn_gain, w_out.astype(BF16),
                                    ffn_gain, w_router, b_router, 2 * tm)

    route, counts_rep = _route(logits, 2 * tm)

    n_blocks = -(-TOP_K * t // MOE_BLOCK) + N_EXPERTS
    counts = counts_rep[:, 0].astype(jnp.int32)
    aligned = ((counts + SEG_ALIGN - 1) // SEG_ALIGN) * SEG_ALIGN
    seg_start = jnp.cumsum(aligned) - aligned
    expert_iota = jnp.arange(N_EXPERTS, dtype=jnp.int32)
    e12 = route[:, ROUTE_E1:ROUTE_E2 + 1].astype(jnp.int32)
    r12 = route[:, ROUTE_R1:ROUTE_R2 + 1].astype(jnp.int32)
    pos = r12 + jnp.sum(jnp.where(e12[..., None] == expert_iota, seg_start, 0), axis=-1)
    nblk = (counts + MOE_BLOCK - 1) // MOE_BLOCK
    blk_end = jnp.cumsum(nblk)
    blk_start = blk_end - nblk
    blk = jnp.arange(n_blocks, dtype=jnp.int32)[:, None]
    owner = (blk >= blk_start) & (blk < blk_end)
    local = (blk - blk_start) * MOE_BLOCK
    block_row0 = jnp.sum(jnp.where(owner, seg_start + local, 0), axis=-1).astype(jnp.int32)
    block_nreal = jnp.sum(jnp.where(owner, jnp.clip(counts - local, 0, MOE_BLOCK), 0), axis=-1).astype(jnp.int32)
    block_expert = jnp.minimum(jnp.sum((blk >= blk_end).astype(jnp.int32), axis=-1), N_EXPERTS - 1)

    n_sorted = TOP_K * t + SORTED_TAIL
    dump_row = n_sorted
    seg_end = seg_start + counts
    hole = seg_end[:, None] + jnp.arange(SEG_ALIGN, dtype=jnp.int32)[None, :]
    hole = jnp.where(hole < (seg_start + aligned)[:, None], hole, dump_row).reshape(-1)
    tail = (seg_start[-1] + aligned[-1]) + jnp.arange(SORTED_TAIL, dtype=jnp.int32)
    tail = jnp.where(tail < n_sorted, tail, dump_row)
    zero_pos = jnp.concatenate([hole, tail]).astype(jnp.int32)

    xs = _dispatch(hn_packed, pos, zero_pos)
    ys = _experts(xs, block_expert, block_nreal, block_row0, w_eg.astype(F32), w_eu.astype(F32),
                  w_ed.astype(F32))
    out = _final(h, _combine_gather(ys, pos), route, final_gain, tm)
    return out.reshape(b, s, d)


def kernel(x, mix_norm_gain, w_in, attn_out_gain, ret_decay_fwd, ret_decay_bwd, ret_out_gain, w_out,
           ffn_norm_gain, w_route_group, b_route_group, w_route_expert, b_route_expert,
           w_expert_gate, w_expert_up, w_expert_down, final_norm_gain):
    depth = mix_norm_gain.shape[0]
    assert depth == 1, "the final rmsnorm is fused into the single layer's combine kernel"
    l = 0
    return _layer(x, mix_norm_gain[l], w_in[l], attn_out_gain[l], ret_decay_fwd[l], ret_decay_bwd[l],
                  ret_out_gain[l], w_out[l], ffn_norm_gain[l], w_route_group[l], b_route_group[l],
                  w_route_expert[l], b_route_expert[l], w_expert_gate[l], w_expert_up[l], w_expert_down[l],
                  final_norm_gain)
```

```python
import functools

import numpy as np
import jax
import jax.numpy as jnp
from jax import lax
from jax.experimental import pallas as pl
from jax.experimental.pallas import tpu as pltpu
from jax.experimental.pallas import tpu_sc as plsc

F32 = jnp.float32
BF16 = jnp.bfloat16

ATTN_HEADS = 8
HEAD_DIM = 64
RET_HEADS = 8
HEAD_GROUP_WIDTH = 512
N_PROJ_GROUPS = 7
DILATED_BRANCHES = ((128, 1), (512, 4), (2048, 16))
ROPE_THETA = 500000.0
ROPE_DIM = HEAD_DIM // 4
RET_THETA = 10000.0
RET_CHUNK = 128
MOE_GROUPS = 4
EXPERTS_PER_GROUP = 8
N_EXPERTS = MOE_GROUPS * EXPERTS_PER_GROUP
MOE_BLOCK = 512
NORM_EPS = 1e-6
NEG_INF = -1e30

LANES = 128
VMEM_LIMIT = 56 * 1024 * 1024


def _cparams(sem):
    return pltpu.CompilerParams(dimension_semantics=sem, vmem_limit_bytes=VMEM_LIMIT)


def _rotary_tables(seq, half, freqs):
    pos = np.arange(seq, dtype=np.float64)[:, None]
    ang = pos * freqs[None, :].astype(np.float64)
    cos, sin = np.cos(ang), np.sin(ang)
    c = np.ones((seq, HEAD_DIM)); sp = np.zeros((seq, HEAD_DIM)); sm = np.zeros((seq, HEAD_DIM))
    c[:, :half] = cos; c[:, half:2 * half] = cos
    sp[:, half:2 * half] = sin
    sm[:, :half] = -sin
    rep = LANES // HEAD_DIM
    return tuple(jnp.asarray(np.tile(t, (1, rep)), F32) for t in (c, sp, sm))


def _rotate(t, c, sp, sm, half):
    outs = []
    for g in range(t.shape[1] // LANES):
        tg = t[:, g * LANES:(g + 1) * LANES]
        outs.append(tg * c + pltpu.roll(tg, half, 1) * sp + pltpu.roll(tg, LANES - half, 1) * sm)
    return jnp.concatenate(outs, axis=1)


CLASS_DILATIONS = tuple(d for _, d in DILATED_BRANCHES if d > 1)
ATTN_Q_SCALE = float(np.log2(np.e)) * HEAD_DIM ** -0.5


def _inproj_kernel(x_ref, gain_ref, w_ref, ca_ref, spa_ref, sma_ref, cr_ref, spr_ref, smr_ref, *rest):
    n_cls = len(CLASS_DILATIONS)
    nat_refs = rest[0:3]
    cls_refs = [rest[3 + 3 * c:6 + 3 * c] for c in range(n_cls)]
    qr_ref, kr_ref, vr_ref, gr_ref = rest[3 + 3 * n_cls:7 + 3 * n_cls]
    stage_ref = rest[7 + 3 * n_cls]
    x = x_ref[0]
    tm = x.shape[0]
    ms = jnp.mean(x * x, axis=-1, keepdims=True)
    xn = (x * lax.rsqrt(ms + NORM_EPS) * gain_ref[...]).astype(BF16)
    gw = HEAD_GROUP_WIDTH

    def proj(c):
        return jnp.dot(xn, w_ref[:, c * gw:(c + 1) * gw], preferred_element_type=F32)

    a_tabs = (ca_ref[...], spa_ref[...], sma_ref[...])
    r_tabs = (cr_ref[...], spr_ref[...], smr_ref[...])
    attn_vals = ((_rotate(proj(0), *a_tabs, ROPE_DIM // 2) * ATTN_Q_SCALE),
                 _rotate(proj(1), *a_tabs, ROPE_DIM // 2),
                 proj(2))
    for j, val in enumerate(attn_vals):
        nat_refs[j][0] = val.astype(BF16)
        for g in range(gw // LANES):
            stage_ref[g] = val[:, g * LANES:(g + 1) * LANES]
        for c, d in enumerate(CLASS_DILATIONS):
            for r in range(d):
                for g in range(gw // LANES):
                    col = r * gw + g * LANES
                    cls_refs[c][j][0, :, col:col + LANES] = (
                        stage_ref[g, pl.ds(r, tm // d, stride=d), :].astype(BF16))
    qr_ref[0] = _rotate(proj(3), *r_tabs, HEAD_DIM // 2).astype(BF16)
    kr_ref[0] = (_rotate(proj(4), *r_tabs, HEAD_DIM // 2) * (HEAD_DIM ** -0.5)).astype(BF16)
    vr_ref[0] = proj(5).astype(BF16)
    g = proj(6)
    gr_ref[0] = (g * jax.nn.sigmoid(g)).astype(BF16)


def _inproj(x, gain, w_in_bf16, tm):
    b, s, d = x.shape
    rope_freqs = ROPE_THETA ** (-np.arange(0, ROPE_DIM, 2, dtype=np.float32) / ROPE_DIM)
    ret_freqs = RET_THETA ** (-np.linspace(0.0, 1.0, HEAD_DIM // 2, dtype=np.float32))
    tabs = _rotary_tables(s, ROPE_DIM // 2, rope_freqs) + _rotary_tables(s, HEAD_DIM // 2, ret_freqs)
    gw = HEAD_GROUP_WIDTH
    tab_spec = pl.BlockSpec((tm, LANES), lambda si, bi: (si, 0))

    def view(dil):
        return (pl.BlockSpec((1, tm // dil, dil * gw), lambda si, bi: (bi, si, 0)),
                jax.ShapeDtypeStruct((b, s // dil, dil * gw), BF16))

    views = [view(1)] * 3 + [view(dil) for dil in CLASS_DILATIONS for _ in range(3)] + [view(1)] * 4
    outs = pl.pallas_call(
        _inproj_kernel,
        grid=(s // tm, b),
        in_specs=[pl.BlockSpec((1, tm, d), lambda si, bi: (bi, si, 0)),
                  pl.BlockSpec((1, d), lambda si, bi: (0, 0)),
                  pl.BlockSpec(w_in_bf16.shape, lambda si, bi: (0, 0))] + [tab_spec] * 6,
        out_specs=[v[0] for v in views],
        out_shape=[v[1] for v in views],
        scratch_shapes=[pltpu.VMEM((gw // LANES, tm, LANES), F32)],
        compiler_params=_cparams(("arbitrary", "arbitrary")),
        name="inproj",
    )(x, gain.reshape(1, d), w_in_bf16, *tabs)
    n_attn = 3 * (1 + len(CLASS_DILATIONS))
    attn_qkv = {dil: outs[3 * c:3 * c + 3] for c, dil in enumerate((1,) + CLASS_DILATIONS)}
    return attn_qkv, outs[n_attn:]


ATTN_WINDOW_CASES = 3


def _attn_kernel(q_ref, k_ref, v_ref, o_ref, lse_ref, bias_ref, *, length, tq, reach):
    tqb = q_ref.shape[1]
    win = tq + 2 * reach
    heads_per_pair = LANES // HEAD_DIM
    qi = pl.program_id(2)
    lane = lax.broadcasted_iota(jnp.int32, (1, LANES), 1)
    lane_t = lax.broadcasted_iota(jnp.int32, (tq, LANES), 1)
    ones = jnp.ones((win, LANES), BF16)

    @pl.when((pl.program_id(0) == 0) & (pl.program_id(1) == 0) & (qi == 0))
    def _():
        diff = (lax.broadcasted_iota(jnp.int32, (heads_per_pair * tq, win), 1)
                - (lax.broadcasted_iota(jnp.int32, (heads_per_pair * tq, win), 0) & (tq - 1)))
        for case in range(ATTN_WINDOW_CASES):
            off = case * reach
            bias_ref[case] = jnp.where((diff >= off - reach) & (diff <= off + reach), 0.0, NEG_INF)

    def sub(t, cls):
        q0 = qi * tqb + t * tq
        ws = jnp.clip(q0 - reach, 0, length - win)
        ws = pl.multiple_of(ws, reach)
        bias = bias_ref[(q0 - ws) // reach]
        rows = pl.ds(t * tq, tq)
        m_tile = jnp.zeros((tq, LANES), F32)
        l_tile = jnp.ones((tq, LANES), F32)
        for g in range(HEAD_GROUP_WIDTH // LANES):
            cols = slice(cls * HEAD_GROUP_WIDTH + g * LANES, cls * HEAD_GROUP_WIDTH + (g + 1) * LANES)
            qg = q_ref[0, rows, cols]
            kw = k_ref[0, pl.ds(ws, win), cols]
            v_ones = jnp.concatenate([v_ref[0, pl.ds(ws, win), cols], ones], axis=1)
            hms = [(lane >= hh * HEAD_DIM) & (lane < (hh + 1) * HEAD_DIM) for hh in range(heads_per_pair)]
            q2 = jnp.concatenate([qg * hm.astype(BF16) for hm in hms], axis=0)
            sc = lax.dot_general(q2, kw, (((1,), (1,)), ((), ())), preferred_element_type=F32) + bias
            m = jnp.max(sc, axis=-1, keepdims=True)
            p = jnp.exp2(sc - m).astype(BF16)
            pv = jnp.dot(p, v_ones, preferred_element_type=F32)
            l = pv[:, LANES:]
            o = pv[:, :LANES] / l
            o_pair = jnp.zeros((tq, LANES), F32)
            for hh in range(heads_per_pair):
                part = slice(hh * tq, (hh + 1) * tq)
                head_lane = lane_t == g * heads_per_pair + hh
                o_pair = jnp.where(hms[hh], o[part], o_pair)
                m_tile = jnp.where(head_lane, m[part], m_tile)
                l_tile = jnp.where(head_lane, l[part], l_tile)
            o_ref[0, rows, cols] = o_pair.astype(BF16)
        lse_ref[0, rows, cls * LANES:(cls + 1) * LANES] = m_tile + jnp.log2(l_tile)

    for cls in range(q_ref.shape[2] // HEAD_GROUP_WIDTH):
        for t in range(tqb // tq):
            sub(t, cls)


def _attn_branch(qc, kc, vc, dilation, reach, tq, tqb, sub_tiles=4):
    b, length, dw = qc.shape
    w = dw // dilation
    tqb = min(tqb, length)
    assert tq % reach == 0 and tq > reach and length % tq == 0 and length >= tq + 2 * reach
    ncls = min(dilation, max(1, sub_tiles // (tqb // tq)))
    q_spec = pl.BlockSpec((1, tqb, ncls * w), lambda bi, r, qi: (bi, qi, r))
    kv_spec = pl.BlockSpec((1, length, ncls * w), lambda bi, r, qi: (bi, 0, r))
    o, lse = pl.pallas_call(
        functools.partial(_attn_kernel, length=length, tq=tq, reach=reach),
        grid=(b, dilation // ncls, length // tqb),
        in_specs=[q_spec, kv_spec, kv_spec],
        out_specs=[q_spec, pl.BlockSpec((1, tqb, ncls * LANES), lambda bi, r, qi: (bi, qi, r))],
        out_shape=[jax.ShapeDtypeStruct((b, length, dilation * w), BF16),
                   jax.ShapeDtypeStruct((b, length, dilation * LANES), F32)],
        scratch_shapes=[pltpu.VMEM((ATTN_WINDOW_CASES, (LANES // HEAD_DIM) * tq, tq + 2 * reach), F32)],
        compiler_params=_cparams(("arbitrary", "arbitrary", "arbitrary")),
        name=f"attn_d{dilation}",
    )(qc, kc, vc)
    return o, lse


RET_TAB_QF, RET_TAB_QB, RET_TAB_KF, RET_TAB_KB = range(4)


def _retention_kernel(lgf_ref, lgb_ref, q_ref, k_ref, v_ref, g_ref, gain_ref, o_ref,
                      tab_ref, dec_ref, sb_ref, st_ref, *, chunk, unroll):
    c = chunk
    n = q_ref.shape[1] // c
    width = q_ref.shape[2]
    n_pairs = width // LANES
    heads_per_pair = LANES // HEAD_DIM
    n_heads = n_pairs * heads_per_pair
    head0 = pl.program_id(1) * n_heads
    lane_w = lax.broadcasted_iota(jnp.int32, (1, width), 1)
    lgf = [lgf_ref[head0 + hd] for hd in range(n_heads)]
    lgb = [lgb_ref[head0 + hd] for hd in range(n_heads)]
    lgf_lane = jnp.zeros((1, width), F32)
    lgb_lane = jnp.zeros((1, width), F32)
    for hd in range(n_heads):
        in_head = (lane_w >= hd * HEAD_DIM) & (lane_w < (hd + 1) * HEAD_DIM)
        lgf_lane = jnp.where(in_head, lgf[hd], lgf_lane)
        lgb_lane = jnp.where(in_head, lgb[hd], lgb_lane)
    idx = lax.broadcasted_iota(jnp.int32, (c, width), 0).astype(F32)
    tab_ref[RET_TAB_QF] = jnp.exp((idx + 1.0) * lgf_lane)
    tab_ref[RET_TAB_QB] = jnp.exp((c - idx) * lgb_lane)
    tab_ref[RET_TAB_KF] = jnp.exp((c - 1.0 - idx) * lgf_lane)
    tab_ref[RET_TAB_KB] = jnp.exp(idx * lgb_lane)
    sdf = jnp.exp(c * lgf_lane)
    sdb = jnp.exp(c * lgb_lane)
    dmat = (lax.broadcasted_iota(jnp.int32, (c, c), 0)
            - lax.broadcasted_iota(jnp.int32, (c, c), 1)).astype(F32)
    for hd in range(n_heads):
        dec_ref[hd // heads_per_pair, :, (hd % heads_per_pair) * c:(hd % heads_per_pair + 1) * c] = (
            jnp.where(dmat >= 0, jnp.exp(dmat * lgf[hd]), jnp.exp(-dmat * lgb[hd])))
    lane = lax.broadcasted_iota(jnp.int32, (1, LANES), 1)
    lane_hi = lane >= HEAD_DIM
    head_masks = [((lane >= hh * HEAD_DIM) & (lane < (hh + 1) * HEAD_DIM)).astype(BF16)
                  for hh in range(heads_per_pair)]
    row_hi = lax.broadcasted_iota(jnp.int32, (LANES, LANES), 0) >= HEAD_DIM
    col_hi = lax.broadcasted_iota(jnp.int32, (LANES, LANES), 1) >= HEAD_DIM
    blockdiag = row_hi == col_hi

    def kv_state(kd, vv):
        kt = jnp.transpose(kd).astype(BF16)
        return jnp.where(blockdiag, jnp.dot(kt, vv, preferred_element_type=F32), 0.0)

    st_ref[...] = jnp.zeros_like(st_ref)
    sb_ref[n - 1] = jnp.zeros(sb_ref.shape[1:], sb_ref.dtype)

    def back(i, carry):
        nn = n - 1 - i
        rows = pl.ds(pl.multiple_of(nn * c, c), c)
        for p in range(n_pairs):
            cols = slice(p * LANES, (p + 1) * LANES)
            kd = k_ref[0, rows, cols].astype(F32) * tab_ref[RET_TAB_KB, :, cols]
            new = st_ref[p] * sdb[:, cols] + kv_state(kd, v_ref[0, rows, cols])
            st_ref[p] = new
            sb_ref[nn - 1, p] = new.astype(BF16)
        return carry

    lax.fori_loop(0, n - 1, back, 0, unroll=unroll)

    st_ref[...] = jnp.zeros_like(st_ref)

    def fwd(nn, carry):
        rows = pl.ds(pl.multiple_of(nn * c, c), c)
        for p in range(n_pairs):
            cols = slice(p * LANES, (p + 1) * LANES)
            qq = q_ref[0, rows, cols]
            kk = k_ref[0, rows, cols]
            vv = v_ref[0, rows, cols]
            qf = qq.astype(F32)
            sf = st_ref[p]
            qcat = jnp.concatenate([(qf * tab_ref[RET_TAB_QF, :, cols]).astype(BF16),
                                    (qf * tab_ref[RET_TAB_QB, :, cols]).astype(BF16)], axis=1)
            scat = jnp.concatenate([sf.astype(BF16), sb_ref[nn, p]], axis=0)
            o = jnp.dot(qcat, scat, preferred_element_type=F32)
            k2 = jnp.concatenate([kk * hm for hm in head_masks], axis=0)
            v2 = jnp.concatenate([vv * hm for hm in head_masks], axis=0)
            sc = lax.dot_general(qq, k2, (((1,), (1,)), ((), ())), preferred_element_type=F32)
            o = o + jnp.dot((sc * dec_ref[p]).astype(BF16), v2, preferred_element_type=F32)
            s_lo = jnp.sum(jnp.where(lane_hi, 0.0, o), axis=-1, keepdims=True)
            s_hi = jnp.sum(jnp.where(lane_hi, o, 0.0), axis=-1, keepdims=True)
            mu = jnp.where(lane_hi, s_hi, s_lo) * (1.0 / HEAD_DIM)
            dev = o - mu
            d2 = dev * dev
            v_lo = jnp.sum(jnp.where(lane_hi, 0.0, d2), axis=-1, keepdims=True)
            v_hi = jnp.sum(jnp.where(lane_hi, d2, 0.0), axis=-1, keepdims=True)
            var = jnp.where(lane_hi, v_hi, v_lo) * (1.0 / HEAD_DIM)
            out = dev * lax.rsqrt(var + NORM_EPS) * gain_ref[:, cols] * g_ref[0, rows, cols].astype(F32)
            o_ref[0, rows, cols] = out.astype(BF16)
            kd = kk.astype(F32) * tab_ref[RET_TAB_KF, :, cols]
            st_ref[p] = sf * sdf[:, cols] + kv_state(kd, vv)
        return carry

    lax.fori_loop(0, n, fwd, 0, unroll=unroll)


def _retention(qr, kr, vr, gate, lg_f, lg_b, out_gain, width=512, unroll=4):
    b, s, w = qr.shape
    n_pairs = width // LANES
    n_heads = width // HEAD_DIM
    spec = pl.BlockSpec((1, s, width), lambda bi, p, *_: (bi, 0, p))
    grid_spec = pltpu.PrefetchScalarGridSpec(
        num_scalar_prefetch=2,
        grid=(b, w // width),
        in_specs=[spec, spec, spec, spec, pl.BlockSpec((1, width), lambda bi, p, *_: (0, p))],
        out_specs=spec,
        scratch_shapes=[pltpu.VMEM((4, RET_CHUNK, width), F32),
                        pltpu.VMEM((n_pairs, RET_CHUNK, (LANES // HEAD_DIM) * RET_CHUNK), F32),
                        pltpu.VMEM((s // RET_CHUNK, n_pairs, LANES, LANES), BF16),
                        pltpu.VMEM((n_pairs, LANES, LANES), F32)],
    )
    return pl.pallas_call(
        functools.partial(_retention_kernel, chunk=RET_CHUNK, unroll=unroll),
        grid_spec=grid_spec,
        out_shape=jax.ShapeDtypeStruct((b, s, w), BF16),
        compiler_params=_cparams(("arbitrary", "arbitrary")),
        name="retention",
    )(lg_f, lg_b, qr, kr, vr, gate, out_gain.reshape(1, w))


def _split_bf16(t):
    hi = t.astype(BF16)
    lo = (t - hi.astype(F32)).astype(BF16)
    return hi, lo


def _pack_bf16_pairs(t):
    n = t.shape[1] // 2
    hi = pltpu.bitcast(t[:, :n].astype(BF16).astype(F32), jnp.uint32)
    lo = pltpu.bitcast(t[:, n:].astype(BF16).astype(F32), jnp.uint32)
    return hi | (lo >> 16)


def _unpack_bf16_pairs(u):
    hi = pltpu.bitcast(u & jnp.uint32(0xFFFF0000), F32)
    lo = pltpu.bitcast(u << 16, F32)
    return jnp.concatenate([hi, lo], axis=1)


def _outproj_kernel(x_ref, o1_ref, o2_ref, o3_ref, l1_ref, l2_ref, l3_ref, orr_ref, ga_ref, expand_ref,
                    wout_ref, gf_ref, wr_ref, br_ref, h_ref, hn_ref, logit_ref,
                    *nat_refs):
    tm = x_ref.shape[0]
    gw = HEAD_GROUP_WIDTH
    os, ls = [], []
    for (_, dil), o_ref, l_ref in zip(DILATED_BRANCHES, (o1_ref, o2_ref, o3_ref), (l1_ref, l2_ref, l3_ref)):
        if dil == 1:
            os.append(o_ref[...].astype(F32))
            ls.append(l_ref[...])
            continue
        c = CLASS_DILATIONS.index(dil)
        onat_ref, lnat_ref = nat_refs[2 * c], nat_refs[2 * c + 1]
        for r in range(dil):
            rows = pl.ds(r, tm // dil, stride=dil)
            for g in range(gw // LANES):
                col = r * gw + g * LANES
                onat_ref[g, rows, :] = o_ref[:, col:col + LANES].astype(F32)
            lnat_ref[rows, :] = l_ref[:, r * LANES:(r + 1) * LANES]
        os.append(jnp.concatenate([onat_ref[g] for g in range(gw // LANES)], axis=1))
        ls.append(lnat_ref[...])
    mx = jnp.maximum(jnp.maximum(ls[0], ls[1]), ls[2])
    es = [jnp.exp2(l - mx) for l in ls]
    inv = 1.0 / (es[0] + es[1] + es[2])
    expand = expand_ref[...]
    oa = jnp.zeros((tm, gw), F32)
    for e, o in zip(es, os):
        wexp = jnp.dot(jnp.concatenate(_split_bf16(e * inv), axis=1), expand, preferred_element_type=F32)
        oa = oa + wexp * o
    oa = oa * lax.rsqrt(jnp.mean(oa * oa, axis=-1, keepdims=True) + NORM_EPS) * ga_ref[...]
    mixed = jnp.concatenate([oa.astype(BF16), orr_ref[...]], axis=1)
    h = x_ref[...] + jnp.dot(mixed, wout_ref[...], preferred_element_type=F32)
    h_ref[...] = h
    hn = h * lax.rsqrt(jnp.mean(h * h, axis=-1, keepdims=True) + NORM_EPS) * gf_ref[...]
    hn_ref[...] = _pack_bf16_pairs(hn)
    prod = jnp.dot(jnp.concatenate(_split_bf16(hn), axis=0), wr_ref[...], preferred_element_type=F32)
    logit_ref[...] = prod[:tm, :LANES] + prod[:tm, LANES:] + prod[tm:, :LANES] + br_ref[...]


def _outproj(x2, o_list, lse_list, orr, attn_gain, w_out_bf16, ffn_gain, w_router, b_router, tm):
    t, d = x2.shape
    w = HEAD_GROUP_WIDTH
    expand = np.zeros((LANES, w), np.float32)
    for hd in range(ATTN_HEADS):
        expand[hd, hd * HEAD_DIM:(hd + 1) * HEAD_DIM] = 1.0
    expand = jnp.asarray(np.concatenate([expand, expand], axis=0), BF16)
    wr_hi = w_router.astype(BF16)
    wr = jnp.concatenate([wr_hi, (w_router - wr_hi.astype(F32)).astype(BF16)], axis=1)
    row = lambda width, dil=1: pl.BlockSpec((tm // dil, dil * width), lambda i: (i, 0))
    full = lambda a: pl.BlockSpec(a.shape, lambda i: (0,) * a.ndim)
    ga = attn_gain.reshape(1, w)
    gf = ffn_gain.reshape(1, d)
    dils = [dil for _, dil in DILATED_BRANCHES]
    o_flat = [o.reshape(t // dil, dil * w) for o, dil in zip(o_list, dils)]
    l_flat = [l.reshape(t // dil, dil * LANES) for l, dil in zip(lse_list, dils)]
    nat_scratch = []
    for _ in CLASS_DILATIONS:
        nat_scratch += [pltpu.VMEM((w // LANES, tm, LANES), F32), pltpu.VMEM((tm, LANES), F32)]
    return pl.pallas_call(
        _outproj_kernel,
        grid=(t // tm,),
        in_specs=[row(d)] + [row(w, dil) for dil in dils] + [row(LANES, dil) for dil in dils] + [row(w)]
                 + [full(ga), full(expand), full(w_out_bf16), full(gf), full(wr), full(b_router)],
        out_specs=[row(d), row(d // 2), row(LANES)],
        out_shape=[jax.ShapeDtypeStruct((t, d), F32),
                   jax.ShapeDtypeStruct((t, d // 2), jnp.uint32),
                   jax.ShapeDtypeStruct((t, LANES), F32)],
        scratch_shapes=nat_scratch,
        compiler_params=_cparams(("arbitrary",)),
        name="outproj",
    )(x2, *o_flat, *l_flat, orr, ga, expand, w_out_bf16, gf, wr, b_router)


ROUTE_E1, ROUTE_E2, ROUTE_G1, ROUTE_G2, ROUTE_R1, ROUTE_R2 = range(6)
GROUP_LANE0 = N_EXPERTS
SUBLANES = 8


def _route_kernel(logit_ref, tri_ref, route_ref, route_t_ref, count_ref, run_ref):
    @pl.when(pl.program_id(0) == 0)
    def _():
        run_ref[...] = jnp.zeros_like(run_ref)

    lg_t = jnp.transpose(logit_ref[...])
    tm = lg_t.shape[1]
    assert EXPERTS_PER_GROUP == SUBLANES and MOE_GROUPS <= SUBLANES
    rid = lax.broadcasted_iota(jnp.int32, (SUBLANES, tm), 0)
    big = jnp.int32(1 << 20)

    def top(vals):
        m = jnp.max(vals, axis=0, keepdims=True)
        i = jnp.min(jnp.where(vals == m, rid, big), axis=0, keepdims=True)
        return m, i

    gl = jnp.where(rid < MOE_GROUPS, lg_t[GROUP_LANE0:GROUP_LANE0 + SUBLANES], -jnp.inf)
    gmax, gidx = top(gl)
    group_gate = 1.0 / jnp.sum(jnp.exp(gl - gmax), axis=0, keepdims=True)
    el = lg_t[0:EXPERTS_PER_GROUP]
    for g in range(1, MOE_GROUPS):
        el = jnp.where(gidx == g, lg_t[g * EXPERTS_PER_GROUP:(g + 1) * EXPERTS_PER_GROUP], el)
    t1, i1 = top(el)
    t2, i2 = top(jnp.where(rid == i1, -jnp.inf, el))
    e21 = jnp.exp(t2 - t1)
    g1 = group_gate / (1.0 + e21)
    g2 = group_gate * e21 / (1.0 + e21)
    e1 = gidx * EXPERTS_PER_GROUP + i1
    e2 = gidx * EXPERTS_PER_GROUP + i2
    erow = lax.broadcasted_iota(jnp.int32, (N_EXPERTS, tm), 0)
    oh1 = erow == e1
    oh2 = erow == e2
    cnt = oh1.astype(F32) + oh2.astype(F32)
    run = run_ref[:, 0:1]
    prefix = jnp.dot(cnt.astype(BF16), tri_ref[...], preferred_element_type=F32) + run
    r1 = jnp.sum(jnp.where(oh1, prefix, 0.0), axis=0, keepdims=True)
    r2 = jnp.sum(jnp.where(oh2, prefix, 0.0), axis=0, keepdims=True)
    new_run = jnp.broadcast_to(run + jnp.sum(cnt, axis=1, keepdims=True), run_ref.shape)
    run_ref[...] = new_run
    count_ref[...] = new_run
    out = jnp.zeros((SUBLANES, tm), F32)
    for row, val in ((ROUTE_E1, e1.astype(F32)), (ROUTE_E2, e2.astype(F32)), (ROUTE_G1, g1),
                     (ROUTE_G2, g2), (ROUTE_R1, r1), (ROUTE_R2, r2)):
        out = jnp.where(rid == row, val, out)
    route_t_ref[0] = out
    out_t = jnp.concatenate([out, jnp.zeros((LANES - SUBLANES, tm), F32)], axis=0)
    route_ref[...] = jnp.transpose(out_t)


def _route(logits, tm):
    t = logits.shape[0]
    tri = jnp.asarray(np.triu(np.ones((tm, tm), np.float32), 1), BF16)
    return pl.pallas_call(
        _route_kernel,
        grid=(t // tm,),
        in_specs=[pl.BlockSpec((tm, LANES), lambda i: (i, 0)), pl.BlockSpec((tm, tm), lambda i: (0, 0))],
        out_specs=[pl.BlockSpec((tm, LANES), lambda i: (i, 0)),
                   pl.BlockSpec((1, SUBLANES, tm), lambda i: (i, 0, 0)),
                   pl.BlockSpec((N_EXPERTS, LANES), lambda i: (0, 0))],
        out_shape=[jax.ShapeDtypeStruct((t, LANES), F32),
                   jax.ShapeDtypeStruct((t // tm, SUBLANES, tm), F32),
                   jax.ShapeDtypeStruct((N_EXPERTS, LANES), F32)],
        scratch_shapes=[pltpu.VMEM((N_EXPERTS, LANES), F32)],
        compiler_params=_cparams(("arbitrary",)),
        name="route",
    )(logits, tri)


TOP_K = 2


SEG_ALIGN = 8
SORTED_TAIL = N_EXPERTS * SEG_ALIGN + MOE_BLOCK


def _tail_pieces():
    full, rest = divmod(SORTED_TAIL, MOE_BLOCK)
    return [MOE_BLOCK] * full + ([rest] if rest else [])


SC_CHUNK = 64
SC_BUFFERS = 3
ZERO_ROWS = N_EXPERTS * SEG_ALIGN + SORTED_TAIL
SORTED_ROWS_EXTRA = SORTED_TAIL + SEG_ALIGN


def _sc_workers():
    info = plsc.get_sparse_core_info()
    return info.num_cores, info.num_cores * info.num_subcores


def _dispatch(hn_packed, pos, zero_pos):
    t, dp = hn_packed.shape
    ncores, nw = _sc_workers()
    per_w = t // nw
    n_ch = per_w // SC_CHUNK
    z_rows = ZERO_ROWS // nw
    assert per_w % SC_CHUNK == 0 and ZERO_ROWS % nw == 0 and z_rows % SEG_ALIGN == 0 and TOP_K == 2
    idx = [pos[k].reshape(nw, n_ch, SC_CHUNK) for k in range(TOP_K)]
    zeros = jnp.zeros((z_rows, dp), hn_packed.dtype)
    mesh = plsc.VectorSubcoreMesh(core_axis_name="c", subcore_axis_name="s")

    @functools.partial(
        pl.kernel, mesh=mesh,
        out_type=jax.ShapeDtypeStruct((TOP_K * t + SORTED_ROWS_EXTRA, dp), hn_packed.dtype),
        scratch_types=[pltpu.VMEM((n_ch, SC_CHUNK), jnp.int32), pltpu.VMEM((n_ch, SC_CHUNK), jnp.int32),
                       pltpu.VMEM((z_rows,), jnp.int32),
                       pltpu.VMEM((SC_BUFFERS, SC_CHUNK, dp), hn_packed.dtype),
                       pltpu.VMEM((z_rows, dp), hn_packed.dtype),
                       pltpu.SemaphoreType.DMA((SC_BUFFERS,)), pltpu.SemaphoreType.DMA((SC_BUFFERS,)),
                       pltpu.SemaphoreType.DMA],
    )
    def scatter(hn_hbm, p0_hbm, p1_hbm, zpos_hbm, zeros_hbm, xs_hbm, i0_v, i1_v, iz_v, rows_v, zero_v,
                lsem, ssem, zsem):
        wid = lax.axis_index("s") * ncores + lax.axis_index("c")
        base = wid * per_w
        zero_load = pltpu.async_copy(zeros_hbm, zero_v, zsem)
        pltpu.sync_copy(zpos_hbm.at[wid], iz_v)
        pltpu.sync_copy(p0_hbm.at[wid], i0_v)
        pltpu.sync_copy(p1_hbm.at[wid], i1_v)

        def load(c):
            return pltpu.async_copy(hn_hbm.at[pl.ds(base + c * SC_CHUNK, SC_CHUNK)], rows_v.at[c % SC_BUFFERS],
                                    lsem.at[c % SC_BUFFERS])

        loads = {c: load(c) for c in range(min(SC_BUFFERS - 1, n_ch))}
        scat = {}
        for c in range(n_ch):
            slot = c % SC_BUFFERS
            loads[c].wait()
            scat[c] = (pltpu.async_copy(rows_v.at[slot], xs_hbm.at[i0_v.at[c]], ssem.at[slot]),
                       pltpu.async_copy(rows_v.at[slot], xs_hbm.at[i1_v.at[c]], ssem.at[slot]))
            if c >= 1:
                for d in scat[c - 1]:
                    d.wait()
            if c + SC_BUFFERS - 1 < n_ch:
                loads[c + SC_BUFFERS - 1] = load(c + SC_BUFFERS - 1)
        zero_load.wait()
        zero_scatter = pltpu.async_copy(zero_v, xs_hbm.at[iz_v], zsem)
        for d in scat[n_ch - 1]:
            d.wait()
        zero_scatter.wait()

    return scatter(hn_packed, idx[0], idx[1], zero_pos.reshape(nw, z_rows), zeros)


def _expert_kernel(bexp_ref, nreal_ref, row0_ref, bnext_ref, xs_hbm, wg_hbm, wu_hbm, wd_hbm, ys_hbm,
                   xbuf, ybuf, wg_f32, wu_f32, wd_f32, wg_bf, wu_bf, wd_bf, isem, osem, wsem, *, n_rows):
    i = pl.program_id(0)
    nb = pl.num_programs(0)
    slot = i % 2
    nslot = 1 - slot
    n_cur = nreal_ref[i]
    prev = jnp.maximum(i - 1, 0)
    nxt = jnp.minimum(i + 1, nb - 1)

    def in_copy(blk, s):
        row0 = pl.multiple_of(row0_ref[blk], SEG_ALIGN)
        return pltpu.make_async_copy(xs_hbm.at[pl.ds(row0, MOE_BLOCK)], xbuf.at[s], isem.at[s])

    def out_copy(blk, s):
        row0 = pl.multiple_of(row0_ref[blk], SEG_ALIGN)
        return pltpu.make_async_copy(ybuf.at[s], ys_hbm.at[pl.ds(row0, MOE_BLOCK)], osem.at[s])

    @pl.when(i == 0)
    def _():
        ybuf[...] = jnp.zeros_like(ybuf)
        tails = [pltpu.make_async_copy(ybuf.at[s, pl.ds(0, size)], ys_hbm.at[pl.ds(n_rows + s * MOE_BLOCK, size)],
                                       osem.at[s])
                 for s, size in enumerate(_tail_pieces())]
        for tail in tails:
            tail.start()
        for tail in tails:
            tail.wait()

        @pl.when(n_cur > 0)
        def _():
            in_copy(i, slot).start()

    @pl.when((i + 1 < nb) & (nreal_ref[nxt] > 0))
    def _():
        in_copy(nxt, nslot).start()

    def weight_copies(e):
        return [pltpu.make_async_copy(w_hbm.at[e], w_f32, wsem.at[j])
                for j, (w_hbm, w_f32) in enumerate(((wg_hbm, wg_f32), (wu_hbm, wu_f32), (wd_hbm, wd_f32)))]

    @pl.when((i == 0) & (n_cur > 0))
    def _():
        for c in weight_copies(bexp_ref[0]):
            c.start()

    @pl.when((n_cur > 0) & ((i == 0) | (bexp_ref[i] != bexp_ref[prev])))
    def _():
        for c in weight_copies(bexp_ref[i]):
            c.wait()
        wg_bf[...] = wg_f32[...].astype(BF16)
        wu_bf[...] = wu_f32[...].astype(BF16)
        wd_bf[...] = wd_f32[...].astype(BF16)

        @pl.when(bnext_ref[i] >= 0)
        def _():
            for c in weight_copies(bnext_ref[i]):
                c.start()

    @pl.when(n_cur > 0)
    def _():
        in_copy(i, slot).wait()
        xb = _unpack_bf16_pairs(xbuf[slot]).astype(BF16)
        gate = jnp.dot(xb, wg_bf[...], preferred_element_type=F32)
        up = jnp.dot(xb, wu_bf[...], preferred_element_type=F32)
        hid = (gate * jax.nn.sigmoid(gate) * up).astype(BF16)
        ybuf[slot] = _pack_bf16_pairs(jnp.dot(hid, wd_bf[...], preferred_element_type=F32))

    @pl.when((i >= 1) & (nreal_ref[prev] > 0))
    def _():
        out_copy(prev, nslot).wait()

    @pl.when(n_cur > 0)
    def _():
        out_copy(i, slot).start()

        @pl.when(i == nb - 1)
        def _():
            out_copy(i, slot).wait()


def _experts(xs, block_expert, block_nreal, block_row0, block_next, wg, wu, wd):
    n_blocks = block_expert.shape[0]
    assert n_blocks >= 2 and len(_tail_pieces()) <= 2
    n_rows_pad = xs.shape[0] - SORTED_ROWS_EXTRA + SORTED_TAIL
    dp = xs.shape[1]
    _, d, ff = wg.shape
    any_spec = pl.BlockSpec(memory_space=pl.ANY)
    grid_spec = pltpu.PrefetchScalarGridSpec(
        num_scalar_prefetch=4,
        grid=(n_blocks,),
        in_specs=[any_spec, any_spec, any_spec, any_spec],
        out_specs=any_spec,
        scratch_shapes=[pltpu.VMEM((2, MOE_BLOCK, dp), jnp.uint32),
                        pltpu.VMEM((2, MOE_BLOCK, dp), jnp.uint32),
                        pltpu.VMEM((d, ff), F32),
                        pltpu.VMEM((d, ff), F32),
                        pltpu.VMEM((ff, d), F32),
                        pltpu.VMEM((d, ff), BF16),
                        pltpu.VMEM((d, ff), BF16),
                        pltpu.VMEM((ff, d), BF16),
                        pltpu.SemaphoreType.DMA((2,)),
                        pltpu.SemaphoreType.DMA((2,)),
                        pltpu.SemaphoreType.DMA((3,))],
    )
    return pl.pallas_call(
        functools.partial(_expert_kernel, n_rows=n_rows_pad - SORTED_TAIL),
        grid_spec=grid_spec,
        out_shape=jax.ShapeDtypeStruct((n_rows_pad, dp), jnp.uint32),
        compiler_params=_cparams(("arbitrary",)),
        name="experts",
    )(block_expert, block_nreal, block_row0, block_next, xs, wg, wu, wd)


def _combine_gather(ys, pos):
    t = pos.shape[1]
    dp = ys.shape[1]
    ncores, nw = _sc_workers()
    per_w = t // nw
    n_ch = per_w // SC_CHUNK
    assert per_w % SC_CHUNK == 0
    idx = [pos[k].reshape(nw, n_ch, SC_CHUNK) for k in range(TOP_K)]
    mesh = plsc.VectorSubcoreMesh(core_axis_name="c", subcore_axis_name="s")

    @functools.partial(
        pl.kernel, mesh=mesh,
        out_type=jax.ShapeDtypeStruct((TOP_K, t, dp), ys.dtype),
        scratch_types=[pltpu.VMEM((TOP_K, n_ch, SC_CHUNK), jnp.int32),
                       pltpu.VMEM((SC_BUFFERS, SC_CHUNK, dp), ys.dtype),
                       pltpu.SemaphoreType.DMA((SC_BUFFERS,)), pltpu.SemaphoreType.DMA((SC_BUFFERS,))],
    )
    def gather(ys_hbm, p0_hbm, p1_hbm, out_hbm, idx_v, rows_v, gsem, wsem):
        wid = lax.axis_index("s") * ncores + lax.axis_index("c")
        base = wid * per_w
        pltpu.sync_copy(p0_hbm.at[wid], idx_v.at[0])
        pltpu.sync_copy(p1_hbm.at[wid], idx_v.at[1])
        units = [(c, k) for c in range(n_ch) for k in range(TOP_K)]

        def fetch(u):
            c, k = units[u]
            return pltpu.async_copy(ys_hbm.at[idx_v.at[k, c]], rows_v.at[u % SC_BUFFERS], gsem.at[u % SC_BUFFERS])

        fetches = {u: fetch(u) for u in range(min(SC_BUFFERS - 1, len(units)))}
        writes = {}
        for u, (c, k) in enumerate(units):
            fetches[u].wait()
            writes[u] = pltpu.async_copy(rows_v.at[u % SC_BUFFERS],
                                         out_hbm.at[k, pl.ds(base + c * SC_CHUNK, SC_CHUNK)],
                                         wsem.at[u % SC_BUFFERS])
            if u >= 1:
                writes[u - 1].wait()
            if u + SC_BUFFERS - 1 < len(units):
                fetches[u + SC_BUFFERS - 1] = fetch(u + SC_BUFFERS - 1)
        writes[len(units) - 1].wait()

    return gather(ys, idx[0], idx[1])


def _final_kernel(h_ref, y_ref, route_ref, gain_ref, out_ref):
    r = route_ref[...]
    y = h_ref[...]
    for k, gate_lane in enumerate((ROUTE_G1, ROUTE_G2)):
        y = y + r[:, gate_lane:gate_lane + 1] * _unpack_bf16_pairs(y_ref[k])
    out_ref[...] = y * lax.rsqrt(jnp.mean(y * y, axis=-1, keepdims=True) + NORM_EPS) * gain_ref[...]


def _final(h, y_rows, route, gain, tm):
    t, d = h.shape
    return pl.pallas_call(
        _final_kernel,
        grid=(t // tm,),
        in_specs=[pl.BlockSpec((tm, d), lambda i: (i, 0)),
                  pl.BlockSpec((TOP_K, tm, d // 2), lambda i: (0, i, 0)),
                  pl.BlockSpec((tm, LANES), lambda i: (i, 0)),
                  pl.BlockSpec((1, d), lambda i: (0, 0))],
        out_specs=pl.BlockSpec((tm, d), lambda i: (i, 0)),
        out_shape=jax.ShapeDtypeStruct((t, d), F32),
        compiler_params=_cparams(("arbitrary",)),
        name="final",
    )(h, y_rows, route, gain.reshape(1, d))


def _layer(h3, mix_gain, w_in, attn_gain, decay_f, decay_b, ret_gain, w_out, ffn_gain,
           w_rg, b_rg, w_re, b_re, w_eg, w_eu, w_ed, final_gain):
    b, s, d = h3.shape
    t = b * s
    tm = 512
    attn_qkv, (qr, kr, vr, gr) = _inproj(h3, mix_gain, w_in.astype(BF16), 1024)

    o_list, lse_list = [], []
    for window, dilation in DILATED_BRANCHES:
        reach = (window // 2) // dilation
        o, lse = _attn_branch(*attn_qkv[dilation], dilation, reach, tq=128, tqb=1024, sub_tiles=8)
        o_list.append(o)
        lse_list.append(lse)

    lg_f = jnp.log1p(-jnp.exp2(decay_f.astype(F32)))
    lg_b = jnp.log1p(-jnp.exp2(decay_b.astype(F32)))
    orr = _retention(qr, kr, vr, gr, lg_f, lg_b, ret_gain).reshape(t, HEAD_GROUP_WIDTH)

    n_route = MOE_GROUPS + N_EXPERTS
    w_router = jnp.zeros((d, LANES), F32).at[:, :n_route].set(jnp.concatenate([w_re, w_rg], axis=1).astype(F32))
    b_router = jnp.zeros((1, LANES), F32).at[0, :n_route].set(jnp.concatenate([b_re, b_rg]).astype(F32))
    h, hn_packed, logits = _outproj(h3.reshape(t, d), o_list, lse_list, orr, attn_gain, w_out.astype(BF16),
                                    ffn_gain, w_router, b_router, 2 * tm)

    route, route_t, counts_rep = _route(logits, 2 * tm)

    n_blocks = -(-TOP_K * t // MOE_BLOCK) + N_EXPERTS
    counts = counts_rep[:, 0].astype(jnp.int32)
    aligned = ((counts + SEG_ALIGN - 1) // SEG_ALIGN) * SEG_ALIGN
    seg_start = jnp.cumsum(aligned) - aligned
    fields = jnp.transpose(route_t, (1, 0, 2)).reshape(SUBLANES, t)
    e12 = fields[ROUTE_E1:ROUTE_E2 + 1].astype(jnp.int32)
    r12 = fields[ROUTE_R1:ROUTE_R2 + 1].astype(jnp.int32)
    pos = r12 + jnp.sum(jnp.where(e12[None] == jnp.arange(N_EXPERTS, dtype=jnp.int32)[:, None, None],
                                  seg_start[:, None, None], 0), axis=0)
    nblk = (counts + MOE_BLOCK - 1) // MOE_BLOCK
    blk_end = jnp.cumsum(nblk)
    blk_start = blk_end - nblk
    blk = jnp.arange(n_blocks, dtype=jnp.int32)[:, None]
    owner = (blk >= blk_start) & (blk < blk_end)
    local = (blk - blk_start) * MOE_BLOCK
    block_row0 = jnp.sum(jnp.where(owner, seg_start + local, 0), axis=-1).astype(jnp.int32)
    block_nreal = jnp.sum(jnp.where(owner, jnp.clip(counts - local, 0, MOE_BLOCK), 0), axis=-1).astype(jnp.int32)
    block_expert = jnp.minimum(jnp.sum((blk >= blk_end).astype(jnp.int32), axis=-1), N_EXPERTS - 1)
    e_iota = jnp.arange(N_EXPERTS, dtype=jnp.int32)
    later = (e_iota[None, :] > e_iota[:, None]) & (counts[None, :] > 0)
    next_expert = jnp.where(jnp.any(later, axis=1), jnp.min(jnp.where(later, e_iota[None, :], N_EXPERTS), axis=1), -1)
    block_next = jnp.sum(jnp.where(owner, next_expert, 0), axis=-1).astype(jnp.int32)

    n_sorted = TOP_K * t + SORTED_TAIL
    dump_row = n_sorted
    seg_end = seg_start + counts
    hole = seg_end[:, None] + jnp.arange(SEG_ALIGN, dtype=jnp.int32)[None, :]
    hole = jnp.where(hole < (seg_start + aligned)[:, None], hole, dump_row).reshape(-1)
    tail = (seg_start[-1] + aligned[-1]) + jnp.arange(SORTED_TAIL, dtype=jnp.int32)
    tail = jnp.where(tail < n_sorted, tail, dump_row)
    zero_pos = jnp.concatenate([hole, tail]).astype(jnp.int32)

    xs = _dispatch(hn_packed, pos, zero_pos)
    ys = _experts(xs, block_expert, block_nreal, block_row0, block_next, w_eg.astype(F32), w_eu.astype(F32),
                  w_ed.astype(F32))
    out = _final(h, _combine_gather(ys, pos), route, final_gain, tm)
    return out.reshape(b, s, d)


def kernel(x, mix_norm_gain, w_in, attn_out_gain, ret_decay_fwd, ret_decay_bwd, ret_out_gain, w_out,
           ffn_norm_gain, w_route_group, b_route_group, w_route_expert, b_route_expert,
           w_expert_gate, w_expert_up, w_expert_down, final_norm_gain):
    depth = mix_norm_gain.shape[0]
    assert depth == 1, "the final rmsnorm is fused into the single layer's combine kernel"
    l = 0
    return _layer(x, mix_norm_gain[l], w_in[l], attn_out_gain[l], ret_decay_fwd[l], ret_decay_bwd[l],
                  ret_out_gain[l], w_out[l], ffn_norm_gain[l], w_route_group[l], b_route_group[l],
                  w_route_expert[l], b_route_expert[l], w_expert_gate[l], w_expert_up[l], w_expert_down[l],
                  final_norm_gain)
```

```python
import functools

import numpy as np
import jax
import jax.numpy as jnp
from jax import lax
from jax.experimental import pallas as pl
from jax.experimental.pallas import tpu as pltpu
from jax.experimental.pallas import tpu_sc as plsc

F32 = jnp.float32
BF16 = jnp.bfloat16

ATTN_HEADS = 8
HEAD_DIM = 64
RET_HEADS = 8
HEAD_GROUP_WIDTH = 512
N_PROJ_GROUPS = 7
DILATED_BRANCHES = ((128, 1), (512, 4), (2048, 16))
ROPE_THETA = 500000.0
ROPE_DIM = HEAD_DIM // 4
RET_THETA = 10000.0
RET_CHUNK = 128
MOE_GROUPS = 4
EXPERTS_PER_GROUP = 8
N_EXPERTS = MOE_GROUPS * EXPERTS_PER_GROUP
MOE_BLOCK = 512
NORM_EPS = 1e-6
NEG_INF = -1e30

LANES = 128
VMEM_LIMIT = 56 * 1024 * 1024


def _cparams(sem):
    return pltpu.CompilerParams(dimension_semantics=sem, vmem_limit_bytes=VMEM_LIMIT)


def _rotary_tables(seq, half, freqs):
    pos = np.arange(seq, dtype=np.float64)[:, None]
    ang = pos * freqs[None, :].astype(np.float64)
    cos, sin = np.cos(ang), np.sin(ang)
    c = np.ones((seq, HEAD_DIM)); sp = np.zeros((seq, HEAD_DIM)); sm = np.zeros((seq, HEAD_DIM))
    c[:, :half] = cos; c[:, half:2 * half] = cos
    sp[:, half:2 * half] = sin
    sm[:, :half] = -sin
    rep = LANES // HEAD_DIM
    return tuple(jnp.asarray(np.tile(t, (1, rep)), F32) for t in (c, sp, sm))


def _rotate(t, c, sp, sm, half):
    outs = []
    for g in range(t.shape[1] // LANES):
        tg = t[:, g * LANES:(g + 1) * LANES]
        outs.append(tg * c + pltpu.roll(tg, half, 1) * sp + pltpu.roll(tg, LANES - half, 1) * sm)
    return jnp.concatenate(outs, axis=1)


CLASS_DILATIONS = tuple(d for _, d in DILATED_BRANCHES if d > 1)
ATTN_Q_SCALE = float(np.log2(np.e)) * HEAD_DIM ** -0.5


def _inproj_kernel(x_ref, gain_ref, w_ref, ca_ref, spa_ref, sma_ref, cr_ref, spr_ref, smr_ref, *rest):
    n_cls = len(CLASS_DILATIONS)
    nat_refs = rest[0:3]
    cls_refs = [rest[3 + 3 * c:6 + 3 * c] for c in range(n_cls)]
    qr_ref, kr_ref, vr_ref, gr_ref = rest[3 + 3 * n_cls:7 + 3 * n_cls]
    stage_ref = rest[7 + 3 * n_cls]
    x = x_ref[0]
    tm = x.shape[0]
    ms = jnp.mean(x * x, axis=-1, keepdims=True)
    xn = (x * lax.rsqrt(ms + NORM_EPS) * gain_ref[...]).astype(BF16)
    gw = HEAD_GROUP_WIDTH

    def proj(c):
        return jnp.dot(xn, w_ref[:, c * gw:(c + 1) * gw], preferred_element_type=F32)

    a_tabs = (ca_ref[...], spa_ref[...], sma_ref[...])
    r_tabs = (cr_ref[...], spr_ref[...], smr_ref[...])
    attn_vals = ((_rotate(proj(0), *a_tabs, ROPE_DIM // 2) * ATTN_Q_SCALE),
                 _rotate(proj(1), *a_tabs, ROPE_DIM // 2),
                 proj(2))
    for j, val in enumerate(attn_vals):
        nat_refs[j][0] = val.astype(BF16)
        for g in range(gw // LANES):
            stage_ref[g] = val[:, g * LANES:(g + 1) * LANES]
        for c, d in enumerate(CLASS_DILATIONS):
            for r in range(d):
                for g in range(gw // LANES):
                    col = r * gw + g * LANES
                    cls_refs[c][j][0, :, col:col + LANES] = (
                        stage_ref[g, pl.ds(r, tm // d, stride=d), :].astype(BF16))
    qr_ref[0] = _rotate(proj(3), *r_tabs, HEAD_DIM // 2).astype(BF16)
    kr_ref[0] = (_rotate(proj(4), *r_tabs, HEAD_DIM // 2) * (HEAD_DIM ** -0.5)).astype(BF16)
    vr_ref[0] = proj(5).astype(BF16)
    g = proj(6)
    gr_ref[0] = (g * jax.nn.sigmoid(g)).astype(BF16)


def _inproj(x, gain, w_in_bf16, tm):
    b, s, d = x.shape
    rope_freqs = ROPE_THETA ** (-np.arange(0, ROPE_DIM, 2, dtype=np.float32) / ROPE_DIM)
    ret_freqs = RET_THETA ** (-np.linspace(0.0, 1.0, HEAD_DIM // 2, dtype=np.float32))
    tabs = _rotary_tables(s, ROPE_DIM // 2, rope_freqs) + _rotary_tables(s, HEAD_DIM // 2, ret_freqs)
    gw = HEAD_GROUP_WIDTH
    tab_spec = pl.BlockSpec((tm, LANES), lambda si, bi: (si, 0))

    def view(dil):
        return (pl.BlockSpec((1, tm // dil, dil * gw), lambda si, bi: (bi, si, 0)),
                jax.ShapeDtypeStruct((b, s // dil, dil * gw), BF16))

    views = [view(1)] * 3 + [view(dil) for dil in CLASS_DILATIONS for _ in range(3)] + [view(1)] * 4
    outs = pl.pallas_call(
        _inproj_kernel,
        grid=(s // tm, b),
        in_specs=[pl.BlockSpec((1, tm, d), lambda si, bi: (bi, si, 0)),
                  pl.BlockSpec((1, d), lambda si, bi: (0, 0)),
                  pl.BlockSpec(w_in_bf16.shape, lambda si, bi: (0, 0))] + [tab_spec] * 6,
        out_specs=[v[0] for v in views],
        out_shape=[v[1] for v in views],
        scratch_shapes=[pltpu.VMEM((gw // LANES, tm, LANES), F32)],
        compiler_params=_cparams(("arbitrary", "arbitrary")),
        name="inproj",
    )(x, gain.reshape(1, d), w_in_bf16, *tabs)
    n_attn = 3 * (1 + len(CLASS_DILATIONS))
    attn_qkv = {dil: outs[3 * c:3 * c + 3] for c, dil in enumerate((1,) + CLASS_DILATIONS)}
    return attn_qkv, outs[n_attn:]


ATTN_WINDOW_CASES = 3


def _attn_kernel(q_ref, k_ref, v_ref, o_ref, lse_ref, bias_ref, *, length, tq, reach):
    tqb = q_ref.shape[1]
    win = tq + 2 * reach
    heads_per_pair = LANES // HEAD_DIM
    qi = pl.program_id(2)
    lane = lax.broadcasted_iota(jnp.int32, (1, LANES), 1)
    lane_t = lax.broadcasted_iota(jnp.int32, (tq, LANES), 1)
    ones = jnp.ones((win, LANES), BF16)

    @pl.when((pl.program_id(0) == 0) & (pl.program_id(1) == 0) & (qi == 0))
    def _():
        diff = (lax.broadcasted_iota(jnp.int32, (heads_per_pair * tq, win), 1)
                - (lax.broadcasted_iota(jnp.int32, (heads_per_pair * tq, win), 0) & (tq - 1)))
        for case in range(ATTN_WINDOW_CASES):
            off = case * reach
            bias_ref[case] = jnp.where((diff >= off - reach) & (diff <= off + reach), 0.0, NEG_INF)

    def sub(t, cls):
        q0 = qi * tqb + t * tq
        ws = jnp.clip(q0 - reach, 0, length - win)
        ws = pl.multiple_of(ws, reach)
        bias = bias_ref[(q0 - ws) // reach]
        rows = pl.ds(t * tq, tq)
        m_tile = jnp.zeros((tq, LANES), F32)
        l_tile = jnp.ones((tq, LANES), F32)
        for g in range(HEAD_GROUP_WIDTH // LANES):
            cols = slice(cls * HEAD_GROUP_WIDTH + g * LANES, cls * HEAD_GROUP_WIDTH + (g + 1) * LANES)
            qg = q_ref[0, rows, cols]
            kw = k_ref[0, pl.ds(ws, win), cols]
            v_ones = jnp.concatenate([v_ref[0, pl.ds(ws, win), cols], ones], axis=1)
            hms = [(lane >= hh * HEAD_DIM) & (lane < (hh + 1) * HEAD_DIM) for hh in range(heads_per_pair)]
            q2 = jnp.concatenate([qg * hm.astype(BF16) for hm in hms], axis=0)
            sc = lax.dot_general(q2, kw, (((1,), (1,)), ((), ())), preferred_element_type=F32) + bias
            m = jnp.max(sc, axis=-1, keepdims=True)
            p = jnp.exp2(sc - m).astype(BF16)
            pv = jnp.dot(p, v_ones, preferred_element_type=F32)
            l = pv[:, LANES:]
            o = pv[:, :LANES] / l
            o_pair = jnp.zeros((tq, LANES), F32)
            for hh in range(heads_per_pair):
                part = slice(hh * tq, (hh + 1) * tq)
                head_lane = lane_t == g * heads_per_pair + hh
                o_pair = jnp.where(hms[hh], o[part], o_pair)
                m_tile = jnp.where(head_lane, m[part], m_tile)
                l_tile = jnp.where(head_lane, l[part], l_tile)
            o_ref[0, rows, cols] = o_pair.astype(BF16)
        lse_ref[0, rows, cls * LANES:(cls + 1) * LANES] = m_tile + jnp.log2(l_tile)

    for cls in range(q_ref.shape[2] // HEAD_GROUP_WIDTH):
        for t in range(tqb // tq):
            sub(t, cls)


def _attn_branch(qc, kc, vc, dilation, reach, tq, tqb, sub_tiles=4):
    b, length, dw = qc.shape
    w = dw // dilation
    tqb = min(tqb, length)
    assert tq % reach == 0 and tq > reach and length % tq == 0 and length >= tq + 2 * reach
    ncls = min(dilation, max(1, sub_tiles // (tqb // tq)))
    q_spec = pl.BlockSpec((1, tqb, ncls * w), lambda bi, r, qi: (bi, qi, r))
    kv_spec = pl.BlockSpec((1, length, ncls * w), lambda bi, r, qi: (bi, 0, r))
    o, lse = pl.pallas_call(
        functools.partial(_attn_kernel, length=length, tq=tq, reach=reach),
        grid=(b, dilation // ncls, length // tqb),
        in_specs=[q_spec, kv_spec, kv_spec],
        out_specs=[q_spec, pl.BlockSpec((1, tqb, ncls * LANES), lambda bi, r, qi: (bi, qi, r))],
        out_shape=[jax.ShapeDtypeStruct((b, length, dilation * w), BF16),
                   jax.ShapeDtypeStruct((b, length, dilation * LANES), F32)],
        scratch_shapes=[pltpu.VMEM((ATTN_WINDOW_CASES, (LANES // HEAD_DIM) * tq, tq + 2 * reach), F32)],
        compiler_params=_cparams(("arbitrary", "arbitrary", "arbitrary")),
        name=f"attn_d{dilation}",
    )(qc, kc, vc)
    return o, lse


RET_TAB_QF, RET_TAB_QB, RET_TAB_KF, RET_TAB_KB = range(4)


def _retention_kernel(lgf_ref, lgb_ref, q_ref, k_ref, v_ref, g_ref, gain_ref, o_ref,
                      tab_ref, dec_ref, sb_ref, st_ref, *, chunk, unroll):
    c = chunk
    n = q_ref.shape[1] // c
    width = q_ref.shape[2]
    n_pairs = width // LANES
    heads_per_pair = LANES // HEAD_DIM
    n_heads = n_pairs * heads_per_pair
    head0 = pl.program_id(1) * n_heads
    lane_w = lax.broadcasted_iota(jnp.int32, (1, width), 1)
    lgf = [lgf_ref[head0 + hd] for hd in range(n_heads)]
    lgb = [lgb_ref[head0 + hd] for hd in range(n_heads)]
    lgf_lane = jnp.zeros((1, width), F32)
    lgb_lane = jnp.zeros((1, width), F32)
    for hd in range(n_heads):
        in_head = (lane_w >= hd * HEAD_DIM) & (lane_w < (hd + 1) * HEAD_DIM)
        lgf_lane = jnp.where(in_head, lgf[hd], lgf_lane)
        lgb_lane = jnp.where(in_head, lgb[hd], lgb_lane)
    idx = lax.broadcasted_iota(jnp.int32, (c, width), 0).astype(F32)
    tab_ref[RET_TAB_QF] = jnp.exp((idx + 1.0) * lgf_lane)
    tab_ref[RET_TAB_QB] = jnp.exp((c - idx) * lgb_lane)
    tab_ref[RET_TAB_KF] = jnp.exp((c - 1.0 - idx) * lgf_lane)
    tab_ref[RET_TAB_KB] = jnp.exp(idx * lgb_lane)
    sdf = jnp.exp(c * lgf_lane)
    sdb = jnp.exp(c * lgb_lane)
    dmat = (lax.broadcasted_iota(jnp.int32, (c, c), 0)
            - lax.broadcasted_iota(jnp.int32, (c, c), 1)).astype(F32)
    for hd in range(n_heads):
        dec_ref[hd // heads_per_pair, :, (hd % heads_per_pair) * c:(hd % heads_per_pair + 1) * c] = (
            jnp.where(dmat >= 0, jnp.exp(dmat * lgf[hd]), jnp.exp(-dmat * lgb[hd])))
    lane = lax.broadcasted_iota(jnp.int32, (1, LANES), 1)
    lane_hi = lane >= HEAD_DIM
    head_masks = [((lane >= hh * HEAD_DIM) & (lane < (hh + 1) * HEAD_DIM)).astype(BF16)
                  for hh in range(heads_per_pair)]
    row_hi = lax.broadcasted_iota(jnp.int32, (LANES, LANES), 0) >= HEAD_DIM
    col_hi = lax.broadcasted_iota(jnp.int32, (LANES, LANES), 1) >= HEAD_DIM
    blockdiag = row_hi == col_hi

    def kv_state(kd, vv):
        kt = jnp.transpose(kd).astype(BF16)
        return jnp.where(blockdiag, jnp.dot(kt, vv, preferred_element_type=F32), 0.0)

    st_ref[...] = jnp.zeros_like(st_ref)
    sb_ref[n - 1] = jnp.zeros(sb_ref.shape[1:], sb_ref.dtype)

    def back(i, carry):
        nn = n - 1 - i
        rows = pl.ds(pl.multiple_of(nn * c, c), c)
        for p in range(n_pairs):
            cols = slice(p * LANES, (p + 1) * LANES)
            kd = k_ref[0, rows, cols].astype(F32) * tab_ref[RET_TAB_KB, :, cols]
            new = st_ref[p] * sdb[:, cols] + kv_state(kd, v_ref[0, rows, cols])
            st_ref[p] = new
            sb_ref[nn - 1, p] = new.astype(BF16)
        return carry

    lax.fori_loop(0, n - 1, back, 0, unroll=unroll)

    st_ref[...] = jnp.zeros_like(st_ref)

    def fwd(nn, carry):
        rows = pl.ds(pl.multiple_of(nn * c, c), c)
        for p in range(n_pairs):
            cols = slice(p * LANES, (p + 1) * LANES)
            qq = q_ref[0, rows, cols]
            kk = k_ref[0, rows, cols]
            vv = v_ref[0, rows, cols]
            qf = qq.astype(F32)
            sf = st_ref[p]
            qcat = jnp.concatenate([(qf * tab_ref[RET_TAB_QF, :, cols]).astype(BF16),
                                    (qf * tab_ref[RET_TAB_QB, :, cols]).astype(BF16)], axis=1)
            scat = jnp.concatenate([sf.astype(BF16), sb_ref[nn, p]], axis=0)
            o = jnp.dot(qcat, scat, preferred_element_type=F32)
            k2 = jnp.concatenate([kk * hm for hm in head_masks], axis=0)
            v2 = jnp.concatenate([vv * hm for hm in head_masks], axis=0)
            sc = lax.dot_general(qq, k2, (((1,), (1,)), ((), ())), preferred_element_type=F32)
            o = o + jnp.dot((sc * dec_ref[p]).astype(BF16), v2, preferred_element_type=F32)
            s_lo = jnp.sum(jnp.where(lane_hi, 0.0, o), axis=-1, keepdims=True)
            s_hi = jnp.sum(jnp.where(lane_hi, o, 0.0), axis=-1, keepdims=True)
            mu = jnp.where(lane_hi, s_hi, s_lo) * (1.0 / HEAD_DIM)
            dev = o - mu
            d2 = dev * dev
            v_lo = jnp.sum(jnp.where(lane_hi, 0.0, d2), axis=-1, keepdims=True)
            v_hi = jnp.sum(jnp.where(lane_hi, d2, 0.0), axis=-1, keepdims=True)
            var = jnp.where(lane_hi, v_hi, v_lo) * (1.0 / HEAD_DIM)
            out = dev * lax.rsqrt(var + NORM_EPS) * gain_ref[:, cols] * g_ref[0, rows, cols].astype(F32)
            o_ref[0, rows, cols] = out.astype(BF16)
            kd = kk.astype(F32) * tab_ref[RET_TAB_KF, :, cols]
            st_ref[p] = sf * sdf[:, cols] + kv_state(kd, vv)
        return carry

    lax.fori_loop(0, n, fwd, 0, unroll=unroll)


def _retention(qr, kr, vr, gate, lg_f, lg_b, out_gain, width=512, unroll=8):
    b, s, w = qr.shape
    n_pairs = width // LANES
    n_heads = width // HEAD_DIM
    spec = pl.BlockSpec((1, s, width), lambda bi, p, *_: (bi, 0, p))
    grid_spec = pltpu.PrefetchScalarGridSpec(
        num_scalar_prefetch=2,
        grid=(b, w // width),
        in_specs=[spec, spec, spec, spec, pl.BlockSpec((1, width), lambda bi, p, *_: (0, p))],
        out_specs=spec,
        scratch_shapes=[pltpu.VMEM((4, RET_CHUNK, width), F32),
                        pltpu.VMEM((n_pairs, RET_CHUNK, (LANES // HEAD_DIM) * RET_CHUNK), F32),
                        pltpu.VMEM((s // RET_CHUNK, n_pairs, LANES, LANES), BF16),
                        pltpu.VMEM((n_pairs, LANES, LANES), F32)],
    )
    return pl.pallas_call(
        functools.partial(_retention_kernel, chunk=RET_CHUNK, unroll=unroll),
        grid_spec=grid_spec,
        out_shape=jax.ShapeDtypeStruct((b, s, w), BF16),
        compiler_params=_cparams(("arbitrary", "arbitrary")),
        name="retention",
    )(lg_f, lg_b, qr, kr, vr, gate, out_gain.reshape(1, w))


def _split_bf16(t):
    hi = t.astype(BF16)
    lo = (t - hi.astype(F32)).astype(BF16)
    return hi, lo


def _pack_bf16_pairs(t):
    n = t.shape[1] // 2
    hi = pltpu.bitcast(t[:, :n].astype(BF16).astype(F32), jnp.uint32)
    lo = pltpu.bitcast(t[:, n:].astype(BF16).astype(F32), jnp.uint32)
    return hi | (lo >> 16)


def _unpack_bf16_pairs(u):
    hi = pltpu.bitcast(u & jnp.uint32(0xFFFF0000), F32)
    lo = pltpu.bitcast(u << 16, F32)
    return jnp.concatenate([hi, lo], axis=1)


def _outproj_kernel(x_ref, o1_ref, o2_ref, o3_ref, l1_ref, l2_ref, l3_ref, orr_ref, ga_ref, expand_ref,
                    wout_ref, gf_ref, wr_ref, br_ref, h_ref, hn_ref, logit_ref,
                    *nat_refs):
    tm = x_ref.shape[0]
    gw = HEAD_GROUP_WIDTH
    os, ls = [], []
    for (_, dil), o_ref, l_ref in zip(DILATED_BRANCHES, (o1_ref, o2_ref, o3_ref), (l1_ref, l2_ref, l3_ref)):
        if dil == 1:
            os.append(o_ref[...].astype(F32))
            ls.append(l_ref[...])
            continue
        c = CLASS_DILATIONS.index(dil)
        onat_ref, lnat_ref = nat_refs[2 * c], nat_refs[2 * c + 1]
        for r in range(dil):
            rows = pl.ds(r, tm // dil, stride=dil)
            for g in range(gw // LANES):
                col = r * gw + g * LANES
                onat_ref[g, rows, :] = o_ref[:, col:col + LANES].astype(F32)
            lnat_ref[rows, :] = l_ref[:, r * LANES:(r + 1) * LANES]
        os.append(jnp.concatenate([onat_ref[g] for g in range(gw // LANES)], axis=1))
        ls.append(lnat_ref[...])
    mx = jnp.maximum(jnp.maximum(ls[0], ls[1]), ls[2])
    es = [jnp.exp2(l - mx) for l in ls]
    inv = 1.0 / (es[0] + es[1] + es[2])
    expand = expand_ref[...]
    oa = jnp.zeros((tm, gw), F32)
    for e, o in zip(es, os):
        wexp = jnp.dot(jnp.concatenate(_split_bf16(e * inv), axis=1), expand, preferred_element_type=F32)
        oa = oa + wexp * o
    oa = oa * lax.rsqrt(jnp.mean(oa * oa, axis=-1, keepdims=True) + NORM_EPS) * ga_ref[...]
    mixed = jnp.concatenate([oa.astype(BF16), orr_ref[...]], axis=1)
    h = x_ref[...] + jnp.dot(mixed, wout_ref[...], preferred_element_type=F32)
    h_ref[...] = h
    hn = h * lax.rsqrt(jnp.mean(h * h, axis=-1, keepdims=True) + NORM_EPS) * gf_ref[...]
    hn_ref[...] = _pack_bf16_pairs(hn)
    prod = jnp.dot(jnp.concatenate(_split_bf16(hn), axis=0), wr_ref[...], preferred_element_type=F32)
    logit_ref[...] = prod[:tm, :LANES] + prod[:tm, LANES:] + prod[tm:, :LANES] + br_ref[...]


def _outproj(x2, o_list, lse_list, orr, attn_gain, w_out_bf16, ffn_gain, w_router, b_router, tm):
    t, d = x2.shape
    w = HEAD_GROUP_WIDTH
    expand = np.zeros((LANES, w), np.float32)
    for hd in range(ATTN_HEADS):
        expand[hd, hd * HEAD_DIM:(hd + 1) * HEAD_DIM] = 1.0
    expand = jnp.asarray(np.concatenate([expand, expand], axis=0), BF16)
    wr_hi = w_router.astype(BF16)
    wr = jnp.concatenate([wr_hi, (w_router - wr_hi.astype(F32)).astype(BF16)], axis=1)
    row = lambda width, dil=1: pl.BlockSpec((tm // dil, dil * width), lambda i: (i, 0))
    full = lambda a: pl.BlockSpec(a.shape, lambda i: (0,) * a.ndim)
    ga = attn_gain.reshape(1, w)
    gf = ffn_gain.reshape(1, d)
    dils = [dil for _, dil in DILATED_BRANCHES]
    o_flat = [o.reshape(t // dil, dil * w) for o, dil in zip(o_list, dils)]
    l_flat = [l.reshape(t // dil, dil * LANES) for l, dil in zip(lse_list, dils)]
    nat_scratch = []
    for _ in CLASS_DILATIONS:
        nat_scratch += [pltpu.VMEM((w // LANES, tm, LANES), F32), pltpu.VMEM((tm, LANES), F32)]
    return pl.pallas_call(
        _outproj_kernel,
        grid=(t // tm,),
        in_specs=[row(d)] + [row(w, dil) for dil in dils] + [row(LANES, dil) for dil in dils] + [row(w)]
                 + [full(ga), full(expand), full(w_out_bf16), full(gf), full(wr), full(b_router)],
        out_specs=[row(d), row(d // 2), row(LANES)],
        out_shape=[jax.ShapeDtypeStruct((t, d), F32),
                   jax.ShapeDtypeStruct((t, d // 2), jnp.uint32),
                   jax.ShapeDtypeStruct((t, LANES), F32)],
        scratch_shapes=nat_scratch,
        compiler_params=_cparams(("arbitrary",)),
        name="outproj",
    )(x2, *o_flat, *l_flat, orr, ga, expand, w_out_bf16, gf, wr, b_router)


ROUTE_E1, ROUTE_E2, ROUTE_G1, ROUTE_G2, ROUTE_R1, ROUTE_R2 = range(6)
GROUP_LANE0 = N_EXPERTS
SUBLANES = 8


def _route_kernel(logit_ref, tri_ref, route_ref, route_t_ref, count_ref, run_ref):
    @pl.when(pl.program_id(0) == 0)
    def _():
        run_ref[...] = jnp.zeros_like(run_ref)

    lg_t = jnp.transpose(logit_ref[...])
    tm = lg_t.shape[1]
    assert EXPERTS_PER_GROUP == SUBLANES and MOE_GROUPS <= SUBLANES
    rid = lax.broadcasted_iota(jnp.int32, (SUBLANES, tm), 0)
    big = jnp.int32(1 << 20)

    def top(vals):
        m = jnp.max(vals, axis=0, keepdims=True)
        i = jnp.min(jnp.where(vals == m, rid, big), axis=0, keepdims=True)
        return m, i

    gl = jnp.where(rid < MOE_GROUPS, lg_t[GROUP_LANE0:GROUP_LANE0 + SUBLANES], -jnp.inf)
    gmax, gidx = top(gl)
    group_gate = 1.0 / jnp.sum(jnp.exp(gl - gmax), axis=0, keepdims=True)
    el = lg_t[0:EXPERTS_PER_GROUP]
    for g in range(1, MOE_GROUPS):
        el = jnp.where(gidx == g, lg_t[g * EXPERTS_PER_GROUP:(g + 1) * EXPERTS_PER_GROUP], el)
    t1, i1 = top(el)
    t2, i2 = top(jnp.where(rid == i1, -jnp.inf, el))
    e21 = jnp.exp(t2 - t1)
    g1 = group_gate / (1.0 + e21)
    g2 = group_gate * e21 / (1.0 + e21)
    e1 = gidx * EXPERTS_PER_GROUP + i1
    e2 = gidx * EXPERTS_PER_GROUP + i2
    erow = lax.broadcasted_iota(jnp.int32, (N_EXPERTS, tm), 0)
    oh1 = erow == e1
    oh2 = erow == e2
    cnt = oh1.astype(F32) + oh2.astype(F32)
    run = run_ref[:, 0:1]
    prefix = jnp.dot(cnt.astype(BF16), tri_ref[...], preferred_element_type=F32) + run
    r1 = jnp.sum(jnp.where(oh1, prefix, 0.0), axis=0, keepdims=True)
    r2 = jnp.sum(jnp.where(oh2, prefix, 0.0), axis=0, keepdims=True)
    new_run = jnp.broadcast_to(run + jnp.sum(cnt, axis=1, keepdims=True), run_ref.shape)
    run_ref[...] = new_run
    count_ref[...] = new_run
    out = jnp.zeros((SUBLANES, tm), F32)
    for row, val in ((ROUTE_E1, e1.astype(F32)), (ROUTE_E2, e2.astype(F32)), (ROUTE_G1, g1),
                     (ROUTE_G2, g2), (ROUTE_R1, r1), (ROUTE_R2, r2)):
        out = jnp.where(rid == row, val, out)
    route_t_ref[0] = out
    out_t = jnp.concatenate([out, jnp.zeros((LANES - SUBLANES, tm), F32)], axis=0)
    route_ref[...] = jnp.transpose(out_t)


def _route(logits, tm):
    t = logits.shape[0]
    tri = jnp.asarray(np.triu(np.ones((tm, tm), np.float32), 1), BF16)
    return pl.pallas_call(
        _route_kernel,
        grid=(t // tm,),
        in_specs=[pl.BlockSpec((tm, LANES), lambda i: (i, 0)), pl.BlockSpec((tm, tm), lambda i: (0, 0))],
        out_specs=[pl.BlockSpec((tm, LANES), lambda i: (i, 0)),
                   pl.BlockSpec((1, SUBLANES, tm), lambda i: (i, 0, 0)),
                   pl.BlockSpec((N_EXPERTS, LANES), lambda i: (0, 0))],
        out_shape=[jax.ShapeDtypeStruct((t, LANES), F32),
                   jax.ShapeDtypeStruct((t // tm, SUBLANES, tm), F32),
                   jax.ShapeDtypeStruct((N_EXPERTS, LANES), F32)],
        scratch_shapes=[pltpu.VMEM((N_EXPERTS, LANES), F32)],
        compiler_params=_cparams(("arbitrary",)),
        name="route",
    )(logits, tri)


TOP_K = 2


SEG_ALIGN = 8
SORTED_TAIL = N_EXPERTS * SEG_ALIGN + MOE_BLOCK


def _tail_pieces():
    full, rest = divmod(SORTED_TAIL, MOE_BLOCK)
    return [MOE_BLOCK] * full + ([rest] if rest else [])


SC_CHUNK = 64
SC_BUFFERS = 3
COMBINE_PARTS = 2
ZERO_ROWS = N_EXPERTS * SEG_ALIGN + SORTED_TAIL
SORTED_ROWS_EXTRA = SORTED_TAIL + SEG_ALIGN


def _sc_workers():
    info = plsc.get_sparse_core_info()
    return info.num_cores, info.num_cores * info.num_subcores


def _dispatch(hn_packed, pos, zero_pos):
    t, dp = hn_packed.shape
    ncores, nw = _sc_workers()
    per_w = t // nw
    n_ch = per_w // SC_CHUNK
    z_rows = ZERO_ROWS // nw
    assert per_w % SC_CHUNK == 0 and ZERO_ROWS % nw == 0 and z_rows % SEG_ALIGN == 0 and TOP_K == 2
    idx = [pos[k].reshape(nw, n_ch, SC_CHUNK) for k in range(TOP_K)]
    zeros = jnp.zeros((z_rows, dp), hn_packed.dtype)
    mesh = plsc.VectorSubcoreMesh(core_axis_name="c", subcore_axis_name="s")

    @functools.partial(
        pl.kernel, mesh=mesh,
        out_type=jax.ShapeDtypeStruct((TOP_K * t + SORTED_ROWS_EXTRA, dp), hn_packed.dtype),
        scratch_types=[pltpu.VMEM((n_ch, SC_CHUNK), jnp.int32), pltpu.VMEM((n_ch, SC_CHUNK), jnp.int32),
                       pltpu.VMEM((z_rows,), jnp.int32),
                       pltpu.VMEM((SC_BUFFERS, SC_CHUNK, dp), hn_packed.dtype),
                       pltpu.VMEM((z_rows, dp), hn_packed.dtype),
                       pltpu.SemaphoreType.DMA((SC_BUFFERS,)), pltpu.SemaphoreType.DMA((SC_BUFFERS,)),
                       pltpu.SemaphoreType.DMA],
    )
    def scatter(hn_hbm, p0_hbm, p1_hbm, zpos_hbm, zeros_hbm, xs_hbm, i0_v, i1_v, iz_v, rows_v, zero_v,
                lsem, ssem, zsem):
        wid = lax.axis_index("s") * ncores + lax.axis_index("c")
        base = wid * per_w
        zero_load = pltpu.async_copy(zeros_hbm, zero_v, zsem)
        pltpu.sync_copy(zpos_hbm.at[wid], iz_v)
        pltpu.sync_copy(p0_hbm.at[wid], i0_v)
        pltpu.sync_copy(p1_hbm.at[wid], i1_v)

        def load(c):
            return pltpu.async_copy(hn_hbm.at[pl.ds(base + c * SC_CHUNK, SC_CHUNK)], rows_v.at[c % SC_BUFFERS],
                                    lsem.at[c % SC_BUFFERS])

        loads = {c: load(c) for c in range(min(SC_BUFFERS - 1, n_ch))}
        scat = {}
        for c in range(n_ch):
            slot = c % SC_BUFFERS
            loads[c].wait()
            scat[c] = (pltpu.async_copy(rows_v.at[slot], xs_hbm.at[i0_v.at[c]], ssem.at[slot]),
                       pltpu.async_copy(rows_v.at[slot], xs_hbm.at[i1_v.at[c]], ssem.at[slot]))
            if c >= 1:
                for d in scat[c - 1]:
                    d.wait()
            if c + SC_BUFFERS - 1 < n_ch:
                loads[c + SC_BUFFERS - 1] = load(c + SC_BUFFERS - 1)
        zero_load.wait()
        zero_scatter = pltpu.async_copy(zero_v, xs_hbm.at[iz_v], zsem)
        for d in scat[n_ch - 1]:
            d.wait()
        zero_scatter.wait()

    return scatter(hn_packed, idx[0], idx[1], zero_pos.reshape(nw, z_rows), zeros)


def _expert_kernel(bexp_ref, nreal_ref, row0_ref, bnext_ref, xs_hbm, wg_hbm, wu_hbm, wd_hbm, ys_hbm,
                   xbuf, ybuf, wg_f32, wu_f32, wd_f32, wg_bf, wu_bf, wd_bf, isem, osem, wsem, *, n_rows):
    i = pl.program_id(0)
    nb = pl.num_programs(0)
    slot = i % 2
    nslot = 1 - slot
    n_cur = nreal_ref[i]
    prev = jnp.maximum(i - 1, 0)
    nxt = jnp.minimum(i + 1, nb - 1)

    def in_copy(blk, s):
        row0 = pl.multiple_of(row0_ref[blk], SEG_ALIGN)
        return pltpu.make_async_copy(xs_hbm.at[pl.ds(row0, MOE_BLOCK)], xbuf.at[s], isem.at[s])

    def out_copy(blk, s):
        row0 = pl.multiple_of(row0_ref[blk], SEG_ALIGN)
        return pltpu.make_async_copy(ybuf.at[s], ys_hbm.at[pl.ds(row0, MOE_BLOCK)], osem.at[s])

    @pl.when(i == 0)
    def _():
        ybuf[...] = jnp.zeros_like(ybuf)
        tails = [pltpu.make_async_copy(ybuf.at[s, pl.ds(0, size)], ys_hbm.at[pl.ds(n_rows + s * MOE_BLOCK, size)],
                                       osem.at[s])
                 for s, size in enumerate(_tail_pieces())]
        for tail in tails:
            tail.start()
        for tail in tails:
            tail.wait()

        @pl.when(n_cur > 0)
        def _():
            in_copy(i, slot).start()

    @pl.when((i + 1 < nb) & (nreal_ref[nxt] > 0))
    def _():
        in_copy(nxt, nslot).start()

    def weight_copies(e):
        return [pltpu.make_async_copy(w_hbm.at[e], w_f32, wsem.at[j])
                for j, (w_hbm, w_f32) in enumerate(((wg_hbm, wg_f32), (wu_hbm, wu_f32), (wd_hbm, wd_f32)))]

    @pl.when((i == 0) & (n_cur > 0))
    def _():
        for c in weight_copies(bexp_ref[0]):
            c.start()

    @pl.when((n_cur > 0) & ((i == 0) | (bexp_ref[i] != bexp_ref[prev])))
    def _():
        for c in weight_copies(bexp_ref[i]):
            c.wait()
        wg_bf[...] = wg_f32[...].astype(BF16)
        wu_bf[...] = wu_f32[...].astype(BF16)
        wd_bf[...] = wd_f32[...].astype(BF16)

        @pl.when(bnext_ref[i] >= 0)
        def _():
            for c in weight_copies(bnext_ref[i]):
                c.start()

    @pl.when(n_cur > 0)
    def _():
        in_copy(i, slot).wait()
        xb = _unpack_bf16_pairs(xbuf[slot]).astype(BF16)
        gate = jnp.dot(xb, wg_bf[...], preferred_element_type=F32)
        up = jnp.dot(xb, wu_bf[...], preferred_element_type=F32)
        hid = (gate * jax.nn.sigmoid(gate) * up).astype(BF16)
        ybuf[slot] = _pack_bf16_pairs(jnp.dot(hid, wd_bf[...], preferred_element_type=F32))

    @pl.when((i >= 1) & (nreal_ref[prev] > 0))
    def _():
        out_copy(prev, nslot).wait()

    @pl.when(n_cur > 0)
    def _():
        out_copy(i, slot).start()

        @pl.when(i == nb - 1)
        def _():
            out_copy(i, slot).wait()


def _experts(xs, block_expert, block_nreal, block_row0, block_next, wg, wu, wd):
    n_blocks = block_expert.shape[0]
    assert n_blocks >= 2 and len(_tail_pieces()) <= 2
    n_rows_pad = xs.shape[0] - SORTED_ROWS_EXTRA + SORTED_TAIL
    dp = xs.shape[1]
    _, d, ff = wg.shape
    any_spec = pl.BlockSpec(memory_space=pl.ANY)
    grid_spec = pltpu.PrefetchScalarGridSpec(
        num_scalar_prefetch=4,
        grid=(n_blocks,),
        in_specs=[any_spec, any_spec, any_spec, any_spec],
        out_specs=any_spec,
        scratch_shapes=[pltpu.VMEM((2, MOE_BLOCK, dp), jnp.uint32),
                        pltpu.VMEM((2, MOE_BLOCK, dp), jnp.uint32),
                        pltpu.VMEM((d, ff), F32),
                        pltpu.VMEM((d, ff), F32),
                        pltpu.VMEM((ff, d), F32),
                        pltpu.VMEM((d, ff), BF16),
                        pltpu.VMEM((d, ff), BF16),
                        pltpu.VMEM((ff, d), BF16),
                        pltpu.SemaphoreType.DMA((2,)),
                        pltpu.SemaphoreType.DMA((2,)),
                        pltpu.SemaphoreType.DMA((3,))],
    )
    return pl.pallas_call(
        functools.partial(_expert_kernel, n_rows=n_rows_pad - SORTED_TAIL),
        grid_spec=grid_spec,
        out_shape=jax.ShapeDtypeStruct((n_rows_pad, dp), jnp.uint32),
        compiler_params=_cparams(("arbitrary",)),
        name="experts",
    )(block_expert, block_nreal, block_row0, block_next, xs, wg, wu, wd)


def _combine_gather(ys, pos):
    t = pos.shape[1]
    dp = ys.shape[1]
    ncores, nw = _sc_workers()
    per_w = t // nw
    n_ch = per_w // SC_CHUNK
    assert per_w % SC_CHUNK == 0
    idx = [pos[k].reshape(nw, n_ch, SC_CHUNK) for k in range(TOP_K)]
    mesh = plsc.VectorSubcoreMesh(core_axis_name="c", subcore_axis_name="s")

    @functools.partial(
        pl.kernel, mesh=mesh,
        out_type=jax.ShapeDtypeStruct((TOP_K, t, dp), ys.dtype),
        scratch_types=[pltpu.VMEM((TOP_K, n_ch, SC_CHUNK), jnp.int32),
                       pltpu.VMEM((SC_BUFFERS, SC_CHUNK, dp), ys.dtype),
                       pltpu.SemaphoreType.DMA((SC_BUFFERS,)), pltpu.SemaphoreType.DMA((SC_BUFFERS,))],
    )
    def gather(ys_hbm, p0_hbm, p1_hbm, out_hbm, idx_v, rows_v, gsem, wsem):
        wid = lax.axis_index("s") * ncores + lax.axis_index("c")
        base = wid * per_w
        pltpu.sync_copy(p0_hbm.at[wid], idx_v.at[0])
        pltpu.sync_copy(p1_hbm.at[wid], idx_v.at[1])
        units = [(c, k) for c in range(n_ch) for k in range(TOP_K)]

        def fetch(u):
            c, k = units[u]
            return pltpu.async_copy(ys_hbm.at[idx_v.at[k, c]], rows_v.at[u % SC_BUFFERS], gsem.at[u % SC_BUFFERS])

        fetches = {u: fetch(u) for u in range(min(SC_BUFFERS - 1, len(units)))}
        writes = {}
        for u, (c, k) in enumerate(units):
            fetches[u].wait()
            writes[u] = pltpu.async_copy(rows_v.at[u % SC_BUFFERS],
                                         out_hbm.at[k, pl.ds(base + c * SC_CHUNK, SC_CHUNK)],
                                         wsem.at[u % SC_BUFFERS])
            if u >= 1:
                writes[u - 1].wait()
            if u + SC_BUFFERS - 1 < len(units):
                fetches[u + SC_BUFFERS - 1] = fetch(u + SC_BUFFERS - 1)
        writes[len(units) - 1].wait()

    return gather(ys, idx[0], idx[1])


def _final_kernel(h_ref, y_ref, route_ref, gain_ref, *rest):
    out_ref = rest[-1]
    r = route_ref[...]
    y = h_ref[...]
    for k, gate_lane in enumerate((ROUTE_G1, ROUTE_G2)):
        y = y + r[:, gate_lane:gate_lane + 1] * _unpack_bf16_pairs(y_ref[k])
    out_ref[...] = y * lax.rsqrt(jnp.mean(y * y, axis=-1, keepdims=True) + NORM_EPS) * gain_ref[...]


def _final(h, y_rows, route, gain, tm, part, out_prev):
    t, d = h.shape
    t_part = y_rows.shape[1]
    first = part * (t_part // tm)
    here = lambda i: (first + i, 0)
    in_specs = [pl.BlockSpec((tm, d), here),
                pl.BlockSpec((TOP_K, tm, d // 2), lambda i: (0, i, 0)),
                pl.BlockSpec((tm, LANES), here),
                pl.BlockSpec((1, d), lambda i: (0, 0))]
    args = [h, y_rows, route, gain.reshape(1, d)]
    aliases = {}
    if out_prev is not None:
        in_specs.append(pl.BlockSpec(memory_space=pl.ANY))
        args.append(out_prev)
        aliases = {len(args) - 1: 0}
    return pl.pallas_call(
        _final_kernel,
        grid=(t_part // tm,),
        in_specs=in_specs,
        out_specs=pl.BlockSpec((tm, d), here),
        out_shape=jax.ShapeDtypeStruct((t, d), F32),
        input_output_aliases=aliases,
        compiler_params=_cparams(("arbitrary",)),
        name=f"final_{part}",
    )(*args)


def _layer(h3, mix_gain, w_in, attn_gain, decay_f, decay_b, ret_gain, w_out, ffn_gain,
           w_rg, b_rg, w_re, b_re, w_eg, w_eu, w_ed, final_gain):
    b, s, d = h3.shape
    t = b * s
    tm = 512
    attn_qkv, (qr, kr, vr, gr) = _inproj(h3, mix_gain, w_in.astype(BF16), 1024)

    o_list, lse_list = [], []
    for window, dilation in DILATED_BRANCHES:
        reach = (window // 2) // dilation
        o, lse = _attn_branch(*attn_qkv[dilation], dilation, reach, tq=128, tqb=1024, sub_tiles=8)
        o_list.append(o)
        lse_list.append(lse)

    lg_f = jnp.log1p(-jnp.exp2(decay_f.astype(F32)))
    lg_b = jnp.log1p(-jnp.exp2(decay_b.astype(F32)))
    orr = _retention(qr, kr, vr, gr, lg_f, lg_b, ret_gain).reshape(t, HEAD_GROUP_WIDTH)

    n_route = MOE_GROUPS + N_EXPERTS
    w_router = jnp.zeros((d, LANES), F32).at[:, :n_route].set(jnp.concatenate([w_re, w_rg], axis=1).astype(F32))
    b_router = jnp.zeros((1, LANES), F32).at[0, :n_route].set(jnp.concatenate([b_re, b_rg]).astype(F32))
    h, hn_packed, logits = _outproj(h3.reshape(t, d), o_list, lse_list, orr, attn_gain, w_out.astype(BF16),
                                    ffn_gain, w_router, b_router, 2 * tm)

    route, route_t, counts_rep = _route(logits, 2 * tm)

    n_blocks = -(-TOP_K * t // MOE_BLOCK) + N_EXPERTS
    counts = counts_rep[:, 0].astype(jnp.int32)
    aligned = ((counts + SEG_ALIGN - 1) // SEG_ALIGN) * SEG_ALIGN
    seg_start = jnp.cumsum(aligned) - aligned
    fields = jnp.transpose(route_t, (1, 0, 2)).reshape(SUBLANES, t)
    e12 = fields[ROUTE_E1:ROUTE_E2 + 1].astype(jnp.int32)
    r12 = fields[ROUTE_R1:ROUTE_R2 + 1].astype(jnp.int32)
    pos = r12 + jnp.sum(jnp.where(e12[None] == jnp.arange(N_EXPERTS, dtype=jnp.int32)[:, None, None],
                                  seg_start[:, None, None], 0), axis=0)
    nblk = (counts + MOE_BLOCK - 1) // MOE_BLOCK
    blk_end = jnp.cumsum(nblk)
    blk_start = blk_end - nblk
    blk = jnp.arange(n_blocks, dtype=jnp.int32)[:, None]
    owner = (blk >= blk_start) & (blk < blk_end)
    local = (blk - blk_start) * MOE_BLOCK
    block_row0 = jnp.sum(jnp.where(owner, seg_start + local, 0), axis=-1).astype(jnp.int32)
    block_nreal = jnp.sum(jnp.where(owner, jnp.clip(counts - local, 0, MOE_BLOCK), 0), axis=-1).astype(jnp.int32)
    block_expert = jnp.minimum(jnp.sum((blk >= blk_end).astype(jnp.int32), axis=-1), N_EXPERTS - 1)
    e_iota = jnp.arange(N_EXPERTS, dtype=jnp.int32)
    later = (e_iota[None, :] > e_iota[:, None]) & (counts[None, :] > 0)
    next_expert = jnp.where(jnp.any(later, axis=1), jnp.min(jnp.where(later, e_iota[None, :], N_EXPERTS), axis=1), -1)
    block_next = jnp.sum(jnp.where(owner, next_expert, 0), axis=-1).astype(jnp.int32)

    n_sorted = TOP_K * t + SORTED_TAIL
    dump_row = n_sorted
    seg_end = seg_start + counts
    hole = seg_end[:, None] + jnp.arange(SEG_ALIGN, dtype=jnp.int32)[None, :]
    hole = jnp.where(hole < (seg_start + aligned)[:, None], hole, dump_row).reshape(-1)
    tail = (seg_start[-1] + aligned[-1]) + jnp.arange(SORTED_TAIL, dtype=jnp.int32)
    tail = jnp.where(tail < n_sorted, tail, dump_row)
    zero_pos = jnp.concatenate([hole, tail]).astype(jnp.int32)

    xs = _dispatch(hn_packed, pos, zero_pos)
    ys = _experts(xs, block_expert, block_nreal, block_row0, block_next, w_eg.astype(F32), w_eu.astype(F32),
                  w_ed.astype(F32))
    out = None
    t_part = t // COMBINE_PARTS
    for part in range(COMBINE_PARTS):
        y_rows = _combine_gather(ys, pos[:, part * t_part:(part + 1) * t_part])
        out = _final(h, y_rows, route, final_gain, tm, part, out)
    return out.reshape(b, s, d)


def kernel(x, mix_norm_gain, w_in, attn_out_gain, ret_decay_fwd, ret_decay_bwd, ret_out_gain, w_out,
           ffn_norm_gain, w_route_group, b_route_group, w_route_expert, b_route_expert,
           w_expert_gate, w_expert_up, w_expert_down, final_norm_gain):
    depth = mix_norm_gain.shape[0]
    assert depth == 1, "the final rmsnorm is fused into the single layer's combine kernel"
    l = 0
    return _layer(x, mix_norm_gain[l], w_in[l], attn_out_gain[l], ret_decay_fwd[l], ret_decay_bwd[l],
                  ret_out_gain[l], w_out[l], ffn_norm_gain[l], w_route_group[l], b_route_group[l],
                  w_route_expert[l], b_route_expert[l], w_expert_gate[l], w_expert_up[l], w_expert_down[l],
                  final_norm_gain)
```

```python
import functools

import numpy as np
import jax
import jax.numpy as jnp
from jax import lax
from jax.experimental import pallas as pl
from jax.experimental.pallas import tpu as pltpu
from jax.experimental.pallas import tpu_sc as plsc

F32 = jnp.float32
BF16 = jnp.bfloat16

ATTN_HEADS = 8
HEAD_DIM = 64
RET_HEADS = 8
HEAD_GROUP_WIDTH = 512
N_PROJ_GROUPS = 7
DILATED_BRANCHES = ((128, 1), (512, 4), (2048, 16))
ROPE_THETA = 500000.0
ROPE_DIM = HEAD_DIM // 4
RET_THETA = 10000.0
RET_CHUNK = 128
MOE_GROUPS = 4
EXPERTS_PER_GROUP = 8
N_EXPERTS = MOE_GROUPS * EXPERTS_PER_GROUP
MOE_BLOCK = 512
NORM_EPS = 1e-6
NEG_INF = -1e30

LANES = 128
VMEM_LIMIT = 56 * 1024 * 1024


def _cparams(sem):
    return pltpu.CompilerParams(dimension_semantics=sem, vmem_limit_bytes=VMEM_LIMIT)


def _rotary_tables(seq, half, freqs):
    pos = np.arange(seq, dtype=np.float64)[:, None]
    ang = pos * freqs[None, :].astype(np.float64)
    cos, sin = np.cos(ang), np.sin(ang)
    c = np.ones((seq, HEAD_DIM)); sp = np.zeros((seq, HEAD_DIM)); sm = np.zeros((seq, HEAD_DIM))
    c[:, :half] = cos; c[:, half:2 * half] = cos
    sp[:, half:2 * half] = sin
    sm[:, :half] = -sin
    rep = LANES // HEAD_DIM
    return tuple(jnp.asarray(np.tile(t, (1, rep)), F32) for t in (c, sp, sm))


def _rotate(t, c, sp, sm, half):
    outs = []
    for g in range(t.shape[1] // LANES):
        tg = t[:, g * LANES:(g + 1) * LANES]
        outs.append(tg * c + pltpu.roll(tg, half, 1) * sp + pltpu.roll(tg, LANES - half, 1) * sm)
    return jnp.concatenate(outs, axis=1)


CLASS_DILATIONS = tuple(d for _, d in DILATED_BRANCHES if d > 1)
ATTN_Q_SCALE = float(np.log2(np.e)) * HEAD_DIM ** -0.5


def _inproj_kernel(x_ref, gain_ref, w_ref, ca_ref, spa_ref, sma_ref, cr_ref, spr_ref, smr_ref, *rest):
    n_cls = len(CLASS_DILATIONS)
    nat_refs = rest[0:3]
    cls_refs = [rest[3 + 3 * c:6 + 3 * c] for c in range(n_cls)]
    qr_ref, kr_ref, vr_ref, gr_ref = rest[3 + 3 * n_cls:7 + 3 * n_cls]
    stage_ref = rest[7 + 3 * n_cls]
    x = x_ref[0]
    tm = x.shape[0]
    ms = jnp.mean(x * x, axis=-1, keepdims=True)
    xn = (x * lax.rsqrt(ms + NORM_EPS) * gain_ref[...]).astype(BF16)
    gw = HEAD_GROUP_WIDTH

    def proj(c):
        return jnp.dot(xn, w_ref[:, c * gw:(c + 1) * gw], preferred_element_type=F32)

    a_tabs = (ca_ref[...], spa_ref[...], sma_ref[...])
    r_tabs = (cr_ref[...], spr_ref[...], smr_ref[...])
    attn_vals = ((_rotate(proj(0), *a_tabs, ROPE_DIM // 2) * ATTN_Q_SCALE),
                 _rotate(proj(1), *a_tabs, ROPE_DIM // 2),
                 proj(2))
    for j, val in enumerate(attn_vals):
        nat_refs[j][0] = val.astype(BF16)
        for g in range(gw // LANES):
            stage_ref[g] = val[:, g * LANES:(g + 1) * LANES]
        for c, d in enumerate(CLASS_DILATIONS):
            for r in range(d):
                for g in range(gw // LANES):
                    col = r * gw + g * LANES
                    cls_refs[c][j][0, :, col:col + LANES] = (
                        stage_ref[g, pl.ds(r, tm // d, stride=d), :].astype(BF16))
    qr_ref[0] = _rotate(proj(3), *r_tabs, HEAD_DIM // 2).astype(BF16)
    kr_ref[0] = (_rotate(proj(4), *r_tabs, HEAD_DIM // 2) * (HEAD_DIM ** -0.5)).astype(BF16)
    vr_ref[0] = proj(5).astype(BF16)
    g = proj(6)
    gr_ref[0] = (g * jax.nn.sigmoid(g)).astype(BF16)


def _inproj(x, gain, w_in_bf16, tm):
    b, s, d = x.shape
    rope_freqs = ROPE_THETA ** (-np.arange(0, ROPE_DIM, 2, dtype=np.float32) / ROPE_DIM)
    ret_freqs = RET_THETA ** (-np.linspace(0.0, 1.0, HEAD_DIM // 2, dtype=np.float32))
    tabs = _rotary_tables(s, ROPE_DIM // 2, rope_freqs) + _rotary_tables(s, HEAD_DIM // 2, ret_freqs)
    gw = HEAD_GROUP_WIDTH
    tab_spec = pl.BlockSpec((tm, LANES), lambda si, bi: (si, 0))

    def view(dil):
        return (pl.BlockSpec((1, tm // dil, dil * gw), lambda si, bi: (bi, si, 0)),
                jax.ShapeDtypeStruct((b, s // dil, dil * gw), BF16))

    views = [view(1)] * 3 + [view(dil) for dil in CLASS_DILATIONS for _ in range(3)] + [view(1)] * 4
    outs = pl.pallas_call(
        _inproj_kernel,
        grid=(s // tm, b),
        in_specs=[pl.BlockSpec((1, tm, d), lambda si, bi: (bi, si, 0)),
                  pl.BlockSpec((1, d), lambda si, bi: (0, 0)),
                  pl.BlockSpec(w_in_bf16.shape, lambda si, bi: (0, 0))] + [tab_spec] * 6,
        out_specs=[v[0] for v in views],
        out_shape=[v[1] for v in views],
        scratch_shapes=[pltpu.VMEM((gw // LANES, tm, LANES), F32)],
        compiler_params=_cparams(("arbitrary", "arbitrary")),
        name="inproj",
    )(x, gain.reshape(1, d), w_in_bf16, *tabs)
    n_attn = 3 * (1 + len(CLASS_DILATIONS))
    attn_qkv = {dil: outs[3 * c:3 * c + 3] for c, dil in enumerate((1,) + CLASS_DILATIONS)}
    return attn_qkv, outs[n_attn:]


ATTN_WINDOW_CASES = 3


def _attn_kernel(q_ref, k_ref, v_ref, o_ref, lse_ref, bias_ref, *, length, tq, reach):
    tqb = q_ref.shape[1]
    win = tq + 2 * reach
    heads_per_pair = LANES // HEAD_DIM
    qi = pl.program_id(2)
    lane = lax.broadcasted_iota(jnp.int32, (1, LANES), 1)
    lane_t = lax.broadcasted_iota(jnp.int32, (tq, LANES), 1)
    ones = jnp.ones((win, LANES), BF16)

    @pl.when((pl.program_id(0) == 0) & (pl.program_id(1) == 0) & (qi == 0))
    def _():
        diff = (lax.broadcasted_iota(jnp.int32, (heads_per_pair * tq, win), 1)
                - (lax.broadcasted_iota(jnp.int32, (heads_per_pair * tq, win), 0) & (tq - 1)))
        for case in range(ATTN_WINDOW_CASES):
            off = case * reach
            bias_ref[case] = jnp.where((diff >= off - reach) & (diff <= off + reach), 0.0, NEG_INF)

    def sub(t, cls):
        q0 = qi * tqb + t * tq
        ws = jnp.clip(q0 - reach, 0, length - win)
        ws = pl.multiple_of(ws, reach)
        bias = bias_ref[(q0 - ws) // reach]
        rows = pl.ds(t * tq, tq)
        m_tile = jnp.zeros((tq, LANES), F32)
        l_tile = jnp.ones((tq, LANES), F32)
        for g in range(HEAD_GROUP_WIDTH // LANES):
            cols = slice(cls * HEAD_GROUP_WIDTH + g * LANES, cls * HEAD_GROUP_WIDTH + (g + 1) * LANES)
            qg = q_ref[0, rows, cols]
            kw = k_ref[0, pl.ds(ws, win), cols]
            v_ones = jnp.concatenate([v_ref[0, pl.ds(ws, win), cols], ones], axis=1)
            hms = [(lane >= hh * HEAD_DIM) & (lane < (hh + 1) * HEAD_DIM) for hh in range(heads_per_pair)]
            q2 = jnp.concatenate([qg * hm.astype(BF16) for hm in hms], axis=0)
            sc = lax.dot_general(q2, kw, (((1,), (1,)), ((), ())), preferred_element_type=F32) + bias
            m = jnp.max(sc, axis=-1, keepdims=True)
            p = jnp.exp2(sc - m).astype(BF16)
            pv = jnp.dot(p, v_ones, preferred_element_type=F32)
            l = pv[:, LANES:]
            o = pv[:, :LANES] / l
            o_pair = jnp.zeros((tq, LANES), F32)
            for hh in range(heads_per_pair):
                part = slice(hh * tq, (hh + 1) * tq)
                head_lane = lane_t == g * heads_per_pair + hh
                o_pair = jnp.where(hms[hh], o[part], o_pair)
                m_tile = jnp.where(head_lane, m[part], m_tile)
                l_tile = jnp.where(head_lane, l[part], l_tile)
            o_ref[0, rows, cols] = o_pair.astype(BF16)
        lse_ref[0, rows, cls * LANES:(cls + 1) * LANES] = m_tile + jnp.log2(l_tile)

    for cls in range(q_ref.shape[2] // HEAD_GROUP_WIDTH):
        for t in range(tqb // tq):
            sub(t, cls)


def _attn_branch(qc, kc, vc, dilation, reach, tq, tqb, sub_tiles=4):
    b, length, dw = qc.shape
    w = dw // dilation
    tqb = min(tqb, length)
    assert tq % reach == 0 and tq > reach and length % tq == 0 and length >= tq + 2 * reach
    ncls = min(dilation, max(1, sub_tiles // (tqb // tq)))
    q_spec = pl.BlockSpec((1, tqb, ncls * w), lambda bi, r, qi: (bi, qi, r))
    kv_spec = pl.BlockSpec((1, length, ncls * w), lambda bi, r, qi: (bi, 0, r))
    o, lse = pl.pallas_call(
        functools.partial(_attn_kernel, length=length, tq=tq, reach=reach),
        grid=(b, dilation // ncls, length // tqb),
        in_specs=[q_spec, kv_spec, kv_spec],
        out_specs=[q_spec, pl.BlockSpec((1, tqb, ncls * LANES), lambda bi, r, qi: (bi, qi, r))],
        out_shape=[jax.ShapeDtypeStruct((b, length, dilation * w), BF16),
                   jax.ShapeDtypeStruct((b, length, dilation * LANES), F32)],
        scratch_shapes=[pltpu.VMEM((ATTN_WINDOW_CASES, (LANES // HEAD_DIM) * tq, tq + 2 * reach), F32)],
        compiler_params=_cparams(("arbitrary", "arbitrary", "arbitrary")),
        name=f"attn_d{dilation}",
    )(qc, kc, vc)
    return o, lse


RET_TAB_QF, RET_TAB_QB, RET_TAB_KF, RET_TAB_KB = range(4)


def _retention_kernel(lgf_ref, lgb_ref, q_ref, k_ref, v_ref, g_ref, gain_ref, o_ref,
                      tab_ref, dec_ref, sb_ref, st_ref, *, chunk, unroll):
    c = chunk
    n = q_ref.shape[1] // c
    width = q_ref.shape[2]
    n_pairs = width // LANES
    heads_per_pair = LANES // HEAD_DIM
    n_heads = n_pairs * heads_per_pair
    head0 = pl.program_id(1) * n_heads
    lane_w = lax.broadcasted_iota(jnp.int32, (1, width), 1)
    lgf = [lgf_ref[head0 + hd] for hd in range(n_heads)]
    lgb = [lgb_ref[head0 + hd] for hd in range(n_heads)]
    lgf_lane = jnp.zeros((1, width), F32)
    lgb_lane = jnp.zeros((1, width), F32)
    for hd in range(n_heads):
        in_head = (lane_w >= hd * HEAD_DIM) & (lane_w < (hd + 1) * HEAD_DIM)
        lgf_lane = jnp.where(in_head, lgf[hd], lgf_lane)
        lgb_lane = jnp.where(in_head, lgb[hd], lgb_lane)
    idx = lax.broadcasted_iota(jnp.int32, (c, width), 0).astype(F32)
    tab_ref[RET_TAB_QF] = jnp.exp((idx + 1.0) * lgf_lane)
    tab_ref[RET_TAB_QB] = jnp.exp((c - idx) * lgb_lane)
    tab_ref[RET_TAB_KF] = jnp.exp((c - 1.0 - idx) * lgf_lane)
    tab_ref[RET_TAB_KB] = jnp.exp(idx * lgb_lane)
    sdf = jnp.exp(c * lgf_lane)
    sdb = jnp.exp(c * lgb_lane)
    dmat = (lax.broadcasted_iota(jnp.int32, (c, c), 0)
            - lax.broadcasted_iota(jnp.int32, (c, c), 1)).astype(F32)
    for hd in range(n_heads):
        dec_ref[hd // heads_per_pair, :, (hd % heads_per_pair) * c:(hd % heads_per_pair + 1) * c] = (
            jnp.where(dmat >= 0, jnp.exp(dmat * lgf[hd]), jnp.exp(-dmat * lgb[hd])))
    lane = lax.broadcasted_iota(jnp.int32, (1, LANES), 1)
    lane_hi = lane >= HEAD_DIM
    head_masks = [((lane >= hh * HEAD_DIM) & (lane < (hh + 1) * HEAD_DIM)).astype(BF16)
                  for hh in range(heads_per_pair)]
    row_hi = lax.broadcasted_iota(jnp.int32, (LANES, LANES), 0) >= HEAD_DIM
    col_hi = lax.broadcasted_iota(jnp.int32, (LANES, LANES), 1) >= HEAD_DIM
    blockdiag = row_hi == col_hi

    def kv_state(kd, vv):
        kt = jnp.transpose(kd).astype(BF16)
        return jnp.where(blockdiag, jnp.dot(kt, vv, preferred_element_type=F32), 0.0)

    st_ref[...] = jnp.zeros_like(st_ref)
    sb_ref[n - 1] = jnp.zeros(sb_ref.shape[1:], sb_ref.dtype)

    def back(i, carry):
        nn = n - 1 - i
        rows = pl.ds(pl.multiple_of(nn * c, c), c)
        for p in range(n_pairs):
            cols = slice(p * LANES, (p + 1) * LANES)
            kd = k_ref[0, rows, cols].astype(F32) * tab_ref[RET_TAB_KB, :, cols]
            new = st_ref[p] * sdb[:, cols] + kv_state(kd, v_ref[0, rows, cols])
            st_ref[p] = new
            sb_ref[nn - 1, p] = new.astype(BF16)
        return carry

    lax.fori_loop(0, n - 1, back, 0, unroll=unroll)

    st_ref[...] = jnp.zeros_like(st_ref)

    def fwd(nn, carry):
        rows = pl.ds(pl.multiple_of(nn * c, c), c)
        for p in range(n_pairs):
            cols = slice(p * LANES, (p + 1) * LANES)
            qq = q_ref[0, rows, cols]
            kk = k_ref[0, rows, cols]
            vv = v_ref[0, rows, cols]
            qf = qq.astype(F32)
            sf = st_ref[p]
            qcat = jnp.concatenate([(qf * tab_ref[RET_TAB_QF, :, cols]).astype(BF16),
                                    (qf * tab_ref[RET_TAB_QB, :, cols]).astype(BF16)], axis=1)
            scat = jnp.concatenate([sf.astype(BF16), sb_ref[nn, p]], axis=0)
            o = jnp.dot(qcat, scat, preferred_element_type=F32)
            k2 = jnp.concatenate([kk * hm for hm in head_masks], axis=0)
            v2 = jnp.concatenate([vv * hm for hm in head_masks], axis=0)
            sc = lax.dot_general(qq, k2, (((1,), (1,)), ((), ())), preferred_element_type=F32)
            o = o + jnp.dot((sc * dec_ref[p]).astype(BF16), v2, preferred_element_type=F32)
            s_lo = jnp.sum(jnp.where(lane_hi, 0.0, o), axis=-1, keepdims=True)
            s_hi = jnp.sum(jnp.where(lane_hi, o, 0.0), axis=-1, keepdims=True)
            mu = jnp.where(lane_hi, s_hi, s_lo) * (1.0 / HEAD_DIM)
            dev = o - mu
            d2 = dev * dev
            v_lo = jnp.sum(jnp.where(lane_hi, 0.0, d2), axis=-1, keepdims=True)
            v_hi = jnp.sum(jnp.where(lane_hi, d2, 0.0), axis=-1, keepdims=True)
            var = jnp.where(lane_hi, v_hi, v_lo) * (1.0 / HEAD_DIM)
            out = dev * lax.rsqrt(var + NORM_EPS) * gain_ref[:, cols] * g_ref[0, rows, cols].astype(F32)
            o_ref[0, rows, cols] = out.astype(BF16)
            kd = kk.astype(F32) * tab_ref[RET_TAB_KF, :, cols]
            st_ref[p] = sf * sdf[:, cols] + kv_state(kd, vv)
        return carry

    lax.fori_loop(0, n, fwd, 0, unroll=unroll)


def _retention(qr, kr, vr, gate, lg_f, lg_b, out_gain, width=512, unroll=8):
    b, s, w = qr.shape
    n_pairs = width // LANES
    n_heads = width // HEAD_DIM
    spec = pl.BlockSpec((1, s, width), lambda bi, p, *_: (bi, 0, p))
    grid_spec = pltpu.PrefetchScalarGridSpec(
        num_scalar_prefetch=2,
        grid=(b, w // width),
        in_specs=[spec, spec, spec, spec, pl.BlockSpec((1, width), lambda bi, p, *_: (0, p))],
        out_specs=spec,
        scratch_shapes=[pltpu.VMEM((4, RET_CHUNK, width), F32),
                        pltpu.VMEM((n_pairs, RET_CHUNK, (LANES // HEAD_DIM) * RET_CHUNK), F32),
                        pltpu.VMEM((s // RET_CHUNK, n_pairs, LANES, LANES), BF16),
                        pltpu.VMEM((n_pairs, LANES, LANES), F32)],
    )
    return pl.pallas_call(
        functools.partial(_retention_kernel, chunk=RET_CHUNK, unroll=unroll),
        grid_spec=grid_spec,
        out_shape=jax.ShapeDtypeStruct((b, s, w), BF16),
        compiler_params=_cparams(("arbitrary", "arbitrary")),
        name="retention",
    )(lg_f, lg_b, qr, kr, vr, gate, out_gain.reshape(1, w))


def _split_bf16(t):
    hi = t.astype(BF16)
    lo = (t - hi.astype(F32)).astype(BF16)
    return hi, lo


def _pack_bf16_pairs(t):
    n = t.shape[1] // 2
    hi = pltpu.bitcast(t[:, :n].astype(BF16).astype(F32), jnp.uint32)
    lo = pltpu.bitcast(t[:, n:].astype(BF16).astype(F32), jnp.uint32)
    return hi | (lo >> 16)


def _unpack_bf16_pairs(u):
    hi = pltpu.bitcast(u & jnp.uint32(0xFFFF0000), F32)
    lo = pltpu.bitcast(u << 16, F32)
    return jnp.concatenate([hi, lo], axis=1)


def _outproj_kernel(x_ref, o1_ref, o2_ref, o3_ref, l1_ref, l2_ref, l3_ref, orr_ref, ga_ref, expand_ref,
                    wout_ref, gf_ref, wr_ref, br_ref, h_ref, hn_ref, logit_ref,
                    *nat_refs):
    tm = x_ref.shape[0]
    gw = HEAD_GROUP_WIDTH
    os, ls = [], []
    for (_, dil), o_ref, l_ref in zip(DILATED_BRANCHES, (o1_ref, o2_ref, o3_ref), (l1_ref, l2_ref, l3_ref)):
        if dil == 1:
            os.append(o_ref[...].astype(F32))
            ls.append(l_ref[...])
            continue
        c = CLASS_DILATIONS.index(dil)
        onat_ref, lnat_ref = nat_refs[2 * c], nat_refs[2 * c + 1]
        for r in range(dil):
            rows = pl.ds(r, tm // dil, stride=dil)
            for g in range(gw // LANES):
                col = r * gw + g * LANES
                onat_ref[g, rows, :] = o_ref[:, col:col + LANES].astype(F32)
            lnat_ref[rows, :] = l_ref[:, r * LANES:(r + 1) * LANES]
        os.append(jnp.concatenate([onat_ref[g] for g in range(gw // LANES)], axis=1))
        ls.append(lnat_ref[...])
    mx = jnp.maximum(jnp.maximum(ls[0], ls[1]), ls[2])
    es = [jnp.exp2(l - mx) for l in ls]
    inv = 1.0 / (es[0] + es[1] + es[2])
    expand = expand_ref[...]
    oa = jnp.zeros((tm, gw), F32)
    for e, o in zip(es, os):
        wexp = jnp.dot(jnp.concatenate(_split_bf16(e * inv), axis=1), expand, preferred_element_type=F32)
        oa = oa + wexp * o
    oa = oa * lax.rsqrt(jnp.mean(oa * oa, axis=-1, keepdims=True) + NORM_EPS) * ga_ref[...]
    mixed = jnp.concatenate([oa.astype(BF16), orr_ref[...]], axis=1)
    h = x_ref[...] + jnp.dot(mixed, wout_ref[...], preferred_element_type=F32)
    h_ref[...] = h
    hn = h * lax.rsqrt(jnp.mean(h * h, axis=-1, keepdims=True) + NORM_EPS) * gf_ref[...]
    hn_ref[...] = _pack_bf16_pairs(hn)
    prod = jnp.dot(jnp.concatenate(_split_bf16(hn), axis=0), wr_ref[...], preferred_element_type=F32)
    logit_ref[...] = prod[:tm, :LANES] + prod[:tm, LANES:] + prod[tm:, :LANES] + br_ref[...]


def _outproj(x2, o_list, lse_list, orr, attn_gain, w_out_bf16, ffn_gain, w_router, b_router, tm):
    t, d = x2.shape
    w = HEAD_GROUP_WIDTH
    expand = np.zeros((LANES, w), np.float32)
    for hd in range(ATTN_HEADS):
        expand[hd, hd * HEAD_DIM:(hd + 1) * HEAD_DIM] = 1.0
    expand = jnp.asarray(np.concatenate([expand, expand], axis=0), BF16)
    wr_hi = w_router.astype(BF16)
    wr = jnp.concatenate([wr_hi, (w_router - wr_hi.astype(F32)).astype(BF16)], axis=1)
    row = lambda width, dil=1: pl.BlockSpec((tm // dil, dil * width), lambda i: (i, 0))
    full = lambda a: pl.BlockSpec(a.shape, lambda i: (0,) * a.ndim)
    ga = attn_gain.reshape(1, w)
    gf = ffn_gain.reshape(1, d)
    dils = [dil for _, dil in DILATED_BRANCHES]
    o_flat = [o.reshape(t // dil, dil * w) for o, dil in zip(o_list, dils)]
    l_flat = [l.reshape(t // dil, dil * LANES) for l, dil in zip(lse_list, dils)]
    nat_scratch = []
    for _ in CLASS_DILATIONS:
        nat_scratch += [pltpu.VMEM((w // LANES, tm, LANES), F32), pltpu.VMEM((tm, LANES), F32)]
    return pl.pallas_call(
        _outproj_kernel,
        grid=(t // tm,),
        in_specs=[row(d)] + [row(w, dil) for dil in dils] + [row(LANES, dil) for dil in dils] + [row(w)]
                 + [full(ga), full(expand), full(w_out_bf16), full(gf), full(wr), full(b_router)],
        out_specs=[row(d), row(d // 2), row(LANES)],
        out_shape=[jax.ShapeDtypeStruct((t, d), F32),
                   jax.ShapeDtypeStruct((t, d // 2), jnp.uint32),
                   jax.ShapeDtypeStruct((t, LANES), F32)],
        scratch_shapes=nat_scratch,
        compiler_params=_cparams(("arbitrary",)),
        name="outproj",
    )(x2, *o_flat, *l_flat, orr, ga, expand, w_out_bf16, gf, wr, b_router)


ROUTE_E1, ROUTE_E2, ROUTE_G1, ROUTE_G2, ROUTE_R1, ROUTE_R2 = range(6)
GROUP_LANE0 = N_EXPERTS
SUBLANES = 8


def _route_kernel(logit_ref, tri_ref, route_ref, route_t_ref, count_ref, run_ref):
    @pl.when(pl.program_id(0) == 0)
    def _():
        run_ref[...] = jnp.zeros_like(run_ref)

    lg_t = jnp.transpose(logit_ref[...])
    tm = lg_t.shape[1]
    assert EXPERTS_PER_GROUP == SUBLANES and MOE_GROUPS <= SUBLANES
    rid = lax.broadcasted_iota(jnp.int32, (SUBLANES, tm), 0)
    big = jnp.int32(1 << 20)

    def top(vals):
        m = jnp.max(vals, axis=0, keepdims=True)
        i = jnp.min(jnp.where(vals == m, rid, big), axis=0, keepdims=True)
        return m, i

    gl = jnp.where(rid < MOE_GROUPS, lg_t[GROUP_LANE0:GROUP_LANE0 + SUBLANES], -jnp.inf)
    gmax, gidx = top(gl)
    group_gate = 1.0 / jnp.sum(jnp.exp(gl - gmax), axis=0, keepdims=True)
    el = lg_t[0:EXPERTS_PER_GROUP]
    for g in range(1, MOE_GROUPS):
        el = jnp.where(gidx == g, lg_t[g * EXPERTS_PER_GROUP:(g + 1) * EXPERTS_PER_GROUP], el)
    t1, i1 = top(el)
    t2, i2 = top(jnp.where(rid == i1, -jnp.inf, el))
    e21 = jnp.exp(t2 - t1)
    g1 = group_gate / (1.0 + e21)
    g2 = group_gate * e21 / (1.0 + e21)
    e1 = gidx * EXPERTS_PER_GROUP + i1
    e2 = gidx * EXPERTS_PER_GROUP + i2
    erow = lax.broadcasted_iota(jnp.int32, (N_EXPERTS, tm), 0)
    oh1 = erow == e1
    oh2 = erow == e2
    cnt = oh1.astype(F32) + oh2.astype(F32)
    run = run_ref[:, 0:1]
    prefix = jnp.dot(cnt.astype(BF16), tri_ref[...], preferred_element_type=F32) + run
    r1 = jnp.sum(jnp.where(oh1, prefix, 0.0), axis=0, keepdims=True)
    r2 = jnp.sum(jnp.where(oh2, prefix, 0.0), axis=0, keepdims=True)
    new_run = jnp.broadcast_to(run + jnp.sum(cnt, axis=1, keepdims=True), run_ref.shape)
    run_ref[...] = new_run
    count_ref[...] = new_run
    out = jnp.zeros((SUBLANES, tm), F32)
    for row, val in ((ROUTE_E1, e1.astype(F32)), (ROUTE_E2, e2.astype(F32)), (ROUTE_G1, g1),
                     (ROUTE_G2, g2), (ROUTE_R1, r1), (ROUTE_R2, r2)):
        out = jnp.where(rid == row, val, out)
    route_t_ref[0] = out
    out_t = jnp.concatenate([out, jnp.zeros((LANES - SUBLANES, tm), F32)], axis=0)
    route_ref[...] = jnp.transpose(out_t)


def _route(logits, tm):
    t = logits.shape[0]
    tri = jnp.asarray(np.triu(np.ones((tm, tm), np.float32), 1), BF16)
    return pl.pallas_call(
        _route_kernel,
        grid=(t // tm,),
        in_specs=[pl.BlockSpec((tm, LANES), lambda i: (i, 0)), pl.BlockSpec((tm, tm), lambda i: (0, 0))],
        out_specs=[pl.BlockSpec((tm, LANES), lambda i: (i, 0)),
                   pl.BlockSpec((1, SUBLANES, tm), lambda i: (i, 0, 0)),
                   pl.BlockSpec((N_EXPERTS, LANES), lambda i: (0, 0))],
        out_shape=[jax.ShapeDtypeStruct((t, LANES), F32),
                   jax.ShapeDtypeStruct((t // tm, SUBLANES, tm), F32),
                   jax.ShapeDtypeStruct((N_EXPERTS, LANES), F32)],
        scratch_shapes=[pltpu.VMEM((N_EXPERTS, LANES), F32)],
        compiler_params=_cparams(("arbitrary",)),
        name="route",
    )(logits, tri)


TOP_K = 2


SEG_ALIGN = 8
SORTED_TAIL = N_EXPERTS * SEG_ALIGN + MOE_BLOCK


def _tail_pieces():
    full, rest = divmod(SORTED_TAIL, MOE_BLOCK)
    return [MOE_BLOCK] * full + ([rest] if rest else [])


SC_CHUNK = 64
SC_BUFFERS = 3
ZERO_ROWS = N_EXPERTS * SEG_ALIGN + SORTED_TAIL
SORTED_ROWS_EXTRA = SORTED_TAIL + SEG_ALIGN


def _sc_workers():
    info = plsc.get_sparse_core_info()
    return info.num_cores, info.num_cores * info.num_subcores


def _dispatch(hn_packed, pos, zero_pos):
    t, dp = hn_packed.shape
    ncores, nw = _sc_workers()
    per_w = t // nw
    n_ch = per_w // SC_CHUNK
    z_rows = ZERO_ROWS // nw
    assert per_w % SC_CHUNK == 0 and ZERO_ROWS % nw == 0 and z_rows % SEG_ALIGN == 0 and TOP_K == 2
    idx = [pos[k].reshape(nw, n_ch, SC_CHUNK) for k in range(TOP_K)]
    zeros = jnp.zeros((z_rows, dp), hn_packed.dtype)
    mesh = plsc.VectorSubcoreMesh(core_axis_name="c", subcore_axis_name="s")

    @functools.partial(
        pl.kernel, mesh=mesh,
        out_type=jax.ShapeDtypeStruct((TOP_K * t + SORTED_ROWS_EXTRA, dp), hn_packed.dtype),
        scratch_types=[pltpu.VMEM((n_ch, SC_CHUNK), jnp.int32), pltpu.VMEM((n_ch, SC_CHUNK), jnp.int32),
                       pltpu.VMEM((z_rows,), jnp.int32),
                       pltpu.VMEM((SC_BUFFERS, SC_CHUNK, dp), hn_packed.dtype),
                       pltpu.VMEM((z_rows, dp), hn_packed.dtype),
                       pltpu.SemaphoreType.DMA((SC_BUFFERS,)), pltpu.SemaphoreType.DMA((SC_BUFFERS,)),
                       pltpu.SemaphoreType.DMA],
    )
    def scatter(hn_hbm, p0_hbm, p1_hbm, zpos_hbm, zeros_hbm, xs_hbm, i0_v, i1_v, iz_v, rows_v, zero_v,
                lsem, ssem, zsem):
        wid = lax.axis_index("s") * ncores + lax.axis_index("c")
        base = wid * per_w
        zero_load = pltpu.async_copy(zeros_hbm, zero_v, zsem)
        pltpu.sync_copy(zpos_hbm.at[wid], iz_v)
        pltpu.sync_copy(p0_hbm.at[wid], i0_v)
        pltpu.sync_copy(p1_hbm.at[wid], i1_v)

        def load(c):
            return pltpu.async_copy(hn_hbm.at[pl.ds(base + c * SC_CHUNK, SC_CHUNK)], rows_v.at[c % SC_BUFFERS],
                                    lsem.at[c % SC_BUFFERS])

        loads = {c: load(c) for c in range(min(SC_BUFFERS - 1, n_ch))}
        scat = {}
        for c in range(n_ch):
            slot = c % SC_BUFFERS
            loads[c].wait()
            scat[c] = (pltpu.async_copy(rows_v.at[slot], xs_hbm.at[i0_v.at[c]], ssem.at[slot]),
                       pltpu.async_copy(rows_v.at[slot], xs_hbm.at[i1_v.at[c]], ssem.at[slot]))
            if c >= 1:
                for d in scat[c - 1]:
                    d.wait()
            if c + SC_BUFFERS - 1 < n_ch:
                loads[c + SC_BUFFERS - 1] = load(c + SC_BUFFERS - 1)
        zero_load.wait()
        zero_scatter = pltpu.async_copy(zero_v, xs_hbm.at[iz_v], zsem)
        for d in scat[n_ch - 1]:
            d.wait()
        zero_scatter.wait()

    return scatter(hn_packed, idx[0], idx[1], zero_pos.reshape(nw, z_rows), zeros)


def _expert_kernel(bexp_ref, nreal_ref, row0_ref, bnext_ref, xs_hbm, wg_hbm, wu_hbm, wd_hbm, ys_hbm,
                   xbuf, ybuf, wg_f32, wu_f32, wd_f32, wg_bf, wu_bf, wd_bf, isem, osem, wsem, *, n_rows):
    i = pl.program_id(0)
    nb = pl.num_programs(0)
    slot = i % 2
    nslot = 1 - slot
    n_cur = nreal_ref[i]
    prev = jnp.maximum(i - 1, 0)
    nxt = jnp.minimum(i + 1, nb - 1)

    def in_copy(blk, s):
        row0 = pl.multiple_of(row0_ref[blk], SEG_ALIGN)
        return pltpu.make_async_copy(xs_hbm.at[pl.ds(row0, MOE_BLOCK)], xbuf.at[s], isem.at[s])

    def out_copy(blk, s):
        row0 = pl.multiple_of(row0_ref[blk], SEG_ALIGN)
        return pltpu.make_async_copy(ybuf.at[s], ys_hbm.at[pl.ds(row0, MOE_BLOCK)], osem.at[s])

    @pl.when(i == 0)
    def _():
        ybuf[...] = jnp.zeros_like(ybuf)
        tails = [pltpu.make_async_copy(ybuf.at[s, pl.ds(0, size)], ys_hbm.at[pl.ds(n_rows + s * MOE_BLOCK, size)],
                                       osem.at[s])
                 for s, size in enumerate(_tail_pieces())]
        for tail in tails:
            tail.start()
        for tail in tails:
            tail.wait()

        @pl.when(n_cur > 0)
        def _():
            in_copy(i, slot).start()

    @pl.when((i + 1 < nb) & (nreal_ref[nxt] > 0))
    def _():
        in_copy(nxt, nslot).start()

    def weight_copies(e):
        return [pltpu.make_async_copy(w_hbm.at[e], w_f32, wsem.at[j])
                for j, (w_hbm, w_f32) in enumerate(((wg_hbm, wg_f32), (wu_hbm, wu_f32), (wd_hbm, wd_f32)))]

    @pl.when((i == 0) & (n_cur > 0))
    def _():
        for c in weight_copies(bexp_ref[0]):
            c.start()

    @pl.when((n_cur > 0) & ((i == 0) | (bexp_ref[i] != bexp_ref[prev])))
    def _():
        for c in weight_copies(bexp_ref[i]):
            c.wait()
        wg_bf[...] = wg_f32[...].astype(BF16)
        wu_bf[...] = wu_f32[...].astype(BF16)
        wd_bf[...] = wd_f32[...].astype(BF16)

        @pl.when(bnext_ref[i] >= 0)
        def _():
            for c in weight_copies(bnext_ref[i]):
                c.start()

    @pl.when(n_cur > 0)
    def _():
        in_copy(i, slot).wait()
        xb = _unpack_bf16_pairs(xbuf[slot]).astype(BF16)
        gate = jnp.dot(xb, wg_bf[...], preferred_element_type=F32)
        up = jnp.dot(xb, wu_bf[...], preferred_element_type=F32)
        hid = (gate * jax.nn.sigmoid(gate) * up).astype(BF16)
        ybuf[slot] = _pack_bf16_pairs(jnp.dot(hid, wd_bf[...], preferred_element_type=F32))

    @pl.when((i >= 1) & (nreal_ref[prev] > 0))
    def _():
        out_copy(prev, nslot).wait()

    @pl.when(n_cur > 0)
    def _():
        out_copy(i, slot).start()

        @pl.when(i == nb - 1)
        def _():
            out_copy(i, slot).wait()


def _experts(xs, block_expert, block_nreal, block_row0, block_next, wg, wu, wd):
    n_blocks = block_expert.shape[0]
    assert n_blocks >= 2 and len(_tail_pieces()) <= 2
    n_rows_pad = xs.shape[0] - SORTED_ROWS_EXTRA + SORTED_TAIL
    dp = xs.shape[1]
    _, d, ff = wg.shape
    any_spec = pl.BlockSpec(memory_space=pl.ANY)
    grid_spec = pltpu.PrefetchScalarGridSpec(
        num_scalar_prefetch=4,
        grid=(n_blocks,),
        in_specs=[any_spec, any_spec, any_spec, any_spec],
        out_specs=any_spec,
        scratch_shapes=[pltpu.VMEM((2, MOE_BLOCK, dp), jnp.uint32),
                        pltpu.VMEM((2, MOE_BLOCK, dp), jnp.uint32),
                        pltpu.VMEM((d, ff), F32),
                        pltpu.VMEM((d, ff), F32),
                        pltpu.VMEM((ff, d), F32),
                        pltpu.VMEM((d, ff), BF16),
                        pltpu.VMEM((d, ff), BF16),
                        pltpu.VMEM((ff, d), BF16),
                        pltpu.SemaphoreType.DMA((2,)),
                        pltpu.SemaphoreType.DMA((2,)),
                        pltpu.SemaphoreType.DMA((3,))],
    )
    return pl.pallas_call(
        functools.partial(_expert_kernel, n_rows=n_rows_pad - SORTED_TAIL),
        grid_spec=grid_spec,
        out_shape=jax.ShapeDtypeStruct((n_rows_pad, dp), jnp.uint32),
        compiler_params=_cparams(("arbitrary",)),
        name="experts",
    )(block_expert, block_nreal, block_row0, block_next, xs, wg, wu, wd)


def _combine_gather(ys, pos):
    t = pos.shape[1]
    dp = ys.shape[1]
    ncores, nw = _sc_workers()
    per_w = t // nw
    n_ch = per_w // SC_CHUNK
    assert per_w % SC_CHUNK == 0
    idx = [pos[k].reshape(nw, n_ch, SC_CHUNK) for k in range(TOP_K)]
    mesh = plsc.VectorSubcoreMesh(core_axis_name="c", subcore_axis_name="s")

    @functools.partial(
        pl.kernel, mesh=mesh,
        out_type=jax.ShapeDtypeStruct((TOP_K, t, dp), ys.dtype),
        scratch_types=[pltpu.VMEM((TOP_K, n_ch, SC_CHUNK), jnp.int32),
                       pltpu.VMEM((SC_BUFFERS, SC_CHUNK, dp), ys.dtype),
                       pltpu.SemaphoreType.DMA((SC_BUFFERS,)), pltpu.SemaphoreType.DMA((SC_BUFFERS,))],
    )
    def gather(ys_hbm, p0_hbm, p1_hbm, out_hbm, idx_v, rows_v, gsem, wsem):
        wid = lax.axis_index("s") * ncores + lax.axis_index("c")
        base = wid * per_w
        pltpu.sync_copy(p0_hbm.at[wid], idx_v.at[0])
        pltpu.sync_copy(p1_hbm.at[wid], idx_v.at[1])
        units = [(c, k) for c in range(n_ch) for k in range(TOP_K)]

        def fetch(u):
            c, k = units[u]
            return pltpu.async_copy(ys_hbm.at[idx_v.at[k, c]], rows_v.at[u % SC_BUFFERS], gsem.at[u % SC_BUFFERS])

        fetches = {u: fetch(u) for u in range(min(SC_BUFFERS - 1, len(units)))}
        writes = {}
        for u, (c, k) in enumerate(units):
            fetches[u].wait()
            writes[u] = pltpu.async_copy(rows_v.at[u % SC_BUFFERS],
                                         out_hbm.at[k, pl.ds(base + c * SC_CHUNK, SC_CHUNK)],
                                         wsem.at[u % SC_BUFFERS])
            if u >= 1:
                writes[u - 1].wait()
            if u + SC_BUFFERS - 1 < len(units):
                fetches[u + SC_BUFFERS - 1] = fetch(u + SC_BUFFERS - 1)
        writes[len(units) - 1].wait()

    return gather(ys, idx[0], idx[1])


def _final_kernel(h_ref, y_ref, route_ref, gain_ref, out_ref):
    r = route_ref[...]
    y = h_ref[...]
    for k, gate_lane in enumerate((ROUTE_G1, ROUTE_G2)):
        y = y + r[:, gate_lane:gate_lane + 1] * _unpack_bf16_pairs(y_ref[k])
    out_ref[...] = y * lax.rsqrt(jnp.mean(y * y, axis=-1, keepdims=True) + NORM_EPS) * gain_ref[...]


def _final(h, y_rows, route, gain, tm):
    t, d = h.shape
    return pl.pallas_call(
        _final_kernel,
        grid=(t // tm,),
        in_specs=[pl.BlockSpec((tm, d), lambda i: (i, 0)),
                  pl.BlockSpec((TOP_K, tm, d // 2), lambda i: (0, i, 0)),
                  pl.BlockSpec((tm, LANES), lambda i: (i, 0)),
                  pl.BlockSpec((1, d), lambda i: (0, 0))],
        out_specs=pl.BlockSpec((tm, d), lambda i: (i, 0)),
        out_shape=jax.ShapeDtypeStruct((t, d), F32),
        compiler_params=_cparams(("arbitrary",)),
        name="final",
    )(h, y_rows, route, gain.reshape(1, d))


def _layer(h3, mix_gain, w_in, attn_gain, decay_f, decay_b, ret_gain, w_out, ffn_gain,
           w_rg, b_rg, w_re, b_re, w_eg, w_eu, w_ed, final_gain):
    b, s, d = h3.shape
    t = b * s
    tm = 512
    attn_qkv, (qr, kr, vr, gr) = _inproj(h3, mix_gain, w_in.astype(BF16), 1024)

    o_list, lse_list = [], []
    for window, dilation in DILATED_BRANCHES:
        reach = (window // 2) // dilation
        o, lse = _attn_branch(*attn_qkv[dilation], dilation, reach, tq=128, tqb=1024, sub_tiles=8)
        o_list.append(o)
        lse_list.append(lse)

    lg_f = jnp.log1p(-jnp.exp2(decay_f.astype(F32)))
    lg_b = jnp.log1p(-jnp.exp2(decay_b.astype(F32)))
    orr = _retention(qr, kr, vr, gr, lg_f, lg_b, ret_gain).reshape(t, HEAD_GROUP_WIDTH)

    n_route = MOE_GROUPS + N_EXPERTS
    w_router = jnp.zeros((d, LANES), F32).at[:, :n_route].set(jnp.concatenate([w_re, w_rg], axis=1).astype(F32))
    b_router = jnp.zeros((1, LANES), F32).at[0, :n_route].set(jnp.concatenate([b_re, b_rg]).astype(F32))
    h, hn_packed, logits = _outproj(h3.reshape(t, d), o_list, lse_list, orr, attn_gain, w_out.astype(BF16),
                                    ffn_gain, w_router, b_router, 2 * tm)

    route, route_t, counts_rep = _route(logits, 2 * tm)

    n_blocks = -(-TOP_K * t // MOE_BLOCK) + N_EXPERTS
    counts = counts_rep[:, 0].astype(jnp.int32)
    aligned = ((counts + SEG_ALIGN - 1) // SEG_ALIGN) * SEG_ALIGN
    seg_start = jnp.cumsum(aligned) - aligned
    fields = jnp.transpose(route_t, (1, 0, 2)).reshape(SUBLANES, t)
    e12 = fields[ROUTE_E1:ROUTE_E2 + 1].astype(jnp.int32)
    r12 = fields[ROUTE_R1:ROUTE_R2 + 1].astype(jnp.int32)
    pos = r12 + jnp.sum(jnp.where(e12[None] == jnp.arange(N_EXPERTS, dtype=jnp.int32)[:, None, None],
                                  seg_start[:, None, None], 0), axis=0)
    nblk = (counts + MOE_BLOCK - 1) // MOE_BLOCK
    blk_end = jnp.cumsum(nblk)
    blk_start = blk_end - nblk
    blk = jnp.arange(n_blocks, dtype=jnp.int32)[:, None]
    owner = (blk >= blk_start) & (blk < blk_end)
    local = (blk - blk_start) * MOE_BLOCK
    block_row0 = jnp.sum(jnp.where(owner, seg_start + local, 0), axis=-1).astype(jnp.int32)
    block_nreal = jnp.sum(jnp.where(owner, jnp.clip(counts - local, 0, MOE_BLOCK), 0), axis=-1).astype(jnp.int32)
    block_expert = jnp.minimum(jnp.sum((blk >= blk_end).astype(jnp.int32), axis=-1), N_EXPERTS - 1)
    e_iota = jnp.arange(N_EXPERTS, dtype=jnp.int32)
    later = (e_iota[None, :] > e_iota[:, None]) & (counts[None, :] > 0)
    next_expert = jnp.where(jnp.any(later, axis=1), jnp.min(jnp.where(later, e_iota[None, :], N_EXPERTS), axis=1), -1)
    block_next = jnp.sum(jnp.where(owner, next_expert, 0), axis=-1).astype(jnp.int32)

    n_sorted = TOP_K * t + SORTED_TAIL
    dump_row = n_sorted
    seg_end = seg_start + counts
    hole = seg_end[:, None] + jnp.arange(SEG_ALIGN, dtype=jnp.int32)[None, :]
    hole = jnp.where(hole < (seg_start + aligned)[:, None], hole, dump_row).reshape(-1)
    tail = (seg_start[-1] + aligned[-1]) + jnp.arange(SORTED_TAIL, dtype=jnp.int32)
    tail = jnp.where(tail < n_sorted, tail, dump_row)
    zero_pos = jnp.concatenate([hole, tail]).astype(jnp.int32)

    xs = _dispatch(hn_packed, pos, zero_pos)
    ys = _experts(xs, block_expert, block_nreal, block_row0, block_next, w_eg.astype(F32), w_eu.astype(F32),
                  w_ed.astype(F32))
    out = _final(h, _combine_gather(ys, pos), route, final_gain, tm)
    return out.reshape(b, s, d)


def kernel(x, mix_norm_gain, w_in, attn_out_gain, ret_decay_fwd, ret_decay_bwd, ret_out_gain, w_out,
           ffn_norm_gain, w_route_group, b_route_group, w_route_expert, b_route_expert,
           w_expert_gate, w_expert_up, w_expert_down, final_norm_gain):
    depth = mix_norm_gain.shape[0]
    assert depth == 1, "the final rmsnorm is fused into the single layer's combine kernel"
    l = 0
    return _layer(x, mix_norm_gain[l], w_in[l], attn_out_gain[l], ret_decay_fwd[l], ret_decay_bwd[l],
                  ret_out_gain[l], w_out[l], ffn_norm_gain[l], w_route_group[l], b_route_group[l],
                  w_route_expert[l], b_route_expert[l], w_expert_gate[l], w_expert_up[l], w_expert_down[l],
                  final_norm_gain)
```

```python
import functools

import numpy as np
import jax
import jax.numpy as jnp
from jax import lax
from jax.experimental import pallas as pl
from jax.experimental.pallas import tpu as pltpu
from jax.experimental.pallas import tpu_sc as plsc

F32 = jnp.float32
BF16 = jnp.bfloat16

ATTN_HEADS = 8
HEAD_DIM = 64
RET_HEADS = 8
HEAD_GROUP_WIDTH = 512
N_PROJ_GROUPS = 7
DILATED_BRANCHES = ((128, 1), (512, 4), (2048, 16))
ROPE_THETA = 500000.0
ROPE_DIM = HEAD_DIM // 4
RET_THETA = 10000.0
RET_CHUNK = 128
MOE_GROUPS = 4
EXPERTS_PER_GROUP = 8
N_EXPERTS = MOE_GROUPS * EXPERTS_PER_GROUP
MOE_BLOCK = 512
NORM_EPS = 1e-6
NEG_INF = -1e30

LANES = 128
VMEM_LIMIT = 56 * 1024 * 1024


def _cparams(sem):
    return pltpu.CompilerParams(dimension_semantics=sem, vmem_limit_bytes=VMEM_LIMIT)


def _rotary_tables(seq, half, freqs):
    pos = np.arange(seq, dtype=np.float64)[:, None]
    ang = pos * freqs[None, :].astype(np.float64)
    cos, sin = np.cos(ang), np.sin(ang)
    c = np.ones((seq, HEAD_DIM)); sp = np.zeros((seq, HEAD_DIM)); sm = np.zeros((seq, HEAD_DIM))
    c[:, :half] = cos; c[:, half:2 * half] = cos
    sp[:, half:2 * half] = sin
    sm[:, :half] = -sin
    rep = LANES // HEAD_DIM
    return tuple(jnp.asarray(np.tile(t, (1, rep)), F32) for t in (c, sp, sm))


def _rotate(t, c, sp, sm, half):
    outs = []
    for g in range(t.shape[1] // LANES):
        tg = t[:, g * LANES:(g + 1) * LANES]
        outs.append(tg * c + pltpu.roll(tg, half, 1) * sp + pltpu.roll(tg, LANES - half, 1) * sm)
    return jnp.concatenate(outs, axis=1)


CLASS_DILATIONS = tuple(d for _, d in DILATED_BRANCHES if d > 1)
ATTN_Q_SCALE = float(np.log2(np.e)) * HEAD_DIM ** -0.5


def _inproj_kernel(x_ref, gain_ref, w_ref, ca_ref, spa_ref, sma_ref, cr_ref, spr_ref, smr_ref, *rest):
    n_cls = len(CLASS_DILATIONS)
    nat_refs = rest[0:3]
    cls_refs = [rest[3 + 3 * c:6 + 3 * c] for c in range(n_cls)]
    qr_ref, kr_ref, vr_ref, gr_ref = rest[3 + 3 * n_cls:7 + 3 * n_cls]
    stage_ref = rest[7 + 3 * n_cls]
    x = x_ref[0]
    tm = x.shape[0]
    ms = jnp.mean(x * x, axis=-1, keepdims=True)
    xn = (x * lax.rsqrt(ms + NORM_EPS) * gain_ref[...]).astype(BF16)
    gw = HEAD_GROUP_WIDTH

    def proj(c):
        return jnp.dot(xn, w_ref[:, c * gw:(c + 1) * gw], preferred_element_type=F32)

    a_tabs = (ca_ref[...], spa_ref[...], sma_ref[...])
    r_tabs = (cr_ref[...], spr_ref[...], smr_ref[...])
    attn_vals = ((_rotate(proj(0), *a_tabs, ROPE_DIM // 2) * ATTN_Q_SCALE),
                 _rotate(proj(1), *a_tabs, ROPE_DIM // 2),
                 proj(2))
    for j, val in enumerate(attn_vals):
        nat_refs[j][0] = val.astype(BF16)
        for g in range(gw // LANES):
            stage_ref[g] = val[:, g * LANES:(g + 1) * LANES]
        for c, d in enumerate(CLASS_DILATIONS):
            for r in range(d):
                for g in range(gw // LANES):
                    col = r * gw + g * LANES
                    cls_refs[c][j][0, :, col:col + LANES] = (
                        stage_ref[g, pl.ds(r, tm // d, stride=d), :].astype(BF16))
    qr_ref[0] = _rotate(proj(3), *r_tabs, HEAD_DIM // 2).astype(BF16)
    kr_ref[0] = (_rotate(proj(4), *r_tabs, HEAD_DIM // 2) * (HEAD_DIM ** -0.5)).astype(BF16)
    vr_ref[0] = proj(5).astype(BF16)
    g = proj(6)
    gr_ref[0] = (g * jax.nn.sigmoid(g)).astype(BF16)


def _inproj(x, gain, w_in_bf16, tm):
    b, s, d = x.shape
    rope_freqs = ROPE_THETA ** (-np.arange(0, ROPE_DIM, 2, dtype=np.float32) / ROPE_DIM)
    ret_freqs = RET_THETA ** (-np.linspace(0.0, 1.0, HEAD_DIM // 2, dtype=np.float32))
    tabs = _rotary_tables(s, ROPE_DIM // 2, rope_freqs) + _rotary_tables(s, HEAD_DIM // 2, ret_freqs)
    gw = HEAD_GROUP_WIDTH
    tab_spec = pl.BlockSpec((tm, LANES), lambda si, bi: (si, 0))

    def view(dil):
        return (pl.BlockSpec((1, tm // dil, dil * gw), lambda si, bi: (bi, si, 0)),
                jax.ShapeDtypeStruct((b, s // dil, dil * gw), BF16))

    views = [view(1)] * 3 + [view(dil) for dil in CLASS_DILATIONS for _ in range(3)] + [view(1)] * 4
    outs = pl.pallas_call(
        _inproj_kernel,
        grid=(s // tm, b),
        in_specs=[pl.BlockSpec((1, tm, d), lambda si, bi: (bi, si, 0)),
                  pl.BlockSpec((1, d), lambda si, bi: (0, 0)),
                  pl.BlockSpec(w_in_bf16.shape, lambda si, bi: (0, 0))] + [tab_spec] * 6,
        out_specs=[v[0] for v in views],
        out_shape=[v[1] for v in views],
        scratch_shapes=[pltpu.VMEM((gw // LANES, tm, LANES), F32)],
        compiler_params=_cparams(("arbitrary", "arbitrary")),
        name="inproj",
    )(x, gain.reshape(1, d), w_in_bf16, *tabs)
    n_attn = 3 * (1 + len(CLASS_DILATIONS))
    attn_qkv = {dil: outs[3 * c:3 * c + 3] for c, dil in enumerate((1,) + CLASS_DILATIONS)}
    return attn_qkv, outs[n_attn:]


ATTN_WINDOW_CASES = 3


def _attn_kernel(q_ref, k_ref, v_ref, o_ref, lse_ref, bias_ref, *, length, tq, reach):
    tqb = q_ref.shape[1]
    win = tq + 2 * reach
    heads_per_pair = LANES // HEAD_DIM
    qi = pl.program_id(2)
    lane = lax.broadcasted_iota(jnp.int32, (1, LANES), 1)
    lane_t = lax.broadcasted_iota(jnp.int32, (tq, LANES), 1)
    ones = jnp.ones((win, LANES), BF16)

    @pl.when((pl.program_id(0) == 0) & (pl.program_id(1) == 0) & (qi == 0))
    def _():
        diff = (lax.broadcasted_iota(jnp.int32, (heads_per_pair * tq, win), 1)
                - (lax.broadcasted_iota(jnp.int32, (heads_per_pair * tq, win), 0) & (tq - 1)))
        for case in range(ATTN_WINDOW_CASES):
            off = case * reach
            bias_ref[case] = jnp.where((diff >= off - reach) & (diff <= off + reach), 0.0, NEG_INF)

    def sub(t, cls):
        q0 = qi * tqb + t * tq
        ws = jnp.clip(q0 - reach, 0, length - win)
        ws = pl.multiple_of(ws, reach)
        bias = bias_ref[(q0 - ws) // reach]
        rows = pl.ds(t * tq, tq)
        m_tile = jnp.zeros((tq, LANES), F32)
        l_tile = jnp.ones((tq, LANES), F32)
        for g in range(HEAD_GROUP_WIDTH // LANES):
            cols = slice(cls * HEAD_GROUP_WIDTH + g * LANES, cls * HEAD_GROUP_WIDTH + (g + 1) * LANES)
            qg = q_ref[0, rows, cols]
            kw = k_ref[0, pl.ds(ws, win), cols]
            v_ones = jnp.concatenate([v_ref[0, pl.ds(ws, win), cols], ones], axis=1)
            hms = [(lane >= hh * HEAD_DIM) & (lane < (hh + 1) * HEAD_DIM) for hh in range(heads_per_pair)]
            q2 = jnp.concatenate([qg * hm.astype(BF16) for hm in hms], axis=0)
            sc = lax.dot_general(q2, kw, (((1,), (1,)), ((), ())), preferred_element_type=F32) + bias
            m = jnp.max(sc, axis=-1, keepdims=True)
            p = jnp.exp2(sc - m).astype(BF16)
            pv = jnp.dot(p, v_ones, preferred_element_type=F32)
            l = pv[:, LANES:]
            o = pv[:, :LANES] / l
            o_pair = jnp.zeros((tq, LANES), F32)
            for hh in range(heads_per_pair):
                part = slice(hh * tq, (hh + 1) * tq)
                head_lane = lane_t == g * heads_per_pair + hh
                o_pair = jnp.where(hms[hh], o[part], o_pair)
                m_tile = jnp.where(head_lane, m[part], m_tile)
                l_tile = jnp.where(head_lane, l[part], l_tile)
            o_ref[0, rows, cols] = o_pair.astype(BF16)
        lse_ref[0, rows, cls * LANES:(cls + 1) * LANES] = m_tile + jnp.log2(l_tile)

    for cls in range(q_ref.shape[2] // HEAD_GROUP_WIDTH):
        for t in range(tqb // tq):
            sub(t, cls)


def _attn_branch(qc, kc, vc, dilation, reach, tq, tqb, sub_tiles=4):
    b, length, dw = qc.shape
    w = dw // dilation
    tqb = min(tqb, length)
    assert tq % reach == 0 and tq > reach and length % tq == 0 and length >= tq + 2 * reach
    ncls = min(dilation, max(1, sub_tiles // (tqb // tq)))
    q_spec = pl.BlockSpec((1, tqb, ncls * w), lambda bi, r, qi: (bi, qi, r))
    kv_spec = pl.BlockSpec((1, length, ncls * w), lambda bi, r, qi: (bi, 0, r))
    o, lse = pl.pallas_call(
        functools.partial(_attn_kernel, length=length, tq=tq, reach=reach),
        grid=(b, dilation // ncls, length // tqb),
        in_specs=[q_spec, kv_spec, kv_spec],
        out_specs=[q_spec, pl.BlockSpec((1, tqb, ncls * LANES), lambda bi, r, qi: (bi, qi, r))],
        out_shape=[jax.ShapeDtypeStruct((b, length, dilation * w), BF16),
                   jax.ShapeDtypeStruct((b, length, dilation * LANES), F32)],
        scratch_shapes=[pltpu.VMEM((ATTN_WINDOW_CASES, (LANES // HEAD_DIM) * tq, tq + 2 * reach), F32)],
        compiler_params=_cparams(("arbitrary", "arbitrary", "arbitrary")),
        name=f"attn_d{dilation}",
    )(qc, kc, vc)
    return o, lse


RET_TAB_QF, RET_TAB_QB, RET_TAB_KF, RET_TAB_KB = range(4)


def _retention_kernel(lgf_ref, lgb_ref, q_ref, k_ref, v_ref, g_ref, gain_ref, o_ref,
                      tab_ref, dec_ref, sb_ref, st_ref, *, chunk, unroll):
    c = chunk
    n = q_ref.shape[1] // c
    width = q_ref.shape[2]
    n_pairs = width // LANES
    heads_per_pair = LANES // HEAD_DIM
    n_heads = n_pairs * heads_per_pair
    head0 = pl.program_id(1) * n_heads
    lane_w = lax.broadcasted_iota(jnp.int32, (1, width), 1)
    lgf = [lgf_ref[head0 + hd] for hd in range(n_heads)]
    lgb = [lgb_ref[head0 + hd] for hd in range(n_heads)]
    lgf_lane = jnp.zeros((1, width), F32)
    lgb_lane = jnp.zeros((1, width), F32)
    for hd in range(n_heads):
        in_head = (lane_w >= hd * HEAD_DIM) & (lane_w < (hd + 1) * HEAD_DIM)
        lgf_lane = jnp.where(in_head, lgf[hd], lgf_lane)
        lgb_lane = jnp.where(in_head, lgb[hd], lgb_lane)
    idx = lax.broadcasted_iota(jnp.int32, (c, width), 0).astype(F32)
    tab_ref[RET_TAB_QF] = jnp.exp((idx + 1.0) * lgf_lane)
    tab_ref[RET_TAB_QB] = jnp.exp((c - idx) * lgb_lane)
    tab_ref[RET_TAB_KF] = jnp.exp((c - 1.0 - idx) * lgf_lane)
    tab_ref[RET_TAB_KB] = jnp.exp(idx * lgb_lane)
    sdf = jnp.exp(c * lgf_lane)
    sdb = jnp.exp(c * lgb_lane)
    dmat = (lax.broadcasted_iota(jnp.int32, (c, c), 0)
            - lax.broadcasted_iota(jnp.int32, (c, c), 1)).astype(F32)
    for hd in range(n_heads):
        dec_ref[hd // heads_per_pair, :, (hd % heads_per_pair) * c:(hd % heads_per_pair + 1) * c] = (
            jnp.where(dmat >= 0, jnp.exp(dmat * lgf[hd]), jnp.exp(-dmat * lgb[hd])))
    lane = lax.broadcasted_iota(jnp.int32, (1, LANES), 1)
    lane_hi = lane >= HEAD_DIM
    head_masks = [((lane >= hh * HEAD_DIM) & (lane < (hh + 1) * HEAD_DIM)).astype(BF16)
                  for hh in range(heads_per_pair)]
    row_hi = lax.broadcasted_iota(jnp.int32, (LANES, LANES), 0) >= HEAD_DIM
    col_hi = lax.broadcasted_iota(jnp.int32, (LANES, LANES), 1) >= HEAD_DIM
    blockdiag = row_hi == col_hi

    def kv_state(kd, vv):
        kt = jnp.transpose(kd).astype(BF16)
        return jnp.where(blockdiag, jnp.dot(kt, vv, preferred_element_type=F32), 0.0)

    st_ref[...] = jnp.zeros_like(st_ref)
    sb_ref[n - 1] = jnp.zeros(sb_ref.shape[1:], sb_ref.dtype)

    def back(i, carry):
        nn = n - 1 - i
        rows = pl.ds(pl.multiple_of(nn * c, c), c)
        for p in range(n_pairs):
            cols = slice(p * LANES, (p + 1) * LANES)
            kd = k_ref[0, rows, cols].astype(F32) * tab_ref[RET_TAB_KB, :, cols]
            new = st_ref[p] * sdb[:, cols] + kv_state(kd, v_ref[0, rows, cols])
            st_ref[p] = new
            sb_ref[nn - 1, p] = new.astype(BF16)
        return carry

    lax.fori_loop(0, n - 1, back, 0, unroll=unroll)

    st_ref[...] = jnp.zeros_like(st_ref)

    def fwd(nn, carry):
        rows = pl.ds(pl.multiple_of(nn * c, c), c)
        for p in range(n_pairs):
            cols = slice(p * LANES, (p + 1) * LANES)
            qq = q_ref[0, rows, cols]
            kk = k_ref[0, rows, cols]
            vv = v_ref[0, rows, cols]
            qf = qq.astype(F32)
            sf = st_ref[p]
            qcat = jnp.concatenate([(qf * tab_ref[RET_TAB_QF, :, cols]).astype(BF16),
                                    (qf * tab_ref[RET_TAB_QB, :, cols]).astype(BF16)], axis=1)
            scat = jnp.concatenate([sf.astype(BF16), sb_ref[nn, p]], axis=0)
            o = jnp.dot(qcat, scat, preferred_element_type=F32)
            k2 = jnp.concatenate([kk * hm for hm in head_masks], axis=0)
            v2 = jnp.concatenate([vv * hm for hm in head_masks], axis=0)
            sc = lax.dot_general(qq, k2, (((1,), (1,)), ((), ())), preferred_element_type=F32)
            o = o + jnp.dot((sc * dec_ref[p]).astype(BF16), v2, preferred_element_type=F32)
            s_lo = jnp.sum(jnp.where(lane_hi, 0.0, o), axis=-1, keepdims=True)
            s_hi = jnp.sum(jnp.where(lane_hi, o, 0.0), axis=-1, keepdims=True)
            mu = jnp.where(lane_hi, s_hi, s_lo) * (1.0 / HEAD_DIM)
            dev = o - mu
            d2 = dev * dev
            v_lo = jnp.sum(jnp.where(lane_hi, 0.0, d2), axis=-1, keepdims=True)
            v_hi = jnp.sum(jnp.where(lane_hi, d2, 0.0), axis=-1, keepdims=True)
            var = jnp.where(lane_hi, v_hi, v_lo) * (1.0 / HEAD_DIM)
            out = dev * lax.rsqrt(var + NORM_EPS) * gain_ref[:, cols] * g_ref[0, rows, cols].astype(F32)
            o_ref[0, rows, cols] = out.astype(BF16)
            kd = kk.astype(F32) * tab_ref[RET_TAB_KF, :, cols]
            st_ref[p] = sf * sdf[:, cols] + kv_state(kd, vv)
        return carry

    lax.fori_loop(0, n, fwd, 0, unroll=unroll)


def _retention(qr, kr, vr, gate, lg_f, lg_b, out_gain, width=512, unroll=8):
    b, s, w = qr.shape
    n_pairs = width // LANES
    n_heads = width // HEAD_DIM
    spec = pl.BlockSpec((1, s, width), lambda bi, p, *_: (bi, 0, p))
    grid_spec = pltpu.PrefetchScalarGridSpec(
        num_scalar_prefetch=2,
        grid=(b, w // width),
        in_specs=[spec, spec, spec, spec, pl.BlockSpec((1, width), lambda bi, p, *_: (0, p))],
        out_specs=spec,
        scratch_shapes=[pltpu.VMEM((4, RET_CHUNK, width), F32),
                        pltpu.VMEM((n_pairs, RET_CHUNK, (LANES // HEAD_DIM) * RET_CHUNK), F32),
                        pltpu.VMEM((s // RET_CHUNK, n_pairs, LANES, LANES), BF16),
                        pltpu.VMEM((n_pairs, LANES, LANES), F32)],
    )
    return pl.pallas_call(
        functools.partial(_retention_kernel, chunk=RET_CHUNK, unroll=unroll),
        grid_spec=grid_spec,
        out_shape=jax.ShapeDtypeStruct((b, s, w), BF16),
        compiler_params=_cparams(("arbitrary", "arbitrary")),
        name="retention",
    )(lg_f, lg_b, qr, kr, vr, gate, out_gain.reshape(1, w))


def _split_bf16(t):
    hi = t.astype(BF16)
    lo = (t - hi.astype(F32)).astype(BF16)
    return hi, lo


def _pack_bf16_pairs(t):
    n = t.shape[1] // 2
    hi = pltpu.bitcast(t[:, :n].astype(BF16).astype(F32), jnp.uint32)
    lo = pltpu.bitcast(t[:, n:].astype(BF16).astype(F32), jnp.uint32)
    return hi | (lo >> 16)


def _unpack_bf16_pairs(u):
    hi = pltpu.bitcast(u & jnp.uint32(0xFFFF0000), F32)
    lo = pltpu.bitcast(u << 16, F32)
    return jnp.concatenate([hi, lo], axis=1)


def _outproj_kernel(x_ref, o1_ref, o2_ref, o3_ref, l1_ref, l2_ref, l3_ref, orr_ref, ga_ref, expand_ref,
                    wout_ref, gf_ref, wr_ref, br_ref, h_ref, hn_ref, logit_ref,
                    *nat_refs):
    tm = x_ref.shape[0]
    gw = HEAD_GROUP_WIDTH
    os, ls = [], []
    for (_, dil), o_ref, l_ref in zip(DILATED_BRANCHES, (o1_ref, o2_ref, o3_ref), (l1_ref, l2_ref, l3_ref)):
        if dil == 1:
            os.append(o_ref[...].astype(F32))
            ls.append(l_ref[...])
            continue
        c = CLASS_DILATIONS.index(dil)
        onat_ref, lnat_ref = nat_refs[2 * c], nat_refs[2 * c + 1]
        for r in range(dil):
            rows = pl.ds(r, tm // dil, stride=dil)
            for g in range(gw // LANES):
                col = r * gw + g * LANES
                onat_ref[g, rows, :] = o_ref[:, col:col + LANES].astype(F32)
            lnat_ref[rows, :] = l_ref[:, r * LANES:(r + 1) * LANES]
        os.append(jnp.concatenate([onat_ref[g] for g in range(gw // LANES)], axis=1))
        ls.append(lnat_ref[...])
    mx = jnp.maximum(jnp.maximum(ls[0], ls[1]), ls[2])
    es = [jnp.exp2(l - mx) for l in ls]
    inv = 1.0 / (es[0] + es[1] + es[2])
    expand = expand_ref[...]
    oa = jnp.zeros((tm, gw), F32)
    for e, o in zip(es, os):
        wexp = jnp.dot(jnp.concatenate(_split_bf16(e * inv), axis=1), expand, preferred_element_type=F32)
        oa = oa + wexp * o
    oa = oa * lax.rsqrt(jnp.mean(oa * oa, axis=-1, keepdims=True) + NORM_EPS) * ga_ref[...]
    mixed = jnp.concatenate([oa.astype(BF16), orr_ref[...]], axis=1)
    h = x_ref[...] + jnp.dot(mixed, wout_ref[...], preferred_element_type=F32)
    h_ref[...] = h
    hn = h * lax.rsqrt(jnp.mean(h * h, axis=-1, keepdims=True) + NORM_EPS) * gf_ref[...]
    hn_ref[...] = _pack_bf16_pairs(hn)
    prod = jnp.dot(jnp.concatenate(_split_bf16(hn), axis=0), wr_ref[...], preferred_element_type=F32)
    logit_ref[...] = prod[:tm, :LANES] + prod[:tm, LANES:] + prod[tm:, :LANES] + br_ref[...]


def _outproj(x2, o_list, lse_list, orr, attn_gain, w_out_bf16, ffn_gain, w_router, b_router, tm):
    t, d = x2.shape
    w = HEAD_GROUP_WIDTH
    expand = np.zeros((LANES, w), np.float32)
    for hd in range(ATTN_HEADS):
        expand[hd, hd * HEAD_DIM:(hd + 1) * HEAD_DIM] = 1.0
    expand = jnp.asarray(np.concatenate([expand, expand], axis=0), BF16)
    wr_hi = w_router.astype(BF16)
    wr = jnp.concatenate([wr_hi, (w_router - wr_hi.astype(F32)).astype(BF16)], axis=1)
    row = lambda width, dil=1: pl.BlockSpec((tm // dil, dil * width), lambda i: (i, 0))
    full = lambda a: pl.BlockSpec(a.shape, lambda i: (0,) * a.ndim)
    ga = attn_gain.reshape(1, w)
    gf = ffn_gain.reshape(1, d)
    dils = [dil for _, dil in DILATED_BRANCHES]
    o_flat = [o.reshape(t // dil, dil * w) for o, dil in zip(o_list, dils)]
    l_flat = [l.reshape(t // dil, dil * LANES) for l, dil in zip(lse_list, dils)]
    nat_scratch = []
    for _ in CLASS_DILATIONS:
        nat_scratch += [pltpu.VMEM((w // LANES, tm, LANES), F32), pltpu.VMEM((tm, LANES), F32)]
    return pl.pallas_call(
        _outproj_kernel,
        grid=(t // tm,),
        in_specs=[row(d)] + [row(w, dil) for dil in dils] + [row(LANES, dil) for dil in dils] + [row(w)]
                 + [full(ga), full(expand), full(w_out_bf16), full(gf), full(wr), full(b_router)],
        out_specs=[row(d), row(d // 2), row(LANES)],
        out_shape=[jax.ShapeDtypeStruct((t, d), F32),
                   jax.ShapeDtypeStruct((t, d // 2), jnp.uint32),
                   jax.ShapeDtypeStruct((t, LANES), F32)],
        scratch_shapes=nat_scratch,
        compiler_params=_cparams(("arbitrary",)),
        name="outproj",
    )(x2, *o_flat, *l_flat, orr, ga, expand, w_out_bf16, gf, wr, b_router)


ROUTE_E1, ROUTE_E2, ROUTE_G1, ROUTE_G2, ROUTE_R1, ROUTE_R2 = range(6)
GROUP_LANE0 = N_EXPERTS
SUBLANES = 8


def _route_kernel(logit_ref, tri_ref, route_ref, route_t_ref, count_ref, run_ref):
    @pl.when(pl.program_id(0) == 0)
    def _():
        run_ref[...] = jnp.zeros_like(run_ref)

    lg_t = jnp.transpose(logit_ref[...])
    tm = lg_t.shape[1]
    assert EXPERTS_PER_GROUP == SUBLANES and MOE_GROUPS <= SUBLANES
    rid = lax.broadcasted_iota(jnp.int32, (SUBLANES, tm), 0)
    big = jnp.int32(1 << 20)

    def top(vals):
        m = jnp.max(vals, axis=0, keepdims=True)
        i = jnp.min(jnp.where(vals == m, rid, big), axis=0, keepdims=True)
        return m, i

    gl = jnp.where(rid < MOE_GROUPS, lg_t[GROUP_LANE0:GROUP_LANE0 + SUBLANES], -jnp.inf)
    gmax, gidx = top(gl)
    group_gate = 1.0 / jnp.sum(jnp.exp(gl - gmax), axis=0, keepdims=True)
    el = lg_t[0:EXPERTS_PER_GROUP]
    for g in range(1, MOE_GROUPS):
        el = jnp.where(gidx == g, lg_t[g * EXPERTS_PER_GROUP:(g + 1) * EXPERTS_PER_GROUP], el)
    t1, i1 = top(el)
    t2, i2 = top(jnp.where(rid == i1, -jnp.inf, el))
    e21 = jnp.exp(t2 - t1)
    g1 = group_gate / (1.0 + e21)
    g2 = group_gate * e21 / (1.0 + e21)
    e1 = gidx * EXPERTS_PER_GROUP + i1
    e2 = gidx * EXPERTS_PER_GROUP + i2
    erow = lax.broadcasted_iota(jnp.int32, (N_EXPERTS, tm), 0)
    oh1 = erow == e1
    oh2 = erow == e2
    cnt = oh1.astype(F32) + oh2.astype(F32)
    run = run_ref[:, 0:1]
    prefix = jnp.dot(cnt.astype(BF16), tri_ref[...], preferred_element_type=F32) + run
    r1 = jnp.sum(jnp.where(oh1, prefix, 0.0), axis=0, keepdims=True)
    r2 = jnp.sum(jnp.where(oh2, prefix, 0.0), axis=0, keepdims=True)
    new_run = jnp.broadcast_to(run + jnp.sum(cnt, axis=1, keepdims=True), run_ref.shape)
    run_ref[...] = new_run
    count_ref[...] = new_run
    out = jnp.zeros((SUBLANES, tm), F32)
    for row, val in ((ROUTE_E1, e1.astype(F32)), (ROUTE_E2, e2.astype(F32)), (ROUTE_G1, g1),
                     (ROUTE_G2, g2), (ROUTE_R1, r1), (ROUTE_R2, r2)):
        out = jnp.where(rid == row, val, out)
    route_t_ref[0] = out
    out_t = jnp.concatenate([out, jnp.zeros((LANES - SUBLANES, tm), F32)], axis=0)
    route_ref[...] = jnp.transpose(out_t)


def _route(logits, tm):
    t = logits.shape[0]
    tri = jnp.asarray(np.triu(np.ones((tm, tm), np.float32), 1), BF16)
    return pl.pallas_call(
        _route_kernel,
        grid=(t // tm,),
        in_specs=[pl.BlockSpec((tm, LANES), lambda i: (i, 0)), pl.BlockSpec((tm, tm), lambda i: (0, 0))],
        out_specs=[pl.BlockSpec((tm, LANES), lambda i: (i, 0)),
                   pl.BlockSpec((1, SUBLANES, tm), lambda i: (i, 0, 0)),
                   pl.BlockSpec((N_EXPERTS, LANES), lambda i: (0, 0))],
        out_shape=[jax.ShapeDtypeStruct((t, LANES), F32),
                   jax.ShapeDtypeStruct((t // tm, SUBLANES, tm), F32),
                   jax.ShapeDtypeStruct((N_EXPERTS, LANES), F32)],
        scratch_shapes=[pltpu.VMEM((N_EXPERTS, LANES), F32)],
        compiler_params=_cparams(("arbitrary",)),
        name="route",
    )(logits, tri)


TOP_K = 2


SEG_ALIGN = 8
SORTED_TAIL = N_EXPERTS * SEG_ALIGN + MOE_BLOCK


def _tail_pieces():
    full, rest = divmod(SORTED_TAIL, MOE_BLOCK)
    return [MOE_BLOCK] * full + ([rest] if rest else [])


SC_CHUNK = 64
SC_BUFFERS = 3
ZERO_ROWS = N_EXPERTS * SEG_ALIGN + SORTED_TAIL
SORTED_ROWS_EXTRA = SORTED_TAIL + SEG_ALIGN


def _sc_workers():
    info = plsc.get_sparse_core_info()
    return info.num_cores, info.num_cores * info.num_subcores


def _dispatch(hn_packed, pos, zero_pos):
    t, dp = hn_packed.shape
    ncores, nw = _sc_workers()
    per_w = t // nw
    n_ch = per_w // SC_CHUNK
    z_rows = ZERO_ROWS // nw
    assert per_w % SC_CHUNK == 0 and ZERO_ROWS % nw == 0 and z_rows % SEG_ALIGN == 0 and TOP_K == 2
    idx = [pos[k].reshape(nw, n_ch, SC_CHUNK) for k in range(TOP_K)]
    zeros = jnp.zeros((z_rows, dp), hn_packed.dtype)
    mesh = plsc.VectorSubcoreMesh(core_axis_name="c", subcore_axis_name="s")

    @functools.partial(
        pl.kernel, mesh=mesh,
        out_type=jax.ShapeDtypeStruct((TOP_K * t + SORTED_ROWS_EXTRA, dp), hn_packed.dtype),
        scratch_types=[pltpu.VMEM((n_ch, SC_CHUNK), jnp.int32), pltpu.VMEM((n_ch, SC_CHUNK), jnp.int32),
                       pltpu.VMEM((z_rows,), jnp.int32),
                       pltpu.VMEM((SC_BUFFERS, SC_CHUNK, dp), hn_packed.dtype),
                       pltpu.VMEM((z_rows, dp), hn_packed.dtype),
                       pltpu.SemaphoreType.DMA((SC_BUFFERS,)), pltpu.SemaphoreType.DMA((SC_BUFFERS,)),
                       pltpu.SemaphoreType.DMA],
    )
    def scatter(hn_hbm, p0_hbm, p1_hbm, zpos_hbm, zeros_hbm, xs_hbm, i0_v, i1_v, iz_v, rows_v, zero_v,
                lsem, ssem, zsem):
        wid = lax.axis_index("s") * ncores + lax.axis_index("c")
        base = wid * per_w
        zero_load = pltpu.async_copy(zeros_hbm, zero_v, zsem)
        pltpu.sync_copy(zpos_hbm.at[wid], iz_v)
        pltpu.sync_copy(p0_hbm.at[wid], i0_v)
        pltpu.sync_copy(p1_hbm.at[wid], i1_v)

        def load(c):
            return pltpu.async_copy(hn_hbm.at[pl.ds(base + c * SC_CHUNK, SC_CHUNK)], rows_v.at[c % SC_BUFFERS],
                                    lsem.at[c % SC_BUFFERS])

        loads = {c: load(c) for c in range(min(SC_BUFFERS - 1, n_ch))}
        scat = {}
        for c in range(n_ch):
            slot = c % SC_BUFFERS
            loads[c].wait()
            scat[c] = (pltpu.async_copy(rows_v.at[slot], xs_hbm.at[i0_v.at[c]], ssem.at[slot]),
                       pltpu.async_copy(rows_v.at[slot], xs_hbm.at[i1_v.at[c]], ssem.at[slot]))
            if c >= 1:
                for d in scat[c - 1]:
                    d.wait()
            if c + SC_BUFFERS - 1 < n_ch:
                loads[c + SC_BUFFERS - 1] = load(c + SC_BUFFERS - 1)
        zero_load.wait()
        zero_scatter = pltpu.async_copy(zero_v, xs_hbm.at[iz_v], zsem)
        for d in scat[n_ch - 1]:
            d.wait()
        zero_scatter.wait()

    return scatter(hn_packed, idx[0], idx[1], zero_pos.reshape(nw, z_rows), zeros)


def _expert_kernel(bexp_ref, nreal_ref, row0_ref, bnext_ref, xs_hbm, wg_hbm, wu_hbm, wd_hbm, ys_hbm,
                   xbuf, ybuf, wg_f32, wu_f32, wd_f32, wg_bf, wu_bf, wd_bf, isem, osem, wsem, *, n_rows):
    i = pl.program_id(0)
    nb = pl.num_programs(0)
    slot = i % 2
    nslot = 1 - slot
    n_cur = nreal_ref[i]
    prev = jnp.maximum(i - 1, 0)
    nxt = jnp.minimum(i + 1, nb - 1)

    def in_copy(blk, s):
        row0 = pl.multiple_of(row0_ref[blk], SEG_ALIGN)
        return pltpu.make_async_copy(xs_hbm.at[pl.ds(row0, MOE_BLOCK)], xbuf.at[s], isem.at[s])

    def out_copy(blk, s):
        row0 = pl.multiple_of(row0_ref[blk], SEG_ALIGN)
        return pltpu.make_async_copy(ybuf.at[s], ys_hbm.at[pl.ds(row0, MOE_BLOCK)], osem.at[s])

    @pl.when(i == 0)
    def _():
        ybuf[...] = jnp.zeros_like(ybuf)
        tails = [pltpu.make_async_copy(ybuf.at[s, pl.ds(0, size)], ys_hbm.at[pl.ds(n_rows + s * MOE_BLOCK, size)],
                                       osem.at[s])
                 for s, size in enumerate(_tail_pieces())]
        for tail in tails:
            tail.start()
        for tail in tails:
            tail.wait()

        @pl.when(n_cur > 0)
        def _():
            in_copy(i, slot).start()

    @pl.when((i + 1 < nb) & (nreal_ref[nxt] > 0))
    def _():
        in_copy(nxt, nslot).start()

    def weight_copies(e):
        return [pltpu.make_async_copy(w_hbm.at[e], w_f32, wsem.at[j])
                for j, (w_hbm, w_f32) in enumerate(((wg_hbm, wg_f32), (wu_hbm, wu_f32), (wd_hbm, wd_f32)))]

    @pl.when((i == 0) & (n_cur > 0))
    def _():
        for c in weight_copies(bexp_ref[0]):
            c.start()

    @pl.when((n_cur > 0) & ((i == 0) | (bexp_ref[i] != bexp_ref[prev])))
    def _():
        for c in weight_copies(bexp_ref[i]):
            c.wait()
        wg_bf[...] = wg_f32[...].astype(BF16)
        wu_bf[...] = wu_f32[...].astype(BF16)
        wd_bf[...] = wd_f32[...].astype(BF16)

        @pl.when(bnext_ref[i] >= 0)
        def _():
            for c in weight_copies(bnext_ref[i]):
                c.start()

    def swiglu(rows):
        xb = _unpack_bf16_pairs(xbuf[slot, rows]).astype(BF16)
        gate = jnp.dot(xb, wg_bf[...], preferred_element_type=F32)
        up = jnp.dot(xb, wu_bf[...], preferred_element_type=F32)
        hid = (gate * jax.nn.sigmoid(gate) * up).astype(BF16)
        ybuf[slot, rows] = _pack_bf16_pairs(jnp.dot(hid, wd_bf[...], preferred_element_type=F32))

    @pl.when(n_cur > 0)
    def _():
        in_copy(i, slot).wait()

    @pl.when(n_cur > MOE_BLOCK // 2)
    def _():
        swiglu(slice(0, MOE_BLOCK))

    @pl.when((n_cur > 0) & (n_cur <= MOE_BLOCK // 2))
    def _():
        swiglu(slice(0, MOE_BLOCK // 2))

    @pl.when((i >= 1) & (nreal_ref[prev] > 0))
    def _():
        out_copy(prev, nslot).wait()

    @pl.when(n_cur > 0)
    def _():
        out_copy(i, slot).start()

        @pl.when(i == nb - 1)
        def _():
            out_copy(i, slot).wait()


def _experts(xs, block_expert, block_nreal, block_row0, block_next, wg, wu, wd):
    n_blocks = block_expert.shape[0]
    assert n_blocks >= 2 and len(_tail_pieces()) <= 2
    n_rows_pad = xs.shape[0] - SORTED_ROWS_EXTRA + SORTED_TAIL
    dp = xs.shape[1]
    _, d, ff = wg.shape
    any_spec = pl.BlockSpec(memory_space=pl.ANY)
    grid_spec = pltpu.PrefetchScalarGridSpec(
        num_scalar_prefetch=4,
        grid=(n_blocks,),
        in_specs=[any_spec, any_spec, any_spec, any_spec],
        out_specs=any_spec,
        scratch_shapes=[pltpu.VMEM((2, MOE_BLOCK, dp), jnp.uint32),
                        pltpu.VMEM((2, MOE_BLOCK, dp), jnp.uint32),
                        pltpu.VMEM((d, ff), F32),
                        pltpu.VMEM((d, ff), F32),
                        pltpu.VMEM((ff, d), F32),
                        pltpu.VMEM((d, ff), BF16),
                        pltpu.VMEM((d, ff), BF16),
                        pltpu.VMEM((ff, d), BF16),
                        pltpu.SemaphoreType.DMA((2,)),
                        pltpu.SemaphoreType.DMA((2,)),
                        pltpu.SemaphoreType.DMA((3,))],
    )
    return pl.pallas_call(
        functools.partial(_expert_kernel, n_rows=n_rows_pad - SORTED_TAIL),
        grid_spec=grid_spec,
        out_shape=jax.ShapeDtypeStruct((n_rows_pad, dp), jnp.uint32),
        compiler_params=_cparams(("arbitrary",)),
        name="experts",
    )(block_expert, block_nreal, block_row0, block_next, xs, wg, wu, wd)


def _combine_gather(ys, pos):
    t = pos.shape[1]
    dp = ys.shape[1]
    ncores, nw = _sc_workers()
    per_w = t // nw
    n_ch = per_w // SC_CHUNK
    assert per_w % SC_CHUNK == 0
    idx = [pos[k].reshape(nw, n_ch, SC_CHUNK) for k in range(TOP_K)]
    mesh = plsc.VectorSubcoreMesh(core_axis_name="c", subcore_axis_name="s")

    @functools.partial(
        pl.kernel, mesh=mesh,
        out_type=jax.ShapeDtypeStruct((TOP_K, t, dp), ys.dtype),
        scratch_types=[pltpu.VMEM((TOP_K, n_ch, SC_CHUNK), jnp.int32),
                       pltpu.VMEM((SC_BUFFERS, SC_CHUNK, dp), ys.dtype),
                       pltpu.SemaphoreType.DMA((SC_BUFFERS,)), pltpu.SemaphoreType.DMA((SC_BUFFERS,))],
    )
    def gather(ys_hbm, p0_hbm, p1_hbm, out_hbm, idx_v, rows_v, gsem, wsem):
        wid = lax.axis_index("s") * ncores + lax.axis_index("c")
        base = wid * per_w
        pltpu.sync_copy(p0_hbm.at[wid], idx_v.at[0])
        pltpu.sync_copy(p1_hbm.at[wid], idx_v.at[1])
        units = [(c, k) for c in range(n_ch) for k in range(TOP_K)]

        def fetch(u):
            c, k = units[u]
            return pltpu.async_copy(ys_hbm.at[idx_v.at[k, c]], rows_v.at[u % SC_BUFFERS], gsem.at[u % SC_BUFFERS])

        fetches = {u: fetch(u) for u in range(min(SC_BUFFERS - 1, len(units)))}
        writes = {}
        for u, (c, k) in enumerate(units):
            fetches[u].wait()
            writes[u] = pltpu.async_copy(rows_v.at[u % SC_BUFFERS],
                                         out_hbm.at[k, pl.ds(base + c * SC_CHUNK, SC_CHUNK)],
                                         wsem.at[u % SC_BUFFERS])
            if u >= 1:
                writes[u - 1].wait()
            if u + SC_BUFFERS - 1 < len(units):
                fetches[u + SC_BUFFERS - 1] = fetch(u + SC_BUFFERS - 1)
        writes[len(units) - 1].wait()

    return gather(ys, idx[0], idx[1])


def _final_kernel(h_ref, y_ref, route_ref, gain_ref, out_ref):
    r = route_ref[...]
    y = h_ref[...]
    for k, gate_lane in enumerate((ROUTE_G1, ROUTE_G2)):
        y = y + r[:, gate_lane:gate_lane + 1] * _unpack_bf16_pairs(y_ref[k])
    out_ref[...] = y * lax.rsqrt(jnp.mean(y * y, axis=-1, keepdims=True) + NORM_EPS) * gain_ref[...]


def _final(h, y_rows, route, gain, tm):
    t, d = h.shape
    return pl.pallas_call(
        _final_kernel,
        grid=(t // tm,),
        in_specs=[pl.BlockSpec((tm, d), lambda i: (i, 0)),
                  pl.BlockSpec((TOP_K, tm, d // 2), lambda i: (0, i, 0)),
                  pl.BlockSpec((tm, LANES), lambda i: (i, 0)),
                  pl.BlockSpec((1, d), lambda i: (0, 0))],
        out_specs=pl.BlockSpec((tm, d), lambda i: (i, 0)),
        out_shape=jax.ShapeDtypeStruct((t, d), F32),
        compiler_params=_cparams(("arbitrary",)),
        name="final",
    )(h, y_rows, route, gain.reshape(1, d))


def _layer(h3, mix_gain, w_in, attn_gain, decay_f, decay_b, ret_gain, w_out, ffn_gain,
           w_rg, b_rg, w_re, b_re, w_eg, w_eu, w_ed, final_gain):
    b, s, d = h3.shape
    t = b * s
    tm = 512
    attn_qkv, (qr, kr, vr, gr) = _inproj(h3, mix_gain, w_in.astype(BF16), 1024)

    o_list, lse_list = [], []
    for window, dilation in DILATED_BRANCHES:
        reach = (window // 2) // dilation
        o, lse = _attn_branch(*attn_qkv[dilation], dilation, reach, tq=128, tqb=1024, sub_tiles=8)
        o_list.append(o)
        lse_list.append(lse)

    lg_f = jnp.log1p(-jnp.exp2(decay_f.astype(F32)))
    lg_b = jnp.log1p(-jnp.exp2(decay_b.astype(F32)))
    orr = _retention(qr, kr, vr, gr, lg_f, lg_b, ret_gain).reshape(t, HEAD_GROUP_WIDTH)

    n_route = MOE_GROUPS + N_EXPERTS
    w_router = jnp.zeros((d, LANES), F32).at[:, :n_route].set(jnp.concatenate([w_re, w_rg], axis=1).astype(F32))
    b_router = jnp.zeros((1, LANES), F32).at[0, :n_route].set(jnp.concatenate([b_re, b_rg]).astype(F32))
    h, hn_packed, logits = _outproj(h3.reshape(t, d), o_list, lse_list, orr, attn_gain, w_out.astype(BF16),
                                    ffn_gain, w_router, b_router, 2 * tm)

    route, route_t, counts_rep = _route(logits, 2 * tm)

    n_blocks = -(-TOP_K * t // MOE_BLOCK) + N_EXPERTS
    counts = counts_rep[:, 0].astype(jnp.int32)
    aligned = ((counts + SEG_ALIGN - 1) // SEG_ALIGN) * SEG_ALIGN
    seg_start = jnp.cumsum(aligned) - aligned
    fields = jnp.transpose(route_t, (1, 0, 2)).reshape(SUBLANES, t)
    e12 = fields[ROUTE_E1:ROUTE_E2 + 1].astype(jnp.int32)
    r12 = fields[ROUTE_R1:ROUTE_R2 + 1].astype(jnp.int32)
    pos = r12 + jnp.sum(jnp.where(e12[None] == jnp.arange(N_EXPERTS, dtype=jnp.int32)[:, None, None],
                                  seg_start[:, None, None], 0), axis=0)
    nblk = (counts + MOE_BLOCK - 1) // MOE_BLOCK
    blk_end = jnp.cumsum(nblk)
    blk_start = blk_end - nblk
    blk = jnp.arange(n_blocks, dtype=jnp.int32)[:, None]
    owner = (blk >= blk_start) & (blk < blk_end)
    local = (blk - blk_start) * MOE_BLOCK
    block_row0 = jnp.sum(jnp.where(owner, seg_start + local, 0), axis=-1).astype(jnp.int32)
    block_nreal = jnp.sum(jnp.where(owner, jnp.clip(counts - local, 0, MOE_BLOCK), 0), axis=-1).astype(jnp.int32)
    block_expert = jnp.minimum(jnp.sum((blk >= blk_end).astype(jnp.int32), axis=-1), N_EXPERTS - 1)
    e_iota = jnp.arange(N_EXPERTS, dtype=jnp.int32)
    later = (e_iota[None, :] > e_iota[:, None]) & (counts[None, :] > 0)
    next_expert = jnp.where(jnp.any(later, axis=1), jnp.min(jnp.where(later, e_iota[None, :], N_EXPERTS), axis=1), -1)
    block_next = jnp.sum(jnp.where(owner, next_expert, 0), axis=-1).astype(jnp.int32)

    n_sorted = TOP_K * t + SORTED_TAIL
    dump_row = n_sorted
    seg_end = seg_start + counts
    hole = seg_end[:, None] + jnp.arange(SEG_ALIGN, dtype=jnp.int32)[None, :]
    hole = jnp.where(hole < (seg_start + aligned)[:, None], hole, dump_row).reshape(-1)
    tail = (seg_start[-1] + aligned[-1]) + jnp.arange(SORTED_TAIL, dtype=jnp.int32)
    tail = jnp.where(tail < n_sorted, tail, dump_row)
    zero_pos = jnp.concatenate([hole, tail]).astype(jnp.int32)

    xs = _dispatch(hn_packed, pos, zero_pos)
    ys = _experts(xs, block_expert, block_nreal, block_row0, block_next, w_eg.astype(F32), w_eu.astype(F32),
                  w_ed.astype(F32))
    out = _final(h, _combine_gather(ys, pos), route, final_gain, tm)
    return out.reshape(b, s, d)


def kernel(x, mix_norm_gain, w_in, attn_out_gain, ret_decay_fwd, ret_decay_bwd, ret_out_gain, w_out,
           ffn_norm_gain, w_route_group, b_route_group, w_route_expert, b_route_expert,
           w_expert_gate, w_expert_up, w_expert_down, final_norm_gain):
    depth = mix_norm_gain.shape[0]
    assert depth == 1, "the final rmsnorm is fused into the single layer's combine kernel"
    l = 0
    return _layer(x, mix_norm_gain[l], w_in[l], attn_out_gain[l], ret_decay_fwd[l], ret_decay_bwd[l],
                  ret_out_gain[l], w_out[l], ffn_norm_gain[l], w_route_group[l], b_route_group[l],
                  w_route_expert[l], b_route_expert[l], w_expert_gate[l], w_expert_up[l], w_expert_down[l],
                  final_norm_gain)
```

```python
import functools

import numpy as np
import jax
import jax.numpy as jnp
from jax import lax
from jax.experimental import pallas as pl
from jax.experimental.pallas import tpu as pltpu
from jax.experimental.pallas import tpu_sc as plsc

F32 = jnp.float32
BF16 = jnp.bfloat16

ATTN_HEADS = 8
HEAD_DIM = 64
RET_HEADS = 8
HEAD_GROUP_WIDTH = 512
N_PROJ_GROUPS = 7
DILATED_BRANCHES = ((128, 1), (512, 4), (2048, 16))
ROPE_THETA = 500000.0
ROPE_DIM = HEAD_DIM // 4
RET_THETA = 10000.0
RET_CHUNK = 128
MOE_GROUPS = 4
EXPERTS_PER_GROUP = 8
N_EXPERTS = MOE_GROUPS * EXPERTS_PER_GROUP
MOE_BLOCK = 512
NORM_EPS = 1e-6
NEG_INF = -1e30

LANES = 128
VMEM_LIMIT = 56 * 1024 * 1024


def _cparams(sem):
    return pltpu.CompilerParams(dimension_semantics=sem, vmem_limit_bytes=VMEM_LIMIT)


def _rotary_tables(seq, half, freqs):
    pos = np.arange(seq, dtype=np.float64)[:, None]
    ang = pos * freqs[None, :].astype(np.float64)
    cos, sin = np.cos(ang), np.sin(ang)
    c = np.ones((seq, HEAD_DIM)); sp = np.zeros((seq, HEAD_DIM)); sm = np.zeros((seq, HEAD_DIM))
    c[:, :half] = cos; c[:, half:2 * half] = cos
    sp[:, half:2 * half] = sin
    sm[:, :half] = -sin
    rep = LANES // HEAD_DIM
    return tuple(jnp.asarray(np.tile(t, (1, rep)), F32) for t in (c, sp, sm))


def _rotate(t, c, sp, sm, half):
    outs = []
    for g in range(t.shape[1] // LANES):
        tg = t[:, g * LANES:(g + 1) * LANES]
        outs.append(tg * c + pltpu.roll(tg, half, 1) * sp + pltpu.roll(tg, LANES - half, 1) * sm)
    return jnp.concatenate(outs, axis=1)


CLASS_DILATIONS = tuple(d for _, d in DILATED_BRANCHES if d > 1)
ATTN_Q_SCALE = float(np.log2(np.e)) * HEAD_DIM ** -0.5


def _inproj_kernel(x_ref, gain_ref, w_ref, ca_ref, spa_ref, sma_ref, cr_ref, spr_ref, smr_ref, *rest):
    n_cls = len(CLASS_DILATIONS)
    nat_refs = rest[0:3]
    cls_refs = [rest[3 + 3 * c:6 + 3 * c] for c in range(n_cls)]
    qr_ref, kr_ref, vr_ref, gr_ref = rest[3 + 3 * n_cls:7 + 3 * n_cls]
    stage_ref = rest[7 + 3 * n_cls]
    x = x_ref[0]
    tm = x.shape[0]
    ms = jnp.mean(x * x, axis=-1, keepdims=True)
    xn = (x * lax.rsqrt(ms + NORM_EPS) * gain_ref[...]).astype(BF16)
    gw = HEAD_GROUP_WIDTH

    def proj(c):
        return jnp.dot(xn, w_ref[:, c * gw:(c + 1) * gw], preferred_element_type=F32)

    a_tabs = (ca_ref[...], spa_ref[...], sma_ref[...])
    r_tabs = (cr_ref[...], spr_ref[...], smr_ref[...])
    attn_vals = ((_rotate(proj(0), *a_tabs, ROPE_DIM // 2) * ATTN_Q_SCALE),
                 _rotate(proj(1), *a_tabs, ROPE_DIM // 2),
                 proj(2))
    for j, val in enumerate(attn_vals):
        nat_refs[j][0] = val.astype(BF16)
        for g in range(gw // LANES):
            stage_ref[g] = val[:, g * LANES:(g + 1) * LANES]
        for c, d in enumerate(CLASS_DILATIONS):
            for r in range(d):
                for g in range(gw // LANES):
                    col = r * gw + g * LANES
                    cls_refs[c][j][0, :, col:col + LANES] = (
                        stage_ref[g, pl.ds(r, tm // d, stride=d), :].astype(BF16))
    qr_ref[0] = _rotate(proj(3), *r_tabs, HEAD_DIM // 2).astype(BF16)
    kr_ref[0] = (_rotate(proj(4), *r_tabs, HEAD_DIM // 2) * (HEAD_DIM ** -0.5)).astype(BF16)
    vr_ref[0] = proj(5).astype(BF16)
    g = proj(6)
    gr_ref[0] = (g * jax.nn.sigmoid(g)).astype(BF16)


def _inproj(x, gain, w_in_bf16, tm):
    b, s, d = x.shape
    rope_freqs = ROPE_THETA ** (-np.arange(0, ROPE_DIM, 2, dtype=np.float32) / ROPE_DIM)
    ret_freqs = RET_THETA ** (-np.linspace(0.0, 1.0, HEAD_DIM // 2, dtype=np.float32))
    tabs = _rotary_tables(s, ROPE_DIM // 2, rope_freqs) + _rotary_tables(s, HEAD_DIM // 2, ret_freqs)
    gw = HEAD_GROUP_WIDTH
    tab_spec = pl.BlockSpec((tm, LANES), lambda si, bi: (si, 0))

    def view(dil):
        return (pl.BlockSpec((1, tm // dil, dil * gw), lambda si, bi: (bi, si, 0)),
                jax.ShapeDtypeStruct((b, s // dil, dil * gw), BF16))

    views = [view(1)] * 3 + [view(dil) for dil in CLASS_DILATIONS for _ in range(3)] + [view(1)] * 4
    outs = pl.pallas_call(
        _inproj_kernel,
        grid=(s // tm, b),
        in_specs=[pl.BlockSpec((1, tm, d), lambda si, bi: (bi, si, 0)),
                  pl.BlockSpec((1, d), lambda si, bi: (0, 0)),
                  pl.BlockSpec(w_in_bf16.shape, lambda si, bi: (0, 0))] + [tab_spec] * 6,
        out_specs=[v[0] for v in views],
        out_shape=[v[1] for v in views],
        scratch_shapes=[pltpu.VMEM((gw // LANES, tm, LANES), F32)],
        compiler_params=_cparams(("arbitrary", "arbitrary")),
        name="inproj",
    )(x, gain.reshape(1, d), w_in_bf16, *tabs)
    n_attn = 3 * (1 + len(CLASS_DILATIONS))
    attn_qkv = {dil: outs[3 * c:3 * c + 3] for c, dil in enumerate((1,) + CLASS_DILATIONS)}
    return attn_qkv, outs[n_attn:]


ATTN_WINDOW_CASES = 3


def _attn_kernel(q_ref, k_ref, v_ref, o_ref, lse_ref, bias_ref, *, length, tq, reach):
    tqb = q_ref.shape[1]
    win = tq + 2 * reach
    heads_per_pair = LANES // HEAD_DIM
    qi = pl.program_id(2)
    lane = lax.broadcasted_iota(jnp.int32, (1, LANES), 1)
    lane_t = lax.broadcasted_iota(jnp.int32, (tq, LANES), 1)
    ones = jnp.ones((win, LANES), BF16)

    @pl.when((pl.program_id(0) == 0) & (pl.program_id(1) == 0) & (qi == 0))
    def _():
        diff = (lax.broadcasted_iota(jnp.int32, (heads_per_pair * tq, win), 1)
                - (lax.broadcasted_iota(jnp.int32, (heads_per_pair * tq, win), 0) & (tq - 1)))
        for case in range(ATTN_WINDOW_CASES):
            off = case * reach
            bias_ref[case] = jnp.where((diff >= off - reach) & (diff <= off + reach), 0.0, NEG_INF)

    def sub(t, cls):
        q0 = qi * tqb + t * tq
        ws = jnp.clip(q0 - reach, 0, length - win)
        ws = pl.multiple_of(ws, reach)
        bias = bias_ref[(q0 - ws) // reach]
        rows = pl.ds(t * tq, tq)
        m_tile = jnp.zeros((tq, LANES), F32)
        l_tile = jnp.ones((tq, LANES), F32)
        for g in range(HEAD_GROUP_WIDTH // LANES):
            cols = slice(cls * HEAD_GROUP_WIDTH + g * LANES, cls * HEAD_GROUP_WIDTH + (g + 1) * LANES)
            qg = q_ref[0, rows, cols]
            kw = k_ref[0, pl.ds(ws, win), cols]
            v_ones = jnp.concatenate([v_ref[0, pl.ds(ws, win), cols], ones], axis=1)
            hms = [(lane >= hh * HEAD_DIM) & (lane < (hh + 1) * HEAD_DIM) for hh in range(heads_per_pair)]
            q2 = jnp.concatenate([qg * hm.astype(BF16) for hm in hms], axis=0)
            sc = lax.dot_general(q2, kw, (((1,), (1,)), ((), ())), preferred_element_type=F32) + bias
            m = jnp.max(sc, axis=-1, keepdims=True)
            p = jnp.exp2(sc - m).astype(BF16)
            pv = jnp.dot(p, v_ones, preferred_element_type=F32)
            l = pv[:, LANES:]
            o = pv[:, :LANES] / l
            o_pair = jnp.zeros((tq, LANES), F32)
            for hh in range(heads_per_pair):
                part = slice(hh * tq, (hh + 1) * tq)
                head_lane = lane_t == g * heads_per_pair + hh
                o_pair = jnp.where(hms[hh], o[part], o_pair)
                m_tile = jnp.where(head_lane, m[part], m_tile)
                l_tile = jnp.where(head_lane, l[part], l_tile)
            o_ref[0, rows, cols] = o_pair.astype(BF16)
        lse_ref[0, rows, cls * LANES:(cls + 1) * LANES] = m_tile + jnp.log2(l_tile)

    for cls in range(q_ref.shape[2] // HEAD_GROUP_WIDTH):
        for t in range(tqb // tq):
            sub(t, cls)


def _attn_branch(qc, kc, vc, dilation, reach, tq, tqb, sub_tiles=4):
    b, length, dw = qc.shape
    w = dw // dilation
    tqb = min(tqb, length)
    assert tq % reach == 0 and tq > reach and length % tq == 0 and length >= tq + 2 * reach
    ncls = min(dilation, max(1, sub_tiles // (tqb // tq)))
    q_spec = pl.BlockSpec((1, tqb, ncls * w), lambda bi, r, qi: (bi, qi, r))
    kv_spec = pl.BlockSpec((1, length, ncls * w), lambda bi, r, qi: (bi, 0, r))
    o, lse = pl.pallas_call(
        functools.partial(_attn_kernel, length=length, tq=tq, reach=reach),
        grid=(b, dilation // ncls, length // tqb),
        in_specs=[q_spec, kv_spec, kv_spec],
        out_specs=[q_spec, pl.BlockSpec((1, tqb, ncls * LANES), lambda bi, r, qi: (bi, qi, r))],
        out_shape=[jax.ShapeDtypeStruct((b, length, dilation * w), BF16),
                   jax.ShapeDtypeStruct((b, length, dilation * LANES), F32)],
        scratch_shapes=[pltpu.VMEM((ATTN_WINDOW_CASES, (LANES // HEAD_DIM) * tq, tq + 2 * reach), F32)],
        compiler_params=_cparams(("arbitrary", "arbitrary", "arbitrary")),
        name=f"attn_d{dilation}",
    )(qc, kc, vc)
    return o, lse


RET_TAB_QF, RET_TAB_QB, RET_TAB_KF, RET_TAB_KB = range(4)


def _retention_kernel(lgf_ref, lgb_ref, q_ref, k_ref, v_ref, g_ref, gain_ref, o_ref,
                      tab_ref, dec_ref, sb_ref, st_ref, *, chunk, unroll):
    c = chunk
    n = q_ref.shape[1] // c
    width = q_ref.shape[2]
    n_pairs = width // LANES
    heads_per_pair = LANES // HEAD_DIM
    n_heads = n_pairs * heads_per_pair
    head0 = pl.program_id(1) * n_heads
    lane_w = lax.broadcasted_iota(jnp.int32, (1, width), 1)
    lgf = [lgf_ref[head0 + hd] for hd in range(n_heads)]
    lgb = [lgb_ref[head0 + hd] for hd in range(n_heads)]
    lgf_lane = jnp.zeros((1, width), F32)
    lgb_lane = jnp.zeros((1, width), F32)
    for hd in range(n_heads):
        in_head = (lane_w >= hd * HEAD_DIM) & (lane_w < (hd + 1) * HEAD_DIM)
        lgf_lane = jnp.where(in_head, lgf[hd], lgf_lane)
        lgb_lane = jnp.where(in_head, lgb[hd], lgb_lane)
    idx = lax.broadcasted_iota(jnp.int32, (c, width), 0).astype(F32)
    tab_ref[RET_TAB_QF] = jnp.exp((idx + 1.0) * lgf_lane)
    tab_ref[RET_TAB_QB] = jnp.exp((c - idx) * lgb_lane)
    tab_ref[RET_TAB_KF] = jnp.exp((c - 1.0 - idx) * lgf_lane)
    tab_ref[RET_TAB_KB] = jnp.exp(idx * lgb_lane)
    sdf = jnp.exp(c * lgf_lane)
    sdb = jnp.exp(c * lgb_lane)
    dmat = (lax.broadcasted_iota(jnp.int32, (c, c), 0)
            - lax.broadcasted_iota(jnp.int32, (c, c), 1)).astype(F32)
    for hd in range(n_heads):
        dec_ref[hd // heads_per_pair, :, (hd % heads_per_pair) * c:(hd % heads_per_pair + 1) * c] = (
            jnp.where(dmat >= 0, jnp.exp(dmat * lgf[hd]), jnp.exp(-dmat * lgb[hd])))
    lane = lax.broadcasted_iota(jnp.int32, (1, LANES), 1)
    lane_hi = lane >= HEAD_DIM
    head_masks = [((lane >= hh * HEAD_DIM) & (lane < (hh + 1) * HEAD_DIM)).astype(BF16)
                  for hh in range(heads_per_pair)]
    row_hi = lax.broadcasted_iota(jnp.int32, (LANES, LANES), 0) >= HEAD_DIM
    col_hi = lax.broadcasted_iota(jnp.int32, (LANES, LANES), 1) >= HEAD_DIM
    blockdiag = row_hi == col_hi

    def kv_state(kd, vv):
        kv = lax.dot_general(kd.astype(BF16), vv, (((0,), (0,)), ((), ())), preferred_element_type=F32)
        return jnp.where(blockdiag, kv, 0.0)

    st_ref[...] = jnp.zeros_like(st_ref)
    sb_ref[n - 1] = jnp.zeros(sb_ref.shape[1:], sb_ref.dtype)

    def back(i, carry):
        nn = n - 1 - i
        rows = pl.ds(pl.multiple_of(nn * c, c), c)
        for p in range(n_pairs):
            cols = slice(p * LANES, (p + 1) * LANES)
            kd = k_ref[0, rows, cols].astype(F32) * tab_ref[RET_TAB_KB, :, cols]
            new = st_ref[p] * sdb[:, cols] + kv_state(kd, v_ref[0, rows, cols])
            st_ref[p] = new
            sb_ref[nn - 1, p] = new.astype(BF16)
        return carry

    lax.fori_loop(0, n - 1, back, 0, unroll=unroll)

    st_ref[...] = jnp.zeros_like(st_ref)

    def fwd(nn, carry):
        rows = pl.ds(pl.multiple_of(nn * c, c), c)
        for p in range(n_pairs):
            cols = slice(p * LANES, (p + 1) * LANES)
            qq = q_ref[0, rows, cols]
            kk = k_ref[0, rows, cols]
            vv = v_ref[0, rows, cols]
            qf = qq.astype(F32)
            sf = st_ref[p]
            k2 = jnp.concatenate([kk * hm for hm in head_masks], axis=0)
            v2 = jnp.concatenate([vv * hm for hm in head_masks], axis=0)
            sc = lax.dot_general(qq, k2, (((1,), (1,)), ((), ())), preferred_element_type=F32)
            lhs = jnp.concatenate([(qf * tab_ref[RET_TAB_QF, :, cols]).astype(BF16),
                                   (qf * tab_ref[RET_TAB_QB, :, cols]).astype(BF16),
                                   (sc * dec_ref[p]).astype(BF16)], axis=1)
            rhs = jnp.concatenate([sf.astype(BF16), sb_ref[nn, p], v2], axis=0)
            o = jnp.dot(lhs, rhs, preferred_element_type=F32)
            s_lo = jnp.sum(jnp.where(lane_hi, 0.0, o), axis=-1, keepdims=True)
            s_hi = jnp.sum(jnp.where(lane_hi, o, 0.0), axis=-1, keepdims=True)
            mu = jnp.where(lane_hi, s_hi, s_lo) * (1.0 / HEAD_DIM)
            dev = o - mu
            d2 = dev * dev
            v_lo = jnp.sum(jnp.where(lane_hi, 0.0, d2), axis=-1, keepdims=True)
            v_hi = jnp.sum(jnp.where(lane_hi, d2, 0.0), axis=-1, keepdims=True)
            var = jnp.where(lane_hi, v_hi, v_lo) * (1.0 / HEAD_DIM)
            out = dev * lax.rsqrt(var + NORM_EPS) * gain_ref[:, cols] * g_ref[0, rows, cols].astype(F32)
            o_ref[0, rows, cols] = out.astype(BF16)
            kd = kk.astype(F32) * tab_ref[RET_TAB_KF, :, cols]
            st_ref[p] = sf * sdf[:, cols] + kv_state(kd, vv)
        return carry

    lax.fori_loop(0, n, fwd, 0, unroll=unroll)


def _retention(qr, kr, vr, gate, lg_f, lg_b, out_gain, width=512, unroll=8):
    b, s, w = qr.shape
    n_pairs = width // LANES
    n_heads = width // HEAD_DIM
    spec = pl.BlockSpec((1, s, width), lambda bi, p, *_: (bi, 0, p))
    grid_spec = pltpu.PrefetchScalarGridSpec(
        num_scalar_prefetch=2,
        grid=(b, w // width),
        in_specs=[spec, spec, spec, spec, pl.BlockSpec((1, width), lambda bi, p, *_: (0, p))],
        out_specs=spec,
        scratch_shapes=[pltpu.VMEM((4, RET_CHUNK, width), F32),
                        pltpu.VMEM((n_pairs, RET_CHUNK, (LANES // HEAD_DIM) * RET_CHUNK), F32),
                        pltpu.VMEM((s // RET_CHUNK, n_pairs, LANES, LANES), BF16),
                        pltpu.VMEM((n_pairs, LANES, LANES), F32)],
    )
    return pl.pallas_call(
        functools.partial(_retention_kernel, chunk=RET_CHUNK, unroll=unroll),
        grid_spec=grid_spec,
        out_shape=jax.ShapeDtypeStruct((b, s, w), BF16),
        compiler_params=_cparams(("arbitrary", "arbitrary")),
        name="retention",
    )(lg_f, lg_b, qr, kr, vr, gate, out_gain.reshape(1, w))


def _split_bf16(t):
    hi = t.astype(BF16)
    lo = (t - hi.astype(F32)).astype(BF16)
    return hi, lo


def _pack_bf16_pairs(t):
    n = t.shape[1] // 2
    hi = pltpu.bitcast(t[:, :n].astype(BF16).astype(F32), jnp.uint32)
    lo = pltpu.bitcast(t[:, n:].astype(BF16).astype(F32), jnp.uint32)
    return hi | (lo >> 16)


def _unpack_bf16_pairs(u):
    hi = pltpu.bitcast(u & jnp.uint32(0xFFFF0000), F32)
    lo = pltpu.bitcast(u << 16, F32)
    return jnp.concatenate([hi, lo], axis=1)


def _outproj_kernel(x_ref, o1_ref, o2_ref, o3_ref, l1_ref, l2_ref, l3_ref, orr_ref, ga_ref, expand_ref,
                    wout_ref, gf_ref, wr_ref, br_ref, h_ref, hn_ref, logit_ref,
                    *nat_refs):
    tm = x_ref.shape[0]
    gw = HEAD_GROUP_WIDTH
    os, ls = [], []
    for (_, dil), o_ref, l_ref in zip(DILATED_BRANCHES, (o1_ref, o2_ref, o3_ref), (l1_ref, l2_ref, l3_ref)):
        if dil == 1:
            os.append(o_ref[...].astype(F32))
            ls.append(l_ref[...])
            continue
        c = CLASS_DILATIONS.index(dil)
        onat_ref, lnat_ref = nat_refs[2 * c], nat_refs[2 * c + 1]
        for r in range(dil):
            rows = pl.ds(r, tm // dil, stride=dil)
            for g in range(gw // LANES):
                col = r * gw + g * LANES
                onat_ref[g, rows, :] = o_ref[:, col:col + LANES].astype(F32)
            lnat_ref[rows, :] = l_ref[:, r * LANES:(r + 1) * LANES]
        os.append(jnp.concatenate([onat_ref[g] for g in range(gw // LANES)], axis=1))
        ls.append(lnat_ref[...])
    mx = jnp.maximum(jnp.maximum(ls[0], ls[1]), ls[2])
    es = [jnp.exp2(l - mx) for l in ls]
    inv = 1.0 / (es[0] + es[1] + es[2])
    expand = expand_ref[...]
    oa = jnp.zeros((tm, gw), F32)
    for e, o in zip(es, os):
        wexp = jnp.dot(jnp.concatenate(_split_bf16(e * inv), axis=1), expand, preferred_element_type=F32)
        oa = oa + wexp * o
    oa = oa * lax.rsqrt(jnp.mean(oa * oa, axis=-1, keepdims=True) + NORM_EPS) * ga_ref[...]
    mixed = jnp.concatenate([oa.astype(BF16), orr_ref[...]], axis=1)
    h = x_ref[...] + jnp.dot(mixed, wout_ref[...], preferred_element_type=F32)
    h_ref[...] = h
    hn = h * lax.rsqrt(jnp.mean(h * h, axis=-1, keepdims=True) + NORM_EPS) * gf_ref[...]
    hn_ref[...] = _pack_bf16_pairs(hn)
    prod = jnp.dot(jnp.concatenate(_split_bf16(hn), axis=0), wr_ref[...], preferred_element_type=F32)
    logit_ref[...] = prod[:tm, :LANES] + prod[:tm, LANES:] + prod[tm:, :LANES] + br_ref[...]


def _outproj(x2, o_list, lse_list, orr, attn_gain, w_out_bf16, ffn_gain, w_router, b_router, tm):
    t, d = x2.shape
    w = HEAD_GROUP_WIDTH
    expand = np.zeros((LANES, w), np.float32)
    for hd in range(ATTN_HEADS):
        expand[hd, hd * HEAD_DIM:(hd + 1) * HEAD_DIM] = 1.0
    expand = jnp.asarray(np.concatenate([expand, expand], axis=0), BF16)
    wr_hi = w_router.astype(BF16)
    wr = jnp.concatenate([wr_hi, (w_router - wr_hi.astype(F32)).astype(BF16)], axis=1)
    row = lambda width, dil=1: pl.BlockSpec((tm // dil, dil * width), lambda i: (i, 0))
    full = lambda a: pl.BlockSpec(a.shape, lambda i: (0,) * a.ndim)
    ga = attn_gain.reshape(1, w)
    gf = ffn_gain.reshape(1, d)
    dils = [dil for _, dil in DILATED_BRANCHES]
    o_flat = [o.reshape(t // dil, dil * w) for o, dil in zip(o_list, dils)]
    l_flat = [l.reshape(t // dil, dil * LANES) for l, dil in zip(lse_list, dils)]
    nat_scratch = []
    for _ in CLASS_DILATIONS:
        nat_scratch += [pltpu.VMEM((w // LANES, tm, LANES), F32), pltpu.VMEM((tm, LANES), F32)]
    return pl.pallas_call(
        _outproj_kernel,
        grid=(t // tm,),
        in_specs=[row(d)] + [row(w, dil) for dil in dils] + [row(LANES, dil) for dil in dils] + [row(w)]
                 + [full(ga), full(expand), full(w_out_bf16), full(gf), full(wr), full(b_router)],
        out_specs=[row(d), row(d // 2), row(LANES)],
        out_shape=[jax.ShapeDtypeStruct((t, d), F32),
                   jax.ShapeDtypeStruct((t, d // 2), jnp.uint32),
                   jax.ShapeDtypeStruct((t, LANES), F32)],
        scratch_shapes=nat_scratch,
        compiler_params=_cparams(("arbitrary",)),
        name="outproj",
    )(x2, *o_flat, *l_flat, orr, ga, expand, w_out_bf16, gf, wr, b_router)


ROUTE_E1, ROUTE_E2, ROUTE_G1, ROUTE_G2, ROUTE_R1, ROUTE_R2 = range(6)
GROUP_LANE0 = N_EXPERTS
SUBLANES = 8


def _route_kernel(logit_ref, tri_ref, route_ref, route_t_ref, count_ref, run_ref):
    @pl.when(pl.program_id(0) == 0)
    def _():
        run_ref[...] = jnp.zeros_like(run_ref)

    lg_t = jnp.transpose(logit_ref[...])
    tm = lg_t.shape[1]
    assert EXPERTS_PER_GROUP == SUBLANES and MOE_GROUPS <= SUBLANES
    rid = lax.broadcasted_iota(jnp.int32, (SUBLANES, tm), 0)
    big = jnp.int32(1 << 20)

    def top(vals):
        m = jnp.max(vals, axis=0, keepdims=True)
        i = jnp.min(jnp.where(vals == m, rid, big), axis=0, keepdims=True)
        return m, i

    gl = jnp.where(rid < MOE_GROUPS, lg_t[GROUP_LANE0:GROUP_LANE0 + SUBLANES], -jnp.inf)
    gmax, gidx = top(gl)
    group_gate = 1.0 / jnp.sum(jnp.exp(gl - gmax), axis=0, keepdims=True)
    el = lg_t[0:EXPERTS_PER_GROUP]
    for g in range(1, MOE_GROUPS):
        el = jnp.where(gidx == g, lg_t[g * EXPERTS_PER_GROUP:(g + 1) * EXPERTS_PER_GROUP], el)
    t1, i1 = top(el)
    t2, i2 = top(jnp.where(rid == i1, -jnp.inf, el))
    e21 = jnp.exp(t2 - t1)
    g1 = group_gate / (1.0 + e21)
    g2 = group_gate * e21 / (1.0 + e21)
    e1 = gidx * EXPERTS_PER_GROUP + i1
    e2 = gidx * EXPERTS_PER_GROUP + i2
    erow = lax.broadcasted_iota(jnp.int32, (N_EXPERTS, tm), 0)
    oh1 = erow == e1
    oh2 = erow == e2
    cnt = oh1.astype(F32) + oh2.astype(F32)
    run = run_ref[:, 0:1]
    prefix = jnp.dot(cnt.astype(BF16), tri_ref[...], preferred_element_type=F32) + run
    r1 = jnp.sum(jnp.where(oh1, prefix, 0.0), axis=0, keepdims=True)
    r2 = jnp.sum(jnp.where(oh2, prefix, 0.0), axis=0, keepdims=True)
    new_run = jnp.broadcast_to(run + jnp.sum(cnt, axis=1, keepdims=True), run_ref.shape)
    run_ref[...] = new_run
    count_ref[...] = new_run
    out = jnp.zeros((SUBLANES, tm), F32)
    for row, val in ((ROUTE_E1, e1.astype(F32)), (ROUTE_E2, e2.astype(F32)), (ROUTE_G1, g1),
                     (ROUTE_G2, g2), (ROUTE_R1, r1), (ROUTE_R2, r2)):
        out = jnp.where(rid == row, val, out)
    route_t_ref[0] = out
    out_t = jnp.concatenate([out, jnp.zeros((LANES - SUBLANES, tm), F32)], axis=0)
    route_ref[...] = jnp.transpose(out_t)


def _route(logits, tm):
    t = logits.shape[0]
    tri = jnp.asarray(np.triu(np.ones((tm, tm), np.float32), 1), BF16)
    return pl.pallas_call(
        _route_kernel,
        grid=(t // tm,),
        in_specs=[pl.BlockSpec((tm, LANES), lambda i: (i, 0)), pl.BlockSpec((tm, tm), lambda i: (0, 0))],
        out_specs=[pl.BlockSpec((tm, LANES), lambda i: (i, 0)),
                   pl.BlockSpec((1, SUBLANES, tm), lambda i: (i, 0, 0)),
                   pl.BlockSpec((N_EXPERTS, LANES), lambda i: (0, 0))],
        out_shape=[jax.ShapeDtypeStruct((t, LANES), F32),
                   jax.ShapeDtypeStruct((t // tm, SUBLANES, tm), F32),
                   jax.ShapeDtypeStruct((N_EXPERTS, LANES), F32)],
        scratch_shapes=[pltpu.VMEM((N_EXPERTS, LANES), F32)],
        compiler_params=_cparams(("arbitrary",)),
        name="route",
    )(logits, tri)


TOP_K = 2


SEG_ALIGN = 8
SORTED_TAIL = N_EXPERTS * SEG_ALIGN + MOE_BLOCK


def _tail_pieces():
    full, rest = divmod(SORTED_TAIL, MOE_BLOCK)
    return [MOE_BLOCK] * full + ([rest] if rest else [])


SC_CHUNK = 64
SC_BUFFERS = 3
ZERO_ROWS = N_EXPERTS * SEG_ALIGN + SORTED_TAIL
SORTED_ROWS_EXTRA = SORTED_TAIL + SEG_ALIGN


def _sc_workers():
    info = plsc.get_sparse_core_info()
    return info.num_cores, info.num_cores * info.num_subcores


def _dispatch(hn_packed, pos, zero_pos):
    t, dp = hn_packed.shape
    ncores, nw = _sc_workers()
    per_w = t // nw
    n_ch = per_w // SC_CHUNK
    z_rows = ZERO_ROWS // nw
    assert per_w % SC_CHUNK == 0 and ZERO_ROWS % nw == 0 and z_rows % SEG_ALIGN == 0 and TOP_K == 2
    idx = [pos[k].reshape(nw, n_ch, SC_CHUNK) for k in range(TOP_K)]
    zeros = jnp.zeros((z_rows, dp), hn_packed.dtype)
    mesh = plsc.VectorSubcoreMesh(core_axis_name="c", subcore_axis_name="s")

    @functools.partial(
        pl.kernel, mesh=mesh,
        out_type=jax.ShapeDtypeStruct((TOP_K * t + SORTED_ROWS_EXTRA, dp), hn_packed.dtype),
        scratch_types=[pltpu.VMEM((n_ch, SC_CHUNK), jnp.int32), pltpu.VMEM((n_ch, SC_CHUNK), jnp.int32),
                       pltpu.VMEM((z_rows,), jnp.int32),
                       pltpu.VMEM((SC_BUFFERS, SC_CHUNK, dp), hn_packed.dtype),
                       pltpu.VMEM((z_rows, dp), hn_packed.dtype),
                       pltpu.SemaphoreType.DMA((SC_BUFFERS,)), pltpu.SemaphoreType.DMA((SC_BUFFERS,)),
                       pltpu.SemaphoreType.DMA],
    )
    def scatter(hn_hbm, p0_hbm, p1_hbm, zpos_hbm, zeros_hbm, xs_hbm, i0_v, i1_v, iz_v, rows_v, zero_v,
                lsem, ssem, zsem):
        wid = lax.axis_index("s") * ncores + lax.axis_index("c")
        base = wid * per_w
        zero_load = pltpu.async_copy(zeros_hbm, zero_v, zsem)
        pltpu.sync_copy(zpos_hbm.at[wid], iz_v)
        pltpu.sync_copy(p0_hbm.at[wid], i0_v)
        pltpu.sync_copy(p1_hbm.at[wid], i1_v)

        def load(c):
            return pltpu.async_copy(hn_hbm.at[pl.ds(base + c * SC_CHUNK, SC_CHUNK)], rows_v.at[c % SC_BUFFERS],
                                    lsem.at[c % SC_BUFFERS])

        loads = {c: load(c) for c in range(min(SC_BUFFERS - 1, n_ch))}
        scat = {}
        for c in range(n_ch):
            slot = c % SC_BUFFERS
            loads[c].wait()
            scat[c] = (pltpu.async_copy(rows_v.at[slot], xs_hbm.at[i0_v.at[c]], ssem.at[slot]),
                       pltpu.async_copy(rows_v.at[slot], xs_hbm.at[i1_v.at[c]], ssem.at[slot]))
            if c >= 1:
                for d in scat[c - 1]:
                    d.wait()
            if c + SC_BUFFERS - 1 < n_ch:
                loads[c + SC_BUFFERS - 1] = load(c + SC_BUFFERS - 1)
        zero_load.wait()
        zero_scatter = pltpu.async_copy(zero_v, xs_hbm.at[iz_v], zsem)
        for d in scat[n_ch - 1]:
            d.wait()
        zero_scatter.wait()

    return scatter(hn_packed, idx[0], idx[1], zero_pos.reshape(nw, z_rows), zeros)


def _expert_kernel(bexp_ref, nreal_ref, row0_ref, bnext_ref, xs_hbm, wg_hbm, wu_hbm, wd_hbm, ys_hbm,
                   xbuf, ybuf, wg_f32, wu_f32, wd_f32, wg_bf, wu_bf, wd_bf, isem, osem, wsem, *, n_rows):
    i = pl.program_id(0)
    nb = pl.num_programs(0)
    slot = i % 2
    nslot = 1 - slot
    n_cur = nreal_ref[i]
    prev = jnp.maximum(i - 1, 0)
    nxt = jnp.minimum(i + 1, nb - 1)

    def in_copy(blk, s):
        row0 = pl.multiple_of(row0_ref[blk], SEG_ALIGN)
        return pltpu.make_async_copy(xs_hbm.at[pl.ds(row0, MOE_BLOCK)], xbuf.at[s], isem.at[s])

    def out_copy(blk, s):
        row0 = pl.multiple_of(row0_ref[blk], SEG_ALIGN)
        return pltpu.make_async_copy(ybuf.at[s], ys_hbm.at[pl.ds(row0, MOE_BLOCK)], osem.at[s])

    @pl.when(i == 0)
    def _():
        ybuf[...] = jnp.zeros_like(ybuf)
        tails = [pltpu.make_async_copy(ybuf.at[s, pl.ds(0, size)], ys_hbm.at[pl.ds(n_rows + s * MOE_BLOCK, size)],
                                       osem.at[s])
                 for s, size in enumerate(_tail_pieces())]
        for tail in tails:
            tail.start()
        for tail in tails:
            tail.wait()

        @pl.when(n_cur > 0)
        def _():
            in_copy(i, slot).start()

    @pl.when((i + 1 < nb) & (nreal_ref[nxt] > 0))
    def _():
        in_copy(nxt, nslot).start()

    def weight_copies(e):
        return [pltpu.make_async_copy(w_hbm.at[e], w_f32, wsem.at[j])
                for j, (w_hbm, w_f32) in enumerate(((wg_hbm, wg_f32), (wu_hbm, wu_f32), (wd_hbm, wd_f32)))]

    @pl.when((i == 0) & (n_cur > 0))
    def _():
        for c in weight_copies(bexp_ref[0]):
            c.start()

    @pl.when((n_cur > 0) & ((i == 0) | (bexp_ref[i] != bexp_ref[prev])))
    def _():
        for c in weight_copies(bexp_ref[i]):
            c.wait()
        wg_bf[...] = wg_f32[...].astype(BF16)
        wu_bf[...] = wu_f32[...].astype(BF16)
        wd_bf[...] = wd_f32[...].astype(BF16)

        @pl.when(bnext_ref[i] >= 0)
        def _():
            for c in weight_copies(bnext_ref[i]):
                c.start()

    def swiglu(rows):
        xb = _unpack_bf16_pairs(xbuf[slot, rows]).astype(BF16)
        gate = jnp.dot(xb, wg_bf[...], preferred_element_type=F32)
        up = jnp.dot(xb, wu_bf[...], preferred_element_type=F32)
        hid = (gate * jax.nn.sigmoid(gate) * up).astype(BF16)
        ybuf[slot, rows] = _pack_bf16_pairs(jnp.dot(hid, wd_bf[...], preferred_element_type=F32))

    @pl.when(n_cur > 0)
    def _():
        in_copy(i, slot).wait()

    @pl.when(n_cur > MOE_BLOCK // 2)
    def _():
        swiglu(slice(0, MOE_BLOCK))

    @pl.when((n_cur > 0) & (n_cur <= MOE_BLOCK // 2))
    def _():
        swiglu(slice(0, MOE_BLOCK // 2))

    @pl.when((i >= 1) & (nreal_ref[prev] > 0))
    def _():
        out_copy(prev, nslot).wait()

    @pl.when(n_cur > 0)
    def _():
        out_copy(i, slot).start()

        @pl.when(i == nb - 1)
        def _():
            out_copy(i, slot).wait()


def _experts(xs, block_expert, block_nreal, block_row0, block_next, wg, wu, wd):
    n_blocks = block_expert.shape[0]
    assert n_blocks >= 2 and len(_tail_pieces()) <= 2
    n_rows_pad = xs.shape[0] - SORTED_ROWS_EXTRA + SORTED_TAIL
    dp = xs.shape[1]
    _, d, ff = wg.shape
    any_spec = pl.BlockSpec(memory_space=pl.ANY)
    grid_spec = pltpu.PrefetchScalarGridSpec(
        num_scalar_prefetch=4,
        grid=(n_blocks,),
        in_specs=[any_spec, any_spec, any_spec, any_spec],
        out_specs=any_spec,
        scratch_shapes=[pltpu.VMEM((2, MOE_BLOCK, dp), jnp.uint32),
                        pltpu.VMEM((2, MOE_BLOCK, dp), jnp.uint32),
                        pltpu.VMEM((d, ff), F32),
                        pltpu.VMEM((d, ff), F32),
                        pltpu.VMEM((ff, d), F32),
                        pltpu.VMEM((d, ff), BF16),
                        pltpu.VMEM((d, ff), BF16),
                        pltpu.VMEM((ff, d), BF16),
                        pltpu.SemaphoreType.DMA((2,)),
                        pltpu.SemaphoreType.DMA((2,)),
                        pltpu.SemaphoreType.DMA((3,))],
    )
    return pl.pallas_call(
        functools.partial(_expert_kernel, n_rows=n_rows_pad - SORTED_TAIL),
        grid_spec=grid_spec,
        out_shape=jax.ShapeDtypeStruct((n_rows_pad, dp), jnp.uint32),
        compiler_params=_cparams(("arbitrary",)),
        name="experts",
    )(block_expert, block_nreal, block_row0, block_next, xs, wg, wu, wd)


def _combine_gather(ys, pos):
    t = pos.shape[1]
    dp = ys.shape[1]
    ncores, nw = _sc_workers()
    per_w = t // nw
    n_ch = per_w // SC_CHUNK
    assert per_w % SC_CHUNK == 0
    idx = [pos[k].reshape(nw, n_ch, SC_CHUNK) for k in range(TOP_K)]
    mesh = plsc.VectorSubcoreMesh(core_axis_name="c", subcore_axis_name="s")

    @functools.partial(
        pl.kernel, mesh=mesh,
        out_type=jax.ShapeDtypeStruct((TOP_K, t, dp), ys.dtype),
        scratch_types=[pltpu.VMEM((TOP_K, n_ch, SC_CHUNK), jnp.int32),
                       pltpu.VMEM((SC_BUFFERS, SC_CHUNK, dp), ys.dtype),
                       pltpu.SemaphoreType.DMA((SC_BUFFERS,)), pltpu.SemaphoreType.DMA((SC_BUFFERS,))],
    )
    def gather(ys_hbm, p0_hbm, p1_hbm, out_hbm, idx_v, rows_v, gsem, wsem):
        wid = lax.axis_index("s") * ncores + lax.axis_index("c")
        base = wid * per_w
        pltpu.sync_copy(p0_hbm.at[wid], idx_v.at[0])
        pltpu.sync_copy(p1_hbm.at[wid], idx_v.at[1])
        units = [(c, k) for c in range(n_ch) for k in range(TOP_K)]

        def fetch(u):
            c, k = units[u]
            return pltpu.async_copy(ys_hbm.at[idx_v.at[k, c]], rows_v.at[u % SC_BUFFERS], gsem.at[u % SC_BUFFERS])

        fetches = {u: fetch(u) for u in range(min(SC_BUFFERS - 1, len(units)))}
        writes = {}
        for u, (c, k) in enumerate(units):
            fetches[u].wait()
            writes[u] = pltpu.async_copy(rows_v.at[u % SC_BUFFERS],
                                         out_hbm.at[k, pl.ds(base + c * SC_CHUNK, SC_CHUNK)],
                                         wsem.at[u % SC_BUFFERS])
            if u >= 1:
                writes[u - 1].wait()
            if u + SC_BUFFERS - 1 < len(units):
                fetches[u + SC_BUFFERS - 1] = fetch(u + SC_BUFFERS - 1)
        writes[len(units) - 1].wait()

    return gather(ys, idx[0], idx[1])


def _final_kernel(h_ref, y_ref, route_ref, gain_ref, out_ref):
    r = route_ref[...]
    y = h_ref[...]
    for k, gate_lane in enumerate((ROUTE_G1, ROUTE_G2)):
        y = y + r[:, gate_lane:gate_lane + 1] * _unpack_bf16_pairs(y_ref[k])
    out_ref[...] = y * lax.rsqrt(jnp.mean(y * y, axis=-1, keepdims=True) + NORM_EPS) * gain_ref[...]


def _final(h, y_rows, route, gain, tm):
    t, d = h.shape
    return pl.pallas_call(
        _final_kernel,
        grid=(t // tm,),
        in_specs=[pl.BlockSpec((tm, d), lambda i: (i, 0)),
                  pl.BlockSpec((TOP_K, tm, d // 2), lambda i: (0, i, 0)),
                  pl.BlockSpec((tm, LANES), lambda i: (i, 0)),
                  pl.BlockSpec((1, d), lambda i: (0, 0))],
        out_specs=pl.BlockSpec((tm, d), lambda i: (i, 0)),
        out_shape=jax.ShapeDtypeStruct((t, d), F32),
        compiler_params=_cparams(("arbitrary",)),
        name="final",
    )(h, y_rows, route, gain.reshape(1, d))


def _layer(h3, mix_gain, w_in, attn_gain, decay_f, decay_b, ret_gain, w_out, ffn_gain,
           w_rg, b_rg, w_re, b_re, w_eg, w_eu, w_ed, final_gain):
    b, s, d = h3.shape
    t = b * s
    tm = 512
    attn_qkv, (qr, kr, vr, gr) = _inproj(h3, mix_gain, w_in.astype(BF16), 1024)

    o_list, lse_list = [], []
    for window, dilation in DILATED_BRANCHES:
        reach = (window // 2) // dilation
        o, lse = _attn_branch(*attn_qkv[dilation], dilation, reach, tq=128, tqb=1024, sub_tiles=8)
        o_list.append(o)
        lse_list.append(lse)

    lg_f = jnp.log1p(-jnp.exp2(decay_f.astype(F32)))
    lg_b = jnp.log1p(-jnp.exp2(decay_b.astype(F32)))
    orr = _retention(qr, kr, vr, gr, lg_f, lg_b, ret_gain).reshape(t, HEAD_GROUP_WIDTH)

    n_route = MOE_GROUPS + N_EXPERTS
    w_router = jnp.zeros((d, LANES), F32).at[:, :n_route].set(jnp.concatenate([w_re, w_rg], axis=1).astype(F32))
    b_router = jnp.zeros((1, LANES), F32).at[0, :n_route].set(jnp.concatenate([b_re, b_rg]).astype(F32))
    h, hn_packed, logits = _outproj(h3.reshape(t, d), o_list, lse_list, orr, attn_gain, w_out.astype(BF16),
                                    ffn_gain, w_router, b_router, 2 * tm)

    route, route_t, counts_rep = _route(logits, 2 * tm)

    n_blocks = -(-TOP_K * t // MOE_BLOCK) + N_EXPERTS
    counts = counts_rep[:, 0].astype(jnp.int32)
    aligned = ((counts + SEG_ALIGN - 1) // SEG_ALIGN) * SEG_ALIGN
    seg_start = jnp.cumsum(aligned) - aligned
    fields = jnp.transpose(route_t, (1, 0, 2)).reshape(SUBLANES, t)
    e12 = fields[ROUTE_E1:ROUTE_E2 + 1].astype(jnp.int32)
    r12 = fields[ROUTE_R1:ROUTE_R2 + 1].astype(jnp.int32)
    pos = r12 + jnp.sum(jnp.where(e12[None] == jnp.arange(N_EXPERTS, dtype=jnp.int32)[:, None, None],
                                  seg_start[:, None, None], 0), axis=0)
    nblk = (counts + MOE_BLOCK - 1) // MOE_BLOCK
    blk_end = jnp.cumsum(nblk)
    blk_start = blk_end - nblk
    blk = jnp.arange(n_blocks, dtype=jnp.int32)[:, None]
    owner = (blk >= blk_start) & (blk < blk_end)
    local = (blk - blk_start) * MOE_BLOCK
    block_row0 = jnp.sum(jnp.where(owner, seg_start + local, 0), axis=-1).astype(jnp.int32)
    block_nreal = jnp.sum(jnp.where(owner, jnp.clip(counts - local, 0, MOE_BLOCK), 0), axis=-1).astype(jnp.int32)
    block_expert = jnp.minimum(jnp.sum((blk >= blk_end).astype(jnp.int32), axis=-1), N_EXPERTS - 1)
    e_iota = jnp.arange(N_EXPERTS, dtype=jnp.int32)
    later = (e_iota[None, :] > e_iota[:, None]) & (counts[None, :] > 0)
    next_expert = jnp.where(jnp.any(later, axis=1), jnp.min(jnp.where(later, e_iota[None, :], N_EXPERTS), axis=1), -1)
    block_next = jnp.sum(jnp.where(owner, next_expert, 0), axis=-1).astype(jnp.int32)

    n_sorted = TOP_K * t + SORTED_TAIL
    dump_row = n_sorted
    seg_end = seg_start + counts
    hole = seg_end[:, None] + jnp.arange(SEG_ALIGN, dtype=jnp.int32)[None, :]
    hole = jnp.where(hole < (seg_start + aligned)[:, None], hole, dump_row).reshape(-1)
    tail = (seg_start[-1] + aligned[-1]) + jnp.arange(SORTED_TAIL, dtype=jnp.int32)
    tail = jnp.where(tail < n_sorted, tail, dump_row)
    zero_pos = jnp.concatenate([hole, tail]).astype(jnp.int32)

    xs = _dispatch(hn_packed, pos, zero_pos)
    ys = _experts(xs, block_expert, block_nreal, block_row0, block_next, w_eg.astype(F32), w_eu.astype(F32),
                  w_ed.astype(F32))
    out = _final(h, _combine_gather(ys, pos), route, final_gain, tm)
    return out.reshape(b, s, d)


def kernel(x, mix_norm_gain, w_in, attn_out_gain, ret_decay_fwd, ret_decay_bwd, ret_out_gain, w_out,
           ffn_norm_gain, w_route_group, b_route_group, w_route_expert, b_route_expert,
           w_expert_gate, w_expert_up, w_expert_down, final_norm_gain):
    depth = mix_norm_gain.shape[0]
    assert depth == 1, "the final rmsnorm is fused into the single layer's combine kernel"
    l = 0
    return _layer(x, mix_norm_gain[l], w_in[l], attn_out_gain[l], ret_decay_fwd[l], ret_decay_bwd[l],
                  ret_out_gain[l], w_out[l], ffn_norm_gain[l], w_route_group[l], b_route_group[l],
                  w_route_expert[l], b_route_expert[l], w_expert_gate[l], w_expert_up[l], w_expert_down[l],
                  final_norm_gain)
```

```python
import functools

import numpy as np
import jax
import jax.numpy as jnp
from jax import lax
from jax.experimental import pallas as pl
from jax.experimental.pallas import tpu as pltpu
from jax.experimental.pallas import tpu_sc as plsc

F32 = jnp.float32
BF16 = jnp.bfloat16

ATTN_HEADS = 8
HEAD_DIM = 64
RET_HEADS = 8
HEAD_GROUP_WIDTH = 512
N_PROJ_GROUPS = 7
DILATED_BRANCHES = ((128, 1), (512, 4), (2048, 16))
ROPE_THETA = 500000.0
ROPE_DIM = HEAD_DIM // 4
RET_THETA = 10000.0
RET_CHUNK = 128
MOE_GROUPS = 4
EXPERTS_PER_GROUP = 8
N_EXPERTS = MOE_GROUPS * EXPERTS_PER_GROUP
MOE_BLOCK = 512
NORM_EPS = 1e-6
NEG_INF = -1e30

TOKEN_TILE = 1024
ATTN_SUB_TILE = 128

LANES = 128
VMEM_LIMIT = 56 * 1024 * 1024


def _cparams(sem):
    return pltpu.CompilerParams(dimension_semantics=sem, vmem_limit_bytes=VMEM_LIMIT)


def _rotary_tables(seq, half, freqs):
    pos = np.arange(seq, dtype=np.float64)[:, None]
    ang = pos * freqs[None, :].astype(np.float64)
    cos, sin = np.cos(ang), np.sin(ang)
    c = np.ones((seq, HEAD_DIM)); sp = np.zeros((seq, HEAD_DIM)); sm = np.zeros((seq, HEAD_DIM))
    c[:, :half] = cos; c[:, half:2 * half] = cos
    sp[:, half:2 * half] = sin
    sm[:, :half] = -sin
    rep = LANES // HEAD_DIM
    return tuple(jnp.asarray(np.tile(t, (1, rep)), F32) for t in (c, sp, sm))


def _rotate(t, c, sp, sm, half):
    outs = []
    for g in range(t.shape[1] // LANES):
        tg = t[:, g * LANES:(g + 1) * LANES]
        outs.append(tg * c + pltpu.roll(tg, half, 1) * sp + pltpu.roll(tg, LANES - half, 1) * sm)
    return jnp.concatenate(outs, axis=1)


CLASS_DILATIONS = tuple(d for _, d in DILATED_BRANCHES if d > 1)
ATTN_Q_SCALE = float(np.log2(np.e)) * HEAD_DIM ** -0.5


def _inproj_kernel(x_ref, gain_ref, w_ref, ca_ref, spa_ref, sma_ref, cr_ref, spr_ref, smr_ref, *rest):
    n_cls = len(CLASS_DILATIONS)
    nat_refs = rest[0:3]
    cls_refs = [rest[3 + 3 * c:6 + 3 * c] for c in range(n_cls)]
    qr_ref, kr_ref, vr_ref, gr_ref = rest[3 + 3 * n_cls:7 + 3 * n_cls]
    stage_ref = rest[7 + 3 * n_cls]
    x = x_ref[0]
    tm = x.shape[0]
    ms = jnp.mean(x * x, axis=-1, keepdims=True)
    xn = (x * lax.rsqrt(ms + NORM_EPS) * gain_ref[...]).astype(BF16)
    gw = HEAD_GROUP_WIDTH

    def proj(c):
        return jnp.dot(xn, w_ref[:, c * gw:(c + 1) * gw], preferred_element_type=F32)

    a_tabs = (ca_ref[...], spa_ref[...], sma_ref[...])
    r_tabs = (cr_ref[...], spr_ref[...], smr_ref[...])
    attn_vals = ((_rotate(proj(0), *a_tabs, ROPE_DIM // 2) * ATTN_Q_SCALE),
                 _rotate(proj(1), *a_tabs, ROPE_DIM // 2),
                 proj(2))
    for j, val in enumerate(attn_vals):
        nat_refs[j][0] = val.astype(BF16)
        for g in range(gw // LANES):
            stage_ref[g] = val[:, g * LANES:(g + 1) * LANES]
        for c, d in enumerate(CLASS_DILATIONS):
            for r in range(d):
                for g in range(gw // LANES):
                    col = r * gw + g * LANES
                    cls_refs[c][j][0, :, col:col + LANES] = (
                        stage_ref[g, pl.ds(r, tm // d, stride=d), :].astype(BF16))
    qr_ref[0] = _rotate(proj(3), *r_tabs, HEAD_DIM // 2).astype(BF16)
    kr_ref[0] = (_rotate(proj(4), *r_tabs, HEAD_DIM // 2) * (HEAD_DIM ** -0.5)).astype(BF16)
    vr_ref[0] = proj(5).astype(BF16)
    g = proj(6)
    gr_ref[0] = (g * jax.nn.sigmoid(g)).astype(BF16)


def _inproj(x, gain, w_in_bf16, tm):
    b, s, d = x.shape
    rope_freqs = ROPE_THETA ** (-np.arange(0, ROPE_DIM, 2, dtype=np.float32) / ROPE_DIM)
    ret_freqs = RET_THETA ** (-np.linspace(0.0, 1.0, HEAD_DIM // 2, dtype=np.float32))
    tabs = _rotary_tables(s, ROPE_DIM // 2, rope_freqs) + _rotary_tables(s, HEAD_DIM // 2, ret_freqs)
    gw = HEAD_GROUP_WIDTH
    tab_spec = pl.BlockSpec((tm, LANES), lambda si, bi: (si, 0))

    def view(dil):
        return (pl.BlockSpec((1, tm // dil, dil * gw), lambda si, bi: (bi, si, 0)),
                jax.ShapeDtypeStruct((b, s // dil, dil * gw), BF16))

    views = [view(1)] * 3 + [view(dil) for dil in CLASS_DILATIONS for _ in range(3)] + [view(1)] * 4
    outs = pl.pallas_call(
        _inproj_kernel,
        grid=(s // tm, b),
        in_specs=[pl.BlockSpec((1, tm, d), lambda si, bi: (bi, si, 0)),
                  pl.BlockSpec((1, d), lambda si, bi: (0, 0)),
                  pl.BlockSpec(w_in_bf16.shape, lambda si, bi: (0, 0))] + [tab_spec] * 6,
        out_specs=[v[0] for v in views],
        out_shape=[v[1] for v in views],
        scratch_shapes=[pltpu.VMEM((gw // LANES, tm, LANES), F32)],
        compiler_params=_cparams(("arbitrary", "arbitrary")),
        name="inproj",
    )(x, gain.reshape(1, d), w_in_bf16, *tabs)
    n_attn = 3 * (1 + len(CLASS_DILATIONS))
    attn_qkv = {dil: outs[3 * c:3 * c + 3] for c, dil in enumerate((1,) + CLASS_DILATIONS)}
    return attn_qkv, outs[n_attn:]


ATTN_WINDOW_CASES = 3


def _attn_kernel(q_ref, k_ref, v_ref, o_ref, lse_ref, bias_ref, *, length, tq, reach):
    tqb = q_ref.shape[1]
    win = tq + 2 * reach
    heads_per_pair = LANES // HEAD_DIM
    qi = pl.program_id(2)
    lane = lax.broadcasted_iota(jnp.int32, (1, LANES), 1)
    lane_t = lax.broadcasted_iota(jnp.int32, (tq, LANES), 1)
    ones = jnp.ones((win, LANES), BF16)

    @pl.when((pl.program_id(0) == 0) & (pl.program_id(1) == 0) & (qi == 0))
    def _():
        diff = (lax.broadcasted_iota(jnp.int32, (heads_per_pair * tq, win), 1)
                - (lax.broadcasted_iota(jnp.int32, (heads_per_pair * tq, win), 0) & (tq - 1)))
        for case in range(ATTN_WINDOW_CASES):
            off = case * reach
            bias_ref[case] = jnp.where((diff >= off - reach) & (diff <= off + reach), 0.0, NEG_INF)

    def sub(t, cls):
        q0 = qi * tqb + t * tq
        ws = jnp.clip(q0 - reach, 0, length - win)
        ws = pl.multiple_of(ws, reach)
        bias = bias_ref[(q0 - ws) // reach]
        rows = pl.ds(t * tq, tq)
        m_tile = jnp.zeros((tq, LANES), F32)
        l_tile = jnp.ones((tq, LANES), F32)
        for g in range(HEAD_GROUP_WIDTH // LANES):
            cols = slice(cls * HEAD_GROUP_WIDTH + g * LANES, cls * HEAD_GROUP_WIDTH + (g + 1) * LANES)
            qg = q_ref[0, rows, cols]
            kw = k_ref[0, pl.ds(ws, win), cols]
            v_ones = jnp.concatenate([v_ref[0, pl.ds(ws, win), cols], ones], axis=1)
            hms = [(lane >= hh * HEAD_DIM) & (lane < (hh + 1) * HEAD_DIM) for hh in range(heads_per_pair)]
            q2 = jnp.concatenate([qg * hm.astype(BF16) for hm in hms], axis=0)
            sc = lax.dot_general(q2, kw, (((1,), (1,)), ((), ())), preferred_element_type=F32) + bias
            m = jnp.max(sc, axis=-1, keepdims=True)
            p = jnp.exp2(sc - m).astype(BF16)
            pv = jnp.dot(p, v_ones, preferred_element_type=F32)
            l = pv[:, LANES:]
            o = pv[:, :LANES] / l
            o_pair = jnp.zeros((tq, LANES), F32)
            for hh in range(heads_per_pair):
                part = slice(hh * tq, (hh + 1) * tq)
                head_lane = lane_t == g * heads_per_pair + hh
                o_pair = jnp.where(hms[hh], o[part], o_pair)
                m_tile = jnp.where(head_lane, m[part], m_tile)
                l_tile = jnp.where(head_lane, l[part], l_tile)
            o_ref[0, rows, cols] = o_pair.astype(BF16)
        lse_ref[0, rows, cls * LANES:(cls + 1) * LANES] = m_tile + jnp.log2(l_tile)

    for cls in range(q_ref.shape[2] // HEAD_GROUP_WIDTH):
        for t in range(tqb // tq):
            sub(t, cls)


def _attn_branch(qc, kc, vc, dilation, reach, tq, tqb, sub_tiles=4):
    b, length, dw = qc.shape
    w = dw // dilation
    tqb = min(tqb, length)
    assert tq % reach == 0 and tq > reach and length % tq == 0 and length >= tq + 2 * reach
    ncls = min(dilation, max(1, sub_tiles // (tqb // tq)))
    q_spec = pl.BlockSpec((1, tqb, ncls * w), lambda bi, r, qi: (bi, qi, r))
    kv_spec = pl.BlockSpec((1, length, ncls * w), lambda bi, r, qi: (bi, 0, r))
    o, lse = pl.pallas_call(
        functools.partial(_attn_kernel, length=length, tq=tq, reach=reach),
        grid=(b, dilation // ncls, length // tqb),
        in_specs=[q_spec, kv_spec, kv_spec],
        out_specs=[q_spec, pl.BlockSpec((1, tqb, ncls * LANES), lambda bi, r, qi: (bi, qi, r))],
        out_shape=[jax.ShapeDtypeStruct((b, length, dilation * w), BF16),
                   jax.ShapeDtypeStruct((b, length, dilation * LANES), F32)],
        scratch_shapes=[pltpu.VMEM((ATTN_WINDOW_CASES, (LANES // HEAD_DIM) * tq, tq + 2 * reach), F32)],
        compiler_params=_cparams(("arbitrary", "arbitrary", "arbitrary")),
        name=f"attn_d{dilation}",
    )(qc, kc, vc)
    return o, lse


RET_TAB_QF, RET_TAB_QB, RET_TAB_KF, RET_TAB_KB = range(4)


def _retention_kernel(lgf_ref, lgb_ref, q_ref, k_ref, v_ref, g_ref, gain_ref, o_ref,
                      tab_ref, dec_ref, sb_ref, st_ref, *, chunk, unroll):
    c = chunk
    n = q_ref.shape[1] // c
    width = q_ref.shape[2]
    n_pairs = width // LANES
    heads_per_pair = LANES // HEAD_DIM
    n_heads = n_pairs * heads_per_pair
    head0 = pl.program_id(1) * n_heads
    lane_w = lax.broadcasted_iota(jnp.int32, (1, width), 1)
    lgf = [lgf_ref[head0 + hd] for hd in range(n_heads)]
    lgb = [lgb_ref[head0 + hd] for hd in range(n_heads)]
    lgf_lane = jnp.zeros((1, width), F32)
    lgb_lane = jnp.zeros((1, width), F32)
    for hd in range(n_heads):
        in_head = (lane_w >= hd * HEAD_DIM) & (lane_w < (hd + 1) * HEAD_DIM)
        lgf_lane = jnp.where(in_head, lgf[hd], lgf_lane)
        lgb_lane = jnp.where(in_head, lgb[hd], lgb_lane)
    idx = lax.broadcasted_iota(jnp.int32, (c, width), 0).astype(F32)
    tab_ref[RET_TAB_QF] = jnp.exp((idx + 1.0) * lgf_lane)
    tab_ref[RET_TAB_QB] = jnp.exp((c - idx) * lgb_lane)
    tab_ref[RET_TAB_KF] = jnp.exp((c - 1.0 - idx) * lgf_lane)
    tab_ref[RET_TAB_KB] = jnp.exp(idx * lgb_lane)
    sdf = jnp.exp(c * lgf_lane)
    sdb = jnp.exp(c * lgb_lane)
    dmat = (lax.broadcasted_iota(jnp.int32, (c, c), 0)
            - lax.broadcasted_iota(jnp.int32, (c, c), 1)).astype(F32)
    for hd in range(n_heads):
        dec_ref[hd // heads_per_pair, :, (hd % heads_per_pair) * c:(hd % heads_per_pair + 1) * c] = (
            jnp.where(dmat >= 0, jnp.exp(dmat * lgf[hd]), jnp.exp(-dmat * lgb[hd])))
    lane = lax.broadcasted_iota(jnp.int32, (1, LANES), 1)
    lane_hi = lane >= HEAD_DIM
    head_masks = [((lane >= hh * HEAD_DIM) & (lane < (hh + 1) * HEAD_DIM)).astype(BF16)
                  for hh in range(heads_per_pair)]
    row_hi = lax.broadcasted_iota(jnp.int32, (LANES, LANES), 0) >= HEAD_DIM
    col_hi = lax.broadcasted_iota(jnp.int32, (LANES, LANES), 1) >= HEAD_DIM
    blockdiag = row_hi == col_hi

    def kv_state(kd, vv):
        kv = lax.dot_general(kd.astype(BF16), vv, (((0,), (0,)), ((), ())), preferred_element_type=F32)
        return jnp.where(blockdiag, kv, 0.0)

    st_ref[...] = jnp.zeros_like(st_ref)
    sb_ref[n - 1] = jnp.zeros(sb_ref.shape[1:], sb_ref.dtype)

    def back(i, carry):
        nn = n - 1 - i
        rows = pl.ds(pl.multiple_of(nn * c, c), c)
        for p in range(n_pairs):
            cols = slice(p * LANES, (p + 1) * LANES)
            kd = k_ref[0, rows, cols].astype(F32) * tab_ref[RET_TAB_KB, :, cols]
            new = st_ref[p] * sdb[:, cols] + kv_state(kd, v_ref[0, rows, cols])
            st_ref[p] = new
            sb_ref[nn - 1, p] = new.astype(BF16)
        return carry

    lax.fori_loop(0, n - 1, back, 0, unroll=unroll)

    st_ref[...] = jnp.zeros_like(st_ref)

    def fwd(nn, carry):
        rows = pl.ds(pl.multiple_of(nn * c, c), c)
        for p in range(n_pairs):
            cols = slice(p * LANES, (p + 1) * LANES)
            qq = q_ref[0, rows, cols]
            kk = k_ref[0, rows, cols]
            vv = v_ref[0, rows, cols]
            qf = qq.astype(F32)
            sf = st_ref[p]
            k2 = jnp.concatenate([kk * hm for hm in head_masks], axis=0)
            v2 = jnp.concatenate([vv * hm for hm in head_masks], axis=0)
            sc = lax.dot_general(qq, k2, (((1,), (1,)), ((), ())), preferred_element_type=F32)
            lhs = jnp.concatenate([(qf * tab_ref[RET_TAB_QF, :, cols]).astype(BF16),
                                   (qf * tab_ref[RET_TAB_QB, :, cols]).astype(BF16),
                                   (sc * dec_ref[p]).astype(BF16)], axis=1)
            rhs = jnp.concatenate([sf.astype(BF16), sb_ref[nn, p], v2], axis=0)
            o = jnp.dot(lhs, rhs, preferred_element_type=F32)
            s_lo = jnp.sum(jnp.where(lane_hi, 0.0, o), axis=-1, keepdims=True)
            s_hi = jnp.sum(jnp.where(lane_hi, o, 0.0), axis=-1, keepdims=True)
            mu = jnp.where(lane_hi, s_hi, s_lo) * (1.0 / HEAD_DIM)
            dev = o - mu
            d2 = dev * dev
            v_lo = jnp.sum(jnp.where(lane_hi, 0.0, d2), axis=-1, keepdims=True)
            v_hi = jnp.sum(jnp.where(lane_hi, d2, 0.0), axis=-1, keepdims=True)
            var = jnp.where(lane_hi, v_hi, v_lo) * (1.0 / HEAD_DIM)
            out = dev * lax.rsqrt(var + NORM_EPS) * gain_ref[:, cols] * g_ref[0, rows, cols].astype(F32)
            o_ref[0, rows, cols] = out.astype(BF16)
            kd = kk.astype(F32) * tab_ref[RET_TAB_KF, :, cols]
            st_ref[p] = sf * sdf[:, cols] + kv_state(kd, vv)
        return carry

    lax.fori_loop(0, n, fwd, 0, unroll=unroll)


def _retention(qr, kr, vr, gate, lg_f, lg_b, out_gain, width=512, unroll=8):
    b, s, w = qr.shape
    n_pairs = width // LANES
    n_heads = width // HEAD_DIM
    spec = pl.BlockSpec((1, s, width), lambda bi, p, *_: (bi, 0, p))
    grid_spec = pltpu.PrefetchScalarGridSpec(
        num_scalar_prefetch=2,
        grid=(b, w // width),
        in_specs=[spec, spec, spec, spec, pl.BlockSpec((1, width), lambda bi, p, *_: (0, p))],
        out_specs=spec,
        scratch_shapes=[pltpu.VMEM((4, RET_CHUNK, width), F32),
                        pltpu.VMEM((n_pairs, RET_CHUNK, (LANES // HEAD_DIM) * RET_CHUNK), F32),
                        pltpu.VMEM((s // RET_CHUNK, n_pairs, LANES, LANES), BF16),
                        pltpu.VMEM((n_pairs, LANES, LANES), F32)],
    )
    return pl.pallas_call(
        functools.partial(_retention_kernel, chunk=RET_CHUNK, unroll=unroll),
        grid_spec=grid_spec,
        out_shape=jax.ShapeDtypeStruct((b, s, w), BF16),
        compiler_params=_cparams(("arbitrary", "arbitrary")),
        name="retention",
    )(lg_f, lg_b, qr, kr, vr, gate, out_gain.reshape(1, w))


def _split_bf16(t):
    hi = t.astype(BF16)
    lo = (t - hi.astype(F32)).astype(BF16)
    return hi, lo


def _pack_bf16_pairs(t):
    n = t.shape[1] // 2
    hi = pltpu.bitcast(t[:, :n].astype(BF16).astype(F32), jnp.uint32)
    lo = pltpu.bitcast(t[:, n:].astype(BF16).astype(F32), jnp.uint32)
    return hi | (lo >> 16)


def _unpack_bf16_pairs(u):
    hi = pltpu.bitcast(u & jnp.uint32(0xFFFF0000), F32)
    lo = pltpu.bitcast(u << 16, F32)
    return jnp.concatenate([hi, lo], axis=1)


def _outproj_kernel(x_ref, o1_ref, o2_ref, o3_ref, l1_ref, l2_ref, l3_ref, orr_ref, ga_ref, expand_ref,
                    wout_ref, gf_ref, wr_ref, br_ref, h_ref, hn_ref, logit_ref,
                    *nat_refs):
    tm = x_ref.shape[0]
    gw = HEAD_GROUP_WIDTH
    os, ls = [], []
    for (_, dil), o_ref, l_ref in zip(DILATED_BRANCHES, (o1_ref, o2_ref, o3_ref), (l1_ref, l2_ref, l3_ref)):
        if dil == 1:
            os.append(o_ref[...].astype(F32))
            ls.append(l_ref[...])
            continue
        c = CLASS_DILATIONS.index(dil)
        onat_ref, lnat_ref = nat_refs[2 * c], nat_refs[2 * c + 1]
        for r in range(dil):
            rows = pl.ds(r, tm // dil, stride=dil)
            for g in range(gw // LANES):
                col = r * gw + g * LANES
                onat_ref[g, rows, :] = o_ref[:, col:col + LANES].astype(F32)
            lnat_ref[rows, :] = l_ref[:, r * LANES:(r + 1) * LANES]
        os.append(jnp.concatenate([onat_ref[g] for g in range(gw // LANES)], axis=1))
        ls.append(lnat_ref[...])
    mx = jnp.maximum(jnp.maximum(ls[0], ls[1]), ls[2])
    es = [jnp.exp2(l - mx) for l in ls]
    inv = 1.0 / (es[0] + es[1] + es[2])
    expand = expand_ref[...]
    oa = jnp.zeros((tm, gw), F32)
    for e, o in zip(es, os):
        wexp = jnp.dot(jnp.concatenate(_split_bf16(e * inv), axis=1), expand, preferred_element_type=F32)
        oa = oa + wexp * o
    oa = oa * lax.rsqrt(jnp.mean(oa * oa, axis=-1, keepdims=True) + NORM_EPS) * ga_ref[...]
    mixed = jnp.concatenate([oa.astype(BF16), orr_ref[...]], axis=1)
    h = x_ref[...] + jnp.dot(mixed, wout_ref[...], preferred_element_type=F32)
    h_ref[...] = h
    hn = h * lax.rsqrt(jnp.mean(h * h, axis=-1, keepdims=True) + NORM_EPS) * gf_ref[...]
    hn_ref[...] = _pack_bf16_pairs(hn)
    prod = jnp.dot(jnp.concatenate(_split_bf16(hn), axis=0), wr_ref[...], preferred_element_type=F32)
    logit_ref[...] = prod[:tm, :LANES] + prod[:tm, LANES:] + prod[tm:, :LANES] + br_ref[...]


def _outproj(x2, o_list, lse_list, orr, attn_gain, w_out_bf16, ffn_gain, w_router, b_router, tm):
    t, d = x2.shape
    w = HEAD_GROUP_WIDTH
    expand = np.zeros((LANES, w), np.float32)
    for hd in range(ATTN_HEADS):
        expand[hd, hd * HEAD_DIM:(hd + 1) * HEAD_DIM] = 1.0
    expand = jnp.asarray(np.concatenate([expand, expand], axis=0), BF16)
    wr_hi = w_router.astype(BF16)
    wr = jnp.concatenate([wr_hi, (w_router - wr_hi.astype(F32)).astype(BF16)], axis=1)
    row = lambda width, dil=1: pl.BlockSpec((tm // dil, dil * width), lambda i: (i, 0))
    full = lambda a: pl.BlockSpec(a.shape, lambda i: (0,) * a.ndim)
    ga = attn_gain.reshape(1, w)
    gf = ffn_gain.reshape(1, d)
    dils = [dil for _, dil in DILATED_BRANCHES]
    o_flat = [o.reshape(t // dil, dil * w) for o, dil in zip(o_list, dils)]
    l_flat = [l.reshape(t // dil, dil * LANES) for l, dil in zip(lse_list, dils)]
    nat_scratch = []
    for _ in CLASS_DILATIONS:
        nat_scratch += [pltpu.VMEM((w // LANES, tm, LANES), F32), pltpu.VMEM((tm, LANES), F32)]
    return pl.pallas_call(
        _outproj_kernel,
        grid=(t // tm,),
        in_specs=[row(d)] + [row(w, dil) for dil in dils] + [row(LANES, dil) for dil in dils] + [row(w)]
                 + [full(ga), full(expand), full(w_out_bf16), full(gf), full(wr), full(b_router)],
        out_specs=[row(d), row(d // 2), row(LANES)],
        out_shape=[jax.ShapeDtypeStruct((t, d), F32),
                   jax.ShapeDtypeStruct((t, d // 2), jnp.uint32),
                   jax.ShapeDtypeStruct((t, LANES), F32)],
        scratch_shapes=nat_scratch,
        compiler_params=_cparams(("arbitrary",)),
        name="outproj",
    )(x2, *o_flat, *l_flat, orr, ga, expand, w_out_bf16, gf, wr, b_router)


ROUTE_E1, ROUTE_E2, ROUTE_G1, ROUTE_G2, ROUTE_R1, ROUTE_R2 = range(6)
GROUP_LANE0 = N_EXPERTS
SUBLANES = 8


def _route_kernel(logit_ref, tri_ref, route_ref, route_t_ref, count_ref, run_ref):
    @pl.when(pl.program_id(0) == 0)
    def _():
        run_ref[...] = jnp.zeros_like(run_ref)

    lg_t = jnp.transpose(logit_ref[...])
    tm = lg_t.shape[1]
    assert EXPERTS_PER_GROUP == SUBLANES and MOE_GROUPS <= SUBLANES
    rid = lax.broadcasted_iota(jnp.int32, (SUBLANES, tm), 0)
    big = jnp.int32(1 << 20)

    def top(vals):
        m = jnp.max(vals, axis=0, keepdims=True)
        i = jnp.min(jnp.where(vals == m, rid, big), axis=0, keepdims=True)
        return m, i

    gl = jnp.where(rid < MOE_GROUPS, lg_t[GROUP_LANE0:GROUP_LANE0 + SUBLANES], -jnp.inf)
    gmax, gidx = top(gl)
    group_gate = 1.0 / jnp.sum(jnp.exp(gl - gmax), axis=0, keepdims=True)
    el = lg_t[0:EXPERTS_PER_GROUP]
    for g in range(1, MOE_GROUPS):
        el = jnp.where(gidx == g, lg_t[g * EXPERTS_PER_GROUP:(g + 1) * EXPERTS_PER_GROUP], el)
    t1, i1 = top(el)
    t2, i2 = top(jnp.where(rid == i1, -jnp.inf, el))
    e21 = jnp.exp(t2 - t1)
    g1 = group_gate / (1.0 + e21)
    g2 = group_gate * e21 / (1.0 + e21)
    e1 = gidx * EXPERTS_PER_GROUP + i1
    e2 = gidx * EXPERTS_PER_GROUP + i2
    erow = lax.broadcasted_iota(jnp.int32, (N_EXPERTS, tm), 0)
    oh1 = erow == e1
    oh2 = erow == e2
    cnt = oh1.astype(F32) + oh2.astype(F32)
    run = run_ref[:, 0:1]
    prefix = jnp.dot(cnt.astype(BF16), tri_ref[...], preferred_element_type=F32) + run
    r1 = jnp.sum(jnp.where(oh1, prefix, 0.0), axis=0, keepdims=True)
    r2 = jnp.sum(jnp.where(oh2, prefix, 0.0), axis=0, keepdims=True)
    new_run = jnp.broadcast_to(run + jnp.sum(cnt, axis=1, keepdims=True), run_ref.shape)
    run_ref[...] = new_run
    count_ref[...] = new_run
    out = jnp.zeros((SUBLANES, tm), F32)
    for row, val in ((ROUTE_E1, e1.astype(F32)), (ROUTE_E2, e2.astype(F32)), (ROUTE_G1, g1),
                     (ROUTE_G2, g2), (ROUTE_R1, r1), (ROUTE_R2, r2)):
        out = jnp.where(rid == row, val, out)
    route_t_ref[0] = out
    out_t = jnp.concatenate([out, jnp.zeros((LANES - SUBLANES, tm), F32)], axis=0)
    route_ref[...] = jnp.transpose(out_t)


def _route(logits, tm):
    t = logits.shape[0]
    tri = jnp.asarray(np.triu(np.ones((tm, tm), np.float32), 1), BF16)
    return pl.pallas_call(
        _route_kernel,
        grid=(t // tm,),
        in_specs=[pl.BlockSpec((tm, LANES), lambda i: (i, 0)), pl.BlockSpec((tm, tm), lambda i: (0, 0))],
        out_specs=[pl.BlockSpec((tm, LANES), lambda i: (i, 0)),
                   pl.BlockSpec((1, SUBLANES, tm), lambda i: (i, 0, 0)),
                   pl.BlockSpec((N_EXPERTS, LANES), lambda i: (0, 0))],
        out_shape=[jax.ShapeDtypeStruct((t, LANES), F32),
                   jax.ShapeDtypeStruct((t // tm, SUBLANES, tm), F32),
                   jax.ShapeDtypeStruct((N_EXPERTS, LANES), F32)],
        scratch_shapes=[pltpu.VMEM((N_EXPERTS, LANES), F32)],
        compiler_params=_cparams(("arbitrary",)),
        name="route",
    )(logits, tri)


TOP_K = 2


SEG_ALIGN = 8
SORTED_TAIL = N_EXPERTS * SEG_ALIGN + MOE_BLOCK


def _tail_pieces():
    full, rest = divmod(SORTED_TAIL, MOE_BLOCK)
    return [MOE_BLOCK] * full + ([rest] if rest else [])


SC_CHUNK = 64
SC_BUFFERS = 3
ZERO_ROWS = N_EXPERTS * SEG_ALIGN + SORTED_TAIL
SORTED_ROWS_EXTRA = SORTED_TAIL + SEG_ALIGN


def _sc_workers():
    info = plsc.get_sparse_core_info()
    return info.num_cores, info.num_cores * info.num_subcores


def _dispatch(hn_packed, pos, zero_pos):
    t, dp = hn_packed.shape
    ncores, nw = _sc_workers()
    per_w = t // nw
    n_ch = per_w // SC_CHUNK
    z_rows = ZERO_ROWS // nw
    assert per_w % SC_CHUNK == 0 and ZERO_ROWS % nw == 0 and z_rows % SEG_ALIGN == 0 and TOP_K == 2
    idx = [pos[k].reshape(nw, n_ch, SC_CHUNK) for k in range(TOP_K)]
    zeros = jnp.zeros((z_rows, dp), hn_packed.dtype)
    mesh = plsc.VectorSubcoreMesh(core_axis_name="c", subcore_axis_name="s")

    @functools.partial(
        pl.kernel, mesh=mesh,
        out_type=jax.ShapeDtypeStruct((TOP_K * t + SORTED_ROWS_EXTRA, dp), hn_packed.dtype),
        scratch_types=[pltpu.VMEM((n_ch, SC_CHUNK), jnp.int32), pltpu.VMEM((n_ch, SC_CHUNK), jnp.int32),
                       pltpu.VMEM((z_rows,), jnp.int32),
                       pltpu.VMEM((SC_BUFFERS, SC_CHUNK, dp), hn_packed.dtype),
                       pltpu.VMEM((z_rows, dp), hn_packed.dtype),
                       pltpu.SemaphoreType.DMA((SC_BUFFERS,)), pltpu.SemaphoreType.DMA((SC_BUFFERS,)),
                       pltpu.SemaphoreType.DMA],
    )
    def scatter(hn_hbm, p0_hbm, p1_hbm, zpos_hbm, zeros_hbm, xs_hbm, i0_v, i1_v, iz_v, rows_v, zero_v,
                lsem, ssem, zsem):
        wid = lax.axis_index("s") * ncores + lax.axis_index("c")
        base = wid * per_w
        zero_load = pltpu.async_copy(zeros_hbm, zero_v, zsem)
        pltpu.sync_copy(zpos_hbm.at[wid], iz_v)
        pltpu.sync_copy(p0_hbm.at[wid], i0_v)
        pltpu.sync_copy(p1_hbm.at[wid], i1_v)

        def load(c):
            return pltpu.async_copy(hn_hbm.at[pl.ds(base + c * SC_CHUNK, SC_CHUNK)], rows_v.at[c % SC_BUFFERS],
                                    lsem.at[c % SC_BUFFERS])

        loads = {c: load(c) for c in range(min(SC_BUFFERS - 1, n_ch))}
        scat = {}
        for c in range(n_ch):
            slot = c % SC_BUFFERS
            loads[c].wait()
            scat[c] = (pltpu.async_copy(rows_v.at[slot], xs_hbm.at[i0_v.at[c]], ssem.at[slot]),
                       pltpu.async_copy(rows_v.at[slot], xs_hbm.at[i1_v.at[c]], ssem.at[slot]))
            if c >= 1:
                for d in scat[c - 1]:
                    d.wait()
            if c + SC_BUFFERS - 1 < n_ch:
                loads[c + SC_BUFFERS - 1] = load(c + SC_BUFFERS - 1)
        zero_load.wait()
        zero_scatter = pltpu.async_copy(zero_v, xs_hbm.at[iz_v], zsem)
        for d in scat[n_ch - 1]:
            d.wait()
        zero_scatter.wait()

    return scatter(hn_packed, idx[0], idx[1], zero_pos.reshape(nw, z_rows), zeros)


def _expert_kernel(bexp_ref, nreal_ref, row0_ref, bnext_ref, xs_hbm, wg_hbm, wu_hbm, wd_hbm, ys_hbm,
                   xbuf, ybuf, wg_f32, wu_f32, wd_f32, wg_bf, wu_bf, wd_bf, isem, osem, wsem, *, n_rows):
    i = pl.program_id(0)
    nb = pl.num_programs(0)
    slot = i % 2
    nslot = 1 - slot
    n_cur = nreal_ref[i]
    prev = jnp.maximum(i - 1, 0)
    nxt = jnp.minimum(i + 1, nb - 1)

    def in_copy(blk, s):
        row0 = pl.multiple_of(row0_ref[blk], SEG_ALIGN)
        return pltpu.make_async_copy(xs_hbm.at[pl.ds(row0, MOE_BLOCK)], xbuf.at[s], isem.at[s])

    def out_copy(blk, s):
        row0 = pl.multiple_of(row0_ref[blk], SEG_ALIGN)
        return pltpu.make_async_copy(ybuf.at[s], ys_hbm.at[pl.ds(row0, MOE_BLOCK)], osem.at[s])

    @pl.when(i == 0)
    def _():
        ybuf[...] = jnp.zeros_like(ybuf)
        tails = [pltpu.make_async_copy(ybuf.at[s, pl.ds(0, size)], ys_hbm.at[pl.ds(n_rows + s * MOE_BLOCK, size)],
                                       osem.at[s])
                 for s, size in enumerate(_tail_pieces())]
        for tail in tails:
            tail.start()
        for tail in tails:
            tail.wait()

        @pl.when(n_cur > 0)
        def _():
            in_copy(i, slot).start()

    @pl.when((i + 1 < nb) & (nreal_ref[nxt] > 0))
    def _():
        in_copy(nxt, nslot).start()

    def weight_copies(e):
        return [pltpu.make_async_copy(w_hbm.at[e], w_f32, wsem.at[j])
                for j, (w_hbm, w_f32) in enumerate(((wg_hbm, wg_f32), (wu_hbm, wu_f32), (wd_hbm, wd_f32)))]

    @pl.when((i == 0) & (n_cur > 0))
    def _():
        for c in weight_copies(bexp_ref[0]):
            c.start()

    @pl.when((n_cur > 0) & ((i == 0) | (bexp_ref[i] != bexp_ref[prev])))
    def _():
        for c in weight_copies(bexp_ref[i]):
            c.wait()
        wg_bf[...] = wg_f32[...].astype(BF16)
        wu_bf[...] = wu_f32[...].astype(BF16)
        wd_bf[...] = wd_f32[...].astype(BF16)

        @pl.when(bnext_ref[i] >= 0)
        def _():
            for c in weight_copies(bnext_ref[i]):
                c.start()

    def swiglu(rows):
        xb = _unpack_bf16_pairs(xbuf[slot, rows]).astype(BF16)
        gate = jnp.dot(xb, wg_bf[...], preferred_element_type=F32)
        up = jnp.dot(xb, wu_bf[...], preferred_element_type=F32)
        hid = (gate * jax.nn.sigmoid(gate) * up).astype(BF16)
        ybuf[slot, rows] = _pack_bf16_pairs(jnp.dot(hid, wd_bf[...], preferred_element_type=F32))

    @pl.when(n_cur > 0)
    def _():
        in_copy(i, slot).wait()

    @pl.when(n_cur > MOE_BLOCK // 2)
    def _():
        swiglu(slice(0, MOE_BLOCK))

    @pl.when((n_cur > 0) & (n_cur <= MOE_BLOCK // 2))
    def _():
        swiglu(slice(0, MOE_BLOCK // 2))

    @pl.when((i >= 1) & (nreal_ref[prev] > 0))
    def _():
        out_copy(prev, nslot).wait()

    @pl.when(n_cur > 0)
    def _():
        out_copy(i, slot).start()

        @pl.when(i == nb - 1)
        def _():
            out_copy(i, slot).wait()


def _experts(xs, block_expert, block_nreal, block_row0, block_next, wg, wu, wd):
    n_blocks = block_expert.shape[0]
    assert n_blocks >= 2 and len(_tail_pieces()) <= 2
    n_rows_pad = xs.shape[0] - SORTED_ROWS_EXTRA + SORTED_TAIL
    dp = xs.shape[1]
    _, d, ff = wg.shape
    any_spec = pl.BlockSpec(memory_space=pl.ANY)
    grid_spec = pltpu.PrefetchScalarGridSpec(
        num_scalar_prefetch=4,
        grid=(n_blocks,),
        in_specs=[any_spec, any_spec, any_spec, any_spec],
        out_specs=any_spec,
        scratch_shapes=[pltpu.VMEM((2, MOE_BLOCK, dp), jnp.uint32),
                        pltpu.VMEM((2, MOE_BLOCK, dp), jnp.uint32),
                        pltpu.VMEM((d, ff), F32),
                        pltpu.VMEM((d, ff), F32),
                        pltpu.VMEM((ff, d), F32),
                        pltpu.VMEM((d, ff), BF16),
                        pltpu.VMEM((d, ff), BF16),
                        pltpu.VMEM((ff, d), BF16),
                        pltpu.SemaphoreType.DMA((2,)),
                        pltpu.SemaphoreType.DMA((2,)),
                        pltpu.SemaphoreType.DMA((3,))],
    )
    return pl.pallas_call(
        functools.partial(_expert_kernel, n_rows=n_rows_pad - SORTED_TAIL),
        grid_spec=grid_spec,
        out_shape=jax.ShapeDtypeStruct((n_rows_pad, dp), jnp.uint32),
        compiler_params=_cparams(("arbitrary",)),
        name="experts",
    )(block_expert, block_nreal, block_row0, block_next, xs, wg, wu, wd)


def _combine_gather(ys, pos):
    t = pos.shape[1]
    dp = ys.shape[1]
    ncores, nw = _sc_workers()
    per_w = t // nw
    n_ch = per_w // SC_CHUNK
    assert per_w % SC_CHUNK == 0
    idx = [pos[k].reshape(nw, n_ch, SC_CHUNK) for k in range(TOP_K)]
    mesh = plsc.VectorSubcoreMesh(core_axis_name="c", subcore_axis_name="s")

    @functools.partial(
        pl.kernel, mesh=mesh,
        out_type=jax.ShapeDtypeStruct((TOP_K, t, dp), ys.dtype),
        scratch_types=[pltpu.VMEM((TOP_K, n_ch, SC_CHUNK), jnp.int32),
                       pltpu.VMEM((SC_BUFFERS, SC_CHUNK, dp), ys.dtype),
                       pltpu.SemaphoreType.DMA((SC_BUFFERS,)), pltpu.SemaphoreType.DMA((SC_BUFFERS,))],
    )
    def gather(ys_hbm, p0_hbm, p1_hbm, out_hbm, idx_v, rows_v, gsem, wsem):
        wid = lax.axis_index("s") * ncores + lax.axis_index("c")
        base = wid * per_w
        pltpu.sync_copy(p0_hbm.at[wid], idx_v.at[0])
        pltpu.sync_copy(p1_hbm.at[wid], idx_v.at[1])
        units = [(c, k) for c in range(n_ch) for k in range(TOP_K)]

        def fetch(u):
            c, k = units[u]
            return pltpu.async_copy(ys_hbm.at[idx_v.at[k, c]], rows_v.at[u % SC_BUFFERS], gsem.at[u % SC_BUFFERS])

        fetches = {u: fetch(u) for u in range(min(SC_BUFFERS - 1, len(units)))}
        writes = {}
        for u, (c, k) in enumerate(units):
            fetches[u].wait()
            writes[u] = pltpu.async_copy(rows_v.at[u % SC_BUFFERS],
                                         out_hbm.at[k, pl.ds(base + c * SC_CHUNK, SC_CHUNK)],
                                         wsem.at[u % SC_BUFFERS])
            if u >= 1:
                writes[u - 1].wait()
            if u + SC_BUFFERS - 1 < len(units):
                fetches[u + SC_BUFFERS - 1] = fetch(u + SC_BUFFERS - 1)
        writes[len(units) - 1].wait()

    return gather(ys, idx[0], idx[1])


def _final_kernel(h_ref, y_ref, route_ref, gain_ref, out_ref):
    r = route_ref[...]
    y = h_ref[...]
    for k, gate_lane in enumerate((ROUTE_G1, ROUTE_G2)):
        y = y + r[:, gate_lane:gate_lane + 1] * _unpack_bf16_pairs(y_ref[k])
    out_ref[...] = y * lax.rsqrt(jnp.mean(y * y, axis=-1, keepdims=True) + NORM_EPS) * gain_ref[...]


def _final(h, y_rows, route, gain, tm):
    t, d = h.shape
    return pl.pallas_call(
        _final_kernel,
        grid=(t // tm,),
        in_specs=[pl.BlockSpec((tm, d), lambda i: (i, 0)),
                  pl.BlockSpec((TOP_K, tm, d // 2), lambda i: (0, i, 0)),
                  pl.BlockSpec((tm, LANES), lambda i: (i, 0)),
                  pl.BlockSpec((1, d), lambda i: (0, 0))],
        out_specs=pl.BlockSpec((tm, d), lambda i: (i, 0)),
        out_shape=jax.ShapeDtypeStruct((t, d), F32),
        compiler_params=_cparams(("arbitrary",)),
        name="final",
    )(h, y_rows, route, gain.reshape(1, d))


def _layer(h3, mix_gain, w_in, attn_gain, decay_f, decay_b, ret_gain, w_out, ffn_gain,
           w_rg, b_rg, w_re, b_re, w_eg, w_eu, w_ed, final_gain):
    b, s, d = h3.shape
    t = b * s
    attn_qkv, (qr, kr, vr, gr) = _inproj(h3, mix_gain, w_in.astype(BF16), TOKEN_TILE)

    o_list, lse_list = [], []
    for window, dilation in DILATED_BRANCHES:
        reach = (window // 2) // dilation
        o, lse = _attn_branch(*attn_qkv[dilation], dilation, reach, tq=ATTN_SUB_TILE, tqb=TOKEN_TILE,
                              sub_tiles=TOKEN_TILE // ATTN_SUB_TILE)
        o_list.append(o)
        lse_list.append(lse)

    lg_f = jnp.log1p(-jnp.exp2(decay_f.astype(F32)))
    lg_b = jnp.log1p(-jnp.exp2(decay_b.astype(F32)))
    orr = _retention(qr, kr, vr, gr, lg_f, lg_b, ret_gain).reshape(t, HEAD_GROUP_WIDTH)

    n_route = MOE_GROUPS + N_EXPERTS
    w_router = jnp.zeros((d, LANES), F32).at[:, :n_route].set(jnp.concatenate([w_re, w_rg], axis=1).astype(F32))
    b_router = jnp.zeros((1, LANES), F32).at[0, :n_route].set(jnp.concatenate([b_re, b_rg]).astype(F32))
    h, hn_packed, logits = _outproj(h3.reshape(t, d), o_list, lse_list, orr, attn_gain, w_out.astype(BF16),
                                    ffn_gain, w_router, b_router, TOKEN_TILE)

    route, route_t, counts_rep = _route(logits, TOKEN_TILE)

    n_blocks = -(-TOP_K * t // MOE_BLOCK) + N_EXPERTS
    counts = counts_rep[:, 0].astype(jnp.int32)
    aligned = ((counts + SEG_ALIGN - 1) // SEG_ALIGN) * SEG_ALIGN
    seg_start = jnp.cumsum(aligned) - aligned
    fields = jnp.transpose(route_t, (1, 0, 2)).reshape(SUBLANES, t)
    e12 = fields[ROUTE_E1:ROUTE_E2 + 1].astype(jnp.int32)
    r12 = fields[ROUTE_R1:ROUTE_R2 + 1].astype(jnp.int32)
    pos = r12 + jnp.sum(jnp.where(e12[None] == jnp.arange(N_EXPERTS, dtype=jnp.int32)[:, None, None],
                                  seg_start[:, None, None], 0), axis=0)
    nblk = (counts + MOE_BLOCK - 1) // MOE_BLOCK
    blk_end = jnp.cumsum(nblk)
    blk_start = blk_end - nblk
    blk = jnp.arange(n_blocks, dtype=jnp.int32)[:, None]
    owner = (blk >= blk_start) & (blk < blk_end)
    local = (blk - blk_start) * MOE_BLOCK
    block_row0 = jnp.sum(jnp.where(owner, seg_start + local, 0), axis=-1).astype(jnp.int32)
    block_nreal = jnp.sum(jnp.where(owner, jnp.clip(counts - local, 0, MOE_BLOCK), 0), axis=-1).astype(jnp.int32)
    block_expert = jnp.minimum(jnp.sum((blk >= blk_end).astype(jnp.int32), axis=-1), N_EXPERTS - 1)
    e_iota = jnp.arange(N_EXPERTS, dtype=jnp.int32)
    later = (e_iota[None, :] > e_iota[:, None]) & (counts[None, :] > 0)
    next_expert = jnp.where(jnp.any(later, axis=1), jnp.min(jnp.where(later, e_iota[None, :], N_EXPERTS), axis=1), -1)
    block_next = jnp.sum(jnp.where(owner, next_expert, 0), axis=-1).astype(jnp.int32)

    n_sorted = TOP_K * t + SORTED_TAIL
    dump_row = n_sorted
    seg_end = seg_start + counts
    hole = seg_end[:, None] + jnp.arange(SEG_ALIGN, dtype=jnp.int32)[None, :]
    hole = jnp.where(hole < (seg_start + aligned)[:, None], hole, dump_row).reshape(-1)
    tail = (seg_start[-1] + aligned[-1]) + jnp.arange(SORTED_TAIL, dtype=jnp.int32)
    tail = jnp.where(tail < n_sorted, tail, dump_row)
    zero_pos = jnp.concatenate([hole, tail]).astype(jnp.int32)

    xs = _dispatch(hn_packed, pos, zero_pos)
    ys = _experts(xs, block_expert, block_nreal, block_row0, block_next, w_eg.astype(F32), w_eu.astype(F32),
                  w_ed.astype(F32))
    out = _final(h, _combine_gather(ys, pos), route, final_gain, TOKEN_TILE)
    return out.reshape(b, s, d)


def kernel(x, mix_norm_gain, w_in, attn_out_gain, ret_decay_fwd, ret_decay_bwd, ret_out_gain, w_out,
           ffn_norm_gain, w_route_group, b_route_group, w_route_expert, b_route_expert,
           w_expert_gate, w_expert_up, w_expert_down, final_norm_gain):
    depth = mix_norm_gain.shape[0]
    assert depth == 1, "the final rmsnorm is fused into the single layer's combine kernel"
    l = 0
    return _layer(x, mix_norm_gain[l], w_in[l], attn_out_gain[l], ret_decay_fwd[l], ret_decay_bwd[l],
                  ret_out_gain[l], w_out[l], ffn_norm_gain[l], w_route_group[l], b_route_group[l],
                  w_route_expert[l], b_route_expert[l], w_expert_gate[l], w_expert_up[l], w_expert_down[l],
                  final_norm_gain)
```

```python
import functools

import numpy as np
import jax
import jax.numpy as jnp
from jax import lax
from jax.experimental import pallas as pl
from jax.experimental.pallas import tpu as pltpu
from jax.experimental.pallas import tpu_sc as plsc

F32 = jnp.float32
BF16 = jnp.bfloat16

ATTN_HEADS = 8
HEAD_DIM = 64
HEAD_GROUP_WIDTH = 512
DILATED_BRANCHES = ((128, 1), (512, 4), (2048, 16))
ROPE_THETA = 500000.0
ROPE_DIM = HEAD_DIM // 4
RET_THETA = 10000.0
RET_CHUNK = 128
MOE_GROUPS = 4
EXPERTS_PER_GROUP = 8
N_EXPERTS = MOE_GROUPS * EXPERTS_PER_GROUP
MOE_BLOCK = 512
NORM_EPS = 1e-6
NEG_INF = -1e30

TOKEN_TILE = 1024
ATTN_SUB_TILE = 128

LANES = 128
VMEM_LIMIT = 56 * 1024 * 1024


def _cparams(sem):
    return pltpu.CompilerParams(dimension_semantics=sem, vmem_limit_bytes=VMEM_LIMIT)


def _rotary_tables(seq, half, freqs):
    pos = np.arange(seq, dtype=np.float64)[:, None]
    ang = pos * freqs[None, :].astype(np.float64)
    cos, sin = np.cos(ang), np.sin(ang)
    c = np.ones((seq, HEAD_DIM)); sp = np.zeros((seq, HEAD_DIM)); sm = np.zeros((seq, HEAD_DIM))
    c[:, :half] = cos; c[:, half:2 * half] = cos
    sp[:, half:2 * half] = sin
    sm[:, :half] = -sin
    rep = LANES // HEAD_DIM
    return tuple(jnp.asarray(np.tile(t, (1, rep)), F32) for t in (c, sp, sm))


def _rotate(t, c, sp, sm, half):
    outs = []
    for g in range(t.shape[1] // LANES):
        tg = t[:, g * LANES:(g + 1) * LANES]
        outs.append(tg * c + pltpu.roll(tg, half, 1) * sp + pltpu.roll(tg, LANES - half, 1) * sm)
    return jnp.concatenate(outs, axis=1)


CLASS_DILATIONS = tuple(d for _, d in DILATED_BRANCHES if d > 1)
ATTN_Q_SCALE = float(np.log2(np.e)) * HEAD_DIM ** -0.5


def _inproj_kernel(x_ref, gain_ref, w_ref, ca_ref, spa_ref, sma_ref, cr_ref, spr_ref, smr_ref, *rest):
    n_cls = len(CLASS_DILATIONS)
    nat_refs = rest[0:3]
    cls_refs = [rest[3 + 3 * c:6 + 3 * c] for c in range(n_cls)]
    qr_ref, kr_ref, vr_ref, gr_ref = rest[3 + 3 * n_cls:7 + 3 * n_cls]
    stage_ref = rest[7 + 3 * n_cls]
    x = x_ref[0]
    tm = x.shape[0]
    ms = jnp.mean(x * x, axis=-1, keepdims=True)
    xn = (x * lax.rsqrt(ms + NORM_EPS) * gain_ref[...]).astype(BF16)
    gw = HEAD_GROUP_WIDTH

    def proj(c):
        return jnp.dot(xn, w_ref[:, c * gw:(c + 1) * gw], preferred_element_type=F32)

    a_tabs = (ca_ref[...], spa_ref[...], sma_ref[...])
    r_tabs = (cr_ref[...], spr_ref[...], smr_ref[...])
    attn_vals = ((_rotate(proj(0), *a_tabs, ROPE_DIM // 2) * ATTN_Q_SCALE),
                 _rotate(proj(1), *a_tabs, ROPE_DIM // 2),
                 proj(2))
    for j, val in enumerate(attn_vals):
        nat_refs[j][0] = val.astype(BF16)
        for g in range(gw // LANES):
            stage_ref[g] = val[:, g * LANES:(g + 1) * LANES]
        for c, d in enumerate(CLASS_DILATIONS):
            for r in range(d):
                for g in range(gw // LANES):
                    col = r * gw + g * LANES
                    cls_refs[c][j][0, :, col:col + LANES] = (
                        stage_ref[g, pl.ds(r, tm // d, stride=d), :].astype(BF16))
    qr_ref[0] = _rotate(proj(3), *r_tabs, HEAD_DIM // 2).astype(BF16)
    kr_ref[0] = (_rotate(proj(4), *r_tabs, HEAD_DIM // 2) * (HEAD_DIM ** -0.5)).astype(BF16)
    vr_ref[0] = proj(5).astype(BF16)
    g = proj(6)
    gr_ref[0] = (g * jax.nn.sigmoid(g)).astype(BF16)


def _inproj(x, gain, w_in_bf16, tm):
    b, s, d = x.shape
    rope_freqs = ROPE_THETA ** (-np.arange(0, ROPE_DIM, 2, dtype=np.float32) / ROPE_DIM)
    ret_freqs = RET_THETA ** (-np.linspace(0.0, 1.0, HEAD_DIM // 2, dtype=np.float32))
    tabs = _rotary_tables(s, ROPE_DIM // 2, rope_freqs) + _rotary_tables(s, HEAD_DIM // 2, ret_freqs)
    gw = HEAD_GROUP_WIDTH
    tab_spec = pl.BlockSpec((tm, LANES), lambda si, bi: (si, 0))

    def view(dil):
        return (pl.BlockSpec((1, tm // dil, dil * gw), lambda si, bi: (bi, si, 0)),
                jax.ShapeDtypeStruct((b, s // dil, dil * gw), BF16))

    views = [view(1)] * 3 + [view(dil) for dil in CLASS_DILATIONS for _ in range(3)] + [view(1)] * 4
    outs = pl.pallas_call(
        _inproj_kernel,
        grid=(s // tm, b),
        in_specs=[pl.BlockSpec((1, tm, d), lambda si, bi: (bi, si, 0)),
                  pl.BlockSpec((1, d), lambda si, bi: (0, 0)),
                  pl.BlockSpec(w_in_bf16.shape, lambda si, bi: (0, 0))] + [tab_spec] * 6,
        out_specs=[v[0] for v in views],
        out_shape=[v[1] for v in views],
        scratch_shapes=[pltpu.VMEM((gw // LANES, tm, LANES), F32)],
        compiler_params=_cparams(("arbitrary", "arbitrary")),
        name="inproj",
    )(x, gain.reshape(1, d), w_in_bf16, *tabs)
    n_attn = 3 * (1 + len(CLASS_DILATIONS))
    attn_qkv = {dil: outs[3 * c:3 * c + 3] for c, dil in enumerate((1,) + CLASS_DILATIONS)}
    return attn_qkv, outs[n_attn:]


ATTN_WINDOW_CASES = 3


def _attn_kernel(q_ref, k_ref, v_ref, o_ref, lse_ref, bias_ref, *, length, tq, reach):
    tqb = q_ref.shape[1]
    win = tq + 2 * reach
    heads_per_pair = LANES // HEAD_DIM
    qi = pl.program_id(2)
    lane = lax.broadcasted_iota(jnp.int32, (1, LANES), 1)
    lane_t = lax.broadcasted_iota(jnp.int32, (tq, LANES), 1)
    ones = jnp.ones((win, LANES), BF16)

    @pl.when((pl.program_id(0) == 0) & (pl.program_id(1) == 0) & (qi == 0))
    def _():
        diff = (lax.broadcasted_iota(jnp.int32, (heads_per_pair * tq, win), 1)
                - (lax.broadcasted_iota(jnp.int32, (heads_per_pair * tq, win), 0) & (tq - 1)))
        for case in range(ATTN_WINDOW_CASES):
            off = case * reach
            bias_ref[case] = jnp.where((diff >= off - reach) & (diff <= off + reach), 0.0, NEG_INF)

    def sub(t, cls):
        q0 = qi * tqb + t * tq
        ws = jnp.clip(q0 - reach, 0, length - win)
        ws = pl.multiple_of(ws, reach)
        bias = bias_ref[(q0 - ws) // reach]
        rows = pl.ds(t * tq, tq)
        m_tile = jnp.zeros((tq, LANES), F32)
        l_tile = jnp.ones((tq, LANES), F32)
        for g in range(HEAD_GROUP_WIDTH // LANES):
            cols = slice(cls * HEAD_GROUP_WIDTH + g * LANES, cls * HEAD_GROUP_WIDTH + (g + 1) * LANES)
            qg = q_ref[0, rows, cols]
            kw = k_ref[0, pl.ds(ws, win), cols]
            v_ones = jnp.concatenate([v_ref[0, pl.ds(ws, win), cols], ones], axis=1)
            hms = [(lane >= hh * HEAD_DIM) & (lane < (hh + 1) * HEAD_DIM) for hh in range(heads_per_pair)]
            q2 = jnp.concatenate([qg * hm.astype(BF16) for hm in hms], axis=0)
            sc = lax.dot_general(q2, kw, (((1,), (1,)), ((), ())), preferred_element_type=F32) + bias
            m = jnp.max(sc, axis=-1, keepdims=True)
            p = jnp.exp2(sc - m).astype(BF16)
            pv = jnp.dot(p, v_ones, preferred_element_type=F32)
            l = pv[:, LANES:]
            o = pv[:, :LANES] / l
            o_pair = jnp.zeros((tq, LANES), F32)
            for hh in range(heads_per_pair):
                part = slice(hh * tq, (hh + 1) * tq)
                head_lane = lane_t == g * heads_per_pair + hh
                o_pair = jnp.where(hms[hh], o[part], o_pair)
                m_tile = jnp.where(head_lane, m[part], m_tile)
                l_tile = jnp.where(head_lane, l[part], l_tile)
            o_ref[0, rows, cols] = o_pair.astype(BF16)
        lse_ref[0, rows, cls * LANES:(cls + 1) * LANES] = m_tile + jnp.log2(l_tile)

    for cls in range(q_ref.shape[2] // HEAD_GROUP_WIDTH):
        for t in range(tqb // tq):
            sub(t, cls)


def _attn_branch(qc, kc, vc, dilation, reach, tq, tqb, sub_tiles=4):
    b, length, dw = qc.shape
    w = dw // dilation
    tqb = min(tqb, length)
    assert tq % reach == 0 and tq > reach and length % tq == 0 and length >= tq + 2 * reach
    ncls = min(dilation, max(1, sub_tiles // (tqb // tq)))
    q_spec = pl.BlockSpec((1, tqb, ncls * w), lambda bi, r, qi: (bi, qi, r))
    kv_spec = pl.BlockSpec((1, length, ncls * w), lambda bi, r, qi: (bi, 0, r))
    o, lse = pl.pallas_call(
        functools.partial(_attn_kernel, length=length, tq=tq, reach=reach),
        grid=(b, dilation // ncls, length // tqb),
        in_specs=[q_spec, kv_spec, kv_spec],
        out_specs=[q_spec, pl.BlockSpec((1, tqb, ncls * LANES), lambda bi, r, qi: (bi, qi, r))],
        out_shape=[jax.ShapeDtypeStruct((b, length, dilation * w), BF16),
                   jax.ShapeDtypeStruct((b, length, dilation * LANES), F32)],
        scratch_shapes=[pltpu.VMEM((ATTN_WINDOW_CASES, (LANES // HEAD_DIM) * tq, tq + 2 * reach), F32)],
        compiler_params=_cparams(("arbitrary", "arbitrary", "arbitrary")),
        name=f"attn_d{dilation}",
    )(qc, kc, vc)
    return o, lse


RET_TAB_QF, RET_TAB_QB, RET_TAB_KF, RET_TAB_KB = range(4)


def _retention_kernel(lgf_ref, lgb_ref, q_ref, k_ref, v_ref, g_ref, gain_ref, o_ref,
                      tab_ref, dec_ref, sb_ref, st_ref, *, chunk, unroll):
    c = chunk
    n = q_ref.shape[1] // c
    width = q_ref.shape[2]
    n_pairs = width // LANES
    heads_per_pair = LANES // HEAD_DIM
    n_heads = n_pairs * heads_per_pair
    head0 = pl.program_id(1) * n_heads
    lane_w = lax.broadcasted_iota(jnp.int32, (1, width), 1)
    lgf = [lgf_ref[head0 + hd] for hd in range(n_heads)]
    lgb = [lgb_ref[head0 + hd] for hd in range(n_heads)]
    lgf_lane = jnp.zeros((1, width), F32)
    lgb_lane = jnp.zeros((1, width), F32)
    for hd in range(n_heads):
        in_head = (lane_w >= hd * HEAD_DIM) & (lane_w < (hd + 1) * HEAD_DIM)
        lgf_lane = jnp.where(in_head, lgf[hd], lgf_lane)
        lgb_lane = jnp.where(in_head, lgb[hd], lgb_lane)
    idx = lax.broadcasted_iota(jnp.int32, (c, width), 0).astype(F32)
    tab_ref[RET_TAB_QF] = jnp.exp((idx + 1.0) * lgf_lane)
    tab_ref[RET_TAB_QB] = jnp.exp((c - idx) * lgb_lane)
    tab_ref[RET_TAB_KF] = jnp.exp((c - 1.0 - idx) * lgf_lane)
    tab_ref[RET_TAB_KB] = jnp.exp(idx * lgb_lane)
    sdf = jnp.exp(c * lgf_lane)
    sdb = jnp.exp(c * lgb_lane)
    dmat = (lax.broadcasted_iota(jnp.int32, (c, c), 0)
            - lax.broadcasted_iota(jnp.int32, (c, c), 1)).astype(F32)
    for hd in range(n_heads):
        dec_ref[hd // heads_per_pair, :, (hd % heads_per_pair) * c:(hd % heads_per_pair + 1) * c] = (
            jnp.where(dmat >= 0, jnp.exp(dmat * lgf[hd]), jnp.exp(-dmat * lgb[hd])))
    lane = lax.broadcasted_iota(jnp.int32, (1, LANES), 1)
    lane_hi = lane >= HEAD_DIM
    head_masks = [((lane >= hh * HEAD_DIM) & (lane < (hh + 1) * HEAD_DIM)).astype(BF16)
                  for hh in range(heads_per_pair)]
    row_hi = lax.broadcasted_iota(jnp.int32, (LANES, LANES), 0) >= HEAD_DIM
    col_hi = lax.broadcasted_iota(jnp.int32, (LANES, LANES), 1) >= HEAD_DIM
    blockdiag = row_hi == col_hi

    def kv_state(kd, vv):
        kv = lax.dot_general(kd.astype(BF16), vv, (((0,), (0,)), ((), ())), preferred_element_type=F32)
        return jnp.where(blockdiag, kv, 0.0)

    st_ref[...] = jnp.zeros_like(st_ref)
    sb_ref[n - 1] = jnp.zeros(sb_ref.shape[1:], sb_ref.dtype)

    def back(i, carry):
        nn = n - 1 - i
        rows = pl.ds(pl.multiple_of(nn * c, c), c)
        for p in range(n_pairs):
            cols = slice(p * LANES, (p + 1) * LANES)
            kd = k_ref[0, rows, cols].astype(F32) * tab_ref[RET_TAB_KB, :, cols]
            new = st_ref[p] * sdb[:, cols] + kv_state(kd, v_ref[0, rows, cols])
            st_ref[p] = new
            sb_ref[nn - 1, p] = new.astype(BF16)
        return carry

    lax.fori_loop(0, n - 1, back, 0, unroll=unroll)

    st_ref[...] = jnp.zeros_like(st_ref)

    def fwd(nn, carry):
        rows = pl.ds(pl.multiple_of(nn * c, c), c)
        for p in range(n_pairs):
            cols = slice(p * LANES, (p + 1) * LANES)
            qq = q_ref[0, rows, cols]
            kk = k_ref[0, rows, cols]
            vv = v_ref[0, rows, cols]
            qf = qq.astype(F32)
            sf = st_ref[p]
            k2 = jnp.concatenate([kk * hm for hm in head_masks], axis=0)
            v2 = jnp.concatenate([vv * hm for hm in head_masks], axis=0)
            sc = lax.dot_general(qq, k2, (((1,), (1,)), ((), ())), preferred_element_type=F32)
            lhs = jnp.concatenate([(qf * tab_ref[RET_TAB_QF, :, cols]).astype(BF16),
                                   (qf * tab_ref[RET_TAB_QB, :, cols]).astype(BF16),
                                   (sc * dec_ref[p]).astype(BF16)], axis=1)
            rhs = jnp.concatenate([sf.astype(BF16), sb_ref[nn, p], v2], axis=0)
            o = jnp.dot(lhs, rhs, preferred_element_type=F32)
            s_lo = jnp.sum(jnp.where(lane_hi, 0.0, o), axis=-1, keepdims=True)
            s_hi = jnp.sum(jnp.where(lane_hi, o, 0.0), axis=-1, keepdims=True)
            mu = jnp.where(lane_hi, s_hi, s_lo) * (1.0 / HEAD_DIM)
            dev = o - mu
            d2 = dev * dev
            v_lo = jnp.sum(jnp.where(lane_hi, 0.0, d2), axis=-1, keepdims=True)
            v_hi = jnp.sum(jnp.where(lane_hi, d2, 0.0), axis=-1, keepdims=True)
            var = jnp.where(lane_hi, v_hi, v_lo) * (1.0 / HEAD_DIM)
            out = dev * lax.rsqrt(var + NORM_EPS) * gain_ref[:, cols] * g_ref[0, rows, cols].astype(F32)
            o_ref[0, rows, cols] = out.astype(BF16)
            kd = kk.astype(F32) * tab_ref[RET_TAB_KF, :, cols]
            st_ref[p] = sf * sdf[:, cols] + kv_state(kd, vv)
        return carry

    lax.fori_loop(0, n, fwd, 0, unroll=unroll)


def _retention(qr, kr, vr, gate, lg_f, lg_b, out_gain, width=512, unroll=8):
    b, s, w = qr.shape
    n_pairs = width // LANES
    n_heads = width // HEAD_DIM
    spec = pl.BlockSpec((1, s, width), lambda bi, p, *_: (bi, 0, p))
    grid_spec = pltpu.PrefetchScalarGridSpec(
        num_scalar_prefetch=2,
        grid=(b, w // width),
        in_specs=[spec, spec, spec, spec, pl.BlockSpec((1, width), lambda bi, p, *_: (0, p))],
        out_specs=spec,
        scratch_shapes=[pltpu.VMEM((4, RET_CHUNK, width), F32),
                        pltpu.VMEM((n_pairs, RET_CHUNK, (LANES // HEAD_DIM) * RET_CHUNK), F32),
                        pltpu.VMEM((s // RET_CHUNK, n_pairs, LANES, LANES), BF16),
                        pltpu.VMEM((n_pairs, LANES, LANES), F32)],
    )
    return pl.pallas_call(
        functools.partial(_retention_kernel, chunk=RET_CHUNK, unroll=unroll),
        grid_spec=grid_spec,
        out_shape=jax.ShapeDtypeStruct((b, s, w), BF16),
        compiler_params=_cparams(("arbitrary", "arbitrary")),
        name="retention",
    )(lg_f, lg_b, qr, kr, vr, gate, out_gain.reshape(1, w))


def _split_bf16(t):
    hi = t.astype(BF16)
    lo = (t - hi.astype(F32)).astype(BF16)
    return hi, lo


def _pack_bf16_pairs(t):
    n = t.shape[1] // 2
    hi = pltpu.bitcast(t[:, :n].astype(BF16).astype(F32), jnp.uint32)
    lo = pltpu.bitcast(t[:, n:].astype(BF16).astype(F32), jnp.uint32)
    return hi | (lo >> 16)


def _unpack_bf16_pairs(u):
    hi = pltpu.bitcast(u & jnp.uint32(0xFFFF0000), F32)
    lo = pltpu.bitcast(u << 16, F32)
    return jnp.concatenate([hi, lo], axis=1)


def _outproj_kernel(x_ref, o1_ref, o2_ref, o3_ref, l1_ref, l2_ref, l3_ref, orr_ref, ga_ref, expand_ref,
                    wout_ref, gf_ref, wr_ref, br_ref, h_ref, hn_ref, logit_ref,
                    *nat_refs):
    tm = x_ref.shape[0]
    gw = HEAD_GROUP_WIDTH
    os, ls = [], []
    for (_, dil), o_ref, l_ref in zip(DILATED_BRANCHES, (o1_ref, o2_ref, o3_ref), (l1_ref, l2_ref, l3_ref)):
        if dil == 1:
            os.append(o_ref[...].astype(F32))
            ls.append(l_ref[...])
            continue
        c = CLASS_DILATIONS.index(dil)
        onat_ref, lnat_ref = nat_refs[2 * c], nat_refs[2 * c + 1]
        for r in range(dil):
            rows = pl.ds(r, tm // dil, stride=dil)
            for g in range(gw // LANES):
                col = r * gw + g * LANES
                onat_ref[g, rows, :] = o_ref[:, col:col + LANES].astype(F32)
            lnat_ref[rows, :] = l_ref[:, r * LANES:(r + 1) * LANES]
        os.append(jnp.concatenate([onat_ref[g] for g in range(gw // LANES)], axis=1))
        ls.append(lnat_ref[...])
    mx = jnp.maximum(jnp.maximum(ls[0], ls[1]), ls[2])
    es = [jnp.exp2(l - mx) for l in ls]
    inv = 1.0 / (es[0] + es[1] + es[2])
    expand = expand_ref[...]
    oa = jnp.zeros((tm, gw), F32)
    for e, o in zip(es, os):
        wexp = jnp.dot(jnp.concatenate(_split_bf16(e * inv), axis=1), expand, preferred_element_type=F32)
        oa = oa + wexp * o
    oa = oa * lax.rsqrt(jnp.mean(oa * oa, axis=-1, keepdims=True) + NORM_EPS) * ga_ref[...]
    mixed = jnp.concatenate([oa.astype(BF16), orr_ref[...]], axis=1)
    h = x_ref[...] + jnp.dot(mixed, wout_ref[...], preferred_element_type=F32)
    h_ref[...] = h
    hn = h * lax.rsqrt(jnp.mean(h * h, axis=-1, keepdims=True) + NORM_EPS) * gf_ref[...]
    hn_ref[...] = _pack_bf16_pairs(hn)
    prod = jnp.dot(jnp.concatenate(_split_bf16(hn), axis=0), wr_ref[...], preferred_element_type=F32)
    logit_ref[...] = prod[:tm, :LANES] + prod[:tm, LANES:] + prod[tm:, :LANES] + br_ref[...]


def _outproj(x2, o_list, lse_list, orr, attn_gain, w_out_bf16, ffn_gain, w_router, b_router, tm):
    t, d = x2.shape
    w = HEAD_GROUP_WIDTH
    expand = np.zeros((LANES, w), np.float32)
    for hd in range(ATTN_HEADS):
        expand[hd, hd * HEAD_DIM:(hd + 1) * HEAD_DIM] = 1.0
    expand = jnp.asarray(np.concatenate([expand, expand], axis=0), BF16)
    wr_hi = w_router.astype(BF16)
    wr = jnp.concatenate([wr_hi, (w_router - wr_hi.astype(F32)).astype(BF16)], axis=1)
    row = lambda width, dil=1: pl.BlockSpec((tm // dil, dil * width), lambda i: (i, 0))
    full = lambda a: pl.BlockSpec(a.shape, lambda i: (0,) * a.ndim)
    ga = attn_gain.reshape(1, w)
    gf = ffn_gain.reshape(1, d)
    dils = [dil for _, dil in DILATED_BRANCHES]
    o_flat = [o.reshape(t // dil, dil * w) for o, dil in zip(o_list, dils)]
    l_flat = [l.reshape(t // dil, dil * LANES) for l, dil in zip(lse_list, dils)]
    nat_scratch = []
    for _ in CLASS_DILATIONS:
        nat_scratch += [pltpu.VMEM((w // LANES, tm, LANES), F32), pltpu.VMEM((tm, LANES), F32)]
    return pl.pallas_call(
        _outproj_kernel,
        grid=(t // tm,),
        in_specs=[row(d)] + [row(w, dil) for dil in dils] + [row(LANES, dil) for dil in dils] + [row(w)]
                 + [full(ga), full(expand), full(w_out_bf16), full(gf), full(wr), full(b_router)],
        out_specs=[row(d), row(d // 2), row(LANES)],
        out_shape=[jax.ShapeDtypeStruct((t, d), F32),
                   jax.ShapeDtypeStruct((t, d // 2), jnp.uint32),
                   jax.ShapeDtypeStruct((t, LANES), F32)],
        scratch_shapes=nat_scratch,
        compiler_params=_cparams(("arbitrary",)),
        name="outproj",
    )(x2, *o_flat, *l_flat, orr, ga, expand, w_out_bf16, gf, wr, b_router)


ROUTE_E1, ROUTE_E2, ROUTE_G1, ROUTE_G2, ROUTE_R1, ROUTE_R2 = range(6)
GROUP_LANE0 = N_EXPERTS
SUBLANES = 8


def _route_kernel(logit_ref, tri_ref, route_ref, route_t_ref, count_ref, run_ref):
    @pl.when(pl.program_id(0) == 0)
    def _():
        run_ref[...] = jnp.zeros_like(run_ref)

    lg_t = jnp.transpose(logit_ref[...])
    tm = lg_t.shape[1]
    assert EXPERTS_PER_GROUP == SUBLANES and MOE_GROUPS <= SUBLANES
    rid = lax.broadcasted_iota(jnp.int32, (SUBLANES, tm), 0)
    big = jnp.int32(1 << 20)

    def top(vals):
        m = jnp.max(vals, axis=0, keepdims=True)
        i = jnp.min(jnp.where(vals == m, rid, big), axis=0, keepdims=True)
        return m, i

    gl = jnp.where(rid < MOE_GROUPS, lg_t[GROUP_LANE0:GROUP_LANE0 + SUBLANES], -jnp.inf)
    gmax, gidx = top(gl)
    group_gate = 1.0 / jnp.sum(jnp.exp(gl - gmax), axis=0, keepdims=True)
    el = lg_t[0:EXPERTS_PER_GROUP]
    for g in range(1, MOE_GROUPS):
        el = jnp.where(gidx == g, lg_t[g * EXPERTS_PER_GROUP:(g + 1) * EXPERTS_PER_GROUP], el)
    t1, i1 = top(el)
    t2, i2 = top(jnp.where(rid == i1, -jnp.inf, el))
    e21 = jnp.exp(t2 - t1)
    g1 = group_gate / (1.0 + e21)
    g2 = group_gate * e21 / (1.0 + e21)
    e1 = gidx * EXPERTS_PER_GROUP + i1
    e2 = gidx * EXPERTS_PER_GROUP + i2
    erow = lax.broadcasted_iota(jnp.int32, (N_EXPERTS, tm), 0)
    oh1 = erow == e1
    oh2 = erow == e2
    cnt = oh1.astype(F32) + oh2.astype(F32)
    run = run_ref[:, 0:1]
    prefix = jnp.dot(cnt.astype(BF16), tri_ref[...], preferred_element_type=F32) + run
    r1 = jnp.sum(jnp.where(oh1, prefix, 0.0), axis=0, keepdims=True)
    r2 = jnp.sum(jnp.where(oh2, prefix, 0.0), axis=0, keepdims=True)
    new_run = jnp.broadcast_to(run + jnp.sum(cnt, axis=1, keepdims=True), run_ref.shape)
    run_ref[...] = new_run
    count_ref[...] = new_run
    out = jnp.zeros((SUBLANES, tm), F32)
    for row, val in ((ROUTE_E1, e1.astype(F32)), (ROUTE_E2, e2.astype(F32)), (ROUTE_G1, g1),
                     (ROUTE_G2, g2), (ROUTE_R1, r1), (ROUTE_R2, r2)):
        out = jnp.where(rid == row, val, out)
    route_t_ref[0] = out
    out_t = jnp.concatenate([out, jnp.zeros((LANES - SUBLANES, tm), F32)], axis=0)
    route_ref[...] = jnp.transpose(out_t)


def _route(logits, tm):
    t = logits.shape[0]
    tri = jnp.asarray(np.triu(np.ones((tm, tm), np.float32), 1), BF16)
    return pl.pallas_call(
        _route_kernel,
        grid=(t // tm,),
        in_specs=[pl.BlockSpec((tm, LANES), lambda i: (i, 0)), pl.BlockSpec((tm, tm), lambda i: (0, 0))],
        out_specs=[pl.BlockSpec((tm, LANES), lambda i: (i, 0)),
                   pl.BlockSpec((1, SUBLANES, tm), lambda i: (i, 0, 0)),
                   pl.BlockSpec((N_EXPERTS, LANES), lambda i: (0, 0))],
        out_shape=[jax.ShapeDtypeStruct((t, LANES), F32),
                   jax.ShapeDtypeStruct((t // tm, SUBLANES, tm), F32),
                   jax.ShapeDtypeStruct((N_EXPERTS, LANES), F32)],
        scratch_shapes=[pltpu.VMEM((N_EXPERTS, LANES), F32)],
        compiler_params=_cparams(("arbitrary",)),
        name="route",
    )(logits, tri)


TOP_K = 2


SEG_ALIGN = 8
SORTED_TAIL = N_EXPERTS * SEG_ALIGN + MOE_BLOCK


def _tail_pieces():
    full, rest = divmod(SORTED_TAIL, MOE_BLOCK)
    return [MOE_BLOCK] * full + ([rest] if rest else [])


SC_CHUNK = 64
SC_BUFFERS = 3
ZERO_ROWS = N_EXPERTS * SEG_ALIGN + SORTED_TAIL
SORTED_ROWS_EXTRA = SORTED_TAIL + SEG_ALIGN


def _sc_workers():
    info = plsc.get_sparse_core_info()
    return info.num_cores, info.num_cores * info.num_subcores


def _dispatch(hn_packed, pos, zero_pos):
    t, dp = hn_packed.shape
    ncores, nw = _sc_workers()
    per_w = t // nw
    n_ch = per_w // SC_CHUNK
    z_rows = ZERO_ROWS // nw
    assert per_w % SC_CHUNK == 0 and ZERO_ROWS % nw == 0 and z_rows % SEG_ALIGN == 0 and TOP_K == 2
    idx = [pos[k].reshape(nw, n_ch, SC_CHUNK) for k in range(TOP_K)]
    zeros = jnp.zeros((z_rows, dp), hn_packed.dtype)
    mesh = plsc.VectorSubcoreMesh(core_axis_name="c", subcore_axis_name="s")

    @functools.partial(
        pl.kernel, mesh=mesh,
        out_type=jax.ShapeDtypeStruct((TOP_K * t + SORTED_ROWS_EXTRA, dp), hn_packed.dtype),
        scratch_types=[pltpu.VMEM((n_ch, SC_CHUNK), jnp.int32), pltpu.VMEM((n_ch, SC_CHUNK), jnp.int32),
                       pltpu.VMEM((z_rows,), jnp.int32),
                       pltpu.VMEM((SC_BUFFERS, SC_CHUNK, dp), hn_packed.dtype),
                       pltpu.VMEM((z_rows, dp), hn_packed.dtype),
                       pltpu.SemaphoreType.DMA((SC_BUFFERS,)), pltpu.SemaphoreType.DMA((SC_BUFFERS,)),
                       pltpu.SemaphoreType.DMA],
    )
    def scatter(hn_hbm, p0_hbm, p1_hbm, zpos_hbm, zeros_hbm, xs_hbm, i0_v, i1_v, iz_v, rows_v, zero_v,
                lsem, ssem, zsem):
        wid = lax.axis_index("s") * ncores + lax.axis_index("c")
        base = wid * per_w
        zero_load = pltpu.async_copy(zeros_hbm, zero_v, zsem)
        pltpu.sync_copy(zpos_hbm.at[wid], iz_v)
        pltpu.sync_copy(p0_hbm.at[wid], i0_v)
        pltpu.sync_copy(p1_hbm.at[wid], i1_v)

        def load(c):
            return pltpu.async_copy(hn_hbm.at[pl.ds(base + c * SC_CHUNK, SC_CHUNK)], rows_v.at[c % SC_BUFFERS],
                                    lsem.at[c % SC_BUFFERS])

        loads = {c: load(c) for c in range(min(SC_BUFFERS - 1, n_ch))}
        scat = {}
        for c in range(n_ch):
            slot = c % SC_BUFFERS
            loads[c].wait()
            scat[c] = (pltpu.async_copy(rows_v.at[slot], xs_hbm.at[i0_v.at[c]], ssem.at[slot]),
                       pltpu.async_copy(rows_v.at[slot], xs_hbm.at[i1_v.at[c]], ssem.at[slot]))
            if c >= 1:
                for d in scat[c - 1]:
                    d.wait()
            if c + SC_BUFFERS - 1 < n_ch:
                loads[c + SC_BUFFERS - 1] = load(c + SC_BUFFERS - 1)
        zero_load.wait()
        zero_scatter = pltpu.async_copy(zero_v, xs_hbm.at[iz_v], zsem)
        for d in scat[n_ch - 1]:
            d.wait()
        zero_scatter.wait()

    return scatter(hn_packed, idx[0], idx[1], zero_pos.reshape(nw, z_rows), zeros)


def _expert_kernel(bexp_ref, nreal_ref, row0_ref, bnext_ref, xs_hbm, wg_hbm, wu_hbm, wd_hbm, ys_hbm,
                   xbuf, ybuf, wg_f32, wu_f32, wd_f32, wg_bf, wu_bf, wd_bf, isem, osem, wsem, *, n_rows):
    i = pl.program_id(0)
    nb = pl.num_programs(0)
    slot = i % 2
    nslot = 1 - slot
    n_cur = nreal_ref[i]
    prev = jnp.maximum(i - 1, 0)
    nxt = jnp.minimum(i + 1, nb - 1)

    def in_copy(blk, s):
        row0 = pl.multiple_of(row0_ref[blk], SEG_ALIGN)
        return pltpu.make_async_copy(xs_hbm.at[pl.ds(row0, MOE_BLOCK)], xbuf.at[s], isem.at[s])

    def out_copy(blk, s):
        row0 = pl.multiple_of(row0_ref[blk], SEG_ALIGN)
        return pltpu.make_async_copy(ybuf.at[s], ys_hbm.at[pl.ds(row0, MOE_BLOCK)], osem.at[s])

    @pl.when(i == 0)
    def _():
        ybuf[...] = jnp.zeros_like(ybuf)
        tails = [pltpu.make_async_copy(ybuf.at[s, pl.ds(0, size)], ys_hbm.at[pl.ds(n_rows + s * MOE_BLOCK, size)],
                                       osem.at[s])
                 for s, size in enumerate(_tail_pieces())]
        for tail in tails:
            tail.start()
        for tail in tails:
            tail.wait()

        @pl.when(n_cur > 0)
        def _():
            in_copy(i, slot).start()

    @pl.when((i + 1 < nb) & (nreal_ref[nxt] > 0))
    def _():
        in_copy(nxt, nslot).start()

    def weight_copies(e):
        return [pltpu.make_async_copy(w_hbm.at[e], w_f32, wsem.at[j])
                for j, (w_hbm, w_f32) in enumerate(((wg_hbm, wg_f32), (wu_hbm, wu_f32), (wd_hbm, wd_f32)))]

    @pl.when((i == 0) & (n_cur > 0))
    def _():
        for c in weight_copies(bexp_ref[0]):
            c.start()

    @pl.when((n_cur > 0) & ((i == 0) | (bexp_ref[i] != bexp_ref[prev])))
    def _():
        for c in weight_copies(bexp_ref[i]):
            c.wait()
        wg_bf[...] = wg_f32[...].astype(BF16)
        wu_bf[...] = wu_f32[...].astype(BF16)
        wd_bf[...] = wd_f32[...].astype(BF16)

        @pl.when(bnext_ref[i] >= 0)
        def _():
            for c in weight_copies(bnext_ref[i]):
                c.start()

    def swiglu(rows):
        xb = _unpack_bf16_pairs(xbuf[slot, rows]).astype(BF16)
        gate = jnp.dot(xb, wg_bf[...], preferred_element_type=F32)
        up = jnp.dot(xb, wu_bf[...], preferred_element_type=F32)
        hid = (gate * jax.nn.sigmoid(gate) * up).astype(BF16)
        ybuf[slot, rows] = _pack_bf16_pairs(jnp.dot(hid, wd_bf[...], preferred_element_type=F32))

    @pl.when(n_cur > 0)
    def _():
        in_copy(i, slot).wait()

    @pl.when(n_cur > MOE_BLOCK // 2)
    def _():
        swiglu(slice(0, MOE_BLOCK))

    @pl.when((n_cur > 0) & (n_cur <= MOE_BLOCK // 2))
    def _():
        swiglu(slice(0, MOE_BLOCK // 2))

    @pl.when((i >= 1) & (nreal_ref[prev] > 0))
    def _():
        out_copy(prev, nslot).wait()

    @pl.when(n_cur > 0)
    def _():
        out_copy(i, slot).start()

        @pl.when(i == nb - 1)
        def _():
            out_copy(i, slot).wait()


def _experts(xs, block_expert, block_nreal, block_row0, block_next, wg, wu, wd):
    n_blocks = block_expert.shape[0]
    assert n_blocks >= 2 and len(_tail_pieces()) <= 2
    n_rows_pad = xs.shape[0] - SORTED_ROWS_EXTRA + SORTED_TAIL
    dp = xs.shape[1]
    _, d, ff = wg.shape
    any_spec = pl.BlockSpec(memory_space=pl.ANY)
    grid_spec = pltpu.PrefetchScalarGridSpec(
        num_scalar_prefetch=4,
        grid=(n_blocks,),
        in_specs=[any_spec, any_spec, any_spec, any_spec],
        out_specs=any_spec,
        scratch_shapes=[pltpu.VMEM((2, MOE_BLOCK, dp), jnp.uint32),
                        pltpu.VMEM((2, MOE_BLOCK, dp), jnp.uint32),
                        pltpu.VMEM((d, ff), F32),
                        pltpu.VMEM((d, ff), F32),
                        pltpu.VMEM((ff, d), F32),
                        pltpu.VMEM((d, ff), BF16),
                        pltpu.VMEM((d, ff), BF16),
                        pltpu.VMEM((ff, d), BF16),
                        pltpu.SemaphoreType.DMA((2,)),
                        pltpu.SemaphoreType.DMA((2,)),
                        pltpu.SemaphoreType.DMA((3,))],
    )
    return pl.pallas_call(
        functools.partial(_expert_kernel, n_rows=n_rows_pad - SORTED_TAIL),
        grid_spec=grid_spec,
        out_shape=jax.ShapeDtypeStruct((n_rows_pad, dp), jnp.uint32),
        compiler_params=_cparams(("arbitrary",)),
        name="experts",
    )(block_expert, block_nreal, block_row0, block_next, xs, wg, wu, wd)


def _combine_gather(ys, pos):
    t = pos.shape[1]
    dp = ys.shape[1]
    ncores, nw = _sc_workers()
    per_w = t // nw
    n_ch = per_w // SC_CHUNK
    assert per_w % SC_CHUNK == 0
    idx = [pos[k].reshape(nw, n_ch, SC_CHUNK) for k in range(TOP_K)]
    mesh = plsc.VectorSubcoreMesh(core_axis_name="c", subcore_axis_name="s")

    @functools.partial(
        pl.kernel, mesh=mesh,
        out_type=jax.ShapeDtypeStruct((TOP_K, t, dp), ys.dtype),
        scratch_types=[pltpu.VMEM((TOP_K, n_ch, SC_CHUNK), jnp.int32),
                       pltpu.VMEM((SC_BUFFERS, SC_CHUNK, dp), ys.dtype),
                       pltpu.SemaphoreType.DMA((SC_BUFFERS,)), pltpu.SemaphoreType.DMA((SC_BUFFERS,))],
    )
    def gather(ys_hbm, p0_hbm, p1_hbm, out_hbm, idx_v, rows_v, gsem, wsem):
        wid = lax.axis_index("s") * ncores + lax.axis_index("c")
        base = wid * per_w
        pltpu.sync_copy(p0_hbm.at[wid], idx_v.at[0])
        pltpu.sync_copy(p1_hbm.at[wid], idx_v.at[1])
        units = [(c, k) for c in range(n_ch) for k in range(TOP_K)]

        def fetch(u):
            c, k = units[u]
            return pltpu.async_copy(ys_hbm.at[idx_v.at[k, c]], rows_v.at[u % SC_BUFFERS], gsem.at[u % SC_BUFFERS])

        fetches = {u: fetch(u) for u in range(min(SC_BUFFERS - 1, len(units)))}
        writes = {}
        for u, (c, k) in enumerate(units):
            fetches[u].wait()
            writes[u] = pltpu.async_copy(rows_v.at[u % SC_BUFFERS],
                                         out_hbm.at[k, pl.ds(base + c * SC_CHUNK, SC_CHUNK)],
                                         wsem.at[u % SC_BUFFERS])
            if u >= 1:
                writes[u - 1].wait()
            if u + SC_BUFFERS - 1 < len(units):
                fetches[u + SC_BUFFERS - 1] = fetch(u + SC_BUFFERS - 1)
        writes[len(units) - 1].wait()

    return gather(ys, idx[0], idx[1])


def _final_kernel(h_ref, y_ref, route_ref, gain_ref, out_ref):
    r = route_ref[...]
    y = h_ref[...]
    for k, gate_lane in enumerate((ROUTE_G1, ROUTE_G2)):
        y = y + r[:, gate_lane:gate_lane + 1] * _unpack_bf16_pairs(y_ref[k])
    out_ref[...] = y * lax.rsqrt(jnp.mean(y * y, axis=-1, keepdims=True) + NORM_EPS) * gain_ref[...]


def _final(h, y_rows, route, gain, tm):
    t, d = h.shape
    return pl.pallas_call(
        _final_kernel,
        grid=(t // tm,),
        in_specs=[pl.BlockSpec((tm, d), lambda i: (i, 0)),
                  pl.BlockSpec((TOP_K, tm, d // 2), lambda i: (0, i, 0)),
                  pl.BlockSpec((tm, LANES), lambda i: (i, 0)),
                  pl.BlockSpec((1, d), lambda i: (0, 0))],
        out_specs=pl.BlockSpec((tm, d), lambda i: (i, 0)),
        out_shape=jax.ShapeDtypeStruct((t, d), F32),
        compiler_params=_cparams(("arbitrary",)),
        name="final",
    )(h, y_rows, route, gain.reshape(1, d))


def _layer(h3, mix_gain, w_in, attn_gain, decay_f, decay_b, ret_gain, w_out, ffn_gain,
           w_rg, b_rg, w_re, b_re, w_eg, w_eu, w_ed, final_gain):
    b, s, d = h3.shape
    t = b * s
    attn_qkv, (qr, kr, vr, gr) = _inproj(h3, mix_gain, w_in.astype(BF16), TOKEN_TILE)

    o_list, lse_list = [], []
    for window, dilation in DILATED_BRANCHES:
        reach = (window // 2) // dilation
        o, lse = _attn_branch(*attn_qkv[dilation], dilation, reach, tq=ATTN_SUB_TILE, tqb=TOKEN_TILE,
                              sub_tiles=TOKEN_TILE // ATTN_SUB_TILE)
        o_list.append(o)
        lse_list.append(lse)

    lg_f = jnp.log1p(-jnp.exp2(decay_f.astype(F32)))
    lg_b = jnp.log1p(-jnp.exp2(decay_b.astype(F32)))
    orr = _retention(qr, kr, vr, gr, lg_f, lg_b, ret_gain).reshape(t, HEAD_GROUP_WIDTH)

    n_route = MOE_GROUPS + N_EXPERTS
    w_router = jnp.zeros((d, LANES), F32).at[:, :n_route].set(jnp.concatenate([w_re, w_rg], axis=1).astype(F32))
    b_router = jnp.zeros((1, LANES), F32).at[0, :n_route].set(jnp.concatenate([b_re, b_rg]).astype(F32))
    h, hn_packed, logits = _outproj(h3.reshape(t, d), o_list, lse_list, orr, attn_gain, w_out.astype(BF16),
                                    ffn_gain, w_router, b_router, TOKEN_TILE)

    route, route_t, counts_rep = _route(logits, TOKEN_TILE)

    n_blocks = -(-TOP_K * t // MOE_BLOCK) + N_EXPERTS
    counts = counts_rep[:, 0].astype(jnp.int32)
    aligned = ((counts + SEG_ALIGN - 1) // SEG_ALIGN) * SEG_ALIGN
    seg_start = jnp.cumsum(aligned) - aligned
    fields = jnp.transpose(route_t, (1, 0, 2)).reshape(SUBLANES, t)
    e12 = fields[ROUTE_E1:ROUTE_E2 + 1].astype(jnp.int32)
    r12 = fields[ROUTE_R1:ROUTE_R2 + 1].astype(jnp.int32)
    pos = r12 + jnp.sum(jnp.where(e12[None] == jnp.arange(N_EXPERTS, dtype=jnp.int32)[:, None, None],
                                  seg_start[:, None, None], 0), axis=0)
    nblk = (counts + MOE_BLOCK - 1) // MOE_BLOCK
    blk_end = jnp.cumsum(nblk)
    blk_start = blk_end - nblk
    blk = jnp.arange(n_blocks, dtype=jnp.int32)[:, None]
    owner = (blk >= blk_start) & (blk < blk_end)
    local = (blk - blk_start) * MOE_BLOCK
    block_row0 = jnp.sum(jnp.where(owner, seg_start + local, 0), axis=-1).astype(jnp.int32)
    block_nreal = jnp.sum(jnp.where(owner, jnp.clip(counts - local, 0, MOE_BLOCK), 0), axis=-1).astype(jnp.int32)
    block_expert = jnp.minimum(jnp.sum((blk >= blk_end).astype(jnp.int32), axis=-1), N_EXPERTS - 1)
    e_iota = jnp.arange(N_EXPERTS, dtype=jnp.int32)
    later = (e_iota[None, :] > e_iota[:, None]) & (counts[None, :] > 0)
    next_expert = jnp.where(jnp.any(later, axis=1), jnp.min(jnp.where(later, e_iota[None, :], N_EXPERTS), axis=1), -1)
    block_next = jnp.sum(jnp.where(owner, next_expert, 0), axis=-1).astype(jnp.int32)

    n_sorted = TOP_K * t + SORTED_TAIL
    dump_row = n_sorted
    seg_end = seg_start + counts
    hole = seg_end[:, None] + jnp.arange(SEG_ALIGN, dtype=jnp.int32)[None, :]
    hole = jnp.where(hole < (seg_start + aligned)[:, None], hole, dump_row).reshape(-1)
    tail = (seg_start[-1] + aligned[-1]) + jnp.arange(SORTED_TAIL, dtype=jnp.int32)
    tail = jnp.where(tail < n_sorted, tail, dump_row)
    zero_pos = jnp.concatenate([hole, tail]).astype(jnp.int32)

    xs = _dispatch(hn_packed, pos, zero_pos)
    ys = _experts(xs, block_expert, block_nreal, block_row0, block_next, w_eg.astype(F32), w_eu.astype(F32),
                  w_ed.astype(F32))
    out = _final(h, _combine_gather(ys, pos), route, final_gain, TOKEN_TILE)
    return out.reshape(b, s, d)


def kernel(x, mix_norm_gain, w_in, attn_out_gain, ret_decay_fwd, ret_decay_bwd, ret_out_gain, w_out,
           ffn_norm_gain, w_route_group, b_route_group, w_route_expert, b_route_expert,
           w_expert_gate, w_expert_up, w_expert_down, final_norm_gain):
    depth = mix_norm_gain.shape[0]
    assert depth == 1, "the final rmsnorm is fused into the single layer's combine kernel"
    l = 0
    return _layer(x, mix_norm_gain[l], w_in[l], attn_out_gain[l], ret_decay_fwd[l], ret_decay_bwd[l],
                  ret_out_gain[l], w_out[l], ffn_norm_gain[l], w_route_group[l], b_route_group[l],
                  w_route_expert[l], b_route_expert[l], w_expert_gate[l], w_expert_up[l], w_expert_down[l],
                  final_norm_gain)
```
